```python
import jax, jax.numpy as jnp
from jax import lax
import numpy as np

D_MODEL = 1024
BATCH = 32
SEQ = 2048
DEPTH = 4

D_RG = D_MODEL
RG_BLOCK_W = 256
RG_BLOCKS = D_RG // RG_BLOCK_W
CONV_W = 4
RG_C = 8.0
HG_EXPAND = 128
HG_HEADS = D_MODEL // HG_EXPAND
HG_DK = HG_EXPAND
HG_DV = D_MODEL // HG_HEADS
HG_CHUNK = 32
F_MIN = 1e-30
D_FF = 4 * D_MODEL
NORM_EPS = 1e-6
SPLIT_SIZES = (D_RG, D_RG, HG_HEADS * HG_DK, HG_HEADS * HG_DK, HG_HEADS * HG_DV, HG_HEADS * HG_DV, D_MODEL, D_MODEL)
D_IN = sum(SPLIT_SIZES)
SPLIT_POINTS = tuple(np.cumsum(SPLIT_SIZES)[:-1].tolist())

kernel_name = 'hybrid_rglru_hgrn2_gated_trunk'


def rms_norm(x, gain):
    xf = x.astype(jnp.float32)
    y = xf * lax.rsqrt(jnp.mean(xf * xf, axis=-1, keepdims=True) + NORM_EPS)
    return (y * gain.astype(jnp.float32)).astype(x.dtype)


def causal_depthwise_conv(x, w, b):
    y = lax.conv_general_dilated(
        x, w[:, None, :].astype(x.dtype), window_strides=(1,),
        padding=((CONV_W - 1, 0),), dimension_numbers=('NWC', 'WIO', 'NWC'),
        feature_group_count=x.shape[-1])
    return y + b.astype(x.dtype)


def rg_lru(x, w_r, b_r, w_i, b_i, lam):
    B_, S_, _ = x.shape
    xf = x.astype(jnp.float32)
    xb = xf.reshape(B_, S_, RG_BLOCKS, RG_BLOCK_W)
    r = jax.nn.sigmoid(jnp.einsum('bsnj,njk->bsnk', xb, w_r.astype(jnp.float32)).reshape(B_, S_, D_RG) + b_r.astype(jnp.float32))
    i = jax.nn.sigmoid(jnp.einsum('bsnj,njk->bsnk', xb, w_i.astype(jnp.float32)).reshape(B_, S_, D_RG) + b_i.astype(jnp.float32))
    log_a = -RG_C * r * jax.nn.softplus(-lam.astype(jnp.float32))
    a = jnp.exp(log_a)
    u = jnp.sqrt(jnp.maximum(-jnp.expm1(2.0 * log_a), 0.0)) * (i * xf)

    def combine(left, right):
        a1, b1 = left
        a2, b2 = right
        return a1 * a2, a2 * b1 + b2

    _, h = lax.associative_scan(combine, (a, u), axis=1)
    return h.astype(x.dtype)


def hgrn2_chunkwise(q, k, log_f, v):
    B_, S_, H, DK = q.shape
    DV = v.shape[-1]
    n_chunks = S_ // HG_CHUNK

    def to_chunks(t):
        return t.reshape(B_, n_chunks, HG_CHUNK, H, t.shape[-1]).transpose(1, 0, 3, 2, 4)

    causal = jnp.tril(jnp.ones((HG_CHUNK, HG_CHUNK), dtype=bool))

    def step(state, chunk):
        qc, kc, lfc, vc = chunk
        cum = jnp.cumsum(lfc, axis=2)
        o_inter = jnp.einsum('bhtk,bhkv->bhtv', qc * jnp.exp(cum), state)
        rel = cum[:, :, :, None, :] - cum[:, :, None, :, :]
        decay = jnp.where(causal[:, :, None], jnp.exp(jnp.minimum(rel, 0.0)), 0.0)
        scores = jnp.einsum('bhtsk,bhsk->bhts', qc[:, :, :, None, :] * decay, kc)
        o_intra = jnp.einsum('bhts,bhsv->bhtv', scores, vc)
        last = cum[:, :, -1, :]
        k_to_end = kc * jnp.exp(jnp.minimum(last[:, :, None, :] - cum, 0.0))
        state = jnp.exp(last)[..., None] * state + jnp.einsum('bhsk,bhsv->bhkv', k_to_end, vc)
        return state, o_inter + o_intra

    state0 = jnp.zeros((B_, H, DK, DV), jnp.float32)
    _, o = lax.scan(step, state0, (to_chunks(q), to_chunks(k), to_chunks(log_f), to_chunks(v)))
    return o.transpose(1, 0, 3, 2, 4).reshape(B_, S_, H, DV)


def hybrid_mixer(h, lower_bound, w_in, conv_w, conv_b, w_r, b_r, w_i, b_i, lam, hg_norm, w_out):
    B_, S_, _ = h.shape
    proj = h @ w_in
    xa, ga, q, f, v, g, m_a, m_b = jnp.split(proj, SPLIT_POINTS, axis=-1)
    xa = causal_depthwise_conv(xa, conv_w, conv_b)
    y_a = rg_lru(xa, w_r, b_r, w_i, b_i, lam) * jax.nn.gelu(ga)
    qf = jax.nn.silu(q.astype(jnp.float32)).reshape(B_, S_, HG_HEADS, HG_DK)
    zf = f.astype(jnp.float32).reshape(B_, S_, HG_HEADS, HG_DK)
    lb = lower_bound.reshape(HG_HEADS, HG_DK)
    sig = jax.nn.sigmoid(zf)
    f_gate = lb + (1.0 - lb) * sig
    log_f = jnp.log(jnp.maximum(f_gate, F_MIN))
    kf = (1.0 - lb) * jax.nn.sigmoid(-zf)
    vf = v.astype(jnp.float32).reshape(B_, S_, HG_HEADS, HG_DV)
    o = rms_norm(hgrn2_chunkwise(qf, kf, log_f, vf), hg_norm)
    y_b = o.reshape(B_, S_, HG_HEADS * HG_DV).astype(h.dtype) * jax.nn.silu(g)
    y = jax.nn.sigmoid(m_a) * y_a + jax.nn.sigmoid(m_b) * y_b
    return y @ w_out


def _fwd_setup_inputs(seed: int = 0) -> dict:
    key = jax.random.key(seed)
    ks = jax.random.split(key, 20)
    f32 = jnp.float32
    nrm = lambda k, shape, scale: jax.random.normal(k, shape, f32) * scale
    a_c = jax.random.uniform(ks[10], (DEPTH, D_RG), f32, minval=0.9, maxval=0.999)
    a0 = a_c ** (1.0 / RG_C)
    lam = jnp.log(a0) - jnp.log1p(-a0)
    return {
        'x': jax.random.normal(ks[0], (BATCH, SEQ, D_MODEL), f32),
        'lb_logits': nrm(ks[1], (DEPTH, HG_HEADS * HG_DK), 0.1),
        'norm_mix': 1.0 + nrm(ks[2], (DEPTH, D_MODEL), 0.02),
        'w_in': nrm(ks[3], (DEPTH, D_MODEL, D_IN), D_MODEL ** -0.5),
        'conv_w': nrm(ks[4], (DEPTH, CONV_W, D_RG), CONV_W ** -0.5),
        'conv_b': nrm(ks[5], (DEPTH, D_RG), 0.02),
        'w_r': nrm(ks[6], (DEPTH, RG_BLOCKS, RG_BLOCK_W, RG_BLOCK_W), RG_BLOCK_W ** -0.5),
        'b_r': nrm(ks[7], (DEPTH, D_RG), 0.02),
        'w_i': nrm(ks[8], (DEPTH, RG_BLOCKS, RG_BLOCK_W, RG_BLOCK_W), RG_BLOCK_W ** -0.5),
        'b_i': nrm(ks[9], (DEPTH, D_RG), 0.02),
        'lam': lam,
        'hg_norm': 1.0 + nrm(ks[11], (DEPTH, HG_DV), 0.02),
        'w_out': nrm(ks[12], (DEPTH, D_MODEL, D_MODEL), D_MODEL ** -0.5),
        'norm_mlp': 1.0 + nrm(ks[13], (DEPTH, D_MODEL), 0.02),
        'w_up': nrm(ks[14], (DEPTH, D_MODEL, D_FF), D_MODEL ** -0.5),
        'w_down': nrm(ks[15], (DEPTH, D_FF, D_MODEL), D_FF ** -0.5),
        'norm_final': 1.0 + nrm(ks[16], (D_MODEL,), 0.02),
    }


def _fwd_reference(x, lb_logits, norm_mix, w_in, conv_w, conv_b, w_r, b_r, w_i, b_i, lam, hg_norm, w_out, norm_mlp, w_up, w_down, norm_final):
    sm = jax.nn.softmax(lb_logits.astype(jnp.float32), axis=0)
    lower_bounds = jnp.clip(jnp.cumsum(sm, axis=0) - sm[0], 0.0, 1.0)
    for l in range(DEPTH):
        h = rms_norm(x, norm_mix[l])
        x = x + hybrid_mixer(h, lower_bounds[l], w_in[l], conv_w[l], conv_b[l], w_r[l], b_r[l],
                             w_i[l], b_i[l], lam[l], hg_norm[l], w_out[l])
        h = rms_norm(x, norm_mlp[l])
        x = x + jnp.square(jax.nn.relu(h @ w_up[l])) @ w_down[l]
    return rms_norm(x, norm_final)


import jax as _jax
import jax.numpy as _jnp

TWIN_FORMAT = 'train_step'
FWD_PARAMS = ['x', 'lb_logits', 'norm_mix', 'w_in', 'conv_w', 'conv_b', 'w_r', 'b_r', 'w_i', 'b_i', 'lam', 'hg_norm', 'w_out', 'norm_mlp', 'w_up', 'w_down', 'norm_final']
TWIN_WEIGHTS = ['lb_logits', 'norm_mix', 'w_in', 'conv_w', 'conv_b', 'w_r', 'b_r', 'w_i', 'b_i', 'lam', 'hg_norm', 'w_out', 'norm_mlp', 'w_up', 'w_down', 'norm_final']
TWIN_DIFF_INPUT = 'x'
TWIN_INPUTS = ['x', 'lb_logits', 'norm_mix', 'w_in', 'conv_w', 'conv_b', 'w_r', 'b_r', 'w_i', 'b_i', 'lam', 'hg_norm', 'w_out', 'norm_mlp', 'w_up', 'w_down', 'norm_final', 'loss_target', 'm_lb_logits', 'm_norm_mix', 'm_w_in', 'm_conv_w', 'm_conv_b', 'm_w_r', 'm_b_r', 'm_w_i', 'm_b_i', 'm_lam', 'm_hg_norm', 'm_w_out', 'm_norm_mlp', 'm_w_up', 'm_w_down', 'm_norm_final', 'v_lb_logits', 'v_norm_mix', 'v_w_in', 'v_conv_w', 'v_conv_b', 'v_w_r', 'v_b_r', 'v_w_i', 'v_b_i', 'v_lam', 'v_hg_norm', 'v_w_out', 'v_norm_mlp', 'v_w_up', 'v_w_down', 'v_norm_final']
TWIN_OUTPUTS = ['loss', 'grad_x', 'grad_lb_logits', 'grad_norm_mix', 'grad_w_in', 'grad_conv_w', 'grad_conv_b', 'grad_w_r', 'grad_b_r', 'grad_w_i', 'grad_b_i', 'grad_lam', 'grad_hg_norm', 'grad_w_out', 'grad_norm_mlp', 'grad_w_up', 'grad_w_down', 'grad_norm_final', 'delta_lb_logits', 'delta_norm_mix', 'delta_w_in', 'delta_conv_w', 'delta_conv_b', 'delta_w_r', 'delta_b_r', 'delta_w_i', 'delta_b_i', 'delta_lam', 'delta_hg_norm', 'delta_w_out', 'delta_norm_mlp', 'delta_w_up', 'delta_w_down', 'delta_norm_final', 'new_m_lb_logits', 'new_m_norm_mix', 'new_m_w_in', 'new_m_conv_w', 'new_m_conv_b', 'new_m_w_r', 'new_m_b_r', 'new_m_w_i', 'new_m_b_i', 'new_m_lam', 'new_m_hg_norm', 'new_m_w_out', 'new_m_norm_mlp', 'new_m_w_up', 'new_m_w_down', 'new_m_norm_final', 'new_v_lb_logits', 'new_v_norm_mix', 'new_v_w_in', 'new_v_conv_w', 'new_v_conv_b', 'new_v_w_r', 'new_v_b_r', 'new_v_w_i', 'new_v_b_i', 'new_v_lam', 'new_v_hg_norm', 'new_v_w_out', 'new_v_norm_mlp', 'new_v_w_up', 'new_v_w_down', 'new_v_norm_final']
TWIN_LEAF_KINDS = {'loss': 'loss', 'grad_x': 'grad_x', 'grad_lb_logits': 'grad_w', 'grad_norm_mix': 'grad_w', 'grad_w_in': 'grad_w', 'grad_conv_w': 'grad_w', 'grad_conv_b': 'grad_w', 'grad_w_r': 'grad_w', 'grad_b_r': 'grad_w', 'grad_w_i': 'grad_w', 'grad_b_i': 'grad_w', 'grad_lam': 'grad_w', 'grad_hg_norm': 'grad_w', 'grad_w_out': 'grad_w', 'grad_norm_mlp': 'grad_w', 'grad_w_up': 'grad_w', 'grad_w_down': 'grad_w', 'grad_norm_final': 'grad_w', 'delta_lb_logits': 'delta_w', 'delta_norm_mix': 'delta_w', 'delta_w_in': 'delta_w', 'delta_conv_w': 'delta_w', 'delta_conv_b': 'delta_w', 'delta_w_r': 'delta_w', 'delta_b_r': 'delta_w', 'delta_w_i': 'delta_w', 'delta_b_i': 'delta_w', 'delta_lam': 'delta_w', 'delta_hg_norm': 'delta_w', 'delta_w_out': 'delta_w', 'delta_norm_mlp': 'delta_w', 'delta_w_up': 'delta_w', 'delta_w_down': 'delta_w', 'delta_norm_final': 'delta_w', 'new_m_lb_logits': 'new_m', 'new_m_norm_mix': 'new_m', 'new_m_w_in': 'new_m', 'new_m_conv_w': 'new_m', 'new_m_conv_b': 'new_m', 'new_m_w_r': 'new_m', 'new_m_b_r': 'new_m', 'new_m_w_i': 'new_m', 'new_m_b_i': 'new_m', 'new_m_lam': 'new_m', 'new_m_hg_norm': 'new_m', 'new_m_w_out': 'new_m', 'new_m_norm_mlp': 'new_m', 'new_m_w_up': 'new_m', 'new_m_w_down': 'new_m', 'new_m_norm_final': 'new_m', 'new_v_lb_logits': 'new_v', 'new_v_norm_mix': 'new_v', 'new_v_w_in': 'new_v', 'new_v_conv_w': 'new_v', 'new_v_conv_b': 'new_v', 'new_v_w_r': 'new_v', 'new_v_b_r': 'new_v', 'new_v_w_i': 'new_v', 'new_v_b_i': 'new_v', 'new_v_lam': 'new_v', 'new_v_hg_norm': 'new_v', 'new_v_w_out': 'new_v', 'new_v_norm_mlp': 'new_v', 'new_v_w_up': 'new_v', 'new_v_w_down': 'new_v', 'new_v_norm_final': 'new_v'}


def _forward(args):
    return _fwd_reference(*[args[k] for k in FWD_PARAMS])


def _output_shape():
    out = _jax.eval_shape(lambda: _forward(_fwd_setup_inputs(0)))
    return out.shape, out.dtype

N_MICROBATCH = 1
ADAM_LR = 0.001
ADAM_B1 = 0.9
ADAM_B2 = 0.999
ADAM_EPS = 1e-08
ADAM_WD = 0.01
ADAM_STEP = 10
PER_EXAMPLE_BATCH_AXIS = {'x': 0, 'loss_target': 0}
SHARED_INPUTS = []
_WEIGHT_DTYPES = {'lb_logits': _jnp.float32, 'norm_mix': _jnp.float32, 'w_in': _jnp.float32, 'conv_w': _jnp.float32, 'conv_b': _jnp.float32, 'w_r': _jnp.float32, 'b_r': _jnp.float32, 'w_i': _jnp.float32, 'b_i': _jnp.float32, 'lam': _jnp.float32, 'hg_norm': _jnp.float32, 'w_out': _jnp.float32, 'norm_mlp': _jnp.float32, 'w_up': _jnp.float32, 'w_down': _jnp.float32, 'norm_final': _jnp.float32}
MOMENT_SCALE = {'lb_logits': 2.508689e-03, 'norm_mix': 2.201877e-01, 'w_in': 7.480929e-02, 'conv_w': 1.740843e-01, 'conv_b': 7.679018e-01, 'w_r': 1.631222e-02, 'b_r': 2.721956e-02, 'w_i': 3.116049e-02, 'b_i': 5.311297e-02, 'lam': 7.671983e-02, 'hg_norm': 1.757751e-01, 'w_out': 1.525782e-01, 'norm_mlp': 1.877944e-01, 'w_up': 9.516721e-02, 'w_down': 2.126894e-01, 'norm_final': 6.620807e+01}


def _to_microbatches(a, axis):
    t = _jnp.moveaxis(a, axis, 0)
    t = t.reshape((N_MICROBATCH, t.shape[0] // N_MICROBATCH) + t.shape[1:])
    return _jnp.moveaxis(t, 1, axis + 1)


def setup_inputs(seed: int = 0) -> dict:
    inp = _fwd_setup_inputs(seed)
    key = _jax.random.fold_in(_jax.random.key(seed), 7919)
    shape, _ = _output_shape()
    out = dict(inp)
    out["loss_target"] = _jax.random.normal(_jax.random.fold_in(key, 0), shape, _jnp.float32)
    for i, name in enumerate(TWIN_WEIGHTS):
        w = inp[name].astype(_jnp.float32)
        if MOMENT_SCALE is None:
            s = _jnp.sqrt(_jnp.mean(_jnp.square(w)) + 1e-30)
        else:
            s = MOMENT_SCALE[name]
        km, kv = _jax.random.split(_jax.random.fold_in(key, i + 1))
        out[name] = w
        out["m_" + name] = s * _jax.random.normal(km, w.shape, _jnp.float32)
        out["v_" + name] = (s * s) * _jax.random.uniform(kv, w.shape, _jnp.float32, 0.5, 1.5)
    if N_MICROBATCH > 1:
        for name, axis in PER_EXAMPLE_BATCH_AXIS.items():
            out[name] = _to_microbatches(out[name], axis)
    return {'x': out['x'], 'lb_logits': out['lb_logits'], 'norm_mix': out['norm_mix'], 'w_in': out['w_in'], 'conv_w': out['conv_w'], 'conv_b': out['conv_b'], 'w_r': out['w_r'], 'b_r': out['b_r'], 'w_i': out['w_i'], 'b_i': out['b_i'], 'lam': out['lam'], 'hg_norm': out['hg_norm'], 'w_out': out['w_out'], 'norm_mlp': out['norm_mlp'], 'w_up': out['w_up'], 'w_down': out['w_down'], 'norm_final': out['norm_final'], 'loss_target': out['loss_target'], 'm_lb_logits': out['m_lb_logits'], 'm_norm_mix': out['m_norm_mix'], 'm_w_in': out['m_w_in'], 'm_conv_w': out['m_conv_w'], 'm_conv_b': out['m_conv_b'], 'm_w_r': out['m_w_r'], 'm_b_r': out['m_b_r'], 'm_w_i': out['m_w_i'], 'm_b_i': out['m_b_i'], 'm_lam': out['m_lam'], 'm_hg_norm': out['m_hg_norm'], 'm_w_out': out['m_w_out'], 'm_norm_mlp': out['m_norm_mlp'], 'm_w_up': out['m_w_up'], 'm_w_down': out['m_w_down'], 'm_norm_final': out['m_norm_final'], 'v_lb_logits': out['v_lb_logits'], 'v_norm_mix': out['v_norm_mix'], 'v_w_in': out['v_w_in'], 'v_conv_w': out['v_conv_w'], 'v_conv_b': out['v_conv_b'], 'v_w_r': out['v_w_r'], 'v_b_r': out['v_b_r'], 'v_w_i': out['v_w_i'], 'v_b_i': out['v_b_i'], 'v_lam': out['v_lam'], 'v_hg_norm': out['v_hg_norm'], 'v_w_out': out['v_w_out'], 'v_norm_mlp': out['v_norm_mlp'], 'v_w_up': out['v_w_up'], 'v_w_down': out['v_w_down'], 'v_norm_final': out['v_norm_final']}


def _loss(weights, diff, rest, loss_target):
    with _jax.named_scope("forward"):
        args = {**rest, TWIN_DIFF_INPUT: diff, **{k: w.astype(_WEIGHT_DTYPES[k]) for k, w in weights.items()}}
        y = _forward(args)
    with _jax.named_scope("loss_head"):
        err = _jnp.square(y.astype(_jnp.float32) - loss_target)
        return 0.5 * _jnp.sum(_jnp.mean(err, axis=-1)) if err.ndim else 0.5 * err


def _adamw(w, g, m, v):
    m = ADAM_B1 * m + (1.0 - ADAM_B1) * g
    v = ADAM_B2 * v + (1.0 - ADAM_B2) * _jnp.square(g)
    m_hat = m / (1.0 - ADAM_B1 ** ADAM_STEP)
    v_hat = v / (1.0 - ADAM_B2 ** ADAM_STEP)
    delta = -ADAM_LR * (m_hat / (_jnp.sqrt(v_hat) + ADAM_EPS) + ADAM_WD * w)
    return delta, m, v


def reference(x, lb_logits, norm_mix, w_in, conv_w, conv_b, w_r, b_r, w_i, b_i, lam, hg_norm, w_out, norm_mlp, w_up, w_down, norm_final, loss_target, m_lb_logits, m_norm_mix, m_w_in, m_conv_w, m_conv_b, m_w_r, m_b_r, m_w_i, m_b_i, m_lam, m_hg_norm, m_w_out, m_norm_mlp, m_w_up, m_w_down, m_norm_final, v_lb_logits, v_norm_mix, v_w_in, v_conv_w, v_conv_b, v_w_r, v_b_r, v_w_i, v_b_i, v_lam, v_hg_norm, v_w_out, v_norm_mlp, v_w_up, v_w_down, v_norm_final):
    given = dict(x=x, lb_logits=lb_logits, norm_mix=norm_mix, w_in=w_in, conv_w=conv_w, conv_b=conv_b, w_r=w_r, b_r=b_r, w_i=w_i, b_i=b_i, lam=lam, hg_norm=hg_norm, w_out=w_out, norm_mlp=norm_mlp, w_up=w_up, w_down=w_down, norm_final=norm_final, loss_target=loss_target, m_lb_logits=m_lb_logits, m_norm_mix=m_norm_mix, m_w_in=m_w_in, m_conv_w=m_conv_w, m_conv_b=m_conv_b, m_w_r=m_w_r, m_b_r=m_b_r, m_w_i=m_w_i, m_b_i=m_b_i, m_lam=m_lam, m_hg_norm=m_hg_norm, m_w_out=m_w_out, m_norm_mlp=m_norm_mlp, m_w_up=m_w_up, m_w_down=m_w_down, m_norm_final=m_norm_final, v_lb_logits=v_lb_logits, v_norm_mix=v_norm_mix, v_w_in=v_w_in, v_conv_w=v_conv_w, v_conv_b=v_conv_b, v_w_r=v_w_r, v_b_r=v_b_r, v_w_i=v_w_i, v_b_i=v_b_i, v_lam=v_lam, v_hg_norm=v_hg_norm, v_w_out=v_w_out, v_norm_mlp=v_norm_mlp, v_w_up=v_w_up, v_w_down=v_w_down, v_norm_final=v_norm_final)
    weights = {n: given[n] for n in TWIN_WEIGHTS}
    shared = {n: given[n] for n in SHARED_INPUTS}
    per_example = {n: given[n] for n in ['x']}
    grad_fn = _jax.value_and_grad(_loss, argnums=(0, 1))

    def one_microbatch(ex, loss_target):
        ex = dict(ex)
        diff = ex.pop(TWIN_DIFF_INPUT)
        return grad_fn(weights, diff, {**shared, **ex}, loss_target)

    if N_MICROBATCH == 1:
        loss, (grad_w, grad_x) = one_microbatch(per_example, given["loss_target"])
    else:
        def body(carry, xs):
            loss_sum, grad_sum = carry
            l_k, (gw_k, gx_k) = one_microbatch(xs[0], xs[1])
            with _jax.named_scope("update"):
                return (loss_sum + l_k, _jax.tree.map(_jnp.add, grad_sum, gw_k)), gx_k

        init = (_jnp.zeros((), _jnp.float32), _jax.tree.map(_jnp.zeros_like, weights))
        (loss, grad_w), grad_x = _jax.lax.scan(body, init, (per_example, given["loss_target"]))
    with _jax.named_scope("update"):
        delta_w, new_m, new_v = {}, {}, {}
        for n in TWIN_WEIGHTS:
            delta_w[n], new_m[n], new_v[n] = _adamw(weights[n], grad_w[n], given["m_" + n], given["v_" + n])
    return (loss, grad_x, *[grad_w[n] for n in TWIN_WEIGHTS], *[delta_w[n] for n in TWIN_WEIGHTS],
            *[new_m[n] for n in TWIN_WEIGHTS], *[new_v[n] for n in TWIN_WEIGHTS])
```

```python
import functools
import math

import jax
import jax.numpy as jnp
from jax import lax
from jax.experimental import pallas as pl
from jax.experimental.pallas import tpu as pltpu

F32 = jnp.float32
BF16 = jnp.bfloat16

HEAD = 128
RG_BLOCK = 256
CONV_TAPS = 4
RG_C = 8.0
F_MIN = 1e-30
NORM_EPS = 1e-6
N_SEG = 8
N_SHARD = 4
HG_CHUNK = 256
RG_TILE = 256
ADAM_LR, ADAM_B1, ADAM_B2, ADAM_EPS, ADAM_WD, ADAM_STEP = 0.001, 0.9, 0.999, 1e-08, 0.01, 10
V7X_VMEM_BYTES = 64 * 1024 * 1024
VMEM_LIMIT = V7X_VMEM_BYTES - 8 * 1024 * 1024
SUBLANES = 8
MESH_ID = pl.DeviceIdType.MESH
ANY = pl.BlockSpec(memory_space=pl.ANY)


def _cp(sem):
    return pltpu.CompilerParams(dimension_semantics=sem, vmem_limit_bytes=VMEM_LIMIT)


def _dot(a, b):
    return jnp.dot(a, b, preferred_element_type=F32)


def _dot_nt(a, b):
    return lax.dot_general(a, b, (((1,), (1,)), ((), ())), preferred_element_type=F32)


def _dot_tn(a, b):
    return lax.dot_general(a, b, (((0,), (0,)), ((), ())), preferred_element_type=F32)


def _dot_f32(a, b):
    return jnp.dot(a, b, preferred_element_type=F32, precision=lax.Precision.HIGHEST)


def _sig(x):
    return jax.nn.sigmoid(x)


def _rows8(x):
    return x.reshape(x.shape[0] // SUBLANES, SUBLANES, x.shape[1]).sum(axis=0)


def _tile(n, cap):
    if n <= cap:
        return n
    t = cap - cap % 16
    while n % t:
        t -= 16
    return t


_GELU_C = math.sqrt(2.0 / math.pi)


def _gelu_and_grad(x):
    x2 = x * x
    t = jnp.tanh(_GELU_C * (x + 0.044715 * x * x2))
    g = 0.5 * x * (1.0 + t)
    dg = 0.5 * (1.0 + t) + 0.5 * x * (1.0 - t * t) * (_GELU_C * (1.0 + 3.0 * 0.044715 * x2))
    return g, dg


def _rms(x):
    return lax.rsqrt(jnp.mean(x * x, axis=-1, keepdims=True) + NORM_EPS)


def _rms_bwd(dh, x, rs, gain):
    xhat = x * rs
    dxhat = dh * gain
    dx = rs * (dxhat - xhat * jnp.mean(dxhat * xhat, axis=-1, keepdims=True))
    return dx, _rows8(dh * xhat)


def _inproj_fwd(x2, gain, w_st, layer):
    T, D = x2.shape
    tm = _tile(T, 1024)

    def body(x_ref, g_ref, w_ref, o_ref, h_ref):
        @pl.when(pl.program_id(1) == 0)
        def _():
            x = x_ref[...]
            h_ref[...] = (x * _rms(x) * g_ref[...]).astype(BF16)
        o_ref[...] = _dot(h_ref[...], w_ref[...])

    return pl.pallas_call(
        body, name="inproj_fwd",
        grid=(T // tm, N_SEG),
        in_specs=[pl.BlockSpec((tm, D), lambda i, k: (i, 0)),
                  pl.BlockSpec((1, D), lambda i, k: (0, 0)),
                  pl.BlockSpec((None, None, D, D), lambda i, k: (k // 2, layer, 0, k % 2))],
        out_specs=[pl.BlockSpec((tm, D), lambda i, k: (i, k)),
                   pl.BlockSpec((tm, D), lambda i, k: (i, 0))],
        out_shape=[jax.ShapeDtypeStruct((T, N_SEG * D), F32), jax.ShapeDtypeStruct((T, D), BF16)],
        compiler_params=_cp(("parallel", "arbitrary")),
    )(x2, gain, w_st)


def _rg_gates(xc, wr_ref, br, wi_ref, bi, sp):
    D = xc.shape[1]
    xcb = xc.astype(BF16)
    pr, pi = [], []
    for n in range(D // RG_BLOCK):
        blk = xcb[:, n * RG_BLOCK:(n + 1) * RG_BLOCK]
        pr.append(_dot(blk, wr_ref[n]))
        pi.append(_dot(blk, wi_ref[n]))
    r = _sig(jnp.concatenate(pr, axis=1) + br) if len(pr) > 1 else _sig(pr[0] + br)
    i = _sig(jnp.concatenate(pi, axis=1) + bi) if len(pi) > 1 else _sig(pi[0] + bi)
    la = (-RG_C) * r * sp
    a = jnp.exp(la)
    y = 2.0 * la
    one_m_e2 = jnp.where(y > -1e-2, -(y * (1.0 + 0.5 * y * (1.0 + y * (1.0 / 3.0)))), 1.0 - jnp.exp(y))
    mult = jnp.sqrt(jnp.maximum(one_m_e2, 0.0))
    return r, i, a, mult


def _conv_taps(xbuf, cw_ref, ts):
    acc = None
    for j in range(CONV_TAPS):
        term = cw_ref[j:j + 1, :] * xbuf[pl.ds(SUBLANES - (CONV_TAPS - 1) + j, ts), :]
        acc = term if acc is None else acc + term
    return acc


def _rg_fwd(proj, B, cw, cb, wr, br, wi, bi, sp):
    T = proj.shape[0]
    D = proj.shape[1] // N_SEG
    S = T // B
    ts = _tile(S, RG_TILE)
    nts = S // ts
    nb = D // RG_BLOCK

    def body(xa_ref, ga_ref, cw_ref, cb_ref, wr_ref, br_ref, wi_ref, bi_ref, sp_ref,
             ya_ref, h_ref, xbuf, a_scr, u_scr, carry):
        @pl.when(pl.program_id(1) == 0)
        def _():
            xbuf[0:SUBLANES, :] = jnp.zeros((SUBLANES, D), F32)
            carry[...] = jnp.zeros((SUBLANES, D), F32)

        xbuf[pl.ds(SUBLANES, ts), :] = xa_ref[...]
        xc = _conv_taps(xbuf, cw_ref, ts) + cb_ref[...]
        r, i, a, mult = _rg_gates(xc, wr_ref, br_ref[...], wi_ref, bi_ref[...], sp_ref[...])
        a_scr[...] = a
        u_scr[...] = mult * (i * xc)
        row8 = lax.broadcasted_iota(jnp.int32, (SUBLANES, 1), 0)

        def blk(n, hprev):
            off = pl.multiple_of(n * SUBLANES, SUBLANES)
            a8 = a_scr[pl.ds(off, SUBLANES), :]
            u8 = u_scr[pl.ds(off, SUBLANES), :]
            for d in (1, 2, 4):
                m = row8 >= d
                ap = jnp.where(m, pltpu.roll(a8, d, 0), 1.0)
                up = jnp.where(m, pltpu.roll(u8, d, 0), 0.0)
                u8 = a8 * up + u8
                a8 = a8 * ap
            h8 = u8 + a8 * hprev
            u_scr[pl.ds(off, SUBLANES), :] = h8
            last = jnp.sum(jnp.where(row8 == SUBLANES - 1, h8, 0.0), axis=0, keepdims=True)
            return jnp.broadcast_to(last, (SUBLANES, D))

        carry[...] = lax.fori_loop(0, ts // SUBLANES, blk, carry[...])
        h = u_scr[...]
        h_ref[...] = h
        g, _ = _gelu_and_grad(ga_ref[...])
        ya_ref[...] = h * g
        xbuf[0:SUBLANES, :] = xa_ref[pl.ds(ts - SUBLANES, SUBLANES), :]

    vec = pl.BlockSpec((1, D), lambda b, j: (0, 0))
    gate = pl.BlockSpec((nb, RG_BLOCK, RG_BLOCK), lambda b, j: (0, 0, 0))
    return pl.pallas_call(
        body, name="rg_fwd",
        grid=(B, nts),
        in_specs=[pl.BlockSpec((ts, D), lambda b, j: (b * nts + j, 0)),
                  pl.BlockSpec((ts, D), lambda b, j: (b * nts + j, 1)),
                  pl.BlockSpec((CONV_TAPS, D), lambda b, j: (0, 0)), vec, gate, vec, gate, vec, vec],
        out_specs=[pl.BlockSpec((ts, D), lambda b, j: (b * nts + j, 0))] * 2,
        out_shape=[jax.ShapeDtypeStruct((T, D), F32)] * 2,
        scratch_shapes=[pltpu.VMEM((SUBLANES + ts, D), F32), pltpu.VMEM((ts, D), F32),
                        pltpu.VMEM((ts, D), F32), pltpu.VMEM((SUBLANES, D), F32)],
        compiler_params=_cp(("arbitrary", "arbitrary")),
    )(proj, proj, cw, cb, wr, br, wi, bi, sp)


def _hg_gates(q, z, lb):
    sig = _sig(z)
    one_m = 1.0 - lb
    fg = lb + one_m * sig
    lf = jnp.log(jnp.maximum(fg, F_MIN))
    kf = one_m * (1.0 - sig)
    qs = _sig(q)
    return q * qs, qs, kf, lf, fg, sig


def _hg_cum(lf, C):
    ri = lax.broadcasted_iota(jnp.int32, (C, C), 0)
    ci = lax.broadcasted_iota(jnp.int32, (C, C), 1)
    return _dot_f32(jnp.where(ci <= ri, 1.0, 0.0).astype(F32), lf)


def _hg_levels(lf, cum, C):
    row = lax.broadcasted_iota(jnp.int32, (C, 1), 0)
    levels = []
    w = C // 2
    while w >= 4:
        blk = 2 * w
        upper = (row & w) != 0
        ref = jnp.min(jnp.where(upper, 0.0, cum).reshape(C // blk, blk, HEAD), axis=1, keepdims=True)
        ref = jnp.broadcast_to(ref, (C // blk, blk, HEAD)).reshape(C, HEAD)
        d = cum - ref
        e = jnp.exp(jnp.where(upper, d, -d))
        levels.append((jnp.where(upper, e, 0.0), jnp.where(upper, 0.0, e), blk))
        w //= 2
    r4 = row & 3
    lf_prev = pltpu.roll(lf, 1, 0)
    lf_next = pltpu.roll(lf, C - 1, 0)
    eq = jnp.where(r4 >= 2, jnp.exp(jnp.where(r4 == 3, lf + lf_prev, lf)), 0.0)
    ek = jnp.where(r4 == 0, jnp.exp(lf_next), jnp.where(r4 == 1, 1.0, 0.0))
    levels.append((eq, ek, 4))
    odd = (row & 1) == 1
    levels.append((jnp.where(odd, jnp.exp(lf), 0.0), jnp.where(odd, 0.0, 1.0), 2))
    ones = jnp.ones_like(lf)
    levels.append((ones, ones, 1))
    return levels


def _same_block(C, blk):
    ri = lax.broadcasted_iota(jnp.int32, (C, C), 0)
    ci = lax.broadcasted_iota(jnp.int32, (C, C), 1)
    if blk == 1:
        return ri == ci
    shift = blk.bit_length() - 1
    return (ri >> shift) == (ci >> shift)


def _hg_scores(qf, kf, levels, C):
    A = None
    for eq, ek, blk in levels:
        a = _dot_nt((qf * eq).astype(BF16), (kf * ek).astype(BF16))
        if blk != C:
            a = jnp.where(_same_block(C, blk), a, 0.0)
        A = a if A is None else A + a
    return A


def _hg_specs(B, NC, D, C, dtype_blocks):
    def spec(col0, rev):
        if rev:
            return pl.BlockSpec((C, HEAD), lambda b, h, j: (b * NC + (NC - 1 - j), col0 + h))
        return pl.BlockSpec((C, HEAD), lambda b, h, j: (b * NC + j, col0 + h))
    return spec


def _hg_fwd(proj, B, lb, gn):
    T = proj.shape[0]
    D = proj.shape[1] // N_SEG
    S = T // B
    C = min(HG_CHUNK, S)
    NC = S // C
    H = D // HEAD
    hpd = D // HEAD
    spec = _hg_specs(B, NC, D, C, None)

    def body(q_ref, z_ref, v_ref, g_ref, lb_ref, gn_ref, yb_ref, o_ref, st_ref, st_scr):
        @pl.when(pl.program_id(2) == 0)
        def _():
            st_scr[...] = jnp.zeros((HEAD, HEAD), F32)

        s_t = st_scr[...]
        st_ref[...] = s_t
        qf, _, kf, lf, _, _ = _hg_gates(q_ref[...], z_ref[...], lb_ref[...])
        cum = _hg_cum(lf, C)
        A = _hg_scores(qf, kf, _hg_levels(lf, cum, C), C)
        vb = v_ref[...].astype(BF16)
        o = _dot_nt((qf * jnp.exp(cum)).astype(BF16), s_t.astype(BF16)) + _dot(A.astype(BF16), vb)
        last = jnp.sum(lf, axis=0, keepdims=True)
        kend = kf * jnp.exp(last - cum)
        st_scr[...] = jnp.exp(last) * s_t + _dot_tn(vb, kend.astype(BF16))
        o_ref[...] = o
        g = g_ref[...]
        yb_ref[...] = (o * _rms(o) * gn_ref[...]) * (g * _sig(g))

    return pl.pallas_call(
        body, name="hg_fwd",
        grid=(B, H, NC),
        in_specs=[spec(2 * hpd, False), spec(3 * hpd, False), spec(4 * hpd, False), spec(5 * hpd, False),
                  pl.BlockSpec((1, HEAD), lambda b, h, j: (0, h)),
                  pl.BlockSpec((1, HEAD), lambda b, h, j: (0, 0))],
        out_specs=[spec(0, False), spec(0, False),
                   pl.BlockSpec((None, None, None, HEAD, HEAD), lambda b, h, j: (b, h, j, 0, 0))],
        out_shape=[jax.ShapeDtypeStruct((T, D), F32), jax.ShapeDtypeStruct((T, D), F32),
                   jax.ShapeDtypeStruct((B, H, NC, HEAD, HEAD), F32)],
        scratch_shapes=[pltpu.VMEM((HEAD, HEAD), F32)],
        compiler_params=_cp(("parallel", "parallel", "arbitrary")),
    )(proj, proj, proj, proj, lb, gn)


def _w_full(ref):
    s, r, c = ref.shape
    return ref[...].reshape(s * r, c)


def _out_fwd(ya, yb, proj, x2, w_st, layer):
    T, D = x2.shape
    tm = _tile(T, 512)

    def body(ya_ref, yb_ref, ma_ref, mb_ref, x_ref, w_ref, xm_ref, y_ref):
        y = (_sig(ma_ref[...]) * ya_ref[...] + _sig(mb_ref[...]) * yb_ref[...]).astype(BF16)
        y_ref[...] = y
        xm_ref[...] = x_ref[...] + _dot(y, _w_full(w_ref))

    row = pl.BlockSpec((tm, D), lambda i: (i, 0))
    return pl.pallas_call(
        body, name="out_fwd",
        grid=(T // tm,),
        in_specs=[row, row, pl.BlockSpec((tm, D), lambda i: (i, 6)), pl.BlockSpec((tm, D), lambda i: (i, 7)), row,
                  pl.BlockSpec((N_SHARD, None, D // N_SHARD, D), lambda i: (0, layer, 0, 0))],
        out_specs=[row, row],
        out_shape=[jax.ShapeDtypeStruct((T, D), F32), jax.ShapeDtypeStruct((T, D), BF16)],
        compiler_params=_cp(("parallel",)),
    )(ya, yb, proj, proj, x2, w_st)


def _mlp_fwd(xm, gain, wup_st, wdn_st, layer):
    T, D = xm.shape
    F4 = wup_st.shape[3]
    tm = _tile(T, 512)

    def body(x_ref, g_ref, wu_ref, wd_ref, xo_ref, up_ref, h_ref):
        @pl.when(pl.program_id(1) == 0)
        def _():
            x = x_ref[...]
            h_ref[...] = (x * _rms(x) * g_ref[...]).astype(BF16)
            xo_ref[...] = x
        up = _dot(h_ref[...], wu_ref[...])
        up_ref[...] = up
        act = jnp.maximum(up, 0.0)
        xo_ref[...] += _dot((act * act).astype(BF16), wd_ref[...])

    row = pl.BlockSpec((tm, D), lambda i, s: (i, 0))
    return pl.pallas_call(
        body, name="mlp_fwd",
        grid=(T // tm, N_SHARD),
        in_specs=[row, pl.BlockSpec((1, D), lambda i, s: (0, 0)),
                  pl.BlockSpec((None, None, D, F4), lambda i, s: (s, layer, 0, 0)),
                  pl.BlockSpec((None, None, F4, D), lambda i, s: (s, layer, 0, 0))],
        out_specs=[row, pl.BlockSpec((tm, F4), lambda i, s: (i, s)), row],
        out_shape=[jax.ShapeDtypeStruct((T, D), F32), jax.ShapeDtypeStruct((T, N_SHARD * F4), F32),
                   jax.ShapeDtypeStruct((T, D), BF16)],
        compiler_params=_cp(("parallel", "arbitrary")),
    )(xm, gain, wup_st, wdn_st)


def _final_loss(x2, gain, tgt):
    T, D = x2.shape
    tm = _tile(T, 512)
    nt = T // tm

    def body(x_ref, g_ref, t_ref, loss_ref, dx_ref, dg_ref):
        x = x_ref[...]
        rs = _rms(x)
        err = x * rs * g_ref[...] - t_ref[...]
        part = 0.5 * jnp.sum(jnp.sum(err * err, axis=-1, keepdims=True) * (1.0 / D), axis=0, keepdims=True)
        loss_ref[...] = jnp.broadcast_to(part, (SUBLANES, 128))
        dx, dg = _rms_bwd(err * (1.0 / D), x, rs, g_ref[...])
        dx_ref[...] = dx
        dg_ref[...] = dg

    row = pl.BlockSpec((tm, D), lambda i: (i, 0))
    return pl.pallas_call(
        body, name="final_loss",
        grid=(nt,),
        in_specs=[row, pl.BlockSpec((1, D), lambda i: (0, 0)), row],
        out_specs=[pl.BlockSpec((None, SUBLANES, 128), lambda i: (i, 0, 0)), row,
                   pl.BlockSpec((None, SUBLANES, D), lambda i: (i, 0, 0))],
        out_shape=[jax.ShapeDtypeStruct((nt, SUBLANES, 128), F32), jax.ShapeDtypeStruct((T, D), F32),
                   jax.ShapeDtypeStruct((nt, SUBLANES, D), F32)],
        compiler_params=_cp(("parallel",)),
    )(x2, gain, tgt)


def _mlp_bwd_x(dx, xm, up, gain, wup_st, wdn_st, layer):
    T, D = xm.shape
    F4 = wup_st.shape[3]
    tm = _tile(T, 512)
    nt = T // tm

    def body(dx_ref, x_ref, up_ref, g_ref, wu_ref, wd_ref, dxm_ref, dup_ref, dg_ref, dxb):
        s = pl.program_id(1)

        @pl.when(s == 0)
        def _():
            dxb[...] = dx_ref[...].astype(BF16)
            dxm_ref[...] = jnp.zeros((tm, D), F32)

        d_act = _dot_nt(dxb[...], wd_ref[...])
        d_up = (d_act * (2.0 * jnp.maximum(up_ref[...], 0.0))).astype(BF16)
        dup_ref[...] = d_up
        dxm_ref[...] += _dot_nt(d_up, wu_ref[...])

        @pl.when(s == N_SHARD - 1)
        def _():
            x = x_ref[...]
            dxn, dg = _rms_bwd(dxm_ref[...], x, _rms(x), g_ref[...])
            dxm_ref[...] = dx_ref[...] + dxn
            dg_ref[...] = dg

    row = pl.BlockSpec((tm, D), lambda i, s: (i, 0))
    return pl.pallas_call(
        body, name="mlp_bwd_x",
        grid=(nt, N_SHARD),
        in_specs=[row, row, pl.BlockSpec((tm, F4), lambda i, s: (i, s)), pl.BlockSpec((1, D), lambda i, s: (0, 0)),
                  pl.BlockSpec((None, None, D, F4), lambda i, s: (s, layer, 0, 0)),
                  pl.BlockSpec((None, None, F4, D), lambda i, s: (s, layer, 0, 0))],
        out_specs=[row, pl.BlockSpec((tm, F4), lambda i, s: (i, s)),
                   pl.BlockSpec((None, SUBLANES, D), lambda i, s: (i, 0, 0))],
        out_shape=[jax.ShapeDtypeStruct((T, D), F32), jax.ShapeDtypeStruct((T, N_SHARD * F4), BF16),
                   jax.ShapeDtypeStruct((nt, SUBLANES, D), F32)],
        scratch_shapes=[pltpu.VMEM((tm, D), BF16)],
        compiler_params=_cp(("parallel", "arbitrary")),
    )(dx, xm, up, gain, wup_st, wdn_st)


def _mlp_bwd_w(up, dx, h, dup):
    T, D = dx.shape
    F4 = up.shape[1] // N_SHARD
    tk = _tile(T, 512)

    def body(up_ref, dx_ref, h_ref, dup_ref, gu_ref, gd_ref):
        @pl.when(pl.program_id(1) == 0)
        def _():
            gu_ref[...] = jnp.zeros((D, F4), F32)
            gd_ref[...] = jnp.zeros((F4, D), F32)
        act = jnp.maximum(up_ref[...], 0.0)
        gd_ref[...] += _dot_tn((act * act).astype(BF16), dx_ref[...].astype(BF16))
        gu_ref[...] += _dot_tn(h_ref[...], dup_ref[...])

    return pl.pallas_call(
        body, name="mlp_bwd_w",
        grid=(N_SHARD, T // tk),
        in_specs=[pl.BlockSpec((tk, F4), lambda s, t: (t, s)), pl.BlockSpec((tk, D), lambda s, t: (t, 0)),
                  pl.BlockSpec((tk, D), lambda s, t: (t, 0)), pl.BlockSpec((tk, F4), lambda s, t: (t, s))],
        out_specs=[pl.BlockSpec((None, D, F4), lambda s, t: (s, 0, 0)),
                   pl.BlockSpec((None, F4, D), lambda s, t: (s, 0, 0))],
        out_shape=[jax.ShapeDtypeStruct((N_SHARD, D, F4), F32), jax.ShapeDtypeStruct((N_SHARD, F4, D), F32)],
        compiler_params=_cp(("parallel", "arbitrary")),
    )(up, dx, h, dup)


def _out_bwd_x(dxm, ya, yb, proj, w_st, layer):
    T, D = dxm.shape
    tm = _tile(T, 512)

    def body(dx_ref, ya_ref, yb_ref, ma_ref, mb_ref, w_ref, dya_ref, dyb_ref, dma_ref, dmb_ref):
        dy = _dot_nt(dx_ref[...].astype(BF16), _w_full(w_ref))
        sa = _sig(ma_ref[...])
        sb = _sig(mb_ref[...])
        dya_ref[...] = dy * sa
        dyb_ref[...] = dy * sb
        dma_ref[...] = (dy * ya_ref[...] * (sa * (1.0 - sa))).astype(BF16)
        dmb_ref[...] = (dy * yb_ref[...] * (sb * (1.0 - sb))).astype(BF16)

    row = pl.BlockSpec((tm, D), lambda i: (i, 0))
    return pl.pallas_call(
        body, name="out_bwd_x",
        grid=(T // tm,),
        in_specs=[row, row, row, pl.BlockSpec((tm, D), lambda i: (i, 6)), pl.BlockSpec((tm, D), lambda i: (i, 7)),
                  pl.BlockSpec((N_SHARD, None, D // N_SHARD, D), lambda i: (0, layer, 0, 0))],
        out_specs=[row] * 4,
        out_shape=[jax.ShapeDtypeStruct((T, D), F32)] * 2 + [jax.ShapeDtypeStruct((T, D), BF16)] * 2,
        compiler_params=_cp(("parallel",)),
    )(dxm, ya, yb, proj, proj, w_st)


def _out_bwd_w(ymix, dxm):
    T, D = dxm.shape
    tk = _tile(T, 512)

    def body(y_ref, dx_ref, g_ref):
        @pl.when(pl.program_id(0) == 0)
        def _():
            g_ref[...] = jnp.zeros((N_SHARD, D // N_SHARD, D), F32)
        g = _dot_tn(y_ref[...], dx_ref[...].astype(BF16))
        g_ref[...] += g.reshape(N_SHARD, D // N_SHARD, D)

    row = pl.BlockSpec((tk, D), lambda t: (t, 0))
    return pl.pallas_call(
        body, name="out_bwd_w",
        grid=(T // tk,),
        in_specs=[row, row],
        out_specs=pl.BlockSpec((N_SHARD, D // N_SHARD, D), lambda t: (0, 0, 0)),
        out_shape=jax.ShapeDtypeStruct((N_SHARD, D // N_SHARD, D), F32),
        compiler_params=_cp(("arbitrary",)),
    )(ymix, dxm)


def _rg_bwd(proj, hrg, dya, B, cw, cb, wr, br, wi, bi, sp):
    T = proj.shape[0]
    D = proj.shape[1] // N_SEG
    S = T // B
    ts = _tile(S, RG_TILE)
    nts = S // ts
    nb = D // RG_BLOCK
    t8 = ts // SUBLANES

    def body(xa_ref, xp_ref, ga_ref, h_ref, hp_ref, dya_ref, cw_ref, cb_ref, wr_ref, br_ref, wi_ref, bi_ref, sp_ref,
             dxa_ref, dga_ref, gwr_ref, gwi_ref, gcw_ref, gcb_ref, gbr_ref, gbi_ref, gsp_ref,
             xbuf, hbuf, abuf, dbuf, g_scr, c_scr, gcar):
        b = pl.program_id(0)
        j = pl.program_id(1)
        first_in_time = j == nts - 1

        @pl.when((b == 0) & (j == 0))
        def _():
            gwr_ref[...] = jnp.zeros((nb, RG_BLOCK, RG_BLOCK), F32)
            gwi_ref[...] = jnp.zeros((nb, RG_BLOCK, RG_BLOCK), F32)
            gcw_ref[...] = jnp.zeros((CONV_TAPS, SUBLANES, D), F32)
            for r in (gcb_ref, gbr_ref, gbi_ref, gsp_ref):
                r[...] = jnp.zeros((SUBLANES, D), F32)

        @pl.when(j == 0)
        def _():
            abuf[pl.ds(ts, SUBLANES), :] = jnp.zeros((SUBLANES, D), F32)
            dbuf[pl.ds(ts, SUBLANES), :] = jnp.zeros((SUBLANES, D), F32)
            gcar[...] = jnp.zeros((SUBLANES, D), F32)

        keep = jnp.where(first_in_time, 0.0, 1.0)
        xbuf[0:SUBLANES, :] = xp_ref[...] * keep
        xbuf[pl.ds(SUBLANES, ts), :] = xa_ref[...]
        hbuf[0:SUBLANES, :] = hp_ref[...] * keep
        hbuf[pl.ds(SUBLANES, ts), :] = h_ref[...]

        xc = _conv_taps(xbuf, cw_ref, ts) + cb_ref[...]
        sp = sp_ref[...]
        r, i, a, mult = _rg_gates(xc, wr_ref, br_ref[...], wi_ref, bi_ref[...], sp)
        g_gate, dg_gate = _gelu_and_grad(ga_ref[...])
        dya = dya_ref[...]
        dga_ref[...] = (dya * h_ref[...] * dg_gate).astype(BF16)

        abuf[0:ts, :] = a
        c_scr[...] = abuf[pl.ds(1, ts), :]
        g_scr[...] = dya * g_gate
        row8 = lax.broadcasted_iota(jnp.int32, (SUBLANES, 1), 0)

        def blk(n, gnext):
            off = pl.multiple_of((t8 - 1 - n) * SUBLANES, SUBLANES)
            c8 = c_scr[pl.ds(off, SUBLANES), :]
            d8 = g_scr[pl.ds(off, SUBLANES), :]
            for d in (1, 2, 4):
                m = row8 < SUBLANES - d
                cn = jnp.where(m, pltpu.roll(c8, SUBLANES - d, 0), 1.0)
                dn = jnp.where(m, pltpu.roll(d8, SUBLANES - d, 0), 0.0)
                d8 = d8 + c8 * dn
                c8 = c8 * cn
            g8 = d8 + c8 * gnext
            g_scr[pl.ds(off, SUBLANES), :] = g8
            first = jnp.sum(jnp.where(row8 == 0, g8, 0.0), axis=0, keepdims=True)
            return jnp.broadcast_to(first, (SUBLANES, D))

        gcar[...] = lax.fori_loop(0, t8, blk, gcar[...])
        abuf[pl.ds(ts, SUBLANES), :] = a[0:SUBLANES, :]

        g = g_scr[...]
        hprev = hbuf[pl.ds(SUBLANES - 1, ts), :]
        gx = i * xc
        e2 = a * a
        dla = g * hprev * a - jnp.where(mult > 0.0, g * gx * e2 / jnp.where(mult > 0.0, mult, 1.0), 0.0)
        dgx = g * mult
        dpr = (dla * ((-RG_C) * sp)) * (r * (1.0 - r))
        dpi = (dgx * xc) * (i * (1.0 - i))
        gsp_ref[...] += _rows8(dla * ((-RG_C) * r))
        gbr_ref[...] += _rows8(dpr)
        gbi_ref[...] += _rows8(dpi)
        dprb = dpr.astype(BF16)
        dpib = dpi.astype(BF16)
        xcb = xc.astype(BF16)
        back = []
        for n in range(nb):
            sl = slice(n * RG_BLOCK, (n + 1) * RG_BLOCK)
            back.append(_dot_nt(dprb[:, sl], wr_ref[n]) + _dot_nt(dpib[:, sl], wi_ref[n]))
            gwr_ref[n] += _dot_tn(xcb[:, sl], dprb[:, sl])
            gwi_ref[n] += _dot_tn(xcb[:, sl], dpib[:, sl])
        dxc = dgx * i + (jnp.concatenate(back, axis=1) if nb > 1 else back[0])
        gcb_ref[...] += _rows8(dxc)

        dbuf[0:ts, :] = dxc
        dxa = None
        for jtap in range(CONV_TAPS):
            term = cw_ref[jtap:jtap + 1, :] * dbuf[pl.ds(CONV_TAPS - 1 - jtap, ts), :]
            dxa = term if dxa is None else dxa + term
            gcw_ref[jtap] += _rows8(dxc * xbuf[pl.ds(SUBLANES - (CONV_TAPS - 1) + jtap, ts), :])
        dxa_ref[...] = dxa.astype(BF16)
        dbuf[pl.ds(ts, SUBLANES), :] = dxc[0:SUBLANES, :]

    def tile_map(col):
        return lambda b, j: (b * nts + (nts - 1 - j), col)

    def prev8_map(col):
        return lambda b, j: (jnp.maximum((b * nts + (nts - 1 - j)) * t8 - 1, 0), col)

    vec = pl.BlockSpec((1, D), lambda b, j: (0, 0))
    gate = pl.BlockSpec((nb, RG_BLOCK, RG_BLOCK), lambda b, j: (0, 0, 0))
    acc8 = pl.BlockSpec((SUBLANES, D), lambda b, j: (0, 0))
    return pl.pallas_call(
        body, name="rg_bwd",
        grid=(B, nts),
        in_specs=[pl.BlockSpec((ts, D), tile_map(0)), pl.BlockSpec((SUBLANES, D), prev8_map(0)),
                  pl.BlockSpec((ts, D), tile_map(1)),
                  pl.BlockSpec((ts, D), tile_map(0)), pl.BlockSpec((SUBLANES, D), prev8_map(0)),
                  pl.BlockSpec((ts, D), tile_map(0)),
                  pl.BlockSpec((CONV_TAPS, D), lambda b, j: (0, 0)), vec, gate, vec, gate, vec, vec],
        out_specs=[pl.BlockSpec((ts, D), tile_map(0)), pl.BlockSpec((ts, D), tile_map(0)), gate, gate,
                   pl.BlockSpec((CONV_TAPS, SUBLANES, D), lambda b, j: (0, 0, 0)), acc8, acc8, acc8, acc8],
        out_shape=[jax.ShapeDtypeStruct((T, D), BF16)] * 2
        + [jax.ShapeDtypeStruct((nb, RG_BLOCK, RG_BLOCK), F32)] * 2
        + [jax.ShapeDtypeStruct((CONV_TAPS, SUBLANES, D), F32)] + [jax.ShapeDtypeStruct((SUBLANES, D), F32)] * 4,
        scratch_shapes=[pltpu.VMEM((SUBLANES + ts, D), F32), pltpu.VMEM((SUBLANES + ts, D), F32),
                        pltpu.VMEM((ts + SUBLANES, D), F32), pltpu.VMEM((ts + SUBLANES, D), F32),
                        pltpu.VMEM((ts, D), F32), pltpu.VMEM((ts, D), F32), pltpu.VMEM((SUBLANES, D), F32)],
        compiler_params=_cp(("arbitrary", "arbitrary")),
    )(proj, proj, proj, hrg, hrg, dya, cw, cb, wr, br, wi, bi, sp)


def _hg_bwd(proj, o_sv, dyb, states, B, lb, gn):
    T = proj.shape[0]
    D = proj.shape[1] // N_SEG
    S = T // B
    C = min(HG_CHUNK, S)
    NC = S // C
    H = D // HEAD
    hpd = D // HEAD
    spec = _hg_specs(B, NC, D, C, None)

    def body(q_ref, z_ref, v_ref, g_ref, o_ref, dyb_ref, st_ref, lb_ref, gn_ref,
             dq_ref, dz_ref, dv_ref, dg_ref, glb_ref, ggn_ref, ds_scr):
        @pl.when(pl.program_id(2) == 0)
        def _():
            ds_scr[...] = jnp.zeros((HEAD, HEAD), F32)
            glb_ref[...] = jnp.zeros((SUBLANES, HEAD), F32)
            ggn_ref[...] = jnp.zeros((SUBLANES, HEAD), F32)

        q = q_ref[...]
        lb = lb_ref[...]
        gn = gn_ref[...]
        qf, qs, kf, lf, fg, sig = _hg_gates(q, z_ref[...], lb)
        cum = _hg_cum(lf, C)
        levels = _hg_levels(lf, cum, C)

        o = o_ref[...]
        g = g_ref[...]
        gs = _sig(g)
        rs = _rms(o)
        dyb = dyb_ref[...]
        don = dyb * (g * gs)
        dg_ref[...] = (dyb * (o * rs * gn) * (gs * (1.0 + g * (1.0 - gs)))).astype(BF16)
        ggn_ref[...] += _rows8(don * o * rs)
        dn = don * gn
        do = rs * (dn - o * (rs * rs) * jnp.mean(dn * o, axis=-1, keepdims=True))

        s_t = st_ref[...].astype(BF16)
        ds_t = ds_scr[...]
        ds_b = ds_t.astype(BF16)
        dob = do.astype(BF16)
        vb = v_ref[...].astype(BF16)
        ecum = jnp.exp(cum)
        last = jnp.sum(lf, axis=0, keepdims=True)
        eend = jnp.exp(last - cum)
        qhat = (qf * ecum).astype(BF16)
        kend = (kf * eend).astype(BF16)

        dA = _dot_nt(dob, vb)
        dq_inter = _dot(dob, s_t)
        dk_state = _dot(vb, ds_b)
        dqf = dq_inter * ecum
        dkf = dk_state * eend
        A = None
        g_intra = None
        for eq, ek, blk in levels:
            qw = (qf * eq).astype(BF16)
            kw = (kf * ek).astype(BF16)
            a = _dot_nt(qw, kw)
            if blk != C:
                mask = _same_block(C, blk)
                a = jnp.where(mask, a, 0.0)
                dam = jnp.where(mask, dA, 0.0).astype(BF16)
            else:
                dam = dA.astype(BF16)
            A = a if A is None else A + a
            rq = _dot(dam, kw)
            rk = _dot_tn(dam, qw)
            dqf += rq * eq
            dkf += rk * ek
            gi = qw.astype(F32) * rq - kw.astype(F32) * rk
            g_intra = gi if g_intra is None else g_intra + gi
        dv_ref[...] = (_dot_tn(A.astype(BF16), dob) + _dot_nt(kend, ds_b)).astype(BF16)
        e_last = jnp.exp(last)
        ds_scr[...] = e_last * ds_t + _dot_tn(dob, qhat)

        ri = lax.broadcasted_iota(jnp.int32, (C, C), 0)
        ci = lax.broadcasted_iota(jnp.int32, (C, C), 1)
        dlf = (_dot_f32(jnp.where(ci >= ri, 1.0, 0.0).astype(F32), g_intra + qhat.astype(F32) * dq_inter)
               + _dot_f32(jnp.where(ci < ri, 1.0, 0.0).astype(F32), kend.astype(F32) * dk_state)
               + jnp.sum(e_last * st_ref[...] * ds_t, axis=0, keepdims=True))
        dfg = jnp.where(fg > F_MIN, dlf / jnp.maximum(fg, F_MIN), 0.0)
        sneg = 1.0 - sig
        diff = dfg - dkf
        dz_ref[...] = ((1.0 - lb) * sig * sneg * diff).astype(BF16)
        glb_ref[...] += _rows8(sneg * diff)
        dq_ref[...] = (dqf * (qs * (1.0 + q * (1.0 - qs)))).astype(BF16)

    return pl.pallas_call(
        body, name="hg_bwd",
        grid=(B, H, NC),
        in_specs=[spec(2 * hpd, True), spec(3 * hpd, True), spec(4 * hpd, True), spec(5 * hpd, True),
                  spec(0, True), spec(0, True),
                  pl.BlockSpec((None, None, None, HEAD, HEAD), lambda b, h, j: (b, h, NC - 1 - j, 0, 0)),
                  pl.BlockSpec((1, HEAD), lambda b, h, j: (0, h)),
                  pl.BlockSpec((1, HEAD), lambda b, h, j: (0, 0))],
        out_specs=[spec(0, True)] * 4
        + [pl.BlockSpec((None, SUBLANES, HEAD), lambda b, h, j: (b, 0, h)),
           pl.BlockSpec((None, None, SUBLANES, HEAD), lambda b, h, j: (b, h, 0, 0))],
        out_shape=[jax.ShapeDtypeStruct((T, D), BF16)] * 4
        + [jax.ShapeDtypeStruct((B, SUBLANES, D), F32), jax.ShapeDtypeStruct((B, H, SUBLANES, HEAD), F32)],
        scratch_shapes=[pltpu.VMEM((HEAD, HEAD), F32)],
        compiler_params=_cp(("parallel", "parallel", "arbitrary")),
    )(proj, proj, proj, proj, o_sv, dyb, states, lb, gn)


def _inproj_bwd_x(dsegs, w_st, layer, x2, gain, dxm):
    T, D = x2.shape
    tm = _tile(T, 512)
    nt = T // tm

    def body(*refs):
        seg_refs = refs[:N_SEG]
        w_ref, x_ref, g_ref, dxm_ref, dx_ref, dg_ref = refs[N_SEG:]
        k = pl.program_id(1)

        @pl.when(k == 0)
        def _():
            dx_ref[...] = jnp.zeros((tm, D), F32)

        for kk in range(N_SEG):
            @pl.when(k == kk)
            def _(kk=kk):
                dx_ref[...] += _dot_nt(seg_refs[kk][...], w_ref[...])

        @pl.when(k == N_SEG - 1)
        def _():
            x = x_ref[...]
            dxn, dg = _rms_bwd(dx_ref[...], x, _rms(x), g_ref[...])
            dx_ref[...] = dxm_ref[...] + dxn
            dg_ref[...] = dg

    row = pl.BlockSpec((tm, D), lambda i, k: (i, 0))
    return pl.pallas_call(
        body, name="inproj_bwd_x",
        grid=(nt, N_SEG),
        in_specs=[row] * N_SEG
        + [pl.BlockSpec((None, None, D, D), lambda i, k: (k // 2, layer, 0, k % 2)), row,
           pl.BlockSpec((1, D), lambda i, k: (0, 0)), row],
        out_specs=[row, pl.BlockSpec((None, SUBLANES, D), lambda i, k: (i, 0, 0))],
        out_shape=[jax.ShapeDtypeStruct((T, D), F32), jax.ShapeDtypeStruct((nt, SUBLANES, D), F32)],
        compiler_params=_cp(("parallel", "arbitrary")),
    )(*dsegs, w_st, x2, gain, dxm)


def _inproj_bwd_w(h, dsegs):
    T, D = h.shape
    tk = _tile(T, 512)

    def body(*refs):
        h_ref = refs[0]
        seg_refs = refs[1:1 + N_SEG]
        g_ref = refs[1 + N_SEG]
        k = pl.program_id(0)

        @pl.when(pl.program_id(1) == 0)
        def _():
            g_ref[...] = jnp.zeros((D, D), F32)

        for kk in range(N_SEG):
            @pl.when(k == kk)
            def _(kk=kk):
                g_ref[...] += _dot_tn(h_ref[...], seg_refs[kk][...])

    def seg_spec(kk):
        return pl.BlockSpec((tk, D), lambda k, t: (jnp.where(k == kk, t, 0), 0))

    return pl.pallas_call(
        body, name="inproj_bwd_w",
        grid=(N_SEG, T // tk),
        in_specs=[pl.BlockSpec((tk, D), lambda k, t: (t, 0))] + [seg_spec(kk) for kk in range(N_SEG)],
        out_specs=pl.BlockSpec((None, D, D), lambda k, t: (k // 2, 0, k % 2)),
        out_shape=jax.ShapeDtypeStruct((N_SHARD, D, 2 * D), F32),
        compiler_params=_cp(("parallel", "arbitrary")),
    )(h, *dsegs)


def _softmax_rows(lg_ref, L):
    rows = [lg_ref[l:l + 1, :] for l in range(L)]
    mx = functools.reduce(jnp.maximum, rows)
    es = [jnp.exp(r - mx) for r in rows]
    den = functools.reduce(lambda p, q: p + q, es)
    return [e / den for e in es]


def _prep(lb_logits, lam):
    L, D = lb_logits.shape

    def body(lg_ref, lam_ref, lowb_ref, sp_ref):
        sm = _softmax_rows(lg_ref, L)
        run = sm[0]
        for l in range(L):
            if l > 0:
                run = run + sm[l]
            lowb_ref[l:l + 1, :] = jnp.clip(run - sm[0], 0.0, 1.0)
        y = -lam_ref[...]
        sp_ref[...] = jnp.maximum(y, 0.0) + jnp.log1p(jnp.exp(-jnp.abs(y)))

    return pl.pallas_call(
        body, name="prep_small",
        out_shape=[jax.ShapeDtypeStruct((L, D), F32)] * 2,
    )(lb_logits, lam)


def _local_step(x, tgt, lowb, sp, norm_mix, w_in_st, conv_w, conv_b, w_r, b_r, w_i, b_i, hg_norm,
                w_out_st, norm_mlp, w_up_st, w_down_st, norm_final):
    B, S, D = x.shape
    L = norm_mix.shape[0]
    T = B * S
    x2 = x.reshape(T, D)
    row = lambda a, l: a[l:l + 1]
    saved = []
    for l in range(L):
        proj, h = _inproj_fwd(x2, row(norm_mix, l), w_in_st, l)
        ya, hrg = _rg_fwd(proj, B, conv_w[l], row(conv_b, l), w_r[l], row(b_r, l), w_i[l], row(b_i, l), row(sp, l))
        yb, o, st = _hg_fwd(proj, B, row(lowb, l), row(hg_norm, l))
        xm, ymix = _out_fwd(ya, yb, proj, x2, w_out_st, l)
        xo, up, h2 = _mlp_fwd(xm, row(norm_mlp, l), w_up_st, w_down_st, l)
        saved.append((x2, proj, h, ya, hrg, yb, o, st, xm, ymix, up, h2))
        x2 = xo
    loss_parts, dx, g_nf = _final_loss(x2, norm_final[None, :], tgt.reshape(T, D))

    big = []
    small = []
    for l in reversed(range(L)):
        x_in, proj, h, ya, hrg, yb, o, st, xm, ymix, up, h2 = saved[l]
        dxm, dup, g_nmlp = _mlp_bwd_x(dx, xm, up, row(norm_mlp, l), w_up_st, w_down_st, l)
        g_up, g_down = _mlp_bwd_w(up, dx, h2, dup)
        dya, dyb, dma, dmb = _out_bwd_x(dxm, ya, yb, proj, w_out_st, l)
        g_out = _out_bwd_w(ymix, dxm)
        dxa, dga, g_wr, g_wi, g_cw, g_cb, g_br, g_bi, g_sp = _rg_bwd(
            proj, hrg, dya, B, conv_w[l], row(conv_b, l), w_r[l], row(b_r, l), w_i[l], row(b_i, l), row(sp, l))
        dq, dz, dv, dg, g_lb, g_gn = _hg_bwd(proj, o, dyb, st, B, row(lowb, l), row(hg_norm, l))
        dsegs = (dxa, dga, dq, dz, dv, dg, dma, dmb)
        dx, g_nmix = _inproj_bwd_x(dsegs, w_in_st, l, x_in, row(norm_mix, l), dxm)
        g_in = _inproj_bwd_w(h, dsegs)
        big.append((g_in, g_out, g_up, g_down, g_wr, g_wi))
        small.append((g_lb, g_nmix, g_cb, g_br, g_bi, g_sp, g_nmlp, g_gn, g_cw))
    big.reverse()
    small.reverse()
    return loss_parts, dx.reshape(B, S, D), big, small, g_nf


def _me():
    return lax.axis_index("x"), lax.axis_index("y"), lax.axis_index("c")


def _gather_weights(shards):
    n = len(shards)

    def body(*refs):
        ins, outs = refs[:n], refs[n:2 * n]
        ssem, rsem, lsem = refs[2 * n:]
        x, y, c = _me()
        mine = 2 * x + y

        def piece(a, s):
            rh = ins[a].shape[0] // 2
            return outs[a].at[s, pl.ds(c * rh, rh), :]

        def other_piece(a, s):
            rh = ins[a].shape[0] // 2
            return outs[a].at[s, pl.ds((1 - c) * rh, rh), :]

        def rcopy(a, k, src, dst, dev):
            return pltpu.make_async_remote_copy(src_ref=src, dst_ref=dst, send_sem=ssem.at[a, k], recv_sem=rsem.at[a, k],
                                                device_id=dev, device_id_type=MESH_ID)

        y_nbr, x_nbr, sib = (x, 1 - y, c), (1 - x, y, c), (x, y, 1 - c)
        local = [pltpu.make_async_copy(ins[a], outs[a].at[mine], lsem.at[a]) for a in range(n)]
        for cp in local:
            cp.start()
        sent = []
        for a in range(n):
            rh = ins[a].shape[0] // 2
            sent.append(rcopy(a, 0, ins[a].at[pl.ds(c * rh, rh), :], piece(a, mine), y_nbr))
            sent[-1].start()
        for cp in local:
            cp.wait()
        for a in range(n):
            sent.append(rcopy(a, 1, piece(a, mine), piece(a, mine), x_nbr))
            sent[-1].start()
        for a in range(n):
            got = piece(a, 2 * x + (1 - y))
            rcopy(a, 0, got, got, y_nbr).wait_recv()
            sent.append(rcopy(a, 2, got, got, x_nbr))
            sent[-1].start()
            sent.append(rcopy(a, 3, got, got, sib))
            sent[-1].start()
        for a in range(n):
            for k, yy in ((1, y), (2, 1 - y)):
                got = piece(a, 2 * (1 - x) + yy)
                rcopy(a, k, got, got, x_nbr).wait_recv()
                sent.append(rcopy(a, 3 + k, got, got, sib))
                sent[-1].start()
        for a in range(n):
            for k, s in ((3, 2 * x + (1 - y)), (4, 2 * (1 - x) + y), (5, 2 * (1 - x) + (1 - y))):
                got = other_piece(a, s)
                rcopy(a, k, got, got, sib).wait_recv()
        for cp in sent:
            cp.wait_send()

    return pl.pallas_call(
        body, name="gather_weights",
        in_specs=[ANY] * n, out_specs=[ANY] * n,
        out_shape=[jax.ShapeDtypeStruct((N_SHARD,) + s.shape, s.dtype) for s in shards],
        scratch_shapes=[pltpu.SemaphoreType.DMA((n, 6)), pltpu.SemaphoreType.DMA((n, 6)), pltpu.SemaphoreType.DMA((n,))],
        compiler_params=pltpu.CompilerParams(has_side_effects=True),
    )(*shards)


def _exchange(arrs, axis, name):
    n = len(arrs)

    def body(*refs):
        ins, outs = refs[:n], refs[n:2 * n]
        ssem, rsem = refs[2 * n:]
        x, y, c = _me()
        my = {"x": x, "y": y, "c": c}[axis]
        partner = {"x": (1 - x, y, c), "y": (x, 1 - y, c), "c": (x, y, 1 - c)}[axis]
        cps = []
        for a in range(n):
            cps.append(pltpu.make_async_remote_copy(
                src_ref=ins[a].at[:, 1 - my], dst_ref=outs[a], send_sem=ssem.at[a], recv_sem=rsem.at[a],
                device_id=partner, device_id_type=MESH_ID))
            cps[-1].start()
        for cp in cps:
            cp.wait()

    return pl.pallas_call(
        body, name=name,
        in_specs=[ANY] * n, out_specs=[ANY] * n,
        out_shape=[jax.ShapeDtypeStruct((a.shape[0],) + a.shape[2:], a.dtype) for a in arrs],
        scratch_shapes=[pltpu.SemaphoreType.DMA((n,)), pltpu.SemaphoreType.DMA((n,))],
        compiler_params=pltpu.CompilerParams(has_side_effects=True),
    )(*arrs)


def _add_kept(arr, got, idx, name):
    P, _, R, N = arr.shape
    tr = _tile(R, max(16, (1 << 20) // N))

    def body(idx_ref, a_ref, g_ref, o_ref):
        o_ref[...] = a_ref[...] + g_ref[...]

    return pl.pallas_call(
        body, name=name,
        grid_spec=pltpu.PrefetchScalarGridSpec(
            num_scalar_prefetch=1, grid=(P, R // tr),
            in_specs=[pl.BlockSpec((None, None, tr, N), lambda p, i, idx: (p, idx[0], i, 0)),
                      pl.BlockSpec((None, tr, N), lambda p, i, idx: (p, i, 0))],
            out_specs=pl.BlockSpec((None, tr, N), lambda p, i, idx: (p, i, 0))),
        out_shape=jax.ShapeDtypeStruct((P, R, N), arr.dtype),
        compiler_params=_cp(("parallel", "parallel")),
    )(idx, arr, got)


def _share_halves(halves):
    n = len(halves)

    def body(*refs):
        ins, outs = refs[:n], refs[n:2 * n]
        ssem, rsem, lsem = refs[2 * n:]
        x, y, c = _me()
        cps = []
        for a in range(n):
            cps.append(pltpu.make_async_copy(ins[a], outs[a].at[c], lsem.at[a]))
            cps[-1].start()
            cps.append(pltpu.make_async_remote_copy(
                src_ref=ins[a], dst_ref=outs[a].at[c], send_sem=ssem.at[a], recv_sem=rsem.at[a],
                device_id=(x, y, 1 - c), device_id_type=MESH_ID))
            cps[-1].start()
        for cp in cps:
            cp.wait()

    return pl.pallas_call(
        body, name="share_halves",
        in_specs=[ANY] * n, out_specs=[ANY] * n,
        out_shape=[jax.ShapeDtypeStruct((2,) + h.shape, h.dtype) for h in halves],
        scratch_shapes=[pltpu.SemaphoreType.DMA((n,)), pltpu.SemaphoreType.DMA((n,)), pltpu.SemaphoreType.DMA((n,))],
        compiler_params=pltpu.CompilerParams(has_side_effects=True),
    )(*halves)


def _reduce_scatter(grads):
    x, y, c = _me()
    idx = lambda v: jnp.reshape(v, (1,)).astype(jnp.int32)
    views = [g.reshape(N_SHARD, 2, g.shape[1] // 2, g.shape[2]) for g in grads]
    got = _exchange(views, "c", "rs_exchange_c")
    kept = [_add_kept(v, r, idx(c), "rs_add_c") for v, r in zip(views, got)]
    views = [k.reshape(1, 2, 2 * k.shape[1], k.shape[2]) for k in kept]
    got = _exchange(views, "x", "rs_exchange_x")
    kept = [_add_kept(v, r, idx(x), "rs_add_x") for v, r in zip(views, got)]
    views = [k.reshape(1, 2, k.shape[1] // 2, k.shape[2]) for k in kept]
    got = _exchange(views, "y", "rs_exchange_y")
    kept = [_add_kept(v, r, idx(y), "rs_add_y")[0] for v, r in zip(views, got)]
    full = _share_halves(kept)
    return [f.reshape(g.shape[1], g.shape[2]) for f, g in zip(full, grads)]


def _allgather_small(p):
    R, D = p.shape

    def body(p_ref, o_ref, ssem, rsem):
        x, y, c = _me()
        me = 4 * x + 2 * y + c
        o_ref[me] = p_ref[...]
        cps = []
        for m in range(1, 8):
            mx, my, mc = (m >> 2) & 1, (m >> 1) & 1, m & 1
            peer = (1 - x if mx else x, 1 - y if my else y, 1 - c if mc else c)
            cps.append(pltpu.make_async_remote_copy(
                src_ref=p_ref, dst_ref=o_ref.at[me], send_sem=ssem.at[m - 1], recv_sem=rsem.at[m - 1],
                device_id=peer, device_id_type=MESH_ID))
            cps[-1].start()
        for cp in cps:
            cp.wait()

    return pl.pallas_call(
        body, name="allgather_small",
        in_specs=[pl.BlockSpec(memory_space=pltpu.VMEM)],
        out_specs=pl.BlockSpec(memory_space=pltpu.VMEM),
        out_shape=jax.ShapeDtypeStruct((8, R, D), p.dtype),
        scratch_shapes=[pltpu.SemaphoreType.DMA((7,)), pltpu.SemaphoreType.DMA((7,))],
        compiler_params=pltpu.CompilerParams(has_side_effects=True, vmem_limit_bytes=VMEM_LIMIT),
    )(p)


def _adam_math(w, g, m, v):
    m = ADAM_B1 * m + (1.0 - ADAM_B1) * g
    v = ADAM_B2 * v + (1.0 - ADAM_B2) * (g * g)
    m_hat = m / (1.0 - ADAM_B1 ** ADAM_STEP)
    v_hat = v / (1.0 - ADAM_B2 ** ADAM_STEP)
    delta = -ADAM_LR * (m_hat / (jnp.sqrt(v_hat) + ADAM_EPS) + ADAM_WD * w)
    return delta, m, v


def _adam(w, g, m, v):
    R, N = w.shape
    tr = _tile(R, max(16, (1 << 19) // N))

    def body(w_ref, g_ref, m_ref, v_ref, d_ref, nm_ref, nv_ref):
        d, nm, nv = _adam_math(w_ref[...], g_ref[...], m_ref[...], v_ref[...])
        d_ref[...] = d
        nm_ref[...] = nm
        nv_ref[...] = nv

    blk = pl.BlockSpec((tr, N), lambda i: (i, 0))
    return pl.pallas_call(
        body, name="adamw",
        grid=(R // tr,),
        in_specs=[blk] * 4, out_specs=[blk] * 3,
        out_shape=[jax.ShapeDtypeStruct((R, N), F32)] * 3,
        compiler_params=_cp(("parallel",)),
    )(w, g, m, v)


def _reduce_rows(parts, sizes, rows_out):
    D = parts.shape[1]

    def body(p_ref, o_ref):
        o_ref[...] = jnp.zeros((rows_out, D), F32)
        off = 0
        for i, sz in enumerate(sizes):
            o_ref[i:i + 1, :] = jnp.sum(p_ref[off:off + sz, :], axis=0, keepdims=True)
            off += sz

    return pl.pallas_call(
        body, name="reduce_rows",
        out_shape=jax.ShapeDtypeStruct((rows_out, D), F32),
        compiler_params=pltpu.CompilerParams(vmem_limit_bytes=VMEM_LIMIT),
    )(parts)


def _sum_devices(g8):
    _, R, D = g8.shape

    def body(g_ref, o_ref):
        tot = g_ref[0]
        for k in range(1, 8):
            tot = tot + g_ref[k]
        o_ref[...] = tot

    return pl.pallas_call(
        body, name="sum_devices",
        out_shape=jax.ShapeDtypeStruct((R, D), F32),
        compiler_params=pltpu.CompilerParams(vmem_limit_bytes=VMEM_LIMIT),
    )(g8)


def _small_update(gathered, w, m, v, L):
    _, R, D = gathered.shape

    def body(g8_ref, w_ref, m_ref, v_ref, g_ref, d_ref, nm_ref, nv_ref):
        tot = g8_ref[0]
        for k in range(1, 8):
            tot = tot + g8_ref[k]
        g_ref[...] = tot
        sm = _softmax_rows(w_ref, L)
        run = sm[0]
        dcum = []
        for l in range(L):
            if l > 0:
                run = run + sm[l]
            cum = run - sm[0]
            dcum.append(jnp.where((cum > 0.0) & (cum < 1.0), g_ref[l:l + 1, :], 0.0))
        dsm = [jnp.zeros((1, D), F32)]
        for i in range(1, L):
            dsm.append(functools.reduce(lambda p, q: p + q, dcum[i:]))
        dot = functools.reduce(lambda p, q: p + q, [s * d for s, d in zip(sm, dsm)])
        for l in range(L):
            g_ref[l:l + 1, :] = sm[l] * (dsm[l] - dot)
        lam = w_ref[5 * L:6 * L, :]
        g_ref[5 * L:6 * L, :] = g_ref[5 * L:6 * L, :] * (-_sig(-lam))
        d, nm, nv = _adam_math(w_ref[...], g_ref[...], m_ref[...], v_ref[...])
        d_ref[...] = d
        nm_ref[...] = nm
        nv_ref[...] = nv

    return pl.pallas_call(
        body, name="small_update",
        out_shape=[jax.ShapeDtypeStruct((R, D), F32)] * 4,
        compiler_params=pltpu.CompilerParams(vmem_limit_bytes=VMEM_LIMIT),
    )(gathered, w, m, v)


def kernel(x, lb_logits, norm_mix, w_in, conv_w, conv_b, w_r, b_r, w_i, b_i, lam, hg_norm, w_out, norm_mlp, w_up, w_down, norm_final, loss_target, m_lb_logits, m_norm_mix, m_w_in, m_conv_w, m_conv_b, m_w_r, m_b_r, m_w_i, m_b_i, m_lam, m_hg_norm, m_w_out, m_norm_mlp, m_w_up, m_w_down, m_norm_final, v_lb_logits, v_norm_mix, v_w_in, v_conv_w, v_conv_b, v_w_r, v_b_r, v_w_i, v_b_i, v_lam, v_hg_norm, v_w_out, v_norm_mlp, v_w_up, v_w_down, v_norm_final):
    B, S, D = x.shape
    L = norm_mix.shape[0]
    nb = D // RG_BLOCK
    Dq = D // N_SHARD
    mx, my, mc = _me()
    shard = 2 * mx + my

    big_w = (w_in, w_out, w_up, w_down, w_r, w_i)
    flat2 = lambda a: a.reshape(-1, a.shape[-1])
    g_in, g_out, g_up, g_down, g_r, g_i = _gather_weights([flat2(w).astype(BF16) for w in big_w])
    w_in_st = g_in.reshape(N_SHARD, L, D, 2 * D)
    w_out_st = g_out.reshape(N_SHARD, L, Dq, D)
    w_up_st = g_up.reshape((N_SHARD,) + w_up.shape)
    w_down_st = g_down.reshape((N_SHARD,) + w_down.shape)
    unshard_gate = lambda g: g.reshape(N_SHARD, L, nb, RG_BLOCK // N_SHARD, RG_BLOCK).transpose(1, 2, 0, 3, 4).reshape(
        L, nb, RG_BLOCK, RG_BLOCK)
    w_r_full, w_i_full = unshard_gate(g_r), unshard_gate(g_i)

    R_LB, R_NMIX, R_CB, R_BR, R_BI, R_LAM, R_NMLP, R_GN, R_CW, R_NF, R_LOSS = (
        0, L, 2 * L, 3 * L, 4 * L, 5 * L, 6 * L, 7 * L, 8 * L, 12 * L, 12 * L + 1)
    n_rows = 12 * L + 2
    rows_pad = n_rows + (-n_rows) % SUBLANES

    def place_cols(a):
        return lax.dynamic_update_slice(jnp.zeros((a.shape[0], D), F32), a, (0, shard * Dq))

    def pack_small(lb_, nmix_, cb_, br_, bi_, lam_, nmlp_, gn_, cw_, nf_):
        gn_pad = jnp.pad(gn_, ((0, 0), (0, D - HEAD)))
        rows = [lb_, nmix_, cb_, br_, bi_, lam_, nmlp_, gn_pad, place_cols(cw_.reshape(L * CONV_TAPS, Dq)),
                nf_[None, :], jnp.zeros((rows_pad - n_rows + 1, D), F32)]
        return jnp.concatenate(rows, axis=0)

    w_small = pack_small(lb_logits, norm_mix, conv_b, b_r, b_i, lam, norm_mlp, hg_norm, conv_w, norm_final)
    m_small = pack_small(m_lb_logits, m_norm_mix, m_conv_b, m_b_r, m_b_i, m_lam, m_norm_mlp, m_hg_norm, m_conv_w,
                         m_norm_final)
    v_small = pack_small(v_lb_logits, v_norm_mix, v_conv_b, v_b_r, v_b_i, v_lam, v_norm_mlp, v_hg_norm, v_conv_w,
                         v_norm_final)
    cw_rows = place_cols(conv_w.reshape(L * CONV_TAPS, Dq)) * jnp.where(mc == 0, 1.0, 0.0)
    conv_w_full = _sum_devices(_allgather_small(cw_rows)).reshape(L, CONV_TAPS, D)

    lowb, sp = _prep(lb_logits, lam)

    loss_parts, grad_x, big, small, g_nf = _local_step(
        x, loss_target, lowb, sp, norm_mix, w_in_st, conv_w_full, conv_b, w_r_full, b_r, w_i_full, b_i, hg_norm,
        w_out_st, norm_mlp, w_up_st, w_down_st, norm_final)

    shard_gate = lambda g: g.reshape(L, nb, N_SHARD, RG_BLOCK // N_SHARD, RG_BLOCK).transpose(2, 0, 1, 3, 4)
    stacked = [jnp.stack([big[l][i] for l in range(L)], axis=1) for i in range(4)]
    stacked += [shard_gate(jnp.stack([big[l][i] for l in range(L)])) for i in (4, 5)]
    summed = _reduce_scatter([s.reshape(N_SHARD, -1, s.shape[-1]) for s in stacked])
    outs = {}
    for name, w, m, v, g in zip(("w_in", "w_out", "w_up", "w_down", "w_r", "w_i"), big_w,
                                (m_w_in, m_w_out, m_w_up, m_w_down, m_w_r, m_w_i),
                                (v_w_in, v_w_out, v_w_up, v_w_down, v_w_r, v_w_i), summed):
        d, nm, nv = _adam(flat2(w), g, flat2(m), flat2(v))
        outs[name] = tuple(t.reshape(w.shape) for t in (g, d, nm, nv))

    parts, sizes = [], []

    def add_rows(a):
        a = a.reshape(-1, a.shape[-1])
        if a.shape[1] != D:
            a = jnp.pad(a, ((0, 0), (0, D - a.shape[1])))
        parts.append(a)
        sizes.append(a.shape[0])

    for i in range(8):
        for l in range(L):
            add_rows(small[l][i])
    for l in range(L):
        for j in range(CONV_TAPS):
            add_rows(small[l][8][j])
    add_rows(g_nf)
    loss_rows = loss_parts[:, 0:1, :]
    add_rows(jnp.where(lax.broadcasted_iota(jnp.int32, loss_rows.shape, 2) == 0, loss_rows, 0.0))
    g_small = _reduce_rows(jnp.concatenate(parts, axis=0), sizes, rows_pad)
    g_small, d_small, nm_small, nv_small = _small_update(_allgather_small(g_small), w_small, m_small, v_small, L)

    def unpack(t):
        take_cols = lambda a: lax.dynamic_slice(a, (0, shard * Dq), (a.shape[0], Dq))
        return {"lb_logits": t[R_LB:R_LB + L], "norm_mix": t[R_NMIX:R_NMIX + L], "conv_b": t[R_CB:R_CB + L],
                "b_r": t[R_BR:R_BR + L], "b_i": t[R_BI:R_BI + L], "lam": t[R_LAM:R_LAM + L],
                "norm_mlp": t[R_NMLP:R_NMLP + L], "hg_norm": t[R_GN:R_GN + L, :HEAD],
                "conv_w": take_cols(t[R_CW:R_CW + L * CONV_TAPS]).reshape(L, CONV_TAPS, Dq), "norm_final": t[R_NF]}

    small_out = [unpack(t) for t in (g_small, d_small, nm_small, nv_small)]
    loss = g_small[R_LOSS, 0]
    names = ("lb_logits", "norm_mix", "w_in", "conv_w", "conv_b", "w_r", "b_r", "w_i", "b_i", "lam", "hg_norm",
             "w_out", "norm_mlp", "w_up", "w_down", "norm_final")
    result = [loss, grad_x]
    for kind in range(4):
        for nme in names:
            result.append(outs[nme][kind] if nme in outs else small_out[kind][nme])
    return tuple(result)
```

```python
import functools
import math

import jax
import jax.numpy as jnp
from jax import lax
from jax.experimental import pallas as pl
from jax.experimental.pallas import tpu as pltpu

F32 = jnp.float32
BF16 = jnp.bfloat16

HEAD = 128
RG_BLOCK = 256
CONV_TAPS = 4
RG_C = 8.0
F_MIN = 1e-30
NORM_EPS = 1e-6
N_SEG = 8
N_SHARD = 4
HG_CHUNK = 256
RG_TILE = 256
ADAM_LR, ADAM_B1, ADAM_B2, ADAM_EPS, ADAM_WD, ADAM_STEP = 0.001, 0.9, 0.999, 1e-08, 0.01, 10
V7X_VMEM_BYTES = 64 * 1024 * 1024
VMEM_LIMIT = V7X_VMEM_BYTES - 8 * 1024 * 1024
SUBLANES = 8
LINK_SPLIT = "xxyyyy"
MESH_ID = pl.DeviceIdType.MESH
ANY = pl.BlockSpec(memory_space=pl.ANY)


def _cp(sem):
    return pltpu.CompilerParams(dimension_semantics=sem, vmem_limit_bytes=VMEM_LIMIT)


def _dot(a, b):
    return jnp.dot(a, b, preferred_element_type=F32)


def _dot_nt(a, b):
    return lax.dot_general(a, b, (((1,), (1,)), ((), ())), preferred_element_type=F32)


def _dot_tn(a, b):
    return lax.dot_general(a, b, (((0,), (0,)), ((), ())), preferred_element_type=F32)


def _dot_01(m01, x):
    n = x.shape[1]
    hi = x.astype(BF16)
    r1 = x - hi.astype(F32)
    mid = r1.astype(BF16)
    lo = (r1 - mid.astype(F32)).astype(BF16)
    y = _dot(m01, jnp.concatenate([hi, mid, lo], axis=1))
    return y[:, :n] + y[:, n:2 * n] + y[:, 2 * n:]


def _sig(x):
    return jax.nn.sigmoid(x)


def _rows8(x):
    return x.reshape(x.shape[0] // SUBLANES, SUBLANES, x.shape[1]).sum(axis=0)


def _tile(n, cap):
    if n <= cap:
        return n
    t = cap - cap % 16
    while n % t:
        t -= 16
    return t


_GELU_C = math.sqrt(2.0 / math.pi)


def _gelu_and_grad(x):
    x2 = x * x
    t = jnp.tanh(_GELU_C * (x + 0.044715 * x * x2))
    g = 0.5 * x * (1.0 + t)
    dg = 0.5 * (1.0 + t) + 0.5 * x * (1.0 - t * t) * (_GELU_C * (1.0 + 3.0 * 0.044715 * x2))
    return g, dg


def _rms(x):
    return lax.rsqrt(jnp.mean(x * x, axis=-1, keepdims=True) + NORM_EPS)


def _rms_bwd(dh, x, rs, gain):
    xhat = x * rs
    dxhat = dh * gain
    dx = rs * (dxhat - xhat * jnp.mean(dxhat * xhat, axis=-1, keepdims=True))
    return dx, _rows8(dh * xhat)


def _inproj_fwd(x2, gain, w_st, layer):
    T, D = x2.shape
    tm = _tile(T, 1024)

    def body(x_ref, g_ref, w_ref, o_ref, h_ref):
        @pl.when(pl.program_id(1) == 0)
        def _():
            x = x_ref[...]
            h_ref[...] = (x * _rms(x) * g_ref[...]).astype(BF16)
        o_ref[...] = _dot(h_ref[...], w_ref[...])

    return pl.pallas_call(
        body, name="inproj_fwd",
        grid=(T // tm, N_SEG),
        in_specs=[pl.BlockSpec((tm, D), lambda i, k: (i, 0)),
                  pl.BlockSpec((1, D), lambda i, k: (0, 0)),
                  pl.BlockSpec((None, None, D, D), lambda i, k: (k // 2, layer, 0, k % 2))],
        out_specs=[pl.BlockSpec((tm, D), lambda i, k: (i, k)),
                   pl.BlockSpec((tm, D), lambda i, k: (i, 0))],
        out_shape=[jax.ShapeDtypeStruct((T, N_SEG * D), F32), jax.ShapeDtypeStruct((T, D), BF16)],
        compiler_params=_cp(("parallel", "arbitrary")),
    )(x2, gain, w_st)


def _rg_gates(xc, wr_ref, br, wi_ref, bi, sp):
    D = xc.shape[1]
    xcb = xc.astype(BF16)
    pr, pi = [], []
    for n in range(D // RG_BLOCK):
        blk = xcb[:, n * RG_BLOCK:(n + 1) * RG_BLOCK]
        pr.append(_dot(blk, wr_ref[n]))
        pi.append(_dot(blk, wi_ref[n]))
    r = _sig(jnp.concatenate(pr, axis=1) + br) if len(pr) > 1 else _sig(pr[0] + br)
    i = _sig(jnp.concatenate(pi, axis=1) + bi) if len(pi) > 1 else _sig(pi[0] + bi)
    la = (-RG_C) * r * sp
    a = jnp.exp(la)
    y = 2.0 * la
    one_m_e2 = jnp.where(y > -1e-2, -(y * (1.0 + 0.5 * y * (1.0 + y * (1.0 / 3.0)))), 1.0 - jnp.exp(y))
    mult = jnp.sqrt(jnp.maximum(one_m_e2, 0.0))
    return r, i, a, mult


def _conv_taps(xbuf, cw_ref, ts):
    acc = None
    for j in range(CONV_TAPS):
        term = cw_ref[j:j + 1, :] * xbuf[pl.ds(SUBLANES - (CONV_TAPS - 1) + j, ts), :]
        acc = term if acc is None else acc + term
    return acc


def _rg_fwd(proj, B, cw, cb, wr, br, wi, bi, sp):
    T = proj.shape[0]
    D = proj.shape[1] // N_SEG
    S = T // B
    ts = _tile(S, RG_TILE)
    nts = S // ts
    nb = D // RG_BLOCK

    def body(xa_ref, ga_ref, cw_ref, cb_ref, wr_ref, br_ref, wi_ref, bi_ref, sp_ref,
             ya_ref, h_ref, xbuf, a_scr, u_scr, carry):
        @pl.when(pl.program_id(1) == 0)
        def _():
            xbuf[0:SUBLANES, :] = jnp.zeros((SUBLANES, D), F32)
            carry[...] = jnp.zeros((SUBLANES, D), F32)

        xbuf[pl.ds(SUBLANES, ts), :] = xa_ref[...]
        xc = _conv_taps(xbuf, cw_ref, ts) + cb_ref[...]
        r, i, a, mult = _rg_gates(xc, wr_ref, br_ref[...], wi_ref, bi_ref[...], sp_ref[...])
        a_scr[...] = a
        u_scr[...] = mult * (i * xc)
        row8 = lax.broadcasted_iota(jnp.int32, (SUBLANES, 1), 0)

        def blk(n, hprev):
            off = pl.multiple_of(n * SUBLANES, SUBLANES)
            a8 = a_scr[pl.ds(off, SUBLANES), :]
            u8 = u_scr[pl.ds(off, SUBLANES), :]
            for d in (1, 2, 4):
                m = row8 >= d
                ap = jnp.where(m, pltpu.roll(a8, d, 0), 1.0)
                up = jnp.where(m, pltpu.roll(u8, d, 0), 0.0)
                u8 = a8 * up + u8
                a8 = a8 * ap
            h8 = u8 + a8 * hprev
            u_scr[pl.ds(off, SUBLANES), :] = h8
            last = jnp.sum(jnp.where(row8 == SUBLANES - 1, h8, 0.0), axis=0, keepdims=True)
            return jnp.broadcast_to(last, (SUBLANES, D))

        carry[...] = lax.fori_loop(0, ts // SUBLANES, blk, carry[...])
        h = u_scr[...]
        h_ref[...] = h
        g, _ = _gelu_and_grad(ga_ref[...])
        ya_ref[...] = h * g
        xbuf[0:SUBLANES, :] = xa_ref[pl.ds(ts - SUBLANES, SUBLANES), :]

    vec = pl.BlockSpec((1, D), lambda b, j: (0, 0))
    gate = pl.BlockSpec((nb, RG_BLOCK, RG_BLOCK), lambda b, j: (0, 0, 0))
    return pl.pallas_call(
        body, name="rg_fwd",
        grid=(B, nts),
        in_specs=[pl.BlockSpec((ts, D), lambda b, j: (b * nts + j, 0)),
                  pl.BlockSpec((ts, D), lambda b, j: (b * nts + j, 1)),
                  pl.BlockSpec((CONV_TAPS, D), lambda b, j: (0, 0)), vec, gate, vec, gate, vec, vec],
        out_specs=[pl.BlockSpec((ts, D), lambda b, j: (b * nts + j, 0))] * 2,
        out_shape=[jax.ShapeDtypeStruct((T, D), F32)] * 2,
        scratch_shapes=[pltpu.VMEM((SUBLANES + ts, D), F32), pltpu.VMEM((ts, D), F32),
                        pltpu.VMEM((ts, D), F32), pltpu.VMEM((SUBLANES, D), F32)],
        compiler_params=_cp(("arbitrary", "arbitrary")),
    )(proj, proj, cw, cb, wr, br, wi, bi, sp)


def _hg_gates(q, z, lb):
    sig = _sig(z)
    one_m = 1.0 - lb
    fg = lb + one_m * sig
    lf = jnp.log(jnp.maximum(fg, F_MIN))
    kf = one_m * (1.0 - sig)
    qs = _sig(q)
    return q * qs, qs, kf, lf, fg, sig


def _hg_cum(lf, C):
    ri = lax.broadcasted_iota(jnp.int32, (C, C), 0)
    ci = lax.broadcasted_iota(jnp.int32, (C, C), 1)
    return _dot_01(jnp.where(ci <= ri, 1.0, 0.0).astype(BF16), lf)


def _hg_levels(lf, cum, C):
    row = lax.broadcasted_iota(jnp.int32, (C, 1), 0)
    levels = []
    w = C // 2
    while w >= 4:
        blk = 2 * w
        upper = (row & w) != 0
        ref = jnp.min(jnp.where(upper, 0.0, cum).reshape(C // blk, blk, HEAD), axis=1, keepdims=True)
        ref = jnp.broadcast_to(ref, (C // blk, blk, HEAD)).reshape(C, HEAD)
        d = cum - ref
        e = jnp.exp(jnp.where(upper, d, -d))
        levels.append((jnp.where(upper, e, 0.0), jnp.where(upper, 0.0, e), blk))
        w //= 2
    r4 = row & 3
    lf_prev = pltpu.roll(lf, 1, 0)
    lf_next = pltpu.roll(lf, C - 1, 0)
    eq = jnp.where(r4 >= 2, jnp.exp(jnp.where(r4 == 3, lf + lf_prev, lf)), 0.0)
    ek = jnp.where(r4 == 0, jnp.exp(lf_next), jnp.where(r4 == 1, 1.0, 0.0))
    levels.append((eq, ek, 4))
    odd = (row & 1) == 1
    levels.append((jnp.where(odd, jnp.exp(lf), 0.0), jnp.where(odd, 0.0, 1.0), 2))
    ones = jnp.ones_like(lf)
    levels.append((ones, ones, 1))
    return levels


def _same_block(C, blk):
    ri = lax.broadcasted_iota(jnp.int32, (C, C), 0)
    ci = lax.broadcasted_iota(jnp.int32, (C, C), 1)
    if blk == 1:
        return ri == ci
    shift = blk.bit_length() - 1
    return (ri >> shift) == (ci >> shift)


def _hg_mask_blocks(C):
    blks = []
    w = C // 4
    while w >= 4:
        blks.append(2 * w)
        w //= 2
    return blks + [4, 2, 1]


def _hg_fill_masks(mask_scr, C):
    for i, blk in enumerate(_hg_mask_blocks(C)):
        mask_scr[i] = jnp.where(_same_block(C, blk), 1.0, 0.0).astype(F32)


def _hg_scores(qf, kf, levels, mask_scr):
    A = None
    for n, (eq, ek, _) in enumerate(levels):
        a = _dot_nt((qf * eq).astype(BF16), (kf * ek).astype(BF16))
        if n > 0:
            a = a * mask_scr[n - 1]
        A = a if A is None else A + a
    return A


def _hg_specs(B, NC, D, C, dtype_blocks):
    def spec(col0, rev):
        if rev:
            return pl.BlockSpec((C, HEAD), lambda b, h, j: (b * NC + (NC - 1 - j), col0 + h))
        return pl.BlockSpec((C, HEAD), lambda b, h, j: (b * NC + j, col0 + h))
    return spec


def _hg_fwd(proj, B, lb, gn):
    T = proj.shape[0]
    D = proj.shape[1] // N_SEG
    S = T // B
    C = min(HG_CHUNK, S)
    NC = S // C
    H = D // HEAD
    hpd = D // HEAD
    spec = _hg_specs(B, NC, D, C, None)

    def body(q_ref, z_ref, v_ref, g_ref, lb_ref, gn_ref, yb_ref, o_ref, st_ref, st_scr, mask_scr):
        @pl.when(pl.program_id(2) == 0)
        def _():
            st_scr[...] = jnp.zeros((HEAD, HEAD), F32)
            _hg_fill_masks(mask_scr, C)

        s_t = st_scr[...]
        st_ref[...] = s_t
        qf, _, kf, lf, _, _ = _hg_gates(q_ref[...], z_ref[...], lb_ref[...])
        cum = _hg_cum(lf, C)
        A = _hg_scores(qf, kf, _hg_levels(lf, cum, C), mask_scr)
        vb = v_ref[...].astype(BF16)
        o = _dot_nt((qf * jnp.exp(cum)).astype(BF16), s_t.astype(BF16)) + _dot(A.astype(BF16), vb)
        last = jnp.sum(lf, axis=0, keepdims=True)
        kend = kf * jnp.exp(last - cum)
        st_scr[...] = jnp.exp(last) * s_t + _dot_tn(vb, kend.astype(BF16))
        o_ref[...] = o
        g = g_ref[...]
        yb_ref[...] = (o * _rms(o) * gn_ref[...]) * (g * _sig(g))

    return pl.pallas_call(
        body, name="hg_fwd",
        grid=(B, H, NC),
        in_specs=[spec(2 * hpd, False), spec(3 * hpd, False), spec(4 * hpd, False), spec(5 * hpd, False),
                  pl.BlockSpec((1, HEAD), lambda b, h, j: (0, h)),
                  pl.BlockSpec((1, HEAD), lambda b, h, j: (0, 0))],
        out_specs=[spec(0, False), spec(0, False),
                   pl.BlockSpec((None, None, None, HEAD, HEAD), lambda b, h, j: (b, h, j, 0, 0))],
        out_shape=[jax.ShapeDtypeStruct((T, D), F32), jax.ShapeDtypeStruct((T, D), F32),
                   jax.ShapeDtypeStruct((B, H, NC, HEAD, HEAD), F32)],
        scratch_shapes=[pltpu.VMEM((HEAD, HEAD), F32), pltpu.VMEM((len(_hg_mask_blocks(C)), C, C), F32)],
        compiler_params=_cp(("parallel", "parallel", "arbitrary")),
    )(proj, proj, proj, proj, lb, gn)


def _w_full(ref):
    s, r, c = ref.shape
    return ref[...].reshape(s * r, c)


def _out_fwd(ya, yb, proj, x2, w_st, layer):
    T, D = x2.shape
    tm = _tile(T, 512)

    def body(ya_ref, yb_ref, ma_ref, mb_ref, x_ref, w_ref, xm_ref, y_ref):
        y = (_sig(ma_ref[...]) * ya_ref[...] + _sig(mb_ref[...]) * yb_ref[...]).astype(BF16)
        y_ref[...] = y
        xm_ref[...] = x_ref[...] + _dot(y, _w_full(w_ref))

    row = pl.BlockSpec((tm, D), lambda i: (i, 0))
    return pl.pallas_call(
        body, name="out_fwd",
        grid=(T // tm,),
        in_specs=[row, row, pl.BlockSpec((tm, D), lambda i: (i, 6)), pl.BlockSpec((tm, D), lambda i: (i, 7)), row,
                  pl.BlockSpec((N_SHARD, None, D // N_SHARD, D), lambda i: (0, layer, 0, 0))],
        out_specs=[row, row],
        out_shape=[jax.ShapeDtypeStruct((T, D), F32), jax.ShapeDtypeStruct((T, D), BF16)],
        compiler_params=_cp(("parallel",)),
    )(ya, yb, proj, proj, x2, w_st)


def _mlp_fwd(xm, gain, wup_st, wdn_st, layer):
    T, D = xm.shape
    F4 = wup_st.shape[3]
    tm = _tile(T, 512)

    def body(x_ref, g_ref, wu_ref, wd_ref, xo_ref, up_ref, h_ref):
        @pl.when(pl.program_id(1) == 0)
        def _():
            x = x_ref[...]
            h_ref[...] = (x * _rms(x) * g_ref[...]).astype(BF16)
            xo_ref[...] = x
        up = _dot(h_ref[...], wu_ref[...])
        up_ref[...] = up
        act = jnp.maximum(up, 0.0)
        xo_ref[...] += _dot((act * act).astype(BF16), wd_ref[...])

    row = pl.BlockSpec((tm, D), lambda i, s: (i, 0))
    return pl.pallas_call(
        body, name="mlp_fwd",
        grid=(T // tm, N_SHARD),
        in_specs=[row, pl.BlockSpec((1, D), lambda i, s: (0, 0)),
                  pl.BlockSpec((None, None, D, F4), lambda i, s: (s, layer, 0, 0)),
                  pl.BlockSpec((None, None, F4, D), lambda i, s: (s, layer, 0, 0))],
        out_specs=[row, pl.BlockSpec((tm, F4), lambda i, s: (i, s)), row],
        out_shape=[jax.ShapeDtypeStruct((T, D), F32), jax.ShapeDtypeStruct((T, N_SHARD * F4), F32),
                   jax.ShapeDtypeStruct((T, D), BF16)],
        compiler_params=_cp(("parallel", "arbitrary")),
    )(xm, gain, wup_st, wdn_st)


def _final_loss(x2, gain, tgt):
    T, D = x2.shape
    tm = _tile(T, 512)
    nt = T // tm

    def body(x_ref, g_ref, t_ref, loss_ref, dx_ref, dg_ref):
        x = x_ref[...]
        rs = _rms(x)
        err = x * rs * g_ref[...] - t_ref[...]
        part = 0.5 * jnp.sum(jnp.sum(err * err, axis=-1, keepdims=True) * (1.0 / D), axis=0, keepdims=True)
        loss_ref[...] = jnp.broadcast_to(part, (SUBLANES, 128))
        dx, dg = _rms_bwd(err * (1.0 / D), x, rs, g_ref[...])
        dx_ref[...] = dx
        dg_ref[...] = dg

    row = pl.BlockSpec((tm, D), lambda i: (i, 0))
    return pl.pallas_call(
        body, name="final_loss",
        grid=(nt,),
        in_specs=[row, pl.BlockSpec((1, D), lambda i: (0, 0)), row],
        out_specs=[pl.BlockSpec((None, SUBLANES, 128), lambda i: (i, 0, 0)), row,
                   pl.BlockSpec((None, SUBLANES, D), lambda i: (i, 0, 0))],
        out_shape=[jax.ShapeDtypeStruct((nt, SUBLANES, 128), F32), jax.ShapeDtypeStruct((T, D), F32),
                   jax.ShapeDtypeStruct((nt, SUBLANES, D), F32)],
        compiler_params=_cp(("parallel",)),
    )(x2, gain, tgt)


def _mlp_bwd_x(dx, xm, up, gain, wup_st, wdn_st, layer):
    T, D = xm.shape
    F4 = wup_st.shape[3]
    tm = _tile(T, 512)
    nt = T // tm

    def body(dx_ref, x_ref, up_ref, g_ref, wu_ref, wd_ref, dxm_ref, dup_ref, dg_ref, dxb):
        s = pl.program_id(1)

        @pl.when(s == 0)
        def _():
            dxb[...] = dx_ref[...].astype(BF16)
            dxm_ref[...] = jnp.zeros((tm, D), F32)

        d_act = _dot_nt(dxb[...], wd_ref[...])
        d_up = (d_act * (2.0 * jnp.maximum(up_ref[...], 0.0))).astype(BF16)
        dup_ref[...] = d_up
        dxm_ref[...] += _dot_nt(d_up, wu_ref[...])

        @pl.when(s == N_SHARD - 1)
        def _():
            x = x_ref[...]
            dxn, dg = _rms_bwd(dxm_ref[...], x, _rms(x), g_ref[...])
            dxm_ref[...] = dx_ref[...] + dxn
            dg_ref[...] = dg

    row = pl.BlockSpec((tm, D), lambda i, s: (i, 0))
    return pl.pallas_call(
        body, name="mlp_bwd_x",
        grid=(nt, N_SHARD),
        in_specs=[row, row, pl.BlockSpec((tm, F4), lambda i, s: (i, s)), pl.BlockSpec((1, D), lambda i, s: (0, 0)),
                  pl.BlockSpec((None, None, D, F4), lambda i, s: (s, layer, 0, 0)),
                  pl.BlockSpec((None, None, F4, D), lambda i, s: (s, layer, 0, 0))],
        out_specs=[row, pl.BlockSpec((tm, F4), lambda i, s: (i, s)),
                   pl.BlockSpec((None, SUBLANES, D), lambda i, s: (i, 0, 0))],
        out_shape=[jax.ShapeDtypeStruct((T, D), F32), jax.ShapeDtypeStruct((T, N_SHARD * F4), BF16),
                   jax.ShapeDtypeStruct((nt, SUBLANES, D), F32)],
        scratch_shapes=[pltpu.VMEM((tm, D), BF16)],
        compiler_params=_cp(("parallel", "arbitrary")),
    )(dx, xm, up, gain, wup_st, wdn_st)


def _mlp_bwd_w(up, dx, h, dup):
    T, D = dx.shape
    F4 = up.shape[1] // N_SHARD
    tk = _tile(T, 512)

    def body(up_ref, dx_ref, h_ref, dup_ref, gu_ref, gd_ref):
        @pl.when(pl.program_id(1) == 0)
        def _():
            gu_ref[...] = jnp.zeros((D, F4), F32)
            gd_ref[...] = jnp.zeros((F4, D), F32)
        act = jnp.maximum(up_ref[...], 0.0)
        gd_ref[...] += _dot_tn((act * act).astype(BF16), dx_ref[...].astype(BF16))
        gu_ref[...] += _dot_tn(h_ref[...], dup_ref[...])

    return pl.pallas_call(
        body, name="mlp_bwd_w",
        grid=(N_SHARD, T // tk),
        in_specs=[pl.BlockSpec((tk, F4), lambda s, t: (t, s)), pl.BlockSpec((tk, D), lambda s, t: (t, 0)),
                  pl.BlockSpec((tk, D), lambda s, t: (t, 0)), pl.BlockSpec((tk, F4), lambda s, t: (t, s))],
        out_specs=[pl.BlockSpec((None, D, F4), lambda s, t: (s, 0, 0)),
                   pl.BlockSpec((None, F4, D), lambda s, t: (s, 0, 0))],
        out_shape=[jax.ShapeDtypeStruct((N_SHARD, D, F4), F32), jax.ShapeDtypeStruct((N_SHARD, F4, D), F32)],
        compiler_params=_cp(("parallel", "arbitrary")),
    )(up, dx, h, dup)


def _out_bwd_x(dxm, ya, yb, proj, w_st, layer):
    T, D = dxm.shape
    tm = _tile(T, 512)

    def body(dx_ref, ya_ref, yb_ref, ma_ref, mb_ref, w_ref, dya_ref, dyb_ref, dma_ref, dmb_ref):
        dy = _dot_nt(dx_ref[...].astype(BF16), _w_full(w_ref))
        sa = _sig(ma_ref[...])
        sb = _sig(mb_ref[...])
        dya_ref[...] = dy * sa
        dyb_ref[...] = dy * sb
        dma_ref[...] = (dy * ya_ref[...] * (sa * (1.0 - sa))).astype(BF16)
        dmb_ref[...] = (dy * yb_ref[...] * (sb * (1.0 - sb))).astype(BF16)

    row = pl.BlockSpec((tm, D), lambda i: (i, 0))
    return pl.pallas_call(
        body, name="out_bwd_x",
        grid=(T // tm,),
        in_specs=[row, row, row, pl.BlockSpec((tm, D), lambda i: (i, 6)), pl.BlockSpec((tm, D), lambda i: (i, 7)),
                  pl.BlockSpec((N_SHARD, None, D // N_SHARD, D), lambda i: (0, layer, 0, 0))],
        out_specs=[row] * 4,
        out_shape=[jax.ShapeDtypeStruct((T, D), F32)] * 2 + [jax.ShapeDtypeStruct((T, D), BF16)] * 2,
        compiler_params=_cp(("parallel",)),
    )(dxm, ya, yb, proj, proj, w_st)


def _out_bwd_w(ymix, dxm):
    T, D = dxm.shape
    tk = _tile(T, 512)

    def body(y_ref, dx_ref, g_ref):
        @pl.when(pl.program_id(0) == 0)
        def _():
            g_ref[...] = jnp.zeros((N_SHARD, D // N_SHARD, D), F32)
        g = _dot_tn(y_ref[...], dx_ref[...].astype(BF16))
        g_ref[...] += g.reshape(N_SHARD, D // N_SHARD, D)

    row = pl.BlockSpec((tk, D), lambda t: (t, 0))
    return pl.pallas_call(
        body, name="out_bwd_w",
        grid=(T // tk,),
        in_specs=[row, row],
        out_specs=pl.BlockSpec((N_SHARD, D // N_SHARD, D), lambda t: (0, 0, 0)),
        out_shape=jax.ShapeDtypeStruct((N_SHARD, D // N_SHARD, D), F32),
        compiler_params=_cp(("arbitrary",)),
    )(ymix, dxm)


def _rg_bwd(proj, hrg, dya, B, cw, cb, wr, br, wi, bi, sp):
    T = proj.shape[0]
    D = proj.shape[1] // N_SEG
    S = T // B
    ts = _tile(S, RG_TILE)
    nts = S // ts
    nb = D // RG_BLOCK
    t8 = ts // SUBLANES

    def body(xa_ref, xp_ref, ga_ref, h_ref, hp_ref, dya_ref, cw_ref, cb_ref, wr_ref, br_ref, wi_ref, bi_ref, sp_ref,
             dxa_ref, dga_ref, gwr_ref, gwi_ref, gcw_ref, gcb_ref, gbr_ref, gbi_ref, gsp_ref,
             xbuf, hbuf, abuf, dbuf, g_scr, c_scr, gcar):
        b = pl.program_id(0)
        j = pl.program_id(1)
        first_in_time = j == nts - 1

        @pl.when((b == 0) & (j == 0))
        def _():
            gwr_ref[...] = jnp.zeros((nb, RG_BLOCK, RG_BLOCK), F32)
            gwi_ref[...] = jnp.zeros((nb, RG_BLOCK, RG_BLOCK), F32)
            gcw_ref[...] = jnp.zeros((CONV_TAPS, SUBLANES, D), F32)
            for r in (gcb_ref, gbr_ref, gbi_ref, gsp_ref):
                r[...] = jnp.zeros((SUBLANES, D), F32)

        @pl.when(j == 0)
        def _():
            abuf[pl.ds(ts, SUBLANES), :] = jnp.zeros((SUBLANES, D), F32)
            dbuf[pl.ds(ts, SUBLANES), :] = jnp.zeros((SUBLANES, D), F32)
            gcar[...] = jnp.zeros((SUBLANES, D), F32)

        keep = jnp.where(first_in_time, 0.0, 1.0)
        xbuf[0:SUBLANES, :] = xp_ref[...] * keep
        xbuf[pl.ds(SUBLANES, ts), :] = xa_ref[...]
        hbuf[0:SUBLANES, :] = hp_ref[...] * keep
        hbuf[pl.ds(SUBLANES, ts), :] = h_ref[...]

        xc = _conv_taps(xbuf, cw_ref, ts) + cb_ref[...]
        sp = sp_ref[...]
        r, i, a, mult = _rg_gates(xc, wr_ref, br_ref[...], wi_ref, bi_ref[...], sp)
        g_gate, dg_gate = _gelu_and_grad(ga_ref[...])
        dya = dya_ref[...]
        dga_ref[...] = (dya * h_ref[...] * dg_gate).astype(BF16)

        abuf[0:ts, :] = a
        c_scr[...] = abuf[pl.ds(1, ts), :]
        g_scr[...] = dya * g_gate
        row8 = lax.broadcasted_iota(jnp.int32, (SUBLANES, 1), 0)

        def blk(n, gnext):
            off = pl.multiple_of((t8 - 1 - n) * SUBLANES, SUBLANES)
            c8 = c_scr[pl.ds(off, SUBLANES), :]
            d8 = g_scr[pl.ds(off, SUBLANES), :]
            for d in (1, 2, 4):
                m = row8 < SUBLANES - d
                cn = jnp.where(m, pltpu.roll(c8, SUBLANES - d, 0), 1.0)
                dn = jnp.where(m, pltpu.roll(d8, SUBLANES - d, 0), 0.0)
                d8 = d8 + c8 * dn
                c8 = c8 * cn
            g8 = d8 + c8 * gnext
            g_scr[pl.ds(off, SUBLANES), :] = g8
            first = jnp.sum(jnp.where(row8 == 0, g8, 0.0), axis=0, keepdims=True)
            return jnp.broadcast_to(first, (SUBLANES, D))

        gcar[...] = lax.fori_loop(0, t8, blk, gcar[...])
        abuf[pl.ds(ts, SUBLANES), :] = a[0:SUBLANES, :]

        g = g_scr[...]
        hprev = hbuf[pl.ds(SUBLANES - 1, ts), :]
        gx = i * xc
        e2 = a * a
        dla = g * hprev * a - jnp.where(mult > 0.0, g * gx * e2 / jnp.where(mult > 0.0, mult, 1.0), 0.0)
        dgx = g * mult
        dpr = (dla * ((-RG_C) * sp)) * (r * (1.0 - r))
        dpi = (dgx * xc) * (i * (1.0 - i))
        gsp_ref[...] += _rows8(dla * ((-RG_C) * r))
        gbr_ref[...] += _rows8(dpr)
        gbi_ref[...] += _rows8(dpi)
        dprb = dpr.astype(BF16)
        dpib = dpi.astype(BF16)
        xcb = xc.astype(BF16)
        back = []
        for n in range(nb):
            sl = slice(n * RG_BLOCK, (n + 1) * RG_BLOCK)
            back.append(_dot_nt(dprb[:, sl], wr_ref[n]) + _dot_nt(dpib[:, sl], wi_ref[n]))
            gwr_ref[n] += _dot_tn(xcb[:, sl], dprb[:, sl])
            gwi_ref[n] += _dot_tn(xcb[:, sl], dpib[:, sl])
        dxc = dgx * i + (jnp.concatenate(back, axis=1) if nb > 1 else back[0])
        gcb_ref[...] += _rows8(dxc)

        dbuf[0:ts, :] = dxc
        dxa = None
        for jtap in range(CONV_TAPS):
            term = cw_ref[jtap:jtap + 1, :] * dbuf[pl.ds(CONV_TAPS - 1 - jtap, ts), :]
            dxa = term if dxa is None else dxa + term
            gcw_ref[jtap] += _rows8(dxc * xbuf[pl.ds(SUBLANES - (CONV_TAPS - 1) + jtap, ts), :])
        dxa_ref[...] = dxa.astype(BF16)
        dbuf[pl.ds(ts, SUBLANES), :] = dxc[0:SUBLANES, :]

    def tile_map(col):
        return lambda b, j: (b * nts + (nts - 1 - j), col)

    def prev8_map(col):
        return lambda b, j: (jnp.maximum((b * nts + (nts - 1 - j)) * t8 - 1, 0), col)

    vec = pl.BlockSpec((1, D), lambda b, j: (0, 0))
    gate = pl.BlockSpec((nb, RG_BLOCK, RG_BLOCK), lambda b, j: (0, 0, 0))
    acc8 = pl.BlockSpec((SUBLANES, D), lambda b, j: (0, 0))
    return pl.pallas_call(
        body, name="rg_bwd",
        grid=(B, nts),
        in_specs=[pl.BlockSpec((ts, D), tile_map(0)), pl.BlockSpec((SUBLANES, D), prev8_map(0)),
                  pl.BlockSpec((ts, D), tile_map(1)),
                  pl.BlockSpec((ts, D), tile_map(0)), pl.BlockSpec((SUBLANES, D), prev8_map(0)),
                  pl.BlockSpec((ts, D), tile_map(0)),
                  pl.BlockSpec((CONV_TAPS, D), lambda b, j: (0, 0)), vec, gate, vec, gate, vec, vec],
        out_specs=[pl.BlockSpec((ts, D), tile_map(0)), pl.BlockSpec((ts, D), tile_map(0)), gate, gate,
                   pl.BlockSpec((CONV_TAPS, SUBLANES, D), lambda b, j: (0, 0, 0)), acc8, acc8, acc8, acc8],
        out_shape=[jax.ShapeDtypeStruct((T, D), BF16)] * 2
        + [jax.ShapeDtypeStruct((nb, RG_BLOCK, RG_BLOCK), F32)] * 2
        + [jax.ShapeDtypeStruct((CONV_TAPS, SUBLANES, D), F32)] + [jax.ShapeDtypeStruct((SUBLANES, D), F32)] * 4,
        scratch_shapes=[pltpu.VMEM((SUBLANES + ts, D), F32), pltpu.VMEM((SUBLANES + ts, D), F32),
                        pltpu.VMEM((ts + SUBLANES, D), F32), pltpu.VMEM((ts + SUBLANES, D), F32),
                        pltpu.VMEM((ts, D), F32), pltpu.VMEM((ts, D), F32), pltpu.VMEM((SUBLANES, D), F32)],
        compiler_params=_cp(("arbitrary", "arbitrary")),
    )(proj, proj, proj, hrg, hrg, dya, cw, cb, wr, br, wi, bi, sp)


def _hg_bwd(proj, o_sv, dyb, states, B, lb, gn):
    T = proj.shape[0]
    D = proj.shape[1] // N_SEG
    S = T // B
    C = min(HG_CHUNK, S)
    NC = S // C
    H = D // HEAD
    hpd = D // HEAD
    spec = _hg_specs(B, NC, D, C, None)

    def body(q_ref, z_ref, v_ref, g_ref, o_ref, dyb_ref, st_ref, lb_ref, gn_ref,
             dq_ref, dz_ref, dv_ref, dg_ref, glb_ref, ggn_ref, ds_scr, mask_scr):
        @pl.when(pl.program_id(2) == 0)
        def _():
            ds_scr[...] = jnp.zeros((HEAD, HEAD), F32)
            _hg_fill_masks(mask_scr, C)
            glb_ref[...] = jnp.zeros((SUBLANES, HEAD), F32)
            ggn_ref[...] = jnp.zeros((SUBLANES, HEAD), F32)

        q = q_ref[...]
        lb = lb_ref[...]
        gn = gn_ref[...]
        qf, qs, kf, lf, fg, sig = _hg_gates(q, z_ref[...], lb)
        cum = _hg_cum(lf, C)
        levels = _hg_levels(lf, cum, C)

        o = o_ref[...]
        g = g_ref[...]
        gs = _sig(g)
        rs = _rms(o)
        dyb = dyb_ref[...]
        don = dyb * (g * gs)
        dg_ref[...] = (dyb * (o * rs * gn) * (gs * (1.0 + g * (1.0 - gs)))).astype(BF16)
        ggn_ref[...] += _rows8(don * o * rs)
        dn = don * gn
        do = rs * (dn - o * (rs * rs) * jnp.mean(dn * o, axis=-1, keepdims=True))

        s_t = st_ref[...].astype(BF16)
        ds_t = ds_scr[...]
        ds_b = ds_t.astype(BF16)
        dob = do.astype(BF16)
        vb = v_ref[...].astype(BF16)
        ecum = jnp.exp(cum)
        last = jnp.sum(lf, axis=0, keepdims=True)
        eend = jnp.exp(last - cum)
        qhat = (qf * ecum).astype(BF16)
        kend = (kf * eend).astype(BF16)

        dA = _dot_nt(dob, vb)
        dq_inter = _dot(dob, s_t)
        dk_state = _dot(vb, ds_b)
        dqf = dq_inter * ecum
        dkf = dk_state * eend
        A = None
        g_intra = None
        for n, (eq, ek, _) in enumerate(levels):
            qw = (qf * eq).astype(BF16)
            kw = (kf * ek).astype(BF16)
            a = _dot_nt(qw, kw)
            if n > 0:
                mask = mask_scr[n - 1]
                a = a * mask
                dam = (dA * mask).astype(BF16)
            else:
                dam = dA.astype(BF16)
            A = a if A is None else A + a
            rq = _dot(dam, kw)
            rk = _dot_tn(dam, qw)
            dqf += rq * eq
            dkf += rk * ek
            gi = qw.astype(F32) * rq - kw.astype(F32) * rk
            g_intra = gi if g_intra is None else g_intra + gi
        dv_ref[...] = (_dot_tn(A.astype(BF16), dob) + _dot_nt(kend, ds_b)).astype(BF16)
        e_last = jnp.exp(last)
        ds_scr[...] = e_last * ds_t + _dot_tn(dob, qhat)

        ri = lax.broadcasted_iota(jnp.int32, (C, C), 0)
        ci = lax.broadcasted_iota(jnp.int32, (C, C), 1)
        y_state = kend.astype(F32) * dk_state
        dlf = (_dot_01(jnp.where(ci >= ri, 1.0, 0.0).astype(BF16), g_intra + qhat.astype(F32) * dq_inter - y_state)
               + jnp.sum(y_state, axis=0, keepdims=True)
               + jnp.sum(e_last * st_ref[...] * ds_t, axis=0, keepdims=True))
        dfg = jnp.where(fg > F_MIN, dlf / jnp.maximum(fg, F_MIN), 0.0)
        sneg = 1.0 - sig
        diff = dfg - dkf
        dz_ref[...] = ((1.0 - lb) * sig * sneg * diff).astype(BF16)
        glb_ref[...] += _rows8(sneg * diff)
        dq_ref[...] = (dqf * (qs * (1.0 + q * (1.0 - qs)))).astype(BF16)

    return pl.pallas_call(
        body, name="hg_bwd",
        grid=(B, H, NC),
        in_specs=[spec(2 * hpd, True), spec(3 * hpd, True), spec(4 * hpd, True), spec(5 * hpd, True),
                  spec(0, True), spec(0, True),
                  pl.BlockSpec((None, None, None, HEAD, HEAD), lambda b, h, j: (b, h, NC - 1 - j, 0, 0)),
                  pl.BlockSpec((1, HEAD), lambda b, h, j: (0, h)),
                  pl.BlockSpec((1, HEAD), lambda b, h, j: (0, 0))],
        out_specs=[spec(0, True)] * 4
        + [pl.BlockSpec((None, SUBLANES, HEAD), lambda b, h, j: (b, 0, h)),
           pl.BlockSpec((None, None, SUBLANES, HEAD), lambda b, h, j: (b, h, 0, 0))],
        out_shape=[jax.ShapeDtypeStruct((T, D), BF16)] * 4
        + [jax.ShapeDtypeStruct((B, SUBLANES, D), F32), jax.ShapeDtypeStruct((B, H, SUBLANES, HEAD), F32)],
        scratch_shapes=[pltpu.VMEM((HEAD, HEAD), F32), pltpu.VMEM((len(_hg_mask_blocks(C)), C, C), F32)],
        compiler_params=_cp(("parallel", "parallel", "arbitrary")),
    )(proj, proj, proj, proj, o_sv, dyb, states, lb, gn)


def _inproj_bwd_x(dsegs, w_st, layer, x2, gain, dxm):
    T, D = x2.shape
    tm = _tile(T, 512)
    nt = T // tm

    def body(*refs):
        seg_refs = refs[:N_SEG]
        w_ref, x_ref, g_ref, dxm_ref, dx_ref, dg_ref = refs[N_SEG:]
        k = pl.program_id(1)

        @pl.when(k == 0)
        def _():
            dx_ref[...] = jnp.zeros((tm, D), F32)

        for kk in range(N_SEG):
            @pl.when(k == kk)
            def _(kk=kk):
                dx_ref[...] += _dot_nt(seg_refs[kk][...], w_ref[...])

        @pl.when(k == N_SEG - 1)
        def _():
            x = x_ref[...]
            dxn, dg = _rms_bwd(dx_ref[...], x, _rms(x), g_ref[...])
            dx_ref[...] = dxm_ref[...] + dxn
            dg_ref[...] = dg

    row = pl.BlockSpec((tm, D), lambda i, k: (i, 0))
    return pl.pallas_call(
        body, name="inproj_bwd_x",
        grid=(nt, N_SEG),
        in_specs=[row] * N_SEG
        + [pl.BlockSpec((None, None, D, D), lambda i, k: (k // 2, layer, 0, k % 2)), row,
           pl.BlockSpec((1, D), lambda i, k: (0, 0)), row],
        out_specs=[row, pl.BlockSpec((None, SUBLANES, D), lambda i, k: (i, 0, 0))],
        out_shape=[jax.ShapeDtypeStruct((T, D), F32), jax.ShapeDtypeStruct((nt, SUBLANES, D), F32)],
        compiler_params=_cp(("parallel", "arbitrary")),
    )(*dsegs, w_st, x2, gain, dxm)


def _inproj_bwd_w(h, dsegs):
    T, D = h.shape
    tk = _tile(T, 512)

    def body(*refs):
        h_ref = refs[0]
        seg_refs = refs[1:1 + N_SEG]
        g_ref = refs[1 + N_SEG]
        k = pl.program_id(0)

        @pl.when(pl.program_id(1) == 0)
        def _():
            g_ref[...] = jnp.zeros((D, D), F32)

        for kk in range(N_SEG):
            @pl.when(k == kk)
            def _(kk=kk):
                g_ref[...] += _dot_tn(h_ref[...], seg_refs[kk][...])

    def seg_spec(kk):
        return pl.BlockSpec((tk, D), lambda k, t: (jnp.where(k == kk, t, 0), 0))

    return pl.pallas_call(
        body, name="inproj_bwd_w",
        grid=(N_SEG, T // tk),
        in_specs=[pl.BlockSpec((tk, D), lambda k, t: (t, 0))] + [seg_spec(kk) for kk in range(N_SEG)],
        out_specs=pl.BlockSpec((None, D, D), lambda k, t: (k // 2, 0, k % 2)),
        out_shape=jax.ShapeDtypeStruct((N_SHARD, D, 2 * D), F32),
        compiler_params=_cp(("parallel", "arbitrary")),
    )(h, *dsegs)


def _softmax_rows(lg_ref, L):
    rows = [lg_ref[l:l + 1, :] for l in range(L)]
    mx = functools.reduce(jnp.maximum, rows)
    es = [jnp.exp(r - mx) for r in rows]
    den = functools.reduce(lambda p, q: p + q, es)
    return [e / den for e in es]


def _prep(lb_logits, lam):
    L, D = lb_logits.shape

    def body(lg_ref, lam_ref, lowb_ref, sp_ref):
        sm = _softmax_rows(lg_ref, L)
        run = sm[0]
        for l in range(L):
            if l > 0:
                run = run + sm[l]
            lowb_ref[l:l + 1, :] = jnp.clip(run - sm[0], 0.0, 1.0)
        y = -lam_ref[...]
        sp_ref[...] = jnp.maximum(y, 0.0) + jnp.log1p(jnp.exp(-jnp.abs(y)))

    return pl.pallas_call(
        body, name="prep_small",
        out_shape=[jax.ShapeDtypeStruct((L, D), F32)] * 2,
    )(lb_logits, lam)


def _local_step(x, tgt, lowb, sp, norm_mix, w_in_st, conv_w, conv_b, w_r, b_r, w_i, b_i, hg_norm,
                w_out_st, norm_mlp, w_up_st, w_down_st, norm_final):
    B, S, D = x.shape
    L = norm_mix.shape[0]
    T = B * S
    x2 = x.reshape(T, D)
    row = lambda a, l: a[l:l + 1]
    saved = []
    for l in range(L):
        proj, h = _inproj_fwd(x2, row(norm_mix, l), w_in_st, l)
        ya, hrg = _rg_fwd(proj, B, conv_w[l], row(conv_b, l), w_r[l], row(b_r, l), w_i[l], row(b_i, l), row(sp, l))
        yb, o, st = _hg_fwd(proj, B, row(lowb, l), row(hg_norm, l))
        xm, ymix = _out_fwd(ya, yb, proj, x2, w_out_st, l)
        xo, up, h2 = _mlp_fwd(xm, row(norm_mlp, l), w_up_st, w_down_st, l)
        saved.append((x2, proj, h, ya, hrg, yb, o, st, xm, ymix, up, h2))
        x2 = xo
    loss_parts, dx, g_nf = _final_loss(x2, norm_final[None, :], tgt.reshape(T, D))

    big = []
    small = []
    for l in reversed(range(L)):
        x_in, proj, h, ya, hrg, yb, o, st, xm, ymix, up, h2 = saved[l]
        dxm, dup, g_nmlp = _mlp_bwd_x(dx, xm, up, row(norm_mlp, l), w_up_st, w_down_st, l)
        g_up, g_down = _mlp_bwd_w(up, dx, h2, dup)
        dya, dyb, dma, dmb = _out_bwd_x(dxm, ya, yb, proj, w_out_st, l)
        g_out = _out_bwd_w(ymix, dxm)
        dxa, dga, g_wr, g_wi, g_cw, g_cb, g_br, g_bi, g_sp = _rg_bwd(
            proj, hrg, dya, B, conv_w[l], row(conv_b, l), w_r[l], row(b_r, l), w_i[l], row(b_i, l), row(sp, l))
        dq, dz, dv, dg, g_lb, g_gn = _hg_bwd(proj, o, dyb, st, B, row(lowb, l), row(hg_norm, l))
        dsegs = (dxa, dga, dq, dz, dv, dg, dma, dmb)
        dx, g_nmix = _inproj_bwd_x(dsegs, w_in_st, l, x_in, row(norm_mix, l), dxm)
        g_in = _inproj_bwd_w(h, dsegs)
        big.append((g_in, g_out, g_up, g_down, g_wr, g_wi))
        small.append((g_lb, g_nmix, g_cb, g_br, g_bi, g_sp, g_nmlp, g_gn, g_cw))
    big.reverse()
    small.reverse()
    return loss_parts, dx.reshape(B, S, D), big, small, g_nf


def _me():
    return lax.axis_index("x"), lax.axis_index("y"), lax.axis_index("c")


def _cast_place(w, slot):
    R, N = w.shape
    tr = _tile(R, max(16, (1 << 20) // N))

    def body(slot_ref, w_ref, o_ref):
        o_ref[...] = w_ref[...].astype(BF16)

    return pl.pallas_call(
        body, name="cast_place",
        grid_spec=pltpu.PrefetchScalarGridSpec(
            num_scalar_prefetch=1, grid=(R // tr,),
            in_specs=[pl.BlockSpec((tr, N), lambda i, slot: (i, 0))],
            out_specs=pl.BlockSpec((None, tr, N), lambda i, slot: (slot[0], i, 0))),
        out_shape=jax.ShapeDtypeStruct((N_SHARD, R, N), BF16),
        compiler_params=_cp(("parallel",)),
    )(slot, w)


def _gather_weights(bufs, first_axes):
    n = len(bufs)

    def body(*refs):
        outs = refs[n:2 * n]
        ssem, rsem = refs[2 * n:]
        x, y, c = _me()
        sib = (x, y, 1 - c)

        def piece(a, fx, fy, half):
            rh = outs[a].shape[1] // 2
            sx = 1 - x if fx else x
            sy = 1 - y if fy else y
            return outs[a].at[2 * sx + sy, pl.ds(half * rh, rh), :]

        def rcopy(a, k, ref, dev):
            return pltpu.make_async_remote_copy(src_ref=ref, dst_ref=ref, send_sem=ssem.at[a, k], recv_sem=rsem.at[a, k],
                                                device_id=dev, device_id_type=MESH_ID)

        sent = []

        def send(a, k, ref, dev):
            sent.append(rcopy(a, k, ref, dev))
            sent[-1].start()

        nbr, flip = [], []
        for a in range(n):
            fx = first_axes[a] == "x"
            f_dev = (1 - x, y, c) if fx else (x, 1 - y, c)
            g_dev = (x, 1 - y, c) if fx else (1 - x, y, c)
            f_flip = (1, 0) if fx else (0, 1)
            g_flip = (0, 1) if fx else (1, 0)
            nbr.append((f_dev, g_dev))
            flip.append((f_flip, g_flip))
            own = piece(a, 0, 0, c)
            send(a, 0, own, f_dev)
            send(a, 1, own, g_dev)
        for a in range(n):
            (f_dev, g_dev), (f_flip, _) = nbr[a], flip[a]
            got = piece(a, *f_flip, c)
            rcopy(a, 0, got, f_dev).wait_recv()
            send(a, 2, got, g_dev)
            send(a, 3, got, sib)
        for a in range(n):
            (_, g_dev), (_, g_flip) = nbr[a], flip[a]
            got = piece(a, *g_flip, c)
            rcopy(a, 1, got, g_dev).wait_recv()
            send(a, 4, got, sib)
        for a in range(n):
            got = piece(a, 1, 1, c)
            rcopy(a, 2, got, nbr[a][1]).wait_recv()
            send(a, 5, got, sib)
        for a in range(n):
            (f_flip, g_flip) = flip[a]
            for k, fl in ((3, f_flip), (4, g_flip), (5, (1, 1))):
                rcopy(a, k, piece(a, *fl, 1 - c), sib).wait_recv()
        for cp in sent:
            cp.wait_send()

    return pl.pallas_call(
        body, name="gather_weights",
        in_specs=[ANY] * n, out_specs=[ANY] * n,
        out_shape=[jax.ShapeDtypeStruct(b.shape, b.dtype) for b in bufs],
        input_output_aliases={a: a for a in range(n)},
        scratch_shapes=[pltpu.SemaphoreType.DMA((n, 6)), pltpu.SemaphoreType.DMA((n, 6))],
        compiler_params=pltpu.CompilerParams(has_side_effects=True),
    )(*bufs)


def _exchange(arrs, axes, name):
    n = len(arrs)

    def body(*refs):
        ins, outs = refs[:n], refs[n:2 * n]
        ssem, rsem = refs[2 * n:]
        x, y, c = _me()
        cps = []
        for a in range(n):
            my = {"x": x, "y": y, "c": c}[axes[a]]
            partner = {"x": (1 - x, y, c), "y": (x, 1 - y, c), "c": (x, y, 1 - c)}[axes[a]]
            cps.append(pltpu.make_async_remote_copy(
                src_ref=ins[a].at[:, 1 - my], dst_ref=outs[a], send_sem=ssem.at[a], recv_sem=rsem.at[a],
                device_id=partner, device_id_type=MESH_ID))
            cps[-1].start()
        for cp in cps:
            cp.wait()

    return pl.pallas_call(
        body, name=name,
        in_specs=[ANY] * n, out_specs=[ANY] * n,
        out_shape=[jax.ShapeDtypeStruct((a.shape[0],) + a.shape[2:], a.dtype) for a in arrs],
        scratch_shapes=[pltpu.SemaphoreType.DMA((n,)), pltpu.SemaphoreType.DMA((n,))],
        compiler_params=pltpu.CompilerParams(has_side_effects=True),
    )(*arrs)


def _add_kept(arr, got, idx, name, with_bf16):
    P, _, R, N = arr.shape
    tr = _tile(R, max(16, (1 << 20) // N))

    def body(idx_ref, a_ref, g_ref, o_ref, *ob_ref):
        s = a_ref[...] + g_ref[...].astype(F32)
        o_ref[...] = s
        if with_bf16:
            ob_ref[0][...] = s.astype(BF16)

    out_blk = pl.BlockSpec((None, tr, N), lambda p, i, idx: (p, i, 0))
    return pl.pallas_call(
        body, name=name,
        grid_spec=pltpu.PrefetchScalarGridSpec(
            num_scalar_prefetch=1, grid=(P, R // tr),
            in_specs=[pl.BlockSpec((None, None, tr, N), lambda p, i, idx: (p, idx[0], i, 0)),
                      pl.BlockSpec((None, tr, N), lambda p, i, idx: (p, i, 0))],
            out_specs=[out_blk] * (2 if with_bf16 else 1)),
        out_shape=[jax.ShapeDtypeStruct((P, R, N), F32)] + ([jax.ShapeDtypeStruct((P, R, N), BF16)] if with_bf16 else []),
        compiler_params=_cp(("parallel", "parallel")),
    )(idx, arr, got)


def _share_halves(halves):
    n = len(halves)

    def body(*refs):
        ins, outs = refs[:n], refs[n:2 * n]
        ssem, rsem = refs[2 * n:]
        x, y, c = _me()
        cps = []
        for a in range(n):
            cps.append(pltpu.make_async_remote_copy(
                src_ref=ins[a], dst_ref=outs[a], send_sem=ssem.at[a], recv_sem=rsem.at[a],
                device_id=(x, y, 1 - c), device_id_type=MESH_ID))
            cps[-1].start()
        for cp in cps:
            cp.wait()

    return pl.pallas_call(
        body, name="share_halves",
        in_specs=[ANY] * n, out_specs=[ANY] * n,
        out_shape=[jax.ShapeDtypeStruct(h.shape, h.dtype) for h in halves],
        scratch_shapes=[pltpu.SemaphoreType.DMA((n,)), pltpu.SemaphoreType.DMA((n,))],
        compiler_params=pltpu.CompilerParams(has_side_effects=True),
    )(*halves)


def _reduce_scatter(grads, first_axes):
    x, y, c = _me()
    idx = lambda v: jnp.reshape(v, (1,)).astype(jnp.int32)
    coord = {"x": idx(x), "y": idx(y), "c": idx(c)}
    n = len(grads)
    second = ["y" if f == "x" else "x" for f in first_axes]
    views = [g.reshape(N_SHARD, 2, g.shape[1] // 2, g.shape[2]) for g in grads]
    got = _exchange(views, "c" * n, "rs_exchange_c")
    summed = [_add_kept(v, r, coord["c"], "rs_add_c", True) for v, r in zip(views, got)]

    def split_view(a, ax):
        _, rh, nn = a.shape
        return a.reshape(1, 2, 2 * rh, nn) if ax == "x" else a.reshape(2, 2, rh, nn)

    got = _exchange([split_view(s[1], f) for s, f in zip(summed, first_axes)], first_axes, "rs_exchange_1")
    summed = [_add_kept(split_view(s[0], f), r, coord[f], "rs_add_1", True)
              for s, r, f in zip(summed, got, first_axes)]
    views32 = [s[0].reshape(1, 2, -1, s[0].shape[-1]) for s in summed]
    views16 = [s[1].reshape(1, 2, -1, s[1].shape[-1]) for s in summed]
    got = _exchange(views16, second, "rs_exchange_2")
    kept = [_add_kept(v, r, coord[g], "rs_add_2", False)[0][0] for v, r, g in zip(views32, got, second)]
    return kept, _share_halves(kept)


def _allgather_small(p):
    R, D = p.shape

    def body(p_ref, o_ref, ssem, rsem):
        x, y, c = _me()
        me = 4 * x + 2 * y + c
        o_ref[me] = p_ref[...]
        cps = []
        for m in range(1, 8):
            mx, my, mc = (m >> 2) & 1, (m >> 1) & 1, m & 1
            peer = (1 - x if mx else x, 1 - y if my else y, 1 - c if mc else c)
            cps.append(pltpu.make_async_remote_copy(
                src_ref=p_ref, dst_ref=o_ref.at[me], send_sem=ssem.at[m - 1], recv_sem=rsem.at[m - 1],
                device_id=peer, device_id_type=MESH_ID))
            cps[-1].start()
        for cp in cps:
            cp.wait()

    return pl.pallas_call(
        body, name="allgather_small",
        in_specs=[pl.BlockSpec(memory_space=pltpu.VMEM)],
        out_specs=pl.BlockSpec(memory_space=pltpu.VMEM),
        out_shape=jax.ShapeDtypeStruct((8, R, D), p.dtype),
        scratch_shapes=[pltpu.SemaphoreType.DMA((7,)), pltpu.SemaphoreType.DMA((7,))],
        compiler_params=pltpu.CompilerParams(has_side_effects=True, vmem_limit_bytes=VMEM_LIMIT),
    )(p)


def _adam_math(w, g, m, v):
    m = ADAM_B1 * m + (1.0 - ADAM_B1) * g
    v = ADAM_B2 * v + (1.0 - ADAM_B2) * (g * g)
    m_hat = m / (1.0 - ADAM_B1 ** ADAM_STEP)
    v_hat = v / (1.0 - ADAM_B2 ** ADAM_STEP)
    delta = -ADAM_LR * (m_hat / (jnp.sqrt(v_hat) + ADAM_EPS) + ADAM_WD * w)
    return delta, m, v


def _adam(w, g_mine, g_sib, m, v, core):
    R, N = w.shape
    rh = R // 2
    tr = _tile(rh, max(16, (1 << 19) // N))
    nt = rh // tr

    def body(c_ref, w_ref, gm_ref, gs_ref, m_ref, v_ref, g_ref, d_ref, nm_ref, nv_ref):
        g = jnp.where(pl.program_id(0) == c_ref[0], gm_ref[...], gs_ref[...])
        d, nm, nv = _adam_math(w_ref[...], g, m_ref[...], v_ref[...])
        g_ref[...] = g
        d_ref[...] = d
        nm_ref[...] = nm
        nv_ref[...] = nv

    blk = pl.BlockSpec((tr, N), lambda h, i, c: (h * nt + i, 0))
    half = pl.BlockSpec((tr, N), lambda h, i, c: (i, 0))
    return pl.pallas_call(
        body, name="adamw",
        grid_spec=pltpu.PrefetchScalarGridSpec(
            num_scalar_prefetch=1, grid=(2, nt),
            in_specs=[blk, half, half, blk, blk], out_specs=[blk] * 4),
        out_shape=[jax.ShapeDtypeStruct((R, N), F32)] * 4,
        compiler_params=_cp(("parallel", "parallel")),
    )(core, w, g_mine, g_sib, m, v)


def _reduce_rows(parts, sizes, rows_out):
    D = parts.shape[1]

    def body(p_ref, o_ref):
        o_ref[...] = jnp.zeros((rows_out, D), F32)
        off = 0
        for i, sz in enumerate(sizes):
            o_ref[i:i + 1, :] = jnp.sum(p_ref[off:off + sz, :], axis=0, keepdims=True)
            off += sz

    return pl.pallas_call(
        body, name="reduce_rows",
        out_shape=jax.ShapeDtypeStruct((rows_out, D), F32),
        compiler_params=pltpu.CompilerParams(vmem_limit_bytes=VMEM_LIMIT),
    )(parts)


def _sum_devices(g8):
    _, R, D = g8.shape

    def body(g_ref, o_ref):
        tot = g_ref[0]
        for k in range(1, 8):
            tot = tot + g_ref[k]
        o_ref[...] = tot

    return pl.pallas_call(
        body, name="sum_devices",
        out_shape=jax.ShapeDtypeStruct((R, D), F32),
        compiler_params=pltpu.CompilerParams(vmem_limit_bytes=VMEM_LIMIT),
    )(g8)


def _small_update(gathered, w, m, v, L):
    _, R, D = gathered.shape

    def body(g8_ref, w_ref, m_ref, v_ref, g_ref, d_ref, nm_ref, nv_ref):
        tot = g8_ref[0]
        for k in range(1, 8):
            tot = tot + g8_ref[k]
        g_ref[...] = tot
        sm = _softmax_rows(w_ref, L)
        run = sm[0]
        dcum = []
        for l in range(L):
            if l > 0:
                run = run + sm[l]
            cum = run - sm[0]
            dcum.append(jnp.where((cum > 0.0) & (cum < 1.0), g_ref[l:l + 1, :], 0.0))
        dsm = [jnp.zeros((1, D), F32)]
        for i in range(1, L):
            dsm.append(functools.reduce(lambda p, q: p + q, dcum[i:]))
        dot = functools.reduce(lambda p, q: p + q, [s * d for s, d in zip(sm, dsm)])
        for l in range(L):
            g_ref[l:l + 1, :] = sm[l] * (dsm[l] - dot)
        lam = w_ref[5 * L:6 * L, :]
        g_ref[5 * L:6 * L, :] = g_ref[5 * L:6 * L, :] * (-_sig(-lam))
        d, nm, nv = _adam_math(w_ref[...], g_ref[...], m_ref[...], v_ref[...])
        d_ref[...] = d
        nm_ref[...] = nm
        nv_ref[...] = nv

    return pl.pallas_call(
        body, name="small_update",
        out_shape=[jax.ShapeDtypeStruct((R, D), F32)] * 4,
        compiler_params=pltpu.CompilerParams(vmem_limit_bytes=VMEM_LIMIT),
    )(gathered, w, m, v)


def kernel(x, lb_logits, norm_mix, w_in, conv_w, conv_b, w_r, b_r, w_i, b_i, lam, hg_norm, w_out, norm_mlp, w_up, w_down, norm_final, loss_target, m_lb_logits, m_norm_mix, m_w_in, m_conv_w, m_conv_b, m_w_r, m_b_r, m_w_i, m_b_i, m_lam, m_hg_norm, m_w_out, m_norm_mlp, m_w_up, m_w_down, m_norm_final, v_lb_logits, v_norm_mix, v_w_in, v_conv_w, v_conv_b, v_w_r, v_b_r, v_w_i, v_b_i, v_lam, v_hg_norm, v_w_out, v_norm_mlp, v_w_up, v_w_down, v_norm_final):
    B, S, D = x.shape
    L = norm_mix.shape[0]
    nb = D // RG_BLOCK
    Dq = D // N_SHARD
    mx, my, mc = _me()
    shard = 2 * mx + my

    big_w = (w_in, w_out, w_up, w_down, w_r, w_i)
    flat2 = lambda a: a.reshape(-1, a.shape[-1])
    slot = jnp.reshape(shard, (1,)).astype(jnp.int32)
    g_in, g_out, g_up, g_down, g_r, g_i = _gather_weights([_cast_place(flat2(w), slot) for w in big_w], LINK_SPLIT)
    w_in_st = g_in.reshape(N_SHARD, L, D, 2 * D)
    w_out_st = g_out.reshape(N_SHARD, L, Dq, D)
    w_up_st = g_up.reshape((N_SHARD,) + w_up.shape)
    w_down_st = g_down.reshape((N_SHARD,) + w_down.shape)
    unshard_gate = lambda g: g.reshape(N_SHARD, L, nb, RG_BLOCK // N_SHARD, RG_BLOCK).transpose(1, 2, 0, 3, 4).reshape(
        L, nb, RG_BLOCK, RG_BLOCK)
    w_r_full, w_i_full = unshard_gate(g_r), unshard_gate(g_i)

    R_LB, R_NMIX, R_CB, R_BR, R_BI, R_LAM, R_NMLP, R_GN, R_CW, R_NF, R_LOSS = (
        0, L, 2 * L, 3 * L, 4 * L, 5 * L, 6 * L, 7 * L, 8 * L, 12 * L, 12 * L + 1)
    n_rows = 12 * L + 2
    rows_pad = n_rows + (-n_rows) % SUBLANES

    def place_cols(a):
        return lax.dynamic_update_slice(jnp.zeros((a.shape[0], D), F32), a, (0, shard * Dq))

    def pack_small(lb_, nmix_, cb_, br_, bi_, lam_, nmlp_, gn_, cw_, nf_):
        gn_pad = jnp.pad(gn_, ((0, 0), (0, D - HEAD)))
        rows = [lb_, nmix_, cb_, br_, bi_, lam_, nmlp_, gn_pad, place_cols(cw_.reshape(L * CONV_TAPS, Dq)),
                nf_[None, :], jnp.zeros((rows_pad - n_rows + 1, D), F32)]
        return jnp.concatenate(rows, axis=0)

    w_small = pack_small(lb_logits, norm_mix, conv_b, b_r, b_i, lam, norm_mlp, hg_norm, conv_w, norm_final)
    m_small = pack_small(m_lb_logits, m_norm_mix, m_conv_b, m_b_r, m_b_i, m_lam, m_norm_mlp, m_hg_norm, m_conv_w,
                         m_norm_final)
    v_small = pack_small(v_lb_logits, v_norm_mix, v_conv_b, v_b_r, v_b_i, v_lam, v_norm_mlp, v_hg_norm, v_conv_w,
                         v_norm_final)
    cw_rows = place_cols(conv_w.reshape(L * CONV_TAPS, Dq)) * jnp.where(mc == 0, 1.0, 0.0)
    conv_w_full = _sum_devices(_allgather_small(cw_rows)).reshape(L, CONV_TAPS, D)

    lowb, sp = _prep(lb_logits, lam)

    loss_parts, grad_x, big, small, g_nf = _local_step(
        x, loss_target, lowb, sp, norm_mix, w_in_st, conv_w_full, conv_b, w_r_full, b_r, w_i_full, b_i, hg_norm,
        w_out_st, norm_mlp, w_up_st, w_down_st, norm_final)

    shard_gate = lambda g: g.reshape(L, nb, N_SHARD, RG_BLOCK // N_SHARD, RG_BLOCK).transpose(2, 0, 1, 3, 4)
    stacked = [jnp.stack([big[l][i] for l in range(L)], axis=1) for i in range(4)]
    stacked += [shard_gate(jnp.stack([big[l][i] for l in range(L)])) for i in (4, 5)]
    mine, sibs = _reduce_scatter([s.reshape(N_SHARD, -1, s.shape[-1]) for s in stacked], LINK_SPLIT)
    core = jnp.reshape(mc, (1,)).astype(jnp.int32)
    outs = {}
    for name, w, m, v, g_mine, g_sib in zip(("w_in", "w_out", "w_up", "w_down", "w_r", "w_i"), big_w,
                                            (m_w_in, m_w_out, m_w_up, m_w_down, m_w_r, m_w_i),
                                            (v_w_in, v_w_out, v_w_up, v_w_down, v_w_r, v_w_i), mine, sibs):
        outs[name] = tuple(t.reshape(w.shape) for t in _adam(flat2(w), g_mine, g_sib, flat2(m), flat2(v), core))

    parts, sizes = [], []

    def add_rows(a):
        a = a.reshape(-1, a.shape[-1])
        if a.shape[1] != D:
            a = jnp.pad(a, ((0, 0), (0, D - a.shape[1])))
        parts.append(a)
        sizes.append(a.shape[0])

    for i in range(8):
        for l in range(L):
            add_rows(small[l][i])
    for l in range(L):
        for j in range(CONV_TAPS):
            add_rows(small[l][8][j])
    add_rows(g_nf)
    loss_rows = loss_parts[:, 0:1, :]
    add_rows(jnp.where(lax.broadcasted_iota(jnp.int32, loss_rows.shape, 2) == 0, loss_rows, 0.0))
    g_small = _reduce_rows(jnp.concatenate(parts, axis=0), sizes, rows_pad)
    g_small, d_small, nm_small, nv_small = _small_update(_allgather_small(g_small), w_small, m_small, v_small, L)

    def unpack(t):
        take_cols = lambda a: lax.dynamic_slice(a, (0, shard * Dq), (a.shape[0], Dq))
        return {"lb_logits": t[R_LB:R_LB + L], "norm_mix": t[R_NMIX:R_NMIX + L], "conv_b": t[R_CB:R_CB + L],
                "b_r": t[R_BR:R_BR + L], "b_i": t[R_BI:R_BI + L], "lam": t[R_LAM:R_LAM + L],
                "norm_mlp": t[R_NMLP:R_NMLP + L], "hg_norm": t[R_GN:R_GN + L, :HEAD],
                "conv_w": take_cols(t[R_CW:R_CW + L * CONV_TAPS]).reshape(L, CONV_TAPS, Dq), "norm_final": t[R_NF]}

    small_out = [unpack(t) for t in (g_small, d_small, nm_small, nv_small)]
    loss = g_small[R_LOSS, 0]
    names = ("lb_logits", "norm_mix", "w_in", "conv_w", "conv_b", "w_r", "b_r", "w_i", "b_i", "lam", "hg_norm",
             "w_out", "norm_mlp", "w_up", "w_down", "norm_final")
    result = [loss, grad_x]
    for kind in range(4):
        for nme in names:
            result.append(outs[nme][kind] if nme in outs else small_out[kind][nme])
    return tuple(result)
```

```python
import functools
import math

import jax
import jax.numpy as jnp
from jax import lax
from jax.experimental import pallas as pl
from jax.experimental.pallas import tpu as pltpu

F32 = jnp.float32
BF16 = jnp.bfloat16

HEAD = 128
RG_BLOCK = 256
CONV_TAPS = 4
RG_C = 8.0
F_MIN = 1e-30
NORM_EPS = 1e-6
N_SEG = 8
N_SHARD = 4
HG_CHUNK = 256
RG_TILE = 256
ADAM_LR, ADAM_B1, ADAM_B2, ADAM_EPS, ADAM_WD, ADAM_STEP = 0.001, 0.9, 0.999, 1e-08, 0.01, 10
V7X_VMEM_BYTES = 64 * 1024 * 1024
VMEM_LIMIT = V7X_VMEM_BYTES - 8 * 1024 * 1024
SUBLANES = 8
LINK_SPLIT = "xxyyyy"
MESH_ID = pl.DeviceIdType.MESH
ANY = pl.BlockSpec(memory_space=pl.ANY)


def _cp(sem):
    return pltpu.CompilerParams(dimension_semantics=sem, vmem_limit_bytes=VMEM_LIMIT)


def _dot(a, b):
    return jnp.dot(a, b, preferred_element_type=F32)


def _dot_nt(a, b):
    return lax.dot_general(a, b, (((1,), (1,)), ((), ())), preferred_element_type=F32)


def _dot_tn(a, b):
    return lax.dot_general(a, b, (((0,), (0,)), ((), ())), preferred_element_type=F32)


def _dot_01(m01, x):
    n = x.shape[1]
    hi = x.astype(BF16)
    r1 = x - hi.astype(F32)
    mid = r1.astype(BF16)
    lo = (r1 - mid.astype(F32)).astype(BF16)
    y = _dot(m01, jnp.concatenate([hi, mid, lo], axis=1))
    return y[:, :n] + y[:, n:2 * n] + y[:, 2 * n:]


def _sig(x):
    return jax.nn.sigmoid(x)


def _rows8(x):
    return x.reshape(x.shape[0] // SUBLANES, SUBLANES, x.shape[1]).sum(axis=0)


def _tile(n, cap):
    if n <= cap:
        return n
    t = cap - cap % 16
    while n % t:
        t -= 16
    return t


_GELU_C = math.sqrt(2.0 / math.pi)


def _gelu_and_grad(x):
    x2 = x * x
    t = jnp.tanh(_GELU_C * (x + 0.044715 * x * x2))
    g = 0.5 * x * (1.0 + t)
    dg = 0.5 * (1.0 + t) + 0.5 * x * (1.0 - t * t) * (_GELU_C * (1.0 + 3.0 * 0.044715 * x2))
    return g, dg


def _rms(x):
    return lax.rsqrt(jnp.mean(x * x, axis=-1, keepdims=True) + NORM_EPS)


def _rms_bwd(dh, x, rs, gain):
    xhat = x * rs
    dxhat = dh * gain
    dx = rs * (dxhat - xhat * jnp.mean(dxhat * xhat, axis=-1, keepdims=True))
    return dx, _rows8(dh * xhat)


def _inproj_fwd(x2, gain, w_st, layer):
    T, D = x2.shape
    tm = _tile(T, 2048)

    def body(x_ref, g_ref, w_ref, o_ref, h_ref):
        @pl.when(pl.program_id(1) == 0)
        def _():
            x = x_ref[...]
            h_ref[...] = (x * _rms(x) * g_ref[...]).astype(BF16)
        o_ref[...] = _dot(h_ref[...], w_ref[...])

    return pl.pallas_call(
        body, name="inproj_fwd",
        grid=(T // tm, N_SEG),
        in_specs=[pl.BlockSpec((tm, D), lambda i, k: (i, 0)),
                  pl.BlockSpec((1, D), lambda i, k: (0, 0)),
                  pl.BlockSpec((None, None, D, D), lambda i, k: (k // 2, layer, 0, k % 2))],
        out_specs=[pl.BlockSpec((tm, D), lambda i, k: (i, k)),
                   pl.BlockSpec((tm, D), lambda i, k: (i, 0))],
        out_shape=[jax.ShapeDtypeStruct((T, N_SEG * D), F32), jax.ShapeDtypeStruct((T, D), BF16)],
        compiler_params=_cp(("parallel", "arbitrary")),
    )(x2, gain, w_st)


def _rg_gates(xc, wr_ref, br, wi_ref, bi, sp):
    D = xc.shape[1]
    xcb = xc.astype(BF16)
    pr, pi = [], []
    for n in range(D // RG_BLOCK):
        blk = xcb[:, n * RG_BLOCK:(n + 1) * RG_BLOCK]
        pr.append(_dot(blk, wr_ref[n]))
        pi.append(_dot(blk, wi_ref[n]))
    r = _sig(jnp.concatenate(pr, axis=1) + br) if len(pr) > 1 else _sig(pr[0] + br)
    i = _sig(jnp.concatenate(pi, axis=1) + bi) if len(pi) > 1 else _sig(pi[0] + bi)
    la = (-RG_C) * r * sp
    a = jnp.exp(la)
    y = 2.0 * la
    one_m_e2 = jnp.where(y > -1e-2, -(y * (1.0 + 0.5 * y * (1.0 + y * (1.0 / 3.0)))), 1.0 - jnp.exp(y))
    mult = jnp.sqrt(jnp.maximum(one_m_e2, 0.0))
    return r, i, a, mult


def _conv_taps(xbuf, cw_ref, ts):
    acc = None
    for j in range(CONV_TAPS):
        term = cw_ref[j:j + 1, :] * xbuf[pl.ds(SUBLANES - (CONV_TAPS - 1) + j, ts), :]
        acc = term if acc is None else acc + term
    return acc


def _rg_fwd(proj, B, cw, cb, wr, br, wi, bi, sp):
    T = proj.shape[0]
    D = proj.shape[1] // N_SEG
    S = T // B
    ts = _tile(S, RG_TILE)
    nts = S // ts
    nb = D // RG_BLOCK

    def body(xa_ref, ga_ref, cw_ref, cb_ref, wr_ref, br_ref, wi_ref, bi_ref, sp_ref,
             ya_ref, h_ref, xbuf, a_scr, u_scr, carry):
        @pl.when(pl.program_id(1) == 0)
        def _():
            xbuf[0:SUBLANES, :] = jnp.zeros((SUBLANES, D), F32)
            carry[...] = jnp.zeros((SUBLANES, D), F32)

        xbuf[pl.ds(SUBLANES, ts), :] = xa_ref[...]
        xc = _conv_taps(xbuf, cw_ref, ts) + cb_ref[...]
        r, i, a, mult = _rg_gates(xc, wr_ref, br_ref[...], wi_ref, bi_ref[...], sp_ref[...])
        a_scr[...] = a
        u_scr[...] = mult * (i * xc)
        row8 = lax.broadcasted_iota(jnp.int32, (SUBLANES, 1), 0)

        def blk(n, hprev):
            off = pl.multiple_of(n * SUBLANES, SUBLANES)
            a8 = a_scr[pl.ds(off, SUBLANES), :]
            u8 = u_scr[pl.ds(off, SUBLANES), :]
            for d in (1, 2, 4):
                m = row8 >= d
                ap = jnp.where(m, pltpu.roll(a8, d, 0), 1.0)
                up = jnp.where(m, pltpu.roll(u8, d, 0), 0.0)
                u8 = a8 * up + u8
                a8 = a8 * ap
            h8 = u8 + a8 * hprev
            u_scr[pl.ds(off, SUBLANES), :] = h8
            last = jnp.sum(jnp.where(row8 == SUBLANES - 1, h8, 0.0), axis=0, keepdims=True)
            return jnp.broadcast_to(last, (SUBLANES, D))

        carry[...] = lax.fori_loop(0, ts // SUBLANES, blk, carry[...])
        h = u_scr[...]
        h_ref[...] = h
        g, _ = _gelu_and_grad(ga_ref[...])
        ya_ref[...] = h * g
        xbuf[0:SUBLANES, :] = xa_ref[pl.ds(ts - SUBLANES, SUBLANES), :]

    vec = pl.BlockSpec((1, D), lambda b, j: (0, 0))
    gate = pl.BlockSpec((nb, RG_BLOCK, RG_BLOCK), lambda b, j: (0, 0, 0))
    return pl.pallas_call(
        body, name="rg_fwd",
        grid=(B, nts),
        in_specs=[pl.BlockSpec((ts, D), lambda b, j: (b * nts + j, 0)),
                  pl.BlockSpec((ts, D), lambda b, j: (b * nts + j, 1)),
                  pl.BlockSpec((CONV_TAPS, D), lambda b, j: (0, 0)), vec, gate, vec, gate, vec, vec],
        out_specs=[pl.BlockSpec((ts, D), lambda b, j: (b * nts + j, 0))] * 2,
        out_shape=[jax.ShapeDtypeStruct((T, D), F32)] * 2,
        scratch_shapes=[pltpu.VMEM((SUBLANES + ts, D), F32), pltpu.VMEM((ts, D), F32),
                        pltpu.VMEM((ts, D), F32), pltpu.VMEM((SUBLANES, D), F32)],
        compiler_params=_cp(("arbitrary", "arbitrary")),
    )(proj, proj, cw, cb, wr, br, wi, bi, sp)


def _hg_gates(q, z, lb):
    sig = _sig(z)
    one_m = 1.0 - lb
    fg = lb + one_m * sig
    lf = jnp.log(jnp.maximum(fg, F_MIN))
    kf = one_m * (1.0 - sig)
    qs = _sig(q)
    return q * qs, qs, kf, lf, fg, sig


def _hg_cum(lf, C):
    ri = lax.broadcasted_iota(jnp.int32, (C, C), 0)
    ci = lax.broadcasted_iota(jnp.int32, (C, C), 1)
    return _dot_01(jnp.where(ci <= ri, 1.0, 0.0).astype(BF16), lf)


def _hg_levels(lf, cum, C):
    row = lax.broadcasted_iota(jnp.int32, (C, 1), 0)
    levels = []
    w = C // 2
    while w >= 4:
        blk = 2 * w
        upper = (row & w) != 0
        ref = jnp.min(jnp.where(upper, 0.0, cum).reshape(C // blk, blk, HEAD), axis=1, keepdims=True)
        ref = jnp.broadcast_to(ref, (C // blk, blk, HEAD)).reshape(C, HEAD)
        d = cum - ref
        e = jnp.exp(jnp.where(upper, d, -d))
        levels.append((jnp.where(upper, e, 0.0), jnp.where(upper, 0.0, e), blk))
        w //= 2
    r4 = row & 3
    lf_prev = pltpu.roll(lf, 1, 0)
    lf_next = pltpu.roll(lf, C - 1, 0)
    eq = jnp.where(r4 >= 2, jnp.exp(jnp.where(r4 == 3, lf + lf_prev, lf)), 0.0)
    ek = jnp.where(r4 == 0, jnp.exp(lf_next), jnp.where(r4 == 1, 1.0, 0.0))
    levels.append((eq, ek, 4))
    odd = (row & 1) == 1
    levels.append((jnp.where(odd, jnp.exp(lf), 0.0), jnp.where(odd, 0.0, 1.0), 2))
    ones = jnp.ones_like(lf)
    levels.append((ones, ones, 1))
    return levels


def _same_block(C, blk):
    ri = lax.broadcasted_iota(jnp.int32, (C, C), 0)
    ci = lax.broadcasted_iota(jnp.int32, (C, C), 1)
    if blk == 1:
        return ri == ci
    shift = blk.bit_length() - 1
    return (ri >> shift) == (ci >> shift)


def _hg_mask_blocks(C):
    blks = []
    w = C // 4
    while w >= 4:
        blks.append(2 * w)
        w //= 2
    return blks + [4, 2, 1]


def _hg_fill_masks(mask_scr, C):
    for i, blk in enumerate(_hg_mask_blocks(C)):
        mask_scr[i] = jnp.where(_same_block(C, blk), 1.0, 0.0).astype(F32)


def _hg_scores(qf, kf, levels, mask_scr):
    A = None
    for n, (eq, ek, _) in enumerate(levels):
        a = _dot_nt((qf * eq).astype(BF16), (kf * ek).astype(BF16))
        if n > 0:
            a = a * mask_scr[n - 1]
        A = a if A is None else A + a
    return A


def _hg_specs(B, NC, D, C, dtype_blocks):
    def spec(col0, rev):
        if rev:
            return pl.BlockSpec((C, HEAD), lambda b, h, j: (b * NC + (NC - 1 - j), col0 + h))
        return pl.BlockSpec((C, HEAD), lambda b, h, j: (b * NC + j, col0 + h))
    return spec


def _hg_fwd(proj, B, lb, gn):
    T = proj.shape[0]
    D = proj.shape[1] // N_SEG
    S = T // B
    C = min(HG_CHUNK, S)
    NC = S // C
    H = D // HEAD
    hpd = D // HEAD
    spec = _hg_specs(B, NC, D, C, None)

    def body(q_ref, z_ref, v_ref, g_ref, lb_ref, gn_ref, yb_ref, o_ref, st_ref, st_scr, mask_scr):
        @pl.when(pl.program_id(2) == 0)
        def _():
            st_scr[...] = jnp.zeros((HEAD, HEAD), F32)
            _hg_fill_masks(mask_scr, C)

        s_t = st_scr[...]
        st_ref[...] = s_t
        qf, _, kf, lf, _, _ = _hg_gates(q_ref[...], z_ref[...], lb_ref[...])
        cum = _hg_cum(lf, C)
        A = _hg_scores(qf, kf, _hg_levels(lf, cum, C), mask_scr)
        vb = v_ref[...].astype(BF16)
        o = _dot_nt((qf * jnp.exp(cum)).astype(BF16), s_t.astype(BF16)) + _dot(A.astype(BF16), vb)
        last = jnp.sum(lf, axis=0, keepdims=True)
        kend = kf * jnp.exp(last - cum)
        st_scr[...] = jnp.exp(last) * s_t + _dot_tn(vb, kend.astype(BF16))
        o_ref[...] = o
        g = g_ref[...]
        yb_ref[...] = (o * _rms(o) * gn_ref[...]) * (g * _sig(g))

    return pl.pallas_call(
        body, name="hg_fwd",
        grid=(B, H, NC),
        in_specs=[spec(2 * hpd, False), spec(3 * hpd, False), spec(4 * hpd, False), spec(5 * hpd, False),
                  pl.BlockSpec((1, HEAD), lambda b, h, j: (0, h)),
                  pl.BlockSpec((1, HEAD), lambda b, h, j: (0, 0))],
        out_specs=[spec(0, False), spec(0, False),
                   pl.BlockSpec((None, None, None, HEAD, HEAD), lambda b, h, j: (b, h, j, 0, 0))],
        out_shape=[jax.ShapeDtypeStruct((T, D), F32), jax.ShapeDtypeStruct((T, D), F32),
                   jax.ShapeDtypeStruct((B, H, NC, HEAD, HEAD), F32)],
        scratch_shapes=[pltpu.VMEM((HEAD, HEAD), F32), pltpu.VMEM((len(_hg_mask_blocks(C)), C, C), F32)],
        compiler_params=_cp(("parallel", "parallel", "arbitrary")),
    )(proj, proj, proj, proj, lb, gn)


def _w_full(ref):
    s, r, c = ref.shape
    return ref[...].reshape(s * r, c)


def _out_fwd(ya, yb, proj, x2, w_st, layer):
    T, D = x2.shape
    tm = _tile(T, 512)

    def body(ya_ref, yb_ref, ma_ref, mb_ref, x_ref, w_ref, xm_ref, y_ref):
        y = (_sig(ma_ref[...]) * ya_ref[...] + _sig(mb_ref[...]) * yb_ref[...]).astype(BF16)
        y_ref[...] = y
        xm_ref[...] = x_ref[...] + _dot(y, _w_full(w_ref))

    row = pl.BlockSpec((tm, D), lambda i: (i, 0))
    return pl.pallas_call(
        body, name="out_fwd",
        grid=(T // tm,),
        in_specs=[row, row, pl.BlockSpec((tm, D), lambda i: (i, 6)), pl.BlockSpec((tm, D), lambda i: (i, 7)), row,
                  pl.BlockSpec((N_SHARD, None, D // N_SHARD, D), lambda i: (0, layer, 0, 0))],
        out_specs=[row, row],
        out_shape=[jax.ShapeDtypeStruct((T, D), F32), jax.ShapeDtypeStruct((T, D), BF16)],
        compiler_params=_cp(("parallel",)),
    )(ya, yb, proj, proj, x2, w_st)


def _mlp_fwd(xm, gain, wup_st, wdn_st, layer):
    T, D = xm.shape
    F4 = wup_st.shape[3]
    tm = _tile(T, 1024)

    def body(x_ref, g_ref, wu_ref, wd_ref, xo_ref, up_ref, h_ref):
        @pl.when(pl.program_id(1) == 0)
        def _():
            x = x_ref[...]
            h_ref[...] = (x * _rms(x) * g_ref[...]).astype(BF16)
            xo_ref[...] = x
        up = _dot(h_ref[...], wu_ref[...])
        up_ref[...] = up.astype(BF16)
        act = jnp.maximum(up, 0.0)
        xo_ref[...] += _dot((act * act).astype(BF16), wd_ref[...])

    row = pl.BlockSpec((tm, D), lambda i, s: (i, 0))
    return pl.pallas_call(
        body, name="mlp_fwd",
        grid=(T // tm, N_SHARD),
        in_specs=[row, pl.BlockSpec((1, D), lambda i, s: (0, 0)),
                  pl.BlockSpec((None, None, D, F4), lambda i, s: (s, layer, 0, 0)),
                  pl.BlockSpec((None, None, F4, D), lambda i, s: (s, layer, 0, 0))],
        out_specs=[row, pl.BlockSpec((tm, F4), lambda i, s: (i, s)), row],
        out_shape=[jax.ShapeDtypeStruct((T, D), F32), jax.ShapeDtypeStruct((T, N_SHARD * F4), BF16),
                   jax.ShapeDtypeStruct((T, D), BF16)],
        compiler_params=_cp(("parallel", "arbitrary")),
    )(xm, gain, wup_st, wdn_st)


def _final_loss(x2, gain, tgt):
    T, D = x2.shape
    tm = _tile(T, 512)
    nt = T // tm

    def body(x_ref, g_ref, t_ref, loss_ref, dx_ref, dg_ref):
        x = x_ref[...]
        rs = _rms(x)
        err = x * rs * g_ref[...] - t_ref[...]
        part = 0.5 * jnp.sum(jnp.sum(err * err, axis=-1, keepdims=True) * (1.0 / D), axis=0, keepdims=True)
        loss_ref[...] = jnp.broadcast_to(part, (SUBLANES, 128))
        dx, dg = _rms_bwd(err * (1.0 / D), x, rs, g_ref[...])
        dx_ref[...] = dx
        dg_ref[...] = dg

    row = pl.BlockSpec((tm, D), lambda i: (i, 0))
    return pl.pallas_call(
        body, name="final_loss",
        grid=(nt,),
        in_specs=[row, pl.BlockSpec((1, D), lambda i: (0, 0)), row],
        out_specs=[pl.BlockSpec((None, SUBLANES, 128), lambda i: (i, 0, 0)), row,
                   pl.BlockSpec((None, SUBLANES, D), lambda i: (i, 0, 0))],
        out_shape=[jax.ShapeDtypeStruct((nt, SUBLANES, 128), F32), jax.ShapeDtypeStruct((T, D), F32),
                   jax.ShapeDtypeStruct((nt, SUBLANES, D), F32)],
        compiler_params=_cp(("parallel",)),
    )(x2, gain, tgt)


def _mlp_bwd_x(dx, xm, up, gain, wup_st, wdn_st, layer):
    T, D = xm.shape
    F4 = wup_st.shape[3]
    tm = _tile(T, 1024)
    nt = T // tm

    def body(dx_ref, x_ref, up_ref, g_ref, wu_ref, wd_ref, dxm_ref, dup_ref, dg_ref, dxb):
        s = pl.program_id(1)

        @pl.when(s == 0)
        def _():
            dxb[...] = dx_ref[...].astype(BF16)
            dxm_ref[...] = jnp.zeros((tm, D), F32)

        d_act = _dot_nt(dxb[...], wd_ref[...])
        d_up = (d_act * (2.0 * jnp.maximum(up_ref[...].astype(F32), 0.0))).astype(BF16)
        dup_ref[...] = d_up
        dxm_ref[...] += _dot_nt(d_up, wu_ref[...])

        @pl.when(s == N_SHARD - 1)
        def _():
            x = x_ref[...]
            dxn, dg = _rms_bwd(dxm_ref[...], x, _rms(x), g_ref[...])
            dxm_ref[...] = dx_ref[...] + dxn
            dg_ref[...] = dg

    row = pl.BlockSpec((tm, D), lambda i, s: (i, 0))
    return pl.pallas_call(
        body, name="mlp_bwd_x",
        grid=(nt, N_SHARD),
        in_specs=[row, row, pl.BlockSpec((tm, F4), lambda i, s: (i, s)), pl.BlockSpec((1, D), lambda i, s: (0, 0)),
                  pl.BlockSpec((None, None, D, F4), lambda i, s: (s, layer, 0, 0)),
                  pl.BlockSpec((None, None, F4, D), lambda i, s: (s, layer, 0, 0))],
        out_specs=[row, pl.BlockSpec((tm, F4), lambda i, s: (i, s)),
                   pl.BlockSpec((None, SUBLANES, D), lambda i, s: (i, 0, 0)), row],
        out_shape=[jax.ShapeDtypeStruct((T, D), F32), jax.ShapeDtypeStruct((T, N_SHARD * F4), BF16),
                   jax.ShapeDtypeStruct((nt, SUBLANES, D), F32), jax.ShapeDtypeStruct((T, D), BF16)],
        compiler_params=_cp(("parallel", "arbitrary")),
    )(dx, xm, up, gain, wup_st, wdn_st)


def _layer_slot(bufs, shapes, n_layers):
    out_shape = [jax.ShapeDtypeStruct((N_SHARD, n_layers) + s, F32) for s in shapes]
    return out_shape, ([] if bufs is None else list(bufs))


def _mlp_bwd_w(up, dxb, h, dup, layer, n_layers, bufs):
    T, D = dxb.shape
    F4 = up.shape[1] // N_SHARD
    tk = _tile(T, 1024)
    out_shape, extra = _layer_slot(bufs, [(D, F4), (F4, D)], n_layers)

    def body(up_ref, dx_ref, h_ref, dup_ref, *rest):
        gu_ref, gd_ref = rest[-2:]

        @pl.when(pl.program_id(1) == 0)
        def _():
            gu_ref[...] = jnp.zeros((D, F4), F32)
            gd_ref[...] = jnp.zeros((F4, D), F32)
        act = jnp.maximum(up_ref[...], 0.0)
        gd_ref[...] += _dot_tn(act * act, dx_ref[...])
        gu_ref[...] += _dot_tn(h_ref[...], dup_ref[...])

    return pl.pallas_call(
        body, name="mlp_bwd_w",
        grid=(N_SHARD, T // tk),
        in_specs=[pl.BlockSpec((tk, F4), lambda s, t: (t, s)), pl.BlockSpec((tk, D), lambda s, t: (t, 0)),
                  pl.BlockSpec((tk, D), lambda s, t: (t, 0)), pl.BlockSpec((tk, F4), lambda s, t: (t, s))]
        + [ANY] * len(extra),
        out_specs=[pl.BlockSpec((None, None, D, F4), lambda s, t: (s, layer, 0, 0)),
                   pl.BlockSpec((None, None, F4, D), lambda s, t: (s, layer, 0, 0))],
        out_shape=out_shape,
        input_output_aliases={4 + i: i for i in range(len(extra))},
        compiler_params=_cp(("parallel", "arbitrary")),
    )(up, dxb, h, dup, *extra)


def _out_bwd_x(dxm, ya, yb, proj, w_st, layer):
    T, D = dxm.shape
    tm = _tile(T, 512)

    def body(dx_ref, ya_ref, yb_ref, ma_ref, mb_ref, w_ref, dya_ref, dyb_ref, dma_ref, dmb_ref):
        dy = _dot_nt(dx_ref[...].astype(BF16), _w_full(w_ref))
        sa = _sig(ma_ref[...])
        sb = _sig(mb_ref[...])
        dya_ref[...] = dy * sa
        dyb_ref[...] = dy * sb
        dma_ref[...] = (dy * ya_ref[...] * (sa * (1.0 - sa))).astype(BF16)
        dmb_ref[...] = (dy * yb_ref[...] * (sb * (1.0 - sb))).astype(BF16)

    row = pl.BlockSpec((tm, D), lambda i: (i, 0))
    return pl.pallas_call(
        body, name="out_bwd_x",
        grid=(T // tm,),
        in_specs=[row, row, row, pl.BlockSpec((tm, D), lambda i: (i, 6)), pl.BlockSpec((tm, D), lambda i: (i, 7)),
                  pl.BlockSpec((N_SHARD, None, D // N_SHARD, D), lambda i: (0, layer, 0, 0))],
        out_specs=[row] * 4,
        out_shape=[jax.ShapeDtypeStruct((T, D), F32)] * 2 + [jax.ShapeDtypeStruct((T, D), BF16)] * 2,
        compiler_params=_cp(("parallel",)),
    )(dxm, ya, yb, proj, proj, w_st)


def _out_bwd_w(ymix, dxm, layer, n_layers, bufs):
    T, D = dxm.shape
    tk = _tile(T, 1024)
    out_shape, extra = _layer_slot(bufs, [(D // N_SHARD, D)], n_layers)

    def body(y_ref, dx_ref, *rest):
        g_ref = rest[-1]

        @pl.when(pl.program_id(0) == 0)
        def _():
            g_ref[...] = jnp.zeros((N_SHARD, D // N_SHARD, D), F32)
        g = _dot_tn(y_ref[...], dx_ref[...].astype(BF16))
        g_ref[...] += g.reshape(N_SHARD, D // N_SHARD, D)

    row = pl.BlockSpec((tk, D), lambda t: (t, 0))
    return pl.pallas_call(
        body, name="out_bwd_w",
        grid=(T // tk,),
        in_specs=[row, row] + [ANY] * len(extra),
        out_specs=[pl.BlockSpec((N_SHARD, None, D // N_SHARD, D), lambda t: (0, layer, 0, 0))],
        out_shape=out_shape,
        input_output_aliases={2 + i: i for i in range(len(extra))},
        compiler_params=_cp(("arbitrary",)),
    )(ymix, dxm, *extra)[0]


def _rg_bwd(proj, hrg, dya, B, cw, cb, wr, br, wi, bi, sp):
    T = proj.shape[0]
    D = proj.shape[1] // N_SEG
    S = T // B
    ts = _tile(S, RG_TILE)
    nts = S // ts
    nb = D // RG_BLOCK
    t8 = ts // SUBLANES

    def body(xa_ref, xp_ref, ga_ref, h_ref, hp_ref, dya_ref, cw_ref, cb_ref, wr_ref, br_ref, wi_ref, bi_ref, sp_ref,
             dxa_ref, dga_ref, gwr_ref, gwi_ref, gcw_ref, gcb_ref, gbr_ref, gbi_ref, gsp_ref,
             xbuf, hbuf, abuf, dbuf, g_scr, c_scr, gcar):
        b = pl.program_id(0)
        j = pl.program_id(1)
        first_in_time = j == nts - 1

        @pl.when((b == 0) & (j == 0))
        def _():
            gwr_ref[...] = jnp.zeros((nb, RG_BLOCK, RG_BLOCK), F32)
            gwi_ref[...] = jnp.zeros((nb, RG_BLOCK, RG_BLOCK), F32)
            gcw_ref[...] = jnp.zeros((CONV_TAPS, SUBLANES, D), F32)
            for r in (gcb_ref, gbr_ref, gbi_ref, gsp_ref):
                r[...] = jnp.zeros((SUBLANES, D), F32)

        @pl.when(j == 0)
        def _():
            abuf[pl.ds(ts, SUBLANES), :] = jnp.zeros((SUBLANES, D), F32)
            dbuf[pl.ds(ts, SUBLANES), :] = jnp.zeros((SUBLANES, D), F32)
            gcar[...] = jnp.zeros((SUBLANES, D), F32)

        keep = jnp.where(first_in_time, 0.0, 1.0)
        xbuf[0:SUBLANES, :] = xp_ref[...] * keep
        xbuf[pl.ds(SUBLANES, ts), :] = xa_ref[...]
        hbuf[0:SUBLANES, :] = hp_ref[...] * keep
        hbuf[pl.ds(SUBLANES, ts), :] = h_ref[...]

        xc = _conv_taps(xbuf, cw_ref, ts) + cb_ref[...]
        sp = sp_ref[...]
        r, i, a, mult = _rg_gates(xc, wr_ref, br_ref[...], wi_ref, bi_ref[...], sp)
        g_gate, dg_gate = _gelu_and_grad(ga_ref[...])
        dya = dya_ref[...]
        dga_ref[...] = (dya * h_ref[...] * dg_gate).astype(BF16)

        abuf[0:ts, :] = a
        c_scr[...] = abuf[pl.ds(1, ts), :]
        g_scr[...] = dya * g_gate
        row8 = lax.broadcasted_iota(jnp.int32, (SUBLANES, 1), 0)

        def blk(n, gnext):
            off = pl.multiple_of((t8 - 1 - n) * SUBLANES, SUBLANES)
            c8 = c_scr[pl.ds(off, SUBLANES), :]
            d8 = g_scr[pl.ds(off, SUBLANES), :]
            for d in (1, 2, 4):
                m = row8 < SUBLANES - d
                cn = jnp.where(m, pltpu.roll(c8, SUBLANES - d, 0), 1.0)
                dn = jnp.where(m, pltpu.roll(d8, SUBLANES - d, 0), 0.0)
                d8 = d8 + c8 * dn
                c8 = c8 * cn
            g8 = d8 + c8 * gnext
            g_scr[pl.ds(off, SUBLANES), :] = g8
            first = jnp.sum(jnp.where(row8 == 0, g8, 0.0), axis=0, keepdims=True)
            return jnp.broadcast_to(first, (SUBLANES, D))

        gcar[...] = lax.fori_loop(0, t8, blk, gcar[...])
        abuf[pl.ds(ts, SUBLANES), :] = a[0:SUBLANES, :]

        g = g_scr[...]
        hprev = hbuf[pl.ds(SUBLANES - 1, ts), :]
        gx = i * xc
        e2 = a * a
        dla = g * hprev * a - jnp.where(mult > 0.0, g * gx * e2 / jnp.where(mult > 0.0, mult, 1.0), 0.0)
        dgx = g * mult
        dpr = (dla * ((-RG_C) * sp)) * (r * (1.0 - r))
        dpi = (dgx * xc) * (i * (1.0 - i))
        gsp_ref[...] += _rows8(dla * ((-RG_C) * r))
        gbr_ref[...] += _rows8(dpr)
        gbi_ref[...] += _rows8(dpi)
        dprb = dpr.astype(BF16)
        dpib = dpi.astype(BF16)
        xcb = xc.astype(BF16)
        back = []
        for n in range(nb):
            sl = slice(n * RG_BLOCK, (n + 1) * RG_BLOCK)
            back.append(_dot_nt(dprb[:, sl], wr_ref[n]) + _dot_nt(dpib[:, sl], wi_ref[n]))
            gwr_ref[n] += _dot_tn(xcb[:, sl], dprb[:, sl])
            gwi_ref[n] += _dot_tn(xcb[:, sl], dpib[:, sl])
        dxc = dgx * i + (jnp.concatenate(back, axis=1) if nb > 1 else back[0])
        gcb_ref[...] += _rows8(dxc)

        dbuf[0:ts, :] = dxc
        dxa = None
        for jtap in range(CONV_TAPS):
            term = cw_ref[jtap:jtap + 1, :] * dbuf[pl.ds(CONV_TAPS - 1 - jtap, ts), :]
            dxa = term if dxa is None else dxa + term
            gcw_ref[jtap] += _rows8(dxc * xbuf[pl.ds(SUBLANES - (CONV_TAPS - 1) + jtap, ts), :])
        dxa_ref[...] = dxa.astype(BF16)
        dbuf[pl.ds(ts, SUBLANES), :] = dxc[0:SUBLANES, :]

    def tile_map(col):
        return lambda b, j: (b * nts + (nts - 1 - j), col)

    def prev8_map(col):
        return lambda b, j: (jnp.maximum((b * nts + (nts - 1 - j)) * t8 - 1, 0), col)

    vec = pl.BlockSpec((1, D), lambda b, j: (0, 0))
    gate = pl.BlockSpec((nb, RG_BLOCK, RG_BLOCK), lambda b, j: (0, 0, 0))
    acc8 = pl.BlockSpec((SUBLANES, D), lambda b, j: (0, 0))
    return pl.pallas_call(
        body, name="rg_bwd",
        grid=(B, nts),
        in_specs=[pl.BlockSpec((ts, D), tile_map(0)), pl.BlockSpec((SUBLANES, D), prev8_map(0)),
                  pl.BlockSpec((ts, D), tile_map(1)),
                  pl.BlockSpec((ts, D), tile_map(0)), pl.BlockSpec((SUBLANES, D), prev8_map(0)),
                  pl.BlockSpec((ts, D), tile_map(0)),
                  pl.BlockSpec((CONV_TAPS, D), lambda b, j: (0, 0)), vec, gate, vec, gate, vec, vec],
        out_specs=[pl.BlockSpec((ts, D), tile_map(0)), pl.BlockSpec((ts, D), tile_map(0)), gate, gate,
                   pl.BlockSpec((CONV_TAPS, SUBLANES, D), lambda b, j: (0, 0, 0)), acc8, acc8, acc8, acc8],
        out_shape=[jax.ShapeDtypeStruct((T, D), BF16)] * 2
        + [jax.ShapeDtypeStruct((nb, RG_BLOCK, RG_BLOCK), F32)] * 2
        + [jax.ShapeDtypeStruct((CONV_TAPS, SUBLANES, D), F32)] + [jax.ShapeDtypeStruct((SUBLANES, D), F32)] * 4,
        scratch_shapes=[pltpu.VMEM((SUBLANES + ts, D), F32), pltpu.VMEM((SUBLANES + ts, D), F32),
                        pltpu.VMEM((ts + SUBLANES, D), F32), pltpu.VMEM((ts + SUBLANES, D), F32),
                        pltpu.VMEM((ts, D), F32), pltpu.VMEM((ts, D), F32), pltpu.VMEM((SUBLANES, D), F32)],
        compiler_params=_cp(("arbitrary", "arbitrary")),
    )(proj, proj, proj, hrg, hrg, dya, cw, cb, wr, br, wi, bi, sp)


def _hg_bwd(proj, o_sv, dyb, states, B, lb, gn):
    T = proj.shape[0]
    D = proj.shape[1] // N_SEG
    S = T // B
    C = min(HG_CHUNK, S)
    NC = S // C
    H = D // HEAD
    hpd = D // HEAD
    spec = _hg_specs(B, NC, D, C, None)

    def body(q_ref, z_ref, v_ref, g_ref, o_ref, dyb_ref, st_ref, lb_ref, gn_ref,
             dq_ref, dz_ref, dv_ref, dg_ref, glb_ref, ggn_ref, ds_scr, mask_scr):
        @pl.when(pl.program_id(2) == 0)
        def _():
            ds_scr[...] = jnp.zeros((HEAD, HEAD), F32)
            _hg_fill_masks(mask_scr, C)
            glb_ref[...] = jnp.zeros((SUBLANES, HEAD), F32)
            ggn_ref[...] = jnp.zeros((SUBLANES, HEAD), F32)

        q = q_ref[...]
        lb = lb_ref[...]
        gn = gn_ref[...]
        qf, qs, kf, lf, fg, sig = _hg_gates(q, z_ref[...], lb)
        cum = _hg_cum(lf, C)
        levels = _hg_levels(lf, cum, C)

        o = o_ref[...]
        g = g_ref[...]
        gs = _sig(g)
        rs = _rms(o)
        dyb = dyb_ref[...]
        don = dyb * (g * gs)
        dg_ref[...] = (dyb * (o * rs * gn) * (gs * (1.0 + g * (1.0 - gs)))).astype(BF16)
        ggn_ref[...] += _rows8(don * o * rs)
        dn = don * gn
        do = rs * (dn - o * (rs * rs) * jnp.mean(dn * o, axis=-1, keepdims=True))

        s_t = st_ref[...].astype(BF16)
        ds_t = ds_scr[...]
        ds_b = ds_t.astype(BF16)
        dob = do.astype(BF16)
        vb = v_ref[...].astype(BF16)
        ecum = jnp.exp(cum)
        last = jnp.sum(lf, axis=0, keepdims=True)
        eend = jnp.exp(last - cum)
        qhat = (qf * ecum).astype(BF16)
        kend = (kf * eend).astype(BF16)

        dA = _dot_nt(dob, vb)
        dq_inter = _dot(dob, s_t)
        dk_state = _dot(vb, ds_b)
        dqf = dq_inter * ecum
        dkf = dk_state * eend
        A = None
        g_intra = None
        for n, (eq, ek, _) in enumerate(levels):
            qw = (qf * eq).astype(BF16)
            kw = (kf * ek).astype(BF16)
            a = _dot_nt(qw, kw)
            if n > 0:
                mask = mask_scr[n - 1]
                a = a * mask
                dam = (dA * mask).astype(BF16)
            else:
                dam = dA.astype(BF16)
            A = a if A is None else A + a
            rq = _dot(dam, kw)
            rk = _dot_tn(dam, qw)
            dqf += rq * eq
            dkf += rk * ek
            gi = qw.astype(F32) * rq - kw.astype(F32) * rk
            g_intra = gi if g_intra is None else g_intra + gi
        dv_ref[...] = (_dot_tn(A.astype(BF16), dob) + _dot_nt(kend, ds_b)).astype(BF16)
        e_last = jnp.exp(last)
        ds_scr[...] = e_last * ds_t + _dot_tn(dob, qhat)

        ri = lax.broadcasted_iota(jnp.int32, (C, C), 0)
        ci = lax.broadcasted_iota(jnp.int32, (C, C), 1)
        y_state = kend.astype(F32) * dk_state
        dlf = (_dot_01(jnp.where(ci >= ri, 1.0, 0.0).astype(BF16), g_intra + qhat.astype(F32) * dq_inter - y_state)
               + jnp.sum(y_state, axis=0, keepdims=True)
               + jnp.sum(e_last * st_ref[...] * ds_t, axis=0, keepdims=True))
        dfg = jnp.where(fg > F_MIN, dlf / jnp.maximum(fg, F_MIN), 0.0)
        sneg = 1.0 - sig
        diff = dfg - dkf
        dz_ref[...] = ((1.0 - lb) * sig * sneg * diff).astype(BF16)
        glb_ref[...] += _rows8(sneg * diff)
        dq_ref[...] = (dqf * (qs * (1.0 + q * (1.0 - qs)))).astype(BF16)

    return pl.pallas_call(
        body, name="hg_bwd",
        grid=(B, H, NC),
        in_specs=[spec(2 * hpd, True), spec(3 * hpd, True), spec(4 * hpd, True), spec(5 * hpd, True),
                  spec(0, True), spec(0, True),
                  pl.BlockSpec((None, None, None, HEAD, HEAD), lambda b, h, j: (b, h, NC - 1 - j, 0, 0)),
                  pl.BlockSpec((1, HEAD), lambda b, h, j: (0, h)),
                  pl.BlockSpec((1, HEAD), lambda b, h, j: (0, 0))],
        out_specs=[spec(0, True)] * 4
        + [pl.BlockSpec((None, SUBLANES, HEAD), lambda b, h, j: (b, 0, h)),
           pl.BlockSpec((None, None, SUBLANES, HEAD), lambda b, h, j: (b, h, 0, 0))],
        out_shape=[jax.ShapeDtypeStruct((T, D), BF16)] * 4
        + [jax.ShapeDtypeStruct((B, SUBLANES, D), F32), jax.ShapeDtypeStruct((B, H, SUBLANES, HEAD), F32)],
        scratch_shapes=[pltpu.VMEM((HEAD, HEAD), F32), pltpu.VMEM((len(_hg_mask_blocks(C)), C, C), F32)],
        compiler_params=_cp(("parallel", "parallel", "arbitrary")),
    )(proj, proj, proj, proj, o_sv, dyb, states, lb, gn)


def _inproj_bwd_x(dsegs, w_st, layer, x2, gain, dxm):
    T, D = x2.shape
    tm = _tile(T, 512)
    nt = T // tm

    def body(*refs):
        seg_refs = refs[:N_SEG]
        w_ref, x_ref, g_ref, dxm_ref, dx_ref, dg_ref = refs[N_SEG:]
        k = pl.program_id(1)

        @pl.when(k == 0)
        def _():
            dx_ref[...] = jnp.zeros((tm, D), F32)

        for kk in range(N_SEG):
            @pl.when(k == kk)
            def _(kk=kk):
                dx_ref[...] += _dot_nt(seg_refs[kk][...], w_ref[...])

        @pl.when(k == N_SEG - 1)
        def _():
            x = x_ref[...]
            dxn, dg = _rms_bwd(dx_ref[...], x, _rms(x), g_ref[...])
            dx_ref[...] = dxm_ref[...] + dxn
            dg_ref[...] = dg

    row = pl.BlockSpec((tm, D), lambda i, k: (i, 0))
    return pl.pallas_call(
        body, name="inproj_bwd_x",
        grid=(nt, N_SEG),
        in_specs=[row] * N_SEG
        + [pl.BlockSpec((None, None, D, D), lambda i, k: (k // 2, layer, 0, k % 2)), row,
           pl.BlockSpec((1, D), lambda i, k: (0, 0)), row],
        out_specs=[row, pl.BlockSpec((None, SUBLANES, D), lambda i, k: (i, 0, 0))],
        out_shape=[jax.ShapeDtypeStruct((T, D), F32), jax.ShapeDtypeStruct((nt, SUBLANES, D), F32)],
        compiler_params=_cp(("parallel", "arbitrary")),
    )(*dsegs, w_st, x2, gain, dxm)


def _inproj_bwd_w(h, dsegs, layer, n_layers, bufs):
    T, D = h.shape
    tk = _tile(T, 1024)
    out_shape, extra = _layer_slot(bufs, [(D, 2 * D)], n_layers)

    def body(*refs):
        h_ref = refs[0]
        seg_refs = refs[1:1 + N_SEG]
        g_ref = refs[-1]
        k = pl.program_id(0)

        @pl.when(pl.program_id(1) == 0)
        def _():
            g_ref[...] = jnp.zeros((D, D), F32)

        for kk in range(N_SEG):
            @pl.when(k == kk)
            def _(kk=kk):
                g_ref[...] += _dot_tn(h_ref[...], seg_refs[kk][...])

    def seg_spec(kk):
        return pl.BlockSpec((tk, D), lambda k, t: (jnp.where(k == kk, t, 0), 0))

    return pl.pallas_call(
        body, name="inproj_bwd_w",
        grid=(N_SEG, T // tk),
        in_specs=[pl.BlockSpec((tk, D), lambda k, t: (t, 0))] + [seg_spec(kk) for kk in range(N_SEG)]
        + [ANY] * len(extra),
        out_specs=[pl.BlockSpec((None, None, D, D), lambda k, t: (k // 2, layer, 0, k % 2))],
        out_shape=out_shape,
        input_output_aliases={1 + N_SEG + i: i for i in range(len(extra))},
        compiler_params=_cp(("parallel", "arbitrary")),
    )(h, *dsegs, *extra)[0]


def _softmax_rows(lg_ref, L):
    rows = [lg_ref[l:l + 1, :] for l in range(L)]
    mx = functools.reduce(jnp.maximum, rows)
    es = [jnp.exp(r - mx) for r in rows]
    den = functools.reduce(lambda p, q: p + q, es)
    return [e / den for e in es]


def _prep(lb_logits, lam):
    L, D = lb_logits.shape

    def body(lg_ref, lam_ref, lowb_ref, sp_ref):
        sm = _softmax_rows(lg_ref, L)
        run = sm[0]
        for l in range(L):
            if l > 0:
                run = run + sm[l]
            lowb_ref[l:l + 1, :] = jnp.clip(run - sm[0], 0.0, 1.0)
        y = -lam_ref[...]
        sp_ref[...] = jnp.maximum(y, 0.0) + jnp.log1p(jnp.exp(-jnp.abs(y)))

    return pl.pallas_call(
        body, name="prep_small",
        out_shape=[jax.ShapeDtypeStruct((L, D), F32)] * 2,
    )(lb_logits, lam)


def _local_step(x, tgt, lowb, sp, norm_mix, w_in_st, conv_w, conv_b, w_r, b_r, w_i, b_i, hg_norm,
                w_out_st, norm_mlp, w_up_st, w_down_st, norm_final):
    B, S, D = x.shape
    L = norm_mix.shape[0]
    T = B * S
    x2 = x.reshape(T, D)
    row = lambda a, l: a[l:l + 1]
    saved = []
    for l in range(L):
        proj, h = _inproj_fwd(x2, row(norm_mix, l), w_in_st, l)
        ya, hrg = _rg_fwd(proj, B, conv_w[l], row(conv_b, l), w_r[l], row(b_r, l), w_i[l], row(b_i, l), row(sp, l))
        yb, o, st = _hg_fwd(proj, B, row(lowb, l), row(hg_norm, l))
        xm, ymix = _out_fwd(ya, yb, proj, x2, w_out_st, l)
        xo, up, h2 = _mlp_fwd(xm, row(norm_mlp, l), w_up_st, w_down_st, l)
        saved.append((x2, proj, h, ya, hrg, yb, o, st, xm, ymix, up, h2))
        x2 = xo
    loss_parts, dx, g_nf = _final_loss(x2, norm_final[None, :], tgt.reshape(T, D))

    gates = []
    small = []
    g_in = g_out = g_mlp = None
    for l in reversed(range(L)):
        x_in, proj, h, ya, hrg, yb, o, st, xm, ymix, up, h2 = saved[l]
        dxm, dup, g_nmlp, dxb = _mlp_bwd_x(dx, xm, up, row(norm_mlp, l), w_up_st, w_down_st, l)
        g_mlp = _mlp_bwd_w(up, dxb, h2, dup, l, L, g_mlp)
        dya, dyb, dma, dmb = _out_bwd_x(dxm, ya, yb, proj, w_out_st, l)
        g_out = _out_bwd_w(ymix, dxm, l, L, None if g_out is None else [g_out])
        dxa, dga, g_wr, g_wi, g_cw, g_cb, g_br, g_bi, g_sp = _rg_bwd(
            proj, hrg, dya, B, conv_w[l], row(conv_b, l), w_r[l], row(b_r, l), w_i[l], row(b_i, l), row(sp, l))
        dq, dz, dv, dg, g_lb, g_gn = _hg_bwd(proj, o, dyb, st, B, row(lowb, l), row(hg_norm, l))
        dsegs = (dxa, dga, dq, dz, dv, dg, dma, dmb)
        dx, g_nmix = _inproj_bwd_x(dsegs, w_in_st, l, x_in, row(norm_mix, l), dxm)
        g_in = _inproj_bwd_w(h, dsegs, l, L, None if g_in is None else [g_in])
        gates.append((g_wr, g_wi))
        small.append((g_lb, g_nmix, g_cb, g_br, g_bi, g_sp, g_nmlp, g_gn, g_cw))
    gates.reverse()
    small.reverse()
    big = (g_in, g_out, g_mlp[0], g_mlp[1], jnp.stack([g[0] for g in gates]), jnp.stack([g[1] for g in gates]))
    return loss_parts, dx.reshape(B, S, D), big, small, g_nf


def _me():
    return lax.axis_index("x"), lax.axis_index("y"), lax.axis_index("c")


def _cast_place(w, slot):
    R, N = w.shape
    tr = _tile(R, max(16, (1 << 20) // N))

    def body(slot_ref, w_ref, o_ref):
        o_ref[...] = w_ref[...].astype(BF16)

    return pl.pallas_call(
        body, name="cast_place",
        grid_spec=pltpu.PrefetchScalarGridSpec(
            num_scalar_prefetch=1, grid=(R // tr,),
            in_specs=[pl.BlockSpec((tr, N), lambda i, slot: (i, 0))],
            out_specs=pl.BlockSpec((None, tr, N), lambda i, slot: (slot[0], i, 0))),
        out_shape=jax.ShapeDtypeStruct((N_SHARD, R, N), BF16),
        compiler_params=_cp(("parallel",)),
    )(slot, w)


def _gather_weights(bufs, first_axes):
    n = len(bufs)

    def body(*refs):
        outs = refs[n:2 * n]
        ssem, rsem = refs[2 * n:]
        x, y, c = _me()
        sib = (x, y, 1 - c)

        def piece(a, fx, fy, half):
            rh = outs[a].shape[1] // 2
            sx = 1 - x if fx else x
            sy = 1 - y if fy else y
            return outs[a].at[2 * sx + sy, pl.ds(half * rh, rh), :]

        def rcopy(a, k, ref, dev):
            return pltpu.make_async_remote_copy(src_ref=ref, dst_ref=ref, send_sem=ssem.at[a, k], recv_sem=rsem.at[a, k],
                                                device_id=dev, device_id_type=MESH_ID)

        sent = []

        def send(a, k, ref, dev):
            sent.append(rcopy(a, k, ref, dev))
            sent[-1].start()

        nbr, flip = [], []
        for a in range(n):
            fx = first_axes[a] == "x"
            f_dev = (1 - x, y, c) if fx else (x, 1 - y, c)
            g_dev = (x, 1 - y, c) if fx else (1 - x, y, c)
            f_flip = (1, 0) if fx else (0, 1)
            g_flip = (0, 1) if fx else (1, 0)
            nbr.append((f_dev, g_dev))
            flip.append((f_flip, g_flip))
            own = piece(a, 0, 0, c)
            send(a, 0, own, f_dev)
            send(a, 1, own, g_dev)
        for a in range(n):
            (f_dev, g_dev), (f_flip, _) = nbr[a], flip[a]
            got = piece(a, *f_flip, c)
            rcopy(a, 0, got, f_dev).wait_recv()
            send(a, 2, got, g_dev)
            send(a, 3, got, sib)
        for a in range(n):
            (_, g_dev), (_, g_flip) = nbr[a], flip[a]
            got = piece(a, *g_flip, c)
            rcopy(a, 1, got, g_dev).wait_recv()
            send(a, 4, got, sib)
        for a in range(n):
            got = piece(a, 1, 1, c)
            rcopy(a, 2, got, nbr[a][1]).wait_recv()
            send(a, 5, got, sib)
        for a in range(n):
            (f_flip, g_flip) = flip[a]
            for k, fl in ((3, f_flip), (4, g_flip), (5, (1, 1))):
                rcopy(a, k, piece(a, *fl, 1 - c), sib).wait_recv()
        for cp in sent:
            cp.wait_send()

    return pl.pallas_call(
        body, name="gather_weights",
        in_specs=[ANY] * n, out_specs=[ANY] * n,
        out_shape=[jax.ShapeDtypeStruct(b.shape, b.dtype) for b in bufs],
        input_output_aliases={a: a for a in range(n)},
        scratch_shapes=[pltpu.SemaphoreType.DMA((n, 6)), pltpu.SemaphoreType.DMA((n, 6))],
        compiler_params=pltpu.CompilerParams(has_side_effects=True),
    )(*bufs)


def _exchange(arrs, axes, name):
    n = len(arrs)

    def body(*refs):
        ins, outs = refs[:n], refs[n:2 * n]
        ssem, rsem = refs[2 * n:]
        x, y, c = _me()
        cps = []
        for a in range(n):
            my = {"x": x, "y": y, "c": c}[axes[a]]
            partner = {"x": (1 - x, y, c), "y": (x, 1 - y, c), "c": (x, y, 1 - c)}[axes[a]]
            cps.append(pltpu.make_async_remote_copy(
                src_ref=ins[a].at[:, 1 - my], dst_ref=outs[a], send_sem=ssem.at[a], recv_sem=rsem.at[a],
                device_id=partner, device_id_type=MESH_ID))
            cps[-1].start()
        for cp in cps:
            cp.wait()

    return pl.pallas_call(
        body, name=name,
        in_specs=[ANY] * n, out_specs=[ANY] * n,
        out_shape=[jax.ShapeDtypeStruct((a.shape[0],) + a.shape[2:], a.dtype) for a in arrs],
        scratch_shapes=[pltpu.SemaphoreType.DMA((n,)), pltpu.SemaphoreType.DMA((n,))],
        compiler_params=pltpu.CompilerParams(has_side_effects=True),
    )(*arrs)


def _add_kept(arr, got, idx, name, with_bf16):
    P, _, R, N = arr.shape
    tr = _tile(R, max(16, (1 << 20) // N))

    def body(idx_ref, a_ref, g_ref, o_ref, *ob_ref):
        s = a_ref[...] + g_ref[...].astype(F32)
        o_ref[...] = s
        if with_bf16:
            ob_ref[0][...] = s.astype(BF16)

    out_blk = pl.BlockSpec((None, tr, N), lambda p, i, idx: (p, i, 0))
    return pl.pallas_call(
        body, name=name,
        grid_spec=pltpu.PrefetchScalarGridSpec(
            num_scalar_prefetch=1, grid=(P, R // tr),
            in_specs=[pl.BlockSpec((None, None, tr, N), lambda p, i, idx: (p, idx[0], i, 0)),
                      pl.BlockSpec((None, tr, N), lambda p, i, idx: (p, i, 0))],
            out_specs=[out_blk] * (2 if with_bf16 else 1)),
        out_shape=[jax.ShapeDtypeStruct((P, R, N), F32)] + ([jax.ShapeDtypeStruct((P, R, N), BF16)] if with_bf16 else []),
        compiler_params=_cp(("parallel", "parallel")),
    )(idx, arr, got)


def _share_halves(halves):
    n = len(halves)

    def body(*refs):
        ins, outs = refs[:n], refs[n:2 * n]
        ssem, rsem = refs[2 * n:]
        x, y, c = _me()
        cps = []
        for a in range(n):
            cps.append(pltpu.make_async_remote_copy(
                src_ref=ins[a], dst_ref=outs[a], send_sem=ssem.at[a], recv_sem=rsem.at[a],
                device_id=(x, y, 1 - c), device_id_type=MESH_ID))
            cps[-1].start()
        for cp in cps:
            cp.wait()

    return pl.pallas_call(
        body, name="share_halves",
        in_specs=[ANY] * n, out_specs=[ANY] * n,
        out_shape=[jax.ShapeDtypeStruct(h.shape, h.dtype) for h in halves],
        scratch_shapes=[pltpu.SemaphoreType.DMA((n,)), pltpu.SemaphoreType.DMA((n,))],
        compiler_params=pltpu.CompilerParams(has_side_effects=True),
    )(*halves)


def _reduce_scatter(grads, first_axes):
    x, y, c = _me()
    idx = lambda v: jnp.reshape(v, (1,)).astype(jnp.int32)
    coord = {"x": idx(x), "y": idx(y), "c": idx(c)}
    n = len(grads)
    second = ["y" if f == "x" else "x" for f in first_axes]
    views = [g.reshape(N_SHARD, 2, g.shape[1] // 2, g.shape[2]) for g in grads]
    got = _exchange(views, "c" * n, "rs_exchange_c")
    summed = [_add_kept(v, r, coord["c"], "rs_add_c", True) for v, r in zip(views, got)]

    def split_view(a, ax):
        _, rh, nn = a.shape
        return a.reshape(1, 2, 2 * rh, nn) if ax == "x" else a.reshape(2, 2, rh, nn)

    got = _exchange([split_view(s[1], f) for s, f in zip(summed, first_axes)], first_axes, "rs_exchange_1")
    summed = [_add_kept(split_view(s[0], f), r, coord[f], "rs_add_1", True)
              for s, r, f in zip(summed, got, first_axes)]
    views32 = [s[0].reshape(1, 2, -1, s[0].shape[-1]) for s in summed]
    views16 = [s[1].reshape(1, 2, -1, s[1].shape[-1]) for s in summed]
    got = _exchange(views16, second, "rs_exchange_2")
    kept = [_add_kept(v, r, coord[g], "rs_add_2", False)[0][0] for v, r, g in zip(views32, got, second)]
    return kept, _share_halves(kept)


def _allgather_small(p):
    R, D = p.shape

    def body(p_ref, o_ref, ssem, rsem):
        x, y, c = _me()
        me = 4 * x + 2 * y + c
        o_ref[me] = p_ref[...]
        cps = []
        for m in range(1, 8):
            mx, my, mc = (m >> 2) & 1, (m >> 1) & 1, m & 1
            peer = (1 - x if mx else x, 1 - y if my else y, 1 - c if mc else c)
            cps.append(pltpu.make_async_remote_copy(
                src_ref=p_ref, dst_ref=o_ref.at[me], send_sem=ssem.at[m - 1], recv_sem=rsem.at[m - 1],
                device_id=peer, device_id_type=MESH_ID))
            cps[-1].start()
        for cp in cps:
            cp.wait()

    return pl.pallas_call(
        body, name="allgather_small",
        in_specs=[pl.BlockSpec(memory_space=pltpu.VMEM)],
        out_specs=pl.BlockSpec(memory_space=pltpu.VMEM),
        out_shape=jax.ShapeDtypeStruct((8, R, D), p.dtype),
        scratch_shapes=[pltpu.SemaphoreType.DMA((7,)), pltpu.SemaphoreType.DMA((7,))],
        compiler_params=pltpu.CompilerParams(has_side_effects=True, vmem_limit_bytes=VMEM_LIMIT),
    )(p)


def _adam_math(w, g, m, v):
    m = ADAM_B1 * m + (1.0 - ADAM_B1) * g
    v = ADAM_B2 * v + (1.0 - ADAM_B2) * (g * g)
    m_hat = m / (1.0 - ADAM_B1 ** ADAM_STEP)
    v_hat = v / (1.0 - ADAM_B2 ** ADAM_STEP)
    delta = -ADAM_LR * (m_hat / (jnp.sqrt(v_hat) + ADAM_EPS) + ADAM_WD * w)
    return delta, m, v


def _adam(w, g_mine, g_sib, m, v, core):
    R, N = w.shape
    rh = R // 2
    tr = _tile(rh, max(16, (1 << 19) // N))
    nt = rh // tr

    def body(c_ref, w_ref, gm_ref, gs_ref, m_ref, v_ref, g_ref, d_ref, nm_ref, nv_ref):
        g = jnp.where(pl.program_id(0) == c_ref[0], gm_ref[...], gs_ref[...])
        d, nm, nv = _adam_math(w_ref[...], g, m_ref[...], v_ref[...])
        g_ref[...] = g
        d_ref[...] = d
        nm_ref[...] = nm
        nv_ref[...] = nv

    blk = pl.BlockSpec((tr, N), lambda h, i, c: (h * nt + i, 0))
    half = pl.BlockSpec((tr, N), lambda h, i, c: (i, 0))
    return pl.pallas_call(
        body, name="adamw",
        grid_spec=pltpu.PrefetchScalarGridSpec(
            num_scalar_prefetch=1, grid=(2, nt),
            in_specs=[blk, half, half, blk, blk], out_specs=[blk] * 4),
        out_shape=[jax.ShapeDtypeStruct((R, N), F32)] * 4,
        compiler_params=_cp(("parallel", "parallel")),
    )(core, w, g_mine, g_sib, m, v)


def _reduce_rows(parts, sizes, rows_out):
    D = parts.shape[1]

    def body(p_ref, o_ref):
        o_ref[...] = jnp.zeros((rows_out, D), F32)
        off = 0
        for i, sz in enumerate(sizes):
            o_ref[i:i + 1, :] = jnp.sum(p_ref[off:off + sz, :], axis=0, keepdims=True)
            off += sz

    return pl.pallas_call(
        body, name="reduce_rows",
        out_shape=jax.ShapeDtypeStruct((rows_out, D), F32),
        compiler_params=pltpu.CompilerParams(vmem_limit_bytes=VMEM_LIMIT),
    )(parts)


def _sum_devices(g8):
    _, R, D = g8.shape

    def body(g_ref, o_ref):
        tot = g_ref[0]
        for k in range(1, 8):
            tot = tot + g_ref[k]
        o_ref[...] = tot

    return pl.pallas_call(
        body, name="sum_devices",
        out_shape=jax.ShapeDtypeStruct((R, D), F32),
        compiler_params=pltpu.CompilerParams(vmem_limit_bytes=VMEM_LIMIT),
    )(g8)


def _small_update(gathered, w, m, v, L):
    _, R, D = gathered.shape

    def body(g8_ref, w_ref, m_ref, v_ref, g_ref, d_ref, nm_ref, nv_ref):
        tot = g8_ref[0]
        for k in range(1, 8):
            tot = tot + g8_ref[k]
        g_ref[...] = tot
        sm = _softmax_rows(w_ref, L)
        run = sm[0]
        dcum = []
        for l in range(L):
            if l > 0:
                run = run + sm[l]
            cum = run - sm[0]
            dcum.append(jnp.where((cum > 0.0) & (cum < 1.0), g_ref[l:l + 1, :], 0.0))
        dsm = [jnp.zeros((1, D), F32)]
        for i in range(1, L):
            dsm.append(functools.reduce(lambda p, q: p + q, dcum[i:]))
        dot = functools.reduce(lambda p, q: p + q, [s * d for s, d in zip(sm, dsm)])
        for l in range(L):
            g_ref[l:l + 1, :] = sm[l] * (dsm[l] - dot)
        lam = w_ref[5 * L:6 * L, :]
        g_ref[5 * L:6 * L, :] = g_ref[5 * L:6 * L, :] * (-_sig(-lam))
        d, nm, nv = _adam_math(w_ref[...], g_ref[...], m_ref[...], v_ref[...])
        d_ref[...] = d
        nm_ref[...] = nm
        nv_ref[...] = nv

    return pl.pallas_call(
        body, name="small_update",
        out_shape=[jax.ShapeDtypeStruct((R, D), F32)] * 4,
        compiler_params=pltpu.CompilerParams(vmem_limit_bytes=VMEM_LIMIT),
    )(gathered, w, m, v)


def kernel(x, lb_logits, norm_mix, w_in, conv_w, conv_b, w_r, b_r, w_i, b_i, lam, hg_norm, w_out, norm_mlp, w_up, w_down, norm_final, loss_target, m_lb_logits, m_norm_mix, m_w_in, m_conv_w, m_conv_b, m_w_r, m_b_r, m_w_i, m_b_i, m_lam, m_hg_norm, m_w_out, m_norm_mlp, m_w_up, m_w_down, m_norm_final, v_lb_logits, v_norm_mix, v_w_in, v_conv_w, v_conv_b, v_w_r, v_b_r, v_w_i, v_b_i, v_lam, v_hg_norm, v_w_out, v_norm_mlp, v_w_up, v_w_down, v_norm_final):
    B, S, D = x.shape
    L = norm_mix.shape[0]
    nb = D // RG_BLOCK
    Dq = D // N_SHARD
    mx, my, mc = _me()
    shard = 2 * mx + my

    big_w = (w_in, w_out, w_up, w_down, w_r, w_i)
    flat2 = lambda a: a.reshape(-1, a.shape[-1])
    slot = jnp.reshape(shard, (1,)).astype(jnp.int32)
    g_in, g_out, g_up, g_down, g_r, g_i = _gather_weights([_cast_place(flat2(w), slot) for w in big_w], LINK_SPLIT)
    w_in_st = g_in.reshape(N_SHARD, L, D, 2 * D)
    w_out_st = g_out.reshape(N_SHARD, L, Dq, D)
    w_up_st = g_up.reshape((N_SHARD,) + w_up.shape)
    w_down_st = g_down.reshape((N_SHARD,) + w_down.shape)
    unshard_gate = lambda g: g.reshape(N_SHARD, L, nb, RG_BLOCK // N_SHARD, RG_BLOCK).transpose(1, 2, 0, 3, 4).reshape(
        L, nb, RG_BLOCK, RG_BLOCK)
    w_r_full, w_i_full = unshard_gate(g_r), unshard_gate(g_i)

    R_LB, R_NMIX, R_CB, R_BR, R_BI, R_LAM, R_NMLP, R_GN, R_CW, R_NF, R_LOSS = (
        0, L, 2 * L, 3 * L, 4 * L, 5 * L, 6 * L, 7 * L, 8 * L, 12 * L, 12 * L + 1)
    n_rows = 12 * L + 2
    rows_pad = n_rows + (-n_rows) % SUBLANES

    def place_cols(a):
        return lax.dynamic_update_slice(jnp.zeros((a.shape[0], D), F32), a, (0, shard * Dq))

    def pack_small(lb_, nmix_, cb_, br_, bi_, lam_, nmlp_, gn_, cw_, nf_):
        gn_pad = jnp.pad(gn_, ((0, 0), (0, D - HEAD)))
        rows = [lb_, nmix_, cb_, br_, bi_, lam_, nmlp_, gn_pad, place_cols(cw_.reshape(L * CONV_TAPS, Dq)),
                nf_[None, :], jnp.zeros((rows_pad - n_rows + 1, D), F32)]
        return jnp.concatenate(rows, axis=0)

    w_small = pack_small(lb_logits, norm_mix, conv_b, b_r, b_i, lam, norm_mlp, hg_norm, conv_w, norm_final)
    m_small = pack_small(m_lb_logits, m_norm_mix, m_conv_b, m_b_r, m_b_i, m_lam, m_norm_mlp, m_hg_norm, m_conv_w,
                         m_norm_final)
    v_small = pack_small(v_lb_logits, v_norm_mix, v_conv_b, v_b_r, v_b_i, v_lam, v_norm_mlp, v_hg_norm, v_conv_w,
                         v_norm_final)
    cw_rows = place_cols(conv_w.reshape(L * CONV_TAPS, Dq)) * jnp.where(mc == 0, 1.0, 0.0)
    conv_w_full = _sum_devices(_allgather_small(cw_rows)).reshape(L, CONV_TAPS, D)

    lowb, sp = _prep(lb_logits, lam)

    loss_parts, grad_x, big, small, g_nf = _local_step(
        x, loss_target, lowb, sp, norm_mix, w_in_st, conv_w_full, conv_b, w_r_full, b_r, w_i_full, b_i, hg_norm,
        w_out_st, norm_mlp, w_up_st, w_down_st, norm_final)

    shard_gate = lambda g: g.reshape(L, nb, N_SHARD, RG_BLOCK // N_SHARD, RG_BLOCK).transpose(2, 0, 1, 3, 4)
    stacked = list(big[:4]) + [shard_gate(big[4]), shard_gate(big[5])]
    mine, sibs = _reduce_scatter([s.reshape(N_SHARD, -1, s.shape[-1]) for s in stacked], LINK_SPLIT)
    core = jnp.reshape(mc, (1,)).astype(jnp.int32)
    outs = {}
    for name, w, m, v, g_mine, g_sib in zip(("w_in", "w_out", "w_up", "w_down", "w_r", "w_i"), big_w,
                                            (m_w_in, m_w_out, m_w_up, m_w_down, m_w_r, m_w_i),
                                            (v_w_in, v_w_out, v_w_up, v_w_down, v_w_r, v_w_i), mine, sibs):
        outs[name] = tuple(t.reshape(w.shape) for t in _adam(flat2(w), g_mine, g_sib, flat2(m), flat2(v), core))

    parts, sizes = [], []

    def add_rows(a):
        a = a.reshape(-1, a.shape[-1])
        if a.shape[1] != D:
            a = jnp.pad(a, ((0, 0), (0, D - a.shape[1])))
        parts.append(a)
        sizes.append(a.shape[0])

    for i in range(8):
        for l in range(L):
            add_rows(small[l][i])
    for l in range(L):
        for j in range(CONV_TAPS):
            add_rows(small[l][8][j])
    add_rows(g_nf)
    loss_rows = loss_parts[:, 0:1, :]
    add_rows(jnp.where(lax.broadcasted_iota(jnp.int32, loss_rows.shape, 2) == 0, loss_rows, 0.0))
    g_small = _reduce_rows(jnp.concatenate(parts, axis=0), sizes, rows_pad)
    g_small, d_small, nm_small, nv_small = _small_update(_allgather_small(g_small), w_small, m_small, v_small, L)

    def unpack(t):
        take_cols = lambda a: lax.dynamic_slice(a, (0, shard * Dq), (a.shape[0], Dq))
        return {"lb_logits": t[R_LB:R_LB + L], "norm_mix": t[R_NMIX:R_NMIX + L], "conv_b": t[R_CB:R_CB + L],
                "b_r": t[R_BR:R_BR + L], "b_i": t[R_BI:R_BI + L], "lam": t[R_LAM:R_LAM + L],
                "norm_mlp": t[R_NMLP:R_NMLP + L], "hg_norm": t[R_GN:R_GN + L, :HEAD],
                "conv_w": take_cols(t[R_CW:R_CW + L * CONV_TAPS]).reshape(L, CONV_TAPS, Dq), "norm_final": t[R_NF]}

    small_out = [unpack(t) for t in (g_small, d_small, nm_small, nv_small)]
    loss = g_small[R_LOSS, 0]
    names = ("lb_logits", "norm_mix", "w_in", "conv_w", "conv_b", "w_r", "b_r", "w_i", "b_i", "lam", "hg_norm",
             "w_out", "norm_mlp", "w_up", "w_down", "norm_final")
    result = [loss, grad_x]
    for kind in range(4):
        for nme in names:
            result.append(outs[nme][kind] if nme in outs else small_out[kind][nme])
    return tuple(result)
```

```python
import functools
import math

import jax
import jax.numpy as jnp
from jax import lax
from jax.experimental import pallas as pl
from jax.experimental.pallas import tpu as pltpu

F32 = jnp.float32
BF16 = jnp.bfloat16

HEAD = 128
RG_BLOCK = 256
CONV_TAPS = 4
RG_C = 8.0
F_MIN = 1e-30
NORM_EPS = 1e-6
N_SEG = 8
N_SHARD = 4
HG_CHUNK = 256
RG_TILE = 256
ADAM_LR, ADAM_B1, ADAM_B2, ADAM_EPS, ADAM_WD, ADAM_STEP = 0.001, 0.9, 0.999, 1e-08, 0.01, 10
V7X_VMEM_BYTES = 64 * 1024 * 1024
VMEM_LIMIT = V7X_VMEM_BYTES - 8 * 1024 * 1024
SUBLANES = 8
LINK_SPLIT = "xxyyyy"
MESH_ID = pl.DeviceIdType.MESH
ANY = pl.BlockSpec(memory_space=pl.ANY)


def _cp(sem):
    return pltpu.CompilerParams(dimension_semantics=sem, vmem_limit_bytes=VMEM_LIMIT)


def _dot(a, b):
    return jnp.dot(a, b, preferred_element_type=F32)


def _dot_nt(a, b):
    return lax.dot_general(a, b, (((1,), (1,)), ((), ())), preferred_element_type=F32)


def _dot_tn(a, b):
    return lax.dot_general(a, b, (((0,), (0,)), ((), ())), preferred_element_type=F32)


def _dot_01(m01, x):
    n = x.shape[1]
    hi = x.astype(BF16)
    r1 = x - hi.astype(F32)
    mid = r1.astype(BF16)
    lo = (r1 - mid.astype(F32)).astype(BF16)
    y = _dot(m01, jnp.concatenate([hi, mid, lo], axis=1))
    return y[:, :n] + y[:, n:2 * n] + y[:, 2 * n:]


def _sig(x):
    return jax.nn.sigmoid(x)


def _rows8(x):
    return x.reshape(x.shape[0] // SUBLANES, SUBLANES, x.shape[1]).sum(axis=0)


def _tile(n, cap):
    if n <= cap:
        return n
    t = cap - cap % 16
    while n % t:
        t -= 16
    return t


_GELU_C = math.sqrt(2.0 / math.pi)


def _gelu_and_grad(x):
    x2 = x * x
    t = jnp.tanh(_GELU_C * (x + 0.044715 * x * x2))
    g = 0.5 * x * (1.0 + t)
    dg = 0.5 * (1.0 + t) + 0.5 * x * (1.0 - t * t) * (_GELU_C * (1.0 + 3.0 * 0.044715 * x2))
    return g, dg


def _rms(x):
    return lax.rsqrt(jnp.mean(x * x, axis=-1, keepdims=True) + NORM_EPS)


def _rms_bwd(dh, x, rs, gain):
    xhat = x * rs
    dxhat = dh * gain
    dx = rs * (dxhat - xhat * jnp.mean(dxhat * xhat, axis=-1, keepdims=True))
    return dx, _rows8(dh * xhat)


def _inproj_fwd(x2, gain, w_st, layer):
    T, D = x2.shape
    tm = _tile(T, 2048)

    def body(x_ref, g_ref, w_ref, o_ref, h_ref):
        @pl.when(pl.program_id(1) == 0)
        def _():
            x = x_ref[...]
            h_ref[...] = (x * _rms(x) * g_ref[...]).astype(BF16)
        o_ref[...] = _dot(h_ref[...], w_ref[...])

    return pl.pallas_call(
        body, name="inproj_fwd",
        grid=(T // tm, N_SEG),
        in_specs=[pl.BlockSpec((tm, D), lambda i, k: (i, 0)),
                  pl.BlockSpec((1, D), lambda i, k: (0, 0)),
                  pl.BlockSpec((None, None, D, D), lambda i, k: (k // 2, layer, 0, k % 2))],
        out_specs=[pl.BlockSpec((tm, D), lambda i, k: (i, k)),
                   pl.BlockSpec((tm, D), lambda i, k: (i, 0))],
        out_shape=[jax.ShapeDtypeStruct((T, N_SEG * D), F32), jax.ShapeDtypeStruct((T, D), BF16)],
        compiler_params=_cp(("parallel", "arbitrary")),
    )(x2, gain, w_st)


def _rg_gates(xc, wr_ref, br, wi_ref, bi, sp):
    D = xc.shape[1]
    xcb = xc.astype(BF16)
    pr, pi = [], []
    for n in range(D // RG_BLOCK):
        blk = xcb[:, n * RG_BLOCK:(n + 1) * RG_BLOCK]
        pr.append(_dot(blk, wr_ref[n]))
        pi.append(_dot(blk, wi_ref[n]))
    r = _sig(jnp.concatenate(pr, axis=1) + br) if len(pr) > 1 else _sig(pr[0] + br)
    i = _sig(jnp.concatenate(pi, axis=1) + bi) if len(pi) > 1 else _sig(pi[0] + bi)
    la = (-RG_C) * r * sp
    a = jnp.exp(la)
    y = 2.0 * la
    one_m_e2 = jnp.where(y > -1e-2, -(y * (1.0 + 0.5 * y * (1.0 + y * (1.0 / 3.0)))), 1.0 - jnp.exp(y))
    mult = jnp.sqrt(jnp.maximum(one_m_e2, 0.0))
    return r, i, a, mult


def _conv_taps(xbuf, cw_ref, ts):
    acc = None
    for j in range(CONV_TAPS):
        term = cw_ref[j:j + 1, :] * xbuf[pl.ds(SUBLANES - (CONV_TAPS - 1) + j, ts), :]
        acc = term if acc is None else acc + term
    return acc


def _rg_fwd(proj, B, cw, cb, wr, br, wi, bi, sp):
    T = proj.shape[0]
    D = proj.shape[1] // N_SEG
    S = T // B
    ts = _tile(S, RG_TILE)
    nts = S // ts
    nb = D // RG_BLOCK

    def body(xa_ref, ga_ref, cw_ref, cb_ref, wr_ref, br_ref, wi_ref, bi_ref, sp_ref,
             ya_ref, h_ref, xbuf, a_scr, u_scr, carry):
        @pl.when(pl.program_id(1) == 0)
        def _():
            xbuf[0:SUBLANES, :] = jnp.zeros((SUBLANES, D), F32)
            carry[...] = jnp.zeros((SUBLANES, D), F32)

        xbuf[pl.ds(SUBLANES, ts), :] = xa_ref[...]
        xc = _conv_taps(xbuf, cw_ref, ts) + cb_ref[...]
        r, i, a, mult = _rg_gates(xc, wr_ref, br_ref[...], wi_ref, bi_ref[...], sp_ref[...])
        a_scr[...] = a
        u_scr[...] = mult * (i * xc)
        row8 = lax.broadcasted_iota(jnp.int32, (SUBLANES, 1), 0)

        def blk(n, hprev):
            off = pl.multiple_of(n * SUBLANES, SUBLANES)
            a8 = a_scr[pl.ds(off, SUBLANES), :]
            u8 = u_scr[pl.ds(off, SUBLANES), :]
            for d in (1, 2, 4):
                m = row8 >= d
                ap = jnp.where(m, pltpu.roll(a8, d, 0), 1.0)
                up = jnp.where(m, pltpu.roll(u8, d, 0), 0.0)
                u8 = a8 * up + u8
                a8 = a8 * ap
            h8 = u8 + a8 * hprev
            u_scr[pl.ds(off, SUBLANES), :] = h8
            last = jnp.sum(jnp.where(row8 == SUBLANES - 1, h8, 0.0), axis=0, keepdims=True)
            return jnp.broadcast_to(last, (SUBLANES, D))

        carry[...] = lax.fori_loop(0, ts // SUBLANES, blk, carry[...])
        h = u_scr[...]
        h_ref[...] = h
        g, _ = _gelu_and_grad(ga_ref[...])
        ya_ref[...] = h * g
        xbuf[0:SUBLANES, :] = xa_ref[pl.ds(ts - SUBLANES, SUBLANES), :]

    vec = pl.BlockSpec((1, D), lambda b, j: (0, 0))
    gate = pl.BlockSpec((nb, RG_BLOCK, RG_BLOCK), lambda b, j: (0, 0, 0))
    return pl.pallas_call(
        body, name="rg_fwd",
        grid=(B, nts),
        in_specs=[pl.BlockSpec((ts, D), lambda b, j: (b * nts + j, 0)),
                  pl.BlockSpec((ts, D), lambda b, j: (b * nts + j, 1)),
                  pl.BlockSpec((CONV_TAPS, D), lambda b, j: (0, 0)), vec, gate, vec, gate, vec, vec],
        out_specs=[pl.BlockSpec((ts, D), lambda b, j: (b * nts + j, 0))] * 2,
        out_shape=[jax.ShapeDtypeStruct((T, D), F32)] * 2,
        scratch_shapes=[pltpu.VMEM((SUBLANES + ts, D), F32), pltpu.VMEM((ts, D), F32),
                        pltpu.VMEM((ts, D), F32), pltpu.VMEM((SUBLANES, D), F32)],
        compiler_params=_cp(("arbitrary", "arbitrary")),
    )(proj, proj, cw, cb, wr, br, wi, bi, sp)


def _hg_gates(q, z, lb):
    sig = _sig(z)
    one_m = 1.0 - lb
    fg = lb + one_m * sig
    lf = jnp.log(jnp.maximum(fg, F_MIN))
    kf = one_m * (1.0 - sig)
    qs = _sig(q)
    return q * qs, qs, kf, lf, fg, sig


def _hg_cum(lf, C):
    ri = lax.broadcasted_iota(jnp.int32, (C, C), 0)
    ci = lax.broadcasted_iota(jnp.int32, (C, C), 1)
    return _dot_01(jnp.where(ci <= ri, 1.0, 0.0).astype(BF16), lf)


def _hg_levels(lf, cum, C):
    row = lax.broadcasted_iota(jnp.int32, (C, 1), 0)
    levels = []
    w = C // 2
    while w >= 4:
        blk = 2 * w
        upper = (row & w) != 0
        ref = jnp.min(jnp.where(upper, 0.0, cum).reshape(C // blk, blk, HEAD), axis=1, keepdims=True)
        ref = jnp.broadcast_to(ref, (C // blk, blk, HEAD)).reshape(C, HEAD)
        d = cum - ref
        e = jnp.exp(jnp.where(upper, d, -d))
        levels.append((jnp.where(upper, e, 0.0), jnp.where(upper, 0.0, e), blk))
        w //= 2
    r4 = row & 3
    lf_prev = pltpu.roll(lf, 1, 0)
    lf_next = pltpu.roll(lf, C - 1, 0)
    eq = jnp.where(r4 >= 2, jnp.exp(jnp.where(r4 == 3, lf + lf_prev, lf)), 0.0)
    ek = jnp.where(r4 == 0, jnp.exp(lf_next), jnp.where(r4 == 1, 1.0, 0.0))
    levels.append((eq, ek, 4))
    odd = (row & 1) == 1
    levels.append((jnp.where(odd, jnp.exp(lf), 0.0), jnp.where(odd, 0.0, 1.0), 2))
    ones = jnp.ones_like(lf)
    levels.append((ones, ones, 1))
    return levels


def _same_block(C, blk):
    ri = lax.broadcasted_iota(jnp.int32, (C, C), 0)
    ci = lax.broadcasted_iota(jnp.int32, (C, C), 1)
    if blk == 1:
        return ri == ci
    shift = blk.bit_length() - 1
    return (ri >> shift) == (ci >> shift)


def _hg_mask_blocks(C):
    blks = []
    w = C // 4
    while w >= 4:
        blks.append(2 * w)
        w //= 2
    return blks + [4, 2, 1]


def _hg_fill_masks(mask_scr, C):
    for i, blk in enumerate(_hg_mask_blocks(C)):
        mask_scr[i] = jnp.where(_same_block(C, blk), 1.0, 0.0).astype(F32)


def _hg_scores(qf, kf, levels, mask_scr):
    A = None
    for n, (eq, ek, _) in enumerate(levels):
        a = _dot_nt((qf * eq).astype(BF16), (kf * ek).astype(BF16))
        if n > 0:
            a = a * mask_scr[n - 1]
        A = a if A is None else A + a
    return A


def _hg_specs(B, NC, D, C, dtype_blocks):
    def spec(col0, rev):
        if rev:
            return pl.BlockSpec((C, HEAD), lambda b, h, j: (b * NC + (NC - 1 - j), col0 + h))
        return pl.BlockSpec((C, HEAD), lambda b, h, j: (b * NC + j, col0 + h))
    return spec


def _hg_fwd(proj, B, lb, gn):
    T = proj.shape[0]
    D = proj.shape[1] // N_SEG
    S = T // B
    C = min(HG_CHUNK, S)
    NC = S // C
    H = D // HEAD
    hpd = D // HEAD
    spec = _hg_specs(B, NC, D, C, None)

    def body(q_ref, z_ref, v_ref, g_ref, lb_ref, gn_ref, yb_ref, o_ref, st_ref, a_ref, cum_ref, st_scr, mask_scr):
        @pl.when(pl.program_id(2) == 0)
        def _():
            st_scr[...] = jnp.zeros((HEAD, HEAD), F32)
            _hg_fill_masks(mask_scr, C)

        s_t = st_scr[...]
        st_ref[...] = s_t
        qf, _, kf, lf, _, _ = _hg_gates(q_ref[...], z_ref[...], lb_ref[...])
        cum = _hg_cum(lf, C)
        cum_ref[...] = cum
        A = _hg_scores(qf, kf, _hg_levels(lf, cum, C), mask_scr).astype(BF16)
        a_ref[...] = A
        vb = v_ref[...].astype(BF16)
        o = _dot_nt((qf * jnp.exp(cum)).astype(BF16), s_t.astype(BF16)) + _dot(A, vb)
        last = jnp.sum(lf, axis=0, keepdims=True)
        kend = kf * jnp.exp(last - cum)
        st_scr[...] = jnp.exp(last) * s_t + _dot_tn(vb, kend.astype(BF16))
        o_ref[...] = o
        g = g_ref[...]
        yb_ref[...] = (o * _rms(o) * gn_ref[...]) * (g * _sig(g))

    return pl.pallas_call(
        body, name="hg_fwd",
        grid=(B, H, NC),
        in_specs=[spec(2 * hpd, False), spec(3 * hpd, False), spec(4 * hpd, False), spec(5 * hpd, False),
                  pl.BlockSpec((1, HEAD), lambda b, h, j: (0, h)),
                  pl.BlockSpec((1, HEAD), lambda b, h, j: (0, 0))],
        out_specs=[spec(0, False), spec(0, False),
                   pl.BlockSpec((None, None, None, HEAD, HEAD), lambda b, h, j: (b, h, j, 0, 0)),
                   pl.BlockSpec((None, None, None, C, C), lambda b, h, j: (b, h, j, 0, 0)), spec(0, False)],
        out_shape=[jax.ShapeDtypeStruct((T, D), F32), jax.ShapeDtypeStruct((T, D), F32),
                   jax.ShapeDtypeStruct((B, H, NC, HEAD, HEAD), F32),
                   jax.ShapeDtypeStruct((B, H, NC, C, C), BF16), jax.ShapeDtypeStruct((T, D), F32)],
        scratch_shapes=[pltpu.VMEM((HEAD, HEAD), F32), pltpu.VMEM((len(_hg_mask_blocks(C)), C, C), F32)],
        compiler_params=_cp(("parallel", "parallel", "arbitrary")),
    )(proj, proj, proj, proj, lb, gn)


def _w_full(ref):
    s, r, c = ref.shape
    return ref[...].reshape(s * r, c)


def _out_fwd(ya, yb, proj, x2, w_st, layer):
    T, D = x2.shape
    tm = _tile(T, 512)

    def body(ya_ref, yb_ref, ma_ref, mb_ref, x_ref, w_ref, xm_ref, y_ref):
        y = (_sig(ma_ref[...]) * ya_ref[...] + _sig(mb_ref[...]) * yb_ref[...]).astype(BF16)
        y_ref[...] = y
        xm_ref[...] = x_ref[...] + _dot(y, _w_full(w_ref))

    row = pl.BlockSpec((tm, D), lambda i: (i, 0))
    return pl.pallas_call(
        body, name="out_fwd",
        grid=(T // tm,),
        in_specs=[row, row, pl.BlockSpec((tm, D), lambda i: (i, 6)), pl.BlockSpec((tm, D), lambda i: (i, 7)), row,
                  pl.BlockSpec((N_SHARD, None, D // N_SHARD, D), lambda i: (0, layer, 0, 0))],
        out_specs=[row, row],
        out_shape=[jax.ShapeDtypeStruct((T, D), F32), jax.ShapeDtypeStruct((T, D), BF16)],
        compiler_params=_cp(("parallel",)),
    )(ya, yb, proj, proj, x2, w_st)


def _mlp_fwd(xm, gain, wup_st, wdn_st, layer):
    T, D = xm.shape
    F4 = wup_st.shape[3]
    tm = _tile(T, 1024)

    def body(x_ref, g_ref, wu_ref, wd_ref, xo_ref, up_ref, h_ref):
        @pl.when(pl.program_id(1) == 0)
        def _():
            x = x_ref[...]
            h_ref[...] = (x * _rms(x) * g_ref[...]).astype(BF16)
            xo_ref[...] = x
        up = _dot(h_ref[...], wu_ref[...])
        up_ref[...] = up.astype(BF16)
        act = jnp.maximum(up, 0.0)
        xo_ref[...] += _dot((act * act).astype(BF16), wd_ref[...])

    row = pl.BlockSpec((tm, D), lambda i, s: (i, 0))
    return pl.pallas_call(
        body, name="mlp_fwd",
        grid=(T // tm, N_SHARD),
        in_specs=[row, pl.BlockSpec((1, D), lambda i, s: (0, 0)),
                  pl.BlockSpec((None, None, D, F4), lambda i, s: (s, layer, 0, 0)),
                  pl.BlockSpec((None, None, F4, D), lambda i, s: (s, layer, 0, 0))],
        out_specs=[row, pl.BlockSpec((tm, F4), lambda i, s: (i, s)), row],
        out_shape=[jax.ShapeDtypeStruct((T, D), F32), jax.ShapeDtypeStruct((T, N_SHARD * F4), BF16),
                   jax.ShapeDtypeStruct((T, D), BF16)],
        compiler_params=_cp(("parallel", "arbitrary")),
    )(xm, gain, wup_st, wdn_st)


def _final_loss(x2, gain, tgt):
    T, D = x2.shape
    tm = _tile(T, 512)
    nt = T // tm

    def body(x_ref, g_ref, t_ref, loss_ref, dx_ref, dg_ref):
        x = x_ref[...]
        rs = _rms(x)
        err = x * rs * g_ref[...] - t_ref[...]
        part = 0.5 * jnp.sum(jnp.sum(err * err, axis=-1, keepdims=True) * (1.0 / D), axis=0, keepdims=True)
        loss_ref[...] = jnp.broadcast_to(part, (SUBLANES, 128))
        dx, dg = _rms_bwd(err * (1.0 / D), x, rs, g_ref[...])
        dx_ref[...] = dx
        dg_ref[...] = dg

    row = pl.BlockSpec((tm, D), lambda i: (i, 0))
    return pl.pallas_call(
        body, name="final_loss",
        grid=(nt,),
        in_specs=[row, pl.BlockSpec((1, D), lambda i: (0, 0)), row],
        out_specs=[pl.BlockSpec((None, SUBLANES, 128), lambda i: (i, 0, 0)), row,
                   pl.BlockSpec((None, SUBLANES, D), lambda i: (i, 0, 0))],
        out_shape=[jax.ShapeDtypeStruct((nt, SUBLANES, 128), F32), jax.ShapeDtypeStruct((T, D), F32),
                   jax.ShapeDtypeStruct((nt, SUBLANES, D), F32)],
        compiler_params=_cp(("parallel",)),
    )(x2, gain, tgt)


def _mlp_bwd_x(dx, xm, up, gain, wup_st, wdn_st, layer):
    T, D = xm.shape
    F4 = wup_st.shape[3]
    tm = _tile(T, 1024)
    nt = T // tm

    def body(dx_ref, x_ref, up_ref, g_ref, wu_ref, wd_ref, dxm_ref, dup_ref, dg_ref, dxb):
        s = pl.program_id(1)

        @pl.when(s == 0)
        def _():
            dxb[...] = dx_ref[...].astype(BF16)
            dxm_ref[...] = jnp.zeros((tm, D), F32)

        d_act = _dot_nt(dxb[...], wd_ref[...])
        d_up = (d_act * (2.0 * jnp.maximum(up_ref[...].astype(F32), 0.0))).astype(BF16)
        dup_ref[...] = d_up
        dxm_ref[...] += _dot_nt(d_up, wu_ref[...])

        @pl.when(s == N_SHARD - 1)
        def _():
            x = x_ref[...]
            dxn, dg = _rms_bwd(dxm_ref[...], x, _rms(x), g_ref[...])
            dxm_ref[...] = dx_ref[...] + dxn
            dg_ref[...] = dg

    row = pl.BlockSpec((tm, D), lambda i, s: (i, 0))
    return pl.pallas_call(
        body, name="mlp_bwd_x",
        grid=(nt, N_SHARD),
        in_specs=[row, row, pl.BlockSpec((tm, F4), lambda i, s: (i, s)), pl.BlockSpec((1, D), lambda i, s: (0, 0)),
                  pl.BlockSpec((None, None, D, F4), lambda i, s: (s, layer, 0, 0)),
                  pl.BlockSpec((None, None, F4, D), lambda i, s: (s, layer, 0, 0))],
        out_specs=[row, pl.BlockSpec((tm, F4), lambda i, s: (i, s)),
                   pl.BlockSpec((None, SUBLANES, D), lambda i, s: (i, 0, 0)), row],
        out_shape=[jax.ShapeDtypeStruct((T, D), F32), jax.ShapeDtypeStruct((T, N_SHARD * F4), BF16),
                   jax.ShapeDtypeStruct((nt, SUBLANES, D), F32), jax.ShapeDtypeStruct((T, D), BF16)],
        compiler_params=_cp(("parallel", "arbitrary")),
    )(dx, xm, up, gain, wup_st, wdn_st)


def _layer_slot(bufs, shapes, n_layers):
    out_shape = [jax.ShapeDtypeStruct((N_SHARD, n_layers) + s, F32) for s in shapes]
    return out_shape, ([] if bufs is None else list(bufs))


def _mlp_bwd_w(up, dxb, h, dup, layer, n_layers, bufs):
    T, D = dxb.shape
    F4 = up.shape[1] // N_SHARD
    tk = _tile(T, 1024)
    out_shape, extra = _layer_slot(bufs, [(D, F4), (F4, D)], n_layers)

    def body(up_ref, dx_ref, h_ref, dup_ref, *rest):
        gu_ref, gd_ref = rest[-2:]

        @pl.when(pl.program_id(1) == 0)
        def _():
            gu_ref[...] = jnp.zeros((D, F4), F32)
            gd_ref[...] = jnp.zeros((F4, D), F32)
        act = jnp.maximum(up_ref[...], 0.0)
        gd_ref[...] += _dot_tn(act * act, dx_ref[...])
        gu_ref[...] += _dot_tn(h_ref[...], dup_ref[...])

    return pl.pallas_call(
        body, name="mlp_bwd_w",
        grid=(N_SHARD, T // tk),
        in_specs=[pl.BlockSpec((tk, F4), lambda s, t: (t, s)), pl.BlockSpec((tk, D), lambda s, t: (t, 0)),
                  pl.BlockSpec((tk, D), lambda s, t: (t, 0)), pl.BlockSpec((tk, F4), lambda s, t: (t, s))]
        + [ANY] * len(extra),
        out_specs=[pl.BlockSpec((None, None, D, F4), lambda s, t: (s, layer, 0, 0)),
                   pl.BlockSpec((None, None, F4, D), lambda s, t: (s, layer, 0, 0))],
        out_shape=out_shape,
        input_output_aliases={4 + i: i for i in range(len(extra))},
        compiler_params=_cp(("parallel", "arbitrary")),
    )(up, dxb, h, dup, *extra)


def _out_bwd_x(dxm, ya, yb, proj, w_st, layer):
    T, D = dxm.shape
    tm = _tile(T, 512)

    def body(dx_ref, ya_ref, yb_ref, ma_ref, mb_ref, w_ref, dya_ref, dyb_ref, dma_ref, dmb_ref):
        dy = _dot_nt(dx_ref[...].astype(BF16), _w_full(w_ref))
        sa = _sig(ma_ref[...])
        sb = _sig(mb_ref[...])
        dya_ref[...] = dy * sa
        dyb_ref[...] = dy * sb
        dma_ref[...] = (dy * ya_ref[...] * (sa * (1.0 - sa))).astype(BF16)
        dmb_ref[...] = (dy * yb_ref[...] * (sb * (1.0 - sb))).astype(BF16)

    row = pl.BlockSpec((tm, D), lambda i: (i, 0))
    return pl.pallas_call(
        body, name="out_bwd_x",
        grid=(T // tm,),
        in_specs=[row, row, row, pl.BlockSpec((tm, D), lambda i: (i, 6)), pl.BlockSpec((tm, D), lambda i: (i, 7)),
                  pl.BlockSpec((N_SHARD, None, D // N_SHARD, D), lambda i: (0, layer, 0, 0))],
        out_specs=[row] * 4,
        out_shape=[jax.ShapeDtypeStruct((T, D), F32)] * 2 + [jax.ShapeDtypeStruct((T, D), BF16)] * 2,
        compiler_params=_cp(("parallel",)),
    )(dxm, ya, yb, proj, proj, w_st)


def _out_bwd_w(ymix, dxm, layer, n_layers, bufs):
    T, D = dxm.shape
    tk = _tile(T, 1024)
    out_shape, extra = _layer_slot(bufs, [(D // N_SHARD, D)], n_layers)

    def body(y_ref, dx_ref, *rest):
        g_ref = rest[-1]

        @pl.when(pl.program_id(0) == 0)
        def _():
            g_ref[...] = jnp.zeros((N_SHARD, D // N_SHARD, D), F32)
        g = _dot_tn(y_ref[...], dx_ref[...].astype(BF16))
        g_ref[...] += g.reshape(N_SHARD, D // N_SHARD, D)

    row = pl.BlockSpec((tk, D), lambda t: (t, 0))
    return pl.pallas_call(
        body, name="out_bwd_w",
        grid=(T // tk,),
        in_specs=[row, row] + [ANY] * len(extra),
        out_specs=[pl.BlockSpec((N_SHARD, None, D // N_SHARD, D), lambda t: (0, layer, 0, 0))],
        out_shape=out_shape,
        input_output_aliases={2 + i: i for i in range(len(extra))},
        compiler_params=_cp(("arbitrary",)),
    )(ymix, dxm, *extra)[0]


def _rg_bwd(proj, hrg, dya, B, cw, cb, wr, br, wi, bi, sp):
    T = proj.shape[0]
    D = proj.shape[1] // N_SEG
    S = T // B
    ts = _tile(S, RG_TILE)
    nts = S // ts
    nb = D // RG_BLOCK
    t8 = ts // SUBLANES

    def body(xa_ref, xp_ref, ga_ref, h_ref, hp_ref, dya_ref, cw_ref, cb_ref, wr_ref, br_ref, wi_ref, bi_ref, sp_ref,
             dxa_ref, dga_ref, gwr_ref, gwi_ref, gcw_ref, gcb_ref, gbr_ref, gbi_ref, gsp_ref,
             xbuf, hbuf, abuf, dbuf, g_scr, c_scr, gcar):
        b = pl.program_id(0)
        j = pl.program_id(1)
        first_in_time = j == nts - 1

        @pl.when((b == 0) & (j == 0))
        def _():
            gwr_ref[...] = jnp.zeros((nb, RG_BLOCK, RG_BLOCK), F32)
            gwi_ref[...] = jnp.zeros((nb, RG_BLOCK, RG_BLOCK), F32)
            gcw_ref[...] = jnp.zeros((CONV_TAPS, SUBLANES, D), F32)
            for r in (gcb_ref, gbr_ref, gbi_ref, gsp_ref):
                r[...] = jnp.zeros((SUBLANES, D), F32)

        @pl.when(j == 0)
        def _():
            abuf[pl.ds(ts, SUBLANES), :] = jnp.zeros((SUBLANES, D), F32)
            dbuf[pl.ds(ts, SUBLANES), :] = jnp.zeros((SUBLANES, D), F32)
            gcar[...] = jnp.zeros((SUBLANES, D), F32)

        keep = jnp.where(first_in_time, 0.0, 1.0)
        xbuf[0:SUBLANES, :] = xp_ref[...] * keep
        xbuf[pl.ds(SUBLANES, ts), :] = xa_ref[...]
        hbuf[0:SUBLANES, :] = hp_ref[...] * keep
        hbuf[pl.ds(SUBLANES, ts), :] = h_ref[...]

        xc = _conv_taps(xbuf, cw_ref, ts) + cb_ref[...]
        sp = sp_ref[...]
        r, i, a, mult = _rg_gates(xc, wr_ref, br_ref[...], wi_ref, bi_ref[...], sp)
        g_gate, dg_gate = _gelu_and_grad(ga_ref[...])
        dya = dya_ref[...]
        dga_ref[...] = (dya * h_ref[...] * dg_gate).astype(BF16)

        abuf[0:ts, :] = a
        c_scr[...] = abuf[pl.ds(1, ts), :]
        g_scr[...] = dya * g_gate
        row8 = lax.broadcasted_iota(jnp.int32, (SUBLANES, 1), 0)

        def blk(n, gnext):
            off = pl.multiple_of((t8 - 1 - n) * SUBLANES, SUBLANES)
            c8 = c_scr[pl.ds(off, SUBLANES), :]
            d8 = g_scr[pl.ds(off, SUBLANES), :]
            for d in (1, 2, 4):
                m = row8 < SUBLANES - d
                cn = jnp.where(m, pltpu.roll(c8, SUBLANES - d, 0), 1.0)
                dn = jnp.where(m, pltpu.roll(d8, SUBLANES - d, 0), 0.0)
                d8 = d8 + c8 * dn
                c8 = c8 * cn
            g8 = d8 + c8 * gnext
            g_scr[pl.ds(off, SUBLANES), :] = g8
            first = jnp.sum(jnp.where(row8 == 0, g8, 0.0), axis=0, keepdims=True)
            return jnp.broadcast_to(first, (SUBLANES, D))

        gcar[...] = lax.fori_loop(0, t8, blk, gcar[...])
        abuf[pl.ds(ts, SUBLANES), :] = a[0:SUBLANES, :]

        g = g_scr[...]
        hprev = hbuf[pl.ds(SUBLANES - 1, ts), :]
        gx = i * xc
        e2 = a * a
        dla = g * hprev * a - jnp.where(mult > 0.0, g * gx * e2 / jnp.where(mult > 0.0, mult, 1.0), 0.0)
        dgx = g * mult
        dpr = (dla * ((-RG_C) * sp)) * (r * (1.0 - r))
        dpi = (dgx * xc) * (i * (1.0 - i))
        gsp_ref[...] += _rows8(dla * ((-RG_C) * r))
        gbr_ref[...] += _rows8(dpr)
        gbi_ref[...] += _rows8(dpi)
        dprb = dpr.astype(BF16)
        dpib = dpi.astype(BF16)
        xcb = xc.astype(BF16)
        back = []
        for n in range(nb):
            sl = slice(n * RG_BLOCK, (n + 1) * RG_BLOCK)
            back.append(_dot_nt(dprb[:, sl], wr_ref[n]) + _dot_nt(dpib[:, sl], wi_ref[n]))
            gwr_ref[n] += _dot_tn(xcb[:, sl], dprb[:, sl])
            gwi_ref[n] += _dot_tn(xcb[:, sl], dpib[:, sl])
        dxc = dgx * i + (jnp.concatenate(back, axis=1) if nb > 1 else back[0])
        gcb_ref[...] += _rows8(dxc)

        dbuf[0:ts, :] = dxc
        dxa = None
        for jtap in range(CONV_TAPS):
            term = cw_ref[jtap:jtap + 1, :] * dbuf[pl.ds(CONV_TAPS - 1 - jtap, ts), :]
            dxa = term if dxa is None else dxa + term
            gcw_ref[jtap] += _rows8(dxc * xbuf[pl.ds(SUBLANES - (CONV_TAPS - 1) + jtap, ts), :])
        dxa_ref[...] = dxa.astype(BF16)
        dbuf[pl.ds(ts, SUBLANES), :] = dxc[0:SUBLANES, :]

    def tile_map(col):
        return lambda b, j: (b * nts + (nts - 1 - j), col)

    def prev8_map(col):
        return lambda b, j: (jnp.maximum((b * nts + (nts - 1 - j)) * t8 - 1, 0), col)

    vec = pl.BlockSpec((1, D), lambda b, j: (0, 0))
    gate = pl.BlockSpec((nb, RG_BLOCK, RG_BLOCK), lambda b, j: (0, 0, 0))
    acc8 = pl.BlockSpec((SUBLANES, D), lambda b, j: (0, 0))
    return pl.pallas_call(
        body, name="rg_bwd",
        grid=(B, nts),
        in_specs=[pl.BlockSpec((ts, D), tile_map(0)), pl.BlockSpec((SUBLANES, D), prev8_map(0)),
                  pl.BlockSpec((ts, D), tile_map(1)),
                  pl.BlockSpec((ts, D), tile_map(0)), pl.BlockSpec((SUBLANES, D), prev8_map(0)),
                  pl.BlockSpec((ts, D), tile_map(0)),
                  pl.BlockSpec((CONV_TAPS, D), lambda b, j: (0, 0)), vec, gate, vec, gate, vec, vec],
        out_specs=[pl.BlockSpec((ts, D), tile_map(0)), pl.BlockSpec((ts, D), tile_map(0)), gate, gate,
                   pl.BlockSpec((CONV_TAPS, SUBLANES, D), lambda b, j: (0, 0, 0)), acc8, acc8, acc8, acc8],
        out_shape=[jax.ShapeDtypeStruct((T, D), BF16)] * 2
        + [jax.ShapeDtypeStruct((nb, RG_BLOCK, RG_BLOCK), F32)] * 2
        + [jax.ShapeDtypeStruct((CONV_TAPS, SUBLANES, D), F32)] + [jax.ShapeDtypeStruct((SUBLANES, D), F32)] * 4,
        scratch_shapes=[pltpu.VMEM((SUBLANES + ts, D), F32), pltpu.VMEM((SUBLANES + ts, D), F32),
                        pltpu.VMEM((ts + SUBLANES, D), F32), pltpu.VMEM((ts + SUBLANES, D), F32),
                        pltpu.VMEM((ts, D), F32), pltpu.VMEM((ts, D), F32), pltpu.VMEM((SUBLANES, D), F32)],
        compiler_params=_cp(("arbitrary", "arbitrary")),
    )(proj, proj, proj, hrg, hrg, dya, cw, cb, wr, br, wi, bi, sp)


def _hg_bwd(proj, o_sv, dyb, states, a_sv, cum_sv, B, lb, gn):
    T = proj.shape[0]
    D = proj.shape[1] // N_SEG
    S = T // B
    C = min(HG_CHUNK, S)
    NC = S // C
    H = D // HEAD
    hpd = D // HEAD
    spec = _hg_specs(B, NC, D, C, None)

    def body(q_ref, z_ref, v_ref, g_ref, o_ref, dyb_ref, st_ref, a_ref, cum_ref, lb_ref, gn_ref,
             dq_ref, dz_ref, dv_ref, dg_ref, glb_ref, ggn_ref, ds_scr, mask_scr):
        @pl.when(pl.program_id(2) == 0)
        def _():
            ds_scr[...] = jnp.zeros((HEAD, HEAD), F32)
            _hg_fill_masks(mask_scr, C)
            glb_ref[...] = jnp.zeros((SUBLANES, HEAD), F32)
            ggn_ref[...] = jnp.zeros((SUBLANES, HEAD), F32)

        q = q_ref[...]
        lb = lb_ref[...]
        gn = gn_ref[...]
        qf, qs, kf, lf, fg, sig = _hg_gates(q, z_ref[...], lb)
        cum = cum_ref[...]
        levels = _hg_levels(lf, cum, C)

        o = o_ref[...]
        g = g_ref[...]
        gs = _sig(g)
        rs = _rms(o)
        dyb = dyb_ref[...]
        don = dyb * (g * gs)
        dg_ref[...] = (dyb * (o * rs * gn) * (gs * (1.0 + g * (1.0 - gs)))).astype(BF16)
        ggn_ref[...] += _rows8(don * o * rs)
        dn = don * gn
        do = rs * (dn - o * (rs * rs) * jnp.mean(dn * o, axis=-1, keepdims=True))

        s_t = st_ref[...].astype(BF16)
        ds_t = ds_scr[...]
        ds_b = ds_t.astype(BF16)
        dob = do.astype(BF16)
        vb = v_ref[...].astype(BF16)
        ecum = jnp.exp(cum)
        last = jnp.sum(lf, axis=0, keepdims=True)
        eend = jnp.exp(last - cum)
        qhat = (qf * ecum).astype(BF16)
        kend = (kf * eend).astype(BF16)

        dA = _dot_nt(dob, vb)
        dq_inter = _dot(dob, s_t)
        dk_state = _dot(vb, ds_b)
        dqf = dq_inter * ecum
        dkf = dk_state * eend
        g_intra = None
        for n, (eq, ek, _) in enumerate(levels):
            qw = (qf * eq).astype(BF16)
            kw = (kf * ek).astype(BF16)
            dam = (dA * mask_scr[n - 1]).astype(BF16) if n > 0 else dA.astype(BF16)
            rq = _dot(dam, kw)
            rk = _dot_tn(dam, qw)
            dqf += rq * eq
            dkf += rk * ek
            gi = qw.astype(F32) * rq - kw.astype(F32) * rk
            g_intra = gi if g_intra is None else g_intra + gi
        dv_ref[...] = (_dot_tn(a_ref[...], dob) + _dot_nt(kend, ds_b)).astype(BF16)
        e_last = jnp.exp(last)
        ds_scr[...] = e_last * ds_t + _dot_tn(dob, qhat)

        ri = lax.broadcasted_iota(jnp.int32, (C, C), 0)
        ci = lax.broadcasted_iota(jnp.int32, (C, C), 1)
        y_state = kend.astype(F32) * dk_state
        dlf = (_dot_01(jnp.where(ci >= ri, 1.0, 0.0).astype(BF16), g_intra + qhat.astype(F32) * dq_inter - y_state)
               + jnp.sum(y_state, axis=0, keepdims=True)
               + jnp.sum(e_last * st_ref[...] * ds_t, axis=0, keepdims=True))
        dfg = jnp.where(fg > F_MIN, dlf / jnp.maximum(fg, F_MIN), 0.0)
        sneg = 1.0 - sig
        diff = dfg - dkf
        dz_ref[...] = ((1.0 - lb) * sig * sneg * diff).astype(BF16)
        glb_ref[...] += _rows8(sneg * diff)
        dq_ref[...] = (dqf * (qs * (1.0 + q * (1.0 - qs)))).astype(BF16)

    return pl.pallas_call(
        body, name="hg_bwd",
        grid=(B, H, NC),
        in_specs=[spec(2 * hpd, True), spec(3 * hpd, True), spec(4 * hpd, True), spec(5 * hpd, True),
                  spec(0, True), spec(0, True),
                  pl.BlockSpec((None, None, None, HEAD, HEAD), lambda b, h, j: (b, h, NC - 1 - j, 0, 0)),
                  pl.BlockSpec((None, None, None, C, C), lambda b, h, j: (b, h, NC - 1 - j, 0, 0)), spec(0, True),
                  pl.BlockSpec((1, HEAD), lambda b, h, j: (0, h)),
                  pl.BlockSpec((1, HEAD), lambda b, h, j: (0, 0))],
        out_specs=[spec(0, True)] * 4
        + [pl.BlockSpec((None, SUBLANES, HEAD), lambda b, h, j: (b, 0, h)),
           pl.BlockSpec((None, None, SUBLANES, HEAD), lambda b, h, j: (b, h, 0, 0))],
        out_shape=[jax.ShapeDtypeStruct((T, D), BF16)] * 4
        + [jax.ShapeDtypeStruct((B, SUBLANES, D), F32), jax.ShapeDtypeStruct((B, H, SUBLANES, HEAD), F32)],
        scratch_shapes=[pltpu.VMEM((HEAD, HEAD), F32), pltpu.VMEM((len(_hg_mask_blocks(C)), C, C), F32)],
        compiler_params=_cp(("parallel", "parallel", "arbitrary")),
    )(proj, proj, proj, proj, o_sv, dyb, states, a_sv, cum_sv, lb, gn)


def _inproj_bwd_x(dsegs, w_st, layer, x2, gain, dxm):
    T, D = x2.shape
    tm = _tile(T, 512)
    nt = T // tm

    def body(*refs):
        seg_refs = refs[:N_SEG]
        w_ref, x_ref, g_ref, dxm_ref, dx_ref, dg_ref = refs[N_SEG:]
        k = pl.program_id(1)

        @pl.when(k == 0)
        def _():
            dx_ref[...] = jnp.zeros((tm, D), F32)

        for kk in range(N_SEG):
            @pl.when(k == kk)
            def _(kk=kk):
                dx_ref[...] += _dot_nt(seg_refs[kk][...], w_ref[...])

        @pl.when(k == N_SEG - 1)
        def _():
            x = x_ref[...]
            dxn, dg = _rms_bwd(dx_ref[...], x, _rms(x), g_ref[...])
            dx_ref[...] = dxm_ref[...] + dxn
            dg_ref[...] = dg

    row = pl.BlockSpec((tm, D), lambda i, k: (i, 0))

    def seg_spec(kk):
        return pl.BlockSpec((tm, D), lambda i, k: (jnp.minimum(i + jnp.where(k > kk, 1, 0), nt - 1), 0))

    return pl.pallas_call(
        body, name="inproj_bwd_x",
        grid=(nt, N_SEG),
        in_specs=[seg_spec(kk) for kk in range(N_SEG)]
        + [pl.BlockSpec((None, None, D, D), lambda i, k: (k // 2, layer, 0, k % 2)), row,
           pl.BlockSpec((1, D), lambda i, k: (0, 0)), row],
        out_specs=[row, pl.BlockSpec((None, SUBLANES, D), lambda i, k: (i, 0, 0))],
        out_shape=[jax.ShapeDtypeStruct((T, D), F32), jax.ShapeDtypeStruct((nt, SUBLANES, D), F32)],
        compiler_params=_cp(("parallel", "arbitrary")),
    )(*dsegs, w_st, x2, gain, dxm)


def _inproj_bwd_w(h, dsegs, layer, n_layers, bufs):
    T, D = h.shape
    tk = _tile(T, 1024)
    out_shape, extra = _layer_slot(bufs, [(D, 2 * D)], n_layers)

    def body(*refs):
        h_ref = refs[0]
        seg_refs = refs[1:1 + N_SEG]
        g_ref = refs[-1]
        k = pl.program_id(0)

        @pl.when(pl.program_id(1) == 0)
        def _():
            g_ref[...] = jnp.zeros((D, D), F32)

        for kk in range(N_SEG):
            @pl.when(k == kk)
            def _(kk=kk):
                g_ref[...] += _dot_tn(h_ref[...], seg_refs[kk][...])

    def seg_spec(kk):
        return pl.BlockSpec((tk, D), lambda k, t: (jnp.where(k == kk, t, 0), 0))

    return pl.pallas_call(
        body, name="inproj_bwd_w",
        grid=(N_SEG, T // tk),
        in_specs=[pl.BlockSpec((tk, D), lambda k, t: (t, 0))] + [seg_spec(kk) for kk in range(N_SEG)]
        + [ANY] * len(extra),
        out_specs=[pl.BlockSpec((None, None, D, D), lambda k, t: (k // 2, layer, 0, k % 2))],
        out_shape=out_shape,
        input_output_aliases={1 + N_SEG + i: i for i in range(len(extra))},
        compiler_params=_cp(("parallel", "arbitrary")),
    )(h, *dsegs, *extra)[0]


def _softmax_rows(lg_ref, L):
    rows = [lg_ref[l:l + 1, :] for l in range(L)]
    mx = functools.reduce(jnp.maximum, rows)
    es = [jnp.exp(r - mx) for r in rows]
    den = functools.reduce(lambda p, q: p + q, es)
    return [e / den for e in es]


def _prep(lb_logits, lam):
    L, D = lb_logits.shape

    def body(lg_ref, lam_ref, lowb_ref, sp_ref):
        sm = _softmax_rows(lg_ref, L)
        run = sm[0]
        for l in range(L):
            if l > 0:
                run = run + sm[l]
            lowb_ref[l:l + 1, :] = jnp.clip(run - sm[0], 0.0, 1.0)
        y = -lam_ref[...]
        sp_ref[...] = jnp.maximum(y, 0.0) + jnp.log1p(jnp.exp(-jnp.abs(y)))

    return pl.pallas_call(
        body, name="prep_small",
        out_shape=[jax.ShapeDtypeStruct((L, D), F32)] * 2,
    )(lb_logits, lam)


def _local_step(x, tgt, lowb, sp, norm_mix, w_in_st, conv_w, conv_b, w_r, b_r, w_i, b_i, hg_norm,
                w_out_st, norm_mlp, w_up_st, w_down_st, norm_final):
    B, S, D = x.shape
    L = norm_mix.shape[0]
    T = B * S
    x2 = x.reshape(T, D)
    row = lambda a, l: a[l:l + 1]
    saved = []
    for l in range(L):
        proj, h = _inproj_fwd(x2, row(norm_mix, l), w_in_st, l)
        ya, hrg = _rg_fwd(proj, B, conv_w[l], row(conv_b, l), w_r[l], row(b_r, l), w_i[l], row(b_i, l), row(sp, l))
        yb, o, st, a_sv, cum_sv = _hg_fwd(proj, B, row(lowb, l), row(hg_norm, l))
        xm, ymix = _out_fwd(ya, yb, proj, x2, w_out_st, l)
        xo, up, h2 = _mlp_fwd(xm, row(norm_mlp, l), w_up_st, w_down_st, l)
        saved.append((x2, proj, h, ya, hrg, yb, o, (st, a_sv, cum_sv), xm, ymix, up, h2))
        x2 = xo
    loss_parts, dx, g_nf = _final_loss(x2, norm_final[None, :], tgt.reshape(T, D))

    gates = []
    small = []
    g_in = g_out = g_mlp = None
    for l in reversed(range(L)):
        x_in, proj, h, ya, hrg, yb, o, st, xm, ymix, up, h2 = saved[l]
        dxm, dup, g_nmlp, dxb = _mlp_bwd_x(dx, xm, up, row(norm_mlp, l), w_up_st, w_down_st, l)
        g_mlp = _mlp_bwd_w(up, dxb, h2, dup, l, L, g_mlp)
        dya, dyb, dma, dmb = _out_bwd_x(dxm, ya, yb, proj, w_out_st, l)
        g_out = _out_bwd_w(ymix, dxm, l, L, None if g_out is None else [g_out])
        dxa, dga, g_wr, g_wi, g_cw, g_cb, g_br, g_bi, g_sp = _rg_bwd(
            proj, hrg, dya, B, conv_w[l], row(conv_b, l), w_r[l], row(b_r, l), w_i[l], row(b_i, l), row(sp, l))
        dq, dz, dv, dg, g_lb, g_gn = _hg_bwd(proj, o, dyb, *st, B, row(lowb, l), row(hg_norm, l))
        dsegs = (dxa, dga, dq, dz, dv, dg, dma, dmb)
        dx, g_nmix = _inproj_bwd_x(dsegs, w_in_st, l, x_in, row(norm_mix, l), dxm)
        g_in = _inproj_bwd_w(h, dsegs, l, L, None if g_in is None else [g_in])
        gates.append((g_wr, g_wi))
        small.append((g_lb, g_nmix, g_cb, g_br, g_bi, g_sp, g_nmlp, g_gn, g_cw))
    gates.reverse()
    small.reverse()
    big = (g_in, g_out, g_mlp[0], g_mlp[1], jnp.stack([g[0] for g in gates]), jnp.stack([g[1] for g in gates]))
    return loss_parts, dx.reshape(B, S, D), big, small, g_nf


def _me():
    return lax.axis_index("x"), lax.axis_index("y"), lax.axis_index("c")


def _cast_place(w, slot):
    R, N = w.shape
    tr = _tile(R, max(16, (1 << 20) // N))

    def body(slot_ref, w_ref, o_ref):
        o_ref[...] = w_ref[...].astype(BF16)

    return pl.pallas_call(
        body, name="cast_place",
        grid_spec=pltpu.PrefetchScalarGridSpec(
            num_scalar_prefetch=1, grid=(R // tr,),
            in_specs=[pl.BlockSpec((tr, N), lambda i, slot: (i, 0))],
            out_specs=pl.BlockSpec((None, tr, N), lambda i, slot: (slot[0], i, 0))),
        out_shape=jax.ShapeDtypeStruct((N_SHARD, R, N), BF16),
        compiler_params=_cp(("parallel",)),
    )(slot, w)


def _gather_weights(bufs, first_axes):
    n = len(bufs)

    def body(*refs):
        outs = refs[n:2 * n]
        ssem, rsem = refs[2 * n:]
        x, y, c = _me()
        sib = (x, y, 1 - c)

        def piece(a, fx, fy, half):
            rh = outs[a].shape[1] // 2
            sx = 1 - x if fx else x
            sy = 1 - y if fy else y
            return outs[a].at[2 * sx + sy, pl.ds(half * rh, rh), :]

        def rcopy(a, k, ref, dev):
            return pltpu.make_async_remote_copy(src_ref=ref, dst_ref=ref, send_sem=ssem.at[a, k], recv_sem=rsem.at[a, k],
                                                device_id=dev, device_id_type=MESH_ID)

        sent = []

        def send(a, k, ref, dev):
            sent.append(rcopy(a, k, ref, dev))
            sent[-1].start()

        nbr, flip = [], []
        for a in range(n):
            fx = first_axes[a] == "x"
            f_dev = (1 - x, y, c) if fx else (x, 1 - y, c)
            g_dev = (x, 1 - y, c) if fx else (1 - x, y, c)
            f_flip = (1, 0) if fx else (0, 1)
            g_flip = (0, 1) if fx else (1, 0)
            nbr.append((f_dev, g_dev))
            flip.append((f_flip, g_flip))
            own = piece(a, 0, 0, c)
            send(a, 0, own, f_dev)
            send(a, 1, own, g_dev)
        for a in range(n):
            (f_dev, g_dev), (f_flip, _) = nbr[a], flip[a]
            got = piece(a, *f_flip, c)
            rcopy(a, 0, got, f_dev).wait_recv()
            send(a, 2, got, g_dev)
            send(a, 3, got, sib)
        for a in range(n):
            (_, g_dev), (_, g_flip) = nbr[a], flip[a]
            got = piece(a, *g_flip, c)
            rcopy(a, 1, got, g_dev).wait_recv()
            send(a, 4, got, sib)
        for a in range(n):
            got = piece(a, 1, 1, c)
            rcopy(a, 2, got, nbr[a][1]).wait_recv()
            send(a, 5, got, sib)
        for a in range(n):
            (f_flip, g_flip) = flip[a]
            for k, fl in ((3, f_flip), (4, g_flip), (5, (1, 1))):
                rcopy(a, k, piece(a, *fl, 1 - c), sib).wait_recv()
        for cp in sent:
            cp.wait_send()

    return pl.pallas_call(
        body, name="gather_weights",
        in_specs=[ANY] * n, out_specs=[ANY] * n,
        out_shape=[jax.ShapeDtypeStruct(b.shape, b.dtype) for b in bufs],
        input_output_aliases={a: a for a in range(n)},
        scratch_shapes=[pltpu.SemaphoreType.DMA((n, 6)), pltpu.SemaphoreType.DMA((n, 6))],
        compiler_params=pltpu.CompilerParams(has_side_effects=True),
    )(*bufs)


def _exchange(arrs, axes, name):
    n = len(arrs)

    def body(*refs):
        ins, outs = refs[:n], refs[n:2 * n]
        ssem, rsem = refs[2 * n:]
        x, y, c = _me()
        cps = []
        for a in range(n):
            my = {"x": x, "y": y, "c": c}[axes[a]]
            partner = {"x": (1 - x, y, c), "y": (x, 1 - y, c), "c": (x, y, 1 - c)}[axes[a]]
            cps.append(pltpu.make_async_remote_copy(
                src_ref=ins[a].at[:, 1 - my], dst_ref=outs[a], send_sem=ssem.at[a], recv_sem=rsem.at[a],
                device_id=partner, device_id_type=MESH_ID))
            cps[-1].start()
        for cp in cps:
            cp.wait()

    return pl.pallas_call(
        body, name=name,
        in_specs=[ANY] * n, out_specs=[ANY] * n,
        out_shape=[jax.ShapeDtypeStruct((a.shape[0],) + a.shape[2:], a.dtype) for a in arrs],
        scratch_shapes=[pltpu.SemaphoreType.DMA((n,)), pltpu.SemaphoreType.DMA((n,))],
        compiler_params=pltpu.CompilerParams(has_side_effects=True),
    )(*arrs)


def _add_kept(arr, got, idx, name, with_bf16):
    P, _, R, N = arr.shape
    tr = _tile(R, max(16, (1 << 20) // N))

    def body(idx_ref, a_ref, g_ref, o_ref, *ob_ref):
        s = a_ref[...] + g_ref[...].astype(F32)
        o_ref[...] = s
        if with_bf16:
            ob_ref[0][...] = s.astype(BF16)

    out_blk = pl.BlockSpec((None, tr, N), lambda p, i, idx: (p, i, 0))
    return pl.pallas_call(
        body, name=name,
        grid_spec=pltpu.PrefetchScalarGridSpec(
            num_scalar_prefetch=1, grid=(P, R // tr),
            in_specs=[pl.BlockSpec((None, None, tr, N), lambda p, i, idx: (p, idx[0], i, 0)),
                      pl.BlockSpec((None, tr, N), lambda p, i, idx: (p, i, 0))],
            out_specs=[out_blk] * (2 if with_bf16 else 1)),
        out_shape=[jax.ShapeDtypeStruct((P, R, N), F32)] + ([jax.ShapeDtypeStruct((P, R, N), BF16)] if with_bf16 else []),
        compiler_params=_cp(("parallel", "parallel")),
    )(idx, arr, got)


def _share_halves(halves):
    n = len(halves)

    def body(*refs):
        ins, outs = refs[:n], refs[n:2 * n]
        ssem, rsem = refs[2 * n:]
        x, y, c = _me()
        cps = []
        for a in range(n):
            cps.append(pltpu.make_async_remote_copy(
                src_ref=ins[a], dst_ref=outs[a], send_sem=ssem.at[a], recv_sem=rsem.at[a],
                device_id=(x, y, 1 - c), device_id_type=MESH_ID))
            cps[-1].start()
        for cp in cps:
            cp.wait()

    return pl.pallas_call(
        body, name="share_halves",
        in_specs=[ANY] * n, out_specs=[ANY] * n,
        out_shape=[jax.ShapeDtypeStruct(h.shape, h.dtype) for h in halves],
        scratch_shapes=[pltpu.SemaphoreType.DMA((n,)), pltpu.SemaphoreType.DMA((n,))],
        compiler_params=pltpu.CompilerParams(has_side_effects=True),
    )(*halves)


def _reduce_scatter(grads, first_axes):
    x, y, c = _me()
    idx = lambda v: jnp.reshape(v, (1,)).astype(jnp.int32)
    coord = {"x": idx(x), "y": idx(y), "c": idx(c)}
    n = len(grads)
    second = ["y" if f == "x" else "x" for f in first_axes]
    views = [g.reshape(N_SHARD, 2, g.shape[1] // 2, g.shape[2]) for g in grads]
    got = _exchange(views, "c" * n, "rs_exchange_c")
    summed = [_add_kept(v, r, coord["c"], "rs_add_c", True) for v, r in zip(views, got)]

    def split_view(a, ax):
        _, rh, nn = a.shape
        return a.reshape(1, 2, 2 * rh, nn) if ax == "x" else a.reshape(2, 2, rh, nn)

    got = _exchange([split_view(s[1], f) for s, f in zip(summed, first_axes)], first_axes, "rs_exchange_1")
    summed = [_add_kept(split_view(s[0], f), r, coord[f], "rs_add_1", True)
              for s, r, f in zip(summed, got, first_axes)]
    views32 = [s[0].reshape(1, 2, -1, s[0].shape[-1]) for s in summed]
    views16 = [s[1].reshape(1, 2, -1, s[1].shape[-1]) for s in summed]
    got = _exchange(views16, second, "rs_exchange_2")
    kept = [_add_kept(v, r, coord[g], "rs_add_2", False)[0][0] for v, r, g in zip(views32, got, second)]
    return kept, _share_halves(kept)


def _allgather_small(p):
    R, D = p.shape

    def body(p_ref, o_ref, ssem, rsem):
        x, y, c = _me()
        me = 4 * x + 2 * y + c
        o_ref[me] = p_ref[...]
        cps = []
        for m in range(1, 8):
            mx, my, mc = (m >> 2) & 1, (m >> 1) & 1, m & 1
            peer = (1 - x if mx else x, 1 - y if my else y, 1 - c if mc else c)
            cps.append(pltpu.make_async_remote_copy(
                src_ref=p_ref, dst_ref=o_ref.at[me], send_sem=ssem.at[m - 1], recv_sem=rsem.at[m - 1],
                device_id=peer, device_id_type=MESH_ID))
            cps[-1].start()
        for cp in cps:
            cp.wait()

    return pl.pallas_call(
        body, name="allgather_small",
        in_specs=[pl.BlockSpec(memory_space=pltpu.VMEM)],
        out_specs=pl.BlockSpec(memory_space=pltpu.VMEM),
        out_shape=jax.ShapeDtypeStruct((8, R, D), p.dtype),
        scratch_shapes=[pltpu.SemaphoreType.DMA((7,)), pltpu.SemaphoreType.DMA((7,))],
        compiler_params=pltpu.CompilerParams(has_side_effects=True, vmem_limit_bytes=VMEM_LIMIT),
    )(p)


def _adam_math(w, g, m, v):
    m = ADAM_B1 * m + (1.0 - ADAM_B1) * g
    v = ADAM_B2 * v + (1.0 - ADAM_B2) * (g * g)
    m_hat = m / (1.0 - ADAM_B1 ** ADAM_STEP)
    v_hat = v / (1.0 - ADAM_B2 ** ADAM_STEP)
    delta = -ADAM_LR * (m_hat / (jnp.sqrt(v_hat) + ADAM_EPS) + ADAM_WD * w)
    return delta, m, v


def _adam(w, g_mine, g_sib, m, v, core):
    R, N = w.shape
    rh = R // 2
    tr = _tile(rh, max(16, (1 << 19) // N))
    nt = rh // tr

    def body(c_ref, w_ref, gm_ref, gs_ref, m_ref, v_ref, g_ref, d_ref, nm_ref, nv_ref):
        g = jnp.where(pl.program_id(0) == c_ref[0], gm_ref[...], gs_ref[...])
        d, nm, nv = _adam_math(w_ref[...], g, m_ref[...], v_ref[...])
        g_ref[...] = g
        d_ref[...] = d
        nm_ref[...] = nm
        nv_ref[...] = nv

    blk = pl.BlockSpec((tr, N), lambda h, i, c: (h * nt + i, 0))
    half = pl.BlockSpec((tr, N), lambda h, i, c: (i, 0))
    return pl.pallas_call(
        body, name="adamw",
        grid_spec=pltpu.PrefetchScalarGridSpec(
            num_scalar_prefetch=1, grid=(2, nt),
            in_specs=[blk, half, half, blk, blk], out_specs=[blk] * 4),
        out_shape=[jax.ShapeDtypeStruct((R, N), F32)] * 4,
        compiler_params=_cp(("parallel", "parallel")),
    )(core, w, g_mine, g_sib, m, v)


def _reduce_rows(parts, sizes, rows_out):
    D = parts.shape[1]

    def body(p_ref, o_ref):
        o_ref[...] = jnp.zeros((rows_out, D), F32)
        off = 0
        for i, sz in enumerate(sizes):
            o_ref[i:i + 1, :] = jnp.sum(p_ref[off:off + sz, :], axis=0, keepdims=True)
            off += sz

    return pl.pallas_call(
        body, name="reduce_rows",
        out_shape=jax.ShapeDtypeStruct((rows_out, D), F32),
        compiler_params=pltpu.CompilerParams(vmem_limit_bytes=VMEM_LIMIT),
    )(parts)


def _sum_devices(g8):
    _, R, D = g8.shape

    def body(g_ref, o_ref):
        tot = g_ref[0]
        for k in range(1, 8):
            tot = tot + g_ref[k]
        o_ref[...] = tot

    return pl.pallas_call(
        body, name="sum_devices",
        out_shape=jax.ShapeDtypeStruct((R, D), F32),
        compiler_params=pltpu.CompilerParams(vmem_limit_bytes=VMEM_LIMIT),
    )(g8)


def _small_update(gathered, w, m, v, L):
    _, R, D = gathered.shape

    def body(g8_ref, w_ref, m_ref, v_ref, g_ref, d_ref, nm_ref, nv_ref):
        tot = g8_ref[0]
        for k in range(1, 8):
            tot = tot + g8_ref[k]
        g_ref[...] = tot
        sm = _softmax_rows(w_ref, L)
        run = sm[0]
        dcum = []
        for l in range(L):
            if l > 0:
                run = run + sm[l]
            cum = run - sm[0]
            dcum.append(jnp.where((cum > 0.0) & (cum < 1.0), g_ref[l:l + 1, :], 0.0))
        dsm = [jnp.zeros((1, D), F32)]
        for i in range(1, L):
            dsm.append(functools.reduce(lambda p, q: p + q, dcum[i:]))
        dot = functools.reduce(lambda p, q: p + q, [s * d for s, d in zip(sm, dsm)])
        for l in range(L):
            g_ref[l:l + 1, :] = sm[l] * (dsm[l] - dot)
        lam = w_ref[5 * L:6 * L, :]
        g_ref[5 * L:6 * L, :] = g_ref[5 * L:6 * L, :] * (-_sig(-lam))
        d, nm, nv = _adam_math(w_ref[...], g_ref[...], m_ref[...], v_ref[...])
        d_ref[...] = d
        nm_ref[...] = nm
        nv_ref[...] = nv

    return pl.pallas_call(
        body, name="small_update",
        out_shape=[jax.ShapeDtypeStruct((R, D), F32)] * 4,
        compiler_params=pltpu.CompilerParams(vmem_limit_bytes=VMEM_LIMIT),
    )(gathered, w, m, v)


def kernel(x, lb_logits, norm_mix, w_in, conv_w, conv_b, w_r, b_r, w_i, b_i, lam, hg_norm, w_out, norm_mlp, w_up, w_down, norm_final, loss_target, m_lb_logits, m_norm_mix, m_w_in, m_conv_w, m_conv_b, m_w_r, m_b_r, m_w_i, m_b_i, m_lam, m_hg_norm, m_w_out, m_norm_mlp, m_w_up, m_w_down, m_norm_final, v_lb_logits, v_norm_mix, v_w_in, v_conv_w, v_conv_b, v_w_r, v_b_r, v_w_i, v_b_i, v_lam, v_hg_norm, v_w_out, v_norm_mlp, v_w_up, v_w_down, v_norm_final):
    B, S, D = x.shape
    L = norm_mix.shape[0]
    nb = D // RG_BLOCK
    Dq = D // N_SHARD
    mx, my, mc = _me()
    shard = 2 * mx + my

    big_w = (w_in, w_out, w_up, w_down, w_r, w_i)
    flat2 = lambda a: a.reshape(-1, a.shape[-1])
    slot = jnp.reshape(shard, (1,)).astype(jnp.int32)
    g_in, g_out, g_up, g_down, g_r, g_i = _gather_weights([_cast_place(flat2(w), slot) for w in big_w], LINK_SPLIT)
    w_in_st = g_in.reshape(N_SHARD, L, D, 2 * D)
    w_out_st = g_out.reshape(N_SHARD, L, Dq, D)
    w_up_st = g_up.reshape((N_SHARD,) + w_up.shape)
    w_down_st = g_down.reshape((N_SHARD,) + w_down.shape)
    unshard_gate = lambda g: g.reshape(N_SHARD, L, nb, RG_BLOCK // N_SHARD, RG_BLOCK).transpose(1, 2, 0, 3, 4).reshape(
        L, nb, RG_BLOCK, RG_BLOCK)
    w_r_full, w_i_full = unshard_gate(g_r), unshard_gate(g_i)

    R_LB, R_NMIX, R_CB, R_BR, R_BI, R_LAM, R_NMLP, R_GN, R_CW, R_NF, R_LOSS = (
        0, L, 2 * L, 3 * L, 4 * L, 5 * L, 6 * L, 7 * L, 8 * L, 12 * L, 12 * L + 1)
    n_rows = 12 * L + 2
    rows_pad = n_rows + (-n_rows) % SUBLANES

    def place_cols(a):
        return lax.dynamic_update_slice(jnp.zeros((a.shape[0], D), F32), a, (0, shard * Dq))

    def pack_small(lb_, nmix_, cb_, br_, bi_, lam_, nmlp_, gn_, cw_, nf_):
        gn_pad = jnp.pad(gn_, ((0, 0), (0, D - HEAD)))
        rows = [lb_, nmix_, cb_, br_, bi_, lam_, nmlp_, gn_pad, place_cols(cw_.reshape(L * CONV_TAPS, Dq)),
                nf_[None, :], jnp.zeros((rows_pad - n_rows + 1, D), F32)]
        return jnp.concatenate(rows, axis=0)

    w_small = pack_small(lb_logits, norm_mix, conv_b, b_r, b_i, lam, norm_mlp, hg_norm, conv_w, norm_final)
    m_small = pack_small(m_lb_logits, m_norm_mix, m_conv_b, m_b_r, m_b_i, m_lam, m_norm_mlp, m_hg_norm, m_conv_w,
                         m_norm_final)
    v_small = pack_small(v_lb_logits, v_norm_mix, v_conv_b, v_b_r, v_b_i, v_lam, v_norm_mlp, v_hg_norm, v_conv_w,
                         v_norm_final)
    cw_rows = place_cols(conv_w.reshape(L * CONV_TAPS, Dq)) * jnp.where(mc == 0, 1.0, 0.0)
    conv_w_full = _sum_devices(_allgather_small(cw_rows)).reshape(L, CONV_TAPS, D)

    lowb, sp = _prep(lb_logits, lam)

    loss_parts, grad_x, big, small, g_nf = _local_step(
        x, loss_target, lowb, sp, norm_mix, w_in_st, conv_w_full, conv_b, w_r_full, b_r, w_i_full, b_i, hg_norm,
        w_out_st, norm_mlp, w_up_st, w_down_st, norm_final)

    shard_gate = lambda g: g.reshape(L, nb, N_SHARD, RG_BLOCK // N_SHARD, RG_BLOCK).transpose(2, 0, 1, 3, 4)
    stacked = list(big[:4]) + [shard_gate(big[4]), shard_gate(big[5])]
    mine, sibs = _reduce_scatter([s.reshape(N_SHARD, -1, s.shape[-1]) for s in stacked], LINK_SPLIT)
    core = jnp.reshape(mc, (1,)).astype(jnp.int32)
    outs = {}
    for name, w, m, v, g_mine, g_sib in zip(("w_in", "w_out", "w_up", "w_down", "w_r", "w_i"), big_w,
                                            (m_w_in, m_w_out, m_w_up, m_w_down, m_w_r, m_w_i),
                                            (v_w_in, v_w_out, v_w_up, v_w_down, v_w_r, v_w_i), mine, sibs):
        outs[name] = tuple(t.reshape(w.shape) for t in _adam(flat2(w), g_mine, g_sib, flat2(m), flat2(v), core))

    parts, sizes = [], []

    def add_rows(a):
        a = a.reshape(-1, a.shape[-1])
        if a.shape[1] != D:
            a = jnp.pad(a, ((0, 0), (0, D - a.shape[1])))
        parts.append(a)
        sizes.append(a.shape[0])

    for i in range(8):
        for l in range(L):
            add_rows(small[l][i])
    for l in range(L):
        for j in range(CONV_TAPS):
            add_rows(small[l][8][j])
    add_rows(g_nf)
    loss_rows = loss_parts[:, 0:1, :]
    add_rows(jnp.where(lax.broadcasted_iota(jnp.int32, loss_rows.shape, 2) == 0, loss_rows, 0.0))
    g_small = _reduce_rows(jnp.concatenate(parts, axis=0), sizes, rows_pad)
    g_small, d_small, nm_small, nv_small = _small_update(_allgather_small(g_small), w_small, m_small, v_small, L)

    def unpack(t):
        take_cols = lambda a: lax.dynamic_slice(a, (0, shard * Dq), (a.shape[0], Dq))
        return {"lb_logits": t[R_LB:R_LB + L], "norm_mix": t[R_NMIX:R_NMIX + L], "conv_b": t[R_CB:R_CB + L],
                "b_r": t[R_BR:R_BR + L], "b_i": t[R_BI:R_BI + L], "lam": t[R_LAM:R_LAM + L],
                "norm_mlp": t[R_NMLP:R_NMLP + L], "hg_norm": t[R_GN:R_GN + L, :HEAD],
                "conv_w": take_cols(t[R_CW:R_CW + L * CONV_TAPS]).reshape(L, CONV_TAPS, Dq), "norm_final": t[R_NF]}

    small_out = [unpack(t) for t in (g_small, d_small, nm_small, nv_small)]
    loss = g_small[R_LOSS, 0]
    names = ("lb_logits", "norm_mix", "w_in", "conv_w", "conv_b", "w_r", "b_r", "w_i", "b_i", "lam", "hg_norm",
             "w_out", "norm_mlp", "w_up", "w_down", "norm_final")
    result = [loss, grad_x]
    for kind in range(4):
        for nme in names:
            result.append(outs[nme][kind] if nme in outs else small_out[kind][nme])
    return tuple(result)
```

```python
import functools
import math

import jax
import jax.numpy as jnp
from jax import lax
from jax.experimental import pallas as pl
from jax.experimental.pallas import tpu as pltpu

F32 = jnp.float32
BF16 = jnp.bfloat16

HEAD = 128
RG_BLOCK = 256
CONV_TAPS = 4
RG_C = 8.0
F_MIN = 1e-30
NORM_EPS = 1e-6
N_SEG = 8
N_SHARD = 4
HG_CHUNK = 256
RG_TILE = 256
ADAM_LR, ADAM_B1, ADAM_B2, ADAM_EPS, ADAM_WD, ADAM_STEP = 0.001, 0.9, 0.999, 1e-08, 0.01, 10
V7X_VMEM_BYTES = 64 * 1024 * 1024
VMEM_LIMIT = V7X_VMEM_BYTES - 8 * 1024 * 1024
SUBLANES = 8
LINK_SPLIT = "xxyyyy"
GATHER_STEPS = (0.0, 0.6, 0.88, 1.0)
MESH_ID = pl.DeviceIdType.MESH
ANY = pl.BlockSpec(memory_space=pl.ANY)


def _cp(sem):
    return pltpu.CompilerParams(dimension_semantics=sem, vmem_limit_bytes=VMEM_LIMIT)


def _dot(a, b):
    return jnp.dot(a, b, preferred_element_type=F32)


def _dot_nt(a, b):
    return lax.dot_general(a, b, (((1,), (1,)), ((), ())), preferred_element_type=F32)


def _dot_tn(a, b):
    return lax.dot_general(a, b, (((0,), (0,)), ((), ())), preferred_element_type=F32)


def _dot_01(m01, x):
    n = x.shape[1]
    hi = x.astype(BF16)
    r1 = x - hi.astype(F32)
    mid = r1.astype(BF16)
    lo = (r1 - mid.astype(F32)).astype(BF16)
    y = _dot(m01, jnp.concatenate([hi, mid, lo], axis=1))
    return y[:, :n] + y[:, n:2 * n] + y[:, 2 * n:]


def _sig(x):
    return jax.nn.sigmoid(x)


def _rows8(x):
    return x.reshape(x.shape[0] // SUBLANES, SUBLANES, x.shape[1]).sum(axis=0)


def _tile(n, cap):
    if n <= cap:
        return n
    t = cap - cap % 16
    while n % t:
        t -= 16
    return t


_GELU_C = math.sqrt(2.0 / math.pi)


def _gelu_and_grad(x):
    x2 = x * x
    t = jnp.tanh(_GELU_C * (x + 0.044715 * x * x2))
    g = 0.5 * x * (1.0 + t)
    dg = 0.5 * (1.0 + t) + 0.5 * x * (1.0 - t * t) * (_GELU_C * (1.0 + 3.0 * 0.044715 * x2))
    return g, dg


def _rms(x):
    return lax.rsqrt(jnp.mean(x * x, axis=-1, keepdims=True) + NORM_EPS)


def _rms_bwd(dh, x, rs, gain):
    xhat = x * rs
    dxhat = dh * gain
    dx = rs * (dxhat - xhat * jnp.mean(dxhat * xhat, axis=-1, keepdims=True))
    return dx, _rows8(dh * xhat)


def _inproj_fwd(x2, gain, w_st, layer):
    T, D = x2.shape
    tm = _tile(T, 2048)

    def body(x_ref, g_ref, w_ref, o_ref, h_ref):
        @pl.when(pl.program_id(1) == 0)
        def _():
            x = x_ref[...]
            h_ref[...] = (x * _rms(x) * g_ref[...]).astype(BF16)
        o_ref[...] = _dot(h_ref[...], w_ref[...])

    return pl.pallas_call(
        body, name="inproj_fwd",
        grid=(T // tm, N_SEG),
        in_specs=[pl.BlockSpec((tm, D), lambda i, k: (i, 0)),
                  pl.BlockSpec((1, D), lambda i, k: (0, 0)),
                  pl.BlockSpec((None, None, D, D), lambda i, k: (k // 2, layer, 0, k % 2))],
        out_specs=[pl.BlockSpec((tm, D), lambda i, k: (i, k)),
                   pl.BlockSpec((tm, D), lambda i, k: (i, 0))],
        out_shape=[jax.ShapeDtypeStruct((T, N_SEG * D), F32), jax.ShapeDtypeStruct((T, D), BF16)],
        compiler_params=_cp(("parallel", "arbitrary")),
    )(x2, gain, w_st)


def _rg_gates(xc, wr_ref, br, wi_ref, bi, sp):
    D = xc.shape[1]
    xcb = xc.astype(BF16)
    pr, pi = [], []
    for n in range(D // RG_BLOCK):
        blk = xcb[:, n * RG_BLOCK:(n + 1) * RG_BLOCK]
        pr.append(_dot(blk, wr_ref[n]))
        pi.append(_dot(blk, wi_ref[n]))
    r = _sig(jnp.concatenate(pr, axis=1) + br) if len(pr) > 1 else _sig(pr[0] + br)
    i = _sig(jnp.concatenate(pi, axis=1) + bi) if len(pi) > 1 else _sig(pi[0] + bi)
    la = (-RG_C) * r * sp
    a = jnp.exp(la)
    y = 2.0 * la
    one_m_e2 = jnp.where(y > -1e-2, -(y * (1.0 + 0.5 * y * (1.0 + y * (1.0 / 3.0)))), 1.0 - jnp.exp(y))
    mult = jnp.sqrt(jnp.maximum(one_m_e2, 0.0))
    return r, i, a, mult


def _conv_taps(xbuf, cw_ref, ts):
    acc = None
    for j in range(CONV_TAPS):
        term = cw_ref[j:j + 1, :] * xbuf[pl.ds(SUBLANES - (CONV_TAPS - 1) + j, ts), :]
        acc = term if acc is None else acc + term
    return acc


def _rg_fwd(proj, B, cw, cb, wr, br, wi, bi, sp):
    T = proj.shape[0]
    D = proj.shape[1] // N_SEG
    S = T // B
    ts = _tile(S, RG_TILE)
    nts = S // ts
    nb = D // RG_BLOCK

    def body(xa_ref, ga_ref, cw_ref, cb_ref, wr_ref, br_ref, wi_ref, bi_ref, sp_ref,
             ya_ref, h_ref, xbuf, a_scr, u_scr, carry):
        @pl.when(pl.program_id(1) == 0)
        def _():
            xbuf[0:SUBLANES, :] = jnp.zeros((SUBLANES, D), F32)
            carry[...] = jnp.zeros((SUBLANES, D), F32)

        xbuf[pl.ds(SUBLANES, ts), :] = xa_ref[...]
        xc = _conv_taps(xbuf, cw_ref, ts) + cb_ref[...]
        r, i, a, mult = _rg_gates(xc, wr_ref, br_ref[...], wi_ref, bi_ref[...], sp_ref[...])
        a_scr[...] = a
        u_scr[...] = mult * (i * xc)
        row8 = lax.broadcasted_iota(jnp.int32, (SUBLANES, 1), 0)

        def blk(n, hprev):
            off = pl.multiple_of(n * SUBLANES, SUBLANES)
            a8 = a_scr[pl.ds(off, SUBLANES), :]
            u8 = u_scr[pl.ds(off, SUBLANES), :]
            for d in (1, 2, 4):
                m = row8 >= d
                ap = jnp.where(m, pltpu.roll(a8, d, 0), 1.0)
                up = jnp.where(m, pltpu.roll(u8, d, 0), 0.0)
                u8 = a8 * up + u8
                a8 = a8 * ap
            h8 = u8 + a8 * hprev
            u_scr[pl.ds(off, SUBLANES), :] = h8
            last = jnp.sum(jnp.where(row8 == SUBLANES - 1, h8, 0.0), axis=0, keepdims=True)
            return jnp.broadcast_to(last, (SUBLANES, D))

        carry[...] = lax.fori_loop(0, ts // SUBLANES, blk, carry[...])
        h = u_scr[...]
        h_ref[...] = h
        g, _ = _gelu_and_grad(ga_ref[...])
        ya_ref[...] = h * g
        xbuf[0:SUBLANES, :] = xa_ref[pl.ds(ts - SUBLANES, SUBLANES), :]

    vec = pl.BlockSpec((1, D), lambda b, j: (0, 0))
    gate = pl.BlockSpec((nb, RG_BLOCK, RG_BLOCK), lambda b, j: (0, 0, 0))
    return pl.pallas_call(
        body, name="rg_fwd",
        grid=(B, nts),
        in_specs=[pl.BlockSpec((ts, D), lambda b, j: (b * nts + j, 0)),
                  pl.BlockSpec((ts, D), lambda b, j: (b * nts + j, 1)),
                  pl.BlockSpec((CONV_TAPS, D), lambda b, j: (0, 0)), vec, gate, vec, gate, vec, vec],
        out_specs=[pl.BlockSpec((ts, D), lambda b, j: (b * nts + j, 0))] * 2,
        out_shape=[jax.ShapeDtypeStruct((T, D), F32)] * 2,
        scratch_shapes=[pltpu.VMEM((SUBLANES + ts, D), F32), pltpu.VMEM((ts, D), F32),
                        pltpu.VMEM((ts, D), F32), pltpu.VMEM((SUBLANES, D), F32)],
        compiler_params=_cp(("arbitrary", "arbitrary")),
    )(proj, proj, cw, cb, wr, br, wi, bi, sp)


def _hg_gates(q, z, lb):
    sig = _sig(z)
    one_m = 1.0 - lb
    fg = lb + one_m * sig
    lf = jnp.log(jnp.maximum(fg, F_MIN))
    kf = one_m * (1.0 - sig)
    qs = _sig(q)
    return q * qs, qs, kf, lf, fg, sig


def _hg_cum(lf, C):
    ri = lax.broadcasted_iota(jnp.int32, (C, C), 0)
    ci = lax.broadcasted_iota(jnp.int32, (C, C), 1)
    return _dot_01(jnp.where(ci <= ri, 1.0, 0.0).astype(BF16), lf)


def _hg_levels(lf, cum, C):
    row = lax.broadcasted_iota(jnp.int32, (C, 1), 0)
    levels = []
    w = C // 2
    while w >= 4:
        blk = 2 * w
        upper = (row & w) != 0
        ref = jnp.min(jnp.where(upper, 0.0, cum).reshape(C // blk, blk, HEAD), axis=1, keepdims=True)
        ref = jnp.broadcast_to(ref, (C // blk, blk, HEAD)).reshape(C, HEAD)
        d = cum - ref
        e = jnp.exp(jnp.where(upper, d, -d))
        levels.append((jnp.where(upper, e, 0.0), jnp.where(upper, 0.0, e), blk))
        w //= 2
    r4 = row & 3
    lf_prev = pltpu.roll(lf, 1, 0)
    lf_next = pltpu.roll(lf, C - 1, 0)
    eq = jnp.where(r4 >= 2, jnp.exp(jnp.where(r4 == 3, lf + lf_prev, lf)), 0.0)
    ek = jnp.where(r4 == 0, jnp.exp(lf_next), jnp.where(r4 == 1, 1.0, 0.0))
    levels.append((eq, ek, 4))
    odd = (row & 1) == 1
    levels.append((jnp.where(odd, jnp.exp(lf), 0.0), jnp.where(odd, 0.0, 1.0), 2))
    ones = jnp.ones_like(lf)
    levels.append((ones, ones, 1))
    return levels


def _same_block(C, blk):
    ri = lax.broadcasted_iota(jnp.int32, (C, C), 0)
    ci = lax.broadcasted_iota(jnp.int32, (C, C), 1)
    if blk == 1:
        return ri == ci
    shift = blk.bit_length() - 1
    return (ri >> shift) == (ci >> shift)


def _hg_mask_blocks(C):
    blks = []
    w = C // 4
    while w >= 4:
        blks.append(2 * w)
        w //= 2
    return blks + [4, 2, 1]


def _hg_fill_masks(mask_scr, C):
    for i, blk in enumerate(_hg_mask_blocks(C)):
        mask_scr[i] = jnp.where(_same_block(C, blk), 1.0, 0.0).astype(F32)


def _hg_scores(qf, kf, levels, mask_scr):
    A = None
    for n, (eq, ek, _) in enumerate(levels):
        a = _dot_nt((qf * eq).astype(BF16), (kf * ek).astype(BF16))
        if n > 0:
            a = a * mask_scr[n - 1]
        A = a if A is None else A + a
    return A


def _hg_specs(B, NC, D, C, dtype_blocks):
    def spec(col0, rev):
        if rev:
            return pl.BlockSpec((C, HEAD), lambda b, h, j: (b * NC + (NC - 1 - j), col0 + h))
        return pl.BlockSpec((C, HEAD), lambda b, h, j: (b * NC + j, col0 + h))
    return spec


def _hg_fwd(proj, B, lb, gn, carried=None):
    T = proj.shape[0]
    D = proj.shape[1] // N_SEG
    S = T // B
    C = min(HG_CHUNK, S)
    NC = S // C
    H = D // HEAD
    hpd = D // HEAD
    spec = _hg_specs(B, NC, D, C, None)
    bufs, phases, fractions, sems = carried if carried is not None else ([], [], [], [])
    n_car = len(bufs)
    last_step = B * H * NC - 1

    def body(q_ref, z_ref, v_ref, g_ref, lb_ref, gn_ref, *rest):
        yb_ref, o_ref, st_ref, a_ref, cum_ref = rest[n_car:n_car + 5]
        st_scr, mask_scr = rest[2 * n_car + 5:2 * n_car + 7]
        if n_car:
            step = (pl.program_id(0) * H + pl.program_id(1)) * NC + pl.program_id(2)
            for phase, frac in zip(phases, fractions):
                @pl.when(step == int(round(frac * last_step)))
                def _(phase=phase):
                    phase(rest[n_car + 5:2 * n_car + 5], *rest[2 * n_car + 7:])

        @pl.when(pl.program_id(2) == 0)
        def _():
            st_scr[...] = jnp.zeros((HEAD, HEAD), F32)
            _hg_fill_masks(mask_scr, C)

        s_t = st_scr[...]
        st_ref[...] = s_t
        qf, _, kf, lf, _, _ = _hg_gates(q_ref[...], z_ref[...], lb_ref[...])
        cum = _hg_cum(lf, C)
        cum_ref[...] = cum
        A = _hg_scores(qf, kf, _hg_levels(lf, cum, C), mask_scr).astype(BF16)
        a_ref[...] = A
        vb = v_ref[...].astype(BF16)
        o = _dot_nt((qf * jnp.exp(cum)).astype(BF16), s_t.astype(BF16)) + _dot(A, vb)
        last = jnp.sum(lf, axis=0, keepdims=True)
        kend = kf * jnp.exp(last - cum)
        st_scr[...] = jnp.exp(last) * s_t + _dot_tn(vb, kend.astype(BF16))
        o_ref[...] = o
        g = g_ref[...]
        yb_ref[...] = (o * _rms(o) * gn_ref[...]) * (g * _sig(g))

    return pl.pallas_call(
        body, name="hg_fwd",
        grid=(B, H, NC),
        in_specs=[spec(2 * hpd, False), spec(3 * hpd, False), spec(4 * hpd, False), spec(5 * hpd, False),
                  pl.BlockSpec((1, HEAD), lambda b, h, j: (0, h)),
                  pl.BlockSpec((1, HEAD), lambda b, h, j: (0, 0))] + [ANY] * n_car,
        out_specs=[spec(0, False), spec(0, False),
                   pl.BlockSpec((None, None, None, HEAD, HEAD), lambda b, h, j: (b, h, j, 0, 0)),
                   pl.BlockSpec((None, None, None, C, C), lambda b, h, j: (b, h, j, 0, 0)), spec(0, False)]
        + [ANY] * n_car,
        out_shape=[jax.ShapeDtypeStruct((T, D), F32), jax.ShapeDtypeStruct((T, D), F32),
                   jax.ShapeDtypeStruct((B, H, NC, HEAD, HEAD), F32),
                   jax.ShapeDtypeStruct((B, H, NC, C, C), BF16), jax.ShapeDtypeStruct((T, D), F32)]
        + [jax.ShapeDtypeStruct(b.shape, b.dtype) for b in bufs],
        input_output_aliases={6 + i: 5 + i for i in range(n_car)},
        scratch_shapes=[pltpu.VMEM((HEAD, HEAD), F32), pltpu.VMEM((len(_hg_mask_blocks(C)), C, C), F32)] + list(sems),
        compiler_params=_cp(("arbitrary", "arbitrary", "arbitrary") if n_car else ("parallel", "parallel", "arbitrary")),
    )(proj, proj, proj, proj, lb, gn, *bufs)


def _w_full(ref):
    s, r, c = ref.shape
    return ref[...].reshape(s * r, c)


def _out_fwd(ya, yb, proj, x2, w_st, layer):
    T, D = x2.shape
    tm = _tile(T, 512)

    def body(ya_ref, yb_ref, ma_ref, mb_ref, x_ref, w_ref, xm_ref, y_ref):
        y = (_sig(ma_ref[...]) * ya_ref[...] + _sig(mb_ref[...]) * yb_ref[...]).astype(BF16)
        y_ref[...] = y
        xm_ref[...] = x_ref[...] + _dot(y, _w_full(w_ref))

    row = pl.BlockSpec((tm, D), lambda i: (i, 0))
    return pl.pallas_call(
        body, name="out_fwd",
        grid=(T // tm,),
        in_specs=[row, row, pl.BlockSpec((tm, D), lambda i: (i, 6)), pl.BlockSpec((tm, D), lambda i: (i, 7)), row,
                  pl.BlockSpec((N_SHARD, None, D // N_SHARD, D), lambda i: (0, layer, 0, 0))],
        out_specs=[row, row],
        out_shape=[jax.ShapeDtypeStruct((T, D), F32), jax.ShapeDtypeStruct((T, D), BF16)],
        compiler_params=_cp(("parallel",)),
    )(ya, yb, proj, proj, x2, w_st)


def _mlp_fwd(xm, gain, wup_st, wdn_st, layer):
    T, D = xm.shape
    F4 = wup_st.shape[3]
    tm = _tile(T, 1024)

    def body(x_ref, g_ref, wu_ref, wd_ref, xo_ref, up_ref, h_ref):
        @pl.when(pl.program_id(1) == 0)
        def _():
            x = x_ref[...]
            h_ref[...] = (x * _rms(x) * g_ref[...]).astype(BF16)
            xo_ref[...] = x
        up = _dot(h_ref[...], wu_ref[...])
        up_ref[...] = up.astype(BF16)
        act = jnp.maximum(up, 0.0)
        xo_ref[...] += _dot((act * act).astype(BF16), wd_ref[...])

    row = pl.BlockSpec((tm, D), lambda i, s: (i, 0))
    return pl.pallas_call(
        body, name="mlp_fwd",
        grid=(T // tm, N_SHARD),
        in_specs=[row, pl.BlockSpec((1, D), lambda i, s: (0, 0)),
                  pl.BlockSpec((None, None, D, F4), lambda i, s: (s, layer, 0, 0)),
                  pl.BlockSpec((None, None, F4, D), lambda i, s: (s, layer, 0, 0))],
        out_specs=[row, pl.BlockSpec((tm, F4), lambda i, s: (i, s)), row],
        out_shape=[jax.ShapeDtypeStruct((T, D), F32), jax.ShapeDtypeStruct((T, N_SHARD * F4), BF16),
                   jax.ShapeDtypeStruct((T, D), BF16)],
        compiler_params=_cp(("parallel", "arbitrary")),
    )(xm, gain, wup_st, wdn_st)


def _final_loss(x2, gain, tgt):
    T, D = x2.shape
    tm = _tile(T, 512)
    nt = T // tm

    def body(x_ref, g_ref, t_ref, loss_ref, dx_ref, dg_ref):
        x = x_ref[...]
        rs = _rms(x)
        err = x * rs * g_ref[...] - t_ref[...]
        part = 0.5 * jnp.sum(jnp.sum(err * err, axis=-1, keepdims=True) * (1.0 / D), axis=0, keepdims=True)
        loss_ref[...] = jnp.broadcast_to(part, (SUBLANES, 128))
        dx, dg = _rms_bwd(err * (1.0 / D), x, rs, g_ref[...])
        dx_ref[...] = dx
        dg_ref[...] = dg

    row = pl.BlockSpec((tm, D), lambda i: (i, 0))
    return pl.pallas_call(
        body, name="final_loss",
        grid=(nt,),
        in_specs=[row, pl.BlockSpec((1, D), lambda i: (0, 0)), row],
        out_specs=[pl.BlockSpec((None, SUBLANES, 128), lambda i: (i, 0, 0)), row,
                   pl.BlockSpec((None, SUBLANES, D), lambda i: (i, 0, 0))],
        out_shape=[jax.ShapeDtypeStruct((nt, SUBLANES, 128), F32), jax.ShapeDtypeStruct((T, D), F32),
                   jax.ShapeDtypeStruct((nt, SUBLANES, D), F32)],
        compiler_params=_cp(("parallel",)),
    )(x2, gain, tgt)


def _mlp_bwd_x(dx, xm, up, gain, wup_st, wdn_st, layer):
    T, D = xm.shape
    F4 = wup_st.shape[3]
    tm = _tile(T, 1024)
    nt = T // tm

    def body(dx_ref, x_ref, up_ref, g_ref, wu_ref, wd_ref, dxm_ref, dup_ref, dg_ref, dxb):
        s = pl.program_id(1)

        @pl.when(s == 0)
        def _():
            dxb[...] = dx_ref[...].astype(BF16)
            dxm_ref[...] = jnp.zeros((tm, D), F32)

        d_act = _dot_nt(dxb[...], wd_ref[...])
        d_up = (d_act * (2.0 * jnp.maximum(up_ref[...].astype(F32), 0.0))).astype(BF16)
        dup_ref[...] = d_up
        dxm_ref[...] += _dot_nt(d_up, wu_ref[...])

        @pl.when(s == N_SHARD - 1)
        def _():
            x = x_ref[...]
            dxn, dg = _rms_bwd(dxm_ref[...], x, _rms(x), g_ref[...])
            dxm_ref[...] = dx_ref[...] + dxn
            dg_ref[...] = dg

    row = pl.BlockSpec((tm, D), lambda i, s: (i, 0))
    return pl.pallas_call(
        body, name="mlp_bwd_x",
        grid=(nt, N_SHARD),
        in_specs=[row, row, pl.BlockSpec((tm, F4), lambda i, s: (i, s)), pl.BlockSpec((1, D), lambda i, s: (0, 0)),
                  pl.BlockSpec((None, None, D, F4), lambda i, s: (s, layer, 0, 0)),
                  pl.BlockSpec((None, None, F4, D), lambda i, s: (s, layer, 0, 0))],
        out_specs=[row, pl.BlockSpec((tm, F4), lambda i, s: (i, s)),
                   pl.BlockSpec((None, SUBLANES, D), lambda i, s: (i, 0, 0)), row],
        out_shape=[jax.ShapeDtypeStruct((T, D), F32), jax.ShapeDtypeStruct((T, N_SHARD * F4), BF16),
                   jax.ShapeDtypeStruct((nt, SUBLANES, D), F32), jax.ShapeDtypeStruct((T, D), BF16)],
        compiler_params=_cp(("parallel", "arbitrary")),
    )(dx, xm, up, gain, wup_st, wdn_st)


def _layer_slot(bufs, shapes, n_layers):
    out_shape = [jax.ShapeDtypeStruct((N_SHARD, n_layers) + s, F32) for s in shapes]
    return out_shape, ([] if bufs is None else list(bufs))


def _mlp_bwd_w(up, dxb, h, dup, layer, n_layers, bufs):
    T, D = dxb.shape
    F4 = up.shape[1] // N_SHARD
    tk = _tile(T, 1024)
    out_shape, extra = _layer_slot(bufs, [(D, F4), (F4, D)], n_layers)

    def body(up_ref, dx_ref, h_ref, dup_ref, *rest):
        gu_ref, gd_ref = rest[-2:]

        @pl.when(pl.program_id(1) == 0)
        def _():
            gu_ref[...] = jnp.zeros((D, F4), F32)
            gd_ref[...] = jnp.zeros((F4, D), F32)
        act = jnp.maximum(up_ref[...], 0.0)
        gd_ref[...] += _dot_tn(act * act, dx_ref[...])
        gu_ref[...] += _dot_tn(h_ref[...], dup_ref[...])

    return pl.pallas_call(
        body, name="mlp_bwd_w",
        grid=(N_SHARD, T // tk),
        in_specs=[pl.BlockSpec((tk, F4), lambda s, t: (t, s)), pl.BlockSpec((tk, D), lambda s, t: (t, 0)),
                  pl.BlockSpec((tk, D), lambda s, t: (t, 0)), pl.BlockSpec((tk, F4), lambda s, t: (t, s))]
        + [ANY] * len(extra),
        out_specs=[pl.BlockSpec((None, None, D, F4), lambda s, t: (s, layer, 0, 0)),
                   pl.BlockSpec((None, None, F4, D), lambda s, t: (s, layer, 0, 0))],
        out_shape=out_shape,
        input_output_aliases={4 + i: i for i in range(len(extra))},
        compiler_params=_cp(("parallel", "arbitrary")),
    )(up, dxb, h, dup, *extra)


def _out_bwd_x(dxm, ya, yb, proj, w_st, layer):
    T, D = dxm.shape
    tm = _tile(T, 512)

    def body(dx_ref, ya_ref, yb_ref, ma_ref, mb_ref, w_ref, dya_ref, dyb_ref, dma_ref, dmb_ref):
        dy = _dot_nt(dx_ref[...].astype(BF16), _w_full(w_ref))
        sa = _sig(ma_ref[...])
        sb = _sig(mb_ref[...])
        dya_ref[...] = dy * sa
        dyb_ref[...] = dy * sb
        dma_ref[...] = (dy * ya_ref[...] * (sa * (1.0 - sa))).astype(BF16)
        dmb_ref[...] = (dy * yb_ref[...] * (sb * (1.0 - sb))).astype(BF16)

    row = pl.BlockSpec((tm, D), lambda i: (i, 0))
    return pl.pallas_call(
        body, name="out_bwd_x",
        grid=(T // tm,),
        in_specs=[row, row, row, pl.BlockSpec((tm, D), lambda i: (i, 6)), pl.BlockSpec((tm, D), lambda i: (i, 7)),
                  pl.BlockSpec((N_SHARD, None, D // N_SHARD, D), lambda i: (0, layer, 0, 0))],
        out_specs=[row] * 4,
        out_shape=[jax.ShapeDtypeStruct((T, D), F32)] * 2 + [jax.ShapeDtypeStruct((T, D), BF16)] * 2,
        compiler_params=_cp(("parallel",)),
    )(dxm, ya, yb, proj, proj, w_st)


def _out_bwd_w(ymix, dxm, layer, n_layers, bufs):
    T, D = dxm.shape
    tk = _tile(T, 1024)
    out_shape, extra = _layer_slot(bufs, [(D // N_SHARD, D)], n_layers)

    def body(y_ref, dx_ref, *rest):
        g_ref = rest[-1]

        @pl.when(pl.program_id(0) == 0)
        def _():
            g_ref[...] = jnp.zeros((N_SHARD, D // N_SHARD, D), F32)
        g = _dot_tn(y_ref[...], dx_ref[...].astype(BF16))
        g_ref[...] += g.reshape(N_SHARD, D // N_SHARD, D)

    row = pl.BlockSpec((tk, D), lambda t: (t, 0))
    return pl.pallas_call(
        body, name="out_bwd_w",
        grid=(T // tk,),
        in_specs=[row, row] + [ANY] * len(extra),
        out_specs=[pl.BlockSpec((N_SHARD, None, D // N_SHARD, D), lambda t: (0, layer, 0, 0))],
        out_shape=out_shape,
        input_output_aliases={2 + i: i for i in range(len(extra))},
        compiler_params=_cp(("arbitrary",)),
    )(ymix, dxm, *extra)[0]


def _rg_bwd(proj, hrg, dya, B, cw, cb, wr, br, wi, bi, sp):
    T = proj.shape[0]
    D = proj.shape[1] // N_SEG
    S = T // B
    ts = _tile(S, RG_TILE)
    nts = S // ts
    nb = D // RG_BLOCK
    t8 = ts // SUBLANES

    def body(xa_ref, xp_ref, ga_ref, h_ref, hp_ref, dya_ref, cw_ref, cb_ref, wr_ref, br_ref, wi_ref, bi_ref, sp_ref,
             dxa_ref, dga_ref, gwr_ref, gwi_ref, gcw_ref, gcb_ref, gbr_ref, gbi_ref, gsp_ref,
             xbuf, hbuf, abuf, dbuf, g_scr, c_scr, gcar):
        b = pl.program_id(0)
        j = pl.program_id(1)
        first_in_time = j == nts - 1

        @pl.when((b == 0) & (j == 0))
        def _():
            gwr_ref[...] = jnp.zeros((nb, RG_BLOCK, RG_BLOCK), F32)
            gwi_ref[...] = jnp.zeros((nb, RG_BLOCK, RG_BLOCK), F32)
            gcw_ref[...] = jnp.zeros((CONV_TAPS, SUBLANES, D), F32)
            for r in (gcb_ref, gbr_ref, gbi_ref, gsp_ref):
                r[...] = jnp.zeros((SUBLANES, D), F32)

        @pl.when(j == 0)
        def _():
            abuf[pl.ds(ts, SUBLANES), :] = jnp.zeros((SUBLANES, D), F32)
            dbuf[pl.ds(ts, SUBLANES), :] = jnp.zeros((SUBLANES, D), F32)
            gcar[...] = jnp.zeros((SUBLANES, D), F32)

        keep = jnp.where(first_in_time, 0.0, 1.0)
        xbuf[0:SUBLANES, :] = xp_ref[...] * keep
        xbuf[pl.ds(SUBLANES, ts), :] = xa_ref[...]
        hbuf[0:SUBLANES, :] = hp_ref[...] * keep
        hbuf[pl.ds(SUBLANES, ts), :] = h_ref[...]

        xc = _conv_taps(xbuf, cw_ref, ts) + cb_ref[...]
        sp = sp_ref[...]
        r, i, a, mult = _rg_gates(xc, wr_ref, br_ref[...], wi_ref, bi_ref[...], sp)
        g_gate, dg_gate = _gelu_and_grad(ga_ref[...])
        dya = dya_ref[...]
        dga_ref[...] = (dya * h_ref[...] * dg_gate).astype(BF16)

        abuf[0:ts, :] = a
        c_scr[...] = abuf[pl.ds(1, ts), :]
        g_scr[...] = dya * g_gate
        row8 = lax.broadcasted_iota(jnp.int32, (SUBLANES, 1), 0)

        def blk(n, gnext):
            off = pl.multiple_of((t8 - 1 - n) * SUBLANES, SUBLANES)
            c8 = c_scr[pl.ds(off, SUBLANES), :]
            d8 = g_scr[pl.ds(off, SUBLANES), :]
            for d in (1, 2, 4):
                m = row8 < SUBLANES - d
                cn = jnp.where(m, pltpu.roll(c8, SUBLANES - d, 0), 1.0)
                dn = jnp.where(m, pltpu.roll(d8, SUBLANES - d, 0), 0.0)
                d8 = d8 + c8 * dn
                c8 = c8 * cn
            g8 = d8 + c8 * gnext
            g_scr[pl.ds(off, SUBLANES), :] = g8
            first = jnp.sum(jnp.where(row8 == 0, g8, 0.0), axis=0, keepdims=True)
            return jnp.broadcast_to(first, (SUBLANES, D))

        gcar[...] = lax.fori_loop(0, t8, blk, gcar[...])
        abuf[pl.ds(ts, SUBLANES), :] = a[0:SUBLANES, :]

        g = g_scr[...]
        hprev = hbuf[pl.ds(SUBLANES - 1, ts), :]
        gx = i * xc
        e2 = a * a
        dla = g * hprev * a - jnp.where(mult > 0.0, g * gx * e2 / jnp.where(mult > 0.0, mult, 1.0), 0.0)
        dgx = g * mult
        dpr = (dla * ((-RG_C) * sp)) * (r * (1.0 - r))
        dpi = (dgx * xc) * (i * (1.0 - i))
        gsp_ref[...] += _rows8(dla * ((-RG_C) * r))
        gbr_ref[...] += _rows8(dpr)
        gbi_ref[...] += _rows8(dpi)
        dprb = dpr.astype(BF16)
        dpib = dpi.astype(BF16)
        xcb = xc.astype(BF16)
        back = []
        for n in range(nb):
            sl = slice(n * RG_BLOCK, (n + 1) * RG_BLOCK)
            back.append(_dot_nt(dprb[:, sl], wr_ref[n]) + _dot_nt(dpib[:, sl], wi_ref[n]))
            gwr_ref[n] += _dot_tn(xcb[:, sl], dprb[:, sl])
            gwi_ref[n] += _dot_tn(xcb[:, sl], dpib[:, sl])
        dxc = dgx * i + (jnp.concatenate(back, axis=1) if nb > 1 else back[0])
        gcb_ref[...] += _rows8(dxc)

        dbuf[0:ts, :] = dxc
        dxa = None
        for jtap in range(CONV_TAPS):
            term = cw_ref[jtap:jtap + 1, :] * dbuf[pl.ds(CONV_TAPS - 1 - jtap, ts), :]
            dxa = term if dxa is None else dxa + term
            gcw_ref[jtap] += _rows8(dxc * xbuf[pl.ds(SUBLANES - (CONV_TAPS - 1) + jtap, ts), :])
        dxa_ref[...] = dxa.astype(BF16)
        dbuf[pl.ds(ts, SUBLANES), :] = dxc[0:SUBLANES, :]

    def tile_map(col):
        return lambda b, j: (b * nts + (nts - 1 - j), col)

    def prev8_map(col):
        return lambda b, j: (jnp.maximum((b * nts + (nts - 1 - j)) * t8 - 1, 0), col)

    vec = pl.BlockSpec((1, D), lambda b, j: (0, 0))
    gate = pl.BlockSpec((nb, RG_BLOCK, RG_BLOCK), lambda b, j: (0, 0, 0))
    acc8 = pl.BlockSpec((SUBLANES, D), lambda b, j: (0, 0))
    return pl.pallas_call(
        body, name="rg_bwd",
        grid=(B, nts),
        in_specs=[pl.BlockSpec((ts, D), tile_map(0)), pl.BlockSpec((SUBLANES, D), prev8_map(0)),
                  pl.BlockSpec((ts, D), tile_map(1)),
                  pl.BlockSpec((ts, D), tile_map(0)), pl.BlockSpec((SUBLANES, D), prev8_map(0)),
                  pl.BlockSpec((ts, D), tile_map(0)),
                  pl.BlockSpec((CONV_TAPS, D), lambda b, j: (0, 0)), vec, gate, vec, gate, vec, vec],
        out_specs=[pl.BlockSpec((ts, D), tile_map(0)), pl.BlockSpec((ts, D), tile_map(0)), gate, gate,
                   pl.BlockSpec((CONV_TAPS, SUBLANES, D), lambda b, j: (0, 0, 0)), acc8, acc8, acc8, acc8],
        out_shape=[jax.ShapeDtypeStruct((T, D), BF16)] * 2
        + [jax.ShapeDtypeStruct((nb, RG_BLOCK, RG_BLOCK), F32)] * 2
        + [jax.ShapeDtypeStruct((CONV_TAPS, SUBLANES, D), F32)] + [jax.ShapeDtypeStruct((SUBLANES, D), F32)] * 4,
        scratch_shapes=[pltpu.VMEM((SUBLANES + ts, D), F32), pltpu.VMEM((SUBLANES + ts, D), F32),
                        pltpu.VMEM((ts + SUBLANES, D), F32), pltpu.VMEM((ts + SUBLANES, D), F32),
                        pltpu.VMEM((ts, D), F32), pltpu.VMEM((ts, D), F32), pltpu.VMEM((SUBLANES, D), F32)],
        compiler_params=_cp(("arbitrary", "arbitrary")),
    )(proj, proj, proj, hrg, hrg, dya, cw, cb, wr, br, wi, bi, sp)


def _hg_bwd(proj, o_sv, dyb, states, a_sv, cum_sv, B, lb, gn):
    T = proj.shape[0]
    D = proj.shape[1] // N_SEG
    S = T // B
    C = min(HG_CHUNK, S)
    NC = S // C
    H = D // HEAD
    hpd = D // HEAD
    spec = _hg_specs(B, NC, D, C, None)

    def body(q_ref, z_ref, v_ref, g_ref, o_ref, dyb_ref, st_ref, a_ref, cum_ref, lb_ref, gn_ref,
             dq_ref, dz_ref, dv_ref, dg_ref, glb_ref, ggn_ref, ds_scr, mask_scr):
        @pl.when(pl.program_id(2) == 0)
        def _():
            ds_scr[...] = jnp.zeros((HEAD, HEAD), F32)
            _hg_fill_masks(mask_scr, C)
            glb_ref[...] = jnp.zeros((SUBLANES, HEAD), F32)
            ggn_ref[...] = jnp.zeros((SUBLANES, HEAD), F32)

        q = q_ref[...]
        lb = lb_ref[...]
        gn = gn_ref[...]
        qf, qs, kf, lf, fg, sig = _hg_gates(q, z_ref[...], lb)
        cum = cum_ref[...]
        levels = _hg_levels(lf, cum, C)

        o = o_ref[...]
        g = g_ref[...]
        gs = _sig(g)
        rs = _rms(o)
        dyb = dyb_ref[...]
        don = dyb * (g * gs)
        dg_ref[...] = (dyb * (o * rs * gn) * (gs * (1.0 + g * (1.0 - gs)))).astype(BF16)
        ggn_ref[...] += _rows8(don * o * rs)
        dn = don * gn
        do = rs * (dn - o * (rs * rs) * jnp.mean(dn * o, axis=-1, keepdims=True))

        s_t = st_ref[...].astype(BF16)
        ds_t = ds_scr[...]
        ds_b = ds_t.astype(BF16)
        dob = do.astype(BF16)
        vb = v_ref[...].astype(BF16)
        ecum = jnp.exp(cum)
        last = jnp.sum(lf, axis=0, keepdims=True)
        eend = jnp.exp(last - cum)
        qhat = (qf * ecum).astype(BF16)
        kend = (kf * eend).astype(BF16)

        dA = _dot_nt(dob, vb)
        dq_inter = _dot(dob, s_t)
        dk_state = _dot(vb, ds_b)
        dqf = dq_inter * ecum
        dkf = dk_state * eend
        g_intra = None
        for n, (eq, ek, _) in enumerate(levels):
            qw = (qf * eq).astype(BF16)
            kw = (kf * ek).astype(BF16)
            dam = (dA * mask_scr[n - 1]).astype(BF16) if n > 0 else dA.astype(BF16)
            rq = _dot(dam, kw)
            rk = _dot_tn(dam, qw)
            dqf += rq * eq
            dkf += rk * ek
            gi = qw.astype(F32) * rq - kw.astype(F32) * rk
            g_intra = gi if g_intra is None else g_intra + gi
        dv_ref[...] = (_dot_tn(a_ref[...], dob) + _dot_nt(kend, ds_b)).astype(BF16)
        e_last = jnp.exp(last)
        ds_scr[...] = e_last * ds_t + _dot_tn(dob, qhat)

        ri = lax.broadcasted_iota(jnp.int32, (C, C), 0)
        ci = lax.broadcasted_iota(jnp.int32, (C, C), 1)
        y_state = kend.astype(F32) * dk_state
        dlf = (_dot_01(jnp.where(ci >= ri, 1.0, 0.0).astype(BF16), g_intra + qhat.astype(F32) * dq_inter - y_state)
               + jnp.sum(y_state, axis=0, keepdims=True)
               + jnp.sum(e_last * st_ref[...] * ds_t, axis=0, keepdims=True))
        dfg = jnp.where(fg > F_MIN, dlf / jnp.maximum(fg, F_MIN), 0.0)
        sneg = 1.0 - sig
        diff = dfg - dkf
        dz_ref[...] = ((1.0 - lb) * sig * sneg * diff).astype(BF16)
        glb_ref[...] += _rows8(sneg * diff)
        dq_ref[...] = (dqf * (qs * (1.0 + q * (1.0 - qs)))).astype(BF16)

    return pl.pallas_call(
        body, name="hg_bwd",
        grid=(B, H, NC),
        in_specs=[spec(2 * hpd, True), spec(3 * hpd, True), spec(4 * hpd, True), spec(5 * hpd, True),
                  spec(0, True), spec(0, True),
                  pl.BlockSpec((None, None, None, HEAD, HEAD), lambda b, h, j: (b, h, NC - 1 - j, 0, 0)),
                  pl.BlockSpec((None, None, None, C, C), lambda b, h, j: (b, h, NC - 1 - j, 0, 0)), spec(0, True),
                  pl.BlockSpec((1, HEAD), lambda b, h, j: (0, h)),
                  pl.BlockSpec((1, HEAD), lambda b, h, j: (0, 0))],
        out_specs=[spec(0, True)] * 4
        + [pl.BlockSpec((None, SUBLANES, HEAD), lambda b, h, j: (b, 0, h)),
           pl.BlockSpec((None, None, SUBLANES, HEAD), lambda b, h, j: (b, h, 0, 0))],
        out_shape=[jax.ShapeDtypeStruct((T, D), BF16)] * 4
        + [jax.ShapeDtypeStruct((B, SUBLANES, D), F32), jax.ShapeDtypeStruct((B, H, SUBLANES, HEAD), F32)],
        scratch_shapes=[pltpu.VMEM((HEAD, HEAD), F32), pltpu.VMEM((len(_hg_mask_blocks(C)), C, C), F32)],
        compiler_params=_cp(("parallel", "parallel", "arbitrary")),
    )(proj, proj, proj, proj, o_sv, dyb, states, a_sv, cum_sv, lb, gn)


def _inproj_bwd_x(dsegs, w_st, layer, x2, gain, dxm):
    T, D = x2.shape
    tm = _tile(T, 512)
    nt = T // tm

    def body(*refs):
        seg_refs = refs[:N_SEG]
        w_ref, x_ref, g_ref, dxm_ref, dx_ref, dg_ref = refs[N_SEG:]
        k = pl.program_id(1)

        @pl.when(k == 0)
        def _():
            dx_ref[...] = jnp.zeros((tm, D), F32)

        for kk in range(N_SEG):
            @pl.when(k == kk)
            def _(kk=kk):
                dx_ref[...] += _dot_nt(seg_refs[kk][...], w_ref[...])

        @pl.when(k == N_SEG - 1)
        def _():
            x = x_ref[...]
            dxn, dg = _rms_bwd(dx_ref[...], x, _rms(x), g_ref[...])
            dx_ref[...] = dxm_ref[...] + dxn
            dg_ref[...] = dg

    row = pl.BlockSpec((tm, D), lambda i, k: (i, 0))

    def seg_spec(kk):
        return pl.BlockSpec((tm, D), lambda i, k: (jnp.minimum(i + jnp.where(k > kk, 1, 0), nt - 1), 0))

    return pl.pallas_call(
        body, name="inproj_bwd_x",
        grid=(nt, N_SEG),
        in_specs=[seg_spec(kk) for kk in range(N_SEG)]
        + [pl.BlockSpec((None, None, D, D), lambda i, k: (k // 2, layer, 0, k % 2)), row,
           pl.BlockSpec((1, D), lambda i, k: (0, 0)), row],
        out_specs=[row, pl.BlockSpec((None, SUBLANES, D), lambda i, k: (i, 0, 0))],
        out_shape=[jax.ShapeDtypeStruct((T, D), F32), jax.ShapeDtypeStruct((nt, SUBLANES, D), F32)],
        compiler_params=_cp(("parallel", "arbitrary")),
    )(*dsegs, w_st, x2, gain, dxm)


def _inproj_bwd_w(h, dsegs, layer, n_layers, bufs):
    T, D = h.shape
    tk = _tile(T, 1024)
    out_shape, extra = _layer_slot(bufs, [(D, 2 * D)], n_layers)

    def body(*refs):
        h_ref = refs[0]
        seg_refs = refs[1:1 + N_SEG]
        g_ref = refs[-1]
        k = pl.program_id(0)

        @pl.when(pl.program_id(1) == 0)
        def _():
            g_ref[...] = jnp.zeros((D, D), F32)

        for kk in range(N_SEG):
            @pl.when(k == kk)
            def _(kk=kk):
                g_ref[...] += _dot_tn(h_ref[...], seg_refs[kk][...])

    def seg_spec(kk):
        return pl.BlockSpec((tk, D), lambda k, t: (jnp.where(k == kk, t, 0), 0))

    return pl.pallas_call(
        body, name="inproj_bwd_w",
        grid=(N_SEG, T // tk),
        in_specs=[pl.BlockSpec((tk, D), lambda k, t: (t, 0))] + [seg_spec(kk) for kk in range(N_SEG)]
        + [ANY] * len(extra),
        out_specs=[pl.BlockSpec((None, None, D, D), lambda k, t: (k // 2, layer, 0, k % 2))],
        out_shape=out_shape,
        input_output_aliases={1 + N_SEG + i: i for i in range(len(extra))},
        compiler_params=_cp(("parallel", "arbitrary")),
    )(h, *dsegs, *extra)[0]


def _softmax_rows(lg_ref, L):
    rows = [lg_ref[l:l + 1, :] for l in range(L)]
    mx = functools.reduce(jnp.maximum, rows)
    es = [jnp.exp(r - mx) for r in rows]
    den = functools.reduce(lambda p, q: p + q, es)
    return [e / den for e in es]


def _prep(lb_logits, lam):
    L, D = lb_logits.shape

    def body(lg_ref, lam_ref, lowb_ref, sp_ref):
        sm = _softmax_rows(lg_ref, L)
        run = sm[0]
        for l in range(L):
            if l > 0:
                run = run + sm[l]
            lowb_ref[l:l + 1, :] = jnp.clip(run - sm[0], 0.0, 1.0)
        y = -lam_ref[...]
        sp_ref[...] = jnp.maximum(y, 0.0) + jnp.log1p(jnp.exp(-jnp.abs(y)))

    return pl.pallas_call(
        body, name="prep_small",
        out_shape=[jax.ShapeDtypeStruct((L, D), F32)] * 2,
    )(lb_logits, lam)


def _local_step(x, tgt, lowb, sp, norm_mix, wbufs, conv_w, conv_b, w_r, b_r, w_i, b_i, hg_norm,
                norm_mlp, norm_final, carried=None):
    B, S, D = x.shape
    L = norm_mix.shape[0]
    T = B * S
    x2 = x.reshape(T, D)
    row = lambda a, l: a[l:l + 1]

    def weight_views(bufs):
        f4 = bufs[2].shape[-1]
        return (bufs[0].reshape(N_SHARD, L, D, 2 * D), bufs[1].reshape(N_SHARD, L, D // N_SHARD, D),
                bufs[2].reshape(N_SHARD, L, D, f4), bufs[3].reshape(N_SHARD, L, f4, D))

    w_in_st, w_out_st, w_up_st, w_down_st = weight_views(wbufs)
    saved = []
    for l in range(L):
        proj, h = _inproj_fwd(x2, row(norm_mix, l), w_in_st, l)
        ya, hrg = _rg_fwd(proj, B, conv_w[l], row(conv_b, l), w_r[l], row(b_r, l), w_i[l], row(b_i, l), row(sp, l))
        if l == 0 and carried is not None:
            yb, o, st, a_sv, cum_sv, *wbufs = _hg_fwd(proj, B, row(lowb, l), row(hg_norm, l),
                                                      (list(wbufs),) + tuple(carried))
            w_in_st, w_out_st, w_up_st, w_down_st = weight_views(wbufs)
        else:
            yb, o, st, a_sv, cum_sv = _hg_fwd(proj, B, row(lowb, l), row(hg_norm, l))
        xm, ymix = _out_fwd(ya, yb, proj, x2, w_out_st, l)
        xo, up, h2 = _mlp_fwd(xm, row(norm_mlp, l), w_up_st, w_down_st, l)
        saved.append((x2, proj, h, ya, hrg, yb, o, (st, a_sv, cum_sv), xm, ymix, up, h2))
        x2 = xo
    loss_parts, dx, g_nf = _final_loss(x2, norm_final[None, :], tgt.reshape(T, D))

    gates = []
    small = []
    g_in = g_out = g_mlp = None
    for l in reversed(range(L)):
        x_in, proj, h, ya, hrg, yb, o, st, xm, ymix, up, h2 = saved[l]
        dxm, dup, g_nmlp, dxb = _mlp_bwd_x(dx, xm, up, row(norm_mlp, l), w_up_st, w_down_st, l)
        g_mlp = _mlp_bwd_w(up, dxb, h2, dup, l, L, g_mlp)
        dya, dyb, dma, dmb = _out_bwd_x(dxm, ya, yb, proj, w_out_st, l)
        g_out = _out_bwd_w(ymix, dxm, l, L, None if g_out is None else [g_out])
        dxa, dga, g_wr, g_wi, g_cw, g_cb, g_br, g_bi, g_sp = _rg_bwd(
            proj, hrg, dya, B, conv_w[l], row(conv_b, l), w_r[l], row(b_r, l), w_i[l], row(b_i, l), row(sp, l))
        dq, dz, dv, dg, g_lb, g_gn = _hg_bwd(proj, o, dyb, *st, B, row(lowb, l), row(hg_norm, l))
        dsegs = (dxa, dga, dq, dz, dv, dg, dma, dmb)
        dx, g_nmix = _inproj_bwd_x(dsegs, w_in_st, l, x_in, row(norm_mix, l), dxm)
        g_in = _inproj_bwd_w(h, dsegs, l, L, None if g_in is None else [g_in])
        gates.append((g_wr, g_wi))
        small.append((g_lb, g_nmix, g_cb, g_br, g_bi, g_sp, g_nmlp, g_gn, g_cw))
    gates.reverse()
    small.reverse()
    big = (g_in, g_out, g_mlp[0], g_mlp[1], jnp.stack([g[0] for g in gates]), jnp.stack([g[1] for g in gates]))
    return loss_parts, dx.reshape(B, S, D), big, small, g_nf


def _me():
    return lax.axis_index("x"), lax.axis_index("y"), lax.axis_index("c")


def _cast_place(w, slot):
    R, N = w.shape
    tr = _tile(R, max(16, (1 << 20) // N))

    def body(slot_ref, w_ref, o_ref):
        o_ref[...] = w_ref[...].astype(BF16)

    return pl.pallas_call(
        body, name="cast_place",
        grid_spec=pltpu.PrefetchScalarGridSpec(
            num_scalar_prefetch=1, grid=(R // tr,),
            in_specs=[pl.BlockSpec((tr, N), lambda i, slot: (i, 0))],
            out_specs=pl.BlockSpec((None, tr, N), lambda i, slot: (slot[0], i, 0))),
        out_shape=jax.ShapeDtypeStruct((N_SHARD, R, N), BF16),
        compiler_params=_cp(("parallel",)),
    )(slot, w)


def _gather_weights(bufs, first_axes, l0, nl):
    n = len(bufs)
    phases = _gather_phases(n, first_axes, l0, nl)

    def body(*refs):
        outs = refs[n:2 * n]
        ssem, rsem = refs[2 * n:]
        for ph in phases:
            ph(outs, ssem, rsem)

    return pl.pallas_call(
        body, name="gather_weights",
        in_specs=[ANY] * n, out_specs=[ANY] * n,
        out_shape=[jax.ShapeDtypeStruct(b.shape, b.dtype) for b in bufs],
        input_output_aliases={a: a for a in range(n)},
        scratch_shapes=_gather_sems(n),
        compiler_params=pltpu.CompilerParams(has_side_effects=True),
    )(*bufs)


def _gather_sems(n):
    return [pltpu.SemaphoreType.DMA((n, 6)), pltpu.SemaphoreType.DMA((n, 6))]


def _gather_phases(n, first_axes, l0, nl):
    def ctx(outs, ssem, rsem):
        x, y, c = _me()

        def piece(a, flips, half):
            sx = 1 - x if flips[0] else x
            sy = 1 - y if flips[1] else y
            return outs[a].at[2 * sx + sy, pl.ds(l0[a], nl[a]), half]

        def rcopy(a, k, ref, dev):
            return pltpu.make_async_remote_copy(src_ref=ref, dst_ref=ref, send_sem=ssem.at[a, k], recv_sem=rsem.at[a, k],
                                                device_id=dev, device_id_type=MESH_ID)

        def route(a):
            fx = first_axes[a] == "x"
            f_dev = (1 - x, y, c) if fx else (x, 1 - y, c)
            g_dev = (x, 1 - y, c) if fx else (1 - x, y, c)
            return f_dev, g_dev, ((1, 0) if fx else (0, 1)), ((0, 1) if fx else (1, 0))

        return c, (x, y, 1 - c), piece, rcopy, route

    def own_halves(outs, ssem, rsem):
        c, sib, piece, rcopy, route = ctx(outs, ssem, rsem)
        for a in range(n):
            f_dev, g_dev, _, _ = route(a)
            own = piece(a, (0, 0), c)
            rcopy(a, 0, own, f_dev).start()
            rcopy(a, 1, own, g_dev).start()

    def pass_on_neighbours(outs, ssem, rsem):
        c, sib, piece, rcopy, route = ctx(outs, ssem, rsem)
        for a in range(n):
            f_dev, g_dev, f_flip, _ = route(a)
            got = piece(a, f_flip, c)
            rcopy(a, 0, got, f_dev).wait_recv()
            rcopy(a, 2, got, g_dev).start()
            rcopy(a, 3, got, sib).start()
        for a in range(n):
            _, g_dev, _, g_flip = route(a)
            got = piece(a, g_flip, c)
            rcopy(a, 1, got, g_dev).wait_recv()
            rcopy(a, 4, got, sib).start()

    def pass_on_diagonal(outs, ssem, rsem):
        c, sib, piece, rcopy, route = ctx(outs, ssem, rsem)
        for a in range(n):
            _, g_dev, _, _ = route(a)
            got = piece(a, (1, 1), c)
            rcopy(a, 2, got, g_dev).wait_recv()
            rcopy(a, 5, got, sib).start()

    def drain(outs, ssem, rsem):
        c, sib, piece, rcopy, route = ctx(outs, ssem, rsem)
        for a in range(n):
            f_dev, g_dev, f_flip, g_flip = route(a)
            for k, fl in ((3, f_flip), (4, g_flip), (5, (1, 1))):
                rcopy(a, k, piece(a, fl, 1 - c), sib).wait_recv()
            own = piece(a, (0, 0), c)
            for k, dev in ((0, f_dev), (1, g_dev), (2, g_dev), (3, sib), (4, sib), (5, sib)):
                rcopy(a, k, own, dev).wait_send()

    return [own_halves, pass_on_neighbours, pass_on_diagonal, drain]


def _exchange(arrs, axes, name):
    n = len(arrs)

    def body(*refs):
        ins, outs = refs[:n], refs[n:2 * n]
        ssem, rsem = refs[2 * n:]
        x, y, c = _me()
        cps = []
        for a in range(n):
            my = {"x": x, "y": y, "c": c}[axes[a]]
            partner = {"x": (1 - x, y, c), "y": (x, 1 - y, c), "c": (x, y, 1 - c)}[axes[a]]
            cps.append(pltpu.make_async_remote_copy(
                src_ref=ins[a].at[:, 1 - my], dst_ref=outs[a], send_sem=ssem.at[a], recv_sem=rsem.at[a],
                device_id=partner, device_id_type=MESH_ID))
            cps[-1].start()
        for cp in cps:
            cp.wait()

    return pl.pallas_call(
        body, name=name,
        in_specs=[ANY] * n, out_specs=[ANY] * n,
        out_shape=[jax.ShapeDtypeStruct((a.shape[0],) + a.shape[2:], a.dtype) for a in arrs],
        scratch_shapes=[pltpu.SemaphoreType.DMA((n,)), pltpu.SemaphoreType.DMA((n,))],
        compiler_params=pltpu.CompilerParams(has_side_effects=True),
    )(*arrs)


def _add_kept(arr, got, idx, name, with_bf16):
    P, _, R, N = arr.shape
    tr = _tile(R, max(16, (1 << 20) // N))

    def body(idx_ref, a_ref, g_ref, o_ref, *ob_ref):
        s = a_ref[...] + g_ref[...].astype(F32)
        o_ref[...] = s
        if with_bf16:
            ob_ref[0][...] = s.astype(BF16)

    out_blk = pl.BlockSpec((None, tr, N), lambda p, i, idx: (p, i, 0))
    return pl.pallas_call(
        body, name=name,
        grid_spec=pltpu.PrefetchScalarGridSpec(
            num_scalar_prefetch=1, grid=(P, R // tr),
            in_specs=[pl.BlockSpec((None, None, tr, N), lambda p, i, idx: (p, idx[0], i, 0)),
                      pl.BlockSpec((None, tr, N), lambda p, i, idx: (p, i, 0))],
            out_specs=[out_blk] * (2 if with_bf16 else 1)),
        out_shape=[jax.ShapeDtypeStruct((P, R, N), F32)] + ([jax.ShapeDtypeStruct((P, R, N), BF16)] if with_bf16 else []),
        compiler_params=_cp(("parallel", "parallel")),
    )(idx, arr, got)


def _share_halves(halves):
    n = len(halves)

    def body(*refs):
        ins, outs = refs[:n], refs[n:2 * n]
        ssem, rsem = refs[2 * n:]
        x, y, c = _me()
        cps = []
        for a in range(n):
            cps.append(pltpu.make_async_remote_copy(
                src_ref=ins[a], dst_ref=outs[a], send_sem=ssem.at[a], recv_sem=rsem.at[a],
                device_id=(x, y, 1 - c), device_id_type=MESH_ID))
            cps[-1].start()
        for cp in cps:
            cp.wait()

    return pl.pallas_call(
        body, name="share_halves",
        in_specs=[ANY] * n, out_specs=[ANY] * n,
        out_shape=[jax.ShapeDtypeStruct(h.shape, h.dtype) for h in halves],
        scratch_shapes=[pltpu.SemaphoreType.DMA((n,)), pltpu.SemaphoreType.DMA((n,))],
        compiler_params=pltpu.CompilerParams(has_side_effects=True),
    )(*halves)


def _reduce_scatter(grads, first_axes):
    x, y, c = _me()
    idx = lambda v: jnp.reshape(v, (1,)).astype(jnp.int32)
    coord = {"x": idx(x), "y": idx(y), "c": idx(c)}
    n = len(grads)
    second = ["y" if f == "x" else "x" for f in first_axes]
    views = [g.reshape(N_SHARD, 2, g.shape[1] // 2, g.shape[2]) for g in grads]
    got = _exchange(views, "c" * n, "rs_exchange_c")
    summed = [_add_kept(v, r, coord["c"], "rs_add_c", True) for v, r in zip(views, got)]

    def split_view(a, ax):
        _, rh, nn = a.shape
        return a.reshape(1, 2, 2 * rh, nn) if ax == "x" else a.reshape(2, 2, rh, nn)

    got = _exchange([split_view(s[1], f) for s, f in zip(summed, first_axes)], first_axes, "rs_exchange_1")
    summed = [_add_kept(split_view(s[0], f), r, coord[f], "rs_add_1", True)
              for s, r, f in zip(summed, got, first_axes)]
    views32 = [s[0].reshape(1, 2, -1, s[0].shape[-1]) for s in summed]
    views16 = [s[1].reshape(1, 2, -1, s[1].shape[-1]) for s in summed]
    got = _exchange(views16, second, "rs_exchange_2")
    kept = [_add_kept(v, r, coord[g], "rs_add_2", False)[0][0] for v, r, g in zip(views32, got, second)]
    return kept, _share_halves(kept)


def _allgather_small(p):
    R, D = p.shape

    def body(p_ref, o_ref, ssem, rsem):
        x, y, c = _me()
        me = 4 * x + 2 * y + c
        o_ref[me] = p_ref[...]
        cps = []
        for m in range(1, 8):
            mx, my, mc = (m >> 2) & 1, (m >> 1) & 1, m & 1
            peer = (1 - x if mx else x, 1 - y if my else y, 1 - c if mc else c)
            cps.append(pltpu.make_async_remote_copy(
                src_ref=p_ref, dst_ref=o_ref.at[me], send_sem=ssem.at[m - 1], recv_sem=rsem.at[m - 1],
                device_id=peer, device_id_type=MESH_ID))
            cps[-1].start()
        for cp in cps:
            cp.wait()

    return pl.pallas_call(
        body, name="allgather_small",
        in_specs=[pl.BlockSpec(memory_space=pltpu.VMEM)],
        out_specs=pl.BlockSpec(memory_space=pltpu.VMEM),
        out_shape=jax.ShapeDtypeStruct((8, R, D), p.dtype),
        scratch_shapes=[pltpu.SemaphoreType.DMA((7,)), pltpu.SemaphoreType.DMA((7,))],
        compiler_params=pltpu.CompilerParams(has_side_effects=True, vmem_limit_bytes=VMEM_LIMIT),
    )(p)


def _adam_math(w, g, m, v):
    m = ADAM_B1 * m + (1.0 - ADAM_B1) * g
    v = ADAM_B2 * v + (1.0 - ADAM_B2) * (g * g)
    m_hat = m / (1.0 - ADAM_B1 ** ADAM_STEP)
    v_hat = v / (1.0 - ADAM_B2 ** ADAM_STEP)
    delta = -ADAM_LR * (m_hat / (jnp.sqrt(v_hat) + ADAM_EPS) + ADAM_WD * w)
    return delta, m, v


def _adam(w, g_mine, g_sib, m, v, core):
    R, N = w.shape
    rh = R // 2
    tr = _tile(rh, max(16, (1 << 19) // N))
    nt = rh // tr

    def body(c_ref, w_ref, gm_ref, gs_ref, m_ref, v_ref, g_ref, d_ref, nm_ref, nv_ref):
        g = jnp.where(pl.program_id(0) == c_ref[0], gm_ref[...], gs_ref[...])
        d, nm, nv = _adam_math(w_ref[...], g, m_ref[...], v_ref[...])
        g_ref[...] = g
        d_ref[...] = d
        nm_ref[...] = nm
        nv_ref[...] = nv

    blk = pl.BlockSpec((tr, N), lambda h, i, c: (h * nt + i, 0))
    half = pl.BlockSpec((tr, N), lambda h, i, c: (i, 0))
    return pl.pallas_call(
        body, name="adamw",
        grid_spec=pltpu.PrefetchScalarGridSpec(
            num_scalar_prefetch=1, grid=(2, nt),
            in_specs=[blk, half, half, blk, blk], out_specs=[blk] * 4),
        out_shape=[jax.ShapeDtypeStruct((R, N), F32)] * 4,
        compiler_params=_cp(("parallel", "parallel")),
    )(core, w, g_mine, g_sib, m, v)


def _reduce_rows(parts, sizes, rows_out):
    D = parts.shape[1]

    def body(p_ref, o_ref):
        o_ref[...] = jnp.zeros((rows_out, D), F32)
        off = 0
        for i, sz in enumerate(sizes):
            o_ref[i:i + 1, :] = jnp.sum(p_ref[off:off + sz, :], axis=0, keepdims=True)
            off += sz

    return pl.pallas_call(
        body, name="reduce_rows",
        out_shape=jax.ShapeDtypeStruct((rows_out, D), F32),
        compiler_params=pltpu.CompilerParams(vmem_limit_bytes=VMEM_LIMIT),
    )(parts)


def _sum_devices(g8):
    _, R, D = g8.shape

    def body(g_ref, o_ref):
        tot = g_ref[0]
        for k in range(1, 8):
            tot = tot + g_ref[k]
        o_ref[...] = tot

    return pl.pallas_call(
        body, name="sum_devices",
        out_shape=jax.ShapeDtypeStruct((R, D), F32),
        compiler_params=pltpu.CompilerParams(vmem_limit_bytes=VMEM_LIMIT),
    )(g8)


def _small_update(gathered, w, m, v, L):
    _, R, D = gathered.shape

    def body(g8_ref, w_ref, m_ref, v_ref, g_ref, d_ref, nm_ref, nv_ref):
        tot = g8_ref[0]
        for k in range(1, 8):
            tot = tot + g8_ref[k]
        g_ref[...] = tot
        sm = _softmax_rows(w_ref, L)
        run = sm[0]
        dcum = []
        for l in range(L):
            if l > 0:
                run = run + sm[l]
            cum = run - sm[0]
            dcum.append(jnp.where((cum > 0.0) & (cum < 1.0), g_ref[l:l + 1, :], 0.0))
        dsm = [jnp.zeros((1, D), F32)]
        for i in range(1, L):
            dsm.append(functools.reduce(lambda p, q: p + q, dcum[i:]))
        dot = functools.reduce(lambda p, q: p + q, [s * d for s, d in zip(sm, dsm)])
        for l in range(L):
            g_ref[l:l + 1, :] = sm[l] * (dsm[l] - dot)
        lam = w_ref[5 * L:6 * L, :]
        g_ref[5 * L:6 * L, :] = g_ref[5 * L:6 * L, :] * (-_sig(-lam))
        d, nm, nv = _adam_math(w_ref[...], g_ref[...], m_ref[...], v_ref[...])
        d_ref[...] = d
        nm_ref[...] = nm
        nv_ref[...] = nv

    return pl.pallas_call(
        body, name="small_update",
        out_shape=[jax.ShapeDtypeStruct((R, D), F32)] * 4,
        compiler_params=pltpu.CompilerParams(vmem_limit_bytes=VMEM_LIMIT),
    )(gathered, w, m, v)


def kernel(x, lb_logits, norm_mix, w_in, conv_w, conv_b, w_r, b_r, w_i, b_i, lam, hg_norm, w_out, norm_mlp, w_up, w_down, norm_final, loss_target, m_lb_logits, m_norm_mix, m_w_in, m_conv_w, m_conv_b, m_w_r, m_b_r, m_w_i, m_b_i, m_lam, m_hg_norm, m_w_out, m_norm_mlp, m_w_up, m_w_down, m_norm_final, v_lb_logits, v_norm_mix, v_w_in, v_conv_w, v_conv_b, v_w_r, v_b_r, v_w_i, v_b_i, v_lam, v_hg_norm, v_w_out, v_norm_mlp, v_w_up, v_w_down, v_norm_final):
    B, S, D = x.shape
    L = norm_mix.shape[0]
    nb = D // RG_BLOCK
    Dq = D // N_SHARD
    mx, my, mc = _me()
    shard = 2 * mx + my

    big_w = (w_in, w_out, w_up, w_down, w_r, w_i)
    flat2 = lambda a: a.reshape(-1, a.shape[-1])
    slot = jnp.reshape(shard, (1,)).astype(jnp.int32)
    def place(w):
        b = _cast_place(flat2(w), slot)
        return b.reshape(N_SHARD, L, 2, b.shape[1] // (2 * L), b.shape[2])

    *wbufs, g_r, g_i = _gather_weights([place(w) for w in big_w], LINK_SPLIT, [0] * 6, [1] * 4 + [L] * 2)
    carried = None
    if L > 1:
        carried = (_gather_phases(4, LINK_SPLIT[:4], [1] * 4, [L - 1] * 4), GATHER_STEPS, _gather_sems(4))
    unshard_gate = lambda g: g.reshape(N_SHARD, L, nb, RG_BLOCK // N_SHARD, RG_BLOCK).transpose(1, 2, 0, 3, 4).reshape(
        L, nb, RG_BLOCK, RG_BLOCK)
    w_r_full, w_i_full = unshard_gate(g_r), unshard_gate(g_i)

    R_LB, R_NMIX, R_CB, R_BR, R_BI, R_LAM, R_NMLP, R_GN, R_CW, R_NF, R_LOSS = (
        0, L, 2 * L, 3 * L, 4 * L, 5 * L, 6 * L, 7 * L, 8 * L, 12 * L, 12 * L + 1)
    n_rows = 12 * L + 2
    rows_pad = n_rows + (-n_rows) % SUBLANES

    def place_cols(a):
        return lax.dynamic_update_slice(jnp.zeros((a.shape[0], D), F32), a, (0, shard * Dq))

    def pack_small(lb_, nmix_, cb_, br_, bi_, lam_, nmlp_, gn_, cw_, nf_):
        gn_pad = jnp.pad(gn_, ((0, 0), (0, D - HEAD)))
        rows = [lb_, nmix_, cb_, br_, bi_, lam_, nmlp_, gn_pad, place_cols(cw_.reshape(L * CONV_TAPS, Dq)),
                nf_[None, :], jnp.zeros((rows_pad - n_rows + 1, D), F32)]
        return jnp.concatenate(rows, axis=0)

    w_small = pack_small(lb_logits, norm_mix, conv_b, b_r, b_i, lam, norm_mlp, hg_norm, conv_w, norm_final)
    m_small = pack_small(m_lb_logits, m_norm_mix, m_conv_b, m_b_r, m_b_i, m_lam, m_norm_mlp, m_hg_norm, m_conv_w,
                         m_norm_final)
    v_small = pack_small(v_lb_logits, v_norm_mix, v_conv_b, v_b_r, v_b_i, v_lam, v_norm_mlp, v_hg_norm, v_conv_w,
                         v_norm_final)
    cw_rows = place_cols(conv_w.reshape(L * CONV_TAPS, Dq)) * jnp.where(mc == 0, 1.0, 0.0)
    conv_w_full = _sum_devices(_allgather_small(cw_rows)).reshape(L, CONV_TAPS, D)

    lowb, sp = _prep(lb_logits, lam)

    loss_parts, grad_x, big, small, g_nf = _local_step(
        x, loss_target, lowb, sp, norm_mix, wbufs, conv_w_full, conv_b, w_r_full, b_r, w_i_full, b_i, hg_norm,
        norm_mlp, norm_final, carried)

    shard_gate = lambda g: g.reshape(L, nb, N_SHARD, RG_BLOCK // N_SHARD, RG_BLOCK).transpose(2, 0, 1, 3, 4)
    stacked = list(big[:4]) + [shard_gate(big[4]), shard_gate(big[5])]
    mine, sibs = _reduce_scatter([s.reshape(N_SHARD, -1, s.shape[-1]) for s in stacked], LINK_SPLIT)
    core = jnp.reshape(mc, (1,)).astype(jnp.int32)
    outs = {}
    for name, w, m, v, g_mine, g_sib in zip(("w_in", "w_out", "w_up", "w_down", "w_r", "w_i"), big_w,
                                            (m_w_in, m_w_out, m_w_up, m_w_down, m_w_r, m_w_i),
                                            (v_w_in, v_w_out, v_w_up, v_w_down, v_w_r, v_w_i), mine, sibs):
        outs[name] = tuple(t.reshape(w.shape) for t in _adam(flat2(w), g_mine, g_sib, flat2(m), flat2(v), core))

    parts, sizes = [], []

    def add_rows(a):
        a = a.reshape(-1, a.shape[-1])
        if a.shape[1] != D:
            a = jnp.pad(a, ((0, 0), (0, D - a.shape[1])))
        parts.append(a)
        sizes.append(a.shape[0])

    for i in range(8):
        for l in range(L):
            add_rows(small[l][i])
    for l in range(L):
        for j in range(CONV_TAPS):
            add_rows(small[l][8][j])
    add_rows(g_nf)
    loss_rows = loss_parts[:, 0:1, :]
    add_rows(jnp.where(lax.broadcasted_iota(jnp.int32, loss_rows.shape, 2) == 0, loss_rows, 0.0))
    g_small = _reduce_rows(jnp.concatenate(parts, axis=0), sizes, rows_pad)
    g_small, d_small, nm_small, nv_small = _small_update(_allgather_small(g_small), w_small, m_small, v_small, L)

    def unpack(t):
        take_cols = lambda a: lax.dynamic_slice(a, (0, shard * Dq), (a.shape[0], Dq))
        return {"lb_logits": t[R_LB:R_LB + L], "norm_mix": t[R_NMIX:R_NMIX + L], "conv_b": t[R_CB:R_CB + L],
                "b_r": t[R_BR:R_BR + L], "b_i": t[R_BI:R_BI + L], "lam": t[R_LAM:R_LAM + L],
                "norm_mlp": t[R_NMLP:R_NMLP + L], "hg_norm": t[R_GN:R_GN + L, :HEAD],
                "conv_w": take_cols(t[R_CW:R_CW + L * CONV_TAPS]).reshape(L, CONV_TAPS, Dq), "norm_final": t[R_NF]}

    small_out = [unpack(t) for t in (g_small, d_small, nm_small, nv_small)]
    loss = g_small[R_LOSS, 0]
    names = ("lb_logits", "norm_mix", "w_in", "conv_w", "conv_b", "w_r", "b_r", "w_i", "b_i", "lam", "hg_norm",
             "w_out", "norm_mlp", "w_up", "w_down", "norm_final")
    result = [loss, grad_x]
    for kind in range(4):
        for nme in names:
            result.append(outs[nme][kind] if nme in outs else small_out[kind][nme])
    return tuple(result)
```

```python
import functools
import math

import jax
import jax.numpy as jnp
from jax import lax
from jax.experimental import pallas as pl
from jax.experimental.pallas import tpu as pltpu

F32 = jnp.float32
BF16 = jnp.bfloat16

HEAD = 128
RG_BLOCK = 256
CONV_TAPS = 4
RG_C = 8.0
F_MIN = 1e-30
NORM_EPS = 1e-6
N_SEG = 8
N_SHARD = 4
HG_CHUNK = 256
RG_TILE = 256
ADAM_LR, ADAM_B1, ADAM_B2, ADAM_EPS, ADAM_WD, ADAM_STEP = 0.001, 0.9, 0.999, 1e-08, 0.01, 10
V7X_VMEM_BYTES = 64 * 1024 * 1024
VMEM_LIMIT = V7X_VMEM_BYTES - 8 * 1024 * 1024
SUBLANES = 8
LINK_SPLIT = "xxyyyy"
GATHER_STEPS = (0.0, 0.6, 0.88, 1.0)
MESH_ID = pl.DeviceIdType.MESH
ANY = pl.BlockSpec(memory_space=pl.ANY)


def _cp(sem):
    return pltpu.CompilerParams(dimension_semantics=sem, vmem_limit_bytes=VMEM_LIMIT)


def _dot(a, b):
    return jnp.dot(a, b, preferred_element_type=F32)


def _dot_nt(a, b):
    return lax.dot_general(a, b, (((1,), (1,)), ((), ())), preferred_element_type=F32)


def _dot_tn(a, b):
    return lax.dot_general(a, b, (((0,), (0,)), ((), ())), preferred_element_type=F32)


def _dot_01(m01, x):
    n = x.shape[1]
    hi = x.astype(BF16)
    r1 = x - hi.astype(F32)
    mid = r1.astype(BF16)
    lo = (r1 - mid.astype(F32)).astype(BF16)
    y = _dot(m01, jnp.concatenate([hi, mid, lo], axis=1))
    return y[:, :n] + y[:, n:2 * n] + y[:, 2 * n:]


def _sig(x):
    return jax.nn.sigmoid(x)


def _rows8(x):
    return x.reshape(x.shape[0] // SUBLANES, SUBLANES, x.shape[1]).sum(axis=0)


def _tile(n, cap):
    if n <= cap:
        return n
    t = cap - cap % 16
    while n % t:
        t -= 16
    return t


_GELU_C = math.sqrt(2.0 / math.pi)


def _gelu_and_grad(x):
    x2 = x * x
    t = jnp.tanh(_GELU_C * (x + 0.044715 * x * x2))
    g = 0.5 * x * (1.0 + t)
    dg = 0.5 * (1.0 + t) + 0.5 * x * (1.0 - t * t) * (_GELU_C * (1.0 + 3.0 * 0.044715 * x2))
    return g, dg


def _rms(x):
    return lax.rsqrt(jnp.mean(x * x, axis=-1, keepdims=True) + NORM_EPS)


def _rms_bwd(dh, x, rs, gain):
    xhat = x * rs
    dxhat = dh * gain
    dx = rs * (dxhat - xhat * jnp.mean(dxhat * xhat, axis=-1, keepdims=True))
    return dx, _rows8(dh * xhat)


def _inproj_fwd(x2, gain, w_st, layer):
    T, D = x2.shape
    tm = _tile(T, 2048)

    def body(x_ref, g_ref, w_ref, o_ref, h_ref):
        @pl.when(pl.program_id(1) == 0)
        def _():
            x = x_ref[...]
            h_ref[...] = (x * _rms(x) * g_ref[...]).astype(BF16)
        o_ref[...] = _dot(h_ref[...], w_ref[...])

    return pl.pallas_call(
        body, name="inproj_fwd",
        grid=(T // tm, N_SEG),
        in_specs=[pl.BlockSpec((tm, D), lambda i, k: (i, 0)),
                  pl.BlockSpec((1, D), lambda i, k: (0, 0)),
                  pl.BlockSpec((None, None, D, D), lambda i, k: (k // 2, layer, 0, k % 2))],
        out_specs=[pl.BlockSpec((tm, D), lambda i, k: (i, k)),
                   pl.BlockSpec((tm, D), lambda i, k: (i, 0))],
        out_shape=[jax.ShapeDtypeStruct((T, N_SEG * D), F32), jax.ShapeDtypeStruct((T, D), BF16)],
        compiler_params=_cp(("parallel", "arbitrary")),
    )(x2, gain, w_st)


def _rg_gates(xc, wr_ref, br, wi_ref, bi, sp):
    D = xc.shape[1]
    xcb = xc.astype(BF16)
    pr, pi = [], []
    for n in range(D // RG_BLOCK):
        blk = xcb[:, n * RG_BLOCK:(n + 1) * RG_BLOCK]
        pr.append(_dot(blk, wr_ref[n]))
        pi.append(_dot(blk, wi_ref[n]))
    r = _sig(jnp.concatenate(pr, axis=1) + br) if len(pr) > 1 else _sig(pr[0] + br)
    i = _sig(jnp.concatenate(pi, axis=1) + bi) if len(pi) > 1 else _sig(pi[0] + bi)
    la = (-RG_C) * r * sp
    a = jnp.exp(la)
    y = 2.0 * la
    one_m_e2 = jnp.where(y > -1e-2, -(y * (1.0 + 0.5 * y * (1.0 + y * (1.0 / 3.0)))), 1.0 - jnp.exp(y))
    mult = jnp.sqrt(jnp.maximum(one_m_e2, 0.0))
    return r, i, a, mult


def _conv_taps(xbuf, cw_ref, ts):
    acc = None
    for j in range(CONV_TAPS):
        term = cw_ref[j:j + 1, :] * xbuf[pl.ds(SUBLANES - (CONV_TAPS - 1) + j, ts), :]
        acc = term if acc is None else acc + term
    return acc


def _rg_fwd(proj, B, cw, cb, wr, br, wi, bi, sp):
    T = proj.shape[0]
    D = proj.shape[1] // N_SEG
    S = T // B
    ts = _tile(S, RG_TILE)
    nts = S // ts
    nb = D // RG_BLOCK

    def body(xa_ref, ga_ref, cw_ref, cb_ref, wr_ref, br_ref, wi_ref, bi_ref, sp_ref,
             ya_ref, h_ref, xbuf, a_scr, u_scr, carry):
        @pl.when(pl.program_id(1) == 0)
        def _():
            xbuf[0:SUBLANES, :] = jnp.zeros((SUBLANES, D), F32)
            carry[...] = jnp.zeros((SUBLANES, D), F32)

        xbuf[pl.ds(SUBLANES, ts), :] = xa_ref[...]
        xc = _conv_taps(xbuf, cw_ref, ts) + cb_ref[...]
        r, i, a, mult = _rg_gates(xc, wr_ref, br_ref[...], wi_ref, bi_ref[...], sp_ref[...])
        a_scr[...] = a
        u_scr[...] = mult * (i * xc)
        row8 = lax.broadcasted_iota(jnp.int32, (SUBLANES, 1), 0)

        def blk(n, hprev):
            off = pl.multiple_of(n * SUBLANES, SUBLANES)
            a8 = a_scr[pl.ds(off, SUBLANES), :]
            u8 = u_scr[pl.ds(off, SUBLANES), :]
            for d in (1, 2, 4):
                m = row8 >= d
                ap = jnp.where(m, pltpu.roll(a8, d, 0), 1.0)
                up = jnp.where(m, pltpu.roll(u8, d, 0), 0.0)
                u8 = a8 * up + u8
                a8 = a8 * ap
            h8 = u8 + a8 * hprev
            u_scr[pl.ds(off, SUBLANES), :] = h8
            last = jnp.sum(jnp.where(row8 == SUBLANES - 1, h8, 0.0), axis=0, keepdims=True)
            return jnp.broadcast_to(last, (SUBLANES, D))

        carry[...] = lax.fori_loop(0, ts // SUBLANES, blk, carry[...])
        h = u_scr[...]
        h_ref[...] = h
        g, _ = _gelu_and_grad(ga_ref[...])
        ya_ref[...] = h * g
        xbuf[0:SUBLANES, :] = xa_ref[pl.ds(ts - SUBLANES, SUBLANES), :]

    vec = pl.BlockSpec((1, D), lambda b, j: (0, 0))
    gate = pl.BlockSpec((nb, RG_BLOCK, RG_BLOCK), lambda b, j: (0, 0, 0))
    return pl.pallas_call(
        body, name="rg_fwd",
        grid=(B, nts),
        in_specs=[pl.BlockSpec((ts, D), lambda b, j: (b * nts + j, 0)),
                  pl.BlockSpec((ts, D), lambda b, j: (b * nts + j, 1)),
                  pl.BlockSpec((CONV_TAPS, D), lambda b, j: (0, 0)), vec, gate, vec, gate, vec, vec],
        out_specs=[pl.BlockSpec((ts, D), lambda b, j: (b * nts + j, 0))] * 2,
        out_shape=[jax.ShapeDtypeStruct((T, D), F32)] * 2,
        scratch_shapes=[pltpu.VMEM((SUBLANES + ts, D), F32), pltpu.VMEM((ts, D), F32),
                        pltpu.VMEM((ts, D), F32), pltpu.VMEM((SUBLANES, D), F32)],
        compiler_params=_cp(("arbitrary", "arbitrary")),
    )(proj, proj, cw, cb, wr, br, wi, bi, sp)


def _hg_gates(q, z, lb):
    sig = _sig(z)
    one_m = 1.0 - lb
    fg = lb + one_m * sig
    lf = jnp.log(jnp.maximum(fg, F_MIN))
    kf = one_m * (1.0 - sig)
    qs = _sig(q)
    return q * qs, qs, kf, lf, fg, sig


def _hg_cum(lf, C):
    ri = lax.broadcasted_iota(jnp.int32, (C, C), 0)
    ci = lax.broadcasted_iota(jnp.int32, (C, C), 1)
    return _dot_01(jnp.where(ci <= ri, 1.0, 0.0).astype(BF16), lf)


def _hg_levels(lf, cum, C):
    row = lax.broadcasted_iota(jnp.int32, (C, 1), 0)
    levels = []
    w = C // 2
    while w >= 4:
        blk = 2 * w
        upper = (row & w) != 0
        ref = jnp.min(jnp.where(upper, 0.0, cum).reshape(C // blk, blk, HEAD), axis=1, keepdims=True)
        ref = jnp.broadcast_to(ref, (C // blk, blk, HEAD)).reshape(C, HEAD)
        d = cum - ref
        e = jnp.exp(jnp.where(upper, d, -d))
        levels.append((jnp.where(upper, e, 0.0), jnp.where(upper, 0.0, e), blk))
        w //= 2
    r4 = row & 3
    lf_prev = pltpu.roll(lf, 1, 0)
    lf_next = pltpu.roll(lf, C - 1, 0)
    eq = jnp.where(r4 >= 2, jnp.exp(jnp.where(r4 == 3, lf + lf_prev, lf)), 0.0)
    ek = jnp.where(r4 == 0, jnp.exp(lf_next), jnp.where(r4 == 1, 1.0, 0.0))
    levels.append((eq, ek, 4))
    odd = (row & 1) == 1
    levels.append((jnp.where(odd, jnp.exp(lf), 0.0), jnp.where(odd, 0.0, 1.0), 2))
    ones = jnp.ones_like(lf)
    levels.append((ones, ones, 1))
    return levels


def _same_block(C, blk):
    ri = lax.broadcasted_iota(jnp.int32, (C, C), 0)
    ci = lax.broadcasted_iota(jnp.int32, (C, C), 1)
    if blk == 1:
        return ri == ci
    shift = blk.bit_length() - 1
    return (ri >> shift) == (ci >> shift)


def _hg_mask_blocks(C):
    blks = []
    w = C // 4
    while w >= 4:
        blks.append(2 * w)
        w //= 2
    return blks + [4, 2, 1]


def _hg_fill_masks(mask_scr, C):
    for i, blk in enumerate(_hg_mask_blocks(C)):
        mask_scr[i] = jnp.where(_same_block(C, blk), 1.0, 0.0).astype(F32)


def _hg_scores(qf, kf, levels, mask_scr):
    A = None
    for n, (eq, ek, _) in enumerate(levels):
        a = _dot_nt((qf * eq).astype(BF16), (kf * ek).astype(BF16))
        if n > 0:
            a = a * mask_scr[n - 1]
        A = a if A is None else A + a
    return A


def _hg_specs(B, NC, D, C, dtype_blocks):
    def spec(col0, rev):
        if rev:
            return pl.BlockSpec((C, HEAD), lambda b, h, j: (b * NC + (NC - 1 - j), col0 + h))
        return pl.BlockSpec((C, HEAD), lambda b, h, j: (b * NC + j, col0 + h))
    return spec


class _Ride:
    def __init__(self, reads, bufs, outs, phases, fractions, sems):
        self.reads, self.bufs, self.outs = list(reads), list(bufs), list(outs)
        self.phases, self.fractions, self.sems = list(phases), list(fractions), list(sems)


def _ride_call(body, ride, *, name, grid, in_specs, out_specs, out_shape, scratch_shapes, semantics, operands):
    if ride is None:
        return pl.pallas_call(body, name=name, grid=grid, in_specs=in_specs, out_specs=out_specs, out_shape=out_shape,
                              scratch_shapes=scratch_shapes, compiler_params=_cp(semantics))(*operands)
    n_in, n_out, n_scr = len(in_specs), len(out_specs), len(scratch_shapes)
    nr, nb, no = len(ride.reads), len(ride.bufs), len(ride.outs)
    last_step = math.prod(grid) - 1

    def full_body(*refs):
        own = refs[:n_in] + refs[n_in + nr + nb:n_in + nr + nb + n_out]
        tail = refs[n_in + nr + nb + n_out:]
        ride_refs = (refs[n_in:n_in + nr], tail[:nb], tail[nb:nb + no])
        scr = tail[nb + no:]
        step = pl.program_id(0)
        for d in range(1, len(grid)):
            step = step * grid[d] + pl.program_id(d)
        for phase, frac in zip(ride.phases, ride.fractions):
            @pl.when(step == int(round(frac * last_step)))
            def _(phase=phase):
                phase(*ride_refs, *scr[n_scr:])
        body(*own, *scr[:n_scr])

    return pl.pallas_call(
        full_body, name=name, grid=grid,
        in_specs=list(in_specs) + [ANY] * (nr + nb),
        out_specs=list(out_specs) + [ANY] * (nb + no),
        out_shape=list(out_shape) + [jax.ShapeDtypeStruct(b.shape, b.dtype) for b in ride.bufs] + ride.outs,
        input_output_aliases={n_in + nr + j: n_out + j for j in range(nb)},
        scratch_shapes=list(scratch_shapes) + ride.sems,
        compiler_params=_cp(("arbitrary",) * len(grid)),
    )(*operands, *ride.reads, *ride.bufs)


def _hg_fwd(proj, B, lb, gn, ride=None):
    T = proj.shape[0]
    D = proj.shape[1] // N_SEG
    S = T // B
    C = min(HG_CHUNK, S)
    NC = S // C
    H = D // HEAD
    hpd = D // HEAD
    spec = _hg_specs(B, NC, D, C, None)

    def body(q_ref, z_ref, v_ref, g_ref, lb_ref, gn_ref, yb_ref, o_ref, st_ref, a_ref, cum_ref, st_scr, mask_scr):
        @pl.when(pl.program_id(2) == 0)
        def _():
            st_scr[...] = jnp.zeros((HEAD, HEAD), F32)
            _hg_fill_masks(mask_scr, C)

        s_t = st_scr[...]
        st_ref[...] = s_t
        qf, _, kf, lf, _, _ = _hg_gates(q_ref[...], z_ref[...], lb_ref[...])
        cum = _hg_cum(lf, C)
        cum_ref[...] = cum
        A = _hg_scores(qf, kf, _hg_levels(lf, cum, C), mask_scr).astype(BF16)
        a_ref[...] = A
        vb = v_ref[...].astype(BF16)
        o = _dot_nt((qf * jnp.exp(cum)).astype(BF16), s_t.astype(BF16)) + _dot(A, vb)
        last = jnp.sum(lf, axis=0, keepdims=True)
        kend = kf * jnp.exp(last - cum)
        st_scr[...] = jnp.exp(last) * s_t + _dot_tn(vb, kend.astype(BF16))
        o_ref[...] = o
        g = g_ref[...]
        yb_ref[...] = (o * _rms(o) * gn_ref[...]) * (g * _sig(g))

    return _ride_call(
        body, ride, name="hg_fwd",
        grid=(B, H, NC),
        in_specs=[spec(2 * hpd, False), spec(3 * hpd, False), spec(4 * hpd, False), spec(5 * hpd, False),
                  pl.BlockSpec((1, HEAD), lambda b, h, j: (0, h)),
                  pl.BlockSpec((1, HEAD), lambda b, h, j: (0, 0))],
        out_specs=[spec(0, False), spec(0, False),
                   pl.BlockSpec((None, None, None, HEAD, HEAD), lambda b, h, j: (b, h, j, 0, 0)),
                   pl.BlockSpec((None, None, None, C, C), lambda b, h, j: (b, h, j, 0, 0)), spec(0, False)],
        out_shape=[jax.ShapeDtypeStruct((T, D), F32), jax.ShapeDtypeStruct((T, D), F32),
                   jax.ShapeDtypeStruct((B, H, NC, HEAD, HEAD), F32),
                   jax.ShapeDtypeStruct((B, H, NC, C, C), BF16), jax.ShapeDtypeStruct((T, D), F32)],
        scratch_shapes=[pltpu.VMEM((HEAD, HEAD), F32), pltpu.VMEM((len(_hg_mask_blocks(C)), C, C), F32)],
        semantics=("parallel", "parallel", "arbitrary"),
        operands=(proj, proj, proj, proj, lb, gn))


def _w_full(ref):
    s, r, c = ref.shape
    return ref[...].reshape(s * r, c)


def _out_fwd(ya, yb, proj, x2, w_st, layer):
    T, D = x2.shape
    tm = _tile(T, 512)

    def body(ya_ref, yb_ref, ma_ref, mb_ref, x_ref, w_ref, xm_ref, y_ref):
        y = (_sig(ma_ref[...]) * ya_ref[...] + _sig(mb_ref[...]) * yb_ref[...]).astype(BF16)
        y_ref[...] = y
        xm_ref[...] = x_ref[...] + _dot(y, _w_full(w_ref))

    row = pl.BlockSpec((tm, D), lambda i: (i, 0))
    return pl.pallas_call(
        body, name="out_fwd",
        grid=(T // tm,),
        in_specs=[row, row, pl.BlockSpec((tm, D), lambda i: (i, 6)), pl.BlockSpec((tm, D), lambda i: (i, 7)), row,
                  pl.BlockSpec((N_SHARD, None, D // N_SHARD, D), lambda i: (0, layer, 0, 0))],
        out_specs=[row, row],
        out_shape=[jax.ShapeDtypeStruct((T, D), F32), jax.ShapeDtypeStruct((T, D), BF16)],
        compiler_params=_cp(("parallel",)),
    )(ya, yb, proj, proj, x2, w_st)


def _mlp_fwd(xm, gain, wup_st, wdn_st, layer):
    T, D = xm.shape
    F4 = wup_st.shape[3]
    tm = _tile(T, 1024)

    def body(x_ref, g_ref, wu_ref, wd_ref, xo_ref, up_ref, h_ref):
        @pl.when(pl.program_id(1) == 0)
        def _():
            x = x_ref[...]
            h_ref[...] = (x * _rms(x) * g_ref[...]).astype(BF16)
            xo_ref[...] = x
        up = _dot(h_ref[...], wu_ref[...])
        up_ref[...] = up.astype(BF16)
        act = jnp.maximum(up, 0.0)
        xo_ref[...] += _dot((act * act).astype(BF16), wd_ref[...])

    row = pl.BlockSpec((tm, D), lambda i, s: (i, 0))
    return pl.pallas_call(
        body, name="mlp_fwd",
        grid=(T // tm, N_SHARD),
        in_specs=[row, pl.BlockSpec((1, D), lambda i, s: (0, 0)),
                  pl.BlockSpec((None, None, D, F4), lambda i, s: (s, layer, 0, 0)),
                  pl.BlockSpec((None, None, F4, D), lambda i, s: (s, layer, 0, 0))],
        out_specs=[row, pl.BlockSpec((tm, F4), lambda i, s: (i, s)), row],
        out_shape=[jax.ShapeDtypeStruct((T, D), F32), jax.ShapeDtypeStruct((T, N_SHARD * F4), BF16),
                   jax.ShapeDtypeStruct((T, D), BF16)],
        compiler_params=_cp(("parallel", "arbitrary")),
    )(xm, gain, wup_st, wdn_st)


def _final_loss(x2, gain, tgt):
    T, D = x2.shape
    tm = _tile(T, 512)
    nt = T // tm

    def body(x_ref, g_ref, t_ref, loss_ref, dx_ref, dg_ref):
        x = x_ref[...]
        rs = _rms(x)
        err = x * rs * g_ref[...] - t_ref[...]
        part = 0.5 * jnp.sum(jnp.sum(err * err, axis=-1, keepdims=True) * (1.0 / D), axis=0, keepdims=True)
        loss_ref[...] = jnp.broadcast_to(part, (SUBLANES, 128))
        dx, dg = _rms_bwd(err * (1.0 / D), x, rs, g_ref[...])
        dx_ref[...] = dx
        dg_ref[...] = dg

    row = pl.BlockSpec((tm, D), lambda i: (i, 0))
    return pl.pallas_call(
        body, name="final_loss",
        grid=(nt,),
        in_specs=[row, pl.BlockSpec((1, D), lambda i: (0, 0)), row],
        out_specs=[pl.BlockSpec((None, SUBLANES, 128), lambda i: (i, 0, 0)), row,
                   pl.BlockSpec((None, SUBLANES, D), lambda i: (i, 0, 0))],
        out_shape=[jax.ShapeDtypeStruct((nt, SUBLANES, 128), F32), jax.ShapeDtypeStruct((T, D), F32),
                   jax.ShapeDtypeStruct((nt, SUBLANES, D), F32)],
        compiler_params=_cp(("parallel",)),
    )(x2, gain, tgt)


def _mlp_bwd_x(dx, xm, up, gain, wup_st, wdn_st, layer):
    T, D = xm.shape
    F4 = wup_st.shape[3]
    tm = _tile(T, 1024)
    nt = T // tm

    def body(dx_ref, x_ref, up_ref, g_ref, wu_ref, wd_ref, dxm_ref, dup_ref, dg_ref, dxb):
        s = pl.program_id(1)

        @pl.when(s == 0)
        def _():
            dxb[...] = dx_ref[...].astype(BF16)
            dxm_ref[...] = jnp.zeros((tm, D), F32)

        d_act = _dot_nt(dxb[...], wd_ref[...])
        d_up = (d_act * (2.0 * jnp.maximum(up_ref[...].astype(F32), 0.0))).astype(BF16)
        dup_ref[...] = d_up
        dxm_ref[...] += _dot_nt(d_up, wu_ref[...])

        @pl.when(s == N_SHARD - 1)
        def _():
            x = x_ref[...]
            dxn, dg = _rms_bwd(dxm_ref[...], x, _rms(x), g_ref[...])
            dxm_ref[...] = dx_ref[...] + dxn
            dg_ref[...] = dg

    row = pl.BlockSpec((tm, D), lambda i, s: (i, 0))
    return pl.pallas_call(
        body, name="mlp_bwd_x",
        grid=(nt, N_SHARD),
        in_specs=[row, row, pl.BlockSpec((tm, F4), lambda i, s: (i, s)), pl.BlockSpec((1, D), lambda i, s: (0, 0)),
                  pl.BlockSpec((None, None, D, F4), lambda i, s: (s, layer, 0, 0)),
                  pl.BlockSpec((None, None, F4, D), lambda i, s: (s, layer, 0, 0))],
        out_specs=[row, pl.BlockSpec((tm, F4), lambda i, s: (i, s)),
                   pl.BlockSpec((None, SUBLANES, D), lambda i, s: (i, 0, 0)), row],
        out_shape=[jax.ShapeDtypeStruct((T, D), F32), jax.ShapeDtypeStruct((T, N_SHARD * F4), BF16),
                   jax.ShapeDtypeStruct((nt, SUBLANES, D), F32), jax.ShapeDtypeStruct((T, D), BF16)],
        compiler_params=_cp(("parallel", "arbitrary")),
    )(dx, xm, up, gain, wup_st, wdn_st)


def _layer_slot(bufs, shapes, n_layers):
    out_shape = [jax.ShapeDtypeStruct((N_SHARD, n_layers) + s, F32) for s in shapes]
    return out_shape, ([] if bufs is None else list(bufs))


def _mlp_bwd_w(up, dxb, h, dup, layer, n_layers, bufs):
    T, D = dxb.shape
    F4 = up.shape[1] // N_SHARD
    tk = _tile(T, 1024)
    out_shape, extra = _layer_slot(bufs, [(D, F4), (F4, D)], n_layers)

    def body(up_ref, dx_ref, h_ref, dup_ref, *rest):
        gu_ref, gd_ref = rest[-2:]

        @pl.when(pl.program_id(1) == 0)
        def _():
            gu_ref[...] = jnp.zeros((D, F4), F32)
            gd_ref[...] = jnp.zeros((F4, D), F32)
        act = jnp.maximum(up_ref[...], 0.0)
        gd_ref[...] += _dot_tn(act * act, dx_ref[...])
        gu_ref[...] += _dot_tn(h_ref[...], dup_ref[...])

    return pl.pallas_call(
        body, name="mlp_bwd_w",
        grid=(N_SHARD, T // tk),
        in_specs=[pl.BlockSpec((tk, F4), lambda s, t: (t, s)), pl.BlockSpec((tk, D), lambda s, t: (t, 0)),
                  pl.BlockSpec((tk, D), lambda s, t: (t, 0)), pl.BlockSpec((tk, F4), lambda s, t: (t, s))]
        + [ANY] * len(extra),
        out_specs=[pl.BlockSpec((None, None, D, F4), lambda s, t: (s, layer, 0, 0)),
                   pl.BlockSpec((None, None, F4, D), lambda s, t: (s, layer, 0, 0))],
        out_shape=out_shape,
        input_output_aliases={4 + i: i for i in range(len(extra))},
        compiler_params=_cp(("parallel", "arbitrary")),
    )(up, dxb, h, dup, *extra)


def _out_bwd_x(dxm, ya, yb, proj, w_st, layer):
    T, D = dxm.shape
    tm = _tile(T, 512)

    def body(dx_ref, ya_ref, yb_ref, ma_ref, mb_ref, w_ref, dya_ref, dyb_ref, dma_ref, dmb_ref):
        dy = _dot_nt(dx_ref[...].astype(BF16), _w_full(w_ref))
        sa = _sig(ma_ref[...])
        sb = _sig(mb_ref[...])
        dya_ref[...] = dy * sa
        dyb_ref[...] = dy * sb
        dma_ref[...] = (dy * ya_ref[...] * (sa * (1.0 - sa))).astype(BF16)
        dmb_ref[...] = (dy * yb_ref[...] * (sb * (1.0 - sb))).astype(BF16)

    row = pl.BlockSpec((tm, D), lambda i: (i, 0))
    return pl.pallas_call(
        body, name="out_bwd_x",
        grid=(T // tm,),
        in_specs=[row, row, row, pl.BlockSpec((tm, D), lambda i: (i, 6)), pl.BlockSpec((tm, D), lambda i: (i, 7)),
                  pl.BlockSpec((N_SHARD, None, D // N_SHARD, D), lambda i: (0, layer, 0, 0))],
        out_specs=[row] * 4,
        out_shape=[jax.ShapeDtypeStruct((T, D), F32)] * 2 + [jax.ShapeDtypeStruct((T, D), BF16)] * 2,
        compiler_params=_cp(("parallel",)),
    )(dxm, ya, yb, proj, proj, w_st)


def _out_bwd_w(ymix, dxm, layer, n_layers, bufs):
    T, D = dxm.shape
    tk = _tile(T, 1024)
    out_shape, extra = _layer_slot(bufs, [(D // N_SHARD, D)], n_layers)

    def body(y_ref, dx_ref, *rest):
        g_ref = rest[-1]

        @pl.when(pl.program_id(0) == 0)
        def _():
            g_ref[...] = jnp.zeros((N_SHARD, D // N_SHARD, D), F32)
        g = _dot_tn(y_ref[...], dx_ref[...].astype(BF16))
        g_ref[...] += g.reshape(N_SHARD, D // N_SHARD, D)

    row = pl.BlockSpec((tk, D), lambda t: (t, 0))
    return pl.pallas_call(
        body, name="out_bwd_w",
        grid=(T // tk,),
        in_specs=[row, row] + [ANY] * len(extra),
        out_specs=[pl.BlockSpec((N_SHARD, None, D // N_SHARD, D), lambda t: (0, layer, 0, 0))],
        out_shape=out_shape,
        input_output_aliases={2 + i: i for i in range(len(extra))},
        compiler_params=_cp(("arbitrary",)),
    )(ymix, dxm, *extra)[0]


def _rg_bwd(proj, hrg, dya, B, cw, cb, wr, br, wi, bi, sp):
    T = proj.shape[0]
    D = proj.shape[1] // N_SEG
    S = T // B
    ts = _tile(S, RG_TILE)
    nts = S // ts
    nb = D // RG_BLOCK
    t8 = ts // SUBLANES

    def body(xa_ref, xp_ref, ga_ref, h_ref, hp_ref, dya_ref, cw_ref, cb_ref, wr_ref, br_ref, wi_ref, bi_ref, sp_ref,
             dxa_ref, dga_ref, gwr_ref, gwi_ref, gcw_ref, gcb_ref, gbr_ref, gbi_ref, gsp_ref,
             xbuf, hbuf, abuf, dbuf, g_scr, c_scr, gcar):
        b = pl.program_id(0)
        j = pl.program_id(1)
        first_in_time = j == nts - 1

        @pl.when((b == 0) & (j == 0))
        def _():
            gwr_ref[...] = jnp.zeros((nb, RG_BLOCK, RG_BLOCK), F32)
            gwi_ref[...] = jnp.zeros((nb, RG_BLOCK, RG_BLOCK), F32)
            gcw_ref[...] = jnp.zeros((CONV_TAPS, SUBLANES, D), F32)
            for r in (gcb_ref, gbr_ref, gbi_ref, gsp_ref):
                r[...] = jnp.zeros((SUBLANES, D), F32)

        @pl.when(j == 0)
        def _():
            abuf[pl.ds(ts, SUBLANES), :] = jnp.zeros((SUBLANES, D), F32)
            dbuf[pl.ds(ts, SUBLANES), :] = jnp.zeros((SUBLANES, D), F32)
            gcar[...] = jnp.zeros((SUBLANES, D), F32)

        keep = jnp.where(first_in_time, 0.0, 1.0)
        xbuf[0:SUBLANES, :] = xp_ref[...] * keep
        xbuf[pl.ds(SUBLANES, ts), :] = xa_ref[...]
        hbuf[0:SUBLANES, :] = hp_ref[...] * keep
        hbuf[pl.ds(SUBLANES, ts), :] = h_ref[...]

        xc = _conv_taps(xbuf, cw_ref, ts) + cb_ref[...]
        sp = sp_ref[...]
        r, i, a, mult = _rg_gates(xc, wr_ref, br_ref[...], wi_ref, bi_ref[...], sp)
        g_gate, dg_gate = _gelu_and_grad(ga_ref[...])
        dya = dya_ref[...]
        dga_ref[...] = (dya * h_ref[...] * dg_gate).astype(BF16)

        abuf[0:ts, :] = a
        c_scr[...] = abuf[pl.ds(1, ts), :]
        g_scr[...] = dya * g_gate
        row8 = lax.broadcasted_iota(jnp.int32, (SUBLANES, 1), 0)

        def blk(n, gnext):
            off = pl.multiple_of((t8 - 1 - n) * SUBLANES, SUBLANES)
            c8 = c_scr[pl.ds(off, SUBLANES), :]
            d8 = g_scr[pl.ds(off, SUBLANES), :]
            for d in (1, 2, 4):
                m = row8 < SUBLANES - d
                cn = jnp.where(m, pltpu.roll(c8, SUBLANES - d, 0), 1.0)
                dn = jnp.where(m, pltpu.roll(d8, SUBLANES - d, 0), 0.0)
                d8 = d8 + c8 * dn
                c8 = c8 * cn
            g8 = d8 + c8 * gnext
            g_scr[pl.ds(off, SUBLANES), :] = g8
            first = jnp.sum(jnp.where(row8 == 0, g8, 0.0), axis=0, keepdims=True)
            return jnp.broadcast_to(first, (SUBLANES, D))

        gcar[...] = lax.fori_loop(0, t8, blk, gcar[...])
        abuf[pl.ds(ts, SUBLANES), :] = a[0:SUBLANES, :]

        g = g_scr[...]
        hprev = hbuf[pl.ds(SUBLANES - 1, ts), :]
        gx = i * xc
        e2 = a * a
        dla = g * hprev * a - jnp.where(mult > 0.0, g * gx * e2 / jnp.where(mult > 0.0, mult, 1.0), 0.0)
        dgx = g * mult
        dpr = (dla * ((-RG_C) * sp)) * (r * (1.0 - r))
        dpi = (dgx * xc) * (i * (1.0 - i))
        gsp_ref[...] += _rows8(dla * ((-RG_C) * r))
        gbr_ref[...] += _rows8(dpr)
        gbi_ref[...] += _rows8(dpi)
        dprb = dpr.astype(BF16)
        dpib = dpi.astype(BF16)
        xcb = xc.astype(BF16)
        back = []
        for n in range(nb):
            sl = slice(n * RG_BLOCK, (n + 1) * RG_BLOCK)
            back.append(_dot_nt(dprb[:, sl], wr_ref[n]) + _dot_nt(dpib[:, sl], wi_ref[n]))
            gwr_ref[n] += _dot_tn(xcb[:, sl], dprb[:, sl])
            gwi_ref[n] += _dot_tn(xcb[:, sl], dpib[:, sl])
        dxc = dgx * i + (jnp.concatenate(back, axis=1) if nb > 1 else back[0])
        gcb_ref[...] += _rows8(dxc)

        dbuf[0:ts, :] = dxc
        dxa = None
        for jtap in range(CONV_TAPS):
            term = cw_ref[jtap:jtap + 1, :] * dbuf[pl.ds(CONV_TAPS - 1 - jtap, ts), :]
            dxa = term if dxa is None else dxa + term
            gcw_ref[jtap] += _rows8(dxc * xbuf[pl.ds(SUBLANES - (CONV_TAPS - 1) + jtap, ts), :])
        dxa_ref[...] = dxa.astype(BF16)
        dbuf[pl.ds(ts, SUBLANES), :] = dxc[0:SUBLANES, :]

    def tile_map(col):
        return lambda b, j: (b * nts + (nts - 1 - j), col)

    def prev8_map(col):
        return lambda b, j: (jnp.maximum((b * nts + (nts - 1 - j)) * t8 - 1, 0), col)

    vec = pl.BlockSpec((1, D), lambda b, j: (0, 0))
    gate = pl.BlockSpec((nb, RG_BLOCK, RG_BLOCK), lambda b, j: (0, 0, 0))
    acc8 = pl.BlockSpec((SUBLANES, D), lambda b, j: (0, 0))
    return pl.pallas_call(
        body, name="rg_bwd",
        grid=(B, nts),
        in_specs=[pl.BlockSpec((ts, D), tile_map(0)), pl.BlockSpec((SUBLANES, D), prev8_map(0)),
                  pl.BlockSpec((ts, D), tile_map(1)),
                  pl.BlockSpec((ts, D), tile_map(0)), pl.BlockSpec((SUBLANES, D), prev8_map(0)),
                  pl.BlockSpec((ts, D), tile_map(0)),
                  pl.BlockSpec((CONV_TAPS, D), lambda b, j: (0, 0)), vec, gate, vec, gate, vec, vec],
        out_specs=[pl.BlockSpec((ts, D), tile_map(0)), pl.BlockSpec((ts, D), tile_map(0)), gate, gate,
                   pl.BlockSpec((CONV_TAPS, SUBLANES, D), lambda b, j: (0, 0, 0)), acc8, acc8, acc8, acc8],
        out_shape=[jax.ShapeDtypeStruct((T, D), BF16)] * 2
        + [jax.ShapeDtypeStruct((nb, RG_BLOCK, RG_BLOCK), F32)] * 2
        + [jax.ShapeDtypeStruct((CONV_TAPS, SUBLANES, D), F32)] + [jax.ShapeDtypeStruct((SUBLANES, D), F32)] * 4,
        scratch_shapes=[pltpu.VMEM((SUBLANES + ts, D), F32), pltpu.VMEM((SUBLANES + ts, D), F32),
                        pltpu.VMEM((ts + SUBLANES, D), F32), pltpu.VMEM((ts + SUBLANES, D), F32),
                        pltpu.VMEM((ts, D), F32), pltpu.VMEM((ts, D), F32), pltpu.VMEM((SUBLANES, D), F32)],
        compiler_params=_cp(("arbitrary", "arbitrary")),
    )(proj, proj, proj, hrg, hrg, dya, cw, cb, wr, br, wi, bi, sp)


def _hg_bwd(proj, o_sv, dyb, states, a_sv, cum_sv, B, lb, gn, ride=None):
    T = proj.shape[0]
    D = proj.shape[1] // N_SEG
    S = T // B
    C = min(HG_CHUNK, S)
    NC = S // C
    H = D // HEAD
    hpd = D // HEAD
    spec = _hg_specs(B, NC, D, C, None)

    def body(q_ref, z_ref, v_ref, g_ref, o_ref, dyb_ref, st_ref, a_ref, cum_ref, lb_ref, gn_ref,
             dq_ref, dz_ref, dv_ref, dg_ref, glb_ref, ggn_ref, ds_scr, mask_scr):
        @pl.when(pl.program_id(2) == 0)
        def _():
            ds_scr[...] = jnp.zeros((HEAD, HEAD), F32)
            _hg_fill_masks(mask_scr, C)
            glb_ref[...] = jnp.zeros((SUBLANES, HEAD), F32)
            ggn_ref[...] = jnp.zeros((SUBLANES, HEAD), F32)

        q = q_ref[...]
        lb = lb_ref[...]
        gn = gn_ref[...]
        qf, qs, kf, lf, fg, sig = _hg_gates(q, z_ref[...], lb)
        cum = cum_ref[...]
        levels = _hg_levels(lf, cum, C)

        o = o_ref[...]
        g = g_ref[...]
        gs = _sig(g)
        rs = _rms(o)
        dyb = dyb_ref[...]
        don = dyb * (g * gs)
        dg_ref[...] = (dyb * (o * rs * gn) * (gs * (1.0 + g * (1.0 - gs)))).astype(BF16)
        ggn_ref[...] += _rows8(don * o * rs)
        dn = don * gn
        do = rs * (dn - o * (rs * rs) * jnp.mean(dn * o, axis=-1, keepdims=True))

        s_t = st_ref[...].astype(BF16)
        ds_t = ds_scr[...]
        ds_b = ds_t.astype(BF16)
        dob = do.astype(BF16)
        vb = v_ref[...].astype(BF16)
        ecum = jnp.exp(cum)
        last = jnp.sum(lf, axis=0, keepdims=True)
        eend = jnp.exp(last - cum)
        qhat = (qf * ecum).astype(BF16)
        kend = (kf * eend).astype(BF16)

        dA = _dot_nt(dob, vb)
        dq_inter = _dot(dob, s_t)
        dk_state = _dot(vb, ds_b)
        dqf = dq_inter * ecum
        dkf = dk_state * eend
        g_intra = None
        for n, (eq, ek, _) in enumerate(levels):
            qw = (qf * eq).astype(BF16)
            kw = (kf * ek).astype(BF16)
            dam = (dA * mask_scr[n - 1]).astype(BF16) if n > 0 else dA.astype(BF16)
            rq = _dot(dam, kw)
            rk = _dot_tn(dam, qw)
            dqf += rq * eq
            dkf += rk * ek
            gi = qw.astype(F32) * rq - kw.astype(F32) * rk
            g_intra = gi if g_intra is None else g_intra + gi
        dv_ref[...] = (_dot_tn(a_ref[...], dob) + _dot_nt(kend, ds_b)).astype(BF16)
        e_last = jnp.exp(last)
        ds_scr[...] = e_last * ds_t + _dot_tn(dob, qhat)

        ri = lax.broadcasted_iota(jnp.int32, (C, C), 0)
        ci = lax.broadcasted_iota(jnp.int32, (C, C), 1)
        y_state = kend.astype(F32) * dk_state
        dlf = (_dot_01(jnp.where(ci >= ri, 1.0, 0.0).astype(BF16), g_intra + qhat.astype(F32) * dq_inter - y_state)
               + jnp.sum(y_state, axis=0, keepdims=True)
               + jnp.sum(e_last * st_ref[...] * ds_t, axis=0, keepdims=True))
        dfg = jnp.where(fg > F_MIN, dlf / jnp.maximum(fg, F_MIN), 0.0)
        sneg = 1.0 - sig
        diff = dfg - dkf
        dz_ref[...] = ((1.0 - lb) * sig * sneg * diff).astype(BF16)
        glb_ref[...] += _rows8(sneg * diff)
        dq_ref[...] = (dqf * (qs * (1.0 + q * (1.0 - qs)))).astype(BF16)

    return _ride_call(
        body, ride, name="hg_bwd",
        grid=(B, H, NC),
        in_specs=[spec(2 * hpd, True), spec(3 * hpd, True), spec(4 * hpd, True), spec(5 * hpd, True),
                  spec(0, True), spec(0, True),
                  pl.BlockSpec((None, None, None, HEAD, HEAD), lambda b, h, j: (b, h, NC - 1 - j, 0, 0)),
                  pl.BlockSpec((None, None, None, C, C), lambda b, h, j: (b, h, NC - 1 - j, 0, 0)), spec(0, True),
                  pl.BlockSpec((1, HEAD), lambda b, h, j: (0, h)),
                  pl.BlockSpec((1, HEAD), lambda b, h, j: (0, 0))],
        out_specs=[spec(0, True)] * 4
        + [pl.BlockSpec((None, SUBLANES, HEAD), lambda b, h, j: (b, 0, h)),
           pl.BlockSpec((None, None, SUBLANES, HEAD), lambda b, h, j: (b, h, 0, 0))],
        out_shape=[jax.ShapeDtypeStruct((T, D), BF16)] * 4
        + [jax.ShapeDtypeStruct((B, SUBLANES, D), F32), jax.ShapeDtypeStruct((B, H, SUBLANES, HEAD), F32)],
        scratch_shapes=[pltpu.VMEM((HEAD, HEAD), F32), pltpu.VMEM((len(_hg_mask_blocks(C)), C, C), F32)],
        semantics=("parallel", "parallel", "arbitrary"),
        operands=(proj, proj, proj, proj, o_sv, dyb, states, a_sv, cum_sv, lb, gn))


def _inproj_bwd_x(dsegs, w_st, layer, x2, gain, dxm, ride=None):
    T, D = x2.shape
    tm = _tile(T, 512)
    nt = T // tm

    def body(*refs):
        seg_refs = refs[:N_SEG]
        w_ref, x_ref, g_ref, dxm_ref, dx_ref, dg_ref = refs[N_SEG:]
        k = pl.program_id(1)

        @pl.when(k == 0)
        def _():
            dx_ref[...] = jnp.zeros((tm, D), F32)

        for kk in range(N_SEG):
            @pl.when(k == kk)
            def _(kk=kk):
                dx_ref[...] += _dot_nt(seg_refs[kk][...], w_ref[...])

        @pl.when(k == N_SEG - 1)
        def _():
            x = x_ref[...]
            dxn, dg = _rms_bwd(dx_ref[...], x, _rms(x), g_ref[...])
            dx_ref[...] = dxm_ref[...] + dxn
            dg_ref[...] = dg

    row = pl.BlockSpec((tm, D), lambda i, k: (i, 0))

    def seg_spec(kk):
        return pl.BlockSpec((tm, D), lambda i, k: (jnp.minimum(i + jnp.where(k > kk, 1, 0), nt - 1), 0))

    return _ride_call(
        body, ride, name="inproj_bwd_x",
        grid=(nt, N_SEG),
        in_specs=[seg_spec(kk) for kk in range(N_SEG)]
        + [pl.BlockSpec((None, None, D, D), lambda i, k: (k // 2, layer, 0, k % 2)), row,
           pl.BlockSpec((1, D), lambda i, k: (0, 0)), row],
        out_specs=[row, pl.BlockSpec((None, SUBLANES, D), lambda i, k: (i, 0, 0))],
        out_shape=[jax.ShapeDtypeStruct((T, D), F32), jax.ShapeDtypeStruct((nt, SUBLANES, D), F32)],
        scratch_shapes=[],
        semantics=("parallel", "arbitrary"),
        operands=(*dsegs, w_st, x2, gain, dxm))


def _inproj_bwd_w(h, dsegs, layer, n_layers, bufs):
    T, D = h.shape
    tk = _tile(T, 1024)
    out_shape, extra = _layer_slot(bufs, [(D, 2 * D)], n_layers)

    def body(*refs):
        h_ref = refs[0]
        seg_refs = refs[1:1 + N_SEG]
        g_ref = refs[-1]
        k = pl.program_id(0)

        @pl.when(pl.program_id(1) == 0)
        def _():
            g_ref[...] = jnp.zeros((D, D), F32)

        for kk in range(N_SEG):
            @pl.when(k == kk)
            def _(kk=kk):
                g_ref[...] += _dot_tn(h_ref[...], seg_refs[kk][...])

    def seg_spec(kk):
        return pl.BlockSpec((tk, D), lambda k, t: (jnp.where(k == kk, t, 0), 0))

    return pl.pallas_call(
        body, name="inproj_bwd_w",
        grid=(N_SEG, T // tk),
        in_specs=[pl.BlockSpec((tk, D), lambda k, t: (t, 0))] + [seg_spec(kk) for kk in range(N_SEG)]
        + [ANY] * len(extra),
        out_specs=[pl.BlockSpec((None, None, D, D), lambda k, t: (k // 2, layer, 0, k % 2))],
        out_shape=out_shape,
        input_output_aliases={1 + N_SEG + i: i for i in range(len(extra))},
        compiler_params=_cp(("parallel", "arbitrary")),
    )(h, *dsegs, *extra)[0]


def _softmax_rows(lg_ref, L):
    rows = [lg_ref[l:l + 1, :] for l in range(L)]
    mx = functools.reduce(jnp.maximum, rows)
    es = [jnp.exp(r - mx) for r in rows]
    den = functools.reduce(lambda p, q: p + q, es)
    return [e / den for e in es]


def _prep(lb_logits, lam):
    L, D = lb_logits.shape

    def body(lg_ref, lam_ref, lowb_ref, sp_ref):
        sm = _softmax_rows(lg_ref, L)
        run = sm[0]
        for l in range(L):
            if l > 0:
                run = run + sm[l]
            lowb_ref[l:l + 1, :] = jnp.clip(run - sm[0], 0.0, 1.0)
        y = -lam_ref[...]
        sp_ref[...] = jnp.maximum(y, 0.0) + jnp.log1p(jnp.exp(-jnp.abs(y)))

    return pl.pallas_call(
        body, name="prep_small",
        out_shape=[jax.ShapeDtypeStruct((L, D), F32)] * 2,
    )(lb_logits, lam)


def _local_step(x, tgt, lowb, sp, norm_mix, wbufs, conv_w, conv_b, w_r, b_r, w_i, b_i, hg_norm,
                norm_mlp, norm_final):
    B, S, D = x.shape
    L = norm_mix.shape[0]
    T = B * S
    x2 = x.reshape(T, D)
    row = lambda a, l: a[l:l + 1]

    def weight_views(bufs):
        f4 = bufs[2].shape[-1]
        return (bufs[0].reshape(N_SHARD, L, D, 2 * D), bufs[1].reshape(N_SHARD, L, D // N_SHARD, D),
                bufs[2].reshape(N_SHARD, L, D, f4), bufs[3].reshape(N_SHARD, L, f4, D))

    w_in_st, w_out_st, w_up_st, w_down_st = weight_views(wbufs)
    saved = []
    for l in range(L):
        proj, h = _inproj_fwd(x2, row(norm_mix, l), w_in_st, l)
        ya, hrg = _rg_fwd(proj, B, conv_w[l], row(conv_b, l), w_r[l], row(b_r, l), w_i[l], row(b_i, l), row(sp, l))
        l0, nl = (1, min(2, L - 1)) if l == 0 else (l + 2, 1 if l + 2 < L else 0)
        if nl > 0:
            yb, o, st, a_sv, cum_sv, *wbufs = _hg_fwd(proj, B, row(lowb, l), row(hg_norm, l),
                                                      _gather_ride(list(wbufs), LINK_SPLIT, l0, nl))
            w_in_st, w_out_st, w_up_st, w_down_st = weight_views(wbufs)
        else:
            yb, o, st, a_sv, cum_sv = _hg_fwd(proj, B, row(lowb, l), row(hg_norm, l))
        xm, ymix = _out_fwd(ya, yb, proj, x2, w_out_st, l)
        xo, up, h2 = _mlp_fwd(xm, row(norm_mlp, l), w_up_st, w_down_st, l)
        saved.append((x2, proj, h, ya, hrg, yb, o, (st, a_sv, cum_sv), xm, ymix, up, h2))
        x2 = xo
    loss_parts, dx, g_nf = _final_loss(x2, norm_final[None, :], tgt.reshape(T, D))

    def reduce_part(g_in, g_out, g_mlp, gate_list):
        gate_list = gate_list[::-1]
        grads = [g_in, g_out, g_mlp[0], g_mlp[1], _shard_gate(jnp.stack([g[0] for g in gate_list])),
                 _shard_gate(jnp.stack([g[1] for g in gate_list]))]
        return _ReduceScatter([g.reshape(N_SHARD, -1, g.shape[-1]) for g in grads], LINK_SPLIT)

    gates = []
    small = []
    g_in = g_out = g_mlp = None
    rest = None
    for l in reversed(range(L)):
        x_in, proj, h, ya, hrg, yb, o, st, xm, ymix, up, h2 = saved[l]
        alone = l == 0 and L > 1
        if alone:
            rest = reduce_part(g_in, g_out, g_mlp, gates)
            g_in = g_out = g_mlp = None
            gates = []
        slot, n_slots = (0, 1) if l == 0 else (l - 1, L - 1)
        dxm, dup, g_nmlp, dxb = _mlp_bwd_x(dx, xm, up, row(norm_mlp, l), w_up_st, w_down_st, l)
        g_mlp = _mlp_bwd_w(up, dxb, h2, dup, slot, n_slots, g_mlp)
        dya, dyb, dma, dmb = _out_bwd_x(dxm, ya, yb, proj, w_out_st, l)
        g_out = _out_bwd_w(ymix, dxm, slot, n_slots, None if g_out is None else [g_out])
        dxa, dga, g_wr, g_wi, g_cw, g_cb, g_br, g_bi, g_sp = _rg_bwd(
            proj, hrg, dya, B, conv_w[l], row(conv_b, l), w_r[l], row(b_r, l), w_i[l], row(b_i, l), row(sp, l))
        dq, dz, dv, dg, g_lb, g_gn, *got = _hg_bwd(proj, o, dyb, *st, B, row(lowb, l), row(hg_norm, l),
                                                   rest.ride_1() if alone else None)
        if alone:
            rest.after_1(got)
        dsegs = (dxa, dga, dq, dz, dv, dg, dma, dmb)
        dx, g_nmix, *got = _inproj_bwd_x(dsegs, w_in_st, l, x_in, row(norm_mix, l), dxm,
                                         rest.ride_2() if alone else None)
        if alone:
            rest.after_2(got)
        g_in = _inproj_bwd_w(h, dsegs, slot, n_slots, None if g_in is None else [g_in])
        gates.append((g_wr, g_wi))
        small.append((g_lb, g_nmix, g_cb, g_br, g_bi, g_sp, g_nmlp, g_gn, g_cw))
    small.reverse()
    first = reduce_part(g_in, g_out, g_mlp, gates)
    first.exchange_1()
    first.exchange_2()
    parts = [first.finish()] + ([rest.finish()] if rest is not None else [])
    return loss_parts, dx.reshape(B, S, D), parts, small, g_nf


def _me():
    return lax.axis_index("x"), lax.axis_index("y"), lax.axis_index("c")


def _cast_place(w, slot):
    R, N = w.shape
    tr = _tile(R, max(16, (1 << 20) // N))

    def body(slot_ref, w_ref, o_ref):
        o_ref[...] = w_ref[...].astype(BF16)

    return pl.pallas_call(
        body, name="cast_place",
        grid_spec=pltpu.PrefetchScalarGridSpec(
            num_scalar_prefetch=1, grid=(R // tr,),
            in_specs=[pl.BlockSpec((tr, N), lambda i, slot: (i, 0))],
            out_specs=pl.BlockSpec((None, tr, N), lambda i, slot: (slot[0], i, 0))),
        out_shape=jax.ShapeDtypeStruct((N_SHARD, R, N), BF16),
        compiler_params=_cp(("parallel",)),
    )(slot, w)


def _gather_weights(bufs, first_axes, l0, nl):
    n = len(bufs)
    phases = _gather_phases(n, first_axes, l0, nl)

    def body(*refs):
        outs = refs[n:2 * n]
        ssem, rsem = refs[2 * n:]
        for ph in phases:
            ph(outs, ssem, rsem)

    return pl.pallas_call(
        body, name="gather_weights",
        in_specs=[ANY] * n, out_specs=[ANY] * n,
        out_shape=[jax.ShapeDtypeStruct(b.shape, b.dtype) for b in bufs],
        input_output_aliases={a: a for a in range(n)},
        scratch_shapes=_gather_sems(n),
        compiler_params=pltpu.CompilerParams(has_side_effects=True),
    )(*bufs)


def _gather_sems(n):
    return [pltpu.SemaphoreType.DMA((n, 6)), pltpu.SemaphoreType.DMA((n, 6))]


def _gather_phases(n, first_axes, l0, nl):
    def ctx(outs, ssem, rsem):
        x, y, c = _me()

        def piece(a, flips, half):
            sx = 1 - x if flips[0] else x
            sy = 1 - y if flips[1] else y
            return outs[a].at[2 * sx + sy, pl.ds(l0[a], nl[a]), half]

        def rcopy(a, k, ref, dev):
            return pltpu.make_async_remote_copy(src_ref=ref, dst_ref=ref, send_sem=ssem.at[a, k], recv_sem=rsem.at[a, k],
                                                device_id=dev, device_id_type=MESH_ID)

        def route(a):
            fx = first_axes[a] == "x"
            f_dev = (1 - x, y, c) if fx else (x, 1 - y, c)
            g_dev = (x, 1 - y, c) if fx else (1 - x, y, c)
            return f_dev, g_dev, ((1, 0) if fx else (0, 1)), ((0, 1) if fx else (1, 0))

        return c, (x, y, 1 - c), piece, rcopy, route

    def own_halves(outs, ssem, rsem):
        c, sib, piece, rcopy, route = ctx(outs, ssem, rsem)
        for a in range(n):
            f_dev, g_dev, _, _ = route(a)
            own = piece(a, (0, 0), c)
            rcopy(a, 0, own, f_dev).start()
            rcopy(a, 1, own, g_dev).start()

    def pass_on_neighbours(outs, ssem, rsem):
        c, sib, piece, rcopy, route = ctx(outs, ssem, rsem)
        for a in range(n):
            f_dev, g_dev, f_flip, _ = route(a)
            got = piece(a, f_flip, c)
            rcopy(a, 0, got, f_dev).wait_recv()
            rcopy(a, 2, got, g_dev).start()
            rcopy(a, 3, got, sib).start()
        for a in range(n):
            _, g_dev, _, g_flip = route(a)
            got = piece(a, g_flip, c)
            rcopy(a, 1, got, g_dev).wait_recv()
            rcopy(a, 4, got, sib).start()

    def pass_on_diagonal(outs, ssem, rsem):
        c, sib, piece, rcopy, route = ctx(outs, ssem, rsem)
        for a in range(n):
            _, g_dev, _, _ = route(a)
            got = piece(a, (1, 1), c)
            rcopy(a, 2, got, g_dev).wait_recv()
            rcopy(a, 5, got, sib).start()

    def drain(outs, ssem, rsem):
        c, sib, piece, rcopy, route = ctx(outs, ssem, rsem)
        for a in range(n):
            f_dev, g_dev, f_flip, g_flip = route(a)
            for k, fl in ((3, f_flip), (4, g_flip), (5, (1, 1))):
                rcopy(a, k, piece(a, fl, 1 - c), sib).wait_recv()
            own = piece(a, (0, 0), c)
            for k, dev in ((0, f_dev), (1, g_dev), (2, g_dev), (3, sib), (4, sib), (5, sib)):
                rcopy(a, k, own, dev).wait_send()

    return [own_halves, pass_on_neighbours, pass_on_diagonal, drain]


def _gather_ride(bufs, first_axes, l0, nl):
    n = len(bufs)
    phases = [lambda reads, refs, outs, ssem, rsem, ph=ph: ph(refs, ssem, rsem)
              for ph in _gather_phases(n, first_axes, [l0] * n, [nl] * n)]
    return _Ride([], bufs, [], phases, GATHER_STEPS, _gather_sems(n))


def _shard_gate(g):
    n_l, nb = g.shape[:2]
    return g.reshape(n_l, nb, N_SHARD, RG_BLOCK // N_SHARD, RG_BLOCK).transpose(2, 0, 1, 3, 4)


def _exchange(arrs, axes, name):
    n = len(arrs)

    def body(*refs):
        ins, outs = refs[:n], refs[n:2 * n]
        ssem, rsem = refs[2 * n:]
        x, y, c = _me()
        cps = []
        for a in range(n):
            my = {"x": x, "y": y, "c": c}[axes[a]]
            partner = {"x": (1 - x, y, c), "y": (x, 1 - y, c), "c": (x, y, 1 - c)}[axes[a]]
            cps.append(pltpu.make_async_remote_copy(
                src_ref=ins[a].at[:, 1 - my], dst_ref=outs[a], send_sem=ssem.at[a], recv_sem=rsem.at[a],
                device_id=partner, device_id_type=MESH_ID))
            cps[-1].start()
        for cp in cps:
            cp.wait()

    return pl.pallas_call(
        body, name=name,
        in_specs=[ANY] * n, out_specs=[ANY] * n,
        out_shape=[jax.ShapeDtypeStruct((a.shape[0],) + a.shape[2:], a.dtype) for a in arrs],
        scratch_shapes=[pltpu.SemaphoreType.DMA((n,)), pltpu.SemaphoreType.DMA((n,))],
        compiler_params=pltpu.CompilerParams(has_side_effects=True),
    )(*arrs)


def _add_kept(arr, got, idx, name, with_bf16):
    P, _, R, N = arr.shape
    tr = _tile(R, max(16, (1 << 20) // N))

    def body(idx_ref, a_ref, g_ref, o_ref, *ob_ref):
        s = a_ref[...] + g_ref[...].astype(F32)
        o_ref[...] = s
        if with_bf16:
            ob_ref[0][...] = s.astype(BF16)

    out_blk = pl.BlockSpec((None, tr, N), lambda p, i, idx: (p, i, 0))
    return pl.pallas_call(
        body, name=name,
        grid_spec=pltpu.PrefetchScalarGridSpec(
            num_scalar_prefetch=1, grid=(P, R // tr),
            in_specs=[pl.BlockSpec((None, None, tr, N), lambda p, i, idx: (p, idx[0], i, 0)),
                      pl.BlockSpec((None, tr, N), lambda p, i, idx: (p, i, 0))],
            out_specs=[out_blk] * (2 if with_bf16 else 1)),
        out_shape=[jax.ShapeDtypeStruct((P, R, N), F32)] + ([jax.ShapeDtypeStruct((P, R, N), BF16)] if with_bf16 else []),
        compiler_params=_cp(("parallel", "parallel")),
    )(idx, arr, got)


def _share_halves(halves):
    n = len(halves)

    def body(*refs):
        ins, outs = refs[:n], refs[n:2 * n]
        ssem, rsem = refs[2 * n:]
        x, y, c = _me()
        cps = []
        for a in range(n):
            cps.append(pltpu.make_async_remote_copy(
                src_ref=ins[a], dst_ref=outs[a], send_sem=ssem.at[a], recv_sem=rsem.at[a],
                device_id=(x, y, 1 - c), device_id_type=MESH_ID))
            cps[-1].start()
        for cp in cps:
            cp.wait()

    return pl.pallas_call(
        body, name="share_halves",
        in_specs=[ANY] * n, out_specs=[ANY] * n,
        out_shape=[jax.ShapeDtypeStruct(h.shape, h.dtype) for h in halves],
        scratch_shapes=[pltpu.SemaphoreType.DMA((n,)), pltpu.SemaphoreType.DMA((n,))],
        compiler_params=pltpu.CompilerParams(has_side_effects=True),
    )(*halves)


def _exchange_ride(arrs, axes):
    n = len(arrs)

    def copies(reads, outs, ssem, rsem):
        x, y, c = _me()
        cps = []
        for a in range(n):
            my = {"x": x, "y": y, "c": c}[axes[a]]
            partner = {"x": (1 - x, y, c), "y": (x, 1 - y, c), "c": (x, y, 1 - c)}[axes[a]]
            cps.append(pltpu.make_async_remote_copy(
                src_ref=reads[a].at[:, 1 - my], dst_ref=outs[a], send_sem=ssem.at[a], recv_sem=rsem.at[a],
                device_id=partner, device_id_type=MESH_ID))
        return cps

    def start(reads, bufs, outs, ssem, rsem):
        for cp in copies(reads, outs, ssem, rsem):
            cp.start()

    def finish(reads, bufs, outs, ssem, rsem):
        for cp in copies(reads, outs, ssem, rsem):
            cp.wait()

    landing = [jax.ShapeDtypeStruct((a.shape[0],) + a.shape[2:], a.dtype) for a in arrs]
    return _Ride(arrs, [], landing, [start, finish], (0.0, 1.0),
                 [pltpu.SemaphoreType.DMA((n,)), pltpu.SemaphoreType.DMA((n,))])


class _ReduceScatter:
    def __init__(self, grads, first_axes):
        x, y, c = _me()
        idx = lambda v: jnp.reshape(v, (1,)).astype(jnp.int32)
        self.coord = {"x": idx(x), "y": idx(y), "c": idx(c)}
        self.first = list(first_axes)
        self.second = ["y" if f == "x" else "x" for f in first_axes]
        views = [g.reshape(N_SHARD, 2, g.shape[1] // 2, g.shape[2]) for g in grads]
        got = _exchange(views, "c" * len(grads), "rs_exchange_c")
        self.summed = [_add_kept(v, r, self.coord["c"], "rs_add_c", True) for v, r in zip(views, got)]

    @staticmethod
    def _split_view(a, ax):
        _, rh, nn = a.shape
        return a.reshape(1, 2, 2 * rh, nn) if ax == "x" else a.reshape(2, 2, rh, nn)

    def _views_1(self):
        return [self._split_view(s[1], f) for s, f in zip(self.summed, self.first)]

    def ride_1(self):
        return _exchange_ride(self._views_1(), self.first)

    def exchange_1(self):
        self.after_1(_exchange(self._views_1(), self.first, "rs_exchange_1"))

    def after_1(self, got):
        self.summed = [_add_kept(self._split_view(s[0], f), r, self.coord[f], "rs_add_1", True)
                       for s, r, f in zip(self.summed, got, self.first)]

    def _views_2(self):
        return [s[1].reshape(1, 2, -1, s[1].shape[-1]) for s in self.summed]

    def ride_2(self):
        return _exchange_ride(self._views_2(), self.second)

    def exchange_2(self):
        self.after_2(_exchange(self._views_2(), self.second, "rs_exchange_2"))

    def after_2(self, got):
        views32 = [s[0].reshape(1, 2, -1, s[0].shape[-1]) for s in self.summed]
        self.kept = [_add_kept(v, r, self.coord[g], "rs_add_2", False)[0][0]
                     for v, r, g in zip(views32, got, self.second)]

    def finish(self):
        return self.kept, _share_halves(self.kept)


def _allgather_small(p):
    R, D = p.shape

    def body(p_ref, o_ref, ssem, rsem):
        x, y, c = _me()
        me = 4 * x + 2 * y + c
        o_ref[me] = p_ref[...]
        cps = []
        for m in range(1, 8):
            mx, my, mc = (m >> 2) & 1, (m >> 1) & 1, m & 1
            peer = (1 - x if mx else x, 1 - y if my else y, 1 - c if mc else c)
            cps.append(pltpu.make_async_remote_copy(
                src_ref=p_ref, dst_ref=o_ref.at[me], send_sem=ssem.at[m - 1], recv_sem=rsem.at[m - 1],
                device_id=peer, device_id_type=MESH_ID))
            cps[-1].start()
        for cp in cps:
            cp.wait()

    return pl.pallas_call(
        body, name="allgather_small",
        in_specs=[pl.BlockSpec(memory_space=pltpu.VMEM)],
        out_specs=pl.BlockSpec(memory_space=pltpu.VMEM),
        out_shape=jax.ShapeDtypeStruct((8, R, D), p.dtype),
        scratch_shapes=[pltpu.SemaphoreType.DMA((7,)), pltpu.SemaphoreType.DMA((7,))],
        compiler_params=pltpu.CompilerParams(has_side_effects=True, vmem_limit_bytes=VMEM_LIMIT),
    )(p)


def _adam_math(w, g, m, v):
    m = ADAM_B1 * m + (1.0 - ADAM_B1) * g
    v = ADAM_B2 * v + (1.0 - ADAM_B2) * (g * g)
    m_hat = m / (1.0 - ADAM_B1 ** ADAM_STEP)
    v_hat = v / (1.0 - ADAM_B2 ** ADAM_STEP)
    delta = -ADAM_LR * (m_hat / (jnp.sqrt(v_hat) + ADAM_EPS) + ADAM_WD * w)
    return delta, m, v


def _adam(w, g_mine, g_sib, m, v, core, row0, layer_rows, outs):
    R, N = w.shape
    rh = g_mine.shape[0]
    tr = _tile(layer_rows // 2, max(16, (1 << 19) // N))
    nt = rh // tr
    t0 = row0 // tr
    extra = [] if outs is None else list(outs)

    def body(c_ref, w_ref, gm_ref, gs_ref, m_ref, v_ref, *rest):
        g_ref, d_ref, nm_ref, nv_ref = rest[-4:]
        g = jnp.where(pl.program_id(0) == c_ref[0], gm_ref[...], gs_ref[...])
        d, nm, nv = _adam_math(w_ref[...], g, m_ref[...], v_ref[...])
        g_ref[...] = g
        d_ref[...] = d
        nm_ref[...] = nm
        nv_ref[...] = nv

    blk = pl.BlockSpec((tr, N), lambda h, i, c: (t0 + h * nt + i, 0))
    half = pl.BlockSpec((tr, N), lambda h, i, c: (i, 0))
    return pl.pallas_call(
        body, name="adamw",
        grid_spec=pltpu.PrefetchScalarGridSpec(
            num_scalar_prefetch=1, grid=(2, nt),
            in_specs=[blk, half, half, blk, blk] + [ANY] * len(extra), out_specs=[blk] * 4),
        out_shape=[jax.ShapeDtypeStruct((R, N), F32)] * 4,
        input_output_aliases={6 + i: i for i in range(len(extra))},
        compiler_params=_cp(("parallel", "parallel")),
    )(core, w, g_mine, g_sib, m, v, *extra)


def _reduce_rows(parts, sizes, rows_out):
    D = parts.shape[1]

    def body(p_ref, o_ref):
        o_ref[...] = jnp.zeros((rows_out, D), F32)
        off = 0
        for i, sz in enumerate(sizes):
            o_ref[i:i + 1, :] = jnp.sum(p_ref[off:off + sz, :], axis=0, keepdims=True)
            off += sz

    return pl.pallas_call(
        body, name="reduce_rows",
        out_shape=jax.ShapeDtypeStruct((rows_out, D), F32),
        compiler_params=pltpu.CompilerParams(vmem_limit_bytes=VMEM_LIMIT),
    )(parts)


def _sum_devices(g8):
    _, R, D = g8.shape

    def body(g_ref, o_ref):
        tot = g_ref[0]
        for k in range(1, 8):
            tot = tot + g_ref[k]
        o_ref[...] = tot

    return pl.pallas_call(
        body, name="sum_devices",
        out_shape=jax.ShapeDtypeStruct((R, D), F32),
        compiler_params=pltpu.CompilerParams(vmem_limit_bytes=VMEM_LIMIT),
    )(g8)


def _small_update(gathered, w, m, v, L):
    _, R, D = gathered.shape

    def body(g8_ref, w_ref, m_ref, v_ref, g_ref, d_ref, nm_ref, nv_ref):
        tot = g8_ref[0]
        for k in range(1, 8):
            tot = tot + g8_ref[k]
        g_ref[...] = tot
        sm = _softmax_rows(w_ref, L)
        run = sm[0]
        dcum = []
        for l in range(L):
            if l > 0:
                run = run + sm[l]
            cum = run - sm[0]
            dcum.append(jnp.where((cum > 0.0) & (cum < 1.0), g_ref[l:l + 1, :], 0.0))
        dsm = [jnp.zeros((1, D), F32)]
        for i in range(1, L):
            dsm.append(functools.reduce(lambda p, q: p + q, dcum[i:]))
        dot = functools.reduce(lambda p, q: p + q, [s * d for s, d in zip(sm, dsm)])
        for l in range(L):
            g_ref[l:l + 1, :] = sm[l] * (dsm[l] - dot)
        lam = w_ref[5 * L:6 * L, :]
        g_ref[5 * L:6 * L, :] = g_ref[5 * L:6 * L, :] * (-_sig(-lam))
        d, nm, nv = _adam_math(w_ref[...], g_ref[...], m_ref[...], v_ref[...])
        d_ref[...] = d
        nm_ref[...] = nm
        nv_ref[...] = nv

    return pl.pallas_call(
        body, name="small_update",
        out_shape=[jax.ShapeDtypeStruct((R, D), F32)] * 4,
        compiler_params=pltpu.CompilerParams(vmem_limit_bytes=VMEM_LIMIT),
    )(gathered, w, m, v)


def kernel(x, lb_logits, norm_mix, w_in, conv_w, conv_b, w_r, b_r, w_i, b_i, lam, hg_norm, w_out, norm_mlp, w_up, w_down, norm_final, loss_target, m_lb_logits, m_norm_mix, m_w_in, m_conv_w, m_conv_b, m_w_r, m_b_r, m_w_i, m_b_i, m_lam, m_hg_norm, m_w_out, m_norm_mlp, m_w_up, m_w_down, m_norm_final, v_lb_logits, v_norm_mix, v_w_in, v_conv_w, v_conv_b, v_w_r, v_b_r, v_w_i, v_b_i, v_lam, v_hg_norm, v_w_out, v_norm_mlp, v_w_up, v_w_down, v_norm_final):
    B, S, D = x.shape
    L = norm_mix.shape[0]
    nb = D // RG_BLOCK
    Dq = D // N_SHARD
    mx, my, mc = _me()
    shard = 2 * mx + my

    big_w = (w_in, w_out, w_up, w_down, w_r, w_i)
    flat2 = lambda a: a.reshape(-1, a.shape[-1])
    slot = jnp.reshape(shard, (1,)).astype(jnp.int32)
    def place(w):
        b = _cast_place(flat2(w), slot)
        return b.reshape(N_SHARD, L, 2, b.shape[1] // (2 * L), b.shape[2])

    *wbufs, g_r, g_i = _gather_weights([place(w) for w in big_w], LINK_SPLIT, [0] * 6, [1] * 4 + [L] * 2)
    unshard_gate = lambda g: g.reshape(N_SHARD, L, nb, RG_BLOCK // N_SHARD, RG_BLOCK).transpose(1, 2, 0, 3, 4).reshape(
        L, nb, RG_BLOCK, RG_BLOCK)
    w_r_full, w_i_full = unshard_gate(g_r), unshard_gate(g_i)

    R_LB, R_NMIX, R_CB, R_BR, R_BI, R_LAM, R_NMLP, R_GN, R_CW, R_NF, R_LOSS = (
        0, L, 2 * L, 3 * L, 4 * L, 5 * L, 6 * L, 7 * L, 8 * L, 12 * L, 12 * L + 1)
    n_rows = 12 * L + 2
    rows_pad = n_rows + (-n_rows) % SUBLANES

    def place_cols(a):
        return lax.dynamic_update_slice(jnp.zeros((a.shape[0], D), F32), a, (0, shard * Dq))

    def pack_small(lb_, nmix_, cb_, br_, bi_, lam_, nmlp_, gn_, cw_, nf_):
        gn_pad = jnp.pad(gn_, ((0, 0), (0, D - HEAD)))
        rows = [lb_, nmix_, cb_, br_, bi_, lam_, nmlp_, gn_pad, place_cols(cw_.reshape(L * CONV_TAPS, Dq)),
                nf_[None, :], jnp.zeros((rows_pad - n_rows + 1, D), F32)]
        return jnp.concatenate(rows, axis=0)

    w_small = pack_small(lb_logits, norm_mix, conv_b, b_r, b_i, lam, norm_mlp, hg_norm, conv_w, norm_final)
    m_small = pack_small(m_lb_logits, m_norm_mix, m_conv_b, m_b_r, m_b_i, m_lam, m_norm_mlp, m_hg_norm, m_conv_w,
                         m_norm_final)
    v_small = pack_small(v_lb_logits, v_norm_mix, v_conv_b, v_b_r, v_b_i, v_lam, v_norm_mlp, v_hg_norm, v_conv_w,
                         v_norm_final)
    cw_rows = place_cols(conv_w.reshape(L * CONV_TAPS, Dq)) * jnp.where(mc == 0, 1.0, 0.0)
    conv_w_full = _sum_devices(_allgather_small(cw_rows)).reshape(L, CONV_TAPS, D)

    lowb, sp = _prep(lb_logits, lam)

    loss_parts, grad_x, parts, small, g_nf = _local_step(
        x, loss_target, lowb, sp, norm_mix, wbufs, conv_w_full, conv_b, w_r_full, b_r, w_i_full, b_i, hg_norm,
        norm_mlp, norm_final)

    core = jnp.reshape(mc, (1,)).astype(jnp.int32)
    outs = {}
    for a, (name, w, m, v) in enumerate(zip(("w_in", "w_out", "w_up", "w_down", "w_r", "w_i"), big_w,
                                            (m_w_in, m_w_out, m_w_up, m_w_down, m_w_r, m_w_i),
                                            (v_w_in, v_w_out, v_w_up, v_w_down, v_w_r, v_w_i))):
        layer_rows = flat2(w).shape[0] // L
        done, row0 = None, 0
        for mine, sibs in parts:
            done = _adam(flat2(w), mine[a], sibs[a], flat2(m), flat2(v), core, row0, layer_rows, done)
            row0 += 2 * mine[a].shape[0]
        outs[name] = tuple(t.reshape(w.shape) for t in done)

    parts, sizes = [], []

    def add_rows(a):
        a = a.reshape(-1, a.shape[-1])
        if a.shape[1] != D:
            a = jnp.pad(a, ((0, 0), (0, D - a.shape[1])))
        parts.append(a)
        sizes.append(a.shape[0])

    for i in range(8):
        for l in range(L):
            add_rows(small[l][i])
    for l in range(L):
        for j in range(CONV_TAPS):
            add_rows(small[l][8][j])
    add_rows(g_nf)
    loss_rows = loss_parts[:, 0:1, :]
    add_rows(jnp.where(lax.broadcasted_iota(jnp.int32, loss_rows.shape, 2) == 0, loss_rows, 0.0))
    g_small = _reduce_rows(jnp.concatenate(parts, axis=0), sizes, rows_pad)
    g_small, d_small, nm_small, nv_small = _small_update(_allgather_small(g_small), w_small, m_small, v_small, L)

    def unpack(t):
        take_cols = lambda a: lax.dynamic_slice(a, (0, shard * Dq), (a.shape[0], Dq))
        return {"lb_logits": t[R_LB:R_LB + L], "norm_mix": t[R_NMIX:R_NMIX + L], "conv_b": t[R_CB:R_CB + L],
                "b_r": t[R_BR:R_BR + L], "b_i": t[R_BI:R_BI + L], "lam": t[R_LAM:R_LAM + L],
                "norm_mlp": t[R_NMLP:R_NMLP + L], "hg_norm": t[R_GN:R_GN + L, :HEAD],
                "conv_w": take_cols(t[R_CW:R_CW + L * CONV_TAPS]).reshape(L, CONV_TAPS, Dq), "norm_final": t[R_NF]}

    small_out = [unpack(t) for t in (g_small, d_small, nm_small, nv_small)]
    loss = g_small[R_LOSS, 0]
    names = ("lb_logits", "norm_mix", "w_in", "conv_w", "conv_b", "w_r", "b_r", "w_i", "b_i", "lam", "hg_norm",
             "w_out", "norm_mlp", "w_up", "w_down", "norm_final")
    result = [loss, grad_x]
    for kind in range(4):
        for nme in names:
            result.append(outs[nme][kind] if nme in outs else small_out[kind][nme])
    return tuple(result)
```

```python
import functools
import math

import jax
import jax.numpy as jnp
from jax import lax
from jax.experimental import pallas as pl
from jax.experimental.pallas import tpu as pltpu

F32 = jnp.float32
BF16 = jnp.bfloat16

HEAD = 128
RG_BLOCK = 256
CONV_TAPS = 4
RG_C = 8.0
F_MIN = 1e-30
NORM_EPS = 1e-6
N_SEG = 8
N_SHARD = 4
HG_CHUNK = 256
RG_TILE = 256
ADAM_LR, ADAM_B1, ADAM_B2, ADAM_EPS, ADAM_WD, ADAM_STEP = 0.001, 0.9, 0.999, 1e-08, 0.01, 10
V7X_VMEM_BYTES = 64 * 1024 * 1024
VMEM_LIMIT = V7X_VMEM_BYTES - 8 * 1024 * 1024
SUBLANES = 8
LINK_SPLIT = "xxyyyy"
GATHER_STEPS = (0.0, 0.6, 0.88, 1.0)
MESH_ID = pl.DeviceIdType.MESH
ANY = pl.BlockSpec(memory_space=pl.ANY)


def _cp(sem):
    return pltpu.CompilerParams(dimension_semantics=sem, vmem_limit_bytes=VMEM_LIMIT)


def _dot(a, b):
    return jnp.dot(a, b, preferred_element_type=F32)


def _dot_nt(a, b):
    return lax.dot_general(a, b, (((1,), (1,)), ((), ())), preferred_element_type=F32)


def _dot_tn(a, b):
    return lax.dot_general(a, b, (((0,), (0,)), ((), ())), preferred_element_type=F32)


def _dot_01(m01, x):
    n = x.shape[1]
    hi = x.astype(BF16)
    r1 = x - hi.astype(F32)
    mid = r1.astype(BF16)
    lo = (r1 - mid.astype(F32)).astype(BF16)
    y = _dot(m01, jnp.concatenate([hi, mid, lo], axis=1))
    return y[:, :n] + y[:, n:2 * n] + y[:, 2 * n:]


def _sig(x):
    return jax.nn.sigmoid(x)


def _rows8(x):
    return x.reshape(x.shape[0] // SUBLANES, SUBLANES, x.shape[1]).sum(axis=0)


def _tile(n, cap):
    if n <= cap:
        return n
    t = cap - cap % 16
    while n % t:
        t -= 16
    return t


_GELU_C = math.sqrt(2.0 / math.pi)


def _gelu_and_grad(x):
    x2 = x * x
    t = jnp.tanh(_GELU_C * (x + 0.044715 * x * x2))
    g = 0.5 * x * (1.0 + t)
    dg = 0.5 * (1.0 + t) + 0.5 * x * (1.0 - t * t) * (_GELU_C * (1.0 + 3.0 * 0.044715 * x2))
    return g, dg


def _rms(x):
    return lax.rsqrt(jnp.mean(x * x, axis=-1, keepdims=True) + NORM_EPS)


def _rms_bwd(dh, x, rs, gain):
    xhat = x * rs
    dxhat = dh * gain
    dx = rs * (dxhat - xhat * jnp.mean(dxhat * xhat, axis=-1, keepdims=True))
    return dx, _rows8(dh * xhat)


def _inproj_fwd(x2, gain, w_st, layer):
    T, D = x2.shape
    tm = _tile(T, 2048)

    def body(x_ref, g_ref, w_ref, o_ref, h_ref):
        @pl.when(pl.program_id(1) == 0)
        def _():
            x = x_ref[...]
            h_ref[...] = (x * _rms(x) * g_ref[...]).astype(BF16)
        o_ref[...] = _dot(h_ref[...], w_ref[...])

    return pl.pallas_call(
        body, name="inproj_fwd",
        grid=(T // tm, N_SEG),
        in_specs=[pl.BlockSpec((tm, D), lambda i, k: (i, 0)),
                  pl.BlockSpec((1, D), lambda i, k: (0, 0)),
                  pl.BlockSpec((None, None, D, D), lambda i, k: (k // 2, layer, 0, k % 2))],
        out_specs=[pl.BlockSpec((tm, D), lambda i, k: (i, k)),
                   pl.BlockSpec((tm, D), lambda i, k: (i, 0))],
        out_shape=[jax.ShapeDtypeStruct((T, N_SEG * D), F32), jax.ShapeDtypeStruct((T, D), BF16)],
        compiler_params=_cp(("parallel", "arbitrary")),
    )(x2, gain, w_st)


def _rg_gates(xc, wr_ref, br, wi_ref, bi, sp):
    D = xc.shape[1]
    xcb = xc.astype(BF16)
    pr, pi = [], []
    for n in range(D // RG_BLOCK):
        blk = xcb[:, n * RG_BLOCK:(n + 1) * RG_BLOCK]
        pr.append(_dot(blk, wr_ref[n]))
        pi.append(_dot(blk, wi_ref[n]))
    r = _sig(jnp.concatenate(pr, axis=1) + br) if len(pr) > 1 else _sig(pr[0] + br)
    i = _sig(jnp.concatenate(pi, axis=1) + bi) if len(pi) > 1 else _sig(pi[0] + bi)
    la = (-RG_C) * r * sp
    a = jnp.exp(la)
    y = 2.0 * la
    one_m_e2 = jnp.where(y > -1e-2, -(y * (1.0 + 0.5 * y * (1.0 + y * (1.0 / 3.0)))), 1.0 - jnp.exp(y))
    mult = jnp.sqrt(jnp.maximum(one_m_e2, 0.0))
    return r, i, a, mult


def _conv_taps(xbuf, cw_ref, ts):
    acc = None
    for j in range(CONV_TAPS):
        term = cw_ref[j:j + 1, :] * xbuf[pl.ds(SUBLANES - (CONV_TAPS - 1) + j, ts), :]
        acc = term if acc is None else acc + term
    return acc


def _rg_fwd(proj, B, cw, cb, wr, br, wi, bi, sp):
    T = proj.shape[0]
    D = proj.shape[1] // N_SEG
    S = T // B
    ts = _tile(S, RG_TILE)
    nts = S // ts
    nb = D // RG_BLOCK

    def body(xa_ref, ga_ref, cw_ref, cb_ref, wr_ref, br_ref, wi_ref, bi_ref, sp_ref,
             ya_ref, h_ref, xbuf, a_scr, u_scr, carry):
        @pl.when(pl.program_id(1) == 0)
        def _():
            xbuf[0:SUBLANES, :] = jnp.zeros((SUBLANES, D), F32)
            carry[...] = jnp.zeros((SUBLANES, D), F32)

        xbuf[pl.ds(SUBLANES, ts), :] = xa_ref[...]
        xc = _conv_taps(xbuf, cw_ref, ts) + cb_ref[...]
        r, i, a, mult = _rg_gates(xc, wr_ref, br_ref[...], wi_ref, bi_ref[...], sp_ref[...])
        a_scr[...] = a
        u_scr[...] = mult * (i * xc)
        row8 = lax.broadcasted_iota(jnp.int32, (SUBLANES, 1), 0)

        def blk(n, hprev):
            off = pl.multiple_of(n * SUBLANES, SUBLANES)
            a8 = a_scr[pl.ds(off, SUBLANES), :]
            u8 = u_scr[pl.ds(off, SUBLANES), :]
            for d in (1, 2, 4):
                m = row8 >= d
                ap = jnp.where(m, pltpu.roll(a8, d, 0), 1.0)
                up = jnp.where(m, pltpu.roll(u8, d, 0), 0.0)
                u8 = a8 * up + u8
                a8 = a8 * ap
            h8 = u8 + a8 * hprev
            u_scr[pl.ds(off, SUBLANES), :] = h8
            last = jnp.sum(jnp.where(row8 == SUBLANES - 1, h8, 0.0), axis=0, keepdims=True)
            return jnp.broadcast_to(last, (SUBLANES, D))

        carry[...] = lax.fori_loop(0, ts // SUBLANES, blk, carry[...])
        h = u_scr[...]
        h_ref[...] = h
        g, _ = _gelu_and_grad(ga_ref[...])
        ya_ref[...] = h * g
        xbuf[0:SUBLANES, :] = xa_ref[pl.ds(ts - SUBLANES, SUBLANES), :]

    vec = pl.BlockSpec((1, D), lambda b, j: (0, 0))
    gate = pl.BlockSpec((nb, RG_BLOCK, RG_BLOCK), lambda b, j: (0, 0, 0))
    return pl.pallas_call(
        body, name="rg_fwd",
        grid=(B, nts),
        in_specs=[pl.BlockSpec((ts, D), lambda b, j: (b * nts + j, 0)),
                  pl.BlockSpec((ts, D), lambda b, j: (b * nts + j, 1)),
                  pl.BlockSpec((CONV_TAPS, D), lambda b, j: (0, 0)), vec, gate, vec, gate, vec, vec],
        out_specs=[pl.BlockSpec((ts, D), lambda b, j: (b * nts + j, 0))] * 2,
        out_shape=[jax.ShapeDtypeStruct((T, D), F32)] * 2,
        scratch_shapes=[pltpu.VMEM((SUBLANES + ts, D), F32), pltpu.VMEM((ts, D), F32),
                        pltpu.VMEM((ts, D), F32), pltpu.VMEM((SUBLANES, D), F32)],
        compiler_params=_cp(("arbitrary", "arbitrary")),
    )(proj, proj, cw, cb, wr, br, wi, bi, sp)


def _hg_gates(q, z, lb):
    sig = _sig(z)
    one_m = 1.0 - lb
    fg = lb + one_m * sig
    lf = jnp.log(jnp.maximum(fg, F_MIN))
    kf = one_m * (1.0 - sig)
    qs = _sig(q)
    return q * qs, qs, kf, lf, fg, sig


def _hg_cum(lf, C):
    ri = lax.broadcasted_iota(jnp.int32, (C, C), 0)
    ci = lax.broadcasted_iota(jnp.int32, (C, C), 1)
    return _dot_01(jnp.where(ci <= ri, 1.0, 0.0).astype(BF16), lf)


def _hg_levels(lf, cum, C):
    row = lax.broadcasted_iota(jnp.int32, (C, 1), 0)
    levels = []
    w = C // 2
    while w >= 4:
        blk = 2 * w
        upper = (row & w) != 0
        ref = jnp.min(jnp.where(upper, 0.0, cum).reshape(C // blk, blk, HEAD), axis=1, keepdims=True)
        ref = jnp.broadcast_to(ref, (C // blk, blk, HEAD)).reshape(C, HEAD)
        d = cum - ref
        levels.append(jnp.exp(jnp.where(upper, d, -d)))
        w //= 2
    r4 = row & 3
    lf_prev = pltpu.roll(lf, 1, 0)
    lf_next = pltpu.roll(lf, C - 1, 0)
    levels.append(jnp.exp(jnp.where(r4 == 3, lf + lf_prev, jnp.where(r4 == 2, lf, jnp.where(r4 == 0, lf_next, 0.0)))))
    levels.append(jnp.exp(jnp.where((row & 1) == 1, lf, 0.0)))
    levels.append(None)
    return levels


def _hg_level_blocks(C):
    blks = []
    w = C // 2
    while w >= 4:
        blks.append(2 * w)
        w //= 2
    return blks + [4, 2, 1]


def _hg_fill_masks(mask_scr, C):
    ri = lax.broadcasted_iota(jnp.int32, (C, C), 0)
    ci = lax.broadcasted_iota(jnp.int32, (C, C), 1)
    for i, blk in enumerate(_hg_level_blocks(C)):
        if blk == 1:
            keep = ri == ci
        else:
            shift, w = blk.bit_length() - 1, blk // 2
            keep = ((ri >> shift) == (ci >> shift)) & ((ri & w) != 0) & ((ci & w) == 0)
        mask_scr[i] = jnp.where(keep, 1.0, 0.0).astype(F32)


def _hg_operands(qf, kf, e):
    if e is None:
        return qf.astype(BF16), kf.astype(BF16)
    return (qf * e).astype(BF16), (kf * e).astype(BF16)


def _hg_scores(qf, kf, levels, mask_scr):
    A = None
    for n, e in enumerate(levels):
        a = _dot_nt(*_hg_operands(qf, kf, e)) * mask_scr[n]
        A = a if A is None else A + a
    return A


def _hg_specs(B, NC, D, C, dtype_blocks):
    def spec(col0, rev):
        if rev:
            return pl.BlockSpec((C, HEAD), lambda b, h, j: (b * NC + (NC - 1 - j), col0 + h))
        return pl.BlockSpec((C, HEAD), lambda b, h, j: (b * NC + j, col0 + h))
    return spec


class _Ride:
    def __init__(self, reads, bufs, outs, phases, fractions, sems):
        self.reads, self.bufs, self.outs = list(reads), list(bufs), list(outs)
        self.phases, self.fractions, self.sems = list(phases), list(fractions), list(sems)


def _ride_call(body, ride, *, name, grid, in_specs, out_specs, out_shape, scratch_shapes, semantics, operands):
    if ride is None:
        return pl.pallas_call(body, name=name, grid=grid, in_specs=in_specs, out_specs=out_specs, out_shape=out_shape,
                              scratch_shapes=scratch_shapes, compiler_params=_cp(semantics))(*operands)
    n_in, n_out, n_scr = len(in_specs), len(out_specs), len(scratch_shapes)
    nr, nb, no = len(ride.reads), len(ride.bufs), len(ride.outs)
    last_step = math.prod(grid) - 1

    def full_body(*refs):
        own = refs[:n_in] + refs[n_in + nr + nb:n_in + nr + nb + n_out]
        tail = refs[n_in + nr + nb + n_out:]
        ride_refs = (refs[n_in:n_in + nr], tail[:nb], tail[nb:nb + no])
        scr = tail[nb + no:]
        step = pl.program_id(0)
        for d in range(1, len(grid)):
            step = step * grid[d] + pl.program_id(d)
        for phase, frac in zip(ride.phases, ride.fractions):
            @pl.when(step == int(round(frac * last_step)))
            def _(phase=phase):
                phase(*ride_refs, *scr[n_scr:])
        body(*own, *scr[:n_scr])

    return pl.pallas_call(
        full_body, name=name, grid=grid,
        in_specs=list(in_specs) + [ANY] * (nr + nb),
        out_specs=list(out_specs) + [ANY] * (nb + no),
        out_shape=list(out_shape) + [jax.ShapeDtypeStruct(b.shape, b.dtype) for b in ride.bufs] + ride.outs,
        input_output_aliases={n_in + nr + j: n_out + j for j in range(nb)},
        scratch_shapes=list(scratch_shapes) + ride.sems,
        compiler_params=_cp(("arbitrary",) * len(grid)),
    )(*operands, *ride.reads, *ride.bufs)


def _hg_fwd(proj, B, lb, gn, ride=None):
    T = proj.shape[0]
    D = proj.shape[1] // N_SEG
    S = T // B
    C = min(HG_CHUNK, S)
    NC = S // C
    H = D // HEAD
    hpd = D // HEAD
    spec = _hg_specs(B, NC, D, C, None)

    def body(q_ref, z_ref, v_ref, g_ref, lb_ref, gn_ref, yb_ref, o_ref, st_ref, a_ref, cum_ref, st_scr, mask_scr):
        @pl.when(pl.program_id(2) == 0)
        def _():
            st_scr[...] = jnp.zeros((HEAD, HEAD), F32)
            _hg_fill_masks(mask_scr, C)

        s_t = st_scr[...]
        st_ref[...] = s_t
        qf, _, kf, lf, _, _ = _hg_gates(q_ref[...], z_ref[...], lb_ref[...])
        cum = _hg_cum(lf, C)
        cum_ref[...] = cum
        A = _hg_scores(qf, kf, _hg_levels(lf, cum, C), mask_scr).astype(BF16)
        a_ref[...] = A
        vb = v_ref[...].astype(BF16)
        o = _dot_nt((qf * jnp.exp(cum)).astype(BF16), s_t.astype(BF16)) + _dot(A, vb)
        last = jnp.sum(lf, axis=0, keepdims=True)
        kend = kf * jnp.exp(last - cum)
        st_scr[...] = jnp.exp(last) * s_t + _dot_tn(vb, kend.astype(BF16))
        o_ref[...] = o
        g = g_ref[...]
        yb_ref[...] = (o * _rms(o) * gn_ref[...]) * (g * _sig(g))

    return _ride_call(
        body, ride, name="hg_fwd",
        grid=(B, H, NC),
        in_specs=[spec(2 * hpd, False), spec(3 * hpd, False), spec(4 * hpd, False), spec(5 * hpd, False),
                  pl.BlockSpec((1, HEAD), lambda b, h, j: (0, h)),
                  pl.BlockSpec((1, HEAD), lambda b, h, j: (0, 0))],
        out_specs=[spec(0, False), spec(0, False),
                   pl.BlockSpec((None, None, None, HEAD, HEAD), lambda b, h, j: (b, h, j, 0, 0)),
                   pl.BlockSpec((None, None, None, C, C), lambda b, h, j: (b, h, j, 0, 0)), spec(0, False)],
        out_shape=[jax.ShapeDtypeStruct((T, D), F32), jax.ShapeDtypeStruct((T, D), F32),
                   jax.ShapeDtypeStruct((B, H, NC, HEAD, HEAD), F32),
                   jax.ShapeDtypeStruct((B, H, NC, C, C), BF16), jax.ShapeDtypeStruct((T, D), F32)],
        scratch_shapes=[pltpu.VMEM((HEAD, HEAD), F32), pltpu.VMEM((len(_hg_level_blocks(C)), C, C), F32)],
        semantics=("parallel", "parallel", "arbitrary"),
        operands=(proj, proj, proj, proj, lb, gn))


def _w_full(ref):
    s, r, c = ref.shape
    return ref[...].reshape(s * r, c)


def _out_fwd(ya, yb, proj, x2, w_st, layer):
    T, D = x2.shape
    tm = _tile(T, 512)

    def body(ya_ref, yb_ref, ma_ref, mb_ref, x_ref, w_ref, xm_ref, y_ref):
        y = (_sig(ma_ref[...]) * ya_ref[...] + _sig(mb_ref[...]) * yb_ref[...]).astype(BF16)
        y_ref[...] = y
        xm_ref[...] = x_ref[...] + _dot(y, _w_full(w_ref))

    row = pl.BlockSpec((tm, D), lambda i: (i, 0))
    return pl.pallas_call(
        body, name="out_fwd",
        grid=(T // tm,),
        in_specs=[row, row, pl.BlockSpec((tm, D), lambda i: (i, 6)), pl.BlockSpec((tm, D), lambda i: (i, 7)), row,
                  pl.BlockSpec((N_SHARD, None, D // N_SHARD, D), lambda i: (0, layer, 0, 0))],
        out_specs=[row, row],
        out_shape=[jax.ShapeDtypeStruct((T, D), F32), jax.ShapeDtypeStruct((T, D), BF16)],
        compiler_params=_cp(("parallel",)),
    )(ya, yb, proj, proj, x2, w_st)


def _mlp_fwd(xm, gain, wup_st, wdn_st, layer):
    T, D = xm.shape
    F4 = wup_st.shape[3]
    tm = _tile(T, 1024)

    def body(x_ref, g_ref, wu_ref, wd_ref, xo_ref, up_ref, h_ref):
        @pl.when(pl.program_id(1) == 0)
        def _():
            x = x_ref[...]
            h_ref[...] = (x * _rms(x) * g_ref[...]).astype(BF16)
            xo_ref[...] = x
        up = _dot(h_ref[...], wu_ref[...])
        up_ref[...] = up.astype(BF16)
        act = jnp.maximum(up, 0.0)
        xo_ref[...] += _dot((act * act).astype(BF16), wd_ref[...])

    row = pl.BlockSpec((tm, D), lambda i, s: (i, 0))
    return pl.pallas_call(
        body, name="mlp_fwd",
        grid=(T // tm, N_SHARD),
        in_specs=[row, pl.BlockSpec((1, D), lambda i, s: (0, 0)),
                  pl.BlockSpec((None, None, D, F4), lambda i, s: (s, layer, 0, 0)),
                  pl.BlockSpec((None, None, F4, D), lambda i, s: (s, layer, 0, 0))],
        out_specs=[row, pl.BlockSpec((tm, F4), lambda i, s: (i, s)), row],
        out_shape=[jax.ShapeDtypeStruct((T, D), F32), jax.ShapeDtypeStruct((T, N_SHARD * F4), BF16),
                   jax.ShapeDtypeStruct((T, D), BF16)],
        compiler_params=_cp(("parallel", "arbitrary")),
    )(xm, gain, wup_st, wdn_st)


def _final_loss(x2, gain, tgt):
    T, D = x2.shape
    tm = _tile(T, 512)
    nt = T // tm

    def body(x_ref, g_ref, t_ref, loss_ref, dx_ref, dg_ref):
        x = x_ref[...]
        rs = _rms(x)
        err = x * rs * g_ref[...] - t_ref[...]
        part = 0.5 * jnp.sum(jnp.sum(err * err, axis=-1, keepdims=True) * (1.0 / D), axis=0, keepdims=True)
        loss_ref[...] = jnp.broadcast_to(part, (SUBLANES, 128))
        dx, dg = _rms_bwd(err * (1.0 / D), x, rs, g_ref[...])
        dx_ref[...] = dx
        dg_ref[...] = dg

    row = pl.BlockSpec((tm, D), lambda i: (i, 0))
    return pl.pallas_call(
        body, name="final_loss",
        grid=(nt,),
        in_specs=[row, pl.BlockSpec((1, D), lambda i: (0, 0)), row],
        out_specs=[pl.BlockSpec((None, SUBLANES, 128), lambda i: (i, 0, 0)), row,
                   pl.BlockSpec((None, SUBLANES, D), lambda i: (i, 0, 0))],
        out_shape=[jax.ShapeDtypeStruct((nt, SUBLANES, 128), F32), jax.ShapeDtypeStruct((T, D), F32),
                   jax.ShapeDtypeStruct((nt, SUBLANES, D), F32)],
        compiler_params=_cp(("parallel",)),
    )(x2, gain, tgt)


def _mlp_bwd_x(dx, xm, up, gain, wup_st, wdn_st, layer, ride=None):
    T, D = xm.shape
    F4 = wup_st.shape[3]
    tm = _tile(T, 1024)
    nt = T // tm

    def body(dx_ref, x_ref, up_ref, g_ref, wu_ref, wd_ref, dxm_ref, dup_ref, dg_ref, dxb):
        s = pl.program_id(1)

        @pl.when(s == 0)
        def _():
            dxb[...] = dx_ref[...].astype(BF16)
            dxm_ref[...] = jnp.zeros((tm, D), F32)

        d_act = _dot_nt(dxb[...], wd_ref[...])
        d_up = (d_act * (2.0 * jnp.maximum(up_ref[...].astype(F32), 0.0))).astype(BF16)
        dup_ref[...] = d_up
        dxm_ref[...] += _dot_nt(d_up, wu_ref[...])

        @pl.when(s == N_SHARD - 1)
        def _():
            x = x_ref[...]
            dxn, dg = _rms_bwd(dxm_ref[...], x, _rms(x), g_ref[...])
            dxm_ref[...] = dx_ref[...] + dxn
            dg_ref[...] = dg

    row = pl.BlockSpec((tm, D), lambda i, s: (i, 0))
    return _ride_call(
        body, ride, name="mlp_bwd_x",
        grid=(nt, N_SHARD),
        in_specs=[row, row, pl.BlockSpec((tm, F4), lambda i, s: (i, s)), pl.BlockSpec((1, D), lambda i, s: (0, 0)),
                  pl.BlockSpec((None, None, D, F4), lambda i, s: (s, layer, 0, 0)),
                  pl.BlockSpec((None, None, F4, D), lambda i, s: (s, layer, 0, 0))],
        out_specs=[row, pl.BlockSpec((tm, F4), lambda i, s: (i, s)),
                   pl.BlockSpec((None, SUBLANES, D), lambda i, s: (i, 0, 0)), row],
        out_shape=[jax.ShapeDtypeStruct((T, D), F32), jax.ShapeDtypeStruct((T, N_SHARD * F4), BF16),
                   jax.ShapeDtypeStruct((nt, SUBLANES, D), F32), jax.ShapeDtypeStruct((T, D), BF16)],
        scratch_shapes=[],
        semantics=("parallel", "arbitrary"),
        operands=(dx, xm, up, gain, wup_st, wdn_st))


def _layer_slot(bufs, shapes, n_layers):
    out_shape = [jax.ShapeDtypeStruct((N_SHARD, n_layers) + s, F32) for s in shapes]
    return out_shape, ([] if bufs is None else list(bufs))


def _mlp_bwd_w(up, dxb, h, dup, layer, n_layers, bufs):
    T, D = dxb.shape
    F4 = up.shape[1] // N_SHARD
    tk = _tile(T, 1024)
    out_shape, extra = _layer_slot(bufs, [(D, F4), (F4, D)], n_layers)

    def body(up_ref, dx_ref, h_ref, dup_ref, *rest):
        gu_ref, gd_ref = rest[-2:]

        @pl.when(pl.program_id(1) == 0)
        def _():
            gu_ref[...] = jnp.zeros((D, F4), F32)
            gd_ref[...] = jnp.zeros((F4, D), F32)
        act = jnp.maximum(up_ref[...], 0.0)
        gd_ref[...] += _dot_tn(act * act, dx_ref[...])
        gu_ref[...] += _dot_tn(h_ref[...], dup_ref[...])

    return pl.pallas_call(
        body, name="mlp_bwd_w",
        grid=(N_SHARD, T // tk),
        in_specs=[pl.BlockSpec((tk, F4), lambda s, t: (t, s)), pl.BlockSpec((tk, D), lambda s, t: (t, 0)),
                  pl.BlockSpec((tk, D), lambda s, t: (t, 0)), pl.BlockSpec((tk, F4), lambda s, t: (t, s))]
        + [ANY] * len(extra),
        out_specs=[pl.BlockSpec((None, None, D, F4), lambda s, t: (s, layer, 0, 0)),
                   pl.BlockSpec((None, None, F4, D), lambda s, t: (s, layer, 0, 0))],
        out_shape=out_shape,
        input_output_aliases={4 + i: i for i in range(len(extra))},
        compiler_params=_cp(("parallel", "arbitrary")),
    )(up, dxb, h, dup, *extra)


def _out_bwd_x(dxm, ya, yb, proj, w_st, layer):
    T, D = dxm.shape
    tm = _tile(T, 512)

    def body(dx_ref, ya_ref, yb_ref, ma_ref, mb_ref, w_ref, dya_ref, dyb_ref, dma_ref, dmb_ref):
        dy = _dot_nt(dx_ref[...].astype(BF16), _w_full(w_ref))
        sa = _sig(ma_ref[...])
        sb = _sig(mb_ref[...])
        dya_ref[...] = dy * sa
        dyb_ref[...] = dy * sb
        dma_ref[...] = (dy * ya_ref[...] * (sa * (1.0 - sa))).astype(BF16)
        dmb_ref[...] = (dy * yb_ref[...] * (sb * (1.0 - sb))).astype(BF16)

    row = pl.BlockSpec((tm, D), lambda i: (i, 0))
    return pl.pallas_call(
        body, name="out_bwd_x",
        grid=(T // tm,),
        in_specs=[row, row, row, pl.BlockSpec((tm, D), lambda i: (i, 6)), pl.BlockSpec((tm, D), lambda i: (i, 7)),
                  pl.BlockSpec((N_SHARD, None, D // N_SHARD, D), lambda i: (0, layer, 0, 0))],
        out_specs=[row] * 4,
        out_shape=[jax.ShapeDtypeStruct((T, D), F32)] * 2 + [jax.ShapeDtypeStruct((T, D), BF16)] * 2,
        compiler_params=_cp(("parallel",)),
    )(dxm, ya, yb, proj, proj, w_st)


def _out_bwd_w(ymix, dxm, layer, n_layers, bufs):
    T, D = dxm.shape
    tk = _tile(T, 1024)
    out_shape, extra = _layer_slot(bufs, [(D // N_SHARD, D)], n_layers)

    def body(y_ref, dx_ref, *rest):
        g_ref = rest[-1]

        @pl.when(pl.program_id(0) == 0)
        def _():
            g_ref[...] = jnp.zeros((N_SHARD, D // N_SHARD, D), F32)
        g = _dot_tn(y_ref[...], dx_ref[...].astype(BF16))
        g_ref[...] += g.reshape(N_SHARD, D // N_SHARD, D)

    row = pl.BlockSpec((tk, D), lambda t: (t, 0))
    return pl.pallas_call(
        body, name="out_bwd_w",
        grid=(T // tk,),
        in_specs=[row, row] + [ANY] * len(extra),
        out_specs=[pl.BlockSpec((N_SHARD, None, D // N_SHARD, D), lambda t: (0, layer, 0, 0))],
        out_shape=out_shape,
        input_output_aliases={2 + i: i for i in range(len(extra))},
        compiler_params=_cp(("arbitrary",)),
    )(ymix, dxm, *extra)[0]


def _rg_bwd(proj, hrg, dya, B, cw, cb, wr, br, wi, bi, sp):
    T = proj.shape[0]
    D = proj.shape[1] // N_SEG
    S = T // B
    ts = _tile(S, RG_TILE)
    nts = S // ts
    nb = D // RG_BLOCK
    t8 = ts // SUBLANES

    def body(xa_ref, xp_ref, ga_ref, h_ref, hp_ref, dya_ref, cw_ref, cb_ref, wr_ref, br_ref, wi_ref, bi_ref, sp_ref,
             dxa_ref, dga_ref, gwr_ref, gwi_ref, gcw_ref, gcb_ref, gbr_ref, gbi_ref, gsp_ref,
             xbuf, hbuf, abuf, dbuf, g_scr, c_scr, gcar):
        b = pl.program_id(0)
        j = pl.program_id(1)
        first_in_time = j == nts - 1

        @pl.when((b == 0) & (j == 0))
        def _():
            gwr_ref[...] = jnp.zeros((nb, RG_BLOCK, RG_BLOCK), F32)
            gwi_ref[...] = jnp.zeros((nb, RG_BLOCK, RG_BLOCK), F32)
            gcw_ref[...] = jnp.zeros((CONV_TAPS, SUBLANES, D), F32)
            for r in (gcb_ref, gbr_ref, gbi_ref, gsp_ref):
                r[...] = jnp.zeros((SUBLANES, D), F32)

        @pl.when(j == 0)
        def _():
            abuf[pl.ds(ts, SUBLANES), :] = jnp.zeros((SUBLANES, D), F32)
            dbuf[pl.ds(ts, SUBLANES), :] = jnp.zeros((SUBLANES, D), F32)
            gcar[...] = jnp.zeros((SUBLANES, D), F32)

        keep = jnp.where(first_in_time, 0.0, 1.0)
        xbuf[0:SUBLANES, :] = xp_ref[...] * keep
        xbuf[pl.ds(SUBLANES, ts), :] = xa_ref[...]
        hbuf[0:SUBLANES, :] = hp_ref[...] * keep
        hbuf[pl.ds(SUBLANES, ts), :] = h_ref[...]

        xc = _conv_taps(xbuf, cw_ref, ts) + cb_ref[...]
        sp = sp_ref[...]
        r, i, a, mult = _rg_gates(xc, wr_ref, br_ref[...], wi_ref, bi_ref[...], sp)
        g_gate, dg_gate = _gelu_and_grad(ga_ref[...])
        dya = dya_ref[...]
        dga_ref[...] = (dya * h_ref[...] * dg_gate).astype(BF16)

        abuf[0:ts, :] = a
        c_scr[...] = abuf[pl.ds(1, ts), :]
        g_scr[...] = dya * g_gate
        row8 = lax.broadcasted_iota(jnp.int32, (SUBLANES, 1), 0)

        def blk(n, gnext):
            off = pl.multiple_of((t8 - 1 - n) * SUBLANES, SUBLANES)
            c8 = c_scr[pl.ds(off, SUBLANES), :]
            d8 = g_scr[pl.ds(off, SUBLANES), :]
            for d in (1, 2, 4):
                m = row8 < SUBLANES - d
                cn = jnp.where(m, pltpu.roll(c8, SUBLANES - d, 0), 1.0)
                dn = jnp.where(m, pltpu.roll(d8, SUBLANES - d, 0), 0.0)
                d8 = d8 + c8 * dn
                c8 = c8 * cn
            g8 = d8 + c8 * gnext
            g_scr[pl.ds(off, SUBLANES), :] = g8
            first = jnp.sum(jnp.where(row8 == 0, g8, 0.0), axis=0, keepdims=True)
            return jnp.broadcast_to(first, (SUBLANES, D))

        gcar[...] = lax.fori_loop(0, t8, blk, gcar[...])
        abuf[pl.ds(ts, SUBLANES), :] = a[0:SUBLANES, :]

        g = g_scr[...]
        hprev = hbuf[pl.ds(SUBLANES - 1, ts), :]
        gx = i * xc
        e2 = a * a
        dla = g * hprev * a - jnp.where(mult > 0.0, g * gx * e2 / jnp.where(mult > 0.0, mult, 1.0), 0.0)
        dgx = g * mult
        dpr = (dla * ((-RG_C) * sp)) * (r * (1.0 - r))
        dpi = (dgx * xc) * (i * (1.0 - i))
        gsp_ref[...] += _rows8(dla * ((-RG_C) * r))
        gbr_ref[...] += _rows8(dpr)
        gbi_ref[...] += _rows8(dpi)
        dprb = dpr.astype(BF16)
        dpib = dpi.astype(BF16)
        xcb = xc.astype(BF16)
        back = []
        for n in range(nb):
            sl = slice(n * RG_BLOCK, (n + 1) * RG_BLOCK)
            back.append(_dot_nt(dprb[:, sl], wr_ref[n]) + _dot_nt(dpib[:, sl], wi_ref[n]))
            gwr_ref[n] += _dot_tn(xcb[:, sl], dprb[:, sl])
            gwi_ref[n] += _dot_tn(xcb[:, sl], dpib[:, sl])
        dxc = dgx * i + (jnp.concatenate(back, axis=1) if nb > 1 else back[0])
        gcb_ref[...] += _rows8(dxc)

        dbuf[0:ts, :] = dxc
        dxa = None
        for jtap in range(CONV_TAPS):
            term = cw_ref[jtap:jtap + 1, :] * dbuf[pl.ds(CONV_TAPS - 1 - jtap, ts), :]
            dxa = term if dxa is None else dxa + term
            gcw_ref[jtap] += _rows8(dxc * xbuf[pl.ds(SUBLANES - (CONV_TAPS - 1) + jtap, ts), :])
        dxa_ref[...] = dxa.astype(BF16)
        dbuf[pl.ds(ts, SUBLANES), :] = dxc[0:SUBLANES, :]

    def tile_map(col):
        return lambda b, j: (b * nts + (nts - 1 - j), col)

    def prev8_map(col):
        return lambda b, j: (jnp.maximum((b * nts + (nts - 1 - j)) * t8 - 1, 0), col)

    vec = pl.BlockSpec((1, D), lambda b, j: (0, 0))
    gate = pl.BlockSpec((nb, RG_BLOCK, RG_BLOCK), lambda b, j: (0, 0, 0))
    acc8 = pl.BlockSpec((SUBLANES, D), lambda b, j: (0, 0))
    return pl.pallas_call(
        body, name="rg_bwd",
        grid=(B, nts),
        in_specs=[pl.BlockSpec((ts, D), tile_map(0)), pl.BlockSpec((SUBLANES, D), prev8_map(0)),
                  pl.BlockSpec((ts, D), tile_map(1)),
                  pl.BlockSpec((ts, D), tile_map(0)), pl.BlockSpec((SUBLANES, D), prev8_map(0)),
                  pl.BlockSpec((ts, D), tile_map(0)),
                  pl.BlockSpec((CONV_TAPS, D), lambda b, j: (0, 0)), vec, gate, vec, gate, vec, vec],
        out_specs=[pl.BlockSpec((ts, D), tile_map(0)), pl.BlockSpec((ts, D), tile_map(0)), gate, gate,
                   pl.BlockSpec((CONV_TAPS, SUBLANES, D), lambda b, j: (0, 0, 0)), acc8, acc8, acc8, acc8],
        out_shape=[jax.ShapeDtypeStruct((T, D), BF16)] * 2
        + [jax.ShapeDtypeStruct((nb, RG_BLOCK, RG_BLOCK), F32)] * 2
        + [jax.ShapeDtypeStruct((CONV_TAPS, SUBLANES, D), F32)] + [jax.ShapeDtypeStruct((SUBLANES, D), F32)] * 4,
        scratch_shapes=[pltpu.VMEM((SUBLANES + ts, D), F32), pltpu.VMEM((SUBLANES + ts, D), F32),
                        pltpu.VMEM((ts + SUBLANES, D), F32), pltpu.VMEM((ts + SUBLANES, D), F32),
                        pltpu.VMEM((ts, D), F32), pltpu.VMEM((ts, D), F32), pltpu.VMEM((SUBLANES, D), F32)],
        compiler_params=_cp(("arbitrary", "arbitrary")),
    )(proj, proj, proj, hrg, hrg, dya, cw, cb, wr, br, wi, bi, sp)


def _hg_bwd(proj, o_sv, dyb, states, a_sv, cum_sv, B, lb, gn, ride=None):
    T = proj.shape[0]
    D = proj.shape[1] // N_SEG
    S = T // B
    C = min(HG_CHUNK, S)
    NC = S // C
    H = D // HEAD
    hpd = D // HEAD
    spec = _hg_specs(B, NC, D, C, None)

    def body(q_ref, z_ref, v_ref, g_ref, o_ref, dyb_ref, st_ref, a_ref, cum_ref, lb_ref, gn_ref,
             dq_ref, dz_ref, dv_ref, dg_ref, glb_ref, ggn_ref, ds_scr, mask_scr):
        @pl.when(pl.program_id(2) == 0)
        def _():
            ds_scr[...] = jnp.zeros((HEAD, HEAD), F32)
            _hg_fill_masks(mask_scr, C)
            glb_ref[...] = jnp.zeros((SUBLANES, HEAD), F32)
            ggn_ref[...] = jnp.zeros((SUBLANES, HEAD), F32)

        q = q_ref[...]
        lb = lb_ref[...]
        gn = gn_ref[...]
        qf, qs, kf, lf, fg, sig = _hg_gates(q, z_ref[...], lb)
        cum = cum_ref[...]
        levels = _hg_levels(lf, cum, C)

        o = o_ref[...]
        g = g_ref[...]
        gs = _sig(g)
        rs = _rms(o)
        dyb = dyb_ref[...]
        don = dyb * (g * gs)
        dg_ref[...] = (dyb * (o * rs * gn) * (gs * (1.0 + g * (1.0 - gs)))).astype(BF16)
        ggn_ref[...] += _rows8(don * o * rs)
        dn = don * gn
        do = rs * (dn - o * (rs * rs) * jnp.mean(dn * o, axis=-1, keepdims=True))

        s_t = st_ref[...].astype(BF16)
        ds_t = ds_scr[...]
        ds_b = ds_t.astype(BF16)
        dob = do.astype(BF16)
        vb = v_ref[...].astype(BF16)
        ecum = jnp.exp(cum)
        last = jnp.sum(lf, axis=0, keepdims=True)
        eend = jnp.exp(last - cum)
        qhat = (qf * ecum).astype(BF16)
        kend = (kf * eend).astype(BF16)

        dA = _dot_nt(dob, vb)
        dq_inter = _dot(dob, s_t)
        dk_state = _dot(vb, ds_b)
        dqf = dq_inter * ecum
        dkf = dk_state * eend
        g_intra = None
        for n, e in enumerate(levels):
            qw, kw = _hg_operands(qf, kf, e)
            dam = (dA * mask_scr[n]).astype(BF16)
            rq = _dot(dam, kw)
            rk = _dot_tn(dam, qw)
            dqf += rq if e is None else rq * e
            dkf += rk if e is None else rk * e
            gi = qw.astype(F32) * rq - kw.astype(F32) * rk
            g_intra = gi if g_intra is None else g_intra + gi
        dv_ref[...] = (_dot_tn(a_ref[...], dob) + _dot_nt(kend, ds_b)).astype(BF16)
        e_last = jnp.exp(last)
        ds_scr[...] = e_last * ds_t + _dot_tn(dob, qhat)

        ri = lax.broadcasted_iota(jnp.int32, (C, C), 0)
        ci = lax.broadcasted_iota(jnp.int32, (C, C), 1)
        y_state = kend.astype(F32) * dk_state
        dlf = (_dot_01(jnp.where(ci >= ri, 1.0, 0.0).astype(BF16), g_intra + qhat.astype(F32) * dq_inter - y_state)
               + jnp.sum(y_state, axis=0, keepdims=True)
               + jnp.sum(e_last * st_ref[...] * ds_t, axis=0, keepdims=True))
        dfg = jnp.where(fg > F_MIN, dlf / jnp.maximum(fg, F_MIN), 0.0)
        sneg = 1.0 - sig
        diff = dfg - dkf
        dz_ref[...] = ((1.0 - lb) * sig * sneg * diff).astype(BF16)
        glb_ref[...] += _rows8(sneg * diff)
        dq_ref[...] = (dqf * (qs * (1.0 + q * (1.0 - qs)))).astype(BF16)

    return _ride_call(
        body, ride, name="hg_bwd",
        grid=(B, H, NC),
        in_specs=[spec(2 * hpd, True), spec(3 * hpd, True), spec(4 * hpd, True), spec(5 * hpd, True),
                  spec(0, True), spec(0, True),
                  pl.BlockSpec((None, None, None, HEAD, HEAD), lambda b, h, j: (b, h, NC - 1 - j, 0, 0)),
                  pl.BlockSpec((None, None, None, C, C), lambda b, h, j: (b, h, NC - 1 - j, 0, 0)), spec(0, True),
                  pl.BlockSpec((1, HEAD), lambda b, h, j: (0, h)),
                  pl.BlockSpec((1, HEAD), lambda b, h, j: (0, 0))],
        out_specs=[spec(0, True)] * 4
        + [pl.BlockSpec((None, SUBLANES, HEAD), lambda b, h, j: (b, 0, h)),
           pl.BlockSpec((None, None, SUBLANES, HEAD), lambda b, h, j: (b, h, 0, 0))],
        out_shape=[jax.ShapeDtypeStruct((T, D), BF16)] * 4
        + [jax.ShapeDtypeStruct((B, SUBLANES, D), F32), jax.ShapeDtypeStruct((B, H, SUBLANES, HEAD), F32)],
        scratch_shapes=[pltpu.VMEM((HEAD, HEAD), F32), pltpu.VMEM((len(_hg_level_blocks(C)), C, C), F32)],
        semantics=("parallel", "parallel", "arbitrary"),
        operands=(proj, proj, proj, proj, o_sv, dyb, states, a_sv, cum_sv, lb, gn))


def _inproj_bwd_x(dsegs, w_st, layer, x2, gain, dxm, ride=None):
    T, D = x2.shape
    tm = _tile(T, 512)
    nt = T // tm

    def body(*refs):
        seg_refs = refs[:N_SEG]
        w_ref, x_ref, g_ref, dxm_ref, dx_ref, dg_ref = refs[N_SEG:]
        k = pl.program_id(1)

        @pl.when(k == 0)
        def _():
            dx_ref[...] = jnp.zeros((tm, D), F32)

        for kk in range(N_SEG):
            @pl.when(k == kk)
            def _(kk=kk):
                dx_ref[...] += _dot_nt(seg_refs[kk][...], w_ref[...])

        @pl.when(k == N_SEG - 1)
        def _():
            x = x_ref[...]
            dxn, dg = _rms_bwd(dx_ref[...], x, _rms(x), g_ref[...])
            dx_ref[...] = dxm_ref[...] + dxn
            dg_ref[...] = dg

    row = pl.BlockSpec((tm, D), lambda i, k: (i, 0))

    def seg_spec(kk):
        return pl.BlockSpec((tm, D), lambda i, k: (jnp.minimum(i + jnp.where(k > kk, 1, 0), nt - 1), 0))

    return _ride_call(
        body, ride, name="inproj_bwd_x",
        grid=(nt, N_SEG),
        in_specs=[seg_spec(kk) for kk in range(N_SEG)]
        + [pl.BlockSpec((None, None, D, D), lambda i, k: (k // 2, layer, 0, k % 2)), row,
           pl.BlockSpec((1, D), lambda i, k: (0, 0)), row],
        out_specs=[row, pl.BlockSpec((None, SUBLANES, D), lambda i, k: (i, 0, 0))],
        out_shape=[jax.ShapeDtypeStruct((T, D), F32), jax.ShapeDtypeStruct((nt, SUBLANES, D), F32)],
        scratch_shapes=[],
        semantics=("parallel", "arbitrary"),
        operands=(*dsegs, w_st, x2, gain, dxm))


def _inproj_bwd_w(h, dsegs, layer, n_layers, bufs):
    T, D = h.shape
    tk = _tile(T, 1024)
    out_shape, extra = _layer_slot(bufs, [(D, 2 * D)], n_layers)

    def body(*refs):
        h_ref = refs[0]
        seg_refs = refs[1:1 + N_SEG]
        g_ref = refs[-1]
        k = pl.program_id(0)

        @pl.when(pl.program_id(1) == 0)
        def _():
            g_ref[...] = jnp.zeros((D, D), F32)

        for kk in range(N_SEG):
            @pl.when(k == kk)
            def _(kk=kk):
                g_ref[...] += _dot_tn(h_ref[...], seg_refs[kk][...])

    def seg_spec(kk):
        return pl.BlockSpec((tk, D), lambda k, t: (jnp.where(k == kk, t, 0), 0))

    return pl.pallas_call(
        body, name="inproj_bwd_w",
        grid=(N_SEG, T // tk),
        in_specs=[pl.BlockSpec((tk, D), lambda k, t: (t, 0))] + [seg_spec(kk) for kk in range(N_SEG)]
        + [ANY] * len(extra),
        out_specs=[pl.BlockSpec((None, None, D, D), lambda k, t: (k // 2, layer, 0, k % 2))],
        out_shape=out_shape,
        input_output_aliases={1 + N_SEG + i: i for i in range(len(extra))},
        compiler_params=_cp(("parallel", "arbitrary")),
    )(h, *dsegs, *extra)[0]


def _softmax_rows(lg_ref, L):
    rows = [lg_ref[l:l + 1, :] for l in range(L)]
    mx = functools.reduce(jnp.maximum, rows)
    es = [jnp.exp(r - mx) for r in rows]
    den = functools.reduce(lambda p, q: p + q, es)
    return [e / den for e in es]


def _prep(lb_logits, lam):
    L, D = lb_logits.shape

    def body(lg_ref, lam_ref, lowb_ref, sp_ref):
        sm = _softmax_rows(lg_ref, L)
        run = sm[0]
        for l in range(L):
            if l > 0:
                run = run + sm[l]
            lowb_ref[l:l + 1, :] = jnp.clip(run - sm[0], 0.0, 1.0)
        y = -lam_ref[...]
        sp_ref[...] = jnp.maximum(y, 0.0) + jnp.log1p(jnp.exp(-jnp.abs(y)))

    return pl.pallas_call(
        body, name="prep_small",
        out_shape=[jax.ShapeDtypeStruct((L, D), F32)] * 2,
    )(lb_logits, lam)


def _local_step(x, tgt, lowb, sp, norm_mix, wbufs, conv_w, conv_b, w_r, b_r, w_i, b_i, hg_norm,
                norm_mlp, norm_final):
    B, S, D = x.shape
    L = norm_mix.shape[0]
    T = B * S
    x2 = x.reshape(T, D)
    row = lambda a, l: a[l:l + 1]

    def weight_views(bufs):
        f4 = bufs[2].shape[-1]
        return (bufs[0].reshape(N_SHARD, L, D, 2 * D), bufs[1].reshape(N_SHARD, L, D // N_SHARD, D),
                bufs[2].reshape(N_SHARD, L, D, f4), bufs[3].reshape(N_SHARD, L, f4, D))

    w_in_st, w_out_st, w_up_st, w_down_st = weight_views(wbufs)
    saved = []
    for l in range(L):
        proj, h = _inproj_fwd(x2, row(norm_mix, l), w_in_st, l)
        ya, hrg = _rg_fwd(proj, B, conv_w[l], row(conv_b, l), w_r[l], row(b_r, l), w_i[l], row(b_i, l), row(sp, l))
        l0, nl = (1, min(2, L - 1)) if l == 0 else (l + 2, 1 if l + 2 < L else 0)
        if nl > 0:
            yb, o, st, a_sv, cum_sv, *wbufs = _hg_fwd(proj, B, row(lowb, l), row(hg_norm, l),
                                                      _gather_ride(list(wbufs), LINK_SPLIT, l0, nl))
            w_in_st, w_out_st, w_up_st, w_down_st = weight_views(wbufs)
        else:
            yb, o, st, a_sv, cum_sv = _hg_fwd(proj, B, row(lowb, l), row(hg_norm, l))
        xm, ymix = _out_fwd(ya, yb, proj, x2, w_out_st, l)
        xo, up, h2 = _mlp_fwd(xm, row(norm_mlp, l), w_up_st, w_down_st, l)
        saved.append((x2, proj, h, ya, hrg, yb, o, (st, a_sv, cum_sv), xm, ymix, up, h2))
        x2 = xo
    loss_parts, dx, g_nf = _final_loss(x2, norm_final[None, :], tgt.reshape(T, D))

    def reduce_part(g_in, g_out, g_mlp, gate_list):
        gate_list = gate_list[::-1]
        grads = [g_in, g_out, g_mlp[0], g_mlp[1], _shard_gate(jnp.stack([g[0] for g in gate_list])),
                 _shard_gate(jnp.stack([g[1] for g in gate_list]))]
        return _ReduceScatter([g.reshape(N_SHARD, -1, g.shape[-1]) for g in grads], LINK_SPLIT)

    gates = []
    small = []
    g_in = g_out = g_mlp = None
    rest = None
    for l in reversed(range(L)):
        x_in, proj, h, ya, hrg, yb, o, st, xm, ymix, up, h2 = saved[l]
        alone = l == 0 and L > 1
        if alone:
            rest = reduce_part(g_in, g_out, g_mlp, gates)
            g_in = g_out = g_mlp = None
            gates = []
        slot, n_slots = (0, 1) if l == 0 else (l - 1, L - 1)
        dxm, dup, g_nmlp, dxb, *got = _mlp_bwd_x(dx, xm, up, row(norm_mlp, l), w_up_st, w_down_st, l,
                                                 rest.ride_c() if alone else None)
        if alone:
            rest.after_c(got)
        g_mlp = _mlp_bwd_w(up, dxb, h2, dup, slot, n_slots, g_mlp)
        dya, dyb, dma, dmb = _out_bwd_x(dxm, ya, yb, proj, w_out_st, l)
        g_out = _out_bwd_w(ymix, dxm, slot, n_slots, None if g_out is None else [g_out])
        dxa, dga, g_wr, g_wi, g_cw, g_cb, g_br, g_bi, g_sp = _rg_bwd(
            proj, hrg, dya, B, conv_w[l], row(conv_b, l), w_r[l], row(b_r, l), w_i[l], row(b_i, l), row(sp, l))
        dq, dz, dv, dg, g_lb, g_gn, *got = _hg_bwd(proj, o, dyb, *st, B, row(lowb, l), row(hg_norm, l),
                                                   rest.ride_1() if alone else None)
        if alone:
            rest.after_1(got)
        dsegs = (dxa, dga, dq, dz, dv, dg, dma, dmb)
        dx, g_nmix, *got = _inproj_bwd_x(dsegs, w_in_st, l, x_in, row(norm_mix, l), dxm,
                                         rest.ride_2() if alone else None)
        if alone:
            rest.after_2(got)
        g_in = _inproj_bwd_w(h, dsegs, slot, n_slots, None if g_in is None else [g_in])
        gates.append((g_wr, g_wi))
        small.append((g_lb, g_nmix, g_cb, g_br, g_bi, g_sp, g_nmlp, g_gn, g_cw))
    small.reverse()
    first = reduce_part(g_in, g_out, g_mlp, gates)
    first.exchange_c()
    first.exchange_1()
    first.exchange_2()
    parts = [first.finish()] + ([rest.finish()] if rest is not None else [])
    return loss_parts, dx.reshape(B, S, D), parts, small, g_nf


def _me():
    return lax.axis_index("x"), lax.axis_index("y"), lax.axis_index("c")


def _cast_place(w, slot):
    R, N = w.shape
    tr = _tile(R, max(16, (1 << 20) // N))

    def body(slot_ref, w_ref, o_ref):
        o_ref[...] = w_ref[...].astype(BF16)

    return pl.pallas_call(
        body, name="cast_place",
        grid_spec=pltpu.PrefetchScalarGridSpec(
            num_scalar_prefetch=1, grid=(R // tr,),
            in_specs=[pl.BlockSpec((tr, N), lambda i, slot: (i, 0))],
            out_specs=pl.BlockSpec((None, tr, N), lambda i, slot: (slot[0], i, 0))),
        out_shape=jax.ShapeDtypeStruct((N_SHARD, R, N), BF16),
        compiler_params=_cp(("parallel",)),
    )(slot, w)


def _gather_weights(bufs, first_axes, l0, nl):
    n = len(bufs)
    phases = _gather_phases(n, first_axes, l0, nl)

    def body(*refs):
        outs = refs[n:2 * n]
        ssem, rsem = refs[2 * n:]
        for ph in phases:
            ph(outs, ssem, rsem)

    return pl.pallas_call(
        body, name="gather_weights",
        in_specs=[ANY] * n, out_specs=[ANY] * n,
        out_shape=[jax.ShapeDtypeStruct(b.shape, b.dtype) for b in bufs],
        input_output_aliases={a: a for a in range(n)},
        scratch_shapes=_gather_sems(n),
        compiler_params=pltpu.CompilerParams(has_side_effects=True),
    )(*bufs)


def _gather_sems(n):
    return [pltpu.SemaphoreType.DMA((n, 6)), pltpu.SemaphoreType.DMA((n, 6))]


def _gather_phases(n, first_axes, l0, nl):
    def ctx(outs, ssem, rsem):
        x, y, c = _me()

        def piece(a, flips, half):
            sx = 1 - x if flips[0] else x
            sy = 1 - y if flips[1] else y
            return outs[a].at[2 * sx + sy, pl.ds(l0[a], nl[a]), half]

        def rcopy(a, k, ref, dev):
            return pltpu.make_async_remote_copy(src_ref=ref, dst_ref=ref, send_sem=ssem.at[a, k], recv_sem=rsem.at[a, k],
                                                device_id=dev, device_id_type=MESH_ID)

        def route(a):
            fx = first_axes[a] == "x"
            f_dev = (1 - x, y, c) if fx else (x, 1 - y, c)
            g_dev = (x, 1 - y, c) if fx else (1 - x, y, c)
            return f_dev, g_dev, ((1, 0) if fx else (0, 1)), ((0, 1) if fx else (1, 0))

        return c, (x, y, 1 - c), piece, rcopy, route

    def own_halves(outs, ssem, rsem):
        c, sib, piece, rcopy, route = ctx(outs, ssem, rsem)
        for a in range(n):
            f_dev, g_dev, _, _ = route(a)
            own = piece(a, (0, 0), c)
            rcopy(a, 0, own, f_dev).start()
            rcopy(a, 1, own, g_dev).start()

    def pass_on_neighbours(outs, ssem, rsem):
        c, sib, piece, rcopy, route = ctx(outs, ssem, rsem)
        for a in range(n):
            f_dev, g_dev, f_flip, _ = route(a)
            got = piece(a, f_flip, c)
            rcopy(a, 0, got, f_dev).wait_recv()
            rcopy(a, 2, got, g_dev).start()
            rcopy(a, 3, got, sib).start()
        for a in range(n):
            _, g_dev, _, g_flip = route(a)
            got = piece(a, g_flip, c)
            rcopy(a, 1, got, g_dev).wait_recv()
            rcopy(a, 4, got, sib).start()

    def pass_on_diagonal(outs, ssem, rsem):
        c, sib, piece, rcopy, route = ctx(outs, ssem, rsem)
        for a in range(n):
            _, g_dev, _, _ = route(a)
            got = piece(a, (1, 1), c)
            rcopy(a, 2, got, g_dev).wait_recv()
            rcopy(a, 5, got, sib).start()

    def drain(outs, ssem, rsem):
        c, sib, piece, rcopy, route = ctx(outs, ssem, rsem)
        for a in range(n):
            f_dev, g_dev, f_flip, g_flip = route(a)
            for k, fl in ((3, f_flip), (4, g_flip), (5, (1, 1))):
                rcopy(a, k, piece(a, fl, 1 - c), sib).wait_recv()
            own = piece(a, (0, 0), c)
            for k, dev in ((0, f_dev), (1, g_dev), (2, g_dev), (3, sib), (4, sib), (5, sib)):
                rcopy(a, k, own, dev).wait_send()

    return [own_halves, pass_on_neighbours, pass_on_diagonal, drain]


def _gather_ride(bufs, first_axes, l0, nl):
    n = len(bufs)
    phases = [lambda reads, refs, outs, ssem, rsem, ph=ph: ph(refs, ssem, rsem)
              for ph in _gather_phases(n, first_axes, [l0] * n, [nl] * n)]
    return _Ride([], bufs, [], phases, GATHER_STEPS, _gather_sems(n))


def _shard_gate(g):
    n_l, nb = g.shape[:2]
    return g.reshape(n_l, nb, N_SHARD, RG_BLOCK // N_SHARD, RG_BLOCK).transpose(2, 0, 1, 3, 4)


def _exchange(arrs, axes, name):
    n = len(arrs)

    def body(*refs):
        ins, outs = refs[:n], refs[n:2 * n]
        ssem, rsem = refs[2 * n:]
        x, y, c = _me()
        cps = []
        for a in range(n):
            my = {"x": x, "y": y, "c": c}[axes[a]]
            partner = {"x": (1 - x, y, c), "y": (x, 1 - y, c), "c": (x, y, 1 - c)}[axes[a]]
            cps.append(pltpu.make_async_remote_copy(
                src_ref=ins[a].at[:, 1 - my], dst_ref=outs[a], send_sem=ssem.at[a], recv_sem=rsem.at[a],
                device_id=partner, device_id_type=MESH_ID))
            cps[-1].start()
        for cp in cps:
            cp.wait()

    return pl.pallas_call(
        body, name=name,
        in_specs=[ANY] * n, out_specs=[ANY] * n,
        out_shape=[jax.ShapeDtypeStruct((a.shape[0],) + a.shape[2:], a.dtype) for a in arrs],
        scratch_shapes=[pltpu.SemaphoreType.DMA((n,)), pltpu.SemaphoreType.DMA((n,))],
        compiler_params=pltpu.CompilerParams(has_side_effects=True),
    )(*arrs)


def _add_kept(arr, got, idx, name, with_bf16):
    P, _, R, N = arr.shape
    tr = _tile(R, max(16, (1 << 20) // N))

    def body(idx_ref, a_ref, g_ref, o_ref, *ob_ref):
        s = a_ref[...] + g_ref[...].astype(F32)
        o_ref[...] = s
        if with_bf16:
            ob_ref[0][...] = s.astype(BF16)

    out_blk = pl.BlockSpec((None, tr, N), lambda p, i, idx: (p, i, 0))
    return pl.pallas_call(
        body, name=name,
        grid_spec=pltpu.PrefetchScalarGridSpec(
            num_scalar_prefetch=1, grid=(P, R // tr),
            in_specs=[pl.BlockSpec((None, None, tr, N), lambda p, i, idx: (p, idx[0], i, 0)),
                      pl.BlockSpec((None, tr, N), lambda p, i, idx: (p, i, 0))],
            out_specs=[out_blk] * (2 if with_bf16 else 1)),
        out_shape=[jax.ShapeDtypeStruct((P, R, N), F32)] + ([jax.ShapeDtypeStruct((P, R, N), BF16)] if with_bf16 else []),
        compiler_params=_cp(("parallel", "parallel")),
    )(idx, arr, got)


def _share_halves(halves):
    n = len(halves)

    def body(*refs):
        ins, outs = refs[:n], refs[n:2 * n]
        ssem, rsem = refs[2 * n:]
        x, y, c = _me()
        cps = []
        for a in range(n):
            cps.append(pltpu.make_async_remote_copy(
                src_ref=ins[a], dst_ref=outs[a], send_sem=ssem.at[a], recv_sem=rsem.at[a],
                device_id=(x, y, 1 - c), device_id_type=MESH_ID))
            cps[-1].start()
        for cp in cps:
            cp.wait()

    return pl.pallas_call(
        body, name="share_halves",
        in_specs=[ANY] * n, out_specs=[ANY] * n,
        out_shape=[jax.ShapeDtypeStruct(h.shape, h.dtype) for h in halves],
        scratch_shapes=[pltpu.SemaphoreType.DMA((n,)), pltpu.SemaphoreType.DMA((n,))],
        compiler_params=pltpu.CompilerParams(has_side_effects=True),
    )(*halves)


def _exchange_ride(arrs, axes):
    n = len(arrs)

    def copies(reads, outs, ssem, rsem):
        x, y, c = _me()
        cps = []
        for a in range(n):
            my = {"x": x, "y": y, "c": c}[axes[a]]
            partner = {"x": (1 - x, y, c), "y": (x, 1 - y, c), "c": (x, y, 1 - c)}[axes[a]]
            cps.append(pltpu.make_async_remote_copy(
                src_ref=reads[a].at[:, 1 - my], dst_ref=outs[a], send_sem=ssem.at[a], recv_sem=rsem.at[a],
                device_id=partner, device_id_type=MESH_ID))
        return cps

    def start(reads, bufs, outs, ssem, rsem):
        for cp in copies(reads, outs, ssem, rsem):
            cp.start()

    def finish(reads, bufs, outs, ssem, rsem):
        for cp in copies(reads, outs, ssem, rsem):
            cp.wait()

    landing = [jax.ShapeDtypeStruct((a.shape[0],) + a.shape[2:], a.dtype) for a in arrs]
    return _Ride(arrs, [], landing, [start, finish], (0.0, 1.0),
                 [pltpu.SemaphoreType.DMA((n,)), pltpu.SemaphoreType.DMA((n,))])


class _ReduceScatter:
    def __init__(self, grads, first_axes):
        x, y, c = _me()
        idx = lambda v: jnp.reshape(v, (1,)).astype(jnp.int32)
        self.coord = {"x": idx(x), "y": idx(y), "c": idx(c)}
        self.first = list(first_axes)
        self.second = ["y" if f == "x" else "x" for f in first_axes]
        self.views_c = [g.reshape(N_SHARD, 2, g.shape[1] // 2, g.shape[2]) for g in grads]

    def ride_c(self):
        return _exchange_ride(self.views_c, "c" * len(self.views_c))

    def exchange_c(self):
        self.after_c(_exchange(self.views_c, "c" * len(self.views_c), "rs_exchange_c"))

    def after_c(self, got):
        self.summed = [_add_kept(v, r, self.coord["c"], "rs_add_c", True) for v, r in zip(self.views_c, got)]

    @staticmethod
    def _split_view(a, ax):
        _, rh, nn = a.shape
        return a.reshape(1, 2, 2 * rh, nn) if ax == "x" else a.reshape(2, 2, rh, nn)

    def _views_1(self):
        return [self._split_view(s[1], f) for s, f in zip(self.summed, self.first)]

    def ride_1(self):
        return _exchange_ride(self._views_1(), self.first)

    def exchange_1(self):
        self.after_1(_exchange(self._views_1(), self.first, "rs_exchange_1"))

    def after_1(self, got):
        self.summed = [_add_kept(self._split_view(s[0], f), r, self.coord[f], "rs_add_1", True)
                       for s, r, f in zip(self.summed, got, self.first)]

    def _views_2(self):
        return [s[1].reshape(1, 2, -1, s[1].shape[-1]) for s in self.summed]

    def ride_2(self):
        return _exchange_ride(self._views_2(), self.second)

    def exchange_2(self):
        self.after_2(_exchange(self._views_2(), self.second, "rs_exchange_2"))

    def after_2(self, got):
        views32 = [s[0].reshape(1, 2, -1, s[0].shape[-1]) for s in self.summed]
        self.kept = [_add_kept(v, r, self.coord[g], "rs_add_2", False)[0][0]
                     for v, r, g in zip(views32, got, self.second)]

    def finish(self):
        return self.kept, _share_halves(self.kept)


def _allgather_small(p):
    R, D = p.shape

    def body(p_ref, o_ref, ssem, rsem):
        x, y, c = _me()
        me = 4 * x + 2 * y + c
        o_ref[me] = p_ref[...]
        cps = []
        for m in range(1, 8):
            mx, my, mc = (m >> 2) & 1, (m >> 1) & 1, m & 1
            peer = (1 - x if mx else x, 1 - y if my else y, 1 - c if mc else c)
            cps.append(pltpu.make_async_remote_copy(
                src_ref=p_ref, dst_ref=o_ref.at[me], send_sem=ssem.at[m - 1], recv_sem=rsem.at[m - 1],
                device_id=peer, device_id_type=MESH_ID))
            cps[-1].start()
        for cp in cps:
            cp.wait()

    return pl.pallas_call(
        body, name="allgather_small",
        in_specs=[pl.BlockSpec(memory_space=pltpu.VMEM)],
        out_specs=pl.BlockSpec(memory_space=pltpu.VMEM),
        out_shape=jax.ShapeDtypeStruct((8, R, D), p.dtype),
        scratch_shapes=[pltpu.SemaphoreType.DMA((7,)), pltpu.SemaphoreType.DMA((7,))],
        compiler_params=pltpu.CompilerParams(has_side_effects=True, vmem_limit_bytes=VMEM_LIMIT),
    )(p)


def _adam_math(w, g, m, v):
    m = ADAM_B1 * m + (1.0 - ADAM_B1) * g
    v = ADAM_B2 * v + (1.0 - ADAM_B2) * (g * g)
    m_hat = m / (1.0 - ADAM_B1 ** ADAM_STEP)
    v_hat = v / (1.0 - ADAM_B2 ** ADAM_STEP)
    delta = -ADAM_LR * (m_hat / (jnp.sqrt(v_hat) + ADAM_EPS) + ADAM_WD * w)
    return delta, m, v


def _adam(w, g_mine, g_sib, m, v, core, row0, layer_rows, outs):
    R, N = w.shape
    rh = g_mine.shape[0]
    tr = _tile(layer_rows // 2, max(16, (1 << 19) // N))
    nt = rh // tr
    t0 = row0 // tr
    extra = [] if outs is None else list(outs)

    def body(c_ref, w_ref, gm_ref, gs_ref, m_ref, v_ref, *rest):
        g_ref, d_ref, nm_ref, nv_ref = rest[-4:]
        g = jnp.where(pl.program_id(0) == c_ref[0], gm_ref[...], gs_ref[...])
        d, nm, nv = _adam_math(w_ref[...], g, m_ref[...], v_ref[...])
        g_ref[...] = g
        d_ref[...] = d
        nm_ref[...] = nm
        nv_ref[...] = nv

    blk = pl.BlockSpec((tr, N), lambda h, i, c: (t0 + h * nt + i, 0))
    half = pl.BlockSpec((tr, N), lambda h, i, c: (i, 0))
    return pl.pallas_call(
        body, name="adamw",
        grid_spec=pltpu.PrefetchScalarGridSpec(
            num_scalar_prefetch=1, grid=(2, nt),
            in_specs=[blk, half, half, blk, blk] + [ANY] * len(extra), out_specs=[blk] * 4),
        out_shape=[jax.ShapeDtypeStruct((R, N), F32)] * 4,
        input_output_aliases={6 + i: i for i in range(len(extra))},
        compiler_params=_cp(("parallel", "parallel")),
    )(core, w, g_mine, g_sib, m, v, *extra)


def _reduce_rows(parts, sizes, rows_out):
    D = parts.shape[1]

    def body(p_ref, o_ref):
        o_ref[...] = jnp.zeros((rows_out, D), F32)
        off = 0
        for i, sz in enumerate(sizes):
            o_ref[i:i + 1, :] = jnp.sum(p_ref[off:off + sz, :], axis=0, keepdims=True)
            off += sz

    return pl.pallas_call(
        body, name="reduce_rows",
        out_shape=jax.ShapeDtypeStruct((rows_out, D), F32),
        compiler_params=pltpu.CompilerParams(vmem_limit_bytes=VMEM_LIMIT),
    )(parts)


def _sum_devices(g8):
    _, R, D = g8.shape

    def body(g_ref, o_ref):
        tot = g_ref[0]
        for k in range(1, 8):
            tot = tot + g_ref[k]
        o_ref[...] = tot

    return pl.pallas_call(
        body, name="sum_devices",
        out_shape=jax.ShapeDtypeStruct((R, D), F32),
        compiler_params=pltpu.CompilerParams(vmem_limit_bytes=VMEM_LIMIT),
    )(g8)


def _small_update(gathered, w, m, v, L):
    _, R, D = gathered.shape

    def body(g8_ref, w_ref, m_ref, v_ref, g_ref, d_ref, nm_ref, nv_ref):
        tot = g8_ref[0]
        for k in range(1, 8):
            tot = tot + g8_ref[k]
        g_ref[...] = tot
        sm = _softmax_rows(w_ref, L)
        run = sm[0]
        dcum = []
        for l in range(L):
            if l > 0:
                run = run + sm[l]
            cum = run - sm[0]
            dcum.append(jnp.where((cum > 0.0) & (cum < 1.0), g_ref[l:l + 1, :], 0.0))
        dsm = [jnp.zeros((1, D), F32)]
        for i in range(1, L):
            dsm.append(functools.reduce(lambda p, q: p + q, dcum[i:]))
        dot = functools.reduce(lambda p, q: p + q, [s * d for s, d in zip(sm, dsm)])
        for l in range(L):
            g_ref[l:l + 1, :] = sm[l] * (dsm[l] - dot)
        lam = w_ref[5 * L:6 * L, :]
        g_ref[5 * L:6 * L, :] = g_ref[5 * L:6 * L, :] * (-_sig(-lam))
        d, nm, nv = _adam_math(w_ref[...], g_ref[...], m_ref[...], v_ref[...])
        d_ref[...] = d
        nm_ref[...] = nm
        nv_ref[...] = nv

    return pl.pallas_call(
        body, name="small_update",
        out_shape=[jax.ShapeDtypeStruct((R, D), F32)] * 4,
        compiler_params=pltpu.CompilerParams(vmem_limit_bytes=VMEM_LIMIT),
    )(gathered, w, m, v)


def kernel(x, lb_logits, norm_mix, w_in, conv_w, conv_b, w_r, b_r, w_i, b_i, lam, hg_norm, w_out, norm_mlp, w_up, w_down, norm_final, loss_target, m_lb_logits, m_norm_mix, m_w_in, m_conv_w, m_conv_b, m_w_r, m_b_r, m_w_i, m_b_i, m_lam, m_hg_norm, m_w_out, m_norm_mlp, m_w_up, m_w_down, m_norm_final, v_lb_logits, v_norm_mix, v_w_in, v_conv_w, v_conv_b, v_w_r, v_b_r, v_w_i, v_b_i, v_lam, v_hg_norm, v_w_out, v_norm_mlp, v_w_up, v_w_down, v_norm_final):
    B, S, D = x.shape
    L = norm_mix.shape[0]
    nb = D // RG_BLOCK
    Dq = D // N_SHARD
    mx, my, mc = _me()
    shard = 2 * mx + my

    big_w = (w_in, w_out, w_up, w_down, w_r, w_i)
    flat2 = lambda a: a.reshape(-1, a.shape[-1])
    slot = jnp.reshape(shard, (1,)).astype(jnp.int32)
    def place(w):
        b = _cast_place(flat2(w), slot)
        return b.reshape(N_SHARD, L, 2, b.shape[1] // (2 * L), b.shape[2])

    *wbufs, g_r, g_i = _gather_weights([place(w) for w in big_w], LINK_SPLIT, [0] * 6, [1] * 4 + [L] * 2)
    unshard_gate = lambda g: g.reshape(N_SHARD, L, nb, RG_BLOCK // N_SHARD, RG_BLOCK).transpose(1, 2, 0, 3, 4).reshape(
        L, nb, RG_BLOCK, RG_BLOCK)
    w_r_full, w_i_full = unshard_gate(g_r), unshard_gate(g_i)

    R_LB, R_NMIX, R_CB, R_BR, R_BI, R_LAM, R_NMLP, R_GN, R_CW, R_NF, R_LOSS = (
        0, L, 2 * L, 3 * L, 4 * L, 5 * L, 6 * L, 7 * L, 8 * L, 12 * L, 12 * L + 1)
    n_rows = 12 * L + 2
    rows_pad = n_rows + (-n_rows) % SUBLANES

    def place_cols(a):
        return lax.dynamic_update_slice(jnp.zeros((a.shape[0], D), F32), a, (0, shard * Dq))

    def pack_small(lb_, nmix_, cb_, br_, bi_, lam_, nmlp_, gn_, cw_, nf_):
        gn_pad = jnp.pad(gn_, ((0, 0), (0, D - HEAD)))
        rows = [lb_, nmix_, cb_, br_, bi_, lam_, nmlp_, gn_pad, place_cols(cw_.reshape(L * CONV_TAPS, Dq)),
                nf_[None, :], jnp.zeros((rows_pad - n_rows + 1, D), F32)]
        return jnp.concatenate(rows, axis=0)

    w_small = pack_small(lb_logits, norm_mix, conv_b, b_r, b_i, lam, norm_mlp, hg_norm, conv_w, norm_final)
    m_small = pack_small(m_lb_logits, m_norm_mix, m_conv_b, m_b_r, m_b_i, m_lam, m_norm_mlp, m_hg_norm, m_conv_w,
                         m_norm_final)
    v_small = pack_small(v_lb_logits, v_norm_mix, v_conv_b, v_b_r, v_b_i, v_lam, v_norm_mlp, v_hg_norm, v_conv_w,
                         v_norm_final)
    cw_rows = place_cols(conv_w.reshape(L * CONV_TAPS, Dq)) * jnp.where(mc == 0, 1.0, 0.0)
    conv_w_full = _sum_devices(_allgather_small(cw_rows)).reshape(L, CONV_TAPS, D)

    lowb, sp = _prep(lb_logits, lam)

    loss_parts, grad_x, parts, small, g_nf = _local_step(
        x, loss_target, lowb, sp, norm_mix, wbufs, conv_w_full, conv_b, w_r_full, b_r, w_i_full, b_i, hg_norm,
        norm_mlp, norm_final)

    core = jnp.reshape(mc, (1,)).astype(jnp.int32)
    outs = {}
    for a, (name, w, m, v) in enumerate(zip(("w_in", "w_out", "w_up", "w_down", "w_r", "w_i"), big_w,
                                            (m_w_in, m_w_out, m_w_up, m_w_down, m_w_r, m_w_i),
                                            (v_w_in, v_w_out, v_w_up, v_w_down, v_w_r, v_w_i))):
        layer_rows = flat2(w).shape[0] // L
        done, row0 = None, 0
        for mine, sibs in parts:
            done = _adam(flat2(w), mine[a], sibs[a], flat2(m), flat2(v), core, row0, layer_rows, done)
            row0 += 2 * mine[a].shape[0]
        outs[name] = tuple(t.reshape(w.shape) for t in done)

    parts, sizes = [], []

    def add_rows(a):
        a = a.reshape(-1, a.shape[-1])
        if a.shape[1] != D:
            a = jnp.pad(a, ((0, 0), (0, D - a.shape[1])))
        parts.append(a)
        sizes.append(a.shape[0])

    for i in range(8):
        for l in range(L):
            add_rows(small[l][i])
    for l in range(L):
        for j in range(CONV_TAPS):
            add_rows(small[l][8][j])
    add_rows(g_nf)
    loss_rows = loss_parts[:, 0:1, :]
    add_rows(jnp.where(lax.broadcasted_iota(jnp.int32, loss_rows.shape, 2) == 0, loss_rows, 0.0))
    g_small = _reduce_rows(jnp.concatenate(parts, axis=0), sizes, rows_pad)
    g_small, d_small, nm_small, nv_small = _small_update(_allgather_small(g_small), w_small, m_small, v_small, L)

    def unpack(t):
        take_cols = lambda a: lax.dynamic_slice(a, (0, shard * Dq), (a.shape[0], Dq))
        return {"lb_logits": t[R_LB:R_LB + L], "norm_mix": t[R_NMIX:R_NMIX + L], "conv_b": t[R_CB:R_CB + L],
                "b_r": t[R_BR:R_BR + L], "b_i": t[R_BI:R_BI + L], "lam": t[R_LAM:R_LAM + L],
                "norm_mlp": t[R_NMLP:R_NMLP + L], "hg_norm": t[R_GN:R_GN + L, :HEAD],
                "conv_w": take_cols(t[R_CW:R_CW + L * CONV_TAPS]).reshape(L, CONV_TAPS, Dq), "norm_final": t[R_NF]}

    small_out = [unpack(t) for t in (g_small, d_small, nm_small, nv_small)]
    loss = g_small[R_LOSS, 0]
    names = ("lb_logits", "norm_mix", "w_in", "conv_w", "conv_b", "w_r", "b_r", "w_i", "b_i", "lam", "hg_norm",
             "w_out", "norm_mlp", "w_up", "w_down", "norm_final")
    result = [loss, grad_x]
    for kind in range(4):
        for nme in names:
            result.append(outs[nme][kind] if nme in outs else small_out[kind][nme])
    return tuple(result)
```

```python
import functools
import math

import jax
import jax.numpy as jnp
from jax import lax
from jax.experimental import pallas as pl
from jax.experimental.pallas import tpu as pltpu

F32 = jnp.float32
BF16 = jnp.bfloat16

HEAD = 128
RG_BLOCK = 256
CONV_TAPS = 4
RG_C = 8.0
F_MIN = 1e-30
NORM_EPS = 1e-6
N_SEG = 8
N_SHARD = 4
HG_CHUNK = 256
HG_PAIR = 2
RG_TILE = 256
ADAM_LR, ADAM_B1, ADAM_B2, ADAM_EPS, ADAM_WD, ADAM_STEP = 0.001, 0.9, 0.999, 1e-08, 0.01, 10
V7X_VMEM_BYTES = 64 * 1024 * 1024
VMEM_LIMIT = V7X_VMEM_BYTES - 8 * 1024 * 1024
SUBLANES = 8
LINK_SPLIT = "xxyyyy"
GATHER_STEPS = (0.0, 0.6, 0.88, 1.0)
MESH_ID = pl.DeviceIdType.MESH
ANY = pl.BlockSpec(memory_space=pl.ANY)


def _cp(sem):
    return pltpu.CompilerParams(dimension_semantics=sem, vmem_limit_bytes=VMEM_LIMIT)


def _dot(a, b):
    return jnp.dot(a, b, preferred_element_type=F32)


def _dot_nt(a, b):
    return lax.dot_general(a, b, (((1,), (1,)), ((), ())), preferred_element_type=F32)


def _dot_tn(a, b):
    return lax.dot_general(a, b, (((0,), (0,)), ((), ())), preferred_element_type=F32)


def _dot_01(m01, x):
    n = x.shape[1]
    hi = x.astype(BF16)
    r1 = x - hi.astype(F32)
    mid = r1.astype(BF16)
    lo = (r1 - mid.astype(F32)).astype(BF16)
    y = _dot(m01, jnp.concatenate([hi, mid, lo], axis=1))
    return y[:, :n] + y[:, n:2 * n] + y[:, 2 * n:]


def _sig(x):
    return jax.nn.sigmoid(x)


def _rows8(x):
    return x.reshape(x.shape[0] // SUBLANES, SUBLANES, x.shape[1]).sum(axis=0)


def _tile(n, cap):
    if n <= cap:
        return n
    t = cap - cap % 16
    while n % t:
        t -= 16
    return t


_GELU_C = math.sqrt(2.0 / math.pi)


def _gelu_and_grad(x):
    x2 = x * x
    t = jnp.tanh(_GELU_C * (x + 0.044715 * x * x2))
    g = 0.5 * x * (1.0 + t)
    dg = 0.5 * (1.0 + t) + 0.5 * x * (1.0 - t * t) * (_GELU_C * (1.0 + 3.0 * 0.044715 * x2))
    return g, dg


def _rms(x):
    return lax.rsqrt(jnp.mean(x * x, axis=-1, keepdims=True) + NORM_EPS)


def _rms_bwd(dh, x, rs, gain):
    xhat = x * rs
    dxhat = dh * gain
    dx = rs * (dxhat - xhat * jnp.mean(dxhat * xhat, axis=-1, keepdims=True))
    return dx, _rows8(dh * xhat)


def _inproj_fwd(x2, gain, w_st, layer):
    T, D = x2.shape
    tm = _tile(T, 2048)

    def body(x_ref, g_ref, w_ref, o_ref, h_ref):
        @pl.when(pl.program_id(1) == 0)
        def _():
            x = x_ref[...]
            h_ref[...] = (x * _rms(x) * g_ref[...]).astype(BF16)
        o_ref[...] = _dot(h_ref[...], w_ref[...])

    return pl.pallas_call(
        body, name="inproj_fwd",
        grid=(T // tm, N_SEG),
        in_specs=[pl.BlockSpec((tm, D), lambda i, k: (i, 0)),
                  pl.BlockSpec((1, D), lambda i, k: (0, 0)),
                  pl.BlockSpec((None, None, D, D), lambda i, k: (k // 2, layer, 0, k % 2))],
        out_specs=[pl.BlockSpec((tm, D), lambda i, k: (i, k)),
                   pl.BlockSpec((tm, D), lambda i, k: (i, 0))],
        out_shape=[jax.ShapeDtypeStruct((T, N_SEG * D), F32), jax.ShapeDtypeStruct((T, D), BF16)],
        compiler_params=_cp(("parallel", "arbitrary")),
    )(x2, gain, w_st)


def _rg_gates(xc, wr_ref, br, wi_ref, bi, sp):
    D = xc.shape[1]
    xcb = xc.astype(BF16)
    pr, pi = [], []
    for n in range(D // RG_BLOCK):
        blk = xcb[:, n * RG_BLOCK:(n + 1) * RG_BLOCK]
        pr.append(_dot(blk, wr_ref[n]))
        pi.append(_dot(blk, wi_ref[n]))
    r = _sig(jnp.concatenate(pr, axis=1) + br) if len(pr) > 1 else _sig(pr[0] + br)
    i = _sig(jnp.concatenate(pi, axis=1) + bi) if len(pi) > 1 else _sig(pi[0] + bi)
    la = (-RG_C) * r * sp
    a = jnp.exp(la)
    y = 2.0 * la
    one_m_e2 = jnp.where(y > -1e-2, -(y * (1.0 + 0.5 * y * (1.0 + y * (1.0 / 3.0)))), 1.0 - jnp.exp(y))
    mult = jnp.sqrt(jnp.maximum(one_m_e2, 0.0))
    return r, i, a, mult


def _conv_taps(xbuf, cw_ref, ts):
    acc = None
    for j in range(CONV_TAPS):
        term = cw_ref[j:j + 1, :] * xbuf[pl.ds(SUBLANES - (CONV_TAPS - 1) + j, ts), :]
        acc = term if acc is None else acc + term
    return acc


def _rg_fwd(proj, B, cw, cb, wr, br, wi, bi, sp):
    T = proj.shape[0]
    D = proj.shape[1] // N_SEG
    S = T // B
    ts = _tile(S, RG_TILE)
    nts = S // ts
    nb = D // RG_BLOCK

    def body(xa_ref, ga_ref, cw_ref, cb_ref, wr_ref, br_ref, wi_ref, bi_ref, sp_ref,
             ya_ref, h_ref, xbuf, a_scr, u_scr, carry):
        @pl.when(pl.program_id(1) == 0)
        def _():
            xbuf[0:SUBLANES, :] = jnp.zeros((SUBLANES, D), F32)
            carry[...] = jnp.zeros((SUBLANES, D), F32)

        xbuf[pl.ds(SUBLANES, ts), :] = xa_ref[...]
        xc = _conv_taps(xbuf, cw_ref, ts) + cb_ref[...]
        r, i, a, mult = _rg_gates(xc, wr_ref, br_ref[...], wi_ref, bi_ref[...], sp_ref[...])
        a_scr[...] = a
        u_scr[...] = mult * (i * xc)
        row8 = lax.broadcasted_iota(jnp.int32, (SUBLANES, 1), 0)

        def blk(n, hprev):
            off = pl.multiple_of(n * SUBLANES, SUBLANES)
            a8 = a_scr[pl.ds(off, SUBLANES), :]
            u8 = u_scr[pl.ds(off, SUBLANES), :]
            for d in (1, 2, 4):
                m = row8 >= d
                ap = jnp.where(m, pltpu.roll(a8, d, 0), 1.0)
                up = jnp.where(m, pltpu.roll(u8, d, 0), 0.0)
                u8 = a8 * up + u8
                a8 = a8 * ap
            h8 = u8 + a8 * hprev
            u_scr[pl.ds(off, SUBLANES), :] = h8
            last = jnp.sum(jnp.where(row8 == SUBLANES - 1, h8, 0.0), axis=0, keepdims=True)
            return jnp.broadcast_to(last, (SUBLANES, D))

        carry[...] = lax.fori_loop(0, ts // SUBLANES, blk, carry[...])
        h = u_scr[...]
        h_ref[...] = h
        g, _ = _gelu_and_grad(ga_ref[...])
        ya_ref[...] = h * g
        xbuf[0:SUBLANES, :] = xa_ref[pl.ds(ts - SUBLANES, SUBLANES), :]

    vec = pl.BlockSpec((1, D), lambda b, j: (0, 0))
    gate = pl.BlockSpec((nb, RG_BLOCK, RG_BLOCK), lambda b, j: (0, 0, 0))
    return pl.pallas_call(
        body, name="rg_fwd",
        grid=(B, nts),
        in_specs=[pl.BlockSpec((ts, D), lambda b, j: (b * nts + j, 0)),
                  pl.BlockSpec((ts, D), lambda b, j: (b * nts + j, 1)),
                  pl.BlockSpec((CONV_TAPS, D), lambda b, j: (0, 0)), vec, gate, vec, gate, vec, vec],
        out_specs=[pl.BlockSpec((ts, D), lambda b, j: (b * nts + j, 0))] * 2,
        out_shape=[jax.ShapeDtypeStruct((T, D), F32)] * 2,
        scratch_shapes=[pltpu.VMEM((SUBLANES + ts, D), F32), pltpu.VMEM((ts, D), F32),
                        pltpu.VMEM((ts, D), F32), pltpu.VMEM((SUBLANES, D), F32)],
        compiler_params=_cp(("arbitrary", "arbitrary")),
    )(proj, proj, cw, cb, wr, br, wi, bi, sp)


def _hg_gates(q, z, lb):
    sig = _sig(z)
    one_m = 1.0 - lb
    fg = lb + one_m * sig
    lf = jnp.log(jnp.maximum(fg, F_MIN))
    kf = one_m * (1.0 - sig)
    qs = _sig(q)
    return q * qs, qs, kf, lf, fg, sig


def _hg_cum(lf, C):
    ri = lax.broadcasted_iota(jnp.int32, (C, C), 0)
    ci = lax.broadcasted_iota(jnp.int32, (C, C), 1)
    return _dot_01(jnp.where(ci <= ri, 1.0, 0.0).astype(BF16), lf)


def _hg_levels(lf, cum, C):
    row = lax.broadcasted_iota(jnp.int32, (C, 1), 0)
    levels = []
    w = C // 2
    while w >= 4:
        blk = 2 * w
        upper = (row & w) != 0
        ref = jnp.min(jnp.where(upper, 0.0, cum).reshape(C // blk, blk, HEAD), axis=1, keepdims=True)
        ref = jnp.broadcast_to(ref, (C // blk, blk, HEAD)).reshape(C, HEAD)
        d = cum - ref
        levels.append(jnp.exp(jnp.where(upper, d, -d)))
        w //= 2
    r4 = row & 3
    lf_prev = pltpu.roll(lf, 1, 0)
    lf_next = pltpu.roll(lf, C - 1, 0)
    levels.append(jnp.exp(jnp.where(r4 == 3, lf + lf_prev, jnp.where(r4 == 2, lf, jnp.where(r4 == 0, lf_next, 0.0)))))
    levels.append(jnp.exp(jnp.where((row & 1) == 1, lf, 0.0)))
    levels.append(None)
    return levels


def _hg_level_blocks(C):
    blks = []
    w = C // 2
    while w >= 4:
        blks.append(2 * w)
        w //= 2
    return blks + [4, 2, 1]


def _hg_fill_masks(mask_scr, C):
    ri = lax.broadcasted_iota(jnp.int32, (C, C), 0)
    ci = lax.broadcasted_iota(jnp.int32, (C, C), 1)
    for i, blk in enumerate(_hg_level_blocks(C)):
        if blk == 1:
            keep = ri == ci
        else:
            shift, w = blk.bit_length() - 1, blk // 2
            keep = ((ri >> shift) == (ci >> shift)) & ((ri & w) != 0) & ((ci & w) == 0)
        mask_scr[i] = jnp.where(keep, 1.0, 0.0).astype(F32)


def _hg_operands(qf, kf, e):
    if e is None:
        return qf.astype(BF16), kf.astype(BF16)
    return (qf * e).astype(BF16), (kf * e).astype(BF16)


def _hg_scores(qf, kf, levels, mask_scr):
    A = None
    for n, e in enumerate(levels):
        a = _dot_nt(*_hg_operands(qf, kf, e)) * mask_scr[n]
        A = a if A is None else A + a
    return A


def _hg_specs(B, NC, D, C, dtype_blocks):
    def spec(col0, rev):
        c0 = col0 // HG_PAIR
        if rev:
            return pl.BlockSpec((C, HG_PAIR * HEAD), lambda b, h, j: (b * NC + (NC - 1 - j), c0 + h))
        return pl.BlockSpec((C, HG_PAIR * HEAD), lambda b, h, j: (b * NC + j, c0 + h))
    return spec


def _head(ref, hh):
    return ref.at[:, pl.ds(hh * HEAD, HEAD)]


class _Ride:
    def __init__(self, reads, bufs, outs, phases, fractions, sems):
        self.reads, self.bufs, self.outs = list(reads), list(bufs), list(outs)
        self.phases, self.fractions, self.sems = list(phases), list(fractions), list(sems)


def _ride_call(body, ride, *, name, grid, in_specs, out_specs, out_shape, scratch_shapes, semantics, operands):
    if ride is None:
        return pl.pallas_call(body, name=name, grid=grid, in_specs=in_specs, out_specs=out_specs, out_shape=out_shape,
                              scratch_shapes=scratch_shapes, compiler_params=_cp(semantics))(*operands)
    n_in, n_out, n_scr = len(in_specs), len(out_specs), len(scratch_shapes)
    nr, nb, no = len(ride.reads), len(ride.bufs), len(ride.outs)
    last_step = math.prod(grid) - 1

    def full_body(*refs):
        own = refs[:n_in] + refs[n_in + nr + nb:n_in + nr + nb + n_out]
        tail = refs[n_in + nr + nb + n_out:]
        ride_refs = (refs[n_in:n_in + nr], tail[:nb], tail[nb:nb + no])
        scr = tail[nb + no:]
        step = pl.program_id(0)
        for d in range(1, len(grid)):
            step = step * grid[d] + pl.program_id(d)
        for phase, frac in zip(ride.phases, ride.fractions):
            @pl.when(step == int(round(frac * last_step)))
            def _(phase=phase):
                phase(*ride_refs, *scr[n_scr:])
        body(*own, *scr[:n_scr])

    return pl.pallas_call(
        full_body, name=name, grid=grid,
        in_specs=list(in_specs) + [ANY] * (nr + nb),
        out_specs=list(out_specs) + [ANY] * (nb + no),
        out_shape=list(out_shape) + [jax.ShapeDtypeStruct(b.shape, b.dtype) for b in ride.bufs] + ride.outs,
        input_output_aliases={n_in + nr + j: n_out + j for j in range(nb)},
        scratch_shapes=list(scratch_shapes) + ride.sems,
        compiler_params=_cp(("arbitrary",) * len(grid)),
    )(*operands, *ride.reads, *ride.bufs)


def _hg_fwd(proj, B, lb, gn, ride=None):
    T = proj.shape[0]
    D = proj.shape[1] // N_SEG
    S = T // B
    C = min(HG_CHUNK, S)
    NC = S // C
    H = D // HEAD
    hpd = D // HEAD
    spec = _hg_specs(B, NC, D, C, None)

    def body(q_ref, z_ref, v_ref, g_ref, lb_ref, gn_ref, yb_ref, o_ref, st_ref, a_ref, cum_ref, st_scr, mask_scr):
        @pl.when(pl.program_id(2) == 0)
        def _():
            st_scr[...] = jnp.zeros((HG_PAIR, HEAD, HEAD), F32)
            _hg_fill_masks(mask_scr, C)

        for hh in range(HG_PAIR):
            s_t = st_scr[hh]
            st_ref[hh] = s_t
            qf, _, kf, lf, _, _ = _hg_gates(_head(q_ref, hh)[...], _head(z_ref, hh)[...], _head(lb_ref, hh)[...])
            cum = _hg_cum(lf, C)
            _head(cum_ref, hh)[...] = cum
            A = _hg_scores(qf, kf, _hg_levels(lf, cum, C), mask_scr).astype(BF16)
            a_ref[hh] = A
            vb = _head(v_ref, hh)[...].astype(BF16)
            o = _dot_nt((qf * jnp.exp(cum)).astype(BF16), s_t.astype(BF16)) + _dot(A, vb)
            last = jnp.sum(lf, axis=0, keepdims=True)
            kend = kf * jnp.exp(last - cum)
            st_scr[hh] = jnp.exp(last) * s_t + _dot_tn(vb, kend.astype(BF16))
            _head(o_ref, hh)[...] = o
            g = _head(g_ref, hh)[...]
            _head(yb_ref, hh)[...] = (o * _rms(o) * gn_ref[...]) * (g * _sig(g))

    HG = H // HG_PAIR
    return _ride_call(
        body, ride, name="hg_fwd",
        grid=(B, HG, NC),
        in_specs=[spec(2 * hpd, False), spec(3 * hpd, False), spec(4 * hpd, False), spec(5 * hpd, False),
                  pl.BlockSpec((1, HG_PAIR * HEAD), lambda b, h, j: (0, h)),
                  pl.BlockSpec((1, HEAD), lambda b, h, j: (0, 0))],
        out_specs=[spec(0, False), spec(0, False),
                   pl.BlockSpec((None, HG_PAIR, None, HEAD, HEAD), lambda b, h, j: (b, h, j, 0, 0)),
                   pl.BlockSpec((None, HG_PAIR, None, C, C), lambda b, h, j: (b, h, j, 0, 0)), spec(0, False)],
        out_shape=[jax.ShapeDtypeStruct((T, D), F32), jax.ShapeDtypeStruct((T, D), F32),
                   jax.ShapeDtypeStruct((B, H, NC, HEAD, HEAD), F32),
                   jax.ShapeDtypeStruct((B, H, NC, C, C), BF16), jax.ShapeDtypeStruct((T, D), F32)],
        scratch_shapes=[pltpu.VMEM((HG_PAIR, HEAD, HEAD), F32), pltpu.VMEM((len(_hg_level_blocks(C)), C, C), F32)],
        semantics=("parallel", "parallel", "arbitrary"),
        operands=(proj, proj, proj, proj, lb, gn))


def _w_full(ref):
    s, r, c = ref.shape
    return ref[...].reshape(s * r, c)


def _out_fwd(ya, yb, proj, x2, w_st, layer):
    T, D = x2.shape
    tm = _tile(T, 512)

    def body(ya_ref, yb_ref, ma_ref, mb_ref, x_ref, w_ref, xm_ref, y_ref):
        y = (_sig(ma_ref[...]) * ya_ref[...] + _sig(mb_ref[...]) * yb_ref[...]).astype(BF16)
        y_ref[...] = y
        xm_ref[...] = x_ref[...] + _dot(y, _w_full(w_ref))

    row = pl.BlockSpec((tm, D), lambda i: (i, 0))
    return pl.pallas_call(
        body, name="out_fwd",
        grid=(T // tm,),
        in_specs=[row, row, pl.BlockSpec((tm, D), lambda i: (i, 6)), pl.BlockSpec((tm, D), lambda i: (i, 7)), row,
                  pl.BlockSpec((N_SHARD, None, D // N_SHARD, D), lambda i: (0, layer, 0, 0))],
        out_specs=[row, row],
        out_shape=[jax.ShapeDtypeStruct((T, D), F32), jax.ShapeDtypeStruct((T, D), BF16)],
        compiler_params=_cp(("parallel",)),
    )(ya, yb, proj, proj, x2, w_st)


def _mlp_fwd(xm, gain, wup_st, wdn_st, layer):
    T, D = xm.shape
    F4 = wup_st.shape[3]
    tm = _tile(T, 1024)

    def body(x_ref, g_ref, wu_ref, wd_ref, xo_ref, up_ref, h_ref):
        @pl.when(pl.program_id(1) == 0)
        def _():
            x = x_ref[...]
            h_ref[...] = (x * _rms(x) * g_ref[...]).astype(BF16)
            xo_ref[...] = x
        up = _dot(h_ref[...], wu_ref[...])
        up_ref[...] = up.astype(BF16)
        act = jnp.maximum(up, 0.0)
        xo_ref[...] += _dot((act * act).astype(BF16), wd_ref[...])

    row = pl.BlockSpec((tm, D), lambda i, s: (i, 0))
    return pl.pallas_call(
        body, name="mlp_fwd",
        grid=(T // tm, N_SHARD),
        in_specs=[row, pl.BlockSpec((1, D), lambda i, s: (0, 0)),
                  pl.BlockSpec((None, None, D, F4), lambda i, s: (s, layer, 0, 0)),
                  pl.BlockSpec((None, None, F4, D), lambda i, s: (s, layer, 0, 0))],
        out_specs=[row, pl.BlockSpec((tm, F4), lambda i, s: (i, s)), row],
        out_shape=[jax.ShapeDtypeStruct((T, D), F32), jax.ShapeDtypeStruct((T, N_SHARD * F4), BF16),
                   jax.ShapeDtypeStruct((T, D), BF16)],
        compiler_params=_cp(("parallel", "arbitrary")),
    )(xm, gain, wup_st, wdn_st)


def _final_loss(x2, gain, tgt):
    T, D = x2.shape
    tm = _tile(T, 512)
    nt = T // tm

    def body(x_ref, g_ref, t_ref, loss_ref, dx_ref, dg_ref):
        x = x_ref[...]
        rs = _rms(x)
        err = x * rs * g_ref[...] - t_ref[...]
        part = 0.5 * jnp.sum(jnp.sum(err * err, axis=-1, keepdims=True) * (1.0 / D), axis=0, keepdims=True)
        loss_ref[...] = jnp.broadcast_to(part, (SUBLANES, 128))
        dx, dg = _rms_bwd(err * (1.0 / D), x, rs, g_ref[...])
        dx_ref[...] = dx
        dg_ref[...] = dg

    row = pl.BlockSpec((tm, D), lambda i: (i, 0))
    return pl.pallas_call(
        body, name="final_loss",
        grid=(nt,),
        in_specs=[row, pl.BlockSpec((1, D), lambda i: (0, 0)), row],
        out_specs=[pl.BlockSpec((None, SUBLANES, 128), lambda i: (i, 0, 0)), row,
                   pl.BlockSpec((None, SUBLANES, D), lambda i: (i, 0, 0))],
        out_shape=[jax.ShapeDtypeStruct((nt, SUBLANES, 128), F32), jax.ShapeDtypeStruct((T, D), F32),
                   jax.ShapeDtypeStruct((nt, SUBLANES, D), F32)],
        compiler_params=_cp(("parallel",)),
    )(x2, gain, tgt)


def _mlp_bwd_x(dx, xm, up, gain, wup_st, wdn_st, layer, ride=None):
    T, D = xm.shape
    F4 = wup_st.shape[3]
    tm = _tile(T, 1024)
    nt = T // tm

    def body(dx_ref, x_ref, up_ref, g_ref, wu_ref, wd_ref, dxm_ref, dup_ref, dg_ref, dxb):
        s = pl.program_id(1)

        @pl.when(s == 0)
        def _():
            dxb[...] = dx_ref[...].astype(BF16)
            dxm_ref[...] = jnp.zeros((tm, D), F32)

        d_act = _dot_nt(dxb[...], wd_ref[...])
        d_up = (d_act * (2.0 * jnp.maximum(up_ref[...].astype(F32), 0.0))).astype(BF16)
        dup_ref[...] = d_up
        dxm_ref[...] += _dot_nt(d_up, wu_ref[...])

        @pl.when(s == N_SHARD - 1)
        def _():
            x = x_ref[...]
            dxn, dg = _rms_bwd(dxm_ref[...], x, _rms(x), g_ref[...])
            dxm_ref[...] = dx_ref[...] + dxn
            dg_ref[...] = dg

    row = pl.BlockSpec((tm, D), lambda i, s: (i, 0))
    return _ride_call(
        body, ride, name="mlp_bwd_x",
        grid=(nt, N_SHARD),
        in_specs=[row, row, pl.BlockSpec((tm, F4), lambda i, s: (i, s)), pl.BlockSpec((1, D), lambda i, s: (0, 0)),
                  pl.BlockSpec((None, None, D, F4), lambda i, s: (s, layer, 0, 0)),
                  pl.BlockSpec((None, None, F4, D), lambda i, s: (s, layer, 0, 0))],
        out_specs=[row, pl.BlockSpec((tm, F4), lambda i, s: (i, s)),
                   pl.BlockSpec((None, SUBLANES, D), lambda i, s: (i, 0, 0)), row],
        out_shape=[jax.ShapeDtypeStruct((T, D), F32), jax.ShapeDtypeStruct((T, N_SHARD * F4), BF16),
                   jax.ShapeDtypeStruct((nt, SUBLANES, D), F32), jax.ShapeDtypeStruct((T, D), BF16)],
        scratch_shapes=[],
        semantics=("parallel", "arbitrary"),
        operands=(dx, xm, up, gain, wup_st, wdn_st))


def _layer_slot(bufs, shapes, n_layers):
    out_shape = [jax.ShapeDtypeStruct((N_SHARD, n_layers) + s, F32) for s in shapes]
    return out_shape, ([] if bufs is None else list(bufs))


def _mlp_bwd_w(up, dxb, h, dup, layer, n_layers, bufs):
    T, D = dxb.shape
    F4 = up.shape[1] // N_SHARD
    tk = _tile(T, 1024)
    out_shape, extra = _layer_slot(bufs, [(D, F4), (F4, D)], n_layers)

    def body(up_ref, dx_ref, h_ref, dup_ref, *rest):
        gu_ref, gd_ref = rest[-2:]

        @pl.when(pl.program_id(1) == 0)
        def _():
            gu_ref[...] = jnp.zeros((D, F4), F32)
            gd_ref[...] = jnp.zeros((F4, D), F32)
        act = jnp.maximum(up_ref[...], 0.0)
        gd_ref[...] += _dot_tn(act * act, dx_ref[...])
        gu_ref[...] += _dot_tn(h_ref[...], dup_ref[...])

    return pl.pallas_call(
        body, name="mlp_bwd_w",
        grid=(N_SHARD, T // tk),
        in_specs=[pl.BlockSpec((tk, F4), lambda s, t: (t, s)), pl.BlockSpec((tk, D), lambda s, t: (t, 0)),
                  pl.BlockSpec((tk, D), lambda s, t: (t, 0)), pl.BlockSpec((tk, F4), lambda s, t: (t, s))]
        + [ANY] * len(extra),
        out_specs=[pl.BlockSpec((None, None, D, F4), lambda s, t: (s, layer, 0, 0)),
                   pl.BlockSpec((None, None, F4, D), lambda s, t: (s, layer, 0, 0))],
        out_shape=out_shape,
        input_output_aliases={4 + i: i for i in range(len(extra))},
        compiler_params=_cp(("parallel", "arbitrary")),
    )(up, dxb, h, dup, *extra)


def _out_bwd_x(dxm, ya, yb, proj, w_st, layer):
    T, D = dxm.shape
    tm = _tile(T, 512)

    def body(dx_ref, ya_ref, yb_ref, ma_ref, mb_ref, w_ref, dya_ref, dyb_ref, dma_ref, dmb_ref):
        dy = _dot_nt(dx_ref[...].astype(BF16), _w_full(w_ref))
        sa = _sig(ma_ref[...])
        sb = _sig(mb_ref[...])
        dya_ref[...] = dy * sa
        dyb_ref[...] = dy * sb
        dma_ref[...] = (dy * ya_ref[...] * (sa * (1.0 - sa))).astype(BF16)
        dmb_ref[...] = (dy * yb_ref[...] * (sb * (1.0 - sb))).astype(BF16)

    row = pl.BlockSpec((tm, D), lambda i: (i, 0))
    return pl.pallas_call(
        body, name="out_bwd_x",
        grid=(T // tm,),
        in_specs=[row, row, row, pl.BlockSpec((tm, D), lambda i: (i, 6)), pl.BlockSpec((tm, D), lambda i: (i, 7)),
                  pl.BlockSpec((N_SHARD, None, D // N_SHARD, D), lambda i: (0, layer, 0, 0))],
        out_specs=[row] * 4,
        out_shape=[jax.ShapeDtypeStruct((T, D), F32)] * 2 + [jax.ShapeDtypeStruct((T, D), BF16)] * 2,
        compiler_params=_cp(("parallel",)),
    )(dxm, ya, yb, proj, proj, w_st)


def _out_bwd_w(ymix, dxm, layer, n_layers, bufs):
    T, D = dxm.shape
    tk = _tile(T, 1024)
    out_shape, extra = _layer_slot(bufs, [(D // N_SHARD, D)], n_layers)

    def body(y_ref, dx_ref, *rest):
        g_ref = rest[-1]

        @pl.when(pl.program_id(0) == 0)
        def _():
            g_ref[...] = jnp.zeros((N_SHARD, D // N_SHARD, D), F32)
        g = _dot_tn(y_ref[...], dx_ref[...].astype(BF16))
        g_ref[...] += g.reshape(N_SHARD, D // N_SHARD, D)

    row = pl.BlockSpec((tk, D), lambda t: (t, 0))
    return pl.pallas_call(
        body, name="out_bwd_w",
        grid=(T // tk,),
        in_specs=[row, row] + [ANY] * len(extra),
        out_specs=[pl.BlockSpec((N_SHARD, None, D // N_SHARD, D), lambda t: (0, layer, 0, 0))],
        out_shape=out_shape,
        input_output_aliases={2 + i: i for i in range(len(extra))},
        compiler_params=_cp(("arbitrary",)),
    )(ymix, dxm, *extra)[0]


def _rg_bwd(proj, hrg, dya, B, cw, cb, wr, br, wi, bi, sp):
    T = proj.shape[0]
    D = proj.shape[1] // N_SEG
    S = T // B
    ts = _tile(S, RG_TILE)
    nts = S // ts
    nb = D // RG_BLOCK
    t8 = ts // SUBLANES

    def body(xa_ref, xp_ref, ga_ref, h_ref, hp_ref, dya_ref, cw_ref, cb_ref, wr_ref, br_ref, wi_ref, bi_ref, sp_ref,
             dxa_ref, dga_ref, gwr_ref, gwi_ref, gcw_ref, gcb_ref, gbr_ref, gbi_ref, gsp_ref,
             xbuf, hbuf, abuf, dbuf, g_scr, c_scr, gcar):
        b = pl.program_id(0)
        j = pl.program_id(1)
        first_in_time = j == nts - 1

        @pl.when((b == 0) & (j == 0))
        def _():
            gwr_ref[...] = jnp.zeros((nb, RG_BLOCK, RG_BLOCK), F32)
            gwi_ref[...] = jnp.zeros((nb, RG_BLOCK, RG_BLOCK), F32)
            gcw_ref[...] = jnp.zeros((CONV_TAPS, SUBLANES, D), F32)
            for r in (gcb_ref, gbr_ref, gbi_ref, gsp_ref):
                r[...] = jnp.zeros((SUBLANES, D), F32)

        @pl.when(j == 0)
        def _():
            abuf[pl.ds(ts, SUBLANES), :] = jnp.zeros((SUBLANES, D), F32)
            dbuf[pl.ds(ts, SUBLANES), :] = jnp.zeros((SUBLANES, D), F32)
            gcar[...] = jnp.zeros((SUBLANES, D), F32)

        keep = jnp.where(first_in_time, 0.0, 1.0)
        xbuf[0:SUBLANES, :] = xp_ref[...] * keep
        xbuf[pl.ds(SUBLANES, ts), :] = xa_ref[...]
        hbuf[0:SUBLANES, :] = hp_ref[...] * keep
        hbuf[pl.ds(SUBLANES, ts), :] = h_ref[...]

        xc = _conv_taps(xbuf, cw_ref, ts) + cb_ref[...]
        sp = sp_ref[...]
        r, i, a, mult = _rg_gates(xc, wr_ref, br_ref[...], wi_ref, bi_ref[...], sp)
        g_gate, dg_gate = _gelu_and_grad(ga_ref[...])
        dya = dya_ref[...]
        dga_ref[...] = (dya * h_ref[...] * dg_gate).astype(BF16)

        abuf[0:ts, :] = a
        c_scr[...] = abuf[pl.ds(1, ts), :]
        g_scr[...] = dya * g_gate
        row8 = lax.broadcasted_iota(jnp.int32, (SUBLANES, 1), 0)

        def blk(n, gnext):
            off = pl.multiple_of((t8 - 1 - n) * SUBLANES, SUBLANES)
            c8 = c_scr[pl.ds(off, SUBLANES), :]
            d8 = g_scr[pl.ds(off, SUBLANES), :]
            for d in (1, 2, 4):
                m = row8 < SUBLANES - d
                cn = jnp.where(m, pltpu.roll(c8, SUBLANES - d, 0), 1.0)
                dn = jnp.where(m, pltpu.roll(d8, SUBLANES - d, 0), 0.0)
                d8 = d8 + c8 * dn
                c8 = c8 * cn
            g8 = d8 + c8 * gnext
            g_scr[pl.ds(off, SUBLANES), :] = g8
            first = jnp.sum(jnp.where(row8 == 0, g8, 0.0), axis=0, keepdims=True)
            return jnp.broadcast_to(first, (SUBLANES, D))

        gcar[...] = lax.fori_loop(0, t8, blk, gcar[...])
        abuf[pl.ds(ts, SUBLANES), :] = a[0:SUBLANES, :]

        g = g_scr[...]
        hprev = hbuf[pl.ds(SUBLANES - 1, ts), :]
        gx = i * xc
        e2 = a * a
        dla = g * hprev * a - jnp.where(mult > 0.0, g * gx * e2 / jnp.where(mult > 0.0, mult, 1.0), 0.0)
        dgx = g * mult
        dpr = (dla * ((-RG_C) * sp)) * (r * (1.0 - r))
        dpi = (dgx * xc) * (i * (1.0 - i))
        gsp_ref[...] += _rows8(dla * ((-RG_C) * r))
        gbr_ref[...] += _rows8(dpr)
        gbi_ref[...] += _rows8(dpi)
        dprb = dpr.astype(BF16)
        dpib = dpi.astype(BF16)
        xcb = xc.astype(BF16)
        back = []
        for n in range(nb):
            sl = slice(n * RG_BLOCK, (n + 1) * RG_BLOCK)
            back.append(_dot_nt(dprb[:, sl], wr_ref[n]) + _dot_nt(dpib[:, sl], wi_ref[n]))
            gwr_ref[n] += _dot_tn(xcb[:, sl], dprb[:, sl])
            gwi_ref[n] += _dot_tn(xcb[:, sl], dpib[:, sl])
        dxc = dgx * i + (jnp.concatenate(back, axis=1) if nb > 1 else back[0])
        gcb_ref[...] += _rows8(dxc)

        dbuf[0:ts, :] = dxc
        dxa = None
        for jtap in range(CONV_TAPS):
            term = cw_ref[jtap:jtap + 1, :] * dbuf[pl.ds(CONV_TAPS - 1 - jtap, ts), :]
            dxa = term if dxa is None else dxa + term
            gcw_ref[jtap] += _rows8(dxc * xbuf[pl.ds(SUBLANES - (CONV_TAPS - 1) + jtap, ts), :])
        dxa_ref[...] = dxa.astype(BF16)
        dbuf[pl.ds(ts, SUBLANES), :] = dxc[0:SUBLANES, :]

    def tile_map(col):
        return lambda b, j: (b * nts + (nts - 1 - j), col)

    def prev8_map(col):
        return lambda b, j: (jnp.maximum((b * nts + (nts - 1 - j)) * t8 - 1, 0), col)

    vec = pl.BlockSpec((1, D), lambda b, j: (0, 0))
    gate = pl.BlockSpec((nb, RG_BLOCK, RG_BLOCK), lambda b, j: (0, 0, 0))
    acc8 = pl.BlockSpec((SUBLANES, D), lambda b, j: (0, 0))
    return pl.pallas_call(
        body, name="rg_bwd",
        grid=(B, nts),
        in_specs=[pl.BlockSpec((ts, D), tile_map(0)), pl.BlockSpec((SUBLANES, D), prev8_map(0)),
                  pl.BlockSpec((ts, D), tile_map(1)),
                  pl.BlockSpec((ts, D), tile_map(0)), pl.BlockSpec((SUBLANES, D), prev8_map(0)),
                  pl.BlockSpec((ts, D), tile_map(0)),
                  pl.BlockSpec((CONV_TAPS, D), lambda b, j: (0, 0)), vec, gate, vec, gate, vec, vec],
        out_specs=[pl.BlockSpec((ts, D), tile_map(0)), pl.BlockSpec((ts, D), tile_map(0)), gate, gate,
                   pl.BlockSpec((CONV_TAPS, SUBLANES, D), lambda b, j: (0, 0, 0)), acc8, acc8, acc8, acc8],
        out_shape=[jax.ShapeDtypeStruct((T, D), BF16)] * 2
        + [jax.ShapeDtypeStruct((nb, RG_BLOCK, RG_BLOCK), F32)] * 2
        + [jax.ShapeDtypeStruct((CONV_TAPS, SUBLANES, D), F32)] + [jax.ShapeDtypeStruct((SUBLANES, D), F32)] * 4,
        scratch_shapes=[pltpu.VMEM((SUBLANES + ts, D), F32), pltpu.VMEM((SUBLANES + ts, D), F32),
                        pltpu.VMEM((ts + SUBLANES, D), F32), pltpu.VMEM((ts + SUBLANES, D), F32),
                        pltpu.VMEM((ts, D), F32), pltpu.VMEM((ts, D), F32), pltpu.VMEM((SUBLANES, D), F32)],
        compiler_params=_cp(("arbitrary", "arbitrary")),
    )(proj, proj, proj, hrg, hrg, dya, cw, cb, wr, br, wi, bi, sp)


def _hg_bwd(proj, o_sv, dyb, states, a_sv, cum_sv, B, lb, gn, ride=None):
    T = proj.shape[0]
    D = proj.shape[1] // N_SEG
    S = T // B
    C = min(HG_CHUNK, S)
    NC = S // C
    H = D // HEAD
    hpd = D // HEAD
    spec = _hg_specs(B, NC, D, C, None)

    def body(q_ref, z_ref, v_ref, g_ref, o_ref, dyb_ref, st_ref, a_ref, cum_ref, lb_ref, gn_ref,
             dq_ref, dz_ref, dv_ref, dg_ref, glb_ref, ggn_ref, ds_scr, mask_scr):
        @pl.when(pl.program_id(2) == 0)
        def _():
            ds_scr[...] = jnp.zeros((HG_PAIR, HEAD, HEAD), F32)
            _hg_fill_masks(mask_scr, C)
            glb_ref[...] = jnp.zeros((SUBLANES, HG_PAIR * HEAD), F32)
            ggn_ref[...] = jnp.zeros((HG_PAIR, SUBLANES, HEAD), F32)

        for hh in range(HG_PAIR):
            cols = [_head(r, hh) for r in (q_ref, z_ref, v_ref, g_ref, o_ref, dyb_ref)]
            outs = [_head(r, hh) for r in (dq_ref, dz_ref, dv_ref, dg_ref, glb_ref)]
            one_head(*cols, st_ref.at[hh], a_ref.at[hh], _head(cum_ref, hh), _head(lb_ref, hh), gn_ref,
                     *outs, ggn_ref.at[hh], ds_scr.at[hh], mask_scr)

    def one_head(q_ref, z_ref, v_ref, g_ref, o_ref, dyb_ref, st_ref, a_ref, cum_ref, lb_ref, gn_ref,
                 dq_ref, dz_ref, dv_ref, dg_ref, glb_ref, ggn_ref, ds_scr, mask_scr):
        q = q_ref[...]
        lb = lb_ref[...]
        gn = gn_ref[...]
        qf, qs, kf, lf, fg, sig = _hg_gates(q, z_ref[...], lb)
        cum = cum_ref[...]
        levels = _hg_levels(lf, cum, C)

        o = o_ref[...]
        g = g_ref[...]
        gs = _sig(g)
        rs = _rms(o)
        dyb = dyb_ref[...]
        don = dyb * (g * gs)
        dg_ref[...] = (dyb * (o * rs * gn) * (gs * (1.0 + g * (1.0 - gs)))).astype(BF16)
        ggn_ref[...] += _rows8(don * o * rs)
        dn = don * gn
        do = rs * (dn - o * (rs * rs) * jnp.mean(dn * o, axis=-1, keepdims=True))

        s_t = st_ref[...].astype(BF16)
        ds_t = ds_scr[...]
        ds_b = ds_t.astype(BF16)
        dob = do.astype(BF16)
        vb = v_ref[...].astype(BF16)
        ecum = jnp.exp(cum)
        last = jnp.sum(lf, axis=0, keepdims=True)
        eend = jnp.exp(last - cum)
        qhat = (qf * ecum).astype(BF16)
        kend = (kf * eend).astype(BF16)

        dA = _dot_nt(dob, vb)
        dq_inter = _dot(dob, s_t)
        dk_state = _dot(vb, ds_b)
        dqf = dq_inter * ecum
        dkf = dk_state * eend
        g_intra = None
        for n, e in enumerate(levels):
            qw, kw = _hg_operands(qf, kf, e)
            dam = (dA * mask_scr[n]).astype(BF16)
            rq = _dot(dam, kw)
            rk = _dot_tn(dam, qw)
            dqf += rq if e is None else rq * e
            dkf += rk if e is None else rk * e
            gi = qw.astype(F32) * rq - kw.astype(F32) * rk
            g_intra = gi if g_intra is None else g_intra + gi
        dv_ref[...] = (_dot_tn(a_ref[...], dob) + _dot_nt(kend, ds_b)).astype(BF16)
        e_last = jnp.exp(last)
        ds_scr[...] = e_last * ds_t + _dot_tn(dob, qhat)

        ri = lax.broadcasted_iota(jnp.int32, (C, C), 0)
        ci = lax.broadcasted_iota(jnp.int32, (C, C), 1)
        y_state = kend.astype(F32) * dk_state
        dlf = (_dot_01(jnp.where(ci >= ri, 1.0, 0.0).astype(BF16), g_intra + qhat.astype(F32) * dq_inter - y_state)
               + jnp.sum(y_state, axis=0, keepdims=True)
               + jnp.sum(e_last * st_ref[...] * ds_t, axis=0, keepdims=True))
        dfg = jnp.where(fg > F_MIN, dlf / jnp.maximum(fg, F_MIN), 0.0)
        sneg = 1.0 - sig
        diff = dfg - dkf
        dz_ref[...] = ((1.0 - lb) * sig * sneg * diff).astype(BF16)
        glb_ref[...] += _rows8(sneg * diff)
        dq_ref[...] = (dqf * (qs * (1.0 + q * (1.0 - qs)))).astype(BF16)

    return _ride_call(
        body, ride, name="hg_bwd",
        grid=(B, H // HG_PAIR, NC),
        in_specs=[spec(2 * hpd, True), spec(3 * hpd, True), spec(4 * hpd, True), spec(5 * hpd, True),
                  spec(0, True), spec(0, True),
                  pl.BlockSpec((None, HG_PAIR, None, HEAD, HEAD), lambda b, h, j: (b, h, NC - 1 - j, 0, 0)),
                  pl.BlockSpec((None, HG_PAIR, None, C, C), lambda b, h, j: (b, h, NC - 1 - j, 0, 0)), spec(0, True),
                  pl.BlockSpec((1, HG_PAIR * HEAD), lambda b, h, j: (0, h)),
                  pl.BlockSpec((1, HEAD), lambda b, h, j: (0, 0))],
        out_specs=[spec(0, True)] * 4
        + [pl.BlockSpec((None, SUBLANES, HG_PAIR * HEAD), lambda b, h, j: (b, 0, h)),
           pl.BlockSpec((None, HG_PAIR, SUBLANES, HEAD), lambda b, h, j: (b, h, 0, 0))],
        out_shape=[jax.ShapeDtypeStruct((T, D), BF16)] * 4
        + [jax.ShapeDtypeStruct((B, SUBLANES, D), F32), jax.ShapeDtypeStruct((B, H, SUBLANES, HEAD), F32)],
        scratch_shapes=[pltpu.VMEM((HG_PAIR, HEAD, HEAD), F32), pltpu.VMEM((len(_hg_level_blocks(C)), C, C), F32)],
        semantics=("parallel", "parallel", "arbitrary"),
        operands=(proj, proj, proj, proj, o_sv, dyb, states, a_sv, cum_sv, lb, gn))


def _inproj_bwd_x(dsegs, w_st, layer, x2, gain, dxm, ride=None):
    T, D = x2.shape
    tm = _tile(T, 512)
    nt = T // tm

    def body(*refs):
        seg_refs = refs[:N_SEG]
        w_ref, x_ref, g_ref, dxm_ref, dx_ref, dg_ref = refs[N_SEG:]
        k = pl.program_id(1)

        @pl.when(k == 0)
        def _():
            dx_ref[...] = jnp.zeros((tm, D), F32)

        for kk in range(N_SEG):
            @pl.when(k == kk)
            def _(kk=kk):
                dx_ref[...] += _dot_nt(seg_refs[kk][...], w_ref[...])

        @pl.when(k == N_SEG - 1)
        def _():
            x = x_ref[...]
            dxn, dg = _rms_bwd(dx_ref[...], x, _rms(x), g_ref[...])
            dx_ref[...] = dxm_ref[...] + dxn
            dg_ref[...] = dg

    row = pl.BlockSpec((tm, D), lambda i, k: (i, 0))

    def seg_spec(kk):
        return pl.BlockSpec((tm, D), lambda i, k: (jnp.minimum(i + jnp.where(k > kk, 1, 0), nt - 1), 0))

    return _ride_call(
        body, ride, name="inproj_bwd_x",
        grid=(nt, N_SEG),
        in_specs=[seg_spec(kk) for kk in range(N_SEG)]
        + [pl.BlockSpec((None, None, D, D), lambda i, k: (k // 2, layer, 0, k % 2)), row,
           pl.BlockSpec((1, D), lambda i, k: (0, 0)), row],
        out_specs=[row, pl.BlockSpec((None, SUBLANES, D), lambda i, k: (i, 0, 0))],
        out_shape=[jax.ShapeDtypeStruct((T, D), F32), jax.ShapeDtypeStruct((nt, SUBLANES, D), F32)],
        scratch_shapes=[],
        semantics=("parallel", "arbitrary"),
        operands=(*dsegs, w_st, x2, gain, dxm))


def _inproj_bwd_w(h, dsegs, layer, n_layers, bufs):
    T, D = h.shape
    tk = _tile(T, 1024)
    out_shape, extra = _layer_slot(bufs, [(D, 2 * D)], n_layers)

    def body(*refs):
        h_ref = refs[0]
        seg_refs = refs[1:1 + N_SEG]
        g_ref = refs[-1]
        k = pl.program_id(0)

        @pl.when(pl.program_id(1) == 0)
        def _():
            g_ref[...] = jnp.zeros((D, D), F32)

        for kk in range(N_SEG):
            @pl.when(k == kk)
            def _(kk=kk):
                g_ref[...] += _dot_tn(h_ref[...], seg_refs[kk][...])

    def seg_spec(kk):
        return pl.BlockSpec((tk, D), lambda k, t: (jnp.where(k == kk, t, 0), 0))

    return pl.pallas_call(
        body, name="inproj_bwd_w",
        grid=(N_SEG, T // tk),
        in_specs=[pl.BlockSpec((tk, D), lambda k, t: (t, 0))] + [seg_spec(kk) for kk in range(N_SEG)]
        + [ANY] * len(extra),
        out_specs=[pl.BlockSpec((None, None, D, D), lambda k, t: (k // 2, layer, 0, k % 2))],
        out_shape=out_shape,
        input_output_aliases={1 + N_SEG + i: i for i in range(len(extra))},
        compiler_params=_cp(("parallel", "arbitrary")),
    )(h, *dsegs, *extra)[0]


def _softmax_rows(lg_ref, L):
    rows = [lg_ref[l:l + 1, :] for l in range(L)]
    mx = functools.reduce(jnp.maximum, rows)
    es = [jnp.exp(r - mx) for r in rows]
    den = functools.reduce(lambda p, q: p + q, es)
    return [e / den for e in es]


def _prep(lb_logits, lam):
    L, D = lb_logits.shape

    def body(lg_ref, lam_ref, lowb_ref, sp_ref):
        sm = _softmax_rows(lg_ref, L)
        run = sm[0]
        for l in range(L):
            if l > 0:
                run = run + sm[l]
            lowb_ref[l:l + 1, :] = jnp.clip(run - sm[0], 0.0, 1.0)
        y = -lam_ref[...]
        sp_ref[...] = jnp.maximum(y, 0.0) + jnp.log1p(jnp.exp(-jnp.abs(y)))

    return pl.pallas_call(
        body, name="prep_small",
        out_shape=[jax.ShapeDtypeStruct((L, D), F32)] * 2,
    )(lb_logits, lam)


def _local_step(x, tgt, lowb, sp, norm_mix, wbufs, conv_w, conv_b, w_r, b_r, w_i, b_i, hg_norm,
                norm_mlp, norm_final):
    B, S, D = x.shape
    L = norm_mix.shape[0]
    T = B * S
    x2 = x.reshape(T, D)
    row = lambda a, l: a[l:l + 1]

    def weight_views(bufs):
        f4 = bufs[2].shape[-1]
        return (bufs[0].reshape(N_SHARD, L, D, 2 * D), bufs[1].reshape(N_SHARD, L, D // N_SHARD, D),
                bufs[2].reshape(N_SHARD, L, D, f4), bufs[3].reshape(N_SHARD, L, f4, D))

    w_in_st, w_out_st, w_up_st, w_down_st = weight_views(wbufs)
    saved = []
    for l in range(L):
        proj, h = _inproj_fwd(x2, row(norm_mix, l), w_in_st, l)
        ya, hrg = _rg_fwd(proj, B, conv_w[l], row(conv_b, l), w_r[l], row(b_r, l), w_i[l], row(b_i, l), row(sp, l))
        l0, nl = (1, min(2, L - 1)) if l == 0 else (l + 2, 1 if l + 2 < L else 0)
        if nl > 0:
            yb, o, st, a_sv, cum_sv, *wbufs = _hg_fwd(proj, B, row(lowb, l), row(hg_norm, l),
                                                      _gather_ride(list(wbufs), LINK_SPLIT, l0, nl))
            w_in_st, w_out_st, w_up_st, w_down_st = weight_views(wbufs)
        else:
            yb, o, st, a_sv, cum_sv = _hg_fwd(proj, B, row(lowb, l), row(hg_norm, l))
        xm, ymix = _out_fwd(ya, yb, proj, x2, w_out_st, l)
        xo, up, h2 = _mlp_fwd(xm, row(norm_mlp, l), w_up_st, w_down_st, l)
        saved.append((x2, proj, h, ya, hrg, yb, o, (st, a_sv, cum_sv), xm, ymix, up, h2))
        x2 = xo
    loss_parts, dx, g_nf = _final_loss(x2, norm_final[None, :], tgt.reshape(T, D))

    def reduce_part(g_in, g_out, g_mlp, gate_list):
        gate_list = gate_list[::-1]
        grads = [g_in, g_out, g_mlp[0], g_mlp[1], _shard_gate(jnp.stack([g[0] for g in gate_list])),
                 _shard_gate(jnp.stack([g[1] for g in gate_list]))]
        return _ReduceScatter([g.reshape(N_SHARD, -1, g.shape[-1]) for g in grads], LINK_SPLIT)

    gates = []
    small = []
    g_in = g_out = g_mlp = None
    rest = None
    for l in reversed(range(L)):
        x_in, proj, h, ya, hrg, yb, o, st, xm, ymix, up, h2 = saved[l]
        alone = l == 0 and L > 1
        if alone:
            rest = reduce_part(g_in, g_out, g_mlp, gates)
            g_in = g_out = g_mlp = None
            gates = []
        slot, n_slots = (0, 1) if l == 0 else (l - 1, L - 1)
        dxm, dup, g_nmlp, dxb, *got = _mlp_bwd_x(dx, xm, up, row(norm_mlp, l), w_up_st, w_down_st, l,
                                                 rest.ride_c() if alone else None)
        if alone:
            rest.after_c(got)
        g_mlp = _mlp_bwd_w(up, dxb, h2, dup, slot, n_slots, g_mlp)
        dya, dyb, dma, dmb = _out_bwd_x(dxm, ya, yb, proj, w_out_st, l)
        g_out = _out_bwd_w(ymix, dxm, slot, n_slots, None if g_out is None else [g_out])
        dxa, dga, g_wr, g_wi, g_cw, g_cb, g_br, g_bi, g_sp = _rg_bwd(
            proj, hrg, dya, B, conv_w[l], row(conv_b, l), w_r[l], row(b_r, l), w_i[l], row(b_i, l), row(sp, l))
        dq, dz, dv, dg, g_lb, g_gn, *got = _hg_bwd(proj, o, dyb, *st, B, row(lowb, l), row(hg_norm, l),
                                                   rest.ride_1() if alone else None)
        if alone:
            rest.after_1(got)
        dsegs = (dxa, dga, dq, dz, dv, dg, dma, dmb)
        dx, g_nmix, *got = _inproj_bwd_x(dsegs, w_in_st, l, x_in, row(norm_mix, l), dxm,
                                         rest.ride_2() if alone else None)
        if alone:
            rest.after_2(got)
        g_in = _inproj_bwd_w(h, dsegs, slot, n_slots, None if g_in is None else [g_in])
        gates.append((g_wr, g_wi))
        small.append((g_lb, g_nmix, g_cb, g_br, g_bi, g_sp, g_nmlp, g_gn, g_cw))
    small.reverse()
    first = reduce_part(g_in, g_out, g_mlp, gates)
    first.exchange_c()
    first.exchange_1()
    first.exchange_2()
    parts = [first.finish()] + ([rest.finish()] if rest is not None else [])
    return loss_parts, dx.reshape(B, S, D), parts, small, g_nf


def _me():
    return lax.axis_index("x"), lax.axis_index("y"), lax.axis_index("c")


def _cast_place(w, slot):
    R, N = w.shape
    tr = _tile(R, max(16, (1 << 20) // N))

    def body(slot_ref, w_ref, o_ref):
        o_ref[...] = w_ref[...].astype(BF16)

    return pl.pallas_call(
        body, name="cast_place",
        grid_spec=pltpu.PrefetchScalarGridSpec(
            num_scalar_prefetch=1, grid=(R // tr,),
            in_specs=[pl.BlockSpec((tr, N), lambda i, slot: (i, 0))],
            out_specs=pl.BlockSpec((None, tr, N), lambda i, slot: (slot[0], i, 0))),
        out_shape=jax.ShapeDtypeStruct((N_SHARD, R, N), BF16),
        compiler_params=_cp(("parallel",)),
    )(slot, w)


def _gather_weights(bufs, first_axes, l0, nl):
    n = len(bufs)
    phases = _gather_phases(n, first_axes, l0, nl)

    def body(*refs):
        outs = refs[n:2 * n]
        ssem, rsem = refs[2 * n:]
        for ph in phases:
            ph(outs, ssem, rsem)

    return pl.pallas_call(
        body, name="gather_weights",
        in_specs=[ANY] * n, out_specs=[ANY] * n,
        out_shape=[jax.ShapeDtypeStruct(b.shape, b.dtype) for b in bufs],
        input_output_aliases={a: a for a in range(n)},
        scratch_shapes=_gather_sems(n),
        compiler_params=pltpu.CompilerParams(has_side_effects=True),
    )(*bufs)


def _gather_sems(n):
    return [pltpu.SemaphoreType.DMA((n, 6)), pltpu.SemaphoreType.DMA((n, 6))]


def _gather_phases(n, first_axes, l0, nl):
    def ctx(outs, ssem, rsem):
        x, y, c = _me()

        def piece(a, flips, half):
            sx = 1 - x if flips[0] else x
            sy = 1 - y if flips[1] else y
            return outs[a].at[2 * sx + sy, pl.ds(l0[a], nl[a]), half]

        def rcopy(a, k, ref, dev):
            return pltpu.make_async_remote_copy(src_ref=ref, dst_ref=ref, send_sem=ssem.at[a, k], recv_sem=rsem.at[a, k],
                                                device_id=dev, device_id_type=MESH_ID)

        def route(a):
            fx = first_axes[a] == "x"
            f_dev = (1 - x, y, c) if fx else (x, 1 - y, c)
            g_dev = (x, 1 - y, c) if fx else (1 - x, y, c)
            return f_dev, g_dev, ((1, 0) if fx else (0, 1)), ((0, 1) if fx else (1, 0))

        return c, (x, y, 1 - c), piece, rcopy, route

    def own_halves(outs, ssem, rsem):
        c, sib, piece, rcopy, route = ctx(outs, ssem, rsem)
        for a in range(n):
            f_dev, g_dev, _, _ = route(a)
            own = piece(a, (0, 0), c)
            rcopy(a, 0, own, f_dev).start()
            rcopy(a, 1, own, g_dev).start()

    def pass_on_neighbours(outs, ssem, rsem):
        c, sib, piece, rcopy, route = ctx(outs, ssem, rsem)
        for a in range(n):
            f_dev, g_dev, f_flip, _ = route(a)
            got = piece(a, f_flip, c)
            rcopy(a, 0, got, f_dev).wait_recv()
            rcopy(a, 2, got, g_dev).start()
            rcopy(a, 3, got, sib).start()
        for a in range(n):
            _, g_dev, _, g_flip = route(a)
            got = piece(a, g_flip, c)
            rcopy(a, 1, got, g_dev).wait_recv()
            rcopy(a, 4, got, sib).start()

    def pass_on_diagonal(outs, ssem, rsem):
        c, sib, piece, rcopy, route = ctx(outs, ssem, rsem)
        for a in range(n):
            _, g_dev, _, _ = route(a)
            got = piece(a, (1, 1), c)
            rcopy(a, 2, got, g_dev).wait_recv()
            rcopy(a, 5, got, sib).start()

    def drain(outs, ssem, rsem):
        c, sib, piece, rcopy, route = ctx(outs, ssem, rsem)
        for a in range(n):
            f_dev, g_dev, f_flip, g_flip = route(a)
            for k, fl in ((3, f_flip), (4, g_flip), (5, (1, 1))):
                rcopy(a, k, piece(a, fl, 1 - c), sib).wait_recv()
            own = piece(a, (0, 0), c)
            for k, dev in ((0, f_dev), (1, g_dev), (2, g_dev), (3, sib), (4, sib), (5, sib)):
                rcopy(a, k, own, dev).wait_send()

    return [own_halves, pass_on_neighbours, pass_on_diagonal, drain]


def _gather_ride(bufs, first_axes, l0, nl):
    n = len(bufs)
    phases = [lambda reads, refs, outs, ssem, rsem, ph=ph: ph(refs, ssem, rsem)
              for ph in _gather_phases(n, first_axes, [l0] * n, [nl] * n)]
    return _Ride([], bufs, [], phases, GATHER_STEPS, _gather_sems(n))


def _shard_gate(g):
    n_l, nb = g.shape[:2]
    return g.reshape(n_l, nb, N_SHARD, RG_BLOCK // N_SHARD, RG_BLOCK).transpose(2, 0, 1, 3, 4)


def _exchange(arrs, axes, name):
    n = len(arrs)

    def body(*refs):
        ins, outs = refs[:n], refs[n:2 * n]
        ssem, rsem = refs[2 * n:]
        x, y, c = _me()
        cps = []
        for a in range(n):
            my = {"x": x, "y": y, "c": c}[axes[a]]
            partner = {"x": (1 - x, y, c), "y": (x, 1 - y, c), "c": (x, y, 1 - c)}[axes[a]]
            cps.append(pltpu.make_async_remote_copy(
                src_ref=ins[a].at[:, 1 - my], dst_ref=outs[a], send_sem=ssem.at[a], recv_sem=rsem.at[a],
                device_id=partner, device_id_type=MESH_ID))
            cps[-1].start()
        for cp in cps:
            cp.wait()

    return pl.pallas_call(
        body, name=name,
        in_specs=[ANY] * n, out_specs=[ANY] * n,
        out_shape=[jax.ShapeDtypeStruct((a.shape[0],) + a.shape[2:], a.dtype) for a in arrs],
        scratch_shapes=[pltpu.SemaphoreType.DMA((n,)), pltpu.SemaphoreType.DMA((n,))],
        compiler_params=pltpu.CompilerParams(has_side_effects=True),
    )(*arrs)


def _add_kept(arr, got, idx, name, with_bf16):
    P, _, R, N = arr.shape
    tr = _tile(R, max(16, (1 << 20) // N))

    def body(idx_ref, a_ref, g_ref, o_ref, *ob_ref):
        s = a_ref[...] + g_ref[...].astype(F32)
        o_ref[...] = s
        if with_bf16:
            ob_ref[0][...] = s.astype(BF16)

    out_blk = pl.BlockSpec((None, tr, N), lambda p, i, idx: (p, i, 0))
    return pl.pallas_call(
        body, name=name,
        grid_spec=pltpu.PrefetchScalarGridSpec(
            num_scalar_prefetch=1, grid=(P, R // tr),
            in_specs=[pl.BlockSpec((None, None, tr, N), lambda p, i, idx: (p, idx[0], i, 0)),
                      pl.BlockSpec((None, tr, N), lambda p, i, idx: (p, i, 0))],
            out_specs=[out_blk] * (2 if with_bf16 else 1)),
        out_shape=[jax.ShapeDtypeStruct((P, R, N), F32)] + ([jax.ShapeDtypeStruct((P, R, N), BF16)] if with_bf16 else []),
        compiler_params=_cp(("parallel", "parallel")),
    )(idx, arr, got)


def _share_halves(halves):
    n = len(halves)

    def body(*refs):
        ins, outs = refs[:n], refs[n:2 * n]
        ssem, rsem = refs[2 * n:]
        x, y, c = _me()
        cps = []
        for a in range(n):
            cps.append(pltpu.make_async_remote_copy(
                src_ref=ins[a], dst_ref=outs[a], send_sem=ssem.at[a], recv_sem=rsem.at[a],
                device_id=(x, y, 1 - c), device_id_type=MESH_ID))
            cps[-1].start()
        for cp in cps:
            cp.wait()

    return pl.pallas_call(
        body, name="share_halves",
        in_specs=[ANY] * n, out_specs=[ANY] * n,
        out_shape=[jax.ShapeDtypeStruct(h.shape, h.dtype) for h in halves],
        scratch_shapes=[pltpu.SemaphoreType.DMA((n,)), pltpu.SemaphoreType.DMA((n,))],
        compiler_params=pltpu.CompilerParams(has_side_effects=True),
    )(*halves)


def _exchange_ride(arrs, axes):
    n = len(arrs)

    def copies(reads, outs, ssem, rsem):
        x, y, c = _me()
        cps = []
        for a in range(n):
            my = {"x": x, "y": y, "c": c}[axes[a]]
            partner = {"x": (1 - x, y, c), "y": (x, 1 - y, c), "c": (x, y, 1 - c)}[axes[a]]
            cps.append(pltpu.make_async_remote_copy(
                src_ref=reads[a].at[:, 1 - my], dst_ref=outs[a], send_sem=ssem.at[a], recv_sem=rsem.at[a],
                device_id=partner, device_id_type=MESH_ID))
        return cps

    def start(reads, bufs, outs, ssem, rsem):
        for cp in copies(reads, outs, ssem, rsem):
            cp.start()

    def finish(reads, bufs, outs, ssem, rsem):
        for cp in copies(reads, outs, ssem, rsem):
            cp.wait()

    landing = [jax.ShapeDtypeStruct((a.shape[0],) + a.shape[2:], a.dtype) for a in arrs]
    return _Ride(arrs, [], landing, [start, finish], (0.0, 1.0),
                 [pltpu.SemaphoreType.DMA((n,)), pltpu.SemaphoreType.DMA((n,))])


class _ReduceScatter:
    def __init__(self, grads, first_axes):
        x, y, c = _me()
        idx = lambda v: jnp.reshape(v, (1,)).astype(jnp.int32)
        self.coord = {"x": idx(x), "y": idx(y), "c": idx(c)}
        self.first = list(first_axes)
        self.second = ["y" if f == "x" else "x" for f in first_axes]
        self.views_c = [g.reshape(N_SHARD, 2, g.shape[1] // 2, g.shape[2]) for g in grads]

    def ride_c(self):
        return _exchange_ride(self.views_c, "c" * len(self.views_c))

    def exchange_c(self):
        self.after_c(_exchange(self.views_c, "c" * len(self.views_c), "rs_exchange_c"))

    def after_c(self, got):
        self.summed = [_add_kept(v, r, self.coord["c"], "rs_add_c", True) for v, r in zip(self.views_c, got)]

    @staticmethod
    def _split_view(a, ax):
        _, rh, nn = a.shape
        return a.reshape(1, 2, 2 * rh, nn) if ax == "x" else a.reshape(2, 2, rh, nn)

    def _views_1(self):
        return [self._split_view(s[1], f) for s, f in zip(self.summed, self.first)]

    def ride_1(self):
        return _exchange_ride(self._views_1(), self.first)

    def exchange_1(self):
        self.after_1(_exchange(self._views_1(), self.first, "rs_exchange_1"))

    def after_1(self, got):
        self.summed = [_add_kept(self._split_view(s[0], f), r, self.coord[f], "rs_add_1", True)
                       for s, r, f in zip(self.summed, got, self.first)]

    def _views_2(self):
        return [s[1].reshape(1, 2, -1, s[1].shape[-1]) for s in self.summed]

    def ride_2(self):
        return _exchange_ride(self._views_2(), self.second)

    def exchange_2(self):
        self.after_2(_exchange(self._views_2(), self.second, "rs_exchange_2"))

    def after_2(self, got):
        views32 = [s[0].reshape(1, 2, -1, s[0].shape[-1]) for s in self.summed]
        self.kept = [_add_kept(v, r, self.coord[g], "rs_add_2", False)[0][0]
                     for v, r, g in zip(views32, got, self.second)]

    def finish(self):
        return self.kept, _share_halves(self.kept)


def _allgather_small(p):
    R, D = p.shape

    def body(p_ref, o_ref, ssem, rsem):
        x, y, c = _me()
        me = 4 * x + 2 * y + c
        o_ref[me] = p_ref[...]
        cps = []
        for m in range(1, 8):
            mx, my, mc = (m >> 2) & 1, (m >> 1) & 1, m & 1
            peer = (1 - x if mx else x, 1 - y if my else y, 1 - c if mc else c)
            cps.append(pltpu.make_async_remote_copy(
                src_ref=p_ref, dst_ref=o_ref.at[me], send_sem=ssem.at[m - 1], recv_sem=rsem.at[m - 1],
                device_id=peer, device_id_type=MESH_ID))
            cps[-1].start()
        for cp in cps:
            cp.wait()

    return pl.pallas_call(
        body, name="allgather_small",
        in_specs=[pl.BlockSpec(memory_space=pltpu.VMEM)],
        out_specs=pl.BlockSpec(memory_space=pltpu.VMEM),
        out_shape=jax.ShapeDtypeStruct((8, R, D), p.dtype),
        scratch_shapes=[pltpu.SemaphoreType.DMA((7,)), pltpu.SemaphoreType.DMA((7,))],
        compiler_params=pltpu.CompilerParams(has_side_effects=True, vmem_limit_bytes=VMEM_LIMIT),
    )(p)


def _adam_math(w, g, m, v):
    m = ADAM_B1 * m + (1.0 - ADAM_B1) * g
    v = ADAM_B2 * v + (1.0 - ADAM_B2) * (g * g)
    m_hat = m / (1.0 - ADAM_B1 ** ADAM_STEP)
    v_hat = v / (1.0 - ADAM_B2 ** ADAM_STEP)
    delta = -ADAM_LR * (m_hat / (jnp.sqrt(v_hat) + ADAM_EPS) + ADAM_WD * w)
    return delta, m, v


def _adam(w, g_mine, g_sib, m, v, core, row0, layer_rows, outs):
    R, N = w.shape
    rh = g_mine.shape[0]
    tr = _tile(layer_rows // 2, max(16, (1 << 19) // N))
    nt = rh // tr
    t0 = row0 // tr
    extra = [] if outs is None else list(outs)

    def body(c_ref, w_ref, gm_ref, gs_ref, m_ref, v_ref, *rest):
        g_ref, d_ref, nm_ref, nv_ref = rest[-4:]
        g = jnp.where(pl.program_id(0) == c_ref[0], gm_ref[...], gs_ref[...])
        d, nm, nv = _adam_math(w_ref[...], g, m_ref[...], v_ref[...])
        g_ref[...] = g
        d_ref[...] = d
        nm_ref[...] = nm
        nv_ref[...] = nv

    blk = pl.BlockSpec((tr, N), lambda h, i, c: (t0 + h * nt + i, 0))
    half = pl.BlockSpec((tr, N), lambda h, i, c: (i, 0))
    return pl.pallas_call(
        body, name="adamw",
        grid_spec=pltpu.PrefetchScalarGridSpec(
            num_scalar_prefetch=1, grid=(2, nt),
            in_specs=[blk, half, half, blk, blk] + [ANY] * len(extra), out_specs=[blk] * 4),
        out_shape=[jax.ShapeDtypeStruct((R, N), F32)] * 4,
        input_output_aliases={6 + i: i for i in range(len(extra))},
        compiler_params=_cp(("parallel", "parallel")),
    )(core, w, g_mine, g_sib, m, v, *extra)


def _reduce_rows(parts, sizes, rows_out):
    D = parts.shape[1]

    def body(p_ref, o_ref):
        o_ref[...] = jnp.zeros((rows_out, D), F32)
        off = 0
        for i, sz in enumerate(sizes):
            o_ref[i:i + 1, :] = jnp.sum(p_ref[off:off + sz, :], axis=0, keepdims=True)
            off += sz

    return pl.pallas_call(
        body, name="reduce_rows",
        out_shape=jax.ShapeDtypeStruct((rows_out, D), F32),
        compiler_params=pltpu.CompilerParams(vmem_limit_bytes=VMEM_LIMIT),
    )(parts)


def _sum_devices(g8):
    _, R, D = g8.shape

    def body(g_ref, o_ref):
        tot = g_ref[0]
        for k in range(1, 8):
            tot = tot + g_ref[k]
        o_ref[...] = tot

    return pl.pallas_call(
        body, name="sum_devices",
        out_shape=jax.ShapeDtypeStruct((R, D), F32),
        compiler_params=pltpu.CompilerParams(vmem_limit_bytes=VMEM_LIMIT),
    )(g8)


def _small_update(gathered, w, m, v, L):
    _, R, D = gathered.shape

    def body(g8_ref, w_ref, m_ref, v_ref, g_ref, d_ref, nm_ref, nv_ref):
        tot = g8_ref[0]
        for k in range(1, 8):
            tot = tot + g8_ref[k]
        g_ref[...] = tot
        sm = _softmax_rows(w_ref, L)
        run = sm[0]
        dcum = []
        for l in range(L):
            if l > 0:
                run = run + sm[l]
            cum = run - sm[0]
            dcum.append(jnp.where((cum > 0.0) & (cum < 1.0), g_ref[l:l + 1, :], 0.0))
        dsm = [jnp.zeros((1, D), F32)]
        for i in range(1, L):
            dsm.append(functools.reduce(lambda p, q: p + q, dcum[i:]))
        dot = functools.reduce(lambda p, q: p + q, [s * d for s, d in zip(sm, dsm)])
        for l in range(L):
            g_ref[l:l + 1, :] = sm[l] * (dsm[l] - dot)
        lam = w_ref[5 * L:6 * L, :]
        g_ref[5 * L:6 * L, :] = g_ref[5 * L:6 * L, :] * (-_sig(-lam))
        d, nm, nv = _adam_math(w_ref[...], g_ref[...], m_ref[...], v_ref[...])
        d_ref[...] = d
        nm_ref[...] = nm
        nv_ref[...] = nv

    return pl.pallas_call(
        body, name="small_update",
        out_shape=[jax.ShapeDtypeStruct((R, D), F32)] * 4,
        compiler_params=pltpu.CompilerParams(vmem_limit_bytes=VMEM_LIMIT),
    )(gathered, w, m, v)


def kernel(x, lb_logits, norm_mix, w_in, conv_w, conv_b, w_r, b_r, w_i, b_i, lam, hg_norm, w_out, norm_mlp, w_up, w_down, norm_final, loss_target, m_lb_logits, m_norm_mix, m_w_in, m_conv_w, m_conv_b, m_w_r, m_b_r, m_w_i, m_b_i, m_lam, m_hg_norm, m_w_out, m_norm_mlp, m_w_up, m_w_down, m_norm_final, v_lb_logits, v_norm_mix, v_w_in, v_conv_w, v_conv_b, v_w_r, v_b_r, v_w_i, v_b_i, v_lam, v_hg_norm, v_w_out, v_norm_mlp, v_w_up, v_w_down, v_norm_final):
    B, S, D = x.shape
    L = norm_mix.shape[0]
    nb = D // RG_BLOCK
    Dq = D // N_SHARD
    mx, my, mc = _me()
    shard = 2 * mx + my

    big_w = (w_in, w_out, w_up, w_down, w_r, w_i)
    flat2 = lambda a: a.reshape(-1, a.shape[-1])
    slot = jnp.reshape(shard, (1,)).astype(jnp.int32)
    def place(w):
        b = _cast_place(flat2(w), slot)
        return b.reshape(N_SHARD, L, 2, b.shape[1] // (2 * L), b.shape[2])

    *wbufs, g_r, g_i = _gather_weights([place(w) for w in big_w], LINK_SPLIT, [0] * 6, [1] * 4 + [L] * 2)
    unshard_gate = lambda g: g.reshape(N_SHARD, L, nb, RG_BLOCK // N_SHARD, RG_BLOCK).transpose(1, 2, 0, 3, 4).reshape(
        L, nb, RG_BLOCK, RG_BLOCK)
    w_r_full, w_i_full = unshard_gate(g_r), unshard_gate(g_i)

    R_LB, R_NMIX, R_CB, R_BR, R_BI, R_LAM, R_NMLP, R_GN, R_CW, R_NF, R_LOSS = (
        0, L, 2 * L, 3 * L, 4 * L, 5 * L, 6 * L, 7 * L, 8 * L, 12 * L, 12 * L + 1)
    n_rows = 12 * L + 2
    rows_pad = n_rows + (-n_rows) % SUBLANES

    def place_cols(a):
        return lax.dynamic_update_slice(jnp.zeros((a.shape[0], D), F32), a, (0, shard * Dq))

    def pack_small(lb_, nmix_, cb_, br_, bi_, lam_, nmlp_, gn_, cw_, nf_):
        gn_pad = jnp.pad(gn_, ((0, 0), (0, D - HEAD)))
        rows = [lb_, nmix_, cb_, br_, bi_, lam_, nmlp_, gn_pad, place_cols(cw_.reshape(L * CONV_TAPS, Dq)),
                nf_[None, :], jnp.zeros((rows_pad - n_rows + 1, D), F32)]
        return jnp.concatenate(rows, axis=0)

    w_small = pack_small(lb_logits, norm_mix, conv_b, b_r, b_i, lam, norm_mlp, hg_norm, conv_w, norm_final)
    m_small = pack_small(m_lb_logits, m_norm_mix, m_conv_b, m_b_r, m_b_i, m_lam, m_norm_mlp, m_hg_norm, m_conv_w,
                         m_norm_final)
    v_small = pack_small(v_lb_logits, v_norm_mix, v_conv_b, v_b_r, v_b_i, v_lam, v_norm_mlp, v_hg_norm, v_conv_w,
                         v_norm_final)
    cw_rows = place_cols(conv_w.reshape(L * CONV_TAPS, Dq)) * jnp.where(mc == 0, 1.0, 0.0)
    conv_w_full = _sum_devices(_allgather_small(cw_rows)).reshape(L, CONV_TAPS, D)

    lowb, sp = _prep(lb_logits, lam)

    loss_parts, grad_x, parts, small, g_nf = _local_step(
        x, loss_target, lowb, sp, norm_mix, wbufs, conv_w_full, conv_b, w_r_full, b_r, w_i_full, b_i, hg_norm,
        norm_mlp, norm_final)

    core = jnp.reshape(mc, (1,)).astype(jnp.int32)
    outs = {}
    for a, (name, w, m, v) in enumerate(zip(("w_in", "w_out", "w_up", "w_down", "w_r", "w_i"), big_w,
                                            (m_w_in, m_w_out, m_w_up, m_w_down, m_w_r, m_w_i),
                                            (v_w_in, v_w_out, v_w_up, v_w_down, v_w_r, v_w_i))):
        layer_rows = flat2(w).shape[0] // L
        done, row0 = None, 0
        for mine, sibs in parts:
            done = _adam(flat2(w), mine[a], sibs[a], flat2(m), flat2(v), core, row0, layer_rows, done)
            row0 += 2 * mine[a].shape[0]
        outs[name] = tuple(t.reshape(w.shape) for t in done)

    parts, sizes = [], []

    def add_rows(a):
        a = a.reshape(-1, a.shape[-1])
        if a.shape[1] != D:
            a = jnp.pad(a, ((0, 0), (0, D - a.shape[1])))
        parts.append(a)
        sizes.append(a.shape[0])

    for i in range(8):
        for l in range(L):
            add_rows(small[l][i])
    for l in range(L):
        for j in range(CONV_TAPS):
            add_rows(small[l][8][j])
    add_rows(g_nf)
    loss_rows = loss_parts[:, 0:1, :]
    add_rows(jnp.where(lax.broadcasted_iota(jnp.int32, loss_rows.shape, 2) == 0, loss_rows, 0.0))
    g_small = _reduce_rows(jnp.concatenate(parts, axis=0), sizes, rows_pad)
    g_small, d_small, nm_small, nv_small = _small_update(_allgather_small(g_small), w_small, m_small, v_small, L)

    def unpack(t):
        take_cols = lambda a: lax.dynamic_slice(a, (0, shard * Dq), (a.shape[0], Dq))
        return {"lb_logits": t[R_LB:R_LB + L], "norm_mix": t[R_NMIX:R_NMIX + L], "conv_b": t[R_CB:R_CB + L],
                "b_r": t[R_BR:R_BR + L], "b_i": t[R_BI:R_BI + L], "lam": t[R_LAM:R_LAM + L],
                "norm_mlp": t[R_NMLP:R_NMLP + L], "hg_norm": t[R_GN:R_GN + L, :HEAD],
                "conv_w": take_cols(t[R_CW:R_CW + L * CONV_TAPS]).reshape(L, CONV_TAPS, Dq), "norm_final": t[R_NF]}

    small_out = [unpack(t) for t in (g_small, d_small, nm_small, nv_small)]
    loss = g_small[R_LOSS, 0]
    names = ("lb_logits", "norm_mix", "w_in", "conv_w", "conv_b", "w_r", "b_r", "w_i", "b_i", "lam", "hg_norm",
             "w_out", "norm_mlp", "w_up", "w_down", "norm_final")
    result = [loss, grad_x]
    for kind in range(4):
        for nme in names:
            result.append(outs[nme][kind] if nme in outs else small_out[kind][nme])
    return tuple(result)
```

```python
import functools
import math

import jax
import jax.numpy as jnp
from jax import lax
from jax.experimental import pallas as pl
from jax.experimental.pallas import tpu as pltpu

F32 = jnp.float32
BF16 = jnp.bfloat16

HEAD = 128
RG_BLOCK = 256
CONV_TAPS = 4
RG_C = 8.0
F_MIN = 1e-30
NORM_EPS = 1e-6
N_SEG = 8
N_SHARD = 4
HG_CHUNK = 256
HG_PAIR = 4
RG_TILE = 256
ADAM_LR, ADAM_B1, ADAM_B2, ADAM_EPS, ADAM_WD, ADAM_STEP = 0.001, 0.9, 0.999, 1e-08, 0.01, 10
V7X_VMEM_BYTES = 64 * 1024 * 1024
VMEM_LIMIT = V7X_VMEM_BYTES - 8 * 1024 * 1024
SUBLANES = 8
LINK_SPLIT = "xxyyyy"
GATHER_STEPS = (0.0, 0.6, 0.88, 1.0)
MESH_ID = pl.DeviceIdType.MESH
ANY = pl.BlockSpec(memory_space=pl.ANY)


def _cp(sem):
    return pltpu.CompilerParams(dimension_semantics=sem, vmem_limit_bytes=VMEM_LIMIT)


def _dot(a, b):
    return jnp.dot(a, b, preferred_element_type=F32)


def _dot_nt(a, b):
    return lax.dot_general(a, b, (((1,), (1,)), ((), ())), preferred_element_type=F32)


def _dot_tn(a, b):
    return lax.dot_general(a, b, (((0,), (0,)), ((), ())), preferred_element_type=F32)


def _dot_01(m01, x):
    n = x.shape[1]
    hi = x.astype(BF16)
    r1 = x - hi.astype(F32)
    mid = r1.astype(BF16)
    lo = (r1 - mid.astype(F32)).astype(BF16)
    y = _dot(m01, jnp.concatenate([hi, mid, lo], axis=1))
    return y[:, :n] + y[:, n:2 * n] + y[:, 2 * n:]


def _sig(x):
    return jax.nn.sigmoid(x)


def _rows8(x):
    return x.reshape(x.shape[0] // SUBLANES, SUBLANES, x.shape[1]).sum(axis=0)


def _tile(n, cap):
    if n <= cap:
        return n
    t = cap - cap % 16
    while n % t:
        t -= 16
    return t


_GELU_C = math.sqrt(2.0 / math.pi)


def _gelu_and_grad(x):
    x2 = x * x
    t = jnp.tanh(_GELU_C * (x + 0.044715 * x * x2))
    g = 0.5 * x * (1.0 + t)
    dg = 0.5 * (1.0 + t) + 0.5 * x * (1.0 - t * t) * (_GELU_C * (1.0 + 3.0 * 0.044715 * x2))
    return g, dg


def _rms(x):
    return lax.rsqrt(jnp.mean(x * x, axis=-1, keepdims=True) + NORM_EPS)


def _rms_bwd(dh, x, rs, gain):
    xhat = x * rs
    dxhat = dh * gain
    dx = rs * (dxhat - xhat * jnp.mean(dxhat * xhat, axis=-1, keepdims=True))
    return dx, _rows8(dh * xhat)


def _inproj_fwd(x2, gain, w_st, layer):
    T, D = x2.shape
    tm = _tile(T, 2048)

    def body(x_ref, g_ref, w_ref, o_ref, h_ref):
        @pl.when(pl.program_id(1) == 0)
        def _():
            x = x_ref[...]
            h_ref[...] = (x * _rms(x) * g_ref[...]).astype(BF16)
        o_ref[...] = _dot(h_ref[...], w_ref[...])

    return pl.pallas_call(
        body, name="inproj_fwd",
        grid=(T // tm, N_SEG),
        in_specs=[pl.BlockSpec((tm, D), lambda i, k: (i, 0)),
                  pl.BlockSpec((1, D), lambda i, k: (0, 0)),
                  pl.BlockSpec((None, None, D, D), lambda i, k: (k // 2, layer, 0, k % 2))],
        out_specs=[pl.BlockSpec((tm, D), lambda i, k: (i, k)),
                   pl.BlockSpec((tm, D), lambda i, k: (i, 0))],
        out_shape=[jax.ShapeDtypeStruct((T, N_SEG * D), F32), jax.ShapeDtypeStruct((T, D), BF16)],
        compiler_params=_cp(("parallel", "arbitrary")),
    )(x2, gain, w_st)


def _rg_gates(xc, wr_ref, br, wi_ref, bi, sp):
    D = xc.shape[1]
    xcb = xc.astype(BF16)
    pr, pi = [], []
    for n in range(D // RG_BLOCK):
        blk = xcb[:, n * RG_BLOCK:(n + 1) * RG_BLOCK]
        pr.append(_dot(blk, wr_ref[n]))
        pi.append(_dot(blk, wi_ref[n]))
    r = _sig(jnp.concatenate(pr, axis=1) + br) if len(pr) > 1 else _sig(pr[0] + br)
    i = _sig(jnp.concatenate(pi, axis=1) + bi) if len(pi) > 1 else _sig(pi[0] + bi)
    la = (-RG_C) * r * sp
    a = jnp.exp(la)
    y = 2.0 * la
    one_m_e2 = jnp.where(y > -1e-2, -(y * (1.0 + 0.5 * y * (1.0 + y * (1.0 / 3.0)))), 1.0 - jnp.exp(y))
    mult = jnp.sqrt(jnp.maximum(one_m_e2, 0.0))
    return r, i, a, mult


def _conv_taps(xbuf, cw_ref, ts):
    acc = None
    for j in range(CONV_TAPS):
        term = cw_ref[j:j + 1, :] * xbuf[pl.ds(SUBLANES - (CONV_TAPS - 1) + j, ts), :]
        acc = term if acc is None else acc + term
    return acc


def _rg_fwd(proj, B, cw, cb, wr, br, wi, bi, sp):
    T = proj.shape[0]
    D = proj.shape[1] // N_SEG
    S = T // B
    ts = _tile(S, RG_TILE)
    nts = S // ts
    nb = D // RG_BLOCK

    def body(xa_ref, ga_ref, cw_ref, cb_ref, wr_ref, br_ref, wi_ref, bi_ref, sp_ref,
             ya_ref, h_ref, xbuf, a_scr, u_scr, carry):
        @pl.when(pl.program_id(1) == 0)
        def _():
            xbuf[0:SUBLANES, :] = jnp.zeros((SUBLANES, D), F32)
            carry[...] = jnp.zeros((SUBLANES, D), F32)

        xbuf[pl.ds(SUBLANES, ts), :] = xa_ref[...]
        xc = _conv_taps(xbuf, cw_ref, ts) + cb_ref[...]
        r, i, a, mult = _rg_gates(xc, wr_ref, br_ref[...], wi_ref, bi_ref[...], sp_ref[...])
        a_scr[...] = a
        u_scr[...] = mult * (i * xc)
        row8 = lax.broadcasted_iota(jnp.int32, (SUBLANES, 1), 0)

        def blk(n, hprev):
            off = pl.multiple_of(n * SUBLANES, SUBLANES)
            a8 = a_scr[pl.ds(off, SUBLANES), :]
            u8 = u_scr[pl.ds(off, SUBLANES), :]
            for d in (1, 2, 4):
                m = row8 >= d
                ap = jnp.where(m, pltpu.roll(a8, d, 0), 1.0)
                up = jnp.where(m, pltpu.roll(u8, d, 0), 0.0)
                u8 = a8 * up + u8
                a8 = a8 * ap
            h8 = u8 + a8 * hprev
            u_scr[pl.ds(off, SUBLANES), :] = h8
            last = jnp.sum(jnp.where(row8 == SUBLANES - 1, h8, 0.0), axis=0, keepdims=True)
            return jnp.broadcast_to(last, (SUBLANES, D))

        carry[...] = lax.fori_loop(0, ts // SUBLANES, blk, carry[...])
        h = u_scr[...]
        h_ref[...] = h
        g, _ = _gelu_and_grad(ga_ref[...])
        ya_ref[...] = h * g
        xbuf[0:SUBLANES, :] = xa_ref[pl.ds(ts - SUBLANES, SUBLANES), :]

    vec = pl.BlockSpec((1, D), lambda b, j: (0, 0))
    gate = pl.BlockSpec((nb, RG_BLOCK, RG_BLOCK), lambda b, j: (0, 0, 0))
    return pl.pallas_call(
        body, name="rg_fwd",
        grid=(B, nts),
        in_specs=[pl.BlockSpec((ts, D), lambda b, j: (b * nts + j, 0)),
                  pl.BlockSpec((ts, D), lambda b, j: (b * nts + j, 1)),
                  pl.BlockSpec((CONV_TAPS, D), lambda b, j: (0, 0)), vec, gate, vec, gate, vec, vec],
        out_specs=[pl.BlockSpec((ts, D), lambda b, j: (b * nts + j, 0))] * 2,
        out_shape=[jax.ShapeDtypeStruct((T, D), F32)] * 2,
        scratch_shapes=[pltpu.VMEM((SUBLANES + ts, D), F32), pltpu.VMEM((ts, D), F32),
                        pltpu.VMEM((ts, D), F32), pltpu.VMEM((SUBLANES, D), F32)],
        compiler_params=_cp(("arbitrary", "arbitrary")),
    )(proj, proj, cw, cb, wr, br, wi, bi, sp)


def _hg_gates(q, z, lb):
    sig = _sig(z)
    one_m = 1.0 - lb
    fg = lb + one_m * sig
    lf = jnp.log(jnp.maximum(fg, F_MIN))
    kf = one_m * (1.0 - sig)
    qs = _sig(q)
    return q * qs, qs, kf, lf, fg, sig


def _hg_cum(lf, C):
    ri = lax.broadcasted_iota(jnp.int32, (C, C), 0)
    ci = lax.broadcasted_iota(jnp.int32, (C, C), 1)
    return _dot_01(jnp.where(ci <= ri, 1.0, 0.0).astype(BF16), lf)


def _hg_levels(lf, cum, C):
    row = lax.broadcasted_iota(jnp.int32, (C, 1), 0)
    levels = []
    w = C // 2
    while w >= 4:
        blk = 2 * w
        upper = (row & w) != 0
        ref = jnp.min(jnp.where(upper, 0.0, cum).reshape(C // blk, blk, HEAD), axis=1, keepdims=True)
        ref = jnp.broadcast_to(ref, (C // blk, blk, HEAD)).reshape(C, HEAD)
        d = cum - ref
        levels.append(jnp.exp(jnp.where(upper, d, -d)))
        w //= 2
    r4 = row & 3
    lf_prev = pltpu.roll(lf, 1, 0)
    lf_next = pltpu.roll(lf, C - 1, 0)
    levels.append(jnp.exp(jnp.where(r4 == 3, lf + lf_prev, jnp.where(r4 == 2, lf, jnp.where(r4 == 0, lf_next, 0.0)))))
    levels.append(jnp.exp(jnp.where((row & 1) == 1, lf, 0.0)))
    levels.append(None)
    return levels


def _hg_level_blocks(C):
    blks = []
    w = C // 2
    while w >= 4:
        blks.append(2 * w)
        w //= 2
    return blks + [4, 2, 1]


def _hg_fill_masks(mask_scr, C):
    ri = lax.broadcasted_iota(jnp.int32, (C, C), 0)
    ci = lax.broadcasted_iota(jnp.int32, (C, C), 1)
    for i, blk in enumerate(_hg_level_blocks(C)):
        if blk == 1:
            keep = ri == ci
        else:
            shift, w = blk.bit_length() - 1, blk // 2
            keep = ((ri >> shift) == (ci >> shift)) & ((ri & w) != 0) & ((ci & w) == 0)
        mask_scr[i] = jnp.where(keep, 1.0, 0.0).astype(F32)


def _hg_operands(qf, kf, e):
    if e is None:
        return qf.astype(BF16), kf.astype(BF16)
    return (qf * e).astype(BF16), (kf * e).astype(BF16)


def _hg_scores(qf, kf, levels, mask_scr):
    A = None
    for n, e in enumerate(levels):
        a = _dot_nt(*_hg_operands(qf, kf, e)) * mask_scr[n]
        A = a if A is None else A + a
    return A


def _hg_specs(B, NC, D, C, dtype_blocks):
    def spec(col0, rev):
        c0 = col0 // HG_PAIR
        if rev:
            return pl.BlockSpec((C, HG_PAIR * HEAD), lambda b, h, j: (b * NC + (NC - 1 - j), c0 + h))
        return pl.BlockSpec((C, HG_PAIR * HEAD), lambda b, h, j: (b * NC + j, c0 + h))
    return spec


def _head(ref, hh):
    return ref.at[:, pl.ds(hh * HEAD, HEAD)]


class _Ride:
    def __init__(self, reads, bufs, outs, phases, fractions, sems):
        self.reads, self.bufs, self.outs = list(reads), list(bufs), list(outs)
        self.phases, self.fractions, self.sems = list(phases), list(fractions), list(sems)


def _ride_call(body, ride, *, name, grid, in_specs, out_specs, out_shape, scratch_shapes, semantics, operands):
    if ride is None:
        return pl.pallas_call(body, name=name, grid=grid, in_specs=in_specs, out_specs=out_specs, out_shape=out_shape,
                              scratch_shapes=scratch_shapes, compiler_params=_cp(semantics))(*operands)
    n_in, n_out, n_scr = len(in_specs), len(out_specs), len(scratch_shapes)
    nr, nb, no = len(ride.reads), len(ride.bufs), len(ride.outs)
    last_step = math.prod(grid) - 1

    def full_body(*refs):
        own = refs[:n_in] + refs[n_in + nr + nb:n_in + nr + nb + n_out]
        tail = refs[n_in + nr + nb + n_out:]
        ride_refs = (refs[n_in:n_in + nr], tail[:nb], tail[nb:nb + no])
        scr = tail[nb + no:]
        step = pl.program_id(0)
        for d in range(1, len(grid)):
            step = step * grid[d] + pl.program_id(d)
        for phase, frac in zip(ride.phases, ride.fractions):
            @pl.when(step == int(round(frac * last_step)))
            def _(phase=phase):
                phase(*ride_refs, *scr[n_scr:])
        body(*own, *scr[:n_scr])

    return pl.pallas_call(
        full_body, name=name, grid=grid,
        in_specs=list(in_specs) + [ANY] * (nr + nb),
        out_specs=list(out_specs) + [ANY] * (nb + no),
        out_shape=list(out_shape) + [jax.ShapeDtypeStruct(b.shape, b.dtype) for b in ride.bufs] + ride.outs,
        input_output_aliases={n_in + nr + j: n_out + j for j in range(nb)},
        scratch_shapes=list(scratch_shapes) + ride.sems,
        compiler_params=_cp(("arbitrary",) * len(grid)),
    )(*operands, *ride.reads, *ride.bufs)


def _hg_fwd(proj, B, lb, gn, ride=None):
    T = proj.shape[0]
    D = proj.shape[1] // N_SEG
    S = T // B
    C = min(HG_CHUNK, S)
    NC = S // C
    H = D // HEAD
    hpd = D // HEAD
    spec = _hg_specs(B, NC, D, C, None)

    def body(q_ref, z_ref, v_ref, g_ref, lb_ref, gn_ref, yb_ref, o_ref, st_ref, a_ref, cum_ref, st_scr, mask_scr):
        @pl.when(pl.program_id(2) == 0)
        def _():
            st_scr[...] = jnp.zeros((HG_PAIR, HEAD, HEAD), F32)
            _hg_fill_masks(mask_scr, C)

        for hh in range(HG_PAIR):
            s_t = st_scr[hh]
            st_ref[hh] = s_t
            qf, _, kf, lf, _, _ = _hg_gates(_head(q_ref, hh)[...], _head(z_ref, hh)[...], _head(lb_ref, hh)[...])
            cum = _hg_cum(lf, C)
            _head(cum_ref, hh)[...] = cum
            A = _hg_scores(qf, kf, _hg_levels(lf, cum, C), mask_scr).astype(BF16)
            a_ref[hh] = A
            vb = _head(v_ref, hh)[...].astype(BF16)
            o = _dot_nt((qf * jnp.exp(cum)).astype(BF16), s_t.astype(BF16)) + _dot(A, vb)
            last = jnp.sum(lf, axis=0, keepdims=True)
            kend = kf * jnp.exp(last - cum)
            st_scr[hh] = jnp.exp(last) * s_t + _dot_tn(vb, kend.astype(BF16))
            _head(o_ref, hh)[...] = o
            g = _head(g_ref, hh)[...]
            _head(yb_ref, hh)[...] = (o * _rms(o) * gn_ref[...]) * (g * _sig(g))

    assert H % HG_PAIR == 0, (H, HG_PAIR)
    HG = H // HG_PAIR
    return _ride_call(
        body, ride, name="hg_fwd",
        grid=(B, HG, NC),
        in_specs=[spec(2 * hpd, False), spec(3 * hpd, False), spec(4 * hpd, False), spec(5 * hpd, False),
                  pl.BlockSpec((1, HG_PAIR * HEAD), lambda b, h, j: (0, h)),
                  pl.BlockSpec((1, HEAD), lambda b, h, j: (0, 0))],
        out_specs=[spec(0, False), spec(0, False),
                   pl.BlockSpec((None, HG_PAIR, None, HEAD, HEAD), lambda b, h, j: (b, h, j, 0, 0)),
                   pl.BlockSpec((None, HG_PAIR, None, C, C), lambda b, h, j: (b, h, j, 0, 0)), spec(0, False)],
        out_shape=[jax.ShapeDtypeStruct((T, D), F32), jax.ShapeDtypeStruct((T, D), F32),
                   jax.ShapeDtypeStruct((B, H, NC, HEAD, HEAD), F32),
                   jax.ShapeDtypeStruct((B, H, NC, C, C), BF16), jax.ShapeDtypeStruct((T, D), F32)],
        scratch_shapes=[pltpu.VMEM((HG_PAIR, HEAD, HEAD), F32), pltpu.VMEM((len(_hg_level_blocks(C)), C, C), F32)],
        semantics=("parallel", "parallel", "arbitrary"),
        operands=(proj, proj, proj, proj, lb, gn))


def _w_full(ref):
    s, r, c = ref.shape
    return ref[...].reshape(s * r, c)


def _out_fwd(ya, yb, proj, x2, w_st, layer):
    T, D = x2.shape
    tm = _tile(T, 512)

    def body(ya_ref, yb_ref, ma_ref, mb_ref, x_ref, w_ref, xm_ref, y_ref):
        y = (_sig(ma_ref[...]) * ya_ref[...] + _sig(mb_ref[...]) * yb_ref[...]).astype(BF16)
        y_ref[...] = y
        xm_ref[...] = x_ref[...] + _dot(y, _w_full(w_ref))

    row = pl.BlockSpec((tm, D), lambda i: (i, 0))
    return pl.pallas_call(
        body, name="out_fwd",
        grid=(T // tm,),
        in_specs=[row, row, pl.BlockSpec((tm, D), lambda i: (i, 6)), pl.BlockSpec((tm, D), lambda i: (i, 7)), row,
                  pl.BlockSpec((N_SHARD, None, D // N_SHARD, D), lambda i: (0, layer, 0, 0))],
        out_specs=[row, row],
        out_shape=[jax.ShapeDtypeStruct((T, D), F32), jax.ShapeDtypeStruct((T, D), BF16)],
        compiler_params=_cp(("parallel",)),
    )(ya, yb, proj, proj, x2, w_st)


def _mlp_fwd(xm, gain, wup_st, wdn_st, layer):
    T, D = xm.shape
    F4 = wup_st.shape[3]
    tm = _tile(T, 1024)

    def body(x_ref, g_ref, wu_ref, wd_ref, xo_ref, up_ref, h_ref):
        @pl.when(pl.program_id(1) == 0)
        def _():
            x = x_ref[...]
            h_ref[...] = (x * _rms(x) * g_ref[...]).astype(BF16)
            xo_ref[...] = x
        up = _dot(h_ref[...], wu_ref[...])
        up_ref[...] = up.astype(BF16)
        act = jnp.maximum(up, 0.0)
        xo_ref[...] += _dot((act * act).astype(BF16), wd_ref[...])

    row = pl.BlockSpec((tm, D), lambda i, s: (i, 0))
    return pl.pallas_call(
        body, name="mlp_fwd",
        grid=(T // tm, N_SHARD),
        in_specs=[row, pl.BlockSpec((1, D), lambda i, s: (0, 0)),
                  pl.BlockSpec((None, None, D, F4), lambda i, s: (s, layer, 0, 0)),
                  pl.BlockSpec((None, None, F4, D), lambda i, s: (s, layer, 0, 0))],
        out_specs=[row, pl.BlockSpec((tm, F4), lambda i, s: (i, s)), row],
        out_shape=[jax.ShapeDtypeStruct((T, D), F32), jax.ShapeDtypeStruct((T, N_SHARD * F4), BF16),
                   jax.ShapeDtypeStruct((T, D), BF16)],
        compiler_params=_cp(("parallel", "arbitrary")),
    )(xm, gain, wup_st, wdn_st)


def _final_loss(x2, gain, tgt):
    T, D = x2.shape
    tm = _tile(T, 512)
    nt = T // tm

    def body(x_ref, g_ref, t_ref, loss_ref, dx_ref, dg_ref):
        x = x_ref[...]
        rs = _rms(x)
        err = x * rs * g_ref[...] - t_ref[...]
        part = 0.5 * jnp.sum(jnp.sum(err * err, axis=-1, keepdims=True) * (1.0 / D), axis=0, keepdims=True)
        loss_ref[...] = jnp.broadcast_to(part, (SUBLANES, 128))
        dx, dg = _rms_bwd(err * (1.0 / D), x, rs, g_ref[...])
        dx_ref[...] = dx
        dg_ref[...] = dg

    row = pl.BlockSpec((tm, D), lambda i: (i, 0))
    return pl.pallas_call(
        body, name="final_loss",
        grid=(nt,),
        in_specs=[row, pl.BlockSpec((1, D), lambda i: (0, 0)), row],
        out_specs=[pl.BlockSpec((None, SUBLANES, 128), lambda i: (i, 0, 0)), row,
                   pl.BlockSpec((None, SUBLANES, D), lambda i: (i, 0, 0))],
        out_shape=[jax.ShapeDtypeStruct((nt, SUBLANES, 128), F32), jax.ShapeDtypeStruct((T, D), F32),
                   jax.ShapeDtypeStruct((nt, SUBLANES, D), F32)],
        compiler_params=_cp(("parallel",)),
    )(x2, gain, tgt)


def _mlp_bwd_x(dx, xm, up, gain, wup_st, wdn_st, layer, ride=None):
    T, D = xm.shape
    F4 = wup_st.shape[3]
    tm = _tile(T, 1024)
    nt = T // tm

    def body(dx_ref, x_ref, up_ref, g_ref, wu_ref, wd_ref, dxm_ref, dup_ref, dg_ref, dxb):
        s = pl.program_id(1)

        @pl.when(s == 0)
        def _():
            dxb[...] = dx_ref[...].astype(BF16)
            dxm_ref[...] = jnp.zeros((tm, D), F32)

        d_act = _dot_nt(dxb[...], wd_ref[...])
        d_up = (d_act * (2.0 * jnp.maximum(up_ref[...].astype(F32), 0.0))).astype(BF16)
        dup_ref[...] = d_up
        dxm_ref[...] += _dot_nt(d_up, wu_ref[...])

        @pl.when(s == N_SHARD - 1)
        def _():
            x = x_ref[...]
            dxn, dg = _rms_bwd(dxm_ref[...], x, _rms(x), g_ref[...])
            dxm_ref[...] = dx_ref[...] + dxn
            dg_ref[...] = dg

    row = pl.BlockSpec((tm, D), lambda i, s: (i, 0))
    return _ride_call(
        body, ride, name="mlp_bwd_x",
        grid=(nt, N_SHARD),
        in_specs=[row, row, pl.BlockSpec((tm, F4), lambda i, s: (i, s)), pl.BlockSpec((1, D), lambda i, s: (0, 0)),
                  pl.BlockSpec((None, None, D, F4), lambda i, s: (s, layer, 0, 0)),
                  pl.BlockSpec((None, None, F4, D), lambda i, s: (s, layer, 0, 0))],
        out_specs=[row, pl.BlockSpec((tm, F4), lambda i, s: (i, s)),
                   pl.BlockSpec((None, SUBLANES, D), lambda i, s: (i, 0, 0)), row],
        out_shape=[jax.ShapeDtypeStruct((T, D), F32), jax.ShapeDtypeStruct((T, N_SHARD * F4), BF16),
                   jax.ShapeDtypeStruct((nt, SUBLANES, D), F32), jax.ShapeDtypeStruct((T, D), BF16)],
        scratch_shapes=[],
        semantics=("parallel", "arbitrary"),
        operands=(dx, xm, up, gain, wup_st, wdn_st))


def _layer_slot(bufs, shapes, n_layers):
    out_shape = [jax.ShapeDtypeStruct((N_SHARD, n_layers) + s, F32) for s in shapes]
    return out_shape, ([] if bufs is None else list(bufs))


def _mlp_bwd_w(up, dxb, h, dup, layer, n_layers, bufs):
    T, D = dxb.shape
    F4 = up.shape[1] // N_SHARD
    tk = _tile(T, 1024)
    out_shape, extra = _layer_slot(bufs, [(D, F4), (F4, D)], n_layers)

    def body(up_ref, dx_ref, h_ref, dup_ref, *rest):
        gu_ref, gd_ref = rest[-2:]

        @pl.when(pl.program_id(1) == 0)
        def _():
            gu_ref[...] = jnp.zeros((D, F4), F32)
            gd_ref[...] = jnp.zeros((F4, D), F32)
        act = jnp.maximum(up_ref[...], 0.0)
        gd_ref[...] += _dot_tn(act * act, dx_ref[...])
        gu_ref[...] += _dot_tn(h_ref[...], dup_ref[...])

    return pl.pallas_call(
        body, name="mlp_bwd_w",
        grid=(N_SHARD, T // tk),
        in_specs=[pl.BlockSpec((tk, F4), lambda s, t: (t, s)), pl.BlockSpec((tk, D), lambda s, t: (t, 0)),
                  pl.BlockSpec((tk, D), lambda s, t: (t, 0)), pl.BlockSpec((tk, F4), lambda s, t: (t, s))]
        + [ANY] * len(extra),
        out_specs=[pl.BlockSpec((None, None, D, F4), lambda s, t: (s, layer, 0, 0)),
                   pl.BlockSpec((None, None, F4, D), lambda s, t: (s, layer, 0, 0))],
        out_shape=out_shape,
        input_output_aliases={4 + i: i for i in range(len(extra))},
        compiler_params=_cp(("parallel", "arbitrary")),
    )(up, dxb, h, dup, *extra)


def _out_bwd_x(dxm, ya, yb, proj, w_st, layer):
    T, D = dxm.shape
    tm = _tile(T, 512)

    def body(dx_ref, ya_ref, yb_ref, ma_ref, mb_ref, w_ref, dya_ref, dyb_ref, dma_ref, dmb_ref):
        dy = _dot_nt(dx_ref[...].astype(BF16), _w_full(w_ref))
        sa = _sig(ma_ref[...])
        sb = _sig(mb_ref[...])
        dya_ref[...] = dy * sa
        dyb_ref[...] = dy * sb
        dma_ref[...] = (dy * ya_ref[...] * (sa * (1.0 - sa))).astype(BF16)
        dmb_ref[...] = (dy * yb_ref[...] * (sb * (1.0 - sb))).astype(BF16)

    row = pl.BlockSpec((tm, D), lambda i: (i, 0))
    return pl.pallas_call(
        body, name="out_bwd_x",
        grid=(T // tm,),
        in_specs=[row, row, row, pl.BlockSpec((tm, D), lambda i: (i, 6)), pl.BlockSpec((tm, D), lambda i: (i, 7)),
                  pl.BlockSpec((N_SHARD, None, D // N_SHARD, D), lambda i: (0, layer, 0, 0))],
        out_specs=[row] * 4,
        out_shape=[jax.ShapeDtypeStruct((T, D), F32)] * 2 + [jax.ShapeDtypeStruct((T, D), BF16)] * 2,
        compiler_params=_cp(("parallel",)),
    )(dxm, ya, yb, proj, proj, w_st)


def _out_bwd_w(ymix, dxm, layer, n_layers, bufs):
    T, D = dxm.shape
    tk = _tile(T, 1024)
    out_shape, extra = _layer_slot(bufs, [(D // N_SHARD, D)], n_layers)

    def body(y_ref, dx_ref, *rest):
        g_ref = rest[-1]

        @pl.when(pl.program_id(0) == 0)
        def _():
            g_ref[...] = jnp.zeros((N_SHARD, D // N_SHARD, D), F32)
        g = _dot_tn(y_ref[...], dx_ref[...].astype(BF16))
        g_ref[...] += g.reshape(N_SHARD, D // N_SHARD, D)

    row = pl.BlockSpec((tk, D), lambda t: (t, 0))
    return pl.pallas_call(
        body, name="out_bwd_w",
        grid=(T // tk,),
        in_specs=[row, row] + [ANY] * len(extra),
        out_specs=[pl.BlockSpec((N_SHARD, None, D // N_SHARD, D), lambda t: (0, layer, 0, 0))],
        out_shape=out_shape,
        input_output_aliases={2 + i: i for i in range(len(extra))},
        compiler_params=_cp(("arbitrary",)),
    )(ymix, dxm, *extra)[0]


def _rg_bwd(proj, hrg, dya, B, cw, cb, wr, br, wi, bi, sp):
    T = proj.shape[0]
    D = proj.shape[1] // N_SEG
    S = T // B
    ts = _tile(S, RG_TILE)
    nts = S // ts
    nb = D // RG_BLOCK
    t8 = ts // SUBLANES

    def body(xa_ref, xp_ref, ga_ref, h_ref, hp_ref, dya_ref, cw_ref, cb_ref, wr_ref, br_ref, wi_ref, bi_ref, sp_ref,
             dxa_ref, dga_ref, gwr_ref, gwi_ref, gcw_ref, gcb_ref, gbr_ref, gbi_ref, gsp_ref,
             xbuf, hbuf, abuf, dbuf, g_scr, c_scr, gcar):
        b = pl.program_id(0)
        j = pl.program_id(1)
        first_in_time = j == nts - 1

        @pl.when((b == 0) & (j == 0))
        def _():
            gwr_ref[...] = jnp.zeros((nb, RG_BLOCK, RG_BLOCK), F32)
            gwi_ref[...] = jnp.zeros((nb, RG_BLOCK, RG_BLOCK), F32)
            gcw_ref[...] = jnp.zeros((CONV_TAPS, SUBLANES, D), F32)
            for r in (gcb_ref, gbr_ref, gbi_ref, gsp_ref):
                r[...] = jnp.zeros((SUBLANES, D), F32)

        @pl.when(j == 0)
        def _():
            abuf[pl.ds(ts, SUBLANES), :] = jnp.zeros((SUBLANES, D), F32)
            dbuf[pl.ds(ts, SUBLANES), :] = jnp.zeros((SUBLANES, D), F32)
            gcar[...] = jnp.zeros((SUBLANES, D), F32)

        keep = jnp.where(first_in_time, 0.0, 1.0)
        xbuf[0:SUBLANES, :] = xp_ref[...] * keep
        xbuf[pl.ds(SUBLANES, ts), :] = xa_ref[...]
        hbuf[0:SUBLANES, :] = hp_ref[...] * keep
        hbuf[pl.ds(SUBLANES, ts), :] = h_ref[...]

        xc = _conv_taps(xbuf, cw_ref, ts) + cb_ref[...]
        sp = sp_ref[...]
        r, i, a, mult = _rg_gates(xc, wr_ref, br_ref[...], wi_ref, bi_ref[...], sp)
        g_gate, dg_gate = _gelu_and_grad(ga_ref[...])
        dya = dya_ref[...]
        dga_ref[...] = (dya * h_ref[...] * dg_gate).astype(BF16)

        abuf[0:ts, :] = a
        c_scr[...] = abuf[pl.ds(1, ts), :]
        g_scr[...] = dya * g_gate
        row8 = lax.broadcasted_iota(jnp.int32, (SUBLANES, 1), 0)

        def blk(n, gnext):
            off = pl.multiple_of((t8 - 1 - n) * SUBLANES, SUBLANES)
            c8 = c_scr[pl.ds(off, SUBLANES), :]
            d8 = g_scr[pl.ds(off, SUBLANES), :]
            for d in (1, 2, 4):
                m = row8 < SUBLANES - d
                cn = jnp.where(m, pltpu.roll(c8, SUBLANES - d, 0), 1.0)
                dn = jnp.where(m, pltpu.roll(d8, SUBLANES - d, 0), 0.0)
                d8 = d8 + c8 * dn
                c8 = c8 * cn
            g8 = d8 + c8 * gnext
            g_scr[pl.ds(off, SUBLANES), :] = g8
            first = jnp.sum(jnp.where(row8 == 0, g8, 0.0), axis=0, keepdims=True)
            return jnp.broadcast_to(first, (SUBLANES, D))

        gcar[...] = lax.fori_loop(0, t8, blk, gcar[...])
        abuf[pl.ds(ts, SUBLANES), :] = a[0:SUBLANES, :]

        g = g_scr[...]
        hprev = hbuf[pl.ds(SUBLANES - 1, ts), :]
        gx = i * xc
        e2 = a * a
        dla = g * hprev * a - jnp.where(mult > 0.0, g * gx * e2 / jnp.where(mult > 0.0, mult, 1.0), 0.0)
        dgx = g * mult
        dpr = (dla * ((-RG_C) * sp)) * (r * (1.0 - r))
        dpi = (dgx * xc) * (i * (1.0 - i))
        gsp_ref[...] += _rows8(dla * ((-RG_C) * r))
        gbr_ref[...] += _rows8(dpr)
        gbi_ref[...] += _rows8(dpi)
        dprb = dpr.astype(BF16)
        dpib = dpi.astype(BF16)
        xcb = xc.astype(BF16)
        back = []
        for n in range(nb):
            sl = slice(n * RG_BLOCK, (n + 1) * RG_BLOCK)
            back.append(_dot_nt(dprb[:, sl], wr_ref[n]) + _dot_nt(dpib[:, sl], wi_ref[n]))
            gwr_ref[n] += _dot_tn(xcb[:, sl], dprb[:, sl])
            gwi_ref[n] += _dot_tn(xcb[:, sl], dpib[:, sl])
        dxc = dgx * i + (jnp.concatenate(back, axis=1) if nb > 1 else back[0])
        gcb_ref[...] += _rows8(dxc)

        dbuf[0:ts, :] = dxc
        dxa = None
        for jtap in range(CONV_TAPS):
            term = cw_ref[jtap:jtap + 1, :] * dbuf[pl.ds(CONV_TAPS - 1 - jtap, ts), :]
            dxa = term if dxa is None else dxa + term
            gcw_ref[jtap] += _rows8(dxc * xbuf[pl.ds(SUBLANES - (CONV_TAPS - 1) + jtap, ts), :])
        dxa_ref[...] = dxa.astype(BF16)
        dbuf[pl.ds(ts, SUBLANES), :] = dxc[0:SUBLANES, :]

    def tile_map(col):
        return lambda b, j: (b * nts + (nts - 1 - j), col)

    def prev8_map(col):
        return lambda b, j: (jnp.maximum((b * nts + (nts - 1 - j)) * t8 - 1, 0), col)

    vec = pl.BlockSpec((1, D), lambda b, j: (0, 0))
    gate = pl.BlockSpec((nb, RG_BLOCK, RG_BLOCK), lambda b, j: (0, 0, 0))
    acc8 = pl.BlockSpec((SUBLANES, D), lambda b, j: (0, 0))
    return pl.pallas_call(
        body, name="rg_bwd",
        grid=(B, nts),
        in_specs=[pl.BlockSpec((ts, D), tile_map(0)), pl.BlockSpec((SUBLANES, D), prev8_map(0)),
                  pl.BlockSpec((ts, D), tile_map(1)),
                  pl.BlockSpec((ts, D), tile_map(0)), pl.BlockSpec((SUBLANES, D), prev8_map(0)),
                  pl.BlockSpec((ts, D), tile_map(0)),
                  pl.BlockSpec((CONV_TAPS, D), lambda b, j: (0, 0)), vec, gate, vec, gate, vec, vec],
        out_specs=[pl.BlockSpec((ts, D), tile_map(0)), pl.BlockSpec((ts, D), tile_map(0)), gate, gate,
                   pl.BlockSpec((CONV_TAPS, SUBLANES, D), lambda b, j: (0, 0, 0)), acc8, acc8, acc8, acc8],
        out_shape=[jax.ShapeDtypeStruct((T, D), BF16)] * 2
        + [jax.ShapeDtypeStruct((nb, RG_BLOCK, RG_BLOCK), F32)] * 2
        + [jax.ShapeDtypeStruct((CONV_TAPS, SUBLANES, D), F32)] + [jax.ShapeDtypeStruct((SUBLANES, D), F32)] * 4,
        scratch_shapes=[pltpu.VMEM((SUBLANES + ts, D), F32), pltpu.VMEM((SUBLANES + ts, D), F32),
                        pltpu.VMEM((ts + SUBLANES, D), F32), pltpu.VMEM((ts + SUBLANES, D), F32),
                        pltpu.VMEM((ts, D), F32), pltpu.VMEM((ts, D), F32), pltpu.VMEM((SUBLANES, D), F32)],
        compiler_params=_cp(("arbitrary", "arbitrary")),
    )(proj, proj, proj, hrg, hrg, dya, cw, cb, wr, br, wi, bi, sp)


def _hg_bwd(proj, o_sv, dyb, states, a_sv, cum_sv, B, lb, gn, ride=None):
    T = proj.shape[0]
    D = proj.shape[1] // N_SEG
    S = T // B
    C = min(HG_CHUNK, S)
    NC = S // C
    H = D // HEAD
    hpd = D // HEAD
    spec = _hg_specs(B, NC, D, C, None)

    def body(q_ref, z_ref, v_ref, g_ref, o_ref, dyb_ref, st_ref, a_ref, cum_ref, lb_ref, gn_ref,
             dq_ref, dz_ref, dv_ref, dg_ref, glb_ref, ggn_ref, ds_scr, mask_scr):
        @pl.when(pl.program_id(2) == 0)
        def _():
            ds_scr[...] = jnp.zeros((HG_PAIR, HEAD, HEAD), F32)
            _hg_fill_masks(mask_scr, C)
            glb_ref[...] = jnp.zeros((SUBLANES, HG_PAIR * HEAD), F32)
            ggn_ref[...] = jnp.zeros((HG_PAIR, SUBLANES, HEAD), F32)

        for hh in range(HG_PAIR):
            cols = [_head(r, hh) for r in (q_ref, z_ref, v_ref, g_ref, o_ref, dyb_ref)]
            outs = [_head(r, hh) for r in (dq_ref, dz_ref, dv_ref, dg_ref, glb_ref)]
            one_head(*cols, st_ref.at[hh], a_ref.at[hh], _head(cum_ref, hh), _head(lb_ref, hh), gn_ref,
                     *outs, ggn_ref.at[hh], ds_scr.at[hh], mask_scr)

    def one_head(q_ref, z_ref, v_ref, g_ref, o_ref, dyb_ref, st_ref, a_ref, cum_ref, lb_ref, gn_ref,
                 dq_ref, dz_ref, dv_ref, dg_ref, glb_ref, ggn_ref, ds_scr, mask_scr):
        q = q_ref[...]
        lb = lb_ref[...]
        gn = gn_ref[...]
        qf, qs, kf, lf, fg, sig = _hg_gates(q, z_ref[...], lb)
        cum = cum_ref[...]
        levels = _hg_levels(lf, cum, C)

        o = o_ref[...]
        g = g_ref[...]
        gs = _sig(g)
        rs = _rms(o)
        dyb = dyb_ref[...]
        don = dyb * (g * gs)
        dg_ref[...] = (dyb * (o * rs * gn) * (gs * (1.0 + g * (1.0 - gs)))).astype(BF16)
        ggn_ref[...] += _rows8(don * o * rs)
        dn = don * gn
        do = rs * (dn - o * (rs * rs) * jnp.mean(dn * o, axis=-1, keepdims=True))

        s_t = st_ref[...].astype(BF16)
        ds_t = ds_scr[...]
        ds_b = ds_t.astype(BF16)
        dob = do.astype(BF16)
        vb = v_ref[...].astype(BF16)
        ecum = jnp.exp(cum)
        last = jnp.sum(lf, axis=0, keepdims=True)
        eend = jnp.exp(last - cum)
        qhat = (qf * ecum).astype(BF16)
        kend = (kf * eend).astype(BF16)

        dA = _dot_nt(dob, vb)
        dq_inter = _dot(dob, s_t)
        dk_state = _dot(vb, ds_b)
        dqf = dq_inter * ecum
        dkf = dk_state * eend
        g_intra = None
        for n, e in enumerate(levels):
            qw, kw = _hg_operands(qf, kf, e)
            dam = (dA * mask_scr[n]).astype(BF16)
            rq = _dot(dam, kw)
            rk = _dot_tn(dam, qw)
            dqf += rq if e is None else rq * e
            dkf += rk if e is None else rk * e
            gi = qw.astype(F32) * rq - kw.astype(F32) * rk
            g_intra = gi if g_intra is None else g_intra + gi
        dv_ref[...] = (_dot_tn(a_ref[...], dob) + _dot_nt(kend, ds_b)).astype(BF16)
        e_last = jnp.exp(last)
        ds_scr[...] = e_last * ds_t + _dot_tn(dob, qhat)

        ri = lax.broadcasted_iota(jnp.int32, (C, C), 0)
        ci = lax.broadcasted_iota(jnp.int32, (C, C), 1)
        y_state = kend.astype(F32) * dk_state
        dlf = (_dot_01(jnp.where(ci >= ri, 1.0, 0.0).astype(BF16), g_intra + qhat.astype(F32) * dq_inter - y_state)
               + jnp.sum(y_state, axis=0, keepdims=True)
               + jnp.sum(e_last * st_ref[...] * ds_t, axis=0, keepdims=True))
        dfg = jnp.where(fg > F_MIN, dlf / jnp.maximum(fg, F_MIN), 0.0)
        sneg = 1.0 - sig
        diff = dfg - dkf
        dz_ref[...] = ((1.0 - lb) * sig * sneg * diff).astype(BF16)
        glb_ref[...] += _rows8(sneg * diff)
        dq_ref[...] = (dqf * (qs * (1.0 + q * (1.0 - qs)))).astype(BF16)

    return _ride_call(
        body, ride, name="hg_bwd",
        grid=(B, H // HG_PAIR, NC),
        in_specs=[spec(2 * hpd, True), spec(3 * hpd, True), spec(4 * hpd, True), spec(5 * hpd, True),
                  spec(0, True), spec(0, True),
                  pl.BlockSpec((None, HG_PAIR, None, HEAD, HEAD), lambda b, h, j: (b, h, NC - 1 - j, 0, 0)),
                  pl.BlockSpec((None, HG_PAIR, None, C, C), lambda b, h, j: (b, h, NC - 1 - j, 0, 0)), spec(0, True),
                  pl.BlockSpec((1, HG_PAIR * HEAD), lambda b, h, j: (0, h)),
                  pl.BlockSpec((1, HEAD), lambda b, h, j: (0, 0))],
        out_specs=[spec(0, True)] * 4
        + [pl.BlockSpec((None, SUBLANES, HG_PAIR * HEAD), lambda b, h, j: (b, 0, h)),
           pl.BlockSpec((None, HG_PAIR, SUBLANES, HEAD), lambda b, h, j: (b, h, 0, 0))],
        out_shape=[jax.ShapeDtypeStruct((T, D), BF16)] * 4
        + [jax.ShapeDtypeStruct((B, SUBLANES, D), F32), jax.ShapeDtypeStruct((B, H, SUBLANES, HEAD), F32)],
        scratch_shapes=[pltpu.VMEM((HG_PAIR, HEAD, HEAD), F32), pltpu.VMEM((len(_hg_level_blocks(C)), C, C), F32)],
        semantics=("parallel", "parallel", "arbitrary"),
        operands=(proj, proj, proj, proj, o_sv, dyb, states, a_sv, cum_sv, lb, gn))


def _inproj_bwd_x(dsegs, w_st, layer, x2, gain, dxm, ride=None):
    T, D = x2.shape
    tm = _tile(T, 512)
    nt = T // tm

    def body(*refs):
        seg_refs = refs[:N_SEG]
        w_ref, x_ref, g_ref, dxm_ref, dx_ref, dg_ref = refs[N_SEG:]
        s = pl.program_id(1)

        @pl.when(s == 0)
        def _():
            dx_ref[...] = jnp.zeros((tm, D), F32)

        for ss in range(N_SHARD):
            @pl.when(s == ss)
            def _(ss=ss):
                dx_ref[...] += (_dot_nt(seg_refs[2 * ss][...], w_ref[:, 0:D])
                                + _dot_nt(seg_refs[2 * ss + 1][...], w_ref[:, D:2 * D]))

        @pl.when(s == N_SHARD - 1)
        def _():
            x = x_ref[...]
            dxn, dg = _rms_bwd(dx_ref[...], x, _rms(x), g_ref[...])
            dx_ref[...] = dxm_ref[...] + dxn
            dg_ref[...] = dg

    row = pl.BlockSpec((tm, D), lambda i, s: (i, 0))

    def seg_spec(kk):
        return pl.BlockSpec((tm, D), lambda i, s: (jnp.minimum(i + jnp.where(s > kk // 2, 1, 0), nt - 1), 0))

    return _ride_call(
        body, ride, name="inproj_bwd_x",
        grid=(nt, N_SHARD),
        in_specs=[seg_spec(kk) for kk in range(N_SEG)]
        + [pl.BlockSpec((None, None, D, 2 * D), lambda i, s: (s, layer, 0, 0)), row,
           pl.BlockSpec((1, D), lambda i, s: (0, 0)), row],
        out_specs=[row, pl.BlockSpec((None, SUBLANES, D), lambda i, s: (i, 0, 0))],
        out_shape=[jax.ShapeDtypeStruct((T, D), F32), jax.ShapeDtypeStruct((nt, SUBLANES, D), F32)],
        scratch_shapes=[],
        semantics=("parallel", "arbitrary"),
        operands=(*dsegs, w_st, x2, gain, dxm))


def _inproj_bwd_w(h, dsegs, layer, n_layers, bufs):
    T, D = h.shape
    tk = _tile(T, 1024)
    out_shape, extra = _layer_slot(bufs, [(D, 2 * D)], n_layers)

    def body(*refs):
        h_ref = refs[0]
        seg_refs = refs[1:1 + N_SEG]
        g_ref = refs[-1]
        k = pl.program_id(0)

        @pl.when(pl.program_id(1) == 0)
        def _():
            g_ref[...] = jnp.zeros((D, D), F32)

        for kk in range(N_SEG):
            @pl.when(k == kk)
            def _(kk=kk):
                g_ref[...] += _dot_tn(h_ref[...], seg_refs[kk][...])

    def seg_spec(kk):
        return pl.BlockSpec((tk, D), lambda k, t: (jnp.where(k == kk, t, 0), 0))

    return pl.pallas_call(
        body, name="inproj_bwd_w",
        grid=(N_SEG, T // tk),
        in_specs=[pl.BlockSpec((tk, D), lambda k, t: (t, 0))] + [seg_spec(kk) for kk in range(N_SEG)]
        + [ANY] * len(extra),
        out_specs=[pl.BlockSpec((None, None, D, D), lambda k, t: (k // 2, layer, 0, k % 2))],
        out_shape=out_shape,
        input_output_aliases={1 + N_SEG + i: i for i in range(len(extra))},
        compiler_params=_cp(("parallel", "arbitrary")),
    )(h, *dsegs, *extra)[0]


def _softmax_rows(lg_ref, L):
    rows = [lg_ref[l:l + 1, :] for l in range(L)]
    mx = functools.reduce(jnp.maximum, rows)
    es = [jnp.exp(r - mx) for r in rows]
    den = functools.reduce(lambda p, q: p + q, es)
    return [e / den for e in es]


def _prep(lb_logits, lam):
    L, D = lb_logits.shape

    def body(lg_ref, lam_ref, lowb_ref, sp_ref):
        sm = _softmax_rows(lg_ref, L)
        run = sm[0]
        for l in range(L):
            if l > 0:
                run = run + sm[l]
            lowb_ref[l:l + 1, :] = jnp.clip(run - sm[0], 0.0, 1.0)
        y = -lam_ref[...]
        sp_ref[...] = jnp.maximum(y, 0.0) + jnp.log1p(jnp.exp(-jnp.abs(y)))

    return pl.pallas_call(
        body, name="prep_small",
        out_shape=[jax.ShapeDtypeStruct((L, D), F32)] * 2,
    )(lb_logits, lam)


def _local_step(x, tgt, lowb, sp, norm_mix, wbufs, conv_w, conv_b, w_r, b_r, w_i, b_i, hg_norm,
                norm_mlp, norm_final):
    B, S, D = x.shape
    L = norm_mix.shape[0]
    T = B * S
    x2 = x.reshape(T, D)
    row = lambda a, l: a[l:l + 1]

    def weight_views(bufs):
        f4 = bufs[2].shape[-1]
        return (bufs[0].reshape(N_SHARD, L, D, 2 * D), bufs[1].reshape(N_SHARD, L, D // N_SHARD, D),
                bufs[2].reshape(N_SHARD, L, D, f4), bufs[3].reshape(N_SHARD, L, f4, D))

    w_in_st, w_out_st, w_up_st, w_down_st = weight_views(wbufs)
    saved = []
    for l in range(L):
        proj, h = _inproj_fwd(x2, row(norm_mix, l), w_in_st, l)
        ya, hrg = _rg_fwd(proj, B, conv_w[l], row(conv_b, l), w_r[l], row(b_r, l), w_i[l], row(b_i, l), row(sp, l))
        l0, nl = (1, min(2, L - 1)) if l == 0 else (l + 2, 1 if l + 2 < L else 0)
        if nl > 0:
            yb, o, st, a_sv, cum_sv, *wbufs = _hg_fwd(proj, B, row(lowb, l), row(hg_norm, l),
                                                      _gather_ride(list(wbufs), LINK_SPLIT, l0, nl))
            w_in_st, w_out_st, w_up_st, w_down_st = weight_views(wbufs)
        else:
            yb, o, st, a_sv, cum_sv = _hg_fwd(proj, B, row(lowb, l), row(hg_norm, l))
        xm, ymix = _out_fwd(ya, yb, proj, x2, w_out_st, l)
        xo, up, h2 = _mlp_fwd(xm, row(norm_mlp, l), w_up_st, w_down_st, l)
        saved.append((x2, proj, h, ya, hrg, yb, o, (st, a_sv, cum_sv), xm, ymix, up, h2))
        x2 = xo
    loss_parts, dx, g_nf = _final_loss(x2, norm_final[None, :], tgt.reshape(T, D))

    def reduce_part(g_in, g_out, g_mlp, gate_list):
        gate_list = gate_list[::-1]
        grads = [g_in, g_out, g_mlp[0], g_mlp[1], _shard_gate(jnp.stack([g[0] for g in gate_list])),
                 _shard_gate(jnp.stack([g[1] for g in gate_list]))]
        return _ReduceScatter([g.reshape(N_SHARD, -1, g.shape[-1]) for g in grads], LINK_SPLIT)

    gates = []
    small = []
    g_in = g_out = g_mlp = None
    rest = None
    for l in reversed(range(L)):
        x_in, proj, h, ya, hrg, yb, o, st, xm, ymix, up, h2 = saved[l]
        alone = l == 0 and L > 1
        if alone:
            rest = reduce_part(g_in, g_out, g_mlp, gates)
            g_in = g_out = g_mlp = None
            gates = []
        slot, n_slots = (0, 1) if l == 0 else (l - 1, L - 1)
        dxm, dup, g_nmlp, dxb, *got = _mlp_bwd_x(dx, xm, up, row(norm_mlp, l), w_up_st, w_down_st, l,
                                                 rest.ride_c() if alone else None)
        if alone:
            rest.after_c(got)
        g_mlp = _mlp_bwd_w(up, dxb, h2, dup, slot, n_slots, g_mlp)
        dya, dyb, dma, dmb = _out_bwd_x(dxm, ya, yb, proj, w_out_st, l)
        g_out = _out_bwd_w(ymix, dxm, slot, n_slots, None if g_out is None else [g_out])
        dxa, dga, g_wr, g_wi, g_cw, g_cb, g_br, g_bi, g_sp = _rg_bwd(
            proj, hrg, dya, B, conv_w[l], row(conv_b, l), w_r[l], row(b_r, l), w_i[l], row(b_i, l), row(sp, l))
        dq, dz, dv, dg, g_lb, g_gn, *got = _hg_bwd(proj, o, dyb, *st, B, row(lowb, l), row(hg_norm, l),
                                                   rest.ride_1() if alone else None)
        if alone:
            rest.after_1(got)
        dsegs = (dxa, dga, dq, dz, dv, dg, dma, dmb)
        dx, g_nmix, *got = _inproj_bwd_x(dsegs, w_in_st, l, x_in, row(norm_mix, l), dxm,
                                         rest.ride_2() if alone else None)
        if alone:
            rest.after_2(got)
        g_in = _inproj_bwd_w(h, dsegs, slot, n_slots, None if g_in is None else [g_in])
        gates.append((g_wr, g_wi))
        small.append((g_lb, g_nmix, g_cb, g_br, g_bi, g_sp, g_nmlp, g_gn, g_cw))
    small.reverse()
    first = reduce_part(g_in, g_out, g_mlp, gates)
    first.exchange_c()
    first.exchange_1()
    first.exchange_2()
    parts = [first.finish()] + ([rest.finish()] if rest is not None else [])
    return loss_parts, dx.reshape(B, S, D), parts, small, g_nf


def _me():
    return lax.axis_index("x"), lax.axis_index("y"), lax.axis_index("c")


def _cast_place(w, slot):
    R, N = w.shape
    tr = _tile(R, max(16, (1 << 20) // N))

    def body(slot_ref, w_ref, o_ref):
        o_ref[...] = w_ref[...].astype(BF16)

    return pl.pallas_call(
        body, name="cast_place",
        grid_spec=pltpu.PrefetchScalarGridSpec(
            num_scalar_prefetch=1, grid=(R // tr,),
            in_specs=[pl.BlockSpec((tr, N), lambda i, slot: (i, 0))],
            out_specs=pl.BlockSpec((None, tr, N), lambda i, slot: (slot[0], i, 0))),
        out_shape=jax.ShapeDtypeStruct((N_SHARD, R, N), BF16),
        compiler_params=_cp(("parallel",)),
    )(slot, w)


def _gather_weights(bufs, first_axes, l0, nl):
    n = len(bufs)
    phases = _gather_phases(n, first_axes, l0, nl)

    def body(*refs):
        outs = refs[n:2 * n]
        ssem, rsem = refs[2 * n:]
        for ph in phases:
            ph(outs, ssem, rsem)

    return pl.pallas_call(
        body, name="gather_weights",
        in_specs=[ANY] * n, out_specs=[ANY] * n,
        out_shape=[jax.ShapeDtypeStruct(b.shape, b.dtype) for b in bufs],
        input_output_aliases={a: a for a in range(n)},
        scratch_shapes=_gather_sems(n),
        compiler_params=pltpu.CompilerParams(has_side_effects=True),
    )(*bufs)


def _gather_sems(n):
    return [pltpu.SemaphoreType.DMA((n, 6)), pltpu.SemaphoreType.DMA((n, 6))]


def _gather_phases(n, first_axes, l0, nl):
    def ctx(outs, ssem, rsem):
        x, y, c = _me()

        def piece(a, flips, half):
            sx = 1 - x if flips[0] else x
            sy = 1 - y if flips[1] else y
            return outs[a].at[2 * sx + sy, pl.ds(l0[a], nl[a]), half]

        def rcopy(a, k, ref, dev):
            return pltpu.make_async_remote_copy(src_ref=ref, dst_ref=ref, send_sem=ssem.at[a, k], recv_sem=rsem.at[a, k],
                                                device_id=dev, device_id_type=MESH_ID)

        def route(a):
            fx = first_axes[a] == "x"
            f_dev = (1 - x, y, c) if fx else (x, 1 - y, c)
            g_dev = (x, 1 - y, c) if fx else (1 - x, y, c)
            return f_dev, g_dev, ((1, 0) if fx else (0, 1)), ((0, 1) if fx else (1, 0))

        return c, (x, y, 1 - c), piece, rcopy, route

    def own_halves(outs, ssem, rsem):
        c, sib, piece, rcopy, route = ctx(outs, ssem, rsem)
        for a in range(n):
            f_dev, g_dev, _, _ = route(a)
            own = piece(a, (0, 0), c)
            rcopy(a, 0, own, f_dev).start()
            rcopy(a, 1, own, g_dev).start()

    def pass_on_neighbours(outs, ssem, rsem):
        c, sib, piece, rcopy, route = ctx(outs, ssem, rsem)
        for a in range(n):
            f_dev, g_dev, f_flip, _ = route(a)
            got = piece(a, f_flip, c)
            rcopy(a, 0, got, f_dev).wait_recv()
            rcopy(a, 2, got, g_dev).start()
            rcopy(a, 3, got, sib).start()
        for a in range(n):
            _, g_dev, _, g_flip = route(a)
            got = piece(a, g_flip, c)
            rcopy(a, 1, got, g_dev).wait_recv()
            rcopy(a, 4, got, sib).start()

    def pass_on_diagonal(outs, ssem, rsem):
        c, sib, piece, rcopy, route = ctx(outs, ssem, rsem)
        for a in range(n):
            _, g_dev, _, _ = route(a)
            got = piece(a, (1, 1), c)
            rcopy(a, 2, got, g_dev).wait_recv()
            rcopy(a, 5, got, sib).start()

    def drain(outs, ssem, rsem):
        c, sib, piece, rcopy, route = ctx(outs, ssem, rsem)
        for a in range(n):
            f_dev, g_dev, f_flip, g_flip = route(a)
            for k, fl in ((3, f_flip), (4, g_flip), (5, (1, 1))):
                rcopy(a, k, piece(a, fl, 1 - c), sib).wait_recv()
            own = piece(a, (0, 0), c)
            for k, dev in ((0, f_dev), (1, g_dev), (2, g_dev), (3, sib), (4, sib), (5, sib)):
                rcopy(a, k, own, dev).wait_send()

    return [own_halves, pass_on_neighbours, pass_on_diagonal, drain]


def _gather_ride(bufs, first_axes, l0, nl):
    n = len(bufs)
    phases = [lambda reads, refs, outs, ssem, rsem, ph=ph: ph(refs, ssem, rsem)
              for ph in _gather_phases(n, first_axes, [l0] * n, [nl] * n)]
    return _Ride([], bufs, [], phases, GATHER_STEPS, _gather_sems(n))


def _shard_gate(g):
    n_l, nb = g.shape[:2]
    return g.reshape(n_l, nb, N_SHARD, RG_BLOCK // N_SHARD, RG_BLOCK).transpose(2, 0, 1, 3, 4)


def _exchange(arrs, axes, name):
    n = len(arrs)

    def body(*refs):
        ins, outs = refs[:n], refs[n:2 * n]
        ssem, rsem = refs[2 * n:]
        x, y, c = _me()
        cps = []
        for a in range(n):
            my = {"x": x, "y": y, "c": c}[axes[a]]
            partner = {"x": (1 - x, y, c), "y": (x, 1 - y, c), "c": (x, y, 1 - c)}[axes[a]]
            cps.append(pltpu.make_async_remote_copy(
                src_ref=ins[a].at[:, 1 - my], dst_ref=outs[a], send_sem=ssem.at[a], recv_sem=rsem.at[a],
                device_id=partner, device_id_type=MESH_ID))
            cps[-1].start()
        for cp in cps:
            cp.wait()

    return pl.pallas_call(
        body, name=name,
        in_specs=[ANY] * n, out_specs=[ANY] * n,
        out_shape=[jax.ShapeDtypeStruct((a.shape[0],) + a.shape[2:], a.dtype) for a in arrs],
        scratch_shapes=[pltpu.SemaphoreType.DMA((n,)), pltpu.SemaphoreType.DMA((n,))],
        compiler_params=pltpu.CompilerParams(has_side_effects=True),
    )(*arrs)


def _add_kept(arr, got, idx, name, with_bf16):
    P, _, R, N = arr.shape
    tr = _tile(R, max(16, (1 << 20) // N))

    def body(idx_ref, a_ref, g_ref, o_ref, *ob_ref):
        s = a_ref[...] + g_ref[...].astype(F32)
        o_ref[...] = s
        if with_bf16:
            ob_ref[0][...] = s.astype(BF16)

    out_blk = pl.BlockSpec((None, tr, N), lambda p, i, idx: (p, i, 0))
    return pl.pallas_call(
        body, name=name,
        grid_spec=pltpu.PrefetchScalarGridSpec(
            num_scalar_prefetch=1, grid=(P, R // tr),
            in_specs=[pl.BlockSpec((None, None, tr, N), lambda p, i, idx: (p, idx[0], i, 0)),
                      pl.BlockSpec((None, tr, N), lambda p, i, idx: (p, i, 0))],
            out_specs=[out_blk] * (2 if with_bf16 else 1)),
        out_shape=[jax.ShapeDtypeStruct((P, R, N), F32)] + ([jax.ShapeDtypeStruct((P, R, N), BF16)] if with_bf16 else []),
        compiler_params=_cp(("parallel", "parallel")),
    )(idx, arr, got)


def _share_halves(halves):
    n = len(halves)

    def body(*refs):
        ins, outs = refs[:n], refs[n:2 * n]
        ssem, rsem = refs[2 * n:]
        x, y, c = _me()
        cps = []
        for a in range(n):
            cps.append(pltpu.make_async_remote_copy(
                src_ref=ins[a], dst_ref=outs[a], send_sem=ssem.at[a], recv_sem=rsem.at[a],
                device_id=(x, y, 1 - c), device_id_type=MESH_ID))
            cps[-1].start()
        for cp in cps:
            cp.wait()

    return pl.pallas_call(
        body, name="share_halves",
        in_specs=[ANY] * n, out_specs=[ANY] * n,
        out_shape=[jax.ShapeDtypeStruct(h.shape, h.dtype) for h in halves],
        scratch_shapes=[pltpu.SemaphoreType.DMA((n,)), pltpu.SemaphoreType.DMA((n,))],
        compiler_params=pltpu.CompilerParams(has_side_effects=True),
    )(*halves)


def _exchange_ride(arrs, axes):
    n = len(arrs)

    def copies(reads, outs, ssem, rsem):
        x, y, c = _me()
        cps = []
        for a in range(n):
            my = {"x": x, "y": y, "c": c}[axes[a]]
            partner = {"x": (1 - x, y, c), "y": (x, 1 - y, c), "c": (x, y, 1 - c)}[axes[a]]
            cps.append(pltpu.make_async_remote_copy(
                src_ref=reads[a].at[:, 1 - my], dst_ref=outs[a], send_sem=ssem.at[a], recv_sem=rsem.at[a],
                device_id=partner, device_id_type=MESH_ID))
        return cps

    def start(reads, bufs, outs, ssem, rsem):
        for cp in copies(reads, outs, ssem, rsem):
            cp.start()

    def finish(reads, bufs, outs, ssem, rsem):
        for cp in copies(reads, outs, ssem, rsem):
            cp.wait()

    landing = [jax.ShapeDtypeStruct((a.shape[0],) + a.shape[2:], a.dtype) for a in arrs]
    return _Ride(arrs, [], landing, [start, finish], (0.0, 1.0),
                 [pltpu.SemaphoreType.DMA((n,)), pltpu.SemaphoreType.DMA((n,))])


class _ReduceScatter:
    def __init__(self, grads, first_axes):
        x, y, c = _me()
        idx = lambda v: jnp.reshape(v, (1,)).astype(jnp.int32)
        self.coord = {"x": idx(x), "y": idx(y), "c": idx(c)}
        self.first = list(first_axes)
        self.second = ["y" if f == "x" else "x" for f in first_axes]
        self.views_c = [g.reshape(N_SHARD, 2, g.shape[1] // 2, g.shape[2]) for g in grads]

    def ride_c(self):
        return _exchange_ride(self.views_c, "c" * len(self.views_c))

    def exchange_c(self):
        self.after_c(_exchange(self.views_c, "c" * len(self.views_c), "rs_exchange_c"))

    def after_c(self, got):
        self.summed = [_add_kept(v, r, self.coord["c"], "rs_add_c", True) for v, r in zip(self.views_c, got)]

    @staticmethod
    def _split_view(a, ax):
        _, rh, nn = a.shape
        return a.reshape(1, 2, 2 * rh, nn) if ax == "x" else a.reshape(2, 2, rh, nn)

    def _views_1(self):
        return [self._split_view(s[1], f) for s, f in zip(self.summed, self.first)]

    def ride_1(self):
        return _exchange_ride(self._views_1(), self.first)

    def exchange_1(self):
        self.after_1(_exchange(self._views_1(), self.first, "rs_exchange_1"))

    def after_1(self, got):
        self.summed = [_add_kept(self._split_view(s[0], f), r, self.coord[f], "rs_add_1", True)
                       for s, r, f in zip(self.summed, got, self.first)]

    def _views_2(self):
        return [s[1].reshape(1, 2, -1, s[1].shape[-1]) for s in self.summed]

    def ride_2(self):
        return _exchange_ride(self._views_2(), self.second)

    def exchange_2(self):
        self.after_2(_exchange(self._views_2(), self.second, "rs_exchange_2"))

    def after_2(self, got):
        views32 = [s[0].reshape(1, 2, -1, s[0].shape[-1]) for s in self.summed]
        self.kept = [_add_kept(v, r, self.coord[g], "rs_add_2", False)[0][0]
                     for v, r, g in zip(views32, got, self.second)]

    def finish(self):
        return self.kept, _share_halves(self.kept)


def _allgather_small(p):
    R, D = p.shape

    def body(p_ref, o_ref, ssem, rsem):
        x, y, c = _me()
        me = 4 * x + 2 * y + c
        o_ref[me] = p_ref[...]
        cps = []
        for m in range(1, 8):
            mx, my, mc = (m >> 2) & 1, (m >> 1) & 1, m & 1
            peer = (1 - x if mx else x, 1 - y if my else y, 1 - c if mc else c)
            cps.append(pltpu.make_async_remote_copy(
                src_ref=p_ref, dst_ref=o_ref.at[me], send_sem=ssem.at[m - 1], recv_sem=rsem.at[m - 1],
                device_id=peer, device_id_type=MESH_ID))
            cps[-1].start()
        for cp in cps:
            cp.wait()

    return pl.pallas_call(
        body, name="allgather_small",
        in_specs=[pl.BlockSpec(memory_space=pltpu.VMEM)],
        out_specs=pl.BlockSpec(memory_space=pltpu.VMEM),
        out_shape=jax.ShapeDtypeStruct((8, R, D), p.dtype),
        scratch_shapes=[pltpu.SemaphoreType.DMA((7,)), pltpu.SemaphoreType.DMA((7,))],
        compiler_params=pltpu.CompilerParams(has_side_effects=True, vmem_limit_bytes=VMEM_LIMIT),
    )(p)


def _adam_math(w, g, m, v):
    m = ADAM_B1 * m + (1.0 - ADAM_B1) * g
    v = ADAM_B2 * v + (1.0 - ADAM_B2) * (g * g)
    m_hat = m / (1.0 - ADAM_B1 ** ADAM_STEP)
    v_hat = v / (1.0 - ADAM_B2 ** ADAM_STEP)
    delta = -ADAM_LR * (m_hat / (jnp.sqrt(v_hat) + ADAM_EPS) + ADAM_WD * w)
    return delta, m, v


def _adam(w, g_mine, g_sib, m, v, core, row0, layer_rows, outs):
    R, N = w.shape
    rh = g_mine.shape[0]
    tr = _tile(layer_rows // 2, max(16, (1 << 19) // N))
    nt = rh // tr
    t0 = row0 // tr
    extra = [] if outs is None else list(outs)

    def body(c_ref, w_ref, gm_ref, gs_ref, m_ref, v_ref, *rest):
        g_ref, d_ref, nm_ref, nv_ref = rest[-4:]
        g = jnp.where(pl.program_id(0) == c_ref[0], gm_ref[...], gs_ref[...])
        d, nm, nv = _adam_math(w_ref[...], g, m_ref[...], v_ref[...])
        g_ref[...] = g
        d_ref[...] = d
        nm_ref[...] = nm
        nv_ref[...] = nv

    blk = pl.BlockSpec((tr, N), lambda h, i, c: (t0 + h * nt + i, 0))
    half = pl.BlockSpec((tr, N), lambda h, i, c: (i, 0))
    return pl.pallas_call(
        body, name="adamw",
        grid_spec=pltpu.PrefetchScalarGridSpec(
            num_scalar_prefetch=1, grid=(2, nt),
            in_specs=[blk, half, half, blk, blk] + [ANY] * len(extra), out_specs=[blk] * 4),
        out_shape=[jax.ShapeDtypeStruct((R, N), F32)] * 4,
        input_output_aliases={6 + i: i for i in range(len(extra))},
        compiler_params=_cp(("parallel", "parallel")),
    )(core, w, g_mine, g_sib, m, v, *extra)


def _reduce_rows(parts, sizes, rows_out):
    D = parts.shape[1]

    def body(p_ref, o_ref):
        o_ref[...] = jnp.zeros((rows_out, D), F32)
        off = 0
        for i, sz in enumerate(sizes):
            o_ref[i:i + 1, :] = jnp.sum(p_ref[off:off + sz, :], axis=0, keepdims=True)
            off += sz

    return pl.pallas_call(
        body, name="reduce_rows",
        out_shape=jax.ShapeDtypeStruct((rows_out, D), F32),
        compiler_params=pltpu.CompilerParams(vmem_limit_bytes=VMEM_LIMIT),
    )(parts)


def _sum_devices(g8):
    _, R, D = g8.shape

    def body(g_ref, o_ref):
        tot = g_ref[0]
        for k in range(1, 8):
            tot = tot + g_ref[k]
        o_ref[...] = tot

    return pl.pallas_call(
        body, name="sum_devices",
        out_shape=jax.ShapeDtypeStruct((R, D), F32),
        compiler_params=pltpu.CompilerParams(vmem_limit_bytes=VMEM_LIMIT),
    )(g8)


def _small_update(gathered, w, m, v, L):
    _, R, D = gathered.shape

    def body(g8_ref, w_ref, m_ref, v_ref, g_ref, d_ref, nm_ref, nv_ref):
        tot = g8_ref[0]
        for k in range(1, 8):
            tot = tot + g8_ref[k]
        g_ref[...] = tot
        sm = _softmax_rows(w_ref, L)
        run = sm[0]
        dcum = []
        for l in range(L):
            if l > 0:
                run = run + sm[l]
            cum = run - sm[0]
            dcum.append(jnp.where((cum > 0.0) & (cum < 1.0), g_ref[l:l + 1, :], 0.0))
        dsm = [jnp.zeros((1, D), F32)]
        for i in range(1, L):
            dsm.append(functools.reduce(lambda p, q: p + q, dcum[i:]))
        dot = functools.reduce(lambda p, q: p + q, [s * d for s, d in zip(sm, dsm)])
        for l in range(L):
            g_ref[l:l + 1, :] = sm[l] * (dsm[l] - dot)
        lam = w_ref[5 * L:6 * L, :]
        g_ref[5 * L:6 * L, :] = g_ref[5 * L:6 * L, :] * (-_sig(-lam))
        d, nm, nv = _adam_math(w_ref[...], g_ref[...], m_ref[...], v_ref[...])
        d_ref[...] = d
        nm_ref[...] = nm
        nv_ref[...] = nv

    return pl.pallas_call(
        body, name="small_update",
        out_shape=[jax.ShapeDtypeStruct((R, D), F32)] * 4,
        compiler_params=pltpu.CompilerParams(vmem_limit_bytes=VMEM_LIMIT),
    )(gathered, w, m, v)


def kernel(x, lb_logits, norm_mix, w_in, conv_w, conv_b, w_r, b_r, w_i, b_i, lam, hg_norm, w_out, norm_mlp, w_up, w_down, norm_final, loss_target, m_lb_logits, m_norm_mix, m_w_in, m_conv_w, m_conv_b, m_w_r, m_b_r, m_w_i, m_b_i, m_lam, m_hg_norm, m_w_out, m_norm_mlp, m_w_up, m_w_down, m_norm_final, v_lb_logits, v_norm_mix, v_w_in, v_conv_w, v_conv_b, v_w_r, v_b_r, v_w_i, v_b_i, v_lam, v_hg_norm, v_w_out, v_norm_mlp, v_w_up, v_w_down, v_norm_final):
    B, S, D = x.shape
    L = norm_mix.shape[0]
    nb = D // RG_BLOCK
    Dq = D // N_SHARD
    mx, my, mc = _me()
    shard = 2 * mx + my

    big_w = (w_in, w_out, w_up, w_down, w_r, w_i)
    flat2 = lambda a: a.reshape(-1, a.shape[-1])
    slot = jnp.reshape(shard, (1,)).astype(jnp.int32)
    def place(w):
        b = _cast_place(flat2(w), slot)
        return b.reshape(N_SHARD, L, 2, b.shape[1] // (2 * L), b.shape[2])

    *wbufs, g_r, g_i = _gather_weights([place(w) for w in big_w], LINK_SPLIT, [0] * 6, [1] * 4 + [L] * 2)
    unshard_gate = lambda g: g.reshape(N_SHARD, L, nb, RG_BLOCK // N_SHARD, RG_BLOCK).transpose(1, 2, 0, 3, 4).reshape(
        L, nb, RG_BLOCK, RG_BLOCK)
    w_r_full, w_i_full = unshard_gate(g_r), unshard_gate(g_i)

    R_LB, R_NMIX, R_CB, R_BR, R_BI, R_LAM, R_NMLP, R_GN, R_CW, R_NF, R_LOSS = (
        0, L, 2 * L, 3 * L, 4 * L, 5 * L, 6 * L, 7 * L, 8 * L, 12 * L, 12 * L + 1)
    n_rows = 12 * L + 2
    rows_pad = n_rows + (-n_rows) % SUBLANES

    def place_cols(a):
        return lax.dynamic_update_slice(jnp.zeros((a.shape[0], D), F32), a, (0, shard * Dq))

    def pack_small(lb_, nmix_, cb_, br_, bi_, lam_, nmlp_, gn_, cw_, nf_):
        gn_pad = jnp.pad(gn_, ((0, 0), (0, D - HEAD)))
        rows = [lb_, nmix_, cb_, br_, bi_, lam_, nmlp_, gn_pad, place_cols(cw_.reshape(L * CONV_TAPS, Dq)),
                nf_[None, :], jnp.zeros((rows_pad - n_rows + 1, D), F32)]
        return jnp.concatenate(rows, axis=0)

    w_small = pack_small(lb_logits, norm_mix, conv_b, b_r, b_i, lam, norm_mlp, hg_norm, conv_w, norm_final)
    m_small = pack_small(m_lb_logits, m_norm_mix, m_conv_b, m_b_r, m_b_i, m_lam, m_norm_mlp, m_hg_norm, m_conv_w,
                         m_norm_final)
    v_small = pack_small(v_lb_logits, v_norm_mix, v_conv_b, v_b_r, v_b_i, v_lam, v_norm_mlp, v_hg_norm, v_conv_w,
                         v_norm_final)
    cw_rows = place_cols(conv_w.reshape(L * CONV_TAPS, Dq)) * jnp.where(mc == 0, 1.0, 0.0)
    conv_w_full = _sum_devices(_allgather_small(cw_rows)).reshape(L, CONV_TAPS, D)

    lowb, sp = _prep(lb_logits, lam)

    loss_parts, grad_x, parts, small, g_nf = _local_step(
        x, loss_target, lowb, sp, norm_mix, wbufs, conv_w_full, conv_b, w_r_full, b_r, w_i_full, b_i, hg_norm,
        norm_mlp, norm_final)

    core = jnp.reshape(mc, (1,)).astype(jnp.int32)
    outs = {}
    for a, (name, w, m, v) in enumerate(zip(("w_in", "w_out", "w_up", "w_down", "w_r", "w_i"), big_w,
                                            (m_w_in, m_w_out, m_w_up, m_w_down, m_w_r, m_w_i),
                                            (v_w_in, v_w_out, v_w_up, v_w_down, v_w_r, v_w_i))):
        layer_rows = flat2(w).shape[0] // L
        done, row0 = None, 0
        for mine, sibs in parts:
            done = _adam(flat2(w), mine[a], sibs[a], flat2(m), flat2(v), core, row0, layer_rows, done)
            row0 += 2 * mine[a].shape[0]
        outs[name] = tuple(t.reshape(w.shape) for t in done)

    parts, sizes = [], []

    def add_rows(a):
        a = a.reshape(-1, a.shape[-1])
        if a.shape[1] != D:
            a = jnp.pad(a, ((0, 0), (0, D - a.shape[1])))
        parts.append(a)
        sizes.append(a.shape[0])

    for i in range(8):
        for l in range(L):
            add_rows(small[l][i])
    for l in range(L):
        for j in range(CONV_TAPS):
            add_rows(small[l][8][j])
    add_rows(g_nf)
    loss_rows = loss_parts[:, 0:1, :]
    add_rows(jnp.where(lax.broadcasted_iota(jnp.int32, loss_rows.shape, 2) == 0, loss_rows, 0.0))
    g_small = _reduce_rows(jnp.concatenate(parts, axis=0), sizes, rows_pad)
    g_small, d_small, nm_small, nv_small = _small_update(_allgather_small(g_small), w_small, m_small, v_small, L)

    def unpack(t):
        take_cols = lambda a: lax.dynamic_slice(a, (0, shard * Dq), (a.shape[0], Dq))
        return {"lb_logits": t[R_LB:R_LB + L], "norm_mix": t[R_NMIX:R_NMIX + L], "conv_b": t[R_CB:R_CB + L],
                "b_r": t[R_BR:R_BR + L], "b_i": t[R_BI:R_BI + L], "lam": t[R_LAM:R_LAM + L],
                "norm_mlp": t[R_NMLP:R_NMLP + L], "hg_norm": t[R_GN:R_GN + L, :HEAD],
                "conv_w": take_cols(t[R_CW:R_CW + L * CONV_TAPS]).reshape(L, CONV_TAPS, Dq), "norm_final": t[R_NF]}

    small_out = [unpack(t) for t in (g_small, d_small, nm_small, nv_small)]
    loss = g_small[R_LOSS, 0]
    names = ("lb_logits", "norm_mix", "w_in", "conv_w", "conv_b", "w_r", "b_r", "w_i", "b_i", "lam", "hg_norm",
             "w_out", "norm_mlp", "w_up", "w_down", "norm_final")
    result = [loss, grad_x]
    for kind in range(4):
        for nme in names:
            result.append(outs[nme][kind] if nme in outs else small_out[kind][nme])
    return tuple(result)
```

```python
import functools
import math

import jax
import jax.numpy as jnp
from jax import lax
from jax.experimental import pallas as pl
from jax.experimental.pallas import tpu as pltpu

F32 = jnp.float32
BF16 = jnp.bfloat16

HEAD = 128
RG_BLOCK = 256
CONV_TAPS = 4
RG_C = 8.0
F_MIN = 1e-30
NORM_EPS = 1e-6
N_SEG = 8
N_SHARD = 4
HG_CHUNK = 256
HG_HEADS_PER_STEP = 8
RG_TILE = 256
ADAM_LR, ADAM_B1, ADAM_B2, ADAM_EPS, ADAM_WD, ADAM_STEP = 0.001, 0.9, 0.999, 1e-08, 0.01, 10
V7X_VMEM_BYTES = 64 * 1024 * 1024
VMEM_LIMIT = V7X_VMEM_BYTES - 8 * 1024 * 1024
SUBLANES = 8
LINK_SPLIT = "xxyyyy"
GATHER_STEPS = (0.0, 0.6, 0.88, 1.0)
MESH_ID = pl.DeviceIdType.MESH
ANY = pl.BlockSpec(memory_space=pl.ANY)


def _cp(sem):
    return pltpu.CompilerParams(dimension_semantics=sem, vmem_limit_bytes=VMEM_LIMIT)


def _dot(a, b):
    return jnp.dot(a, b, preferred_element_type=F32)


def _dot_nt(a, b):
    return lax.dot_general(a, b, (((1,), (1,)), ((), ())), preferred_element_type=F32)


def _dot_tn(a, b):
    return lax.dot_general(a, b, (((0,), (0,)), ((), ())), preferred_element_type=F32)


def _dot_01(m01, x):
    n = x.shape[1]
    hi = x.astype(BF16)
    r1 = x - hi.astype(F32)
    mid = r1.astype(BF16)
    lo = (r1 - mid.astype(F32)).astype(BF16)
    y = _dot(m01, jnp.concatenate([hi, mid, lo], axis=1))
    return y[:, :n] + y[:, n:2 * n] + y[:, 2 * n:]


def _sig(x):
    return jax.nn.sigmoid(x)


def _rows8(x):
    return x.reshape(x.shape[0] // SUBLANES, SUBLANES, x.shape[1]).sum(axis=0)


def _tile(n, cap):
    if n <= cap:
        return n
    t = cap - cap % 16
    while n % t:
        t -= 16
    return t


_GELU_C = math.sqrt(2.0 / math.pi)


def _gelu_and_grad(x):
    x2 = x * x
    t = jnp.tanh(_GELU_C * (x + 0.044715 * x * x2))
    g = 0.5 * x * (1.0 + t)
    dg = 0.5 * (1.0 + t) + 0.5 * x * (1.0 - t * t) * (_GELU_C * (1.0 + 3.0 * 0.044715 * x2))
    return g, dg


def _rms(x):
    return lax.rsqrt(jnp.mean(x * x, axis=-1, keepdims=True) + NORM_EPS)


def _rms_bwd(dh, x, rs, gain):
    xhat = x * rs
    dxhat = dh * gain
    dx = rs * (dxhat - xhat * jnp.mean(dxhat * xhat, axis=-1, keepdims=True))
    return dx, _rows8(dh * xhat)


def _inproj_fwd(x2, gain, w_st, layer):
    T, D = x2.shape
    tm = _tile(T, 2048)

    def body(x_ref, g_ref, w_ref, o_ref, h_ref):
        @pl.when(pl.program_id(1) == 0)
        def _():
            x = x_ref[...]
            h_ref[...] = (x * _rms(x) * g_ref[...]).astype(BF16)
        o_ref[...] = _dot(h_ref[...], w_ref[...])

    return pl.pallas_call(
        body, name="inproj_fwd",
        grid=(T // tm, N_SEG),
        in_specs=[pl.BlockSpec((tm, D), lambda i, k: (i, 0)),
                  pl.BlockSpec((1, D), lambda i, k: (0, 0)),
                  pl.BlockSpec((None, None, D, D), lambda i, k: (k // 2, layer, 0, k % 2))],
        out_specs=[pl.BlockSpec((tm, D), lambda i, k: (i, k)),
                   pl.BlockSpec((tm, D), lambda i, k: (i, 0))],
        out_shape=[jax.ShapeDtypeStruct((T, N_SEG * D), F32), jax.ShapeDtypeStruct((T, D), BF16)],
        compiler_params=_cp(("parallel", "arbitrary")),
    )(x2, gain, w_st)


def _rg_gates(xc, wr_ref, br, wi_ref, bi, sp):
    D = xc.shape[1]
    xcb = xc.astype(BF16)
    pr, pi = [], []
    for n in range(D // RG_BLOCK):
        blk = xcb[:, n * RG_BLOCK:(n + 1) * RG_BLOCK]
        pr.append(_dot(blk, wr_ref[n]))
        pi.append(_dot(blk, wi_ref[n]))
    r = _sig(jnp.concatenate(pr, axis=1) + br) if len(pr) > 1 else _sig(pr[0] + br)
    i = _sig(jnp.concatenate(pi, axis=1) + bi) if len(pi) > 1 else _sig(pi[0] + bi)
    la = (-RG_C) * r * sp
    a = jnp.exp(la)
    y = 2.0 * la
    one_m_e2 = jnp.where(y > -1e-2, -(y * (1.0 + 0.5 * y * (1.0 + y * (1.0 / 3.0)))), 1.0 - jnp.exp(y))
    mult = jnp.sqrt(jnp.maximum(one_m_e2, 0.0))
    return r, i, a, mult


def _conv_taps(xbuf, cw_ref, ts):
    acc = None
    for j in range(CONV_TAPS):
        term = cw_ref[j:j + 1, :] * xbuf[pl.ds(SUBLANES - (CONV_TAPS - 1) + j, ts), :]
        acc = term if acc is None else acc + term
    return acc


def _rg_fwd(proj, B, cw, cb, wr, br, wi, bi, sp):
    T = proj.shape[0]
    D = proj.shape[1] // N_SEG
    S = T // B
    ts = _tile(S, RG_TILE)
    nts = S // ts
    nb = D // RG_BLOCK

    def body(xa_ref, ga_ref, cw_ref, cb_ref, wr_ref, br_ref, wi_ref, bi_ref, sp_ref,
             ya_ref, h_ref, xbuf, a_scr, u_scr, carry):
        @pl.when(pl.program_id(1) == 0)
        def _():
            xbuf[0:SUBLANES, :] = jnp.zeros((SUBLANES, D), F32)
            carry[...] = jnp.zeros((SUBLANES, D), F32)

        xbuf[pl.ds(SUBLANES, ts), :] = xa_ref[...]
        xc = _conv_taps(xbuf, cw_ref, ts) + cb_ref[...]
        r, i, a, mult = _rg_gates(xc, wr_ref, br_ref[...], wi_ref, bi_ref[...], sp_ref[...])
        a_scr[...] = a
        u_scr[...] = mult * (i * xc)
        row8 = lax.broadcasted_iota(jnp.int32, (SUBLANES, 1), 0)

        def blk(n, hprev):
            off = pl.multiple_of(n * SUBLANES, SUBLANES)
            a8 = a_scr[pl.ds(off, SUBLANES), :]
            u8 = u_scr[pl.ds(off, SUBLANES), :]
            for d in (1, 2, 4):
                m = row8 >= d
                ap = jnp.where(m, pltpu.roll(a8, d, 0), 1.0)
                up = jnp.where(m, pltpu.roll(u8, d, 0), 0.0)
                u8 = a8 * up + u8
                a8 = a8 * ap
            h8 = u8 + a8 * hprev
            u_scr[pl.ds(off, SUBLANES), :] = h8
            last = jnp.sum(jnp.where(row8 == SUBLANES - 1, h8, 0.0), axis=0, keepdims=True)
            return jnp.broadcast_to(last, (SUBLANES, D))

        carry[...] = lax.fori_loop(0, ts // SUBLANES, blk, carry[...])
        h = u_scr[...]
        h_ref[...] = h
        g, _ = _gelu_and_grad(ga_ref[...])
        ya_ref[...] = (h * g).astype(BF16)
        xbuf[0:SUBLANES, :] = xa_ref[pl.ds(ts - SUBLANES, SUBLANES), :]

    vec = pl.BlockSpec((1, D), lambda b, j: (0, 0))
    gate = pl.BlockSpec((nb, RG_BLOCK, RG_BLOCK), lambda b, j: (0, 0, 0))
    return pl.pallas_call(
        body, name="rg_fwd",
        grid=(B, nts),
        in_specs=[pl.BlockSpec((ts, D), lambda b, j: (b * nts + j, 0)),
                  pl.BlockSpec((ts, D), lambda b, j: (b * nts + j, 1)),
                  pl.BlockSpec((CONV_TAPS, D), lambda b, j: (0, 0)), vec, gate, vec, gate, vec, vec],
        out_specs=[pl.BlockSpec((ts, D), lambda b, j: (b * nts + j, 0))] * 2,
        out_shape=[jax.ShapeDtypeStruct((T, D), BF16), jax.ShapeDtypeStruct((T, D), F32)],
        scratch_shapes=[pltpu.VMEM((SUBLANES + ts, D), F32), pltpu.VMEM((ts, D), F32),
                        pltpu.VMEM((ts, D), F32), pltpu.VMEM((SUBLANES, D), F32)],
        compiler_params=_cp(("arbitrary", "arbitrary")),
    )(proj, proj, cw, cb, wr, br, wi, bi, sp)


def _hg_gates(q, z, lb):
    sig = _sig(z)
    one_m = 1.0 - lb
    fg = lb + one_m * sig
    lf = jnp.log(jnp.maximum(fg, F_MIN))
    kf = one_m * (1.0 - sig)
    qs = _sig(q)
    return q * qs, qs, kf, lf, fg, sig


def _hg_cum(lf, C):
    ri = lax.broadcasted_iota(jnp.int32, (C, C), 0)
    ci = lax.broadcasted_iota(jnp.int32, (C, C), 1)
    return _dot_01(jnp.where(ci <= ri, 1.0, 0.0).astype(BF16), lf)


def _hg_levels(lf, cum, C):
    row = lax.broadcasted_iota(jnp.int32, (C, 1), 0)
    levels = []
    w = C // 2
    while w >= 4:
        blk = 2 * w
        upper = (row & w) != 0
        ref = jnp.min(jnp.where(upper, 0.0, cum).reshape(C // blk, blk, HEAD), axis=1, keepdims=True)
        ref = jnp.broadcast_to(ref, (C // blk, blk, HEAD)).reshape(C, HEAD)
        d = cum - ref
        levels.append(jnp.exp(jnp.where(upper, d, -d)))
        w //= 2
    r4 = row & 3
    lf_prev = pltpu.roll(lf, 1, 0)
    lf_next = pltpu.roll(lf, C - 1, 0)
    levels.append(jnp.exp(jnp.where(r4 == 3, lf + lf_prev, jnp.where(r4 == 2, lf, jnp.where(r4 == 0, lf_next, 0.0)))))
    levels.append(jnp.exp(jnp.where((row & 1) == 1, lf, 0.0)))
    levels.append(None)
    return levels


def _hg_level_blocks(C):
    blks = []
    w = C // 2
    while w >= 4:
        blks.append(2 * w)
        w //= 2
    return blks + [4, 2, 1]


def _hg_fill_masks(mask_scr, C):
    ri = lax.broadcasted_iota(jnp.int32, (C, C), 0)
    ci = lax.broadcasted_iota(jnp.int32, (C, C), 1)
    for i, blk in enumerate(_hg_level_blocks(C)):
        if blk == 1:
            keep = ri == ci
        else:
            shift, w = blk.bit_length() - 1, blk // 2
            keep = ((ri >> shift) == (ci >> shift)) & ((ri & w) != 0) & ((ci & w) == 0)
        mask_scr[i] = jnp.where(keep, 1.0, 0.0).astype(F32)


def _hg_operands(qf, kf, e):
    if e is None:
        return qf.astype(BF16), kf.astype(BF16)
    return (qf * e).astype(BF16), (kf * e).astype(BF16)


def _hg_scores(qf, kf, levels, mask_scr):
    A = None
    for n, e in enumerate(levels):
        a = _dot_nt(*_hg_operands(qf, kf, e)) * mask_scr[n]
        A = a if A is None else A + a
    return A


def _hg_specs(B, NC, D, C, pair):
    def spec(col0, rev):
        c0 = col0 // pair
        if rev:
            return pl.BlockSpec((C, pair * HEAD), lambda b, h, j: (b * NC + (NC - 1 - j), c0 + h))
        return pl.BlockSpec((C, pair * HEAD), lambda b, h, j: (b * NC + j, c0 + h))
    return spec


def _head(ref, hh):
    return ref.at[:, pl.ds(hh * HEAD, HEAD)]


class _Ride:
    def __init__(self, reads, bufs, outs, phases, fractions, sems):
        self.reads, self.bufs, self.outs = list(reads), list(bufs), list(outs)
        self.phases, self.fractions, self.sems = list(phases), list(fractions), list(sems)


def _ride_call(body, ride, *, name, grid, in_specs, out_specs, out_shape, scratch_shapes, semantics, operands):
    if ride is None:
        return pl.pallas_call(body, name=name, grid=grid, in_specs=in_specs, out_specs=out_specs, out_shape=out_shape,
                              scratch_shapes=scratch_shapes, compiler_params=_cp(semantics))(*operands)
    n_in, n_out, n_scr = len(in_specs), len(out_specs), len(scratch_shapes)
    nr, nb, no = len(ride.reads), len(ride.bufs), len(ride.outs)
    last_step = math.prod(grid) - 1

    def full_body(*refs):
        own = refs[:n_in] + refs[n_in + nr + nb:n_in + nr + nb + n_out]
        tail = refs[n_in + nr + nb + n_out:]
        ride_refs = (refs[n_in:n_in + nr], tail[:nb], tail[nb:nb + no])
        scr = tail[nb + no:]
        step = pl.program_id(0)
        for d in range(1, len(grid)):
            step = step * grid[d] + pl.program_id(d)
        for phase, frac in zip(ride.phases, ride.fractions):
            @pl.when(step == int(round(frac * last_step)))
            def _(phase=phase):
                phase(*ride_refs, *scr[n_scr:])
        body(*own, *scr[:n_scr])

    return pl.pallas_call(
        full_body, name=name, grid=grid,
        in_specs=list(in_specs) + [ANY] * (nr + nb),
        out_specs=list(out_specs) + [ANY] * (nb + no),
        out_shape=list(out_shape) + [jax.ShapeDtypeStruct(b.shape, b.dtype) for b in ride.bufs] + ride.outs,
        input_output_aliases={n_in + nr + j: n_out + j for j in range(nb)},
        scratch_shapes=list(scratch_shapes) + ride.sems,
        compiler_params=_cp(("arbitrary",) * len(grid)),
    )(*operands, *ride.reads, *ride.bufs)


def _hg_fwd(proj, B, lb, gn, ride=None):
    T = proj.shape[0]
    D = proj.shape[1] // N_SEG
    S = T // B
    C = min(HG_CHUNK, S)
    NC = S // C
    H = D // HEAD
    hpd = D // HEAD
    pair = math.gcd(H, HG_HEADS_PER_STEP)
    spec = _hg_specs(B, NC, D, C, pair)

    def body(q_ref, z_ref, v_ref, g_ref, lb_ref, gn_ref, yb_ref, o_ref, st_ref, a_ref, cum_ref, st_scr, mask_scr):
        @pl.when(pl.program_id(2) == 0)
        def _():
            st_scr[...] = jnp.zeros((pair, HEAD, HEAD), F32)
            _hg_fill_masks(mask_scr, C)

        for hh in range(pair):
            s_t = st_scr[hh]
            st_ref[hh] = s_t
            qf, _, kf, lf, _, _ = _hg_gates(_head(q_ref, hh)[...], _head(z_ref, hh)[...], _head(lb_ref, hh)[...])
            cum = _hg_cum(lf, C)
            _head(cum_ref, hh)[...] = cum
            A = _hg_scores(qf, kf, _hg_levels(lf, cum, C), mask_scr).astype(BF16)
            a_ref[hh] = A
            vb = _head(v_ref, hh)[...].astype(BF16)
            o = _dot_nt((qf * jnp.exp(cum)).astype(BF16), s_t.astype(BF16)) + _dot(A, vb)
            last = jnp.sum(lf, axis=0, keepdims=True)
            kend = kf * jnp.exp(last - cum)
            st_scr[hh] = jnp.exp(last) * s_t + _dot_tn(vb, kend.astype(BF16))
            _head(o_ref, hh)[...] = o
            g = _head(g_ref, hh)[...]
            _head(yb_ref, hh)[...] = ((o * _rms(o) * gn_ref[...]) * (g * _sig(g))).astype(BF16)

    HG = H // pair
    return _ride_call(
        body, ride, name="hg_fwd",
        grid=(B, HG, NC),
        in_specs=[spec(2 * hpd, False), spec(3 * hpd, False), spec(4 * hpd, False), spec(5 * hpd, False),
                  pl.BlockSpec((1, pair * HEAD), lambda b, h, j: (0, h)),
                  pl.BlockSpec((1, HEAD), lambda b, h, j: (0, 0))],
        out_specs=[spec(0, False), spec(0, False),
                   pl.BlockSpec((None, pair, None, HEAD, HEAD), lambda b, h, j: (b, h, j, 0, 0)),
                   pl.BlockSpec((None, pair, None, C, C), lambda b, h, j: (b, h, j, 0, 0)), spec(0, False)],
        out_shape=[jax.ShapeDtypeStruct((T, D), BF16), jax.ShapeDtypeStruct((T, D), F32),
                   jax.ShapeDtypeStruct((B, H, NC, HEAD, HEAD), F32),
                   jax.ShapeDtypeStruct((B, H, NC, C, C), BF16), jax.ShapeDtypeStruct((T, D), F32)],
        scratch_shapes=[pltpu.VMEM((pair, HEAD, HEAD), F32), pltpu.VMEM((len(_hg_level_blocks(C)), C, C), F32)],
        semantics=("parallel", "parallel", "arbitrary"),
        operands=(proj, proj, proj, proj, lb, gn))


def _w_full(ref):
    s, r, c = ref.shape
    return ref[...].reshape(s * r, c)


def _out_fwd(ya, yb, proj, x2, w_st, layer):
    T, D = x2.shape
    tm = _tile(T, 512)

    def body(ya_ref, yb_ref, ma_ref, mb_ref, x_ref, w_ref, xm_ref, y_ref):
        y = (_sig(ma_ref[...]) * ya_ref[...] + _sig(mb_ref[...]) * yb_ref[...]).astype(BF16)
        y_ref[...] = y
        xm_ref[...] = x_ref[...] + _dot(y, _w_full(w_ref))

    row = pl.BlockSpec((tm, D), lambda i: (i, 0))
    return pl.pallas_call(
        body, name="out_fwd",
        grid=(T // tm,),
        in_specs=[row, row, pl.BlockSpec((tm, D), lambda i: (i, 6)), pl.BlockSpec((tm, D), lambda i: (i, 7)), row,
                  pl.BlockSpec((N_SHARD, None, D // N_SHARD, D), lambda i: (0, layer, 0, 0))],
        out_specs=[row, row],
        out_shape=[jax.ShapeDtypeStruct((T, D), F32), jax.ShapeDtypeStruct((T, D), BF16)],
        compiler_params=_cp(("parallel",)),
    )(ya, yb, proj, proj, x2, w_st)


def _mlp_fwd(xm, gain, wup_st, wdn_st, layer):
    T, D = xm.shape
    F4 = wup_st.shape[3]
    tm = _tile(T, 1024)

    def body(x_ref, g_ref, wu_ref, wd_ref, xo_ref, up_ref, h_ref):
        @pl.when(pl.program_id(1) == 0)
        def _():
            x = x_ref[...]
            h_ref[...] = (x * _rms(x) * g_ref[...]).astype(BF16)
            xo_ref[...] = x
        up = _dot(h_ref[...], wu_ref[...])
        up_ref[...] = up.astype(BF16)
        act = jnp.maximum(up, 0.0)
        xo_ref[...] += _dot((act * act).astype(BF16), wd_ref[...])

    row = pl.BlockSpec((tm, D), lambda i, s: (i, 0))
    return pl.pallas_call(
        body, name="mlp_fwd",
        grid=(T // tm, N_SHARD),
        in_specs=[row, pl.BlockSpec((1, D), lambda i, s: (0, 0)),
                  pl.BlockSpec((None, None, D, F4), lambda i, s: (s, layer, 0, 0)),
                  pl.BlockSpec((None, None, F4, D), lambda i, s: (s, layer, 0, 0))],
        out_specs=[row, pl.BlockSpec((tm, F4), lambda i, s: (i, s)), row],
        out_shape=[jax.ShapeDtypeStruct((T, D), F32), jax.ShapeDtypeStruct((T, N_SHARD * F4), BF16),
                   jax.ShapeDtypeStruct((T, D), BF16)],
        compiler_params=_cp(("parallel", "arbitrary")),
    )(xm, gain, wup_st, wdn_st)


def _final_loss(x2, gain, tgt):
    T, D = x2.shape
    tm = _tile(T, 512)
    nt = T // tm

    def body(x_ref, g_ref, t_ref, loss_ref, dx_ref, dg_ref):
        x = x_ref[...]
        rs = _rms(x)
        err = x * rs * g_ref[...] - t_ref[...]
        part = 0.5 * jnp.sum(jnp.sum(err * err, axis=-1, keepdims=True) * (1.0 / D), axis=0, keepdims=True)
        loss_ref[...] = jnp.broadcast_to(part, (SUBLANES, 128))
        dx, dg = _rms_bwd(err * (1.0 / D), x, rs, g_ref[...])
        dx_ref[...] = dx
        dg_ref[...] = dg

    row = pl.BlockSpec((tm, D), lambda i: (i, 0))
    return pl.pallas_call(
        body, name="final_loss",
        grid=(nt,),
        in_specs=[row, pl.BlockSpec((1, D), lambda i: (0, 0)), row],
        out_specs=[pl.BlockSpec((None, SUBLANES, 128), lambda i: (i, 0, 0)), row,
                   pl.BlockSpec((None, SUBLANES, D), lambda i: (i, 0, 0))],
        out_shape=[jax.ShapeDtypeStruct((nt, SUBLANES, 128), F32), jax.ShapeDtypeStruct((T, D), F32),
                   jax.ShapeDtypeStruct((nt, SUBLANES, D), F32)],
        compiler_params=_cp(("parallel",)),
    )(x2, gain, tgt)


def _mlp_bwd_x(dx, xm, up, gain, wup_st, wdn_st, layer, ride=None):
    T, D = xm.shape
    F4 = wup_st.shape[3]
    tm = _tile(T, 1024)
    nt = T // tm

    def body(dx_ref, x_ref, up_ref, g_ref, wu_ref, wd_ref, dxm_ref, dup_ref, dg_ref, dxb):
        s = pl.program_id(1)

        @pl.when(s == 0)
        def _():
            dxb[...] = dx_ref[...].astype(BF16)
            dxm_ref[...] = jnp.zeros((tm, D), F32)

        d_act = _dot_nt(dxb[...], wd_ref[...])
        d_up = (d_act * (2.0 * jnp.maximum(up_ref[...].astype(F32), 0.0))).astype(BF16)
        dup_ref[...] = d_up
        dxm_ref[...] += _dot_nt(d_up, wu_ref[...])

        @pl.when(s == N_SHARD - 1)
        def _():
            x = x_ref[...]
            dxn, dg = _rms_bwd(dxm_ref[...], x, _rms(x), g_ref[...])
            dxm_ref[...] = dx_ref[...] + dxn
            dg_ref[...] = dg

    row = pl.BlockSpec((tm, D), lambda i, s: (i, 0))
    return _ride_call(
        body, ride, name="mlp_bwd_x",
        grid=(nt, N_SHARD),
        in_specs=[row, row, pl.BlockSpec((tm, F4), lambda i, s: (i, s)), pl.BlockSpec((1, D), lambda i, s: (0, 0)),
                  pl.BlockSpec((None, None, D, F4), lambda i, s: (s, layer, 0, 0)),
                  pl.BlockSpec((None, None, F4, D), lambda i, s: (s, layer, 0, 0))],
        out_specs=[row, pl.BlockSpec((tm, F4), lambda i, s: (i, s)),
                   pl.BlockSpec((None, SUBLANES, D), lambda i, s: (i, 0, 0)), row],
        out_shape=[jax.ShapeDtypeStruct((T, D), F32), jax.ShapeDtypeStruct((T, N_SHARD * F4), BF16),
                   jax.ShapeDtypeStruct((nt, SUBLANES, D), F32), jax.ShapeDtypeStruct((T, D), BF16)],
        scratch_shapes=[],
        semantics=("parallel", "arbitrary"),
        operands=(dx, xm, up, gain, wup_st, wdn_st))


def _layer_slot(bufs, shapes, n_layers):
    out_shape = [jax.ShapeDtypeStruct((N_SHARD, n_layers) + s, F32) for s in shapes]
    return out_shape, ([] if bufs is None else list(bufs))


def _mlp_bwd_w(up, dxb, h, dup, layer, n_layers, bufs):
    T, D = dxb.shape
    F4 = up.shape[1] // N_SHARD
    tk = _tile(T, 1024)
    out_shape, extra = _layer_slot(bufs, [(D, F4), (F4, D)], n_layers)

    def body(up_ref, dx_ref, h_ref, dup_ref, *rest):
        gu_ref, gd_ref = rest[-2:]

        @pl.when(pl.program_id(1) == 0)
        def _():
            gu_ref[...] = jnp.zeros((D, F4), F32)
            gd_ref[...] = jnp.zeros((F4, D), F32)
        act = jnp.maximum(up_ref[...], 0.0)
        gd_ref[...] += _dot_tn(act * act, dx_ref[...])
        gu_ref[...] += _dot_tn(h_ref[...], dup_ref[...])

    return pl.pallas_call(
        body, name="mlp_bwd_w",
        grid=(N_SHARD, T // tk),
        in_specs=[pl.BlockSpec((tk, F4), lambda s, t: (t, s)), pl.BlockSpec((tk, D), lambda s, t: (t, 0)),
                  pl.BlockSpec((tk, D), lambda s, t: (t, 0)), pl.BlockSpec((tk, F4), lambda s, t: (t, s))]
        + [ANY] * len(extra),
        out_specs=[pl.BlockSpec((None, None, D, F4), lambda s, t: (s, layer, 0, 0)),
                   pl.BlockSpec((None, None, F4, D), lambda s, t: (s, layer, 0, 0))],
        out_shape=out_shape,
        input_output_aliases={4 + i: i for i in range(len(extra))},
        compiler_params=_cp(("parallel", "arbitrary")),
    )(up, dxb, h, dup, *extra)


def _out_bwd_x(dxm, ya, yb, proj, w_st, layer):
    T, D = dxm.shape
    tm = _tile(T, 512)

    def body(dx_ref, ya_ref, yb_ref, ma_ref, mb_ref, w_ref, dya_ref, dyb_ref, dma_ref, dmb_ref):
        dy = _dot_nt(dx_ref[...].astype(BF16), _w_full(w_ref))
        sa = _sig(ma_ref[...])
        sb = _sig(mb_ref[...])
        dya_ref[...] = (dy * sa).astype(BF16)
        dyb_ref[...] = (dy * sb).astype(BF16)
        dma_ref[...] = (dy * ya_ref[...] * (sa * (1.0 - sa))).astype(BF16)
        dmb_ref[...] = (dy * yb_ref[...] * (sb * (1.0 - sb))).astype(BF16)

    row = pl.BlockSpec((tm, D), lambda i: (i, 0))
    return pl.pallas_call(
        body, name="out_bwd_x",
        grid=(T // tm,),
        in_specs=[row, row, row, pl.BlockSpec((tm, D), lambda i: (i, 6)), pl.BlockSpec((tm, D), lambda i: (i, 7)),
                  pl.BlockSpec((N_SHARD, None, D // N_SHARD, D), lambda i: (0, layer, 0, 0))],
        out_specs=[row] * 4,
        out_shape=[jax.ShapeDtypeStruct((T, D), BF16)] * 4,
        compiler_params=_cp(("parallel",)),
    )(dxm, ya, yb, proj, proj, w_st)


def _out_bwd_w(ymix, dxm, layer, n_layers, bufs):
    T, D = dxm.shape
    tk = _tile(T, 1024)
    out_shape, extra = _layer_slot(bufs, [(D // N_SHARD, D)], n_layers)

    def body(y_ref, dx_ref, *rest):
        g_ref = rest[-1]

        @pl.when(pl.program_id(0) == 0)
        def _():
            g_ref[...] = jnp.zeros((N_SHARD, D // N_SHARD, D), F32)
        g = _dot_tn(y_ref[...], dx_ref[...].astype(BF16))
        g_ref[...] += g.reshape(N_SHARD, D // N_SHARD, D)

    row = pl.BlockSpec((tk, D), lambda t: (t, 0))
    return pl.pallas_call(
        body, name="out_bwd_w",
        grid=(T // tk,),
        in_specs=[row, row] + [ANY] * len(extra),
        out_specs=[pl.BlockSpec((N_SHARD, None, D // N_SHARD, D), lambda t: (0, layer, 0, 0))],
        out_shape=out_shape,
        input_output_aliases={2 + i: i for i in range(len(extra))},
        compiler_params=_cp(("arbitrary",)),
    )(ymix, dxm, *extra)[0]


def _rg_bwd(proj, hrg, dya, B, cw, cb, wr, br, wi, bi, sp):
    T = proj.shape[0]
    D = proj.shape[1] // N_SEG
    S = T // B
    ts = _tile(S, RG_TILE)
    nts = S // ts
    nb = D // RG_BLOCK
    t8 = ts // SUBLANES

    def body(xa_ref, xp_ref, ga_ref, h_ref, hp_ref, dya_ref, cw_ref, cb_ref, wr_ref, br_ref, wi_ref, bi_ref, sp_ref,
             dxa_ref, dga_ref, gwr_ref, gwi_ref, gcw_ref, gcb_ref, gbr_ref, gbi_ref, gsp_ref,
             xbuf, hbuf, abuf, dbuf, g_scr, c_scr, gcar):
        b = pl.program_id(0)
        j = pl.program_id(1)
        first_in_time = j == nts - 1

        @pl.when((b == 0) & (j == 0))
        def _():
            gwr_ref[...] = jnp.zeros((nb, RG_BLOCK, RG_BLOCK), F32)
            gwi_ref[...] = jnp.zeros((nb, RG_BLOCK, RG_BLOCK), F32)
            gcw_ref[...] = jnp.zeros((CONV_TAPS, SUBLANES, D), F32)
            for r in (gcb_ref, gbr_ref, gbi_ref, gsp_ref):
                r[...] = jnp.zeros((SUBLANES, D), F32)

        @pl.when(j == 0)
        def _():
            abuf[pl.ds(ts, SUBLANES), :] = jnp.zeros((SUBLANES, D), F32)
            dbuf[pl.ds(ts, SUBLANES), :] = jnp.zeros((SUBLANES, D), F32)
            gcar[...] = jnp.zeros((SUBLANES, D), F32)

        keep = jnp.where(first_in_time, 0.0, 1.0)
        xbuf[0:SUBLANES, :] = xp_ref[...] * keep
        xbuf[pl.ds(SUBLANES, ts), :] = xa_ref[...]
        hbuf[0:SUBLANES, :] = hp_ref[...] * keep
        hbuf[pl.ds(SUBLANES, ts), :] = h_ref[...]

        xc = _conv_taps(xbuf, cw_ref, ts) + cb_ref[...]
        sp = sp_ref[...]
        r, i, a, mult = _rg_gates(xc, wr_ref, br_ref[...], wi_ref, bi_ref[...], sp)
        g_gate, dg_gate = _gelu_and_grad(ga_ref[...])
        dya = dya_ref[...].astype(F32)
        dga_ref[...] = (dya * h_ref[...] * dg_gate).astype(BF16)

        abuf[0:ts, :] = a
        c_scr[...] = abuf[pl.ds(1, ts), :]
        g_scr[...] = dya * g_gate
        row8 = lax.broadcasted_iota(jnp.int32, (SUBLANES, 1), 0)

        def blk(n, gnext):
            off = pl.multiple_of((t8 - 1 - n) * SUBLANES, SUBLANES)
            c8 = c_scr[pl.ds(off, SUBLANES), :]
            d8 = g_scr[pl.ds(off, SUBLANES), :]
            for d in (1, 2, 4):
                m = row8 < SUBLANES - d
                cn = jnp.where(m, pltpu.roll(c8, SUBLANES - d, 0), 1.0)
                dn = jnp.where(m, pltpu.roll(d8, SUBLANES - d, 0), 0.0)
                d8 = d8 + c8 * dn
                c8 = c8 * cn
            g8 = d8 + c8 * gnext
            g_scr[pl.ds(off, SUBLANES), :] = g8
            first = jnp.sum(jnp.where(row8 == 0, g8, 0.0), axis=0, keepdims=True)
            return jnp.broadcast_to(first, (SUBLANES, D))

        gcar[...] = lax.fori_loop(0, t8, blk, gcar[...])
        abuf[pl.ds(ts, SUBLANES), :] = a[0:SUBLANES, :]

        g = g_scr[...]
        hprev = hbuf[pl.ds(SUBLANES - 1, ts), :]
        gx = i * xc
        e2 = a * a
        dla = g * hprev * a - jnp.where(mult > 0.0, g * gx * e2 / jnp.where(mult > 0.0, mult, 1.0), 0.0)
        dgx = g * mult
        dpr = (dla * ((-RG_C) * sp)) * (r * (1.0 - r))
        dpi = (dgx * xc) * (i * (1.0 - i))
        gsp_ref[...] += _rows8(dla * ((-RG_C) * r))
        gbr_ref[...] += _rows8(dpr)
        gbi_ref[...] += _rows8(dpi)
        dprb = dpr.astype(BF16)
        dpib = dpi.astype(BF16)
        xcb = xc.astype(BF16)
        back = []
        for n in range(nb):
            sl = slice(n * RG_BLOCK, (n + 1) * RG_BLOCK)
            back.append(_dot_nt(dprb[:, sl], wr_ref[n]) + _dot_nt(dpib[:, sl], wi_ref[n]))
            gwr_ref[n] += _dot_tn(xcb[:, sl], dprb[:, sl])
            gwi_ref[n] += _dot_tn(xcb[:, sl], dpib[:, sl])
        dxc = dgx * i + (jnp.concatenate(back, axis=1) if nb > 1 else back[0])
        gcb_ref[...] += _rows8(dxc)

        dbuf[0:ts, :] = dxc
        dxa = None
        for jtap in range(CONV_TAPS):
            term = cw_ref[jtap:jtap + 1, :] * dbuf[pl.ds(CONV_TAPS - 1 - jtap, ts), :]
            dxa = term if dxa is None else dxa + term
            gcw_ref[jtap] += _rows8(dxc * xbuf[pl.ds(SUBLANES - (CONV_TAPS - 1) + jtap, ts), :])
        dxa_ref[...] = dxa.astype(BF16)
        dbuf[pl.ds(ts, SUBLANES), :] = dxc[0:SUBLANES, :]

    def tile_map(col):
        return lambda b, j: (b * nts + (nts - 1 - j), col)

    def prev8_map(col):
        return lambda b, j: (jnp.maximum((b * nts + (nts - 1 - j)) * t8 - 1, 0), col)

    vec = pl.BlockSpec((1, D), lambda b, j: (0, 0))
    gate = pl.BlockSpec((nb, RG_BLOCK, RG_BLOCK), lambda b, j: (0, 0, 0))
    acc8 = pl.BlockSpec((SUBLANES, D), lambda b, j: (0, 0))
    return pl.pallas_call(
        body, name="rg_bwd",
        grid=(B, nts),
        in_specs=[pl.BlockSpec((ts, D), tile_map(0)), pl.BlockSpec((SUBLANES, D), prev8_map(0)),
                  pl.BlockSpec((ts, D), tile_map(1)),
                  pl.BlockSpec((ts, D), tile_map(0)), pl.BlockSpec((SUBLANES, D), prev8_map(0)),
                  pl.BlockSpec((ts, D), tile_map(0)),
                  pl.BlockSpec((CONV_TAPS, D), lambda b, j: (0, 0)), vec, gate, vec, gate, vec, vec],
        out_specs=[pl.BlockSpec((ts, D), tile_map(0)), pl.BlockSpec((ts, D), tile_map(0)), gate, gate,
                   pl.BlockSpec((CONV_TAPS, SUBLANES, D), lambda b, j: (0, 0, 0)), acc8, acc8, acc8, acc8],
        out_shape=[jax.ShapeDtypeStruct((T, D), BF16)] * 2
        + [jax.ShapeDtypeStruct((nb, RG_BLOCK, RG_BLOCK), F32)] * 2
        + [jax.ShapeDtypeStruct((CONV_TAPS, SUBLANES, D), F32)] + [jax.ShapeDtypeStruct((SUBLANES, D), F32)] * 4,
        scratch_shapes=[pltpu.VMEM((SUBLANES + ts, D), F32), pltpu.VMEM((SUBLANES + ts, D), F32),
                        pltpu.VMEM((ts + SUBLANES, D), F32), pltpu.VMEM((ts + SUBLANES, D), F32),
                        pltpu.VMEM((ts, D), F32), pltpu.VMEM((ts, D), F32), pltpu.VMEM((SUBLANES, D), F32)],
        compiler_params=_cp(("arbitrary", "arbitrary")),
    )(proj, proj, proj, hrg, hrg, dya, cw, cb, wr, br, wi, bi, sp)


def _hg_bwd(proj, o_sv, dyb, states, a_sv, cum_sv, B, lb, gn, ride=None):
    T = proj.shape[0]
    D = proj.shape[1] // N_SEG
    S = T // B
    C = min(HG_CHUNK, S)
    NC = S // C
    H = D // HEAD
    hpd = D // HEAD
    pair = math.gcd(H, HG_HEADS_PER_STEP)
    spec = _hg_specs(B, NC, D, C, pair)

    def body(q_ref, z_ref, v_ref, g_ref, o_ref, dyb_ref, st_ref, a_ref, cum_ref, lb_ref, gn_ref,
             dq_ref, dz_ref, dv_ref, dg_ref, glb_ref, ggn_ref, ds_scr, mask_scr):
        @pl.when(pl.program_id(2) == 0)
        def _():
            ds_scr[...] = jnp.zeros((pair, HEAD, HEAD), F32)
            _hg_fill_masks(mask_scr, C)
            glb_ref[...] = jnp.zeros((SUBLANES, pair * HEAD), F32)
            ggn_ref[...] = jnp.zeros((pair, SUBLANES, HEAD), F32)

        for hh in range(pair):
            cols = [_head(r, hh) for r in (q_ref, z_ref, v_ref, g_ref, o_ref, dyb_ref)]
            outs = [_head(r, hh) for r in (dq_ref, dz_ref, dv_ref, dg_ref, glb_ref)]
            one_head(*cols, st_ref.at[hh], a_ref.at[hh], _head(cum_ref, hh), _head(lb_ref, hh), gn_ref,
                     *outs, ggn_ref.at[hh], ds_scr.at[hh], mask_scr)

    def one_head(q_ref, z_ref, v_ref, g_ref, o_ref, dyb_ref, st_ref, a_ref, cum_ref, lb_ref, gn_ref,
                 dq_ref, dz_ref, dv_ref, dg_ref, glb_ref, ggn_ref, ds_scr, mask_scr):
        q = q_ref[...]
        lb = lb_ref[...]
        gn = gn_ref[...]
        qf, qs, kf, lf, fg, sig = _hg_gates(q, z_ref[...], lb)
        cum = cum_ref[...]
        levels = _hg_levels(lf, cum, C)

        o = o_ref[...]
        g = g_ref[...]
        gs = _sig(g)
        rs = _rms(o)
        dyb = dyb_ref[...].astype(F32)
        don = dyb * (g * gs)
        dg_ref[...] = (dyb * (o * rs * gn) * (gs * (1.0 + g * (1.0 - gs)))).astype(BF16)
        ggn_ref[...] += _rows8(don * o * rs)
        dn = don * gn
        do = rs * (dn - o * (rs * rs) * jnp.mean(dn * o, axis=-1, keepdims=True))

        s_t = st_ref[...].astype(BF16)
        ds_t = ds_scr[...]
        ds_b = ds_t.astype(BF16)
        dob = do.astype(BF16)
        vb = v_ref[...].astype(BF16)
        ecum = jnp.exp(cum)
        last = jnp.sum(lf, axis=0, keepdims=True)
        eend = jnp.exp(last - cum)
        qhat = (qf * ecum).astype(BF16)
        kend = (kf * eend).astype(BF16)

        dA = _dot_nt(dob, vb)
        dq_inter = _dot(dob, s_t)
        dk_state = _dot(vb, ds_b)
        dqf = dq_inter * ecum
        dkf = dk_state * eend
        g_intra = None
        for n, e in enumerate(levels):
            qw, kw = _hg_operands(qf, kf, e)
            dam = (dA * mask_scr[n]).astype(BF16)
            rq = _dot(dam, kw)
            rk = _dot_tn(dam, qw)
            dqf += rq if e is None else rq * e
            dkf += rk if e is None else rk * e
            gi = qw.astype(F32) * rq - kw.astype(F32) * rk
            g_intra = gi if g_intra is None else g_intra + gi
        dv_ref[...] = (_dot_tn(a_ref[...], dob) + _dot_nt(kend, ds_b)).astype(BF16)
        e_last = jnp.exp(last)
        ds_scr[...] = e_last * ds_t + _dot_tn(dob, qhat)

        ri = lax.broadcasted_iota(jnp.int32, (C, C), 0)
        ci = lax.broadcasted_iota(jnp.int32, (C, C), 1)
        y_state = kend.astype(F32) * dk_state
        dlf = (_dot_01(jnp.where(ci >= ri, 1.0, 0.0).astype(BF16), g_intra + qhat.astype(F32) * dq_inter - y_state)
               + jnp.sum(y_state, axis=0, keepdims=True)
               + jnp.sum(e_last * st_ref[...] * ds_t, axis=0, keepdims=True))
        dfg = jnp.where(fg > F_MIN, dlf / jnp.maximum(fg, F_MIN), 0.0)
        sneg = 1.0 - sig
        diff = dfg - dkf
        dz_ref[...] = ((1.0 - lb) * sig * sneg * diff).astype(BF16)
        glb_ref[...] += _rows8(sneg * diff)
        dq_ref[...] = (dqf * (qs * (1.0 + q * (1.0 - qs)))).astype(BF16)

    return _ride_call(
        body, ride, name="hg_bwd",
        grid=(B, H // pair, NC),
        in_specs=[spec(2 * hpd, True), spec(3 * hpd, True), spec(4 * hpd, True), spec(5 * hpd, True),
                  spec(0, True), spec(0, True),
                  pl.BlockSpec((None, pair, None, HEAD, HEAD), lambda b, h, j: (b, h, NC - 1 - j, 0, 0)),
                  pl.BlockSpec((None, pair, None, C, C), lambda b, h, j: (b, h, NC - 1 - j, 0, 0)), spec(0, True),
                  pl.BlockSpec((1, pair * HEAD), lambda b, h, j: (0, h)),
                  pl.BlockSpec((1, HEAD), lambda b, h, j: (0, 0))],
        out_specs=[spec(0, True)] * 4
        + [pl.BlockSpec((None, SUBLANES, pair * HEAD), lambda b, h, j: (b, 0, h)),
           pl.BlockSpec((None, pair, SUBLANES, HEAD), lambda b, h, j: (b, h, 0, 0))],
        out_shape=[jax.ShapeDtypeStruct((T, D), BF16)] * 4
        + [jax.ShapeDtypeStruct((B, SUBLANES, D), F32), jax.ShapeDtypeStruct((B, H, SUBLANES, HEAD), F32)],
        scratch_shapes=[pltpu.VMEM((pair, HEAD, HEAD), F32), pltpu.VMEM((len(_hg_level_blocks(C)), C, C), F32)],
        semantics=("parallel", "parallel", "arbitrary"),
        operands=(proj, proj, proj, proj, o_sv, dyb, states, a_sv, cum_sv, lb, gn))


def _inproj_bwd_x(dsegs, w_st, layer, x2, gain, dxm, ride=None):
    T, D = x2.shape
    tm = _tile(T, 512)
    nt = T // tm

    def body(*refs):
        seg_refs = refs[:N_SEG]
        w_ref, x_ref, g_ref, dxm_ref, dx_ref, dg_ref = refs[N_SEG:]
        s = pl.program_id(1)

        @pl.when(s == 0)
        def _():
            dx_ref[...] = jnp.zeros((tm, D), F32)

        for ss in range(N_SHARD):
            @pl.when(s == ss)
            def _(ss=ss):
                dx_ref[...] += (_dot_nt(seg_refs[2 * ss][...], w_ref[:, 0:D])
                                + _dot_nt(seg_refs[2 * ss + 1][...], w_ref[:, D:2 * D]))

        @pl.when(s == N_SHARD - 1)
        def _():
            x = x_ref[...]
            dxn, dg = _rms_bwd(dx_ref[...], x, _rms(x), g_ref[...])
            dx_ref[...] = dxm_ref[...] + dxn
            dg_ref[...] = dg

    row = pl.BlockSpec((tm, D), lambda i, s: (i, 0))

    def seg_spec(kk):
        return pl.BlockSpec((tm, D), lambda i, s: (jnp.minimum(i + jnp.where(s > kk // 2, 1, 0), nt - 1), 0))

    return _ride_call(
        body, ride, name="inproj_bwd_x",
        grid=(nt, N_SHARD),
        in_specs=[seg_spec(kk) for kk in range(N_SEG)]
        + [pl.BlockSpec((None, None, D, 2 * D), lambda i, s: (s, layer, 0, 0)), row,
           pl.BlockSpec((1, D), lambda i, s: (0, 0)), row],
        out_specs=[row, pl.BlockSpec((None, SUBLANES, D), lambda i, s: (i, 0, 0))],
        out_shape=[jax.ShapeDtypeStruct((T, D), F32), jax.ShapeDtypeStruct((nt, SUBLANES, D), F32)],
        scratch_shapes=[],
        semantics=("parallel", "arbitrary"),
        operands=(*dsegs, w_st, x2, gain, dxm))


def _inproj_bwd_w(h, dsegs, layer, n_layers, bufs):
    T, D = h.shape
    tk = _tile(T, 1024)
    out_shape, extra = _layer_slot(bufs, [(D, 2 * D)], n_layers)

    def body(*refs):
        h_ref = refs[0]
        seg_refs = refs[1:1 + N_SEG]
        g_ref = refs[-1]
        k = pl.program_id(0)

        @pl.when(pl.program_id(1) == 0)
        def _():
            g_ref[...] = jnp.zeros((D, D), F32)

        for kk in range(N_SEG):
            @pl.when(k == kk)
            def _(kk=kk):
                g_ref[...] += _dot_tn(h_ref[...], seg_refs[kk][...])

    def seg_spec(kk):
        return pl.BlockSpec((tk, D), lambda k, t: (jnp.where(k == kk, t, 0), 0))

    return pl.pallas_call(
        body, name="inproj_bwd_w",
        grid=(N_SEG, T // tk),
        in_specs=[pl.BlockSpec((tk, D), lambda k, t: (t, 0))] + [seg_spec(kk) for kk in range(N_SEG)]
        + [ANY] * len(extra),
        out_specs=[pl.BlockSpec((None, None, D, D), lambda k, t: (k // 2, layer, 0, k % 2))],
        out_shape=out_shape,
        input_output_aliases={1 + N_SEG + i: i for i in range(len(extra))},
        compiler_params=_cp(("parallel", "arbitrary")),
    )(h, *dsegs, *extra)[0]


def _softmax_rows(lg_ref, L):
    rows = [lg_ref[l:l + 1, :] for l in range(L)]
    mx = functools.reduce(jnp.maximum, rows)
    es = [jnp.exp(r - mx) for r in rows]
    den = functools.reduce(lambda p, q: p + q, es)
    return [e / den for e in es]


def _prep(lb_logits, lam):
    L, D = lb_logits.shape

    def body(lg_ref, lam_ref, lowb_ref, sp_ref):
        sm = _softmax_rows(lg_ref, L)
        run = sm[0]
        for l in range(L):
            if l > 0:
                run = run + sm[l]
            lowb_ref[l:l + 1, :] = jnp.clip(run - sm[0], 0.0, 1.0)
        y = -lam_ref[...]
        sp_ref[...] = jnp.maximum(y, 0.0) + jnp.log1p(jnp.exp(-jnp.abs(y)))

    return pl.pallas_call(
        body, name="prep_small",
        out_shape=[jax.ShapeDtypeStruct((L, D), F32)] * 2,
    )(lb_logits, lam)


def _local_step(x, tgt, lowb, sp, norm_mix, wbufs, conv_w, conv_b, w_r, b_r, w_i, b_i, hg_norm,
                norm_mlp, norm_final):
    B, S, D = x.shape
    L = norm_mix.shape[0]
    T = B * S
    x2 = x.reshape(T, D)
    row = lambda a, l: a[l:l + 1]

    def weight_views(bufs):
        f4 = bufs[2].shape[-1]
        return (bufs[0].reshape(N_SHARD, L, D, 2 * D), bufs[1].reshape(N_SHARD, L, D // N_SHARD, D),
                bufs[2].reshape(N_SHARD, L, D, f4), bufs[3].reshape(N_SHARD, L, f4, D))

    w_in_st, w_out_st, w_up_st, w_down_st = weight_views(wbufs)
    saved = []
    for l in range(L):
        proj, h = _inproj_fwd(x2, row(norm_mix, l), w_in_st, l)
        ya, hrg = _rg_fwd(proj, B, conv_w[l], row(conv_b, l), w_r[l], row(b_r, l), w_i[l], row(b_i, l), row(sp, l))
        l0, nl = (1, min(2, L - 1)) if l == 0 else (l + 2, 1 if l + 2 < L else 0)
        if nl > 0:
            yb, o, st, a_sv, cum_sv, *wbufs = _hg_fwd(proj, B, row(lowb, l), row(hg_norm, l),
                                                      _gather_ride(list(wbufs), LINK_SPLIT, l0, nl))
            w_in_st, w_out_st, w_up_st, w_down_st = weight_views(wbufs)
        else:
            yb, o, st, a_sv, cum_sv = _hg_fwd(proj, B, row(lowb, l), row(hg_norm, l))
        xm, ymix = _out_fwd(ya, yb, proj, x2, w_out_st, l)
        xo, up, h2 = _mlp_fwd(xm, row(norm_mlp, l), w_up_st, w_down_st, l)
        saved.append((x2, proj, h, ya, hrg, yb, o, (st, a_sv, cum_sv), xm, ymix, up, h2))
        x2 = xo
    loss_parts, dx, g_nf = _final_loss(x2, norm_final[None, :], tgt.reshape(T, D))

    def reduce_part(g_in, g_out, g_mlp, gate_list):
        gate_list = gate_list[::-1]
        grads = [g_in, g_out, g_mlp[0], g_mlp[1], _shard_gate(jnp.stack([g[0] for g in gate_list])),
                 _shard_gate(jnp.stack([g[1] for g in gate_list]))]
        return _ReduceScatter([g.reshape(N_SHARD, -1, g.shape[-1]) for g in grads], LINK_SPLIT)

    gates = []
    small = []
    g_in = g_out = g_mlp = None
    rest = None
    for l in reversed(range(L)):
        x_in, proj, h, ya, hrg, yb, o, st, xm, ymix, up, h2 = saved[l]
        alone = l == 0 and L > 1
        if alone:
            rest = reduce_part(g_in, g_out, g_mlp, gates)
            g_in = g_out = g_mlp = None
            gates = []
        slot, n_slots = (0, 1) if l == 0 else (l - 1, L - 1)
        dxm, dup, g_nmlp, dxb, *got = _mlp_bwd_x(dx, xm, up, row(norm_mlp, l), w_up_st, w_down_st, l,
                                                 rest.ride_c() if alone else None)
        if alone:
            rest.after_c(got)
        g_mlp = _mlp_bwd_w(up, dxb, h2, dup, slot, n_slots, g_mlp)
        dya, dyb, dma, dmb = _out_bwd_x(dxm, ya, yb, proj, w_out_st, l)
        g_out = _out_bwd_w(ymix, dxm, slot, n_slots, None if g_out is None else [g_out])
        dxa, dga, g_wr, g_wi, g_cw, g_cb, g_br, g_bi, g_sp = _rg_bwd(
            proj, hrg, dya, B, conv_w[l], row(conv_b, l), w_r[l], row(b_r, l), w_i[l], row(b_i, l), row(sp, l))
        dq, dz, dv, dg, g_lb, g_gn, *got = _hg_bwd(proj, o, dyb, *st, B, row(lowb, l), row(hg_norm, l),
                                                   rest.ride_1() if alone else None)
        if alone:
            rest.after_1(got)
        dsegs = (dxa, dga, dq, dz, dv, dg, dma, dmb)
        dx, g_nmix, *got = _inproj_bwd_x(dsegs, w_in_st, l, x_in, row(norm_mix, l), dxm,
                                         rest.ride_2() if alone else None)
        if alone:
            rest.after_2(got)
        g_in = _inproj_bwd_w(h, dsegs, slot, n_slots, None if g_in is None else [g_in])
        gates.append((g_wr, g_wi))
        small.append((g_lb, g_nmix, g_cb, g_br, g_bi, g_sp, g_nmlp, g_gn, g_cw))
    small.reverse()
    first = reduce_part(g_in, g_out, g_mlp, gates)
    first.exchange_c()
    first.exchange_1()
    first.exchange_2()
    parts = [first.finish()] + ([rest.finish()] if rest is not None else [])
    return loss_parts, dx.reshape(B, S, D), parts, small, g_nf


def _me():
    return lax.axis_index("x"), lax.axis_index("y"), lax.axis_index("c")


def _cast_place(w, slot):
    R, N = w.shape
    tr = _tile(R, max(16, (1 << 20) // N))

    def body(slot_ref, w_ref, o_ref):
        o_ref[...] = w_ref[...].astype(BF16)

    return pl.pallas_call(
        body, name="cast_place",
        grid_spec=pltpu.PrefetchScalarGridSpec(
            num_scalar_prefetch=1, grid=(R // tr,),
            in_specs=[pl.BlockSpec((tr, N), lambda i, slot: (i, 0))],
            out_specs=pl.BlockSpec((None, tr, N), lambda i, slot: (slot[0], i, 0))),
        out_shape=jax.ShapeDtypeStruct((N_SHARD, R, N), BF16),
        compiler_params=_cp(("parallel",)),
    )(slot, w)


def _gather_weights(bufs, first_axes, l0, nl):
    n = len(bufs)
    phases = _gather_phases(n, first_axes, l0, nl)

    def body(*refs):
        outs = refs[n:2 * n]
        ssem, rsem = refs[2 * n:]
        for ph in phases:
            ph(outs, ssem, rsem)

    return pl.pallas_call(
        body, name="gather_weights",
        in_specs=[ANY] * n, out_specs=[ANY] * n,
        out_shape=[jax.ShapeDtypeStruct(b.shape, b.dtype) for b in bufs],
        input_output_aliases={a: a for a in range(n)},
        scratch_shapes=_gather_sems(n),
        compiler_params=pltpu.CompilerParams(has_side_effects=True),
    )(*bufs)


def _gather_sems(n):
    return [pltpu.SemaphoreType.DMA((n, 6)), pltpu.SemaphoreType.DMA((n, 6))]


def _gather_phases(n, first_axes, l0, nl):
    def ctx(outs, ssem, rsem):
        x, y, c = _me()

        def piece(a, flips, half):
            sx = 1 - x if flips[0] else x
            sy = 1 - y if flips[1] else y
            return outs[a].at[2 * sx + sy, pl.ds(l0[a], nl[a]), half]

        def rcopy(a, k, ref, dev):
            return pltpu.make_async_remote_copy(src_ref=ref, dst_ref=ref, send_sem=ssem.at[a, k], recv_sem=rsem.at[a, k],
                                                device_id=dev, device_id_type=MESH_ID)

        def route(a):
            fx = first_axes[a] == "x"
            f_dev = (1 - x, y, c) if fx else (x, 1 - y, c)
            g_dev = (x, 1 - y, c) if fx else (1 - x, y, c)
            return f_dev, g_dev, ((1, 0) if fx else (0, 1)), ((0, 1) if fx else (1, 0))

        return c, (x, y, 1 - c), piece, rcopy, route

    def own_halves(outs, ssem, rsem):
        c, sib, piece, rcopy, route = ctx(outs, ssem, rsem)
        for a in range(n):
            f_dev, g_dev, _, _ = route(a)
            own = piece(a, (0, 0), c)
            rcopy(a, 0, own, f_dev).start()
            rcopy(a, 1, own, g_dev).start()

    def pass_on_neighbours(outs, ssem, rsem):
        c, sib, piece, rcopy, route = ctx(outs, ssem, rsem)
        for a in range(n):
            f_dev, g_dev, f_flip, _ = route(a)
            got = piece(a, f_flip, c)
            rcopy(a, 0, got, f_dev).wait_recv()
            rcopy(a, 2, got, g_dev).start()
            rcopy(a, 3, got, sib).start()
        for a in range(n):
            _, g_dev, _, g_flip = route(a)
            got = piece(a, g_flip, c)
            rcopy(a, 1, got, g_dev).wait_recv()
            rcopy(a, 4, got, sib).start()

    def pass_on_diagonal(outs, ssem, rsem):
        c, sib, piece, rcopy, route = ctx(outs, ssem, rsem)
        for a in range(n):
            _, g_dev, _, _ = route(a)
            got = piece(a, (1, 1), c)
            rcopy(a, 2, got, g_dev).wait_recv()
            rcopy(a, 5, got, sib).start()

    def drain(outs, ssem, rsem):
        c, sib, piece, rcopy, route = ctx(outs, ssem, rsem)
        for a in range(n):
            f_dev, g_dev, f_flip, g_flip = route(a)
            for k, fl in ((3, f_flip), (4, g_flip), (5, (1, 1))):
                rcopy(a, k, piece(a, fl, 1 - c), sib).wait_recv()
            own = piece(a, (0, 0), c)
            for k, dev in ((0, f_dev), (1, g_dev), (2, g_dev), (3, sib), (4, sib), (5, sib)):
                rcopy(a, k, own, dev).wait_send()

    return [own_halves, pass_on_neighbours, pass_on_diagonal, drain]


def _gather_ride(bufs, first_axes, l0, nl):
    n = len(bufs)
    phases = [lambda reads, refs, outs, ssem, rsem, ph=ph: ph(refs, ssem, rsem)
              for ph in _gather_phases(n, first_axes, [l0] * n, [nl] * n)]
    return _Ride([], bufs, [], phases, GATHER_STEPS, _gather_sems(n))


def _shard_gate(g):
    n_l, nb = g.shape[:2]
    return g.reshape(n_l, nb, N_SHARD, RG_BLOCK // N_SHARD, RG_BLOCK).transpose(2, 0, 1, 3, 4)


def _exchange(arrs, axes, name):
    n = len(arrs)

    def body(*refs):
        ins, outs = refs[:n], refs[n:2 * n]
        ssem, rsem = refs[2 * n:]
        x, y, c = _me()
        cps = []
        for a in range(n):
            my = {"x": x, "y": y, "c": c}[axes[a]]
            partner = {"x": (1 - x, y, c), "y": (x, 1 - y, c), "c": (x, y, 1 - c)}[axes[a]]
            cps.append(pltpu.make_async_remote_copy(
                src_ref=ins[a].at[:, 1 - my], dst_ref=outs[a], send_sem=ssem.at[a], recv_sem=rsem.at[a],
                device_id=partner, device_id_type=MESH_ID))
            cps[-1].start()
        for cp in cps:
            cp.wait()

    return pl.pallas_call(
        body, name=name,
        in_specs=[ANY] * n, out_specs=[ANY] * n,
        out_shape=[jax.ShapeDtypeStruct((a.shape[0],) + a.shape[2:], a.dtype) for a in arrs],
        scratch_shapes=[pltpu.SemaphoreType.DMA((n,)), pltpu.SemaphoreType.DMA((n,))],
        compiler_params=pltpu.CompilerParams(has_side_effects=True),
    )(*arrs)


def _add_kept(arr, got, idx, name, with_bf16):
    P, _, R, N = arr.shape
    tr = _tile(R, max(16, (1 << 20) // N))

    def body(idx_ref, a_ref, g_ref, o_ref, *ob_ref):
        s = a_ref[...] + g_ref[...].astype(F32)
        o_ref[...] = s
        if with_bf16:
            ob_ref[0][...] = s.astype(BF16)

    out_blk = pl.BlockSpec((None, tr, N), lambda p, i, idx: (p, i, 0))
    return pl.pallas_call(
        body, name=name,
        grid_spec=pltpu.PrefetchScalarGridSpec(
            num_scalar_prefetch=1, grid=(P, R // tr),
            in_specs=[pl.BlockSpec((None, None, tr, N), lambda p, i, idx: (p, idx[0], i, 0)),
                      pl.BlockSpec((None, tr, N), lambda p, i, idx: (p, i, 0))],
            out_specs=[out_blk] * (2 if with_bf16 else 1)),
        out_shape=[jax.ShapeDtypeStruct((P, R, N), F32)] + ([jax.ShapeDtypeStruct((P, R, N), BF16)] if with_bf16 else []),
        compiler_params=_cp(("parallel", "parallel")),
    )(idx, arr, got)


def _share_halves(halves):
    n = len(halves)

    def body(*refs):
        ins, outs = refs[:n], refs[n:2 * n]
        ssem, rsem = refs[2 * n:]
        x, y, c = _me()
        cps = []
        for a in range(n):
            cps.append(pltpu.make_async_remote_copy(
                src_ref=ins[a], dst_ref=outs[a], send_sem=ssem.at[a], recv_sem=rsem.at[a],
                device_id=(x, y, 1 - c), device_id_type=MESH_ID))
            cps[-1].start()
        for cp in cps:
            cp.wait()

    return pl.pallas_call(
        body, name="share_halves",
        in_specs=[ANY] * n, out_specs=[ANY] * n,
        out_shape=[jax.ShapeDtypeStruct(h.shape, h.dtype) for h in halves],
        scratch_shapes=[pltpu.SemaphoreType.DMA((n,)), pltpu.SemaphoreType.DMA((n,))],
        compiler_params=pltpu.CompilerParams(has_side_effects=True),
    )(*halves)


def _exchange_ride(arrs, axes):
    n = len(arrs)

    def copies(reads, outs, ssem, rsem):
        x, y, c = _me()
        cps = []
        for a in range(n):
            my = {"x": x, "y": y, "c": c}[axes[a]]
            partner = {"x": (1 - x, y, c), "y": (x, 1 - y, c), "c": (x, y, 1 - c)}[axes[a]]
            cps.append(pltpu.make_async_remote_copy(
                src_ref=reads[a].at[:, 1 - my], dst_ref=outs[a], send_sem=ssem.at[a], recv_sem=rsem.at[a],
                device_id=partner, device_id_type=MESH_ID))
        return cps

    def start(reads, bufs, outs, ssem, rsem):
        for cp in copies(reads, outs, ssem, rsem):
            cp.start()

    def finish(reads, bufs, outs, ssem, rsem):
        for cp in copies(reads, outs, ssem, rsem):
            cp.wait()

    landing = [jax.ShapeDtypeStruct((a.shape[0],) + a.shape[2:], a.dtype) for a in arrs]
    return _Ride(arrs, [], landing, [start, finish], (0.0, 1.0),
                 [pltpu.SemaphoreType.DMA((n,)), pltpu.SemaphoreType.DMA((n,))])


class _ReduceScatter:
    def __init__(self, grads, first_axes):
        x, y, c = _me()
        idx = lambda v: jnp.reshape(v, (1,)).astype(jnp.int32)
        self.coord = {"x": idx(x), "y": idx(y), "c": idx(c)}
        self.first = list(first_axes)
        self.second = ["y" if f == "x" else "x" for f in first_axes]
        self.views_c = [g.reshape(N_SHARD, 2, g.shape[1] // 2, g.shape[2]) for g in grads]

    def ride_c(self):
        return _exchange_ride(self.views_c, "c" * len(self.views_c))

    def exchange_c(self):
        self.after_c(_exchange(self.views_c, "c" * len(self.views_c), "rs_exchange_c"))

    def after_c(self, got):
        self.summed = [_add_kept(v, r, self.coord["c"], "rs_add_c", True) for v, r in zip(self.views_c, got)]

    @staticmethod
    def _split_view(a, ax):
        _, rh, nn = a.shape
        return a.reshape(1, 2, 2 * rh, nn) if ax == "x" else a.reshape(2, 2, rh, nn)

    def _views_1(self):
        return [self._split_view(s[1], f) for s, f in zip(self.summed, self.first)]

    def ride_1(self):
        return _exchange_ride(self._views_1(), self.first)

    def exchange_1(self):
        self.after_1(_exchange(self._views_1(), self.first, "rs_exchange_1"))

    def after_1(self, got):
        self.summed = [_add_kept(self._split_view(s[0], f), r, self.coord[f], "rs_add_1", True)
                       for s, r, f in zip(self.summed, got, self.first)]

    def _views_2(self):
        return [s[1].reshape(1, 2, -1, s[1].shape[-1]) for s in self.summed]

    def ride_2(self):
        return _exchange_ride(self._views_2(), self.second)

    def exchange_2(self):
        self.after_2(_exchange(self._views_2(), self.second, "rs_exchange_2"))

    def after_2(self, got):
        views32 = [s[0].reshape(1, 2, -1, s[0].shape[-1]) for s in self.summed]
        self.kept = [_add_kept(v, r, self.coord[g], "rs_add_2", False)[0][0]
                     for v, r, g in zip(views32, got, self.second)]

    def finish(self):
        return self.kept, _share_halves(self.kept)


def _allgather_small(p):
    R, D = p.shape

    def body(p_ref, o_ref, ssem, rsem):
        x, y, c = _me()
        me = 4 * x + 2 * y + c
        o_ref[me] = p_ref[...]
        cps = []
        for m in range(1, 8):
            mx, my, mc = (m >> 2) & 1, (m >> 1) & 1, m & 1
            peer = (1 - x if mx else x, 1 - y if my else y, 1 - c if mc else c)
            cps.append(pltpu.make_async_remote_copy(
                src_ref=p_ref, dst_ref=o_ref.at[me], send_sem=ssem.at[m - 1], recv_sem=rsem.at[m - 1],
                device_id=peer, device_id_type=MESH_ID))
            cps[-1].start()
        for cp in cps:
            cp.wait()

    return pl.pallas_call(
        body, name="allgather_small",
        in_specs=[pl.BlockSpec(memory_space=pltpu.VMEM)],
        out_specs=pl.BlockSpec(memory_space=pltpu.VMEM),
        out_shape=jax.ShapeDtypeStruct((8, R, D), p.dtype),
        scratch_shapes=[pltpu.SemaphoreType.DMA((7,)), pltpu.SemaphoreType.DMA((7,))],
        compiler_params=pltpu.CompilerParams(has_side_effects=True, vmem_limit_bytes=VMEM_LIMIT),
    )(p)


def _adam_math(w, g, m, v):
    m = ADAM_B1 * m + (1.0 - ADAM_B1) * g
    v = ADAM_B2 * v + (1.0 - ADAM_B2) * (g * g)
    m_hat = m / (1.0 - ADAM_B1 ** ADAM_STEP)
    v_hat = v / (1.0 - ADAM_B2 ** ADAM_STEP)
    delta = -ADAM_LR * (m_hat / (jnp.sqrt(v_hat) + ADAM_EPS) + ADAM_WD * w)
    return delta, m, v


def _adam(w, g_mine, g_sib, m, v, core, row0, layer_rows, outs):
    R, N = w.shape
    rh = g_mine.shape[0]
    tr = _tile(layer_rows // 2, max(16, (1 << 19) // N))
    nt = rh // tr
    t0 = row0 // tr
    extra = [] if outs is None else list(outs)

    def body(c_ref, w_ref, gm_ref, gs_ref, m_ref, v_ref, *rest):
        g_ref, d_ref, nm_ref, nv_ref = rest[-4:]
        g = jnp.where(pl.program_id(0) == c_ref[0], gm_ref[...], gs_ref[...])
        d, nm, nv = _adam_math(w_ref[...], g, m_ref[...], v_ref[...])
        g_ref[...] = g
        d_ref[...] = d
        nm_ref[...] = nm
        nv_ref[...] = nv

    blk = pl.BlockSpec((tr, N), lambda h, i, c: (t0 + h * nt + i, 0))
    mine = pl.BlockSpec((tr, N), lambda h, i, c: (jnp.where(h == c[0], i, 0), 0))
    sib = pl.BlockSpec((tr, N), lambda h, i, c: (jnp.where(h == c[0], 0, i), 0))
    return pl.pallas_call(
        body, name="adamw",
        grid_spec=pltpu.PrefetchScalarGridSpec(
            num_scalar_prefetch=1, grid=(2, nt),
            in_specs=[blk, mine, sib, blk, blk] + [ANY] * len(extra), out_specs=[blk] * 4),
        out_shape=[jax.ShapeDtypeStruct((R, N), F32)] * 4,
        input_output_aliases={6 + i: i for i in range(len(extra))},
        compiler_params=_cp(("parallel", "parallel")),
    )(core, w, g_mine, g_sib, m, v, *extra)


def _reduce_rows(parts, sizes, rows_out):
    D = parts.shape[1]

    def body(p_ref, o_ref):
        o_ref[...] = jnp.zeros((rows_out, D), F32)
        off = 0
        for i, sz in enumerate(sizes):
            o_ref[i:i + 1, :] = jnp.sum(p_ref[off:off + sz, :], axis=0, keepdims=True)
            off += sz

    return pl.pallas_call(
        body, name="reduce_rows",
        out_shape=jax.ShapeDtypeStruct((rows_out, D), F32),
        compiler_params=pltpu.CompilerParams(vmem_limit_bytes=VMEM_LIMIT),
    )(parts)


def _sum_devices(g8):
    _, R, D = g8.shape

    def body(g_ref, o_ref):
        tot = g_ref[0]
        for k in range(1, 8):
            tot = tot + g_ref[k]
        o_ref[...] = tot

    return pl.pallas_call(
        body, name="sum_devices",
        out_shape=jax.ShapeDtypeStruct((R, D), F32),
        compiler_params=pltpu.CompilerParams(vmem_limit_bytes=VMEM_LIMIT),
    )(g8)


def _small_update(gathered, w, m, v, L):
    _, R, D = gathered.shape

    def body(g8_ref, w_ref, m_ref, v_ref, g_ref, d_ref, nm_ref, nv_ref):
        tot = g8_ref[0]
        for k in range(1, 8):
            tot = tot + g8_ref[k]
        g_ref[...] = tot
        sm = _softmax_rows(w_ref, L)
        run = sm[0]
        dcum = []
        for l in range(L):
            if l > 0:
                run = run + sm[l]
            cum = run - sm[0]
            dcum.append(jnp.where((cum > 0.0) & (cum < 1.0), g_ref[l:l + 1, :], 0.0))
        dsm = [jnp.zeros((1, D), F32)]
        for i in range(1, L):
            dsm.append(functools.reduce(lambda p, q: p + q, dcum[i:]))
        dot = functools.reduce(lambda p, q: p + q, [s * d for s, d in zip(sm, dsm)])
        for l in range(L):
            g_ref[l:l + 1, :] = sm[l] * (dsm[l] - dot)
        lam = w_ref[5 * L:6 * L, :]
        g_ref[5 * L:6 * L, :] = g_ref[5 * L:6 * L, :] * (-_sig(-lam))
        d, nm, nv = _adam_math(w_ref[...], g_ref[...], m_ref[...], v_ref[...])
        d_ref[...] = d
        nm_ref[...] = nm
        nv_ref[...] = nv

    return pl.pallas_call(
        body, name="small_update",
        out_shape=[jax.ShapeDtypeStruct((R, D), F32)] * 4,
        compiler_params=pltpu.CompilerParams(vmem_limit_bytes=VMEM_LIMIT),
    )(gathered, w, m, v)


def kernel(x, lb_logits, norm_mix, w_in, conv_w, conv_b, w_r, b_r, w_i, b_i, lam, hg_norm, w_out, norm_mlp, w_up, w_down, norm_final, loss_target, m_lb_logits, m_norm_mix, m_w_in, m_conv_w, m_conv_b, m_w_r, m_b_r, m_w_i, m_b_i, m_lam, m_hg_norm, m_w_out, m_norm_mlp, m_w_up, m_w_down, m_norm_final, v_lb_logits, v_norm_mix, v_w_in, v_conv_w, v_conv_b, v_w_r, v_b_r, v_w_i, v_b_i, v_lam, v_hg_norm, v_w_out, v_norm_mlp, v_w_up, v_w_down, v_norm_final):
    B, S, D = x.shape
    L = norm_mix.shape[0]
    nb = D // RG_BLOCK
    Dq = D // N_SHARD
    mx, my, mc = _me()
    shard = 2 * mx + my

    big_w = (w_in, w_out, w_up, w_down, w_r, w_i)
    flat2 = lambda a: a.reshape(-1, a.shape[-1])
    slot = jnp.reshape(shard, (1,)).astype(jnp.int32)
    def place(w):
        b = _cast_place(flat2(w), slot)
        return b.reshape(N_SHARD, L, 2, b.shape[1] // (2 * L), b.shape[2])

    *wbufs, g_r, g_i = _gather_weights([place(w) for w in big_w], LINK_SPLIT, [0] * 6, [1] * 4 + [L] * 2)
    unshard_gate = lambda g: g.reshape(N_SHARD, L, nb, RG_BLOCK // N_SHARD, RG_BLOCK).transpose(1, 2, 0, 3, 4).reshape(
        L, nb, RG_BLOCK, RG_BLOCK)
    w_r_full, w_i_full = unshard_gate(g_r), unshard_gate(g_i)

    R_LB, R_NMIX, R_CB, R_BR, R_BI, R_LAM, R_NMLP, R_GN, R_CW, R_NF, R_LOSS = (
        0, L, 2 * L, 3 * L, 4 * L, 5 * L, 6 * L, 7 * L, 8 * L, 12 * L, 12 * L + 1)
    n_rows = 12 * L + 2
    rows_pad = n_rows + (-n_rows) % SUBLANES

    def place_cols(a):
        return lax.dynamic_update_slice(jnp.zeros((a.shape[0], D), F32), a, (0, shard * Dq))

    def pack_small(lb_, nmix_, cb_, br_, bi_, lam_, nmlp_, gn_, cw_, nf_):
        gn_pad = jnp.pad(gn_, ((0, 0), (0, D - HEAD)))
        rows = [lb_, nmix_, cb_, br_, bi_, lam_, nmlp_, gn_pad, place_cols(cw_.reshape(L * CONV_TAPS, Dq)),
                nf_[None, :], jnp.zeros((rows_pad - n_rows + 1, D), F32)]
        return jnp.concatenate(rows, axis=0)

    w_small = pack_small(lb_logits, norm_mix, conv_b, b_r, b_i, lam, norm_mlp, hg_norm, conv_w, norm_final)
    m_small = pack_small(m_lb_logits, m_norm_mix, m_conv_b, m_b_r, m_b_i, m_lam, m_norm_mlp, m_hg_norm, m_conv_w,
                         m_norm_final)
    v_small = pack_small(v_lb_logits, v_norm_mix, v_conv_b, v_b_r, v_b_i, v_lam, v_norm_mlp, v_hg_norm, v_conv_w,
                         v_norm_final)
    cw_rows = place_cols(conv_w.reshape(L * CONV_TAPS, Dq)) * jnp.where(mc == 0, 1.0, 0.0)
    conv_w_full = _sum_devices(_allgather_small(cw_rows)).reshape(L, CONV_TAPS, D)

    lowb, sp = _prep(lb_logits, lam)

    loss_parts, grad_x, parts, small, g_nf = _local_step(
        x, loss_target, lowb, sp, norm_mix, wbufs, conv_w_full, conv_b, w_r_full, b_r, w_i_full, b_i, hg_norm,
        norm_mlp, norm_final)

    core = jnp.reshape(mc, (1,)).astype(jnp.int32)
    outs = {}
    for a, (name, w, m, v) in enumerate(zip(("w_in", "w_out", "w_up", "w_down", "w_r", "w_i"), big_w,
                                            (m_w_in, m_w_out, m_w_up, m_w_down, m_w_r, m_w_i),
                                            (v_w_in, v_w_out, v_w_up, v_w_down, v_w_r, v_w_i))):
        layer_rows = flat2(w).shape[0] // L
        done, row0 = None, 0
        for mine, sibs in parts:
            done = _adam(flat2(w), mine[a], sibs[a], flat2(m), flat2(v), core, row0, layer_rows, done)
            row0 += 2 * mine[a].shape[0]
        outs[name] = tuple(t.reshape(w.shape) for t in done)

    parts, sizes = [], []

    def add_rows(a):
        a = a.reshape(-1, a.shape[-1])
        if a.shape[1] != D:
            a = jnp.pad(a, ((0, 0), (0, D - a.shape[1])))
        parts.append(a)
        sizes.append(a.shape[0])

    for i in range(8):
        for l in range(L):
            add_rows(small[l][i])
    for l in range(L):
        for j in range(CONV_TAPS):
            add_rows(small[l][8][j])
    add_rows(g_nf)
    loss_rows = loss_parts[:, 0:1, :]
    add_rows(jnp.where(lax.broadcasted_iota(jnp.int32, loss_rows.shape, 2) == 0, loss_rows, 0.0))
    g_small = _reduce_rows(jnp.concatenate(parts, axis=0), sizes, rows_pad)
    g_small, d_small, nm_small, nv_small = _small_update(_allgather_small(g_small), w_small, m_small, v_small, L)

    def unpack(t):
        take_cols = lambda a: lax.dynamic_slice(a, (0, shard * Dq), (a.shape[0], Dq))
        return {"lb_logits": t[R_LB:R_LB + L], "norm_mix": t[R_NMIX:R_NMIX + L], "conv_b": t[R_CB:R_CB + L],
                "b_r": t[R_BR:R_BR + L], "b_i": t[R_BI:R_BI + L], "lam": t[R_LAM:R_LAM + L],
                "norm_mlp": t[R_NMLP:R_NMLP + L], "hg_norm": t[R_GN:R_GN + L, :HEAD],
                "conv_w": take_cols(t[R_CW:R_CW + L * CONV_TAPS]).reshape(L, CONV_TAPS, Dq), "norm_final": t[R_NF]}

    small_out = [unpack(t) for t in (g_small, d_small, nm_small, nv_small)]
    loss = g_small[R_LOSS, 0]
    names = ("lb_logits", "norm_mix", "w_in", "conv_w", "conv_b", "w_r", "b_r", "w_i", "b_i", "lam", "hg_norm",
             "w_out", "norm_mlp", "w_up", "w_down", "norm_final")
    result = [loss, grad_x]
    for kind in range(4):
        for nme in names:
            result.append(outs[nme][kind] if nme in outs else small_out[kind][nme])
    return tuple(result)
```

```python
import functools
import math

import jax
import jax.numpy as jnp
from jax import lax
from jax.experimental import pallas as pl
from jax.experimental.pallas import tpu as pltpu

F32 = jnp.float32
BF16 = jnp.bfloat16

HEAD = 128
RG_BLOCK = 256
CONV_TAPS = 4
RG_C = 8.0
F_MIN = 1e-30
NORM_EPS = 1e-6
N_SEG = 8
N_SHARD = 4
HG_CHUNK = 256
HG_HEADS_PER_STEP = 8
RG_TILE = 256
ADAM_LR, ADAM_B1, ADAM_B2, ADAM_EPS, ADAM_WD, ADAM_STEP = 0.001, 0.9, 0.999, 1e-08, 0.01, 10
V7X_VMEM_BYTES = 64 * 1024 * 1024
VMEM_LIMIT = V7X_VMEM_BYTES - 8 * 1024 * 1024
SUBLANES = 8
LINK_SPLIT = "xxyyyy"
GATHER_STEPS = (0.0, 0.6, 0.88, 1.0)
MESH_ID = pl.DeviceIdType.MESH
ANY = pl.BlockSpec(memory_space=pl.ANY)


def _cp(sem):
    return pltpu.CompilerParams(dimension_semantics=sem, vmem_limit_bytes=VMEM_LIMIT)


def _dot(a, b):
    return jnp.dot(a, b, preferred_element_type=F32)


def _dot_nt(a, b):
    return lax.dot_general(a, b, (((1,), (1,)), ((), ())), preferred_element_type=F32)


def _dot_tn(a, b):
    return lax.dot_general(a, b, (((0,), (0,)), ((), ())), preferred_element_type=F32)


def _dot_01(m01, x):
    n = x.shape[1]
    hi = x.astype(BF16)
    r1 = x - hi.astype(F32)
    mid = r1.astype(BF16)
    lo = (r1 - mid.astype(F32)).astype(BF16)
    y = _dot(m01, jnp.concatenate([hi, mid, lo], axis=1))
    return y[:, :n] + y[:, n:2 * n] + y[:, 2 * n:]


def _sig(x):
    return jax.nn.sigmoid(x)


def _rows8(x):
    return x.reshape(x.shape[0] // SUBLANES, SUBLANES, x.shape[1]).sum(axis=0)


def _tile(n, cap):
    if n <= cap:
        return n
    t = cap - cap % 16
    while n % t:
        t -= 16
    return t


_GELU_C = math.sqrt(2.0 / math.pi)


def _gelu_and_grad(x):
    x2 = x * x
    t = jnp.tanh(_GELU_C * (x + 0.044715 * x * x2))
    g = 0.5 * x * (1.0 + t)
    dg = 0.5 * (1.0 + t) + 0.5 * x * (1.0 - t * t) * (_GELU_C * (1.0 + 3.0 * 0.044715 * x2))
    return g, dg


def _rms(x):
    return lax.rsqrt(jnp.mean(x * x, axis=-1, keepdims=True) + NORM_EPS)


def _rms_bwd(dh, x, rs, gain):
    xhat = x * rs
    dxhat = dh * gain
    dx = rs * (dxhat - xhat * jnp.mean(dxhat * xhat, axis=-1, keepdims=True))
    return dx, _rows8(dh * xhat)


def _inproj_fwd(x2, gain, w_st, layer):
    T, D = x2.shape
    tm = _tile(T, 2048)

    def body(x_ref, g_ref, w_ref, o_ref, h_ref):
        @pl.when(pl.program_id(1) == 0)
        def _():
            x = x_ref[...]
            h_ref[...] = (x * _rms(x) * g_ref[...]).astype(BF16)
        o_ref[...] = _dot(h_ref[...], w_ref[...])

    return pl.pallas_call(
        body, name="inproj_fwd",
        grid=(T // tm, N_SEG),
        in_specs=[pl.BlockSpec((tm, D), lambda i, k: (i, 0)),
                  pl.BlockSpec((1, D), lambda i, k: (0, 0)),
                  pl.BlockSpec((None, None, D, D), lambda i, k: (k // 2, layer, 0, k % 2))],
        out_specs=[pl.BlockSpec((tm, D), lambda i, k: (i, k)),
                   pl.BlockSpec((tm, D), lambda i, k: (i, 0))],
        out_shape=[jax.ShapeDtypeStruct((T, N_SEG * D), F32), jax.ShapeDtypeStruct((T, D), BF16)],
        compiler_params=_cp(("parallel", "arbitrary")),
    )(x2, gain, w_st)


def _rg_gates(xc, wr_ref, br, wi_ref, bi, sp):
    D = xc.shape[1]
    xcb = xc.astype(BF16)
    pr, pi = [], []
    for n in range(D // RG_BLOCK):
        blk = xcb[:, n * RG_BLOCK:(n + 1) * RG_BLOCK]
        pr.append(_dot(blk, wr_ref[n]))
        pi.append(_dot(blk, wi_ref[n]))
    r = _sig(jnp.concatenate(pr, axis=1) + br) if len(pr) > 1 else _sig(pr[0] + br)
    i = _sig(jnp.concatenate(pi, axis=1) + bi) if len(pi) > 1 else _sig(pi[0] + bi)
    la = (-RG_C) * r * sp
    a = jnp.exp(la)
    y = 2.0 * la
    one_m_e2 = jnp.where(y > -1e-2, -(y * (1.0 + 0.5 * y * (1.0 + y * (1.0 / 3.0)))), 1.0 - a * a)
    mult = jnp.sqrt(jnp.maximum(one_m_e2, 0.0))
    return r, i, a, mult


def _conv_taps(xbuf, cw_ref, ts):
    acc = None
    for j in range(CONV_TAPS):
        term = cw_ref[j:j + 1, :] * xbuf[pl.ds(SUBLANES - (CONV_TAPS - 1) + j, ts), :]
        acc = term if acc is None else acc + term
    return acc


def _rg_fwd(proj, B, cw, cb, wr, br, wi, bi, sp):
    T = proj.shape[0]
    D = proj.shape[1] // N_SEG
    S = T // B
    ts = _tile(S, RG_TILE)
    nts = S // ts
    nb = D // RG_BLOCK

    def body(xa_ref, ga_ref, cw_ref, cb_ref, wr_ref, br_ref, wi_ref, bi_ref, sp_ref,
             ya_ref, h_ref, xbuf, a_scr, u_scr, carry):
        @pl.when(pl.program_id(1) == 0)
        def _():
            xbuf[0:SUBLANES, :] = jnp.zeros((SUBLANES, D), F32)
            carry[...] = jnp.zeros((SUBLANES, D), F32)

        xbuf[pl.ds(SUBLANES, ts), :] = xa_ref[...]
        xc = _conv_taps(xbuf, cw_ref, ts) + cb_ref[...]
        r, i, a, mult = _rg_gates(xc, wr_ref, br_ref[...], wi_ref, bi_ref[...], sp_ref[...])
        a_scr[...] = a
        u_scr[...] = mult * (i * xc)
        row8 = lax.broadcasted_iota(jnp.int32, (SUBLANES, 1), 0)

        def blk(n, hprev):
            off = pl.multiple_of(n * SUBLANES, SUBLANES)
            a8 = a_scr[pl.ds(off, SUBLANES), :]
            u8 = u_scr[pl.ds(off, SUBLANES), :]
            for d in (1, 2, 4):
                m = row8 >= d
                ap = jnp.where(m, pltpu.roll(a8, d, 0), 1.0)
                up = jnp.where(m, pltpu.roll(u8, d, 0), 0.0)
                u8 = a8 * up + u8
                a8 = a8 * ap
            h8 = u8 + a8 * hprev
            u_scr[pl.ds(off, SUBLANES), :] = h8
            last = jnp.sum(jnp.where(row8 == SUBLANES - 1, h8, 0.0), axis=0, keepdims=True)
            return jnp.broadcast_to(last, (SUBLANES, D))

        carry[...] = lax.fori_loop(0, ts // SUBLANES, blk, carry[...])
        h = u_scr[...]
        h_ref[...] = h
        g, _ = _gelu_and_grad(ga_ref[...])
        ya_ref[...] = (h * g).astype(BF16)
        xbuf[0:SUBLANES, :] = xa_ref[pl.ds(ts - SUBLANES, SUBLANES), :]

    vec = pl.BlockSpec((1, D), lambda b, j: (0, 0))
    gate = pl.BlockSpec((nb, RG_BLOCK, RG_BLOCK), lambda b, j: (0, 0, 0))
    return pl.pallas_call(
        body, name="rg_fwd",
        grid=(B, nts),
        in_specs=[pl.BlockSpec((ts, D), lambda b, j: (b * nts + j, 0)),
                  pl.BlockSpec((ts, D), lambda b, j: (b * nts + j, 1)),
                  pl.BlockSpec((CONV_TAPS, D), lambda b, j: (0, 0)), vec, gate, vec, gate, vec, vec],
        out_specs=[pl.BlockSpec((ts, D), lambda b, j: (b * nts + j, 0))] * 2,
        out_shape=[jax.ShapeDtypeStruct((T, D), BF16), jax.ShapeDtypeStruct((T, D), F32)],
        scratch_shapes=[pltpu.VMEM((SUBLANES + ts, D), F32), pltpu.VMEM((ts, D), F32),
                        pltpu.VMEM((ts, D), F32), pltpu.VMEM((SUBLANES, D), F32)],
        compiler_params=_cp(("arbitrary", "arbitrary")),
    )(proj, proj, cw, cb, wr, br, wi, bi, sp)


def _hg_gates(q, z, lb):
    sig = _sig(z)
    one_m = 1.0 - lb
    fg = lb + one_m * sig
    lf = jnp.log(jnp.maximum(fg, F_MIN))
    kf = one_m * (1.0 - sig)
    qs = _sig(q)
    return q * qs, qs, kf, lf, fg, sig


def _hg_cum(lf, C):
    ri = lax.broadcasted_iota(jnp.int32, (C, C), 0)
    ci = lax.broadcasted_iota(jnp.int32, (C, C), 1)
    return _dot_01(jnp.where(ci <= ri, 1.0, 0.0).astype(BF16), lf)


def _hg_levels(lf, cum, C):
    row = lax.broadcasted_iota(jnp.int32, (C, 1), 0)
    levels = []
    w = C // 2
    while w >= 4:
        blk = 2 * w
        upper = (row & w) != 0
        ref = jnp.min(jnp.where(upper, 0.0, cum).reshape(C // blk, blk, HEAD), axis=1, keepdims=True)
        ref = jnp.broadcast_to(ref, (C // blk, blk, HEAD)).reshape(C, HEAD)
        d = cum - ref
        levels.append(jnp.exp(jnp.where(upper, d, -d)))
        w //= 2
    r4 = row & 3
    lf_prev = pltpu.roll(lf, 1, 0)
    lf_next = pltpu.roll(lf, C - 1, 0)
    levels.append(jnp.exp(jnp.where(r4 == 3, lf + lf_prev, jnp.where(r4 == 2, lf, jnp.where(r4 == 0, lf_next, 0.0)))))
    levels.append(jnp.exp(jnp.where((row & 1) == 1, lf, 0.0)))
    levels.append(None)
    return levels


def _hg_level_blocks(C):
    blks = []
    w = C // 2
    while w >= 4:
        blks.append(2 * w)
        w //= 2
    return blks + [4, 2, 1]


def _hg_fill_masks(mask_scr, C):
    ri = lax.broadcasted_iota(jnp.int32, (C, C), 0)
    ci = lax.broadcasted_iota(jnp.int32, (C, C), 1)
    for i, blk in enumerate(_hg_level_blocks(C)):
        if blk == 1:
            keep = ri == ci
        else:
            shift, w = blk.bit_length() - 1, blk // 2
            keep = ((ri >> shift) == (ci >> shift)) & ((ri & w) != 0) & ((ci & w) == 0)
        mask_scr[i] = jnp.where(keep, 1.0, 0.0).astype(F32)


def _hg_operands(qf, kf, e):
    if e is None:
        return qf.astype(BF16), kf.astype(BF16)
    return (qf * e).astype(BF16), (kf * e).astype(BF16)


def _hg_scores(qf, kf, levels, mask_scr):
    A = None
    for n, e in enumerate(levels):
        a = _dot_nt(*_hg_operands(qf, kf, e)) * mask_scr[n]
        A = a if A is None else A + a
    return A


def _hg_specs(B, NC, D, C, pair):
    def spec(col0, rev):
        c0 = col0 // pair
        if rev:
            return pl.BlockSpec((C, pair * HEAD), lambda b, h, j: (b * NC + (NC - 1 - j), c0 + h))
        return pl.BlockSpec((C, pair * HEAD), lambda b, h, j: (b * NC + j, c0 + h))
    return spec


def _head(ref, hh):
    return ref.at[:, pl.ds(hh * HEAD, HEAD)]


class _Ride:
    def __init__(self, reads, bufs, outs, phases, fractions, sems):
        self.reads, self.bufs, self.outs = list(reads), list(bufs), list(outs)
        self.phases, self.fractions, self.sems = list(phases), list(fractions), list(sems)


def _ride_call(body, ride, *, name, grid, in_specs, out_specs, out_shape, scratch_shapes, semantics, operands):
    if ride is None:
        return pl.pallas_call(body, name=name, grid=grid, in_specs=in_specs, out_specs=out_specs, out_shape=out_shape,
                              scratch_shapes=scratch_shapes, compiler_params=_cp(semantics))(*operands)
    n_in, n_out, n_scr = len(in_specs), len(out_specs), len(scratch_shapes)
    nr, nb, no = len(ride.reads), len(ride.bufs), len(ride.outs)
    last_step = math.prod(grid) - 1

    def full_body(*refs):
        own = refs[:n_in] + refs[n_in + nr + nb:n_in + nr + nb + n_out]
        tail = refs[n_in + nr + nb + n_out:]
        ride_refs = (refs[n_in:n_in + nr], tail[:nb], tail[nb:nb + no])
        scr = tail[nb + no:]
        step = pl.program_id(0)
        for d in range(1, len(grid)):
            step = step * grid[d] + pl.program_id(d)
        for phase, frac in zip(ride.phases, ride.fractions):
            @pl.when(step == int(round(frac * last_step)))
            def _(phase=phase):
                phase(*ride_refs, *scr[n_scr:])
        body(*own, *scr[:n_scr])

    return pl.pallas_call(
        full_body, name=name, grid=grid,
        in_specs=list(in_specs) + [ANY] * (nr + nb),
        out_specs=list(out_specs) + [ANY] * (nb + no),
        out_shape=list(out_shape) + [jax.ShapeDtypeStruct(b.shape, b.dtype) for b in ride.bufs] + ride.outs,
        input_output_aliases={n_in + nr + j: n_out + j for j in range(nb)},
        scratch_shapes=list(scratch_shapes) + ride.sems,
        compiler_params=_cp(("arbitrary",) * len(grid)),
    )(*operands, *ride.reads, *ride.bufs)


def _hg_fwd(proj, B, lb, gn, ride=None):
    T = proj.shape[0]
    D = proj.shape[1] // N_SEG
    S = T // B
    C = min(HG_CHUNK, S)
    NC = S // C
    H = D // HEAD
    hpd = D // HEAD
    pair = math.gcd(H, HG_HEADS_PER_STEP)
    spec = _hg_specs(B, NC, D, C, pair)

    def body(q_ref, z_ref, v_ref, g_ref, lb_ref, gn_ref, yb_ref, o_ref, st_ref, a_ref, cum_ref, st_scr, mask_scr):
        @pl.when(pl.program_id(2) == 0)
        def _():
            st_scr[...] = jnp.zeros((pair, HEAD, HEAD), F32)
            _hg_fill_masks(mask_scr, C)

        for hh in range(pair):
            s_t = st_scr[hh]
            st_ref[hh] = s_t
            qf, _, kf, lf, _, _ = _hg_gates(_head(q_ref, hh)[...], _head(z_ref, hh)[...], _head(lb_ref, hh)[...])
            cum = _hg_cum(lf, C)
            _head(cum_ref, hh)[...] = cum
            A = _hg_scores(qf, kf, _hg_levels(lf, cum, C), mask_scr).astype(BF16)
            a_ref[hh] = A
            vb = _head(v_ref, hh)[...].astype(BF16)
            o = _dot_nt((qf * jnp.exp(cum)).astype(BF16), s_t.astype(BF16)) + _dot(A, vb)
            last = jnp.sum(lf, axis=0, keepdims=True)
            kend = kf * jnp.exp(last - cum)
            st_scr[hh] = jnp.exp(last) * s_t + _dot_tn(vb, kend.astype(BF16))
            _head(o_ref, hh)[...] = o
            g = _head(g_ref, hh)[...]
            _head(yb_ref, hh)[...] = ((o * _rms(o) * gn_ref[...]) * (g * _sig(g))).astype(BF16)

    HG = H // pair
    return _ride_call(
        body, ride, name="hg_fwd",
        grid=(B, HG, NC),
        in_specs=[spec(2 * hpd, False), spec(3 * hpd, False), spec(4 * hpd, False), spec(5 * hpd, False),
                  pl.BlockSpec((1, pair * HEAD), lambda b, h, j: (0, h)),
                  pl.BlockSpec((1, HEAD), lambda b, h, j: (0, 0))],
        out_specs=[spec(0, False), spec(0, False),
                   pl.BlockSpec((None, pair, None, HEAD, HEAD), lambda b, h, j: (b, h, j, 0, 0)),
                   pl.BlockSpec((None, pair, None, C, C), lambda b, h, j: (b, h, j, 0, 0)), spec(0, False)],
        out_shape=[jax.ShapeDtypeStruct((T, D), BF16), jax.ShapeDtypeStruct((T, D), F32),
                   jax.ShapeDtypeStruct((B, H, NC, HEAD, HEAD), F32),
                   jax.ShapeDtypeStruct((B, H, NC, C, C), BF16), jax.ShapeDtypeStruct((T, D), F32)],
        scratch_shapes=[pltpu.VMEM((pair, HEAD, HEAD), F32), pltpu.VMEM((len(_hg_level_blocks(C)), C, C), F32)],
        semantics=("parallel", "parallel", "arbitrary"),
        operands=(proj, proj, proj, proj, lb, gn))


def _w_full(ref):
    s, r, c = ref.shape
    return ref[...].reshape(s * r, c)


def _out_fwd(ya, yb, proj, x2, w_st, layer):
    T, D = x2.shape
    tm = _tile(T, 512)

    def body(ya_ref, yb_ref, ma_ref, mb_ref, x_ref, w_ref, xm_ref, y_ref):
        y = (_sig(ma_ref[...]) * ya_ref[...] + _sig(mb_ref[...]) * yb_ref[...]).astype(BF16)
        y_ref[...] = y
        xm_ref[...] = x_ref[...] + _dot(y, _w_full(w_ref))

    row = pl.BlockSpec((tm, D), lambda i: (i, 0))
    return pl.pallas_call(
        body, name="out_fwd",
        grid=(T // tm,),
        in_specs=[row, row, pl.BlockSpec((tm, D), lambda i: (i, 6)), pl.BlockSpec((tm, D), lambda i: (i, 7)), row,
                  pl.BlockSpec((N_SHARD, None, D // N_SHARD, D), lambda i: (0, layer, 0, 0))],
        out_specs=[row, row],
        out_shape=[jax.ShapeDtypeStruct((T, D), F32), jax.ShapeDtypeStruct((T, D), BF16)],
        compiler_params=_cp(("parallel",)),
    )(ya, yb, proj, proj, x2, w_st)


def _mlp_fwd(xm, gain, wup_st, wdn_st, layer):
    T, D = xm.shape
    F4 = wup_st.shape[3]
    tm = _tile(T, 1024)

    def body(x_ref, g_ref, wu_ref, wd_ref, xo_ref, up_ref, h_ref):
        @pl.when(pl.program_id(1) == 0)
        def _():
            x = x_ref[...]
            h_ref[...] = (x * _rms(x) * g_ref[...]).astype(BF16)
            xo_ref[...] = x
        up = _dot(h_ref[...], wu_ref[...])
        up_ref[...] = up.astype(BF16)
        act = jnp.maximum(up, 0.0)
        xo_ref[...] += _dot((act * act).astype(BF16), wd_ref[...])

    row = pl.BlockSpec((tm, D), lambda i, s: (i, 0))
    return pl.pallas_call(
        body, name="mlp_fwd",
        grid=(T // tm, N_SHARD),
        in_specs=[row, pl.BlockSpec((1, D), lambda i, s: (0, 0)),
                  pl.BlockSpec((None, None, D, F4), lambda i, s: (s, layer, 0, 0)),
                  pl.BlockSpec((None, None, F4, D), lambda i, s: (s, layer, 0, 0))],
        out_specs=[row, pl.BlockSpec((tm, F4), lambda i, s: (i, s)), row],
        out_shape=[jax.ShapeDtypeStruct((T, D), F32), jax.ShapeDtypeStruct((T, N_SHARD * F4), BF16),
                   jax.ShapeDtypeStruct((T, D), BF16)],
        compiler_params=_cp(("parallel", "arbitrary")),
    )(xm, gain, wup_st, wdn_st)


def _final_loss(x2, gain, tgt):
    T, D = x2.shape
    tm = _tile(T, 512)
    nt = T // tm

    def body(x_ref, g_ref, t_ref, loss_ref, dx_ref, dg_ref):
        x = x_ref[...]
        rs = _rms(x)
        err = x * rs * g_ref[...] - t_ref[...]
        part = 0.5 * jnp.sum(jnp.sum(err * err, axis=-1, keepdims=True) * (1.0 / D), axis=0, keepdims=True)
        loss_ref[...] = jnp.broadcast_to(part, (SUBLANES, 128))
        dx, dg = _rms_bwd(err * (1.0 / D), x, rs, g_ref[...])
        dx_ref[...] = dx
        dg_ref[...] = dg

    row = pl.BlockSpec((tm, D), lambda i: (i, 0))
    return pl.pallas_call(
        body, name="final_loss",
        grid=(nt,),
        in_specs=[row, pl.BlockSpec((1, D), lambda i: (0, 0)), row],
        out_specs=[pl.BlockSpec((None, SUBLANES, 128), lambda i: (i, 0, 0)), row,
                   pl.BlockSpec((None, SUBLANES, D), lambda i: (i, 0, 0))],
        out_shape=[jax.ShapeDtypeStruct((nt, SUBLANES, 128), F32), jax.ShapeDtypeStruct((T, D), F32),
                   jax.ShapeDtypeStruct((nt, SUBLANES, D), F32)],
        compiler_params=_cp(("parallel",)),
    )(x2, gain, tgt)


def _mlp_bwd_x(dx, xm, up, gain, wup_st, wdn_st, layer, ride=None):
    T, D = xm.shape
    F4 = wup_st.shape[3]
    tm = _tile(T, 1024)
    nt = T // tm

    def body(dx_ref, x_ref, up_ref, g_ref, wu_ref, wd_ref, dxm_ref, dup_ref, dg_ref, dxb):
        s = pl.program_id(1)

        @pl.when(s == 0)
        def _():
            dxb[...] = dx_ref[...].astype(BF16)
            dxm_ref[...] = jnp.zeros((tm, D), F32)

        d_act = _dot_nt(dxb[...], wd_ref[...])
        d_up = (d_act * (2.0 * jnp.maximum(up_ref[...].astype(F32), 0.0))).astype(BF16)
        dup_ref[...] = d_up
        dxm_ref[...] += _dot_nt(d_up, wu_ref[...])

        @pl.when(s == N_SHARD - 1)
        def _():
            x = x_ref[...]
            dxn, dg = _rms_bwd(dxm_ref[...], x, _rms(x), g_ref[...])
            dxm_ref[...] = dx_ref[...] + dxn
            dg_ref[...] = dg

    row = pl.BlockSpec((tm, D), lambda i, s: (i, 0))
    return _ride_call(
        body, ride, name="mlp_bwd_x",
        grid=(nt, N_SHARD),
        in_specs=[row, row, pl.BlockSpec((tm, F4), lambda i, s: (i, s)), pl.BlockSpec((1, D), lambda i, s: (0, 0)),
                  pl.BlockSpec((None, None, D, F4), lambda i, s: (s, layer, 0, 0)),
                  pl.BlockSpec((None, None, F4, D), lambda i, s: (s, layer, 0, 0))],
        out_specs=[row, pl.BlockSpec((tm, F4), lambda i, s: (i, s)),
                   pl.BlockSpec((None, SUBLANES, D), lambda i, s: (i, 0, 0)), row],
        out_shape=[jax.ShapeDtypeStruct((T, D), F32), jax.ShapeDtypeStruct((T, N_SHARD * F4), BF16),
                   jax.ShapeDtypeStruct((nt, SUBLANES, D), F32), jax.ShapeDtypeStruct((T, D), BF16)],
        scratch_shapes=[],
        semantics=("parallel", "arbitrary"),
        operands=(dx, xm, up, gain, wup_st, wdn_st))


def _layer_slot(bufs, shapes, n_layers):
    out_shape = [jax.ShapeDtypeStruct((N_SHARD, n_layers) + s, F32) for s in shapes]
    return out_shape, ([] if bufs is None else list(bufs))


def _mlp_bwd_w(up, dxb, h, dup, layer, n_layers, bufs):
    T, D = dxb.shape
    F4 = up.shape[1] // N_SHARD
    tk = _tile(T, 1024)
    out_shape, extra = _layer_slot(bufs, [(D, F4), (F4, D)], n_layers)

    def body(up_ref, dx_ref, h_ref, dup_ref, *rest):
        gu_ref, gd_ref = rest[-2:]

        @pl.when(pl.program_id(1) == 0)
        def _():
            gu_ref[...] = jnp.zeros((D, F4), F32)
            gd_ref[...] = jnp.zeros((F4, D), F32)
        act = jnp.maximum(up_ref[...], 0.0)
        gd_ref[...] += _dot_tn(act * act, dx_ref[...])
        gu_ref[...] += _dot_tn(h_ref[...], dup_ref[...])

    return pl.pallas_call(
        body, name="mlp_bwd_w",
        grid=(N_SHARD, T // tk),
        in_specs=[pl.BlockSpec((tk, F4), lambda s, t: (t, s)), pl.BlockSpec((tk, D), lambda s, t: (t, 0)),
                  pl.BlockSpec((tk, D), lambda s, t: (t, 0)), pl.BlockSpec((tk, F4), lambda s, t: (t, s))]
        + [ANY] * len(extra),
        out_specs=[pl.BlockSpec((None, None, D, F4), lambda s, t: (s, layer, 0, 0)),
                   pl.BlockSpec((None, None, F4, D), lambda s, t: (s, layer, 0, 0))],
        out_shape=out_shape,
        input_output_aliases={4 + i: i for i in range(len(extra))},
        compiler_params=_cp(("parallel", "arbitrary")),
    )(up, dxb, h, dup, *extra)


def _out_bwd_x(dxm, ya, yb, proj, w_st, layer):
    T, D = dxm.shape
    tm = _tile(T, 512)

    def body(dx_ref, ya_ref, yb_ref, ma_ref, mb_ref, w_ref, dya_ref, dyb_ref, dma_ref, dmb_ref):
        dy = _dot_nt(dx_ref[...].astype(BF16), _w_full(w_ref))
        sa = _sig(ma_ref[...])
        sb = _sig(mb_ref[...])
        dya_ref[...] = (dy * sa).astype(BF16)
        dyb_ref[...] = (dy * sb).astype(BF16)
        dma_ref[...] = (dy * ya_ref[...] * (sa * (1.0 - sa))).astype(BF16)
        dmb_ref[...] = (dy * yb_ref[...] * (sb * (1.0 - sb))).astype(BF16)

    row = pl.BlockSpec((tm, D), lambda i: (i, 0))
    return pl.pallas_call(
        body, name="out_bwd_x",
        grid=(T // tm,),
        in_specs=[row, row, row, pl.BlockSpec((tm, D), lambda i: (i, 6)), pl.BlockSpec((tm, D), lambda i: (i, 7)),
                  pl.BlockSpec((N_SHARD, None, D // N_SHARD, D), lambda i: (0, layer, 0, 0))],
        out_specs=[row] * 4,
        out_shape=[jax.ShapeDtypeStruct((T, D), BF16)] * 4,
        compiler_params=_cp(("parallel",)),
    )(dxm, ya, yb, proj, proj, w_st)


def _out_bwd_w(ymix, dxm, layer, n_layers, bufs):
    T, D = dxm.shape
    tk = _tile(T, 1024)
    out_shape, extra = _layer_slot(bufs, [(D // N_SHARD, D)], n_layers)

    def body(y_ref, dx_ref, *rest):
        g_ref = rest[-1]

        @pl.when(pl.program_id(0) == 0)
        def _():
            g_ref[...] = jnp.zeros((N_SHARD, D // N_SHARD, D), F32)
        g = _dot_tn(y_ref[...], dx_ref[...].astype(BF16))
        g_ref[...] += g.reshape(N_SHARD, D // N_SHARD, D)

    row = pl.BlockSpec((tk, D), lambda t: (t, 0))
    return pl.pallas_call(
        body, name="out_bwd_w",
        grid=(T // tk,),
        in_specs=[row, row] + [ANY] * len(extra),
        out_specs=[pl.BlockSpec((N_SHARD, None, D // N_SHARD, D), lambda t: (0, layer, 0, 0))],
        out_shape=out_shape,
        input_output_aliases={2 + i: i for i in range(len(extra))},
        compiler_params=_cp(("arbitrary",)),
    )(ymix, dxm, *extra)[0]


def _rg_bwd(proj, hrg, dya, B, cw, cb, wr, br, wi, bi, sp):
    T = proj.shape[0]
    D = proj.shape[1] // N_SEG
    S = T // B
    ts = _tile(S, RG_TILE)
    nts = S // ts
    nb = D // RG_BLOCK
    t8 = ts // SUBLANES

    def body(xa_ref, xp_ref, ga_ref, h_ref, hp_ref, dya_ref, cw_ref, cb_ref, wr_ref, br_ref, wi_ref, bi_ref, sp_ref,
             dxa_ref, dga_ref, gwr_ref, gwi_ref, gcw_ref, gcb_ref, gbr_ref, gbi_ref, gsp_ref,
             xbuf, hbuf, abuf, dbuf, g_scr, c_scr, gcar):
        b = pl.program_id(0)
        j = pl.program_id(1)
        first_in_time = j == nts - 1

        @pl.when((b == 0) & (j == 0))
        def _():
            gwr_ref[...] = jnp.zeros((nb, RG_BLOCK, RG_BLOCK), F32)
            gwi_ref[...] = jnp.zeros((nb, RG_BLOCK, RG_BLOCK), F32)
            gcw_ref[...] = jnp.zeros((CONV_TAPS, SUBLANES, D), F32)
            for r in (gcb_ref, gbr_ref, gbi_ref, gsp_ref):
                r[...] = jnp.zeros((SUBLANES, D), F32)

        @pl.when(j == 0)
        def _():
            abuf[pl.ds(ts, SUBLANES), :] = jnp.zeros((SUBLANES, D), F32)
            dbuf[pl.ds(ts, SUBLANES), :] = jnp.zeros((SUBLANES, D), F32)
            gcar[...] = jnp.zeros((SUBLANES, D), F32)

        keep = jnp.where(first_in_time, 0.0, 1.0)
        xbuf[0:SUBLANES, :] = xp_ref[...] * keep
        xbuf[pl.ds(SUBLANES, ts), :] = xa_ref[...]
        hbuf[0:SUBLANES, :] = hp_ref[...] * keep
        hbuf[pl.ds(SUBLANES, ts), :] = h_ref[...]

        xc = _conv_taps(xbuf, cw_ref, ts) + cb_ref[...]
        sp = sp_ref[...]
        r, i, a, mult = _rg_gates(xc, wr_ref, br_ref[...], wi_ref, bi_ref[...], sp)
        g_gate, dg_gate = _gelu_and_grad(ga_ref[...])
        dya = dya_ref[...].astype(F32)
        dga_ref[...] = (dya * h_ref[...] * dg_gate).astype(BF16)

        abuf[0:ts, :] = a
        c_scr[...] = abuf[pl.ds(1, ts), :]
        g_scr[...] = dya * g_gate
        row8 = lax.broadcasted_iota(jnp.int32, (SUBLANES, 1), 0)

        def blk(n, gnext):
            off = pl.multiple_of((t8 - 1 - n) * SUBLANES, SUBLANES)
            c8 = c_scr[pl.ds(off, SUBLANES), :]
            d8 = g_scr[pl.ds(off, SUBLANES), :]
            for d in (1, 2, 4):
                m = row8 < SUBLANES - d
                cn = jnp.where(m, pltpu.roll(c8, SUBLANES - d, 0), 1.0)
                dn = jnp.where(m, pltpu.roll(d8, SUBLANES - d, 0), 0.0)
                d8 = d8 + c8 * dn
                c8 = c8 * cn
            g8 = d8 + c8 * gnext
            g_scr[pl.ds(off, SUBLANES), :] = g8
            first = jnp.sum(jnp.where(row8 == 0, g8, 0.0), axis=0, keepdims=True)
            return jnp.broadcast_to(first, (SUBLANES, D))

        gcar[...] = lax.fori_loop(0, t8, blk, gcar[...])
        abuf[pl.ds(ts, SUBLANES), :] = a[0:SUBLANES, :]

        g = g_scr[...]
        hprev = hbuf[pl.ds(SUBLANES - 1, ts), :]
        gx = i * xc
        e2 = a * a
        dla = g * hprev * a - jnp.where(mult > 0.0, g * gx * e2 / jnp.where(mult > 0.0, mult, 1.0), 0.0)
        dgx = g * mult
        dpr = (dla * ((-RG_C) * sp)) * (r * (1.0 - r))
        dpi = (dgx * xc) * (i * (1.0 - i))
        gsp_ref[...] += _rows8(dla * ((-RG_C) * r))
        gbr_ref[...] += _rows8(dpr)
        gbi_ref[...] += _rows8(dpi)
        dprb = dpr.astype(BF16)
        dpib = dpi.astype(BF16)
        xcb = xc.astype(BF16)
        back = []
        for n in range(nb):
            sl = slice(n * RG_BLOCK, (n + 1) * RG_BLOCK)
            back.append(_dot_nt(dprb[:, sl], wr_ref[n]) + _dot_nt(dpib[:, sl], wi_ref[n]))
            gwr_ref[n] += _dot_tn(xcb[:, sl], dprb[:, sl])
            gwi_ref[n] += _dot_tn(xcb[:, sl], dpib[:, sl])
        dxc = dgx * i + (jnp.concatenate(back, axis=1) if nb > 1 else back[0])
        gcb_ref[...] += _rows8(dxc)

        dbuf[0:ts, :] = dxc
        dxa = None
        for jtap in range(CONV_TAPS):
            term = cw_ref[jtap:jtap + 1, :] * dbuf[pl.ds(CONV_TAPS - 1 - jtap, ts), :]
            dxa = term if dxa is None else dxa + term
            gcw_ref[jtap] += _rows8(dxc * xbuf[pl.ds(SUBLANES - (CONV_TAPS - 1) + jtap, ts), :])
        dxa_ref[...] = dxa.astype(BF16)
        dbuf[pl.ds(ts, SUBLANES), :] = dxc[0:SUBLANES, :]

    def tile_map(col):
        return lambda b, j: (b * nts + (nts - 1 - j), col)

    def prev8_map(col):
        return lambda b, j: (jnp.maximum((b * nts + (nts - 1 - j)) * t8 - 1, 0), col)

    vec = pl.BlockSpec((1, D), lambda b, j: (0, 0))
    gate = pl.BlockSpec((nb, RG_BLOCK, RG_BLOCK), lambda b, j: (0, 0, 0))
    acc8 = pl.BlockSpec((SUBLANES, D), lambda b, j: (0, 0))
    return pl.pallas_call(
        body, name="rg_bwd",
        grid=(B, nts),
        in_specs=[pl.BlockSpec((ts, D), tile_map(0)), pl.BlockSpec((SUBLANES, D), prev8_map(0)),
                  pl.BlockSpec((ts, D), tile_map(1)),
                  pl.BlockSpec((ts, D), tile_map(0)), pl.BlockSpec((SUBLANES, D), prev8_map(0)),
                  pl.BlockSpec((ts, D), tile_map(0)),
                  pl.BlockSpec((CONV_TAPS, D), lambda b, j: (0, 0)), vec, gate, vec, gate, vec, vec],
        out_specs=[pl.BlockSpec((ts, D), tile_map(0)), pl.BlockSpec((ts, D), tile_map(0)), gate, gate,
                   pl.BlockSpec((CONV_TAPS, SUBLANES, D), lambda b, j: (0, 0, 0)), acc8, acc8, acc8, acc8],
        out_shape=[jax.ShapeDtypeStruct((T, D), BF16)] * 2
        + [jax.ShapeDtypeStruct((nb, RG_BLOCK, RG_BLOCK), F32)] * 2
        + [jax.ShapeDtypeStruct((CONV_TAPS, SUBLANES, D), F32)] + [jax.ShapeDtypeStruct((SUBLANES, D), F32)] * 4,
        scratch_shapes=[pltpu.VMEM((SUBLANES + ts, D), F32), pltpu.VMEM((SUBLANES + ts, D), F32),
                        pltpu.VMEM((ts + SUBLANES, D), F32), pltpu.VMEM((ts + SUBLANES, D), F32),
                        pltpu.VMEM((ts, D), F32), pltpu.VMEM((ts, D), F32), pltpu.VMEM((SUBLANES, D), F32)],
        compiler_params=_cp(("arbitrary", "arbitrary")),
    )(proj, proj, proj, hrg, hrg, dya, cw, cb, wr, br, wi, bi, sp)


def _hg_bwd(proj, o_sv, dyb, states, a_sv, cum_sv, B, lb, gn, ride=None):
    T = proj.shape[0]
    D = proj.shape[1] // N_SEG
    S = T // B
    C = min(HG_CHUNK, S)
    NC = S // C
    H = D // HEAD
    hpd = D // HEAD
    pair = math.gcd(H, HG_HEADS_PER_STEP)
    spec = _hg_specs(B, NC, D, C, pair)

    def body(q_ref, z_ref, v_ref, g_ref, o_ref, dyb_ref, st_ref, a_ref, cum_ref, lb_ref, gn_ref,
             dq_ref, dz_ref, dv_ref, dg_ref, glb_ref, ggn_ref, ds_scr, mask_scr):
        @pl.when(pl.program_id(2) == 0)
        def _():
            ds_scr[...] = jnp.zeros((pair, HEAD, HEAD), F32)
            _hg_fill_masks(mask_scr, C)
            glb_ref[...] = jnp.zeros((SUBLANES, pair * HEAD), F32)
            ggn_ref[...] = jnp.zeros((pair, SUBLANES, HEAD), F32)

        for hh in range(pair):
            cols = [_head(r, hh) for r in (q_ref, z_ref, v_ref, g_ref, o_ref, dyb_ref)]
            outs = [_head(r, hh) for r in (dq_ref, dz_ref, dv_ref, dg_ref, glb_ref)]
            one_head(*cols, st_ref.at[hh], a_ref.at[hh], _head(cum_ref, hh), _head(lb_ref, hh), gn_ref,
                     *outs, ggn_ref.at[hh], ds_scr.at[hh], mask_scr)

    def one_head(q_ref, z_ref, v_ref, g_ref, o_ref, dyb_ref, st_ref, a_ref, cum_ref, lb_ref, gn_ref,
                 dq_ref, dz_ref, dv_ref, dg_ref, glb_ref, ggn_ref, ds_scr, mask_scr):
        q = q_ref[...]
        lb = lb_ref[...]
        gn = gn_ref[...]
        qf, qs, kf, lf, fg, sig = _hg_gates(q, z_ref[...], lb)
        cum = cum_ref[...]
        levels = _hg_levels(lf, cum, C)

        o = o_ref[...]
        g = g_ref[...]
        gs = _sig(g)
        rs = _rms(o)
        dyb = dyb_ref[...].astype(F32)
        don = dyb * (g * gs)
        dg_ref[...] = (dyb * (o * rs * gn) * (gs * (1.0 + g * (1.0 - gs)))).astype(BF16)
        ggn_ref[...] += _rows8(don * o * rs)
        dn = don * gn
        do = rs * (dn - o * (rs * rs) * jnp.mean(dn * o, axis=-1, keepdims=True))

        s_t = st_ref[...].astype(BF16)
        ds_t = ds_scr[...]
        ds_b = ds_t.astype(BF16)
        dob = do.astype(BF16)
        vb = v_ref[...].astype(BF16)
        ecum = jnp.exp(cum)
        last = jnp.sum(lf, axis=0, keepdims=True)
        eend = jnp.exp(last - cum)
        qhat = (qf * ecum).astype(BF16)
        kend = (kf * eend).astype(BF16)

        dA = _dot_nt(dob, vb)
        dq_inter = _dot(dob, s_t)
        dk_state = _dot(vb, ds_b)
        dqf = dq_inter * ecum
        dkf = dk_state * eend
        g_intra = None
        for n, e in enumerate(levels):
            qw, kw = _hg_operands(qf, kf, e)
            dam = (dA * mask_scr[n]).astype(BF16)
            rq = _dot(dam, kw)
            rk = _dot_tn(dam, qw)
            dqf += rq if e is None else rq * e
            dkf += rk if e is None else rk * e
            gi = qw.astype(F32) * rq - kw.astype(F32) * rk
            g_intra = gi if g_intra is None else g_intra + gi
        dv_ref[...] = (_dot_tn(a_ref[...], dob) + _dot_nt(kend, ds_b)).astype(BF16)
        e_last = jnp.exp(last)
        ds_scr[...] = e_last * ds_t + _dot_tn(dob, qhat)

        ri = lax.broadcasted_iota(jnp.int32, (C, C), 0)
        ci = lax.broadcasted_iota(jnp.int32, (C, C), 1)
        y_state = kend.astype(F32) * dk_state
        dlf = (_dot_01(jnp.where(ci >= ri, 1.0, 0.0).astype(BF16), g_intra + qhat.astype(F32) * dq_inter - y_state)
               + jnp.sum(y_state, axis=0, keepdims=True)
               + jnp.sum(e_last * st_ref[...] * ds_t, axis=0, keepdims=True))
        dfg = jnp.where(fg > F_MIN, dlf / jnp.maximum(fg, F_MIN), 0.0)
        sneg = 1.0 - sig
        diff = dfg - dkf
        dz_ref[...] = ((1.0 - lb) * sig * sneg * diff).astype(BF16)
        glb_ref[...] += _rows8(sneg * diff)
        dq_ref[...] = (dqf * (qs * (1.0 + q * (1.0 - qs)))).astype(BF16)

    return _ride_call(
        body, ride, name="hg_bwd",
        grid=(B, H // pair, NC),
        in_specs=[spec(2 * hpd, True), spec(3 * hpd, True), spec(4 * hpd, True), spec(5 * hpd, True),
                  spec(0, True), spec(0, True),
                  pl.BlockSpec((None, pair, None, HEAD, HEAD), lambda b, h, j: (b, h, NC - 1 - j, 0, 0)),
                  pl.BlockSpec((None, pair, None, C, C), lambda b, h, j: (b, h, NC - 1 - j, 0, 0)), spec(0, True),
                  pl.BlockSpec((1, pair * HEAD), lambda b, h, j: (0, h)),
                  pl.BlockSpec((1, HEAD), lambda b, h, j: (0, 0))],
        out_specs=[spec(0, True)] * 4
        + [pl.BlockSpec((None, SUBLANES, pair * HEAD), lambda b, h, j: (b, 0, h)),
           pl.BlockSpec((None, pair, SUBLANES, HEAD), lambda b, h, j: (b, h, 0, 0))],
        out_shape=[jax.ShapeDtypeStruct((T, D), BF16)] * 4
        + [jax.ShapeDtypeStruct((B, SUBLANES, D), F32), jax.ShapeDtypeStruct((B, H, SUBLANES, HEAD), F32)],
        scratch_shapes=[pltpu.VMEM((pair, HEAD, HEAD), F32), pltpu.VMEM((len(_hg_level_blocks(C)), C, C), F32)],
        semantics=("parallel", "parallel", "arbitrary"),
        operands=(proj, proj, proj, proj, o_sv, dyb, states, a_sv, cum_sv, lb, gn))


def _inproj_bwd_x(dsegs, w_st, layer, x2, gain, dxm, ride=None):
    T, D = x2.shape
    tm = _tile(T, 512)
    nt = T // tm

    def body(*refs):
        seg_refs = refs[:N_SEG]
        w_ref, x_ref, g_ref, dxm_ref, dx_ref, dg_ref = refs[N_SEG:]
        s = pl.program_id(1)

        @pl.when(s == 0)
        def _():
            dx_ref[...] = jnp.zeros((tm, D), F32)

        for ss in range(N_SHARD):
            @pl.when(s == ss)
            def _(ss=ss):
                dx_ref[...] += (_dot_nt(seg_refs[2 * ss][...], w_ref[:, 0:D])
                                + _dot_nt(seg_refs[2 * ss + 1][...], w_ref[:, D:2 * D]))

        @pl.when(s == N_SHARD - 1)
        def _():
            x = x_ref[...]
            dxn, dg = _rms_bwd(dx_ref[...], x, _rms(x), g_ref[...])
            dx_ref[...] = dxm_ref[...] + dxn
            dg_ref[...] = dg

    row = pl.BlockSpec((tm, D), lambda i, s: (i, 0))

    def seg_spec(kk):
        return pl.BlockSpec((tm, D), lambda i, s: (jnp.minimum(i + jnp.where(s > kk // 2, 1, 0), nt - 1), 0))

    return _ride_call(
        body, ride, name="inproj_bwd_x",
        grid=(nt, N_SHARD),
        in_specs=[seg_spec(kk) for kk in range(N_SEG)]
        + [pl.BlockSpec((None, None, D, 2 * D), lambda i, s: (s, layer, 0, 0)), row,
           pl.BlockSpec((1, D), lambda i, s: (0, 0)), row],
        out_specs=[row, pl.BlockSpec((None, SUBLANES, D), lambda i, s: (i, 0, 0))],
        out_shape=[jax.ShapeDtypeStruct((T, D), F32), jax.ShapeDtypeStruct((nt, SUBLANES, D), F32)],
        scratch_shapes=[],
        semantics=("parallel", "arbitrary"),
        operands=(*dsegs, w_st, x2, gain, dxm))


def _inproj_bwd_w(h, dsegs, layer, n_layers, bufs):
    T, D = h.shape
    tk = _tile(T, 1024)
    out_shape, extra = _layer_slot(bufs, [(D, 2 * D)], n_layers)

    def body(*refs):
        h_ref = refs[0]
        seg_refs = refs[1:1 + N_SEG]
        g_ref = refs[-1]
        k = pl.program_id(0)

        @pl.when(pl.program_id(1) == 0)
        def _():
            g_ref[...] = jnp.zeros((D, D), F32)

        for kk in range(N_SEG):
            @pl.when(k == kk)
            def _(kk=kk):
                g_ref[...] += _dot_tn(h_ref[...], seg_refs[kk][...])

    def seg_spec(kk):
        return pl.BlockSpec((tk, D), lambda k, t: (jnp.where(k == kk, t, 0), 0))

    return pl.pallas_call(
        body, name="inproj_bwd_w",
        grid=(N_SEG, T // tk),
        in_specs=[pl.BlockSpec((tk, D), lambda k, t: (t, 0))] + [seg_spec(kk) for kk in range(N_SEG)]
        + [ANY] * len(extra),
        out_specs=[pl.BlockSpec((None, None, D, D), lambda k, t: (k // 2, layer, 0, k % 2))],
        out_shape=out_shape,
        input_output_aliases={1 + N_SEG + i: i for i in range(len(extra))},
        compiler_params=_cp(("parallel", "arbitrary")),
    )(h, *dsegs, *extra)[0]


def _softmax_rows(lg_ref, L):
    rows = [lg_ref[l:l + 1, :] for l in range(L)]
    mx = functools.reduce(jnp.maximum, rows)
    es = [jnp.exp(r - mx) for r in rows]
    den = functools.reduce(lambda p, q: p + q, es)
    return [e / den for e in es]


def _prep(lb_logits, lam):
    L, D = lb_logits.shape

    def body(lg_ref, lam_ref, lowb_ref, sp_ref):
        sm = _softmax_rows(lg_ref, L)
        run = sm[0]
        for l in range(L):
            if l > 0:
                run = run + sm[l]
            lowb_ref[l:l + 1, :] = jnp.clip(run - sm[0], 0.0, 1.0)
        y = -lam_ref[...]
        sp_ref[...] = jnp.maximum(y, 0.0) + jnp.log1p(jnp.exp(-jnp.abs(y)))

    return pl.pallas_call(
        body, name="prep_small",
        out_shape=[jax.ShapeDtypeStruct((L, D), F32)] * 2,
    )(lb_logits, lam)


def _local_step(x, tgt, lowb, sp, norm_mix, wbufs, conv_w, conv_b, w_r, b_r, w_i, b_i, hg_norm,
                norm_mlp, norm_final):
    B, S, D = x.shape
    L = norm_mix.shape[0]
    T = B * S
    x2 = x.reshape(T, D)
    row = lambda a, l: a[l:l + 1]

    def weight_views(bufs):
        f4 = bufs[2].shape[-1]
        return (bufs[0].reshape(N_SHARD, L, D, 2 * D), bufs[1].reshape(N_SHARD, L, D // N_SHARD, D),
                bufs[2].reshape(N_SHARD, L, D, f4), bufs[3].reshape(N_SHARD, L, f4, D))

    w_in_st, w_out_st, w_up_st, w_down_st = weight_views(wbufs)
    saved = []
    for l in range(L):
        proj, h = _inproj_fwd(x2, row(norm_mix, l), w_in_st, l)
        ya, hrg = _rg_fwd(proj, B, conv_w[l], row(conv_b, l), w_r[l], row(b_r, l), w_i[l], row(b_i, l), row(sp, l))
        l0, nl = (1, min(2, L - 1)) if l == 0 else (l + 2, 1 if l + 2 < L else 0)
        if nl > 0:
            yb, o, st, a_sv, cum_sv, *wbufs = _hg_fwd(proj, B, row(lowb, l), row(hg_norm, l),
                                                      _gather_ride(list(wbufs), LINK_SPLIT, l0, nl))
            w_in_st, w_out_st, w_up_st, w_down_st = weight_views(wbufs)
        else:
            yb, o, st, a_sv, cum_sv = _hg_fwd(proj, B, row(lowb, l), row(hg_norm, l))
        xm, ymix = _out_fwd(ya, yb, proj, x2, w_out_st, l)
        xo, up, h2 = _mlp_fwd(xm, row(norm_mlp, l), w_up_st, w_down_st, l)
        saved.append((x2, proj, h, ya, hrg, yb, o, (st, a_sv, cum_sv), xm, ymix, up, h2))
        x2 = xo
    loss_parts, dx, g_nf = _final_loss(x2, norm_final[None, :], tgt.reshape(T, D))

    def reduce_part(g_in, g_out, g_mlp, gate_list):
        gate_list = gate_list[::-1]
        grads = [g_in, g_out, g_mlp[0], g_mlp[1], _shard_gate(jnp.stack([g[0] for g in gate_list])),
                 _shard_gate(jnp.stack([g[1] for g in gate_list]))]
        return _ReduceScatter([g.reshape(N_SHARD, -1, g.shape[-1]) for g in grads], LINK_SPLIT)

    gates = []
    small = []
    g_in = g_out = g_mlp = None
    rest = None
    for l in reversed(range(L)):
        x_in, proj, h, ya, hrg, yb, o, st, xm, ymix, up, h2 = saved[l]
        alone = l == 0 and L > 1
        if alone:
            rest = reduce_part(g_in, g_out, g_mlp, gates)
            g_in = g_out = g_mlp = None
            gates = []
        slot, n_slots = (0, 1) if l == 0 else (l - 1, L - 1)
        dxm, dup, g_nmlp, dxb, *got = _mlp_bwd_x(dx, xm, up, row(norm_mlp, l), w_up_st, w_down_st, l,
                                                 rest.ride_c() if alone else None)
        if alone:
            rest.after_c(got)
        g_mlp = _mlp_bwd_w(up, dxb, h2, dup, slot, n_slots, g_mlp)
        dya, dyb, dma, dmb = _out_bwd_x(dxm, ya, yb, proj, w_out_st, l)
        g_out = _out_bwd_w(ymix, dxm, slot, n_slots, None if g_out is None else [g_out])
        dxa, dga, g_wr, g_wi, g_cw, g_cb, g_br, g_bi, g_sp = _rg_bwd(
            proj, hrg, dya, B, conv_w[l], row(conv_b, l), w_r[l], row(b_r, l), w_i[l], row(b_i, l), row(sp, l))
        dq, dz, dv, dg, g_lb, g_gn, *got = _hg_bwd(proj, o, dyb, *st, B, row(lowb, l), row(hg_norm, l),
                                                   rest.ride_1() if alone else None)
        if alone:
            rest.after_1(got)
        dsegs = (dxa, dga, dq, dz, dv, dg, dma, dmb)
        dx, g_nmix, *got = _inproj_bwd_x(dsegs, w_in_st, l, x_in, row(norm_mix, l), dxm,
                                         rest.ride_2() if alone else None)
        if alone:
            rest.after_2(got)
        g_in = _inproj_bwd_w(h, dsegs, slot, n_slots, None if g_in is None else [g_in])
        gates.append((g_wr, g_wi))
        small.append((g_lb, g_nmix, g_cb, g_br, g_bi, g_sp, g_nmlp, g_gn, g_cw))
    small.reverse()
    first = reduce_part(g_in, g_out, g_mlp, gates)
    first.exchange_c()
    first.exchange_1()
    first.exchange_2()
    parts = [first.finish()] + ([rest.finish()] if rest is not None else [])
    return loss_parts, dx.reshape(B, S, D), parts, small, g_nf


def _me():
    return lax.axis_index("x"), lax.axis_index("y"), lax.axis_index("c")


def _cast_place(w, slot):
    R, N = w.shape
    tr = _tile(R, max(16, (1 << 20) // N))

    def body(slot_ref, w_ref, o_ref):
        o_ref[...] = w_ref[...].astype(BF16)

    return pl.pallas_call(
        body, name="cast_place",
        grid_spec=pltpu.PrefetchScalarGridSpec(
            num_scalar_prefetch=1, grid=(R // tr,),
            in_specs=[pl.BlockSpec((tr, N), lambda i, slot: (i, 0))],
            out_specs=pl.BlockSpec((None, tr, N), lambda i, slot: (slot[0], i, 0))),
        out_shape=jax.ShapeDtypeStruct((N_SHARD, R, N), BF16),
        compiler_params=_cp(("parallel",)),
    )(slot, w)


def _gather_weights(bufs, first_axes, l0, nl):
    n = len(bufs)
    phases = _gather_phases(n, first_axes, l0, nl)

    def body(*refs):
        outs = refs[n:2 * n]
        ssem, rsem = refs[2 * n:]
        for ph in phases:
            ph(outs, ssem, rsem)

    return pl.pallas_call(
        body, name="gather_weights",
        in_specs=[ANY] * n, out_specs=[ANY] * n,
        out_shape=[jax.ShapeDtypeStruct(b.shape, b.dtype) for b in bufs],
        input_output_aliases={a: a for a in range(n)},
        scratch_shapes=_gather_sems(n),
        compiler_params=pltpu.CompilerParams(has_side_effects=True),
    )(*bufs)


def _gather_sems(n):
    return [pltpu.SemaphoreType.DMA((n, 6)), pltpu.SemaphoreType.DMA((n, 6))]


def _gather_phases(n, first_axes, l0, nl):
    def ctx(outs, ssem, rsem):
        x, y, c = _me()

        def piece(a, flips, half):
            sx = 1 - x if flips[0] else x
            sy = 1 - y if flips[1] else y
            return outs[a].at[2 * sx + sy, pl.ds(l0[a], nl[a]), half]

        def rcopy(a, k, ref, dev):
            return pltpu.make_async_remote_copy(src_ref=ref, dst_ref=ref, send_sem=ssem.at[a, k], recv_sem=rsem.at[a, k],
                                                device_id=dev, device_id_type=MESH_ID)

        def route(a):
            fx = first_axes[a] == "x"
            f_dev = (1 - x, y, c) if fx else (x, 1 - y, c)
            g_dev = (x, 1 - y, c) if fx else (1 - x, y, c)
            return f_dev, g_dev, ((1, 0) if fx else (0, 1)), ((0, 1) if fx else (1, 0))

        return c, (x, y, 1 - c), piece, rcopy, route

    def own_halves(outs, ssem, rsem):
        c, sib, piece, rcopy, route = ctx(outs, ssem, rsem)
        for a in range(n):
            f_dev, g_dev, _, _ = route(a)
            own = piece(a, (0, 0), c)
            rcopy(a, 0, own, f_dev).start()
            rcopy(a, 1, own, g_dev).start()

    def pass_on_neighbours(outs, ssem, rsem):
        c, sib, piece, rcopy, route = ctx(outs, ssem, rsem)
        for a in range(n):
            f_dev, g_dev, f_flip, _ = route(a)
            got = piece(a, f_flip, c)
            rcopy(a, 0, got, f_dev).wait_recv()
            rcopy(a, 2, got, g_dev).start()
            rcopy(a, 3, got, sib).start()
        for a in range(n):
            _, g_dev, _, g_flip = route(a)
            got = piece(a, g_flip, c)
            rcopy(a, 1, got, g_dev).wait_recv()
            rcopy(a, 4, got, sib).start()

    def pass_on_diagonal(outs, ssem, rsem):
        c, sib, piece, rcopy, route = ctx(outs, ssem, rsem)
        for a in range(n):
            _, g_dev, _, _ = route(a)
            got = piece(a, (1, 1), c)
            rcopy(a, 2, got, g_dev).wait_recv()
            rcopy(a, 5, got, sib).start()

    def drain(outs, ssem, rsem):
        c, sib, piece, rcopy, route = ctx(outs, ssem, rsem)
        for a in range(n):
            f_dev, g_dev, f_flip, g_flip = route(a)
            for k, fl in ((3, f_flip), (4, g_flip), (5, (1, 1))):
                rcopy(a, k, piece(a, fl, 1 - c), sib).wait_recv()
            own = piece(a, (0, 0), c)
            for k, dev in ((0, f_dev), (1, g_dev), (2, g_dev), (3, sib), (4, sib), (5, sib)):
                rcopy(a, k, own, dev).wait_send()

    return [own_halves, pass_on_neighbours, pass_on_diagonal, drain]


def _gather_ride(bufs, first_axes, l0, nl):
    n = len(bufs)
    phases = [lambda reads, refs, outs, ssem, rsem, ph=ph: ph(refs, ssem, rsem)
              for ph in _gather_phases(n, first_axes, [l0] * n, [nl] * n)]
    return _Ride([], bufs, [], phases, GATHER_STEPS, _gather_sems(n))


def _shard_gate(g):
    n_l, nb = g.shape[:2]
    return g.reshape(n_l, nb, N_SHARD, RG_BLOCK // N_SHARD, RG_BLOCK).transpose(2, 0, 1, 3, 4)


def _exchange(arrs, axes, name):
    n = len(arrs)

    def body(*refs):
        ins, outs = refs[:n], refs[n:2 * n]
        ssem, rsem = refs[2 * n:]
        x, y, c = _me()
        cps = []
        for a in range(n):
            my = {"x": x, "y": y, "c": c}[axes[a]]
            partner = {"x": (1 - x, y, c), "y": (x, 1 - y, c), "c": (x, y, 1 - c)}[axes[a]]
            cps.append(pltpu.make_async_remote_copy(
                src_ref=ins[a].at[:, 1 - my], dst_ref=outs[a], send_sem=ssem.at[a], recv_sem=rsem.at[a],
                device_id=partner, device_id_type=MESH_ID))
            cps[-1].start()
        for cp in cps:
            cp.wait()

    return pl.pallas_call(
        body, name=name,
        in_specs=[ANY] * n, out_specs=[ANY] * n,
        out_shape=[jax.ShapeDtypeStruct((a.shape[0],) + a.shape[2:], a.dtype) for a in arrs],
        scratch_shapes=[pltpu.SemaphoreType.DMA((n,)), pltpu.SemaphoreType.DMA((n,))],
        compiler_params=pltpu.CompilerParams(has_side_effects=True),
    )(*arrs)


def _add_kept(arr, got, idx, name, with_bf16):
    P, _, R, N = arr.shape
    tr = _tile(R, max(16, (1 << 20) // N))

    def body(idx_ref, a_ref, g_ref, o_ref, *ob_ref):
        s = a_ref[...] + g_ref[...].astype(F32)
        o_ref[...] = s
        if with_bf16:
            ob_ref[0][...] = s.astype(BF16)

    out_blk = pl.BlockSpec((None, tr, N), lambda p, i, idx: (p, i, 0))
    return pl.pallas_call(
        body, name=name,
        grid_spec=pltpu.PrefetchScalarGridSpec(
            num_scalar_prefetch=1, grid=(P, R // tr),
            in_specs=[pl.BlockSpec((None, None, tr, N), lambda p, i, idx: (p, idx[0], i, 0)),
                      pl.BlockSpec((None, tr, N), lambda p, i, idx: (p, i, 0))],
            out_specs=[out_blk] * (2 if with_bf16 else 1)),
        out_shape=[jax.ShapeDtypeStruct((P, R, N), F32)] + ([jax.ShapeDtypeStruct((P, R, N), BF16)] if with_bf16 else []),
        compiler_params=_cp(("parallel", "parallel")),
    )(idx, arr, got)


def _share_halves(halves):
    n = len(halves)

    def body(*refs):
        ins, outs = refs[:n], refs[n:2 * n]
        ssem, rsem = refs[2 * n:]
        x, y, c = _me()
        cps = []
        for a in range(n):
            cps.append(pltpu.make_async_remote_copy(
                src_ref=ins[a], dst_ref=outs[a], send_sem=ssem.at[a], recv_sem=rsem.at[a],
                device_id=(x, y, 1 - c), device_id_type=MESH_ID))
            cps[-1].start()
        for cp in cps:
            cp.wait()

    return pl.pallas_call(
        body, name="share_halves",
        in_specs=[ANY] * n, out_specs=[ANY] * n,
        out_shape=[jax.ShapeDtypeStruct(h.shape, h.dtype) for h in halves],
        scratch_shapes=[pltpu.SemaphoreType.DMA((n,)), pltpu.SemaphoreType.DMA((n,))],
        compiler_params=pltpu.CompilerParams(has_side_effects=True),
    )(*halves)


def _exchange_ride(arrs, axes):
    n = len(arrs)

    def copies(reads, outs, ssem, rsem):
        x, y, c = _me()
        cps = []
        for a in range(n):
            my = {"x": x, "y": y, "c": c}[axes[a]]
            partner = {"x": (1 - x, y, c), "y": (x, 1 - y, c), "c": (x, y, 1 - c)}[axes[a]]
            cps.append(pltpu.make_async_remote_copy(
                src_ref=reads[a].at[:, 1 - my], dst_ref=outs[a], send_sem=ssem.at[a], recv_sem=rsem.at[a],
                device_id=partner, device_id_type=MESH_ID))
        return cps

    def start(reads, bufs, outs, ssem, rsem):
        for cp in copies(reads, outs, ssem, rsem):
            cp.start()

    def finish(reads, bufs, outs, ssem, rsem):
        for cp in copies(reads, outs, ssem, rsem):
            cp.wait()

    landing = [jax.ShapeDtypeStruct((a.shape[0],) + a.shape[2:], a.dtype) for a in arrs]
    return _Ride(arrs, [], landing, [start, finish], (0.0, 1.0),
                 [pltpu.SemaphoreType.DMA((n,)), pltpu.SemaphoreType.DMA((n,))])


class _ReduceScatter:
    def __init__(self, grads, first_axes):
        x, y, c = _me()
        idx = lambda v: jnp.reshape(v, (1,)).astype(jnp.int32)
        self.coord = {"x": idx(x), "y": idx(y), "c": idx(c)}
        self.first = list(first_axes)
        self.second = ["y" if f == "x" else "x" for f in first_axes]
        self.views_c = [g.reshape(N_SHARD, 2, g.shape[1] // 2, g.shape[2]) for g in grads]

    def ride_c(self):
        return _exchange_ride(self.views_c, "c" * len(self.views_c))

    def exchange_c(self):
        self.after_c(_exchange(self.views_c, "c" * len(self.views_c), "rs_exchange_c"))

    def after_c(self, got):
        self.summed = [_add_kept(v, r, self.coord["c"], "rs_add_c", True) for v, r in zip(self.views_c, got)]

    @staticmethod
    def _split_view(a, ax):
        _, rh, nn = a.shape
        return a.reshape(1, 2, 2 * rh, nn) if ax == "x" else a.reshape(2, 2, rh, nn)

    def _views_1(self):
        return [self._split_view(s[1], f) for s, f in zip(self.summed, self.first)]

    def ride_1(self):
        return _exchange_ride(self._views_1(), self.first)

    def exchange_1(self):
        self.after_1(_exchange(self._views_1(), self.first, "rs_exchange_1"))

    def after_1(self, got):
        self.summed = [_add_kept(self._split_view(s[0], f), r, self.coord[f], "rs_add_1", True)
                       for s, r, f in zip(self.summed, got, self.first)]

    def _views_2(self):
        return [s[1].reshape(1, 2, -1, s[1].shape[-1]) for s in self.summed]

    def ride_2(self):
        return _exchange_ride(self._views_2(), self.second)

    def exchange_2(self):
        self.after_2(_exchange(self._views_2(), self.second, "rs_exchange_2"))

    def after_2(self, got):
        views32 = [s[0].reshape(1, 2, -1, s[0].shape[-1]) for s in self.summed]
        self.kept = [_add_kept(v, r, self.coord[g], "rs_add_2", False)[0][0]
                     for v, r, g in zip(views32, got, self.second)]

    def finish(self):
        return self.kept, _share_halves(self.kept)


def _allgather_small(p):
    R, D = p.shape

    def body(p_ref, o_ref, ssem, rsem):
        x, y, c = _me()
        me = 4 * x + 2 * y + c
        o_ref[me] = p_ref[...]
        cps = []
        for m in range(1, 8):
            mx, my, mc = (m >> 2) & 1, (m >> 1) & 1, m & 1
            peer = (1 - x if mx else x, 1 - y if my else y, 1 - c if mc else c)
            cps.append(pltpu.make_async_remote_copy(
                src_ref=p_ref, dst_ref=o_ref.at[me], send_sem=ssem.at[m - 1], recv_sem=rsem.at[m - 1],
                device_id=peer, device_id_type=MESH_ID))
            cps[-1].start()
        for cp in cps:
            cp.wait()

    return pl.pallas_call(
        body, name="allgather_small",
        in_specs=[pl.BlockSpec(memory_space=pltpu.VMEM)],
        out_specs=pl.BlockSpec(memory_space=pltpu.VMEM),
        out_shape=jax.ShapeDtypeStruct((8, R, D), p.dtype),
        scratch_shapes=[pltpu.SemaphoreType.DMA((7,)), pltpu.SemaphoreType.DMA((7,))],
        compiler_params=pltpu.CompilerParams(has_side_effects=True, vmem_limit_bytes=VMEM_LIMIT),
    )(p)


def _adam_math(w, g, m, v):
    m = ADAM_B1 * m + (1.0 - ADAM_B1) * g
    v = ADAM_B2 * v + (1.0 - ADAM_B2) * (g * g)
    m_hat = m / (1.0 - ADAM_B1 ** ADAM_STEP)
    v_hat = v / (1.0 - ADAM_B2 ** ADAM_STEP)
    delta = -ADAM_LR * (m_hat / (jnp.sqrt(v_hat) + ADAM_EPS) + ADAM_WD * w)
    return delta, m, v


def _adam(w, g_mine, g_sib, m, v, core, row0, layer_rows, outs):
    R, N = w.shape
    rh = g_mine.shape[0]
    tr = _tile(layer_rows // 2, max(16, (1 << 19) // N))
    nt = rh // tr
    t0 = row0 // tr
    extra = [] if outs is None else list(outs)

    def body(c_ref, w_ref, gm_ref, gs_ref, m_ref, v_ref, *rest):
        g_ref, d_ref, nm_ref, nv_ref = rest[-4:]
        g = jnp.where(pl.program_id(0) == c_ref[0], gm_ref[...], gs_ref[...])
        d, nm, nv = _adam_math(w_ref[...], g, m_ref[...], v_ref[...])
        g_ref[...] = g
        d_ref[...] = d
        nm_ref[...] = nm
        nv_ref[...] = nv

    blk = pl.BlockSpec((tr, N), lambda h, i, c: (t0 + h * nt + i, 0))
    mine = pl.BlockSpec((tr, N), lambda h, i, c: (jnp.where(h == c[0], i, 0), 0))
    sib = pl.BlockSpec((tr, N), lambda h, i, c: (jnp.where(h == c[0], 0, i), 0))
    return pl.pallas_call(
        body, name="adamw",
        grid_spec=pltpu.PrefetchScalarGridSpec(
            num_scalar_prefetch=1, grid=(2, nt),
            in_specs=[blk, mine, sib, blk, blk] + [ANY] * len(extra), out_specs=[blk] * 4),
        out_shape=[jax.ShapeDtypeStruct((R, N), F32)] * 4,
        input_output_aliases={6 + i: i for i in range(len(extra))},
        compiler_params=_cp(("parallel", "parallel")),
    )(core, w, g_mine, g_sib, m, v, *extra)


def _reduce_rows(parts, sizes, rows_out):
    D = parts.shape[1]

    def body(p_ref, o_ref):
        o_ref[...] = jnp.zeros((rows_out, D), F32)
        off = 0
        for i, sz in enumerate(sizes):
            o_ref[i:i + 1, :] = jnp.sum(p_ref[off:off + sz, :], axis=0, keepdims=True)
            off += sz

    return pl.pallas_call(
        body, name="reduce_rows",
        out_shape=jax.ShapeDtypeStruct((rows_out, D), F32),
        compiler_params=pltpu.CompilerParams(vmem_limit_bytes=VMEM_LIMIT),
    )(parts)


def _sum_devices(g8):
    _, R, D = g8.shape

    def body(g_ref, o_ref):
        tot = g_ref[0]
        for k in range(1, 8):
            tot = tot + g_ref[k]
        o_ref[...] = tot

    return pl.pallas_call(
        body, name="sum_devices",
        out_shape=jax.ShapeDtypeStruct((R, D), F32),
        compiler_params=pltpu.CompilerParams(vmem_limit_bytes=VMEM_LIMIT),
    )(g8)


def _small_update(gathered, w, m, v, L):
    _, R, D = gathered.shape

    def body(g8_ref, w_ref, m_ref, v_ref, g_ref, d_ref, nm_ref, nv_ref):
        tot = g8_ref[0]
        for k in range(1, 8):
            tot = tot + g8_ref[k]
        g_ref[...] = tot
        sm = _softmax_rows(w_ref, L)
        run = sm[0]
        dcum = []
        for l in range(L):
            if l > 0:
                run = run + sm[l]
            cum = run - sm[0]
            dcum.append(jnp.where((cum > 0.0) & (cum < 1.0), g_ref[l:l + 1, :], 0.0))
        dsm = [jnp.zeros((1, D), F32)]
        for i in range(1, L):
            dsm.append(functools.reduce(lambda p, q: p + q, dcum[i:]))
        dot = functools.reduce(lambda p, q: p + q, [s * d for s, d in zip(sm, dsm)])
        for l in range(L):
            g_ref[l:l + 1, :] = sm[l] * (dsm[l] - dot)
        lam = w_ref[5 * L:6 * L, :]
        g_ref[5 * L:6 * L, :] = g_ref[5 * L:6 * L, :] * (-_sig(-lam))
        d, nm, nv = _adam_math(w_ref[...], g_ref[...], m_ref[...], v_ref[...])
        d_ref[...] = d
        nm_ref[...] = nm
        nv_ref[...] = nv

    return pl.pallas_call(
        body, name="small_update",
        out_shape=[jax.ShapeDtypeStruct((R, D), F32)] * 4,
        compiler_params=pltpu.CompilerParams(vmem_limit_bytes=VMEM_LIMIT),
    )(gathered, w, m, v)


def kernel(x, lb_logits, norm_mix, w_in, conv_w, conv_b, w_r, b_r, w_i, b_i, lam, hg_norm, w_out, norm_mlp, w_up, w_down, norm_final, loss_target, m_lb_logits, m_norm_mix, m_w_in, m_conv_w, m_conv_b, m_w_r, m_b_r, m_w_i, m_b_i, m_lam, m_hg_norm, m_w_out, m_norm_mlp, m_w_up, m_w_down, m_norm_final, v_lb_logits, v_norm_mix, v_w_in, v_conv_w, v_conv_b, v_w_r, v_b_r, v_w_i, v_b_i, v_lam, v_hg_norm, v_w_out, v_norm_mlp, v_w_up, v_w_down, v_norm_final):
    B, S, D = x.shape
    L = norm_mix.shape[0]
    nb = D // RG_BLOCK
    Dq = D // N_SHARD
    mx, my, mc = _me()
    shard = 2 * mx + my

    big_w = (w_in, w_out, w_up, w_down, w_r, w_i)
    flat2 = lambda a: a.reshape(-1, a.shape[-1])
    slot = jnp.reshape(shard, (1,)).astype(jnp.int32)
    def place(w):
        b = _cast_place(flat2(w), slot)
        return b.reshape(N_SHARD, L, 2, b.shape[1] // (2 * L), b.shape[2])

    *wbufs, g_r, g_i = _gather_weights([place(w) for w in big_w], LINK_SPLIT, [0] * 6, [1] * 4 + [L] * 2)
    unshard_gate = lambda g: g.reshape(N_SHARD, L, nb, RG_BLOCK // N_SHARD, RG_BLOCK).transpose(1, 2, 0, 3, 4).reshape(
        L, nb, RG_BLOCK, RG_BLOCK)
    w_r_full, w_i_full = unshard_gate(g_r), unshard_gate(g_i)

    R_LB, R_NMIX, R_CB, R_BR, R_BI, R_LAM, R_NMLP, R_GN, R_CW, R_NF, R_LOSS = (
        0, L, 2 * L, 3 * L, 4 * L, 5 * L, 6 * L, 7 * L, 8 * L, 12 * L, 12 * L + 1)
    n_rows = 12 * L + 2
    rows_pad = n_rows + (-n_rows) % SUBLANES

    def place_cols(a):
        return lax.dynamic_update_slice(jnp.zeros((a.shape[0], D), F32), a, (0, shard * Dq))

    def pack_small(lb_, nmix_, cb_, br_, bi_, lam_, nmlp_, gn_, cw_, nf_):
        gn_pad = jnp.pad(gn_, ((0, 0), (0, D - HEAD)))
        rows = [lb_, nmix_, cb_, br_, bi_, lam_, nmlp_, gn_pad, place_cols(cw_.reshape(L * CONV_TAPS, Dq)),
                nf_[None, :], jnp.zeros((rows_pad - n_rows + 1, D), F32)]
        return jnp.concatenate(rows, axis=0)

    w_small = pack_small(lb_logits, norm_mix, conv_b, b_r, b_i, lam, norm_mlp, hg_norm, conv_w, norm_final)
    m_small = pack_small(m_lb_logits, m_norm_mix, m_conv_b, m_b_r, m_b_i, m_lam, m_norm_mlp, m_hg_norm, m_conv_w,
                         m_norm_final)
    v_small = pack_small(v_lb_logits, v_norm_mix, v_conv_b, v_b_r, v_b_i, v_lam, v_norm_mlp, v_hg_norm, v_conv_w,
                         v_norm_final)
    cw_rows = place_cols(conv_w.reshape(L * CONV_TAPS, Dq)) * jnp.where(mc == 0, 1.0, 0.0)
    conv_w_full = _sum_devices(_allgather_small(cw_rows)).reshape(L, CONV_TAPS, D)

    lowb, sp = _prep(lb_logits, lam)

    loss_parts, grad_x, parts, small, g_nf = _local_step(
        x, loss_target, lowb, sp, norm_mix, wbufs, conv_w_full, conv_b, w_r_full, b_r, w_i_full, b_i, hg_norm,
        norm_mlp, norm_final)

    core = jnp.reshape(mc, (1,)).astype(jnp.int32)
    outs = {}
    for a, (name, w, m, v) in enumerate(zip(("w_in", "w_out", "w_up", "w_down", "w_r", "w_i"), big_w,
                                            (m_w_in, m_w_out, m_w_up, m_w_down, m_w_r, m_w_i),
                                            (v_w_in, v_w_out, v_w_up, v_w_down, v_w_r, v_w_i))):
        layer_rows = flat2(w).shape[0] // L
        done, row0 = None, 0
        for mine, sibs in parts:
            done = _adam(flat2(w), mine[a], sibs[a], flat2(m), flat2(v), core, row0, layer_rows, done)
            row0 += 2 * mine[a].shape[0]
        outs[name] = tuple(t.reshape(w.shape) for t in done)

    parts, sizes = [], []

    def add_rows(a):
        a = a.reshape(-1, a.shape[-1])
        if a.shape[1] != D:
            a = jnp.pad(a, ((0, 0), (0, D - a.shape[1])))
        parts.append(a)
        sizes.append(a.shape[0])

    for i in range(8):
        for l in range(L):
            add_rows(small[l][i])
    for l in range(L):
        for j in range(CONV_TAPS):
            add_rows(small[l][8][j])
    add_rows(g_nf)
    loss_rows = loss_parts[:, 0:1, :]
    add_rows(jnp.where(lax.broadcasted_iota(jnp.int32, loss_rows.shape, 2) == 0, loss_rows, 0.0))
    g_small = _reduce_rows(jnp.concatenate(parts, axis=0), sizes, rows_pad)
    g_small, d_small, nm_small, nv_small = _small_update(_allgather_small(g_small), w_small, m_small, v_small, L)

    def unpack(t):
        take_cols = lambda a: lax.dynamic_slice(a, (0, shard * Dq), (a.shape[0], Dq))
        return {"lb_logits": t[R_LB:R_LB + L], "norm_mix": t[R_NMIX:R_NMIX + L], "conv_b": t[R_CB:R_CB + L],
                "b_r": t[R_BR:R_BR + L], "b_i": t[R_BI:R_BI + L], "lam": t[R_LAM:R_LAM + L],
                "norm_mlp": t[R_NMLP:R_NMLP + L], "hg_norm": t[R_GN:R_GN + L, :HEAD],
                "conv_w": take_cols(t[R_CW:R_CW + L * CONV_TAPS]).reshape(L, CONV_TAPS, Dq), "norm_final": t[R_NF]}

    small_out = [unpack(t) for t in (g_small, d_small, nm_small, nv_small)]
    loss = g_small[R_LOSS, 0]
    names = ("lb_logits", "norm_mix", "w_in", "conv_w", "conv_b", "w_r", "b_r", "w_i", "b_i", "lam", "hg_norm",
             "w_out", "norm_mlp", "w_up", "w_down", "norm_final")
    result = [loss, grad_x]
    for kind in range(4):
        for nme in names:
            result.append(outs[nme][kind] if nme in outs else small_out[kind][nme])
    return tuple(result)
```

```python
import functools
import math

import jax
import jax.numpy as jnp
from jax import lax
from jax.experimental import pallas as pl
from jax.experimental.pallas import tpu as pltpu

F32 = jnp.float32
BF16 = jnp.bfloat16

HEAD = 128
RG_BLOCK = 256
CONV_TAPS = 4
RG_C = 8.0
F_MIN = 1e-30
NORM_EPS = 1e-6
N_SEG = 8
N_SHARD = 4
HG_CHUNK = 256
HG_HEADS_PER_STEP = 8
RG_TILE = 128
ADAM_LR, ADAM_B1, ADAM_B2, ADAM_EPS, ADAM_WD, ADAM_STEP = 0.001, 0.9, 0.999, 1e-08, 0.01, 10
V7X_VMEM_BYTES = 64 * 1024 * 1024
VMEM_LIMIT = V7X_VMEM_BYTES - 8 * 1024 * 1024
SUBLANES = 8
LINK_SPLIT = "xxyyyy"
GATHER_STEPS = (0.0, 0.6, 0.88, 1.0)
MESH_ID = pl.DeviceIdType.MESH
ANY = pl.BlockSpec(memory_space=pl.ANY)


def _cp(sem):
    return pltpu.CompilerParams(dimension_semantics=sem, vmem_limit_bytes=VMEM_LIMIT)


def _dot(a, b):
    return jnp.dot(a, b, preferred_element_type=F32)


def _dot_nt(a, b):
    return lax.dot_general(a, b, (((1,), (1,)), ((), ())), preferred_element_type=F32)


def _dot_tn(a, b):
    return lax.dot_general(a, b, (((0,), (0,)), ((), ())), preferred_element_type=F32)


def _dot_01(m01, x):
    n = x.shape[1]
    hi = x.astype(BF16)
    r1 = x - hi.astype(F32)
    mid = r1.astype(BF16)
    lo = (r1 - mid.astype(F32)).astype(BF16)
    y = _dot(m01, jnp.concatenate([hi, mid, lo], axis=1))
    return y[:, :n] + y[:, n:2 * n] + y[:, 2 * n:]


def _sig(x):
    return jax.nn.sigmoid(x)


def _rows8(x):
    return x.reshape(x.shape[0] // SUBLANES, SUBLANES, x.shape[1]).sum(axis=0)


def _tile(n, cap):
    if n <= cap:
        return n
    t = cap - cap % 16
    while n % t:
        t -= 16
    return t


_GELU_C = math.sqrt(2.0 / math.pi)


def _gelu_and_grad(x):
    x2 = x * x
    t = jnp.tanh(_GELU_C * (x + 0.044715 * x * x2))
    g = 0.5 * x * (1.0 + t)
    dg = 0.5 * (1.0 + t) + 0.5 * x * (1.0 - t * t) * (_GELU_C * (1.0 + 3.0 * 0.044715 * x2))
    return g, dg


def _rms(x):
    return lax.rsqrt(jnp.mean(x * x, axis=-1, keepdims=True) + NORM_EPS)


def _rms_bwd(dh, x, rs, gain):
    xhat = x * rs
    dxhat = dh * gain
    dx = rs * (dxhat - xhat * jnp.mean(dxhat * xhat, axis=-1, keepdims=True))
    return dx, _rows8(dh * xhat)


def _inproj_fwd(x2, gain, w_st, layer):
    T, D = x2.shape
    tm = _tile(T, 2048)

    def body(x_ref, g_ref, w_ref, o_ref, h_ref):
        @pl.when(pl.program_id(1) == 0)
        def _():
            x = x_ref[...]
            h_ref[...] = (x * _rms(x) * g_ref[...]).astype(BF16)
        o_ref[...] = _dot(h_ref[...], w_ref[...])

    return pl.pallas_call(
        body, name="inproj_fwd",
        grid=(T // tm, N_SEG),
        in_specs=[pl.BlockSpec((tm, D), lambda i, k: (i, 0)),
                  pl.BlockSpec((1, D), lambda i, k: (0, 0)),
                  pl.BlockSpec((None, None, D, D), lambda i, k: (k // 2, layer, 0, k % 2))],
        out_specs=[pl.BlockSpec((tm, D), lambda i, k: (i, k)),
                   pl.BlockSpec((tm, D), lambda i, k: (i, 0))],
        out_shape=[jax.ShapeDtypeStruct((T, N_SEG * D), F32), jax.ShapeDtypeStruct((T, D), BF16)],
        compiler_params=_cp(("parallel", "arbitrary")),
    )(x2, gain, w_st)


def _rg_gates(xc, wr_ref, br, wi_ref, bi, sp):
    D = xc.shape[1]
    xcb = xc.astype(BF16)
    pr, pi = [], []
    for n in range(D // RG_BLOCK):
        blk = xcb[:, n * RG_BLOCK:(n + 1) * RG_BLOCK]
        pr.append(_dot(blk, wr_ref[n]))
        pi.append(_dot(blk, wi_ref[n]))
    r = _sig(jnp.concatenate(pr, axis=1) + br) if len(pr) > 1 else _sig(pr[0] + br)
    i = _sig(jnp.concatenate(pi, axis=1) + bi) if len(pi) > 1 else _sig(pi[0] + bi)
    la = (-RG_C) * r * sp
    a = jnp.exp(la)
    y = 2.0 * la
    one_m_e2 = jnp.where(y > -1e-2, -(y * (1.0 + 0.5 * y * (1.0 + y * (1.0 / 3.0)))), 1.0 - a * a)
    mult = jnp.sqrt(jnp.maximum(one_m_e2, 0.0))
    return r, i, a, mult


def _conv_taps(xbuf, cw_ref, ts):
    acc = None
    for j in range(CONV_TAPS):
        term = cw_ref[j:j + 1, :] * xbuf[pl.ds(SUBLANES - (CONV_TAPS - 1) + j, ts), :]
        acc = term if acc is None else acc + term
    return acc


def _rg_fwd(proj, B, cw, cb, wr, br, wi, bi, sp):
    T = proj.shape[0]
    D = proj.shape[1] // N_SEG
    S = T // B
    ts = _tile(S, RG_TILE)
    nts = S // ts
    nb = D // RG_BLOCK

    def body(xa_ref, ga_ref, cw_ref, cb_ref, wr_ref, br_ref, wi_ref, bi_ref, sp_ref,
             ya_ref, h_ref, xbuf, a_scr, u_scr, carry):
        @pl.when(pl.program_id(1) == 0)
        def _():
            xbuf[0:SUBLANES, :] = jnp.zeros((SUBLANES, D), F32)
            carry[...] = jnp.zeros((SUBLANES, D), F32)

        xbuf[pl.ds(SUBLANES, ts), :] = xa_ref[...]
        xc = _conv_taps(xbuf, cw_ref, ts) + cb_ref[...]
        r, i, a, mult = _rg_gates(xc, wr_ref, br_ref[...], wi_ref, bi_ref[...], sp_ref[...])
        a_scr[...] = a
        u_scr[...] = mult * (i * xc)
        row8 = lax.broadcasted_iota(jnp.int32, (SUBLANES, 1), 0)

        def blk(n, hprev):
            off = pl.multiple_of(n * SUBLANES, SUBLANES)
            a8 = a_scr[pl.ds(off, SUBLANES), :]
            u8 = u_scr[pl.ds(off, SUBLANES), :]
            for d in (1, 2, 4):
                m = row8 >= d
                ap = jnp.where(m, pltpu.roll(a8, d, 0), 1.0)
                up = jnp.where(m, pltpu.roll(u8, d, 0), 0.0)
                u8 = a8 * up + u8
                a8 = a8 * ap
            h8 = u8 + a8 * hprev
            u_scr[pl.ds(off, SUBLANES), :] = h8
            last = jnp.sum(jnp.where(row8 == SUBLANES - 1, h8, 0.0), axis=0, keepdims=True)
            return jnp.broadcast_to(last, (SUBLANES, D))

        carry[...] = lax.fori_loop(0, ts // SUBLANES, blk, carry[...])
        h = u_scr[...]
        h_ref[...] = h
        g, _ = _gelu_and_grad(ga_ref[...])
        ya_ref[...] = (h * g).astype(BF16)
        xbuf[0:SUBLANES, :] = xa_ref[pl.ds(ts - SUBLANES, SUBLANES), :]

    vec = pl.BlockSpec((1, D), lambda b, j: (0, 0))
    gate = pl.BlockSpec((nb, RG_BLOCK, RG_BLOCK), lambda b, j: (0, 0, 0))
    return pl.pallas_call(
        body, name="rg_fwd",
        grid=(B, nts),
        in_specs=[pl.BlockSpec((ts, D), lambda b, j: (b * nts + j, 0)),
                  pl.BlockSpec((ts, D), lambda b, j: (b * nts + j, 1)),
                  pl.BlockSpec((CONV_TAPS, D), lambda b, j: (0, 0)), vec, gate, vec, gate, vec, vec],
        out_specs=[pl.BlockSpec((ts, D), lambda b, j: (b * nts + j, 0))] * 2,
        out_shape=[jax.ShapeDtypeStruct((T, D), BF16), jax.ShapeDtypeStruct((T, D), F32)],
        scratch_shapes=[pltpu.VMEM((SUBLANES + ts, D), F32), pltpu.VMEM((ts, D), F32),
                        pltpu.VMEM((ts, D), F32), pltpu.VMEM((SUBLANES, D), F32)],
        compiler_params=_cp(("arbitrary", "arbitrary")),
    )(proj, proj, cw, cb, wr, br, wi, bi, sp)


def _hg_gates(q, z, lb):
    sig = _sig(z)
    one_m = 1.0 - lb
    fg = lb + one_m * sig
    lf = jnp.log(jnp.maximum(fg, F_MIN))
    kf = one_m * (1.0 - sig)
    qs = _sig(q)
    return q * qs, qs, kf, lf, fg, sig


def _hg_cum(lf, C):
    ri = lax.broadcasted_iota(jnp.int32, (C, C), 0)
    ci = lax.broadcasted_iota(jnp.int32, (C, C), 1)
    return _dot_01(jnp.where(ci <= ri, 1.0, 0.0).astype(BF16), lf)


def _hg_levels(lf, cum, C):
    row = lax.broadcasted_iota(jnp.int32, (C, 1), 0)
    levels = []
    w = C // 2
    while w >= 4:
        blk = 2 * w
        upper = (row & w) != 0
        ref = jnp.min(jnp.where(upper, 0.0, cum).reshape(C // blk, blk, HEAD), axis=1, keepdims=True)
        ref = jnp.broadcast_to(ref, (C // blk, blk, HEAD)).reshape(C, HEAD)
        d = cum - ref
        levels.append(jnp.exp(jnp.where(upper, d, -d)))
        w //= 2
    r4 = row & 3
    lf_prev = pltpu.roll(lf, 1, 0)
    lf_next = pltpu.roll(lf, C - 1, 0)
    levels.append(jnp.exp(jnp.where(r4 == 3, lf + lf_prev, jnp.where(r4 == 2, lf, jnp.where(r4 == 0, lf_next, 0.0)))))
    levels.append(jnp.exp(jnp.where((row & 1) == 1, lf, 0.0)))
    levels.append(None)
    return levels


def _hg_level_blocks(C):
    blks = []
    w = C // 2
    while w >= 4:
        blks.append(2 * w)
        w //= 2
    return blks + [4, 2, 1]


def _hg_fill_masks(mask_scr, C):
    ri = lax.broadcasted_iota(jnp.int32, (C, C), 0)
    ci = lax.broadcasted_iota(jnp.int32, (C, C), 1)
    for i, blk in enumerate(_hg_level_blocks(C)):
        if blk == 1:
            keep = ri == ci
        else:
            shift, w = blk.bit_length() - 1, blk // 2
            keep = ((ri >> shift) == (ci >> shift)) & ((ri & w) != 0) & ((ci & w) == 0)
        mask_scr[i] = jnp.where(keep, 1.0, 0.0).astype(F32)


def _hg_operands(qf, kf, e):
    if e is None:
        return qf.astype(BF16), kf.astype(BF16)
    return (qf * e).astype(BF16), (kf * e).astype(BF16)


def _hg_scores(qf, kf, levels, mask_scr):
    A = None
    for n, e in enumerate(levels):
        a = _dot_nt(*_hg_operands(qf, kf, e)) * mask_scr[n]
        A = a if A is None else A + a
    return A


def _hg_specs(B, NC, D, C, pair):
    def spec(col0, rev):
        c0 = col0 // pair
        if rev:
            return pl.BlockSpec((C, pair * HEAD), lambda b, h, j: (b * NC + (NC - 1 - j), c0 + h))
        return pl.BlockSpec((C, pair * HEAD), lambda b, h, j: (b * NC + j, c0 + h))
    return spec


def _head(ref, hh):
    return ref.at[:, pl.ds(hh * HEAD, HEAD)]


class _Ride:
    def __init__(self, reads, bufs, outs, phases, fractions, sems):
        self.reads, self.bufs, self.outs = list(reads), list(bufs), list(outs)
        self.phases, self.fractions, self.sems = list(phases), list(fractions), list(sems)


def _ride_call(body, ride, *, name, grid, in_specs, out_specs, out_shape, scratch_shapes, semantics, operands):
    if ride is None:
        return pl.pallas_call(body, name=name, grid=grid, in_specs=in_specs, out_specs=out_specs, out_shape=out_shape,
                              scratch_shapes=scratch_shapes, compiler_params=_cp(semantics))(*operands)
    n_in, n_out, n_scr = len(in_specs), len(out_specs), len(scratch_shapes)
    nr, nb, no = len(ride.reads), len(ride.bufs), len(ride.outs)
    last_step = math.prod(grid) - 1

    def full_body(*refs):
        own = refs[:n_in] + refs[n_in + nr + nb:n_in + nr + nb + n_out]
        tail = refs[n_in + nr + nb + n_out:]
        ride_refs = (refs[n_in:n_in + nr], tail[:nb], tail[nb:nb + no])
        scr = tail[nb + no:]
        step = pl.program_id(0)
        for d in range(1, len(grid)):
            step = step * grid[d] + pl.program_id(d)
        for phase, frac in zip(ride.phases, ride.fractions):
            @pl.when(step == int(round(frac * last_step)))
            def _(phase=phase):
                phase(*ride_refs, *scr[n_scr:])
        body(*own, *scr[:n_scr])

    return pl.pallas_call(
        full_body, name=name, grid=grid,
        in_specs=list(in_specs) + [ANY] * (nr + nb),
        out_specs=list(out_specs) + [ANY] * (nb + no),
        out_shape=list(out_shape) + [jax.ShapeDtypeStruct(b.shape, b.dtype) for b in ride.bufs] + ride.outs,
        input_output_aliases={n_in + nr + j: n_out + j for j in range(nb)},
        scratch_shapes=list(scratch_shapes) + ride.sems,
        compiler_params=_cp(("arbitrary",) * len(grid)),
    )(*operands, *ride.reads, *ride.bufs)


def _hg_fwd(proj, B, lb, gn, ride=None):
    T = proj.shape[0]
    D = proj.shape[1] // N_SEG
    S = T // B
    C = min(HG_CHUNK, S)
    NC = S // C
    H = D // HEAD
    hpd = D // HEAD
    pair = math.gcd(H, HG_HEADS_PER_STEP)
    spec = _hg_specs(B, NC, D, C, pair)

    def body(q_ref, z_ref, v_ref, g_ref, lb_ref, gn_ref, yb_ref, o_ref, st_ref, a_ref, cum_ref, st_scr, mask_scr):
        @pl.when(pl.program_id(2) == 0)
        def _():
            st_scr[...] = jnp.zeros((pair, HEAD, HEAD), F32)
            _hg_fill_masks(mask_scr, C)

        for hh in range(pair):
            s_t = st_scr[hh]
            st_ref[hh] = s_t
            qf, _, kf, lf, _, _ = _hg_gates(_head(q_ref, hh)[...], _head(z_ref, hh)[...], _head(lb_ref, hh)[...])
            cum = _hg_cum(lf, C)
            _head(cum_ref, hh)[...] = cum
            A = _hg_scores(qf, kf, _hg_levels(lf, cum, C), mask_scr).astype(BF16)
            a_ref[hh] = A
            vb = _head(v_ref, hh)[...].astype(BF16)
            o = _dot_nt((qf * jnp.exp(cum)).astype(BF16), s_t.astype(BF16)) + _dot(A, vb)
            last = jnp.sum(lf, axis=0, keepdims=True)
            kend = kf * jnp.exp(last - cum)
            st_scr[hh] = jnp.exp(last) * s_t + _dot_tn(vb, kend.astype(BF16))
            _head(o_ref, hh)[...] = o
            g = _head(g_ref, hh)[...]
            _head(yb_ref, hh)[...] = ((o * _rms(o) * gn_ref[...]) * (g * _sig(g))).astype(BF16)

    HG = H // pair
    return _ride_call(
        body, ride, name="hg_fwd",
        grid=(B, HG, NC),
        in_specs=[spec(2 * hpd, False), spec(3 * hpd, False), spec(4 * hpd, False), spec(5 * hpd, False),
                  pl.BlockSpec((1, pair * HEAD), lambda b, h, j: (0, h)),
                  pl.BlockSpec((1, HEAD), lambda b, h, j: (0, 0))],
        out_specs=[spec(0, False), spec(0, False),
                   pl.BlockSpec((None, pair, None, HEAD, HEAD), lambda b, h, j: (b, h, j, 0, 0)),
                   pl.BlockSpec((None, pair, None, C, C), lambda b, h, j: (b, h, j, 0, 0)), spec(0, False)],
        out_shape=[jax.ShapeDtypeStruct((T, D), BF16), jax.ShapeDtypeStruct((T, D), F32),
                   jax.ShapeDtypeStruct((B, H, NC, HEAD, HEAD), F32),
                   jax.ShapeDtypeStruct((B, H, NC, C, C), BF16), jax.ShapeDtypeStruct((T, D), F32)],
        scratch_shapes=[pltpu.VMEM((pair, HEAD, HEAD), F32), pltpu.VMEM((len(_hg_level_blocks(C)), C, C), F32)],
        semantics=("parallel", "parallel", "arbitrary"),
        operands=(proj, proj, proj, proj, lb, gn))


def _w_full(ref):
    s, r, c = ref.shape
    return ref[...].reshape(s * r, c)


def _out_fwd(ya, yb, proj, x2, w_st, layer):
    T, D = x2.shape
    tm = _tile(T, 512)

    def body(ya_ref, yb_ref, ma_ref, mb_ref, x_ref, w_ref, xm_ref, y_ref):
        y = (_sig(ma_ref[...]) * ya_ref[...] + _sig(mb_ref[...]) * yb_ref[...]).astype(BF16)
        y_ref[...] = y
        xm_ref[...] = x_ref[...] + _dot(y, _w_full(w_ref))

    row = pl.BlockSpec((tm, D), lambda i: (i, 0))
    return pl.pallas_call(
        body, name="out_fwd",
        grid=(T // tm,),
        in_specs=[row, row, pl.BlockSpec((tm, D), lambda i: (i, 6)), pl.BlockSpec((tm, D), lambda i: (i, 7)), row,
                  pl.BlockSpec((N_SHARD, None, D // N_SHARD, D), lambda i: (0, layer, 0, 0))],
        out_specs=[row, row],
        out_shape=[jax.ShapeDtypeStruct((T, D), F32), jax.ShapeDtypeStruct((T, D), BF16)],
        compiler_params=_cp(("parallel",)),
    )(ya, yb, proj, proj, x2, w_st)


def _mlp_fwd(xm, gain, wup_st, wdn_st, layer):
    T, D = xm.shape
    F4 = wup_st.shape[3]
    tm = _tile(T, 1024)

    def body(x_ref, g_ref, wu_ref, wd_ref, xo_ref, up_ref, h_ref):
        @pl.when(pl.program_id(1) == 0)
        def _():
            x = x_ref[...]
            h_ref[...] = (x * _rms(x) * g_ref[...]).astype(BF16)
            xo_ref[...] = x
        up = _dot(h_ref[...], wu_ref[...])
        up_ref[...] = up.astype(BF16)
        act = jnp.maximum(up, 0.0)
        xo_ref[...] += _dot((act * act).astype(BF16), wd_ref[...])

    row = pl.BlockSpec((tm, D), lambda i, s: (i, 0))
    return pl.pallas_call(
        body, name="mlp_fwd",
        grid=(T // tm, N_SHARD),
        in_specs=[row, pl.BlockSpec((1, D), lambda i, s: (0, 0)),
                  pl.BlockSpec((None, None, D, F4), lambda i, s: (s, layer, 0, 0)),
                  pl.BlockSpec((None, None, F4, D), lambda i, s: (s, layer, 0, 0))],
        out_specs=[row, pl.BlockSpec((tm, F4), lambda i, s: (i, s)), row],
        out_shape=[jax.ShapeDtypeStruct((T, D), F32), jax.ShapeDtypeStruct((T, N_SHARD * F4), BF16),
                   jax.ShapeDtypeStruct((T, D), BF16)],
        compiler_params=_cp(("parallel", "arbitrary")),
    )(xm, gain, wup_st, wdn_st)


def _final_loss(x2, gain, tgt):
    T, D = x2.shape
    tm = _tile(T, 512)
    nt = T // tm

    def body(x_ref, g_ref, t_ref, loss_ref, dx_ref, dg_ref):
        x = x_ref[...]
        rs = _rms(x)
        err = x * rs * g_ref[...] - t_ref[...]
        part = 0.5 * jnp.sum(jnp.sum(err * err, axis=-1, keepdims=True) * (1.0 / D), axis=0, keepdims=True)
        loss_ref[...] = jnp.broadcast_to(part, (SUBLANES, 128))
        dx, dg = _rms_bwd(err * (1.0 / D), x, rs, g_ref[...])
        dx_ref[...] = dx
        dg_ref[...] = dg

    row = pl.BlockSpec((tm, D), lambda i: (i, 0))
    return pl.pallas_call(
        body, name="final_loss",
        grid=(nt,),
        in_specs=[row, pl.BlockSpec((1, D), lambda i: (0, 0)), row],
        out_specs=[pl.BlockSpec((None, SUBLANES, 128), lambda i: (i, 0, 0)), row,
                   pl.BlockSpec((None, SUBLANES, D), lambda i: (i, 0, 0))],
        out_shape=[jax.ShapeDtypeStruct((nt, SUBLANES, 128), F32), jax.ShapeDtypeStruct((T, D), F32),
                   jax.ShapeDtypeStruct((nt, SUBLANES, D), F32)],
        compiler_params=_cp(("parallel",)),
    )(x2, gain, tgt)


def _mlp_bwd_x(dx, xm, up, gain, wup_st, wdn_st, layer, ride=None):
    T, D = xm.shape
    F4 = wup_st.shape[3]
    tm = _tile(T, 1024)
    nt = T // tm

    def body(dx_ref, x_ref, up_ref, g_ref, wu_ref, wd_ref, dxm_ref, dup_ref, dg_ref, dxb):
        s = pl.program_id(1)

        @pl.when(s == 0)
        def _():
            dxb[...] = dx_ref[...].astype(BF16)
            dxm_ref[...] = jnp.zeros((tm, D), F32)

        d_act = _dot_nt(dxb[...], wd_ref[...])
        d_up = (d_act * (2.0 * jnp.maximum(up_ref[...].astype(F32), 0.0))).astype(BF16)
        dup_ref[...] = d_up
        dxm_ref[...] += _dot_nt(d_up, wu_ref[...])

        @pl.when(s == N_SHARD - 1)
        def _():
            x = x_ref[...]
            dxn, dg = _rms_bwd(dxm_ref[...], x, _rms(x), g_ref[...])
            dxm_ref[...] = dx_ref[...] + dxn
            dg_ref[...] = dg

    row = pl.BlockSpec((tm, D), lambda i, s: (i, 0))
    return _ride_call(
        body, ride, name="mlp_bwd_x",
        grid=(nt, N_SHARD),
        in_specs=[row, row, pl.BlockSpec((tm, F4), lambda i, s: (i, s)), pl.BlockSpec((1, D), lambda i, s: (0, 0)),
                  pl.BlockSpec((None, None, D, F4), lambda i, s: (s, layer, 0, 0)),
                  pl.BlockSpec((None, None, F4, D), lambda i, s: (s, layer, 0, 0))],
        out_specs=[row, pl.BlockSpec((tm, F4), lambda i, s: (i, s)),
                   pl.BlockSpec((None, SUBLANES, D), lambda i, s: (i, 0, 0)), row],
        out_shape=[jax.ShapeDtypeStruct((T, D), F32), jax.ShapeDtypeStruct((T, N_SHARD * F4), BF16),
                   jax.ShapeDtypeStruct((nt, SUBLANES, D), F32), jax.ShapeDtypeStruct((T, D), BF16)],
        scratch_shapes=[],
        semantics=("parallel", "arbitrary"),
        operands=(dx, xm, up, gain, wup_st, wdn_st))


def _layer_slot(bufs, shapes, n_layers):
    out_shape = [jax.ShapeDtypeStruct((N_SHARD, n_layers) + s, F32) for s in shapes]
    return out_shape, ([] if bufs is None else list(bufs))


def _mlp_bwd_w(up, dxb, h, dup, layer, n_layers, bufs):
    T, D = dxb.shape
    F4 = up.shape[1] // N_SHARD
    tk = _tile(T, 1024)
    out_shape, extra = _layer_slot(bufs, [(D, F4), (F4, D)], n_layers)

    def body(up_ref, dx_ref, h_ref, dup_ref, *rest):
        gu_ref, gd_ref = rest[-2:]

        @pl.when(pl.program_id(1) == 0)
        def _():
            gu_ref[...] = jnp.zeros((D, F4), F32)
            gd_ref[...] = jnp.zeros((F4, D), F32)
        act = jnp.maximum(up_ref[...], 0.0)
        gd_ref[...] += _dot_tn(act * act, dx_ref[...])
        gu_ref[...] += _dot_tn(h_ref[...], dup_ref[...])

    return pl.pallas_call(
        body, name="mlp_bwd_w",
        grid=(N_SHARD, T // tk),
        in_specs=[pl.BlockSpec((tk, F4), lambda s, t: (t, s)), pl.BlockSpec((tk, D), lambda s, t: (t, 0)),
                  pl.BlockSpec((tk, D), lambda s, t: (t, 0)), pl.BlockSpec((tk, F4), lambda s, t: (t, s))]
        + [ANY] * len(extra),
        out_specs=[pl.BlockSpec((None, None, D, F4), lambda s, t: (s, layer, 0, 0)),
                   pl.BlockSpec((None, None, F4, D), lambda s, t: (s, layer, 0, 0))],
        out_shape=out_shape,
        input_output_aliases={4 + i: i for i in range(len(extra))},
        compiler_params=_cp(("parallel", "arbitrary")),
    )(up, dxb, h, dup, *extra)


def _out_bwd_x(dxm, ya, yb, proj, w_st, layer):
    T, D = dxm.shape
    tm = _tile(T, 512)

    def body(dx_ref, ya_ref, yb_ref, ma_ref, mb_ref, w_ref, dya_ref, dyb_ref, dma_ref, dmb_ref):
        dy = _dot_nt(dx_ref[...].astype(BF16), _w_full(w_ref))
        sa = _sig(ma_ref[...])
        sb = _sig(mb_ref[...])
        dya_ref[...] = (dy * sa).astype(BF16)
        dyb_ref[...] = (dy * sb).astype(BF16)
        dma_ref[...] = (dy * ya_ref[...] * (sa * (1.0 - sa))).astype(BF16)
        dmb_ref[...] = (dy * yb_ref[...] * (sb * (1.0 - sb))).astype(BF16)

    row = pl.BlockSpec((tm, D), lambda i: (i, 0))
    return pl.pallas_call(
        body, name="out_bwd_x",
        grid=(T // tm,),
        in_specs=[row, row, row, pl.BlockSpec((tm, D), lambda i: (i, 6)), pl.BlockSpec((tm, D), lambda i: (i, 7)),
                  pl.BlockSpec((N_SHARD, None, D // N_SHARD, D), lambda i: (0, layer, 0, 0))],
        out_specs=[row] * 4,
        out_shape=[jax.ShapeDtypeStruct((T, D), BF16)] * 4,
        compiler_params=_cp(("parallel",)),
    )(dxm, ya, yb, proj, proj, w_st)


def _out_bwd_w(ymix, dxm, layer, n_layers, bufs):
    T, D = dxm.shape
    tk = _tile(T, 1024)
    out_shape, extra = _layer_slot(bufs, [(D // N_SHARD, D)], n_layers)

    def body(y_ref, dx_ref, *rest):
        g_ref = rest[-1]

        @pl.when(pl.program_id(0) == 0)
        def _():
            g_ref[...] = jnp.zeros((N_SHARD, D // N_SHARD, D), F32)
        g = _dot_tn(y_ref[...], dx_ref[...].astype(BF16))
        g_ref[...] += g.reshape(N_SHARD, D // N_SHARD, D)

    row = pl.BlockSpec((tk, D), lambda t: (t, 0))
    return pl.pallas_call(
        body, name="out_bwd_w",
        grid=(T // tk,),
        in_specs=[row, row] + [ANY] * len(extra),
        out_specs=[pl.BlockSpec((N_SHARD, None, D // N_SHARD, D), lambda t: (0, layer, 0, 0))],
        out_shape=out_shape,
        input_output_aliases={2 + i: i for i in range(len(extra))},
        compiler_params=_cp(("arbitrary",)),
    )(ymix, dxm, *extra)[0]


def _rg_bwd(proj, hrg, dya, B, cw, cb, wr, br, wi, bi, sp):
    T = proj.shape[0]
    D = proj.shape[1] // N_SEG
    S = T // B
    ts = _tile(S, RG_TILE)
    nts = S // ts
    nb = D // RG_BLOCK
    t8 = ts // SUBLANES

    def body(xa_ref, xp_ref, ga_ref, h_ref, hp_ref, dya_ref, cw_ref, cb_ref, wr_ref, br_ref, wi_ref, bi_ref, sp_ref,
             dxa_ref, dga_ref, gwr_ref, gwi_ref, gcw_ref, gcb_ref, gbr_ref, gbi_ref, gsp_ref,
             xbuf, hbuf, abuf, dbuf, g_scr, c_scr, gcar):
        b = pl.program_id(0)
        j = pl.program_id(1)
        first_in_time = j == nts - 1

        @pl.when((b == 0) & (j == 0))
        def _():
            gwr_ref[...] = jnp.zeros((nb, RG_BLOCK, RG_BLOCK), F32)
            gwi_ref[...] = jnp.zeros((nb, RG_BLOCK, RG_BLOCK), F32)
            gcw_ref[...] = jnp.zeros((CONV_TAPS, SUBLANES, D), F32)
            for r in (gcb_ref, gbr_ref, gbi_ref, gsp_ref):
                r[...] = jnp.zeros((SUBLANES, D), F32)

        @pl.when(j == 0)
        def _():
            abuf[pl.ds(ts, SUBLANES), :] = jnp.zeros((SUBLANES, D), F32)
            dbuf[pl.ds(ts, SUBLANES), :] = jnp.zeros((SUBLANES, D), F32)
            gcar[...] = jnp.zeros((SUBLANES, D), F32)

        keep = jnp.where(first_in_time, 0.0, 1.0)
        xbuf[0:SUBLANES, :] = xp_ref[...] * keep
        xbuf[pl.ds(SUBLANES, ts), :] = xa_ref[...]
        hbuf[0:SUBLANES, :] = hp_ref[...] * keep
        hbuf[pl.ds(SUBLANES, ts), :] = h_ref[...]

        xc = _conv_taps(xbuf, cw_ref, ts) + cb_ref[...]
        sp = sp_ref[...]
        r, i, a, mult = _rg_gates(xc, wr_ref, br_ref[...], wi_ref, bi_ref[...], sp)
        g_gate, dg_gate = _gelu_and_grad(ga_ref[...])
        dya = dya_ref[...].astype(F32)
        dga_ref[...] = (dya * h_ref[...] * dg_gate).astype(BF16)

        abuf[0:ts, :] = a
        c_scr[...] = abuf[pl.ds(1, ts), :]
        g_scr[...] = dya * g_gate
        row8 = lax.broadcasted_iota(jnp.int32, (SUBLANES, 1), 0)

        def blk(n, gnext):
            off = pl.multiple_of((t8 - 1 - n) * SUBLANES, SUBLANES)
            c8 = c_scr[pl.ds(off, SUBLANES), :]
            d8 = g_scr[pl.ds(off, SUBLANES), :]
            for d in (1, 2, 4):
                m = row8 < SUBLANES - d
                cn = jnp.where(m, pltpu.roll(c8, SUBLANES - d, 0), 1.0)
                dn = jnp.where(m, pltpu.roll(d8, SUBLANES - d, 0), 0.0)
                d8 = d8 + c8 * dn
                c8 = c8 * cn
            g8 = d8 + c8 * gnext
            g_scr[pl.ds(off, SUBLANES), :] = g8
            first = jnp.sum(jnp.where(row8 == 0, g8, 0.0), axis=0, keepdims=True)
            return jnp.broadcast_to(first, (SUBLANES, D))

        gcar[...] = lax.fori_loop(0, t8, blk, gcar[...])
        abuf[pl.ds(ts, SUBLANES), :] = a[0:SUBLANES, :]

        g = g_scr[...]
        hprev = hbuf[pl.ds(SUBLANES - 1, ts), :]
        gx = i * xc
        e2 = a * a
        dla = g * hprev * a - jnp.where(mult > 0.0, g * gx * e2 / jnp.where(mult > 0.0, mult, 1.0), 0.0)
        dgx = g * mult
        dpr = (dla * ((-RG_C) * sp)) * (r * (1.0 - r))
        dpi = (dgx * xc) * (i * (1.0 - i))
        gsp_ref[...] += _rows8(dla * ((-RG_C) * r))
        gbr_ref[...] += _rows8(dpr)
        gbi_ref[...] += _rows8(dpi)
        dprb = dpr.astype(BF16)
        dpib = dpi.astype(BF16)
        xcb = xc.astype(BF16)
        back = []
        for n in range(nb):
            sl = slice(n * RG_BLOCK, (n + 1) * RG_BLOCK)
            back.append(_dot_nt(dprb[:, sl], wr_ref[n]) + _dot_nt(dpib[:, sl], wi_ref[n]))
            gwr_ref[n] += _dot_tn(xcb[:, sl], dprb[:, sl])
            gwi_ref[n] += _dot_tn(xcb[:, sl], dpib[:, sl])
        dxc = dgx * i + (jnp.concatenate(back, axis=1) if nb > 1 else back[0])
        gcb_ref[...] += _rows8(dxc)

        dbuf[0:ts, :] = dxc
        dxa = None
        for jtap in range(CONV_TAPS):
            term = cw_ref[jtap:jtap + 1, :] * dbuf[pl.ds(CONV_TAPS - 1 - jtap, ts), :]
            dxa = term if dxa is None else dxa + term
            gcw_ref[jtap] += _rows8(dxc * xbuf[pl.ds(SUBLANES - (CONV_TAPS - 1) + jtap, ts), :])
        dxa_ref[...] = dxa.astype(BF16)
        dbuf[pl.ds(ts, SUBLANES), :] = dxc[0:SUBLANES, :]

    def tile_map(col):
        return lambda b, j: (b * nts + (nts - 1 - j), col)

    def prev8_map(col):
        return lambda b, j: (jnp.maximum((b * nts + (nts - 1 - j)) * t8 - 1, 0), col)

    vec = pl.BlockSpec((1, D), lambda b, j: (0, 0))
    gate = pl.BlockSpec((nb, RG_BLOCK, RG_BLOCK), lambda b, j: (0, 0, 0))
    acc8 = pl.BlockSpec((SUBLANES, D), lambda b, j: (0, 0))
    return pl.pallas_call(
        body, name="rg_bwd",
        grid=(B, nts),
        in_specs=[pl.BlockSpec((ts, D), tile_map(0)), pl.BlockSpec((SUBLANES, D), prev8_map(0)),
                  pl.BlockSpec((ts, D), tile_map(1)),
                  pl.BlockSpec((ts, D), tile_map(0)), pl.BlockSpec((SUBLANES, D), prev8_map(0)),
                  pl.BlockSpec((ts, D), tile_map(0)),
                  pl.BlockSpec((CONV_TAPS, D), lambda b, j: (0, 0)), vec, gate, vec, gate, vec, vec],
        out_specs=[pl.BlockSpec((ts, D), tile_map(0)), pl.BlockSpec((ts, D), tile_map(0)), gate, gate,
                   pl.BlockSpec((CONV_TAPS, SUBLANES, D), lambda b, j: (0, 0, 0)), acc8, acc8, acc8, acc8],
        out_shape=[jax.ShapeDtypeStruct((T, D), BF16)] * 2
        + [jax.ShapeDtypeStruct((nb, RG_BLOCK, RG_BLOCK), F32)] * 2
        + [jax.ShapeDtypeStruct((CONV_TAPS, SUBLANES, D), F32)] + [jax.ShapeDtypeStruct((SUBLANES, D), F32)] * 4,
        scratch_shapes=[pltpu.VMEM((SUBLANES + ts, D), F32), pltpu.VMEM((SUBLANES + ts, D), F32),
                        pltpu.VMEM((ts + SUBLANES, D), F32), pltpu.VMEM((ts + SUBLANES, D), F32),
                        pltpu.VMEM((ts, D), F32), pltpu.VMEM((ts, D), F32), pltpu.VMEM((SUBLANES, D), F32)],
        compiler_params=_cp(("arbitrary", "arbitrary")),
    )(proj, proj, proj, hrg, hrg, dya, cw, cb, wr, br, wi, bi, sp)


def _hg_bwd(proj, o_sv, dyb, states, a_sv, cum_sv, B, lb, gn, ride=None):
    T = proj.shape[0]
    D = proj.shape[1] // N_SEG
    S = T // B
    C = min(HG_CHUNK, S)
    NC = S // C
    H = D // HEAD
    hpd = D // HEAD
    pair = math.gcd(H, HG_HEADS_PER_STEP)
    spec = _hg_specs(B, NC, D, C, pair)

    def body(q_ref, z_ref, v_ref, g_ref, o_ref, dyb_ref, st_ref, a_ref, cum_ref, lb_ref, gn_ref,
             dq_ref, dz_ref, dv_ref, dg_ref, glb_ref, ggn_ref, ds_scr, mask_scr):
        @pl.when(pl.program_id(2) == 0)
        def _():
            ds_scr[...] = jnp.zeros((pair, HEAD, HEAD), F32)
            _hg_fill_masks(mask_scr, C)
            glb_ref[...] = jnp.zeros((SUBLANES, pair * HEAD), F32)
            ggn_ref[...] = jnp.zeros((pair, SUBLANES, HEAD), F32)

        for hh in range(pair):
            cols = [_head(r, hh) for r in (q_ref, z_ref, v_ref, g_ref, o_ref, dyb_ref)]
            outs = [_head(r, hh) for r in (dq_ref, dz_ref, dv_ref, dg_ref, glb_ref)]
            one_head(*cols, st_ref.at[hh], a_ref.at[hh], _head(cum_ref, hh), _head(lb_ref, hh), gn_ref,
                     *outs, ggn_ref.at[hh], ds_scr.at[hh], mask_scr)

    def one_head(q_ref, z_ref, v_ref, g_ref, o_ref, dyb_ref, st_ref, a_ref, cum_ref, lb_ref, gn_ref,
                 dq_ref, dz_ref, dv_ref, dg_ref, glb_ref, ggn_ref, ds_scr, mask_scr):
        q = q_ref[...]
        lb = lb_ref[...]
        gn = gn_ref[...]
        qf, qs, kf, lf, fg, sig = _hg_gates(q, z_ref[...], lb)
        cum = cum_ref[...]
        levels = _hg_levels(lf, cum, C)

        o = o_ref[...]
        g = g_ref[...]
        gs = _sig(g)
        rs = _rms(o)
        dyb = dyb_ref[...].astype(F32)
        don = dyb * (g * gs)
        dg_ref[...] = (dyb * (o * rs * gn) * (gs * (1.0 + g * (1.0 - gs)))).astype(BF16)
        ggn_ref[...] += _rows8(don * o * rs)
        dn = don * gn
        do = rs * (dn - o * (rs * rs) * jnp.mean(dn * o, axis=-1, keepdims=True))

        s_t = st_ref[...].astype(BF16)
        ds_t = ds_scr[...]
        ds_b = ds_t.astype(BF16)
        dob = do.astype(BF16)
        vb = v_ref[...].astype(BF16)
        ecum = jnp.exp(cum)
        last = jnp.sum(lf, axis=0, keepdims=True)
        eend = jnp.exp(last - cum)
        qhat = (qf * ecum).astype(BF16)
        kend = (kf * eend).astype(BF16)

        dA = _dot_nt(dob, vb)
        dq_inter = _dot(dob, s_t)
        dk_state = _dot(vb, ds_b)
        dqf = dq_inter * ecum
        dkf = dk_state * eend
        g_intra = None
        for n, e in enumerate(levels):
            qw, kw = _hg_operands(qf, kf, e)
            dam = (dA * mask_scr[n]).astype(BF16)
            rq = _dot(dam, kw)
            rk = _dot_tn(dam, qw)
            dqf += rq if e is None else rq * e
            dkf += rk if e is None else rk * e
            gi = qw.astype(F32) * rq - kw.astype(F32) * rk
            g_intra = gi if g_intra is None else g_intra + gi
        dv_ref[...] = (_dot_tn(a_ref[...], dob) + _dot_nt(kend, ds_b)).astype(BF16)
        e_last = jnp.exp(last)
        ds_scr[...] = e_last * ds_t + _dot_tn(dob, qhat)

        ri = lax.broadcasted_iota(jnp.int32, (C, C), 0)
        ci = lax.broadcasted_iota(jnp.int32, (C, C), 1)
        y_state = kend.astype(F32) * dk_state
        dlf = (_dot_01(jnp.where(ci >= ri, 1.0, 0.0).astype(BF16), g_intra + qhat.astype(F32) * dq_inter - y_state)
               + jnp.sum(y_state, axis=0, keepdims=True)
               + jnp.sum(e_last * st_ref[...] * ds_t, axis=0, keepdims=True))
        dfg = jnp.where(fg > F_MIN, dlf / jnp.maximum(fg, F_MIN), 0.0)
        sneg = 1.0 - sig
        diff = dfg - dkf
        dz_ref[...] = ((1.0 - lb) * sig * sneg * diff).astype(BF16)
        glb_ref[...] += _rows8(sneg * diff)
        dq_ref[...] = (dqf * (qs * (1.0 + q * (1.0 - qs)))).astype(BF16)

    return _ride_call(
        body, ride, name="hg_bwd",
        grid=(B, H // pair, NC),
        in_specs=[spec(2 * hpd, True), spec(3 * hpd, True), spec(4 * hpd, True), spec(5 * hpd, True),
                  spec(0, True), spec(0, True),
                  pl.BlockSpec((None, pair, None, HEAD, HEAD), lambda b, h, j: (b, h, NC - 1 - j, 0, 0)),
                  pl.BlockSpec((None, pair, None, C, C), lambda b, h, j: (b, h, NC - 1 - j, 0, 0)), spec(0, True),
                  pl.BlockSpec((1, pair * HEAD), lambda b, h, j: (0, h)),
                  pl.BlockSpec((1, HEAD), lambda b, h, j: (0, 0))],
        out_specs=[spec(0, True)] * 4
        + [pl.BlockSpec((None, SUBLANES, pair * HEAD), lambda b, h, j: (b, 0, h)),
           pl.BlockSpec((None, pair, SUBLANES, HEAD), lambda b, h, j: (b, h, 0, 0))],
        out_shape=[jax.ShapeDtypeStruct((T, D), BF16)] * 4
        + [jax.ShapeDtypeStruct((B, SUBLANES, D), F32), jax.ShapeDtypeStruct((B, H, SUBLANES, HEAD), F32)],
        scratch_shapes=[pltpu.VMEM((pair, HEAD, HEAD), F32), pltpu.VMEM((len(_hg_level_blocks(C)), C, C), F32)],
        semantics=("parallel", "parallel", "arbitrary"),
        operands=(proj, proj, proj, proj, o_sv, dyb, states, a_sv, cum_sv, lb, gn))


def _inproj_bwd_x(dsegs, w_st, layer, x2, gain, dxm, ride=None):
    T, D = x2.shape
    tm = _tile(T, 512)
    nt = T // tm

    def body(*refs):
        seg_refs = refs[:N_SEG]
        w_ref, x_ref, g_ref, dxm_ref, dx_ref, dg_ref = refs[N_SEG:]
        s = pl.program_id(1)

        @pl.when(s == 0)
        def _():
            dx_ref[...] = jnp.zeros((tm, D), F32)

        for ss in range(N_SHARD):
            @pl.when(s == ss)
            def _(ss=ss):
                dx_ref[...] += (_dot_nt(seg_refs[2 * ss][...], w_ref[:, 0:D])
                                + _dot_nt(seg_refs[2 * ss + 1][...], w_ref[:, D:2 * D]))

        @pl.when(s == N_SHARD - 1)
        def _():
            x = x_ref[...]
            dxn, dg = _rms_bwd(dx_ref[...], x, _rms(x), g_ref[...])
            dx_ref[...] = dxm_ref[...] + dxn
            dg_ref[...] = dg

    row = pl.BlockSpec((tm, D), lambda i, s: (i, 0))

    def seg_spec(kk):
        return pl.BlockSpec((tm, D), lambda i, s: (jnp.minimum(i + jnp.where(s > kk // 2, 1, 0), nt - 1), 0))

    return _ride_call(
        body, ride, name="inproj_bwd_x",
        grid=(nt, N_SHARD),
        in_specs=[seg_spec(kk) for kk in range(N_SEG)]
        + [pl.BlockSpec((None, None, D, 2 * D), lambda i, s: (s, layer, 0, 0)), row,
           pl.BlockSpec((1, D), lambda i, s: (0, 0)), row],
        out_specs=[row, pl.BlockSpec((None, SUBLANES, D), lambda i, s: (i, 0, 0))],
        out_shape=[jax.ShapeDtypeStruct((T, D), F32), jax.ShapeDtypeStruct((nt, SUBLANES, D), F32)],
        scratch_shapes=[],
        semantics=("parallel", "arbitrary"),
        operands=(*dsegs, w_st, x2, gain, dxm))


def _inproj_bwd_w(h, dsegs, layer, n_layers, bufs):
    T, D = h.shape
    tk = _tile(T, 1024)
    out_shape, extra = _layer_slot(bufs, [(D, 2 * D)], n_layers)

    def body(*refs):
        h_ref = refs[0]
        seg_refs = refs[1:1 + N_SEG]
        g_ref = refs[-1]
        k = pl.program_id(0)

        @pl.when(pl.program_id(1) == 0)
        def _():
            g_ref[...] = jnp.zeros((D, D), F32)

        for kk in range(N_SEG):
            @pl.when(k == kk)
            def _(kk=kk):
                g_ref[...] += _dot_tn(h_ref[...], seg_refs[kk][...])

    def seg_spec(kk):
        return pl.BlockSpec((tk, D), lambda k, t: (jnp.where(k == kk, t, 0), 0))

    return pl.pallas_call(
        body, name="inproj_bwd_w",
        grid=(N_SEG, T // tk),
        in_specs=[pl.BlockSpec((tk, D), lambda k, t: (t, 0))] + [seg_spec(kk) for kk in range(N_SEG)]
        + [ANY] * len(extra),
        out_specs=[pl.BlockSpec((None, None, D, D), lambda k, t: (k // 2, layer, 0, k % 2))],
        out_shape=out_shape,
        input_output_aliases={1 + N_SEG + i: i for i in range(len(extra))},
        compiler_params=_cp(("parallel", "arbitrary")),
    )(h, *dsegs, *extra)[0]


def _softmax_rows(lg_ref, L):
    rows = [lg_ref[l:l + 1, :] for l in range(L)]
    mx = functools.reduce(jnp.maximum, rows)
    es = [jnp.exp(r - mx) for r in rows]
    den = functools.reduce(lambda p, q: p + q, es)
    return [e / den for e in es]


def _prep(lb_logits, lam):
    L, D = lb_logits.shape

    def body(lg_ref, lam_ref, lowb_ref, sp_ref):
        sm = _softmax_rows(lg_ref, L)
        run = sm[0]
        for l in range(L):
            if l > 0:
                run = run + sm[l]
            lowb_ref[l:l + 1, :] = jnp.clip(run - sm[0], 0.0, 1.0)
        y = -lam_ref[...]
        sp_ref[...] = jnp.maximum(y, 0.0) + jnp.log1p(jnp.exp(-jnp.abs(y)))

    return pl.pallas_call(
        body, name="prep_small",
        out_shape=[jax.ShapeDtypeStruct((L, D), F32)] * 2,
    )(lb_logits, lam)


def _local_step(x, tgt, lowb, sp, norm_mix, wbufs, conv_w, conv_b, w_r, b_r, w_i, b_i, hg_norm,
                norm_mlp, norm_final):
    B, S, D = x.shape
    L = norm_mix.shape[0]
    T = B * S
    x2 = x.reshape(T, D)
    row = lambda a, l: a[l:l + 1]

    def weight_views(bufs):
        f4 = bufs[2].shape[-1]
        return (bufs[0].reshape(N_SHARD, L, D, 2 * D), bufs[1].reshape(N_SHARD, L, D // N_SHARD, D),
                bufs[2].reshape(N_SHARD, L, D, f4), bufs[3].reshape(N_SHARD, L, f4, D))

    w_in_st, w_out_st, w_up_st, w_down_st = weight_views(wbufs)
    saved = []
    for l in range(L):
        proj, h = _inproj_fwd(x2, row(norm_mix, l), w_in_st, l)
        ya, hrg = _rg_fwd(proj, B, conv_w[l], row(conv_b, l), w_r[l], row(b_r, l), w_i[l], row(b_i, l), row(sp, l))
        l0, nl = (1, min(2, L - 1)) if l == 0 else (l + 2, 1 if l + 2 < L else 0)
        if nl > 0:
            yb, o, st, a_sv, cum_sv, *wbufs = _hg_fwd(proj, B, row(lowb, l), row(hg_norm, l),
                                                      _gather_ride(list(wbufs), LINK_SPLIT, l0, nl))
            w_in_st, w_out_st, w_up_st, w_down_st = weight_views(wbufs)
        else:
            yb, o, st, a_sv, cum_sv = _hg_fwd(proj, B, row(lowb, l), row(hg_norm, l))
        xm, ymix = _out_fwd(ya, yb, proj, x2, w_out_st, l)
        xo, up, h2 = _mlp_fwd(xm, row(norm_mlp, l), w_up_st, w_down_st, l)
        saved.append((x2, proj, h, ya, hrg, yb, o, (st, a_sv, cum_sv), xm, ymix, up, h2))
        x2 = xo
    loss_parts, dx, g_nf = _final_loss(x2, norm_final[None, :], tgt.reshape(T, D))

    def reduce_part(g_in, g_out, g_mlp, gate_list):
        gate_list = gate_list[::-1]
        grads = [g_in, g_out, g_mlp[0], g_mlp[1], _shard_gate(jnp.stack([g[0] for g in gate_list])),
                 _shard_gate(jnp.stack([g[1] for g in gate_list]))]
        return _ReduceScatter([g.reshape(N_SHARD, -1, g.shape[-1]) for g in grads], LINK_SPLIT)

    gates = []
    small = []
    g_in = g_out = g_mlp = None
    rest = None
    for l in reversed(range(L)):
        x_in, proj, h, ya, hrg, yb, o, st, xm, ymix, up, h2 = saved[l]
        alone = l == 0 and L > 1
        if alone:
            rest = reduce_part(g_in, g_out, g_mlp, gates)
            g_in = g_out = g_mlp = None
            gates = []
        slot, n_slots = (0, 1) if l == 0 else (l - 1, L - 1)
        dxm, dup, g_nmlp, dxb, *got = _mlp_bwd_x(dx, xm, up, row(norm_mlp, l), w_up_st, w_down_st, l,
                                                 rest.ride_c() if alone else None)
        if alone:
            rest.after_c(got)
        g_mlp = _mlp_bwd_w(up, dxb, h2, dup, slot, n_slots, g_mlp)
        dya, dyb, dma, dmb = _out_bwd_x(dxm, ya, yb, proj, w_out_st, l)
        g_out = _out_bwd_w(ymix, dxm, slot, n_slots, None if g_out is None else [g_out])
        dxa, dga, g_wr, g_wi, g_cw, g_cb, g_br, g_bi, g_sp = _rg_bwd(
            proj, hrg, dya, B, conv_w[l], row(conv_b, l), w_r[l], row(b_r, l), w_i[l], row(b_i, l), row(sp, l))
        dq, dz, dv, dg, g_lb, g_gn, *got = _hg_bwd(proj, o, dyb, *st, B, row(lowb, l), row(hg_norm, l),
                                                   rest.ride_1() if alone else None)
        if alone:
            rest.after_1(got)
        dsegs = (dxa, dga, dq, dz, dv, dg, dma, dmb)
        dx, g_nmix, *got = _inproj_bwd_x(dsegs, w_in_st, l, x_in, row(norm_mix, l), dxm,
                                         rest.ride_2() if alone else None)
        if alone:
            rest.after_2(got)
        g_in = _inproj_bwd_w(h, dsegs, slot, n_slots, None if g_in is None else [g_in])
        gates.append((g_wr, g_wi))
        small.append((g_lb, g_nmix, g_cb, g_br, g_bi, g_sp, g_nmlp, g_gn, g_cw))
    small.reverse()
    first = reduce_part(g_in, g_out, g_mlp, gates)
    first.exchange_c()
    first.exchange_1()
    first.exchange_2()
    parts = [first.finish()] + ([rest.finish()] if rest is not None else [])
    return loss_parts, dx.reshape(B, S, D), parts, small, g_nf


def _me():
    return lax.axis_index("x"), lax.axis_index("y"), lax.axis_index("c")


def _cast_place(w, slot):
    R, N = w.shape
    tr = _tile(R, max(16, (1 << 20) // N))

    def body(slot_ref, w_ref, o_ref):
        o_ref[...] = w_ref[...].astype(BF16)

    return pl.pallas_call(
        body, name="cast_place",
        grid_spec=pltpu.PrefetchScalarGridSpec(
            num_scalar_prefetch=1, grid=(R // tr,),
            in_specs=[pl.BlockSpec((tr, N), lambda i, slot: (i, 0))],
            out_specs=pl.BlockSpec((None, tr, N), lambda i, slot: (slot[0], i, 0))),
        out_shape=jax.ShapeDtypeStruct((N_SHARD, R, N), BF16),
        compiler_params=_cp(("parallel",)),
    )(slot, w)


def _gather_weights(bufs, first_axes, l0, nl):
    n = len(bufs)
    phases = _gather_phases(n, first_axes, l0, nl)

    def body(*refs):
        outs = refs[n:2 * n]
        ssem, rsem = refs[2 * n:]
        for ph in phases:
            ph(outs, ssem, rsem)

    return pl.pallas_call(
        body, name="gather_weights",
        in_specs=[ANY] * n, out_specs=[ANY] * n,
        out_shape=[jax.ShapeDtypeStruct(b.shape, b.dtype) for b in bufs],
        input_output_aliases={a: a for a in range(n)},
        scratch_shapes=_gather_sems(n),
        compiler_params=pltpu.CompilerParams(has_side_effects=True),
    )(*bufs)


def _gather_sems(n):
    return [pltpu.SemaphoreType.DMA((n, 6)), pltpu.SemaphoreType.DMA((n, 6))]


def _gather_phases(n, first_axes, l0, nl):
    def ctx(outs, ssem, rsem):
        x, y, c = _me()

        def piece(a, flips, half):
            sx = 1 - x if flips[0] else x
            sy = 1 - y if flips[1] else y
            return outs[a].at[2 * sx + sy, pl.ds(l0[a], nl[a]), half]

        def rcopy(a, k, ref, dev):
            return pltpu.make_async_remote_copy(src_ref=ref, dst_ref=ref, send_sem=ssem.at[a, k], recv_sem=rsem.at[a, k],
                                                device_id=dev, device_id_type=MESH_ID)

        def route(a):
            fx = first_axes[a] == "x"
            f_dev = (1 - x, y, c) if fx else (x, 1 - y, c)
            g_dev = (x, 1 - y, c) if fx else (1 - x, y, c)
            return f_dev, g_dev, ((1, 0) if fx else (0, 1)), ((0, 1) if fx else (1, 0))

        return c, (x, y, 1 - c), piece, rcopy, route

    def own_halves(outs, ssem, rsem):
        c, sib, piece, rcopy, route = ctx(outs, ssem, rsem)
        for a in range(n):
            f_dev, g_dev, _, _ = route(a)
            own = piece(a, (0, 0), c)
            rcopy(a, 0, own, f_dev).start()
            rcopy(a, 1, own, g_dev).start()

    def pass_on_neighbours(outs, ssem, rsem):
        c, sib, piece, rcopy, route = ctx(outs, ssem, rsem)
        for a in range(n):
            f_dev, g_dev, f_flip, _ = route(a)
            got = piece(a, f_flip, c)
            rcopy(a, 0, got, f_dev).wait_recv()
            rcopy(a, 2, got, g_dev).start()
            rcopy(a, 3, got, sib).start()
        for a in range(n):
            _, g_dev, _, g_flip = route(a)
            got = piece(a, g_flip, c)
            rcopy(a, 1, got, g_dev).wait_recv()
            rcopy(a, 4, got, sib).start()

    def pass_on_diagonal(outs, ssem, rsem):
        c, sib, piece, rcopy, route = ctx(outs, ssem, rsem)
        for a in range(n):
            _, g_dev, _, _ = route(a)
            got = piece(a, (1, 1), c)
            rcopy(a, 2, got, g_dev).wait_recv()
            rcopy(a, 5, got, sib).start()

    def drain(outs, ssem, rsem):
        c, sib, piece, rcopy, route = ctx(outs, ssem, rsem)
        for a in range(n):
            f_dev, g_dev, f_flip, g_flip = route(a)
            for k, fl in ((3, f_flip), (4, g_flip), (5, (1, 1))):
                rcopy(a, k, piece(a, fl, 1 - c), sib).wait_recv()
            own = piece(a, (0, 0), c)
            for k, dev in ((0, f_dev), (1, g_dev), (2, g_dev), (3, sib), (4, sib), (5, sib)):
                rcopy(a, k, own, dev).wait_send()

    return [own_halves, pass_on_neighbours, pass_on_diagonal, drain]


def _gather_ride(bufs, first_axes, l0, nl):
    n = len(bufs)
    phases = [lambda reads, refs, outs, ssem, rsem, ph=ph: ph(refs, ssem, rsem)
              for ph in _gather_phases(n, first_axes, [l0] * n, [nl] * n)]
    return _Ride([], bufs, [], phases, GATHER_STEPS, _gather_sems(n))


def _shard_gate(g):
    n_l, nb = g.shape[:2]
    return g.reshape(n_l, nb, N_SHARD, RG_BLOCK // N_SHARD, RG_BLOCK).transpose(2, 0, 1, 3, 4)


def _exchange(arrs, axes, name):
    n = len(arrs)

    def body(*refs):
        ins, outs = refs[:n], refs[n:2 * n]
        ssem, rsem = refs[2 * n:]
        x, y, c = _me()
        cps = []
        for a in range(n):
            my = {"x": x, "y": y, "c": c}[axes[a]]
            partner = {"x": (1 - x, y, c), "y": (x, 1 - y, c), "c": (x, y, 1 - c)}[axes[a]]
            cps.append(pltpu.make_async_remote_copy(
                src_ref=ins[a].at[:, 1 - my], dst_ref=outs[a], send_sem=ssem.at[a], recv_sem=rsem.at[a],
                device_id=partner, device_id_type=MESH_ID))
            cps[-1].start()
        for cp in cps:
            cp.wait()

    return pl.pallas_call(
        body, name=name,
        in_specs=[ANY] * n, out_specs=[ANY] * n,
        out_shape=[jax.ShapeDtypeStruct((a.shape[0],) + a.shape[2:], a.dtype) for a in arrs],
        scratch_shapes=[pltpu.SemaphoreType.DMA((n,)), pltpu.SemaphoreType.DMA((n,))],
        compiler_params=pltpu.CompilerParams(has_side_effects=True),
    )(*arrs)


def _add_kept(arr, got, idx, name, with_bf16):
    P, _, R, N = arr.shape
    tr = _tile(R, max(16, (1 << 20) // N))

    def body(idx_ref, a_ref, g_ref, o_ref, *ob_ref):
        s = a_ref[...] + g_ref[...].astype(F32)
        o_ref[...] = s
        if with_bf16:
            ob_ref[0][...] = s.astype(BF16)

    out_blk = pl.BlockSpec((None, tr, N), lambda p, i, idx: (p, i, 0))
    return pl.pallas_call(
        body, name=name,
        grid_spec=pltpu.PrefetchScalarGridSpec(
            num_scalar_prefetch=1, grid=(P, R // tr),
            in_specs=[pl.BlockSpec((None, None, tr, N), lambda p, i, idx: (p, idx[0], i, 0)),
                      pl.BlockSpec((None, tr, N), lambda p, i, idx: (p, i, 0))],
            out_specs=[out_blk] * (2 if with_bf16 else 1)),
        out_shape=[jax.ShapeDtypeStruct((P, R, N), F32)] + ([jax.ShapeDtypeStruct((P, R, N), BF16)] if with_bf16 else []),
        compiler_params=_cp(("parallel", "parallel")),
    )(idx, arr, got)


def _share_halves(halves):
    n = len(halves)

    def body(*refs):
        ins, outs = refs[:n], refs[n:2 * n]
        ssem, rsem = refs[2 * n:]
        x, y, c = _me()
        cps = []
        for a in range(n):
            cps.append(pltpu.make_async_remote_copy(
                src_ref=ins[a], dst_ref=outs[a], send_sem=ssem.at[a], recv_sem=rsem.at[a],
                device_id=(x, y, 1 - c), device_id_type=MESH_ID))
            cps[-1].start()
        for cp in cps:
            cp.wait()

    return pl.pallas_call(
        body, name="share_halves",
        in_specs=[ANY] * n, out_specs=[ANY] * n,
        out_shape=[jax.ShapeDtypeStruct(h.shape, h.dtype) for h in halves],
        scratch_shapes=[pltpu.SemaphoreType.DMA((n,)), pltpu.SemaphoreType.DMA((n,))],
        compiler_params=pltpu.CompilerParams(has_side_effects=True),
    )(*halves)


def _exchange_ride(arrs, axes):
    n = len(arrs)

    def copies(reads, outs, ssem, rsem):
        x, y, c = _me()
        cps = []
        for a in range(n):
            my = {"x": x, "y": y, "c": c}[axes[a]]
            partner = {"x": (1 - x, y, c), "y": (x, 1 - y, c), "c": (x, y, 1 - c)}[axes[a]]
            cps.append(pltpu.make_async_remote_copy(
                src_ref=reads[a].at[:, 1 - my], dst_ref=outs[a], send_sem=ssem.at[a], recv_sem=rsem.at[a],
                device_id=partner, device_id_type=MESH_ID))
        return cps

    def start(reads, bufs, outs, ssem, rsem):
        for cp in copies(reads, outs, ssem, rsem):
            cp.start()

    def finish(reads, bufs, outs, ssem, rsem):
        for cp in copies(reads, outs, ssem, rsem):
            cp.wait()

    landing = [jax.ShapeDtypeStruct((a.shape[0],) + a.shape[2:], a.dtype) for a in arrs]
    return _Ride(arrs, [], landing, [start, finish], (0.0, 1.0),
                 [pltpu.SemaphoreType.DMA((n,)), pltpu.SemaphoreType.DMA((n,))])


class _ReduceScatter:
    def __init__(self, grads, first_axes):
        x, y, c = _me()
        idx = lambda v: jnp.reshape(v, (1,)).astype(jnp.int32)
        self.coord = {"x": idx(x), "y": idx(y), "c": idx(c)}
        self.first = list(first_axes)
        self.second = ["y" if f == "x" else "x" for f in first_axes]
        self.views_c = [g.reshape(N_SHARD, 2, g.shape[1] // 2, g.shape[2]) for g in grads]

    def ride_c(self):
        return _exchange_ride(self.views_c, "c" * len(self.views_c))

    def exchange_c(self):
        self.after_c(_exchange(self.views_c, "c" * len(self.views_c), "rs_exchange_c"))

    def after_c(self, got):
        self.summed = [_add_kept(v, r, self.coord["c"], "rs_add_c", True) for v, r in zip(self.views_c, got)]

    @staticmethod
    def _split_view(a, ax):
        _, rh, nn = a.shape
        return a.reshape(1, 2, 2 * rh, nn) if ax == "x" else a.reshape(2, 2, rh, nn)

    def _views_1(self):
        return [self._split_view(s[1], f) for s, f in zip(self.summed, self.first)]

    def ride_1(self):
        return _exchange_ride(self._views_1(), self.first)

    def exchange_1(self):
        self.after_1(_exchange(self._views_1(), self.first, "rs_exchange_1"))

    def after_1(self, got):
        self.summed = [_add_kept(self._split_view(s[0], f), r, self.coord[f], "rs_add_1", True)
                       for s, r, f in zip(self.summed, got, self.first)]

    def _views_2(self):
        return [s[1].reshape(1, 2, -1, s[1].shape[-1]) for s in self.summed]

    def ride_2(self):
        return _exchange_ride(self._views_2(), self.second)

    def exchange_2(self):
        self.after_2(_exchange(self._views_2(), self.second, "rs_exchange_2"))

    def after_2(self, got):
        views32 = [s[0].reshape(1, 2, -1, s[0].shape[-1]) for s in self.summed]
        self.kept = [_add_kept(v, r, self.coord[g], "rs_add_2", False)[0][0]
                     for v, r, g in zip(views32, got, self.second)]

    def finish(self):
        return self.kept, _share_halves(self.kept)


def _allgather_small(p):
    R, D = p.shape

    def body(p_ref, o_ref, ssem, rsem):
        x, y, c = _me()
        me = 4 * x + 2 * y + c
        o_ref[me] = p_ref[...]
        cps = []
        for m in range(1, 8):
            mx, my, mc = (m >> 2) & 1, (m >> 1) & 1, m & 1
            peer = (1 - x if mx else x, 1 - y if my else y, 1 - c if mc else c)
            cps.append(pltpu.make_async_remote_copy(
                src_ref=p_ref, dst_ref=o_ref.at[me], send_sem=ssem.at[m - 1], recv_sem=rsem.at[m - 1],
                device_id=peer, device_id_type=MESH_ID))
            cps[-1].start()
        for cp in cps:
            cp.wait()

    return pl.pallas_call(
        body, name="allgather_small",
        in_specs=[pl.BlockSpec(memory_space=pltpu.VMEM)],
        out_specs=pl.BlockSpec(memory_space=pltpu.VMEM),
        out_shape=jax.ShapeDtypeStruct((8, R, D), p.dtype),
        scratch_shapes=[pltpu.SemaphoreType.DMA((7,)), pltpu.SemaphoreType.DMA((7,))],
        compiler_params=pltpu.CompilerParams(has_side_effects=True, vmem_limit_bytes=VMEM_LIMIT),
    )(p)


def _adam_math(w, g, m, v):
    m = ADAM_B1 * m + (1.0 - ADAM_B1) * g
    v = ADAM_B2 * v + (1.0 - ADAM_B2) * (g * g)
    m_hat = m / (1.0 - ADAM_B1 ** ADAM_STEP)
    v_hat = v / (1.0 - ADAM_B2 ** ADAM_STEP)
    delta = -ADAM_LR * (m_hat / (jnp.sqrt(v_hat) + ADAM_EPS) + ADAM_WD * w)
    return delta, m, v


def _adam(w, g_mine, g_sib, m, v, core, row0, layer_rows, outs):
    R, N = w.shape
    rh = g_mine.shape[0]
    tr = _tile(layer_rows // 2, max(16, (1 << 19) // N))
    nt = rh // tr
    t0 = row0 // tr
    extra = [] if outs is None else list(outs)

    def body(c_ref, w_ref, gm_ref, gs_ref, m_ref, v_ref, *rest):
        g_ref, d_ref, nm_ref, nv_ref = rest[-4:]
        g = jnp.where(pl.program_id(0) == c_ref[0], gm_ref[...], gs_ref[...])
        d, nm, nv = _adam_math(w_ref[...], g, m_ref[...], v_ref[...])
        g_ref[...] = g
        d_ref[...] = d
        nm_ref[...] = nm
        nv_ref[...] = nv

    blk = pl.BlockSpec((tr, N), lambda h, i, c: (t0 + h * nt + i, 0))
    mine = pl.BlockSpec((tr, N), lambda h, i, c: (jnp.where(h == c[0], i, 0), 0))
    sib = pl.BlockSpec((tr, N), lambda h, i, c: (jnp.where(h == c[0], 0, i), 0))
    return pl.pallas_call(
        body, name="adamw",
        grid_spec=pltpu.PrefetchScalarGridSpec(
            num_scalar_prefetch=1, grid=(2, nt),
            in_specs=[blk, mine, sib, blk, blk] + [ANY] * len(extra), out_specs=[blk] * 4),
        out_shape=[jax.ShapeDtypeStruct((R, N), F32)] * 4,
        input_output_aliases={6 + i: i for i in range(len(extra))},
        compiler_params=_cp(("parallel", "parallel")),
    )(core, w, g_mine, g_sib, m, v, *extra)


def _reduce_rows(parts, sizes, rows_out):
    D = parts.shape[1]

    def body(p_ref, o_ref):
        o_ref[...] = jnp.zeros((rows_out, D), F32)
        off = 0
        for i, sz in enumerate(sizes):
            o_ref[i:i + 1, :] = jnp.sum(p_ref[off:off + sz, :], axis=0, keepdims=True)
            off += sz

    return pl.pallas_call(
        body, name="reduce_rows",
        out_shape=jax.ShapeDtypeStruct((rows_out, D), F32),
        compiler_params=pltpu.CompilerParams(vmem_limit_bytes=VMEM_LIMIT),
    )(parts)


def _sum_devices(g8):
    _, R, D = g8.shape

    def body(g_ref, o_ref):
        tot = g_ref[0]
        for k in range(1, 8):
            tot = tot + g_ref[k]
        o_ref[...] = tot

    return pl.pallas_call(
        body, name="sum_devices",
        out_shape=jax.ShapeDtypeStruct((R, D), F32),
        compiler_params=pltpu.CompilerParams(vmem_limit_bytes=VMEM_LIMIT),
    )(g8)


def _small_update(gathered, w, m, v, L):
    _, R, D = gathered.shape

    def body(g8_ref, w_ref, m_ref, v_ref, g_ref, d_ref, nm_ref, nv_ref):
        tot = g8_ref[0]
        for k in range(1, 8):
            tot = tot + g8_ref[k]
        g_ref[...] = tot
        sm = _softmax_rows(w_ref, L)
        run = sm[0]
        dcum = []
        for l in range(L):
            if l > 0:
                run = run + sm[l]
            cum = run - sm[0]
            dcum.append(jnp.where((cum > 0.0) & (cum < 1.0), g_ref[l:l + 1, :], 0.0))
        dsm = [jnp.zeros((1, D), F32)]
        for i in range(1, L):
            dsm.append(functools.reduce(lambda p, q: p + q, dcum[i:]))
        dot = functools.reduce(lambda p, q: p + q, [s * d for s, d in zip(sm, dsm)])
        for l in range(L):
            g_ref[l:l + 1, :] = sm[l] * (dsm[l] - dot)
        lam = w_ref[5 * L:6 * L, :]
        g_ref[5 * L:6 * L, :] = g_ref[5 * L:6 * L, :] * (-_sig(-lam))
        d, nm, nv = _adam_math(w_ref[...], g_ref[...], m_ref[...], v_ref[...])
        d_ref[...] = d
        nm_ref[...] = nm
        nv_ref[...] = nv

    return pl.pallas_call(
        body, name="small_update",
        out_shape=[jax.ShapeDtypeStruct((R, D), F32)] * 4,
        compiler_params=pltpu.CompilerParams(vmem_limit_bytes=VMEM_LIMIT),
    )(gathered, w, m, v)


def kernel(x, lb_logits, norm_mix, w_in, conv_w, conv_b, w_r, b_r, w_i, b_i, lam, hg_norm, w_out, norm_mlp, w_up, w_down, norm_final, loss_target, m_lb_logits, m_norm_mix, m_w_in, m_conv_w, m_conv_b, m_w_r, m_b_r, m_w_i, m_b_i, m_lam, m_hg_norm, m_w_out, m_norm_mlp, m_w_up, m_w_down, m_norm_final, v_lb_logits, v_norm_mix, v_w_in, v_conv_w, v_conv_b, v_w_r, v_b_r, v_w_i, v_b_i, v_lam, v_hg_norm, v_w_out, v_norm_mlp, v_w_up, v_w_down, v_norm_final):
    B, S, D = x.shape
    L = norm_mix.shape[0]
    nb = D // RG_BLOCK
    Dq = D // N_SHARD
    mx, my, mc = _me()
    shard = 2 * mx + my

    big_w = (w_in, w_out, w_up, w_down, w_r, w_i)
    flat2 = lambda a: a.reshape(-1, a.shape[-1])
    slot = jnp.reshape(shard, (1,)).astype(jnp.int32)
    def place(w):
        b = _cast_place(flat2(w), slot)
        return b.reshape(N_SHARD, L, 2, b.shape[1] // (2 * L), b.shape[2])

    *wbufs, g_r, g_i = _gather_weights([place(w) for w in big_w], LINK_SPLIT, [0] * 6, [1] * 4 + [L] * 2)
    unshard_gate = lambda g: g.reshape(N_SHARD, L, nb, RG_BLOCK // N_SHARD, RG_BLOCK).transpose(1, 2, 0, 3, 4).reshape(
        L, nb, RG_BLOCK, RG_BLOCK)
    w_r_full, w_i_full = unshard_gate(g_r), unshard_gate(g_i)

    R_LB, R_NMIX, R_CB, R_BR, R_BI, R_LAM, R_NMLP, R_GN, R_CW, R_NF, R_LOSS = (
        0, L, 2 * L, 3 * L, 4 * L, 5 * L, 6 * L, 7 * L, 8 * L, 12 * L, 12 * L + 1)
    n_rows = 12 * L + 2
    rows_pad = n_rows + (-n_rows) % SUBLANES

    def place_cols(a):
        return lax.dynamic_update_slice(jnp.zeros((a.shape[0], D), F32), a, (0, shard * Dq))

    def pack_small(lb_, nmix_, cb_, br_, bi_, lam_, nmlp_, gn_, cw_, nf_):
        gn_pad = jnp.pad(gn_, ((0, 0), (0, D - HEAD)))
        rows = [lb_, nmix_, cb_, br_, bi_, lam_, nmlp_, gn_pad, place_cols(cw_.reshape(L * CONV_TAPS, Dq)),
                nf_[None, :], jnp.zeros((rows_pad - n_rows + 1, D), F32)]
        return jnp.concatenate(rows, axis=0)

    w_small = pack_small(lb_logits, norm_mix, conv_b, b_r, b_i, lam, norm_mlp, hg_norm, conv_w, norm_final)
    m_small = pack_small(m_lb_logits, m_norm_mix, m_conv_b, m_b_r, m_b_i, m_lam, m_norm_mlp, m_hg_norm, m_conv_w,
                         m_norm_final)
    v_small = pack_small(v_lb_logits, v_norm_mix, v_conv_b, v_b_r, v_b_i, v_lam, v_norm_mlp, v_hg_norm, v_conv_w,
                         v_norm_final)
    cw_rows = place_cols(conv_w.reshape(L * CONV_TAPS, Dq)) * jnp.where(mc == 0, 1.0, 0.0)
    conv_w_full = _sum_devices(_allgather_small(cw_rows)).reshape(L, CONV_TAPS, D)

    lowb, sp = _prep(lb_logits, lam)

    loss_parts, grad_x, parts, small, g_nf = _local_step(
        x, loss_target, lowb, sp, norm_mix, wbufs, conv_w_full, conv_b, w_r_full, b_r, w_i_full, b_i, hg_norm,
        norm_mlp, norm_final)

    core = jnp.reshape(mc, (1,)).astype(jnp.int32)
    outs = {}
    for a, (name, w, m, v) in enumerate(zip(("w_in", "w_out", "w_up", "w_down", "w_r", "w_i"), big_w,
                                            (m_w_in, m_w_out, m_w_up, m_w_down, m_w_r, m_w_i),
                                            (v_w_in, v_w_out, v_w_up, v_w_down, v_w_r, v_w_i))):
        layer_rows = flat2(w).shape[0] // L
        done, row0 = None, 0
        for mine, sibs in parts:
            done = _adam(flat2(w), mine[a], sibs[a], flat2(m), flat2(v), core, row0, layer_rows, done)
            row0 += 2 * mine[a].shape[0]
        outs[name] = tuple(t.reshape(w.shape) for t in done)

    parts, sizes = [], []

    def add_rows(a):
        a = a.reshape(-1, a.shape[-1])
        if a.shape[1] != D:
            a = jnp.pad(a, ((0, 0), (0, D - a.shape[1])))
        parts.append(a)
        sizes.append(a.shape[0])

    for i in range(8):
        for l in range(L):
            add_rows(small[l][i])
    for l in range(L):
        for j in range(CONV_TAPS):
            add_rows(small[l][8][j])
    add_rows(g_nf)
    loss_rows = loss_parts[:, 0:1, :]
    add_rows(jnp.where(lax.broadcasted_iota(jnp.int32, loss_rows.shape, 2) == 0, loss_rows, 0.0))
    g_small = _reduce_rows(jnp.concatenate(parts, axis=0), sizes, rows_pad)
    g_small, d_small, nm_small, nv_small = _small_update(_allgather_small(g_small), w_small, m_small, v_small, L)

    def unpack(t):
        take_cols = lambda a: lax.dynamic_slice(a, (0, shard * Dq), (a.shape[0], Dq))
        return {"lb_logits": t[R_LB:R_LB + L], "norm_mix": t[R_NMIX:R_NMIX + L], "conv_b": t[R_CB:R_CB + L],
                "b_r": t[R_BR:R_BR + L], "b_i": t[R_BI:R_BI + L], "lam": t[R_LAM:R_LAM + L],
                "norm_mlp": t[R_NMLP:R_NMLP + L], "hg_norm": t[R_GN:R_GN + L, :HEAD],
                "conv_w": take_cols(t[R_CW:R_CW + L * CONV_TAPS]).reshape(L, CONV_TAPS, Dq), "norm_final": t[R_NF]}

    small_out = [unpack(t) for t in (g_small, d_small, nm_small, nv_small)]
    loss = g_small[R_LOSS, 0]
    names = ("lb_logits", "norm_mix", "w_in", "conv_w", "conv_b", "w_r", "b_r", "w_i", "b_i", "lam", "hg_norm",
             "w_out", "norm_mlp", "w_up", "w_down", "norm_final")
    result = [loss, grad_x]
    for kind in range(4):
        for nme in names:
            result.append(outs[nme][kind] if nme in outs else small_out[kind][nme])
    return tuple(result)
```
